```python
import jax, jax.numpy as jnp
from jax import lax
import numpy as np

D_MODEL = 1024
BATCH = 8
SEQ = 8192
DEPTH = 1

LRU_WIDTH = D_MODEL
LRU_HEADS = 16
LRU_BLOCK = LRU_WIDTH // LRU_HEADS
CONV_WIDTH = 4
CONV_LEFT = 2
RGLRU_C = 8.0
N_DIR = 2
N_HEADS = 16
N_KV_HEADS = 4
HEAD_DIM = 64
GROUP = N_HEADS // N_KV_HEADS
WINDOW = 128
BLOCK = 128
D_FF = ((8 * D_MODEL // 3 + 255) // 256) * 256
N_BRANCH = 2
Q_W = N_HEADS * HEAD_DIM
KV_W = N_KV_HEADS * HEAD_DIM
IN_W = 2 * LRU_WIDTH + Q_W + 2 * KV_W + N_BRANCH * D_MODEL
EPS = 1e-6
NEG_INF = -1e30

kernel_name = "hybrid_rglru_swa_gated_encoder"


def rmsnorm(x, g):
    xf = x.astype(jnp.float32)
    y = xf * lax.rsqrt(jnp.mean(xf * xf, axis=-1, keepdims=True) + EPS)
    return (y * g.astype(jnp.float32)).astype(x.dtype)


def centred_depthwise_conv(u, w, b):
    s = u.shape[1]
    up = jnp.pad(u, ((0, 0), (CONV_LEFT, CONV_WIDTH - 1 - CONV_LEFT), (0, 0)))
    out = up[:, 0:s] * w[0]
    for k in range(1, CONV_WIDTH):
        out = out + up[:, k:k + s] * w[k]
    return out + b


def _linear_combine(left, right):
    a1, b1 = left
    a2, b2 = right
    return a1 * a2, a2 * b1 + b2


def rg_lru(u, lam, wa, ba, wx, bx, reverse):
    bsz, s, c = u.shape
    ub = u.reshape(bsz, s, LRU_HEADS, LRU_BLOCK)
    r = jax.nn.sigmoid(jnp.einsum("bshi,hij->bshj", ub, wa.astype(jnp.float32)).reshape(bsz, s, c) + ba.astype(jnp.float32))
    i = jax.nn.sigmoid(jnp.einsum("bshi,hij->bshj", ub, wx.astype(jnp.float32)).reshape(bsz, s, c) + bx.astype(jnp.float32))
    log_a = -RGLRU_C * r * jax.nn.softplus(-lam.astype(jnp.float32))
    a = jnp.exp(log_a)
    beta = jnp.sqrt(jnp.maximum(-jnp.expm1(2.0 * log_a), 0.0))
    _, h = lax.associative_scan(_linear_combine, (a, beta * (i * u)), axis=1, reverse=reverse)
    return h


def banded_alibi_sink_attention(q, k, v, sink):
    bsz, s = q.shape[0], q.shape[1]
    nb = s // BLOCK
    qb = (q.astype(jnp.float32) * (HEAD_DIM ** -0.5)).reshape(bsz, nb, BLOCK, N_KV_HEADS, GROUP, HEAD_DIM)

    def key_blocks(t):
        tp = jnp.pad(t.astype(jnp.float32), ((0, 0), (BLOCK, BLOCK), (0, 0), (0, 0)))
        tp = tp.reshape(bsz, nb + 2, BLOCK, N_KV_HEADS, HEAD_DIM)
        return jnp.concatenate([tp[:, j:j + nb] for j in range(3)], axis=2)

    kb = key_blocks(k)
    vb = key_blocks(v)
    scores = jnp.einsum("bnqkgd,bnskd->bnkgqs", qb, kb)

    q_loc = jnp.arange(BLOCK)
    k_loc = jnp.arange(3 * BLOCK)
    dist = q_loc[:, None] + BLOCK - k_loc[None, :]
    kpos = jnp.arange(nb)[:, None] * BLOCK - BLOCK + k_loc[None, :]
    valid = (jnp.abs(dist) <= WINDOW)[None] & ((kpos >= 0) & (kpos < s))[:, None, :]

    slopes = jnp.exp2(-8.0 * (jnp.arange(N_HEADS, dtype=jnp.float32) + 1.0) / N_HEADS)
    alibi = -slopes.reshape(N_KV_HEADS, GROUP, 1, 1) * jnp.abs(dist).astype(jnp.float32)
    scores = jnp.where(valid[None, :, None, None], scores + alibi, NEG_INF)

    sink_l = sink.astype(jnp.float32).reshape(1, 1, N_KV_HEADS, GROUP, 1, 1)
    m = jnp.maximum(jnp.max(scores, axis=-1, keepdims=True), sink_l)
    p = jnp.exp(scores - m)
    denom = jnp.sum(p, axis=-1, keepdims=True) + jnp.exp(sink_l - m)
    o = jnp.einsum("bnkgqs,bnskd->bnqkgd", p / denom, vb)
    return o.reshape(bsz, s, Q_W)


def _fwd_setup_inputs(seed: int = 0) -> dict:
    key = jax.random.key(seed)
    ks = jax.random.split(key, 20)
    f32 = jnp.float32
    x = jax.random.normal(ks[0], (BATCH, SEQ, D_MODEL), f32)
    norm_mix_g = 1.0 + 0.05 * jax.random.normal(ks[1], (DEPTH, D_MODEL), f32)
    w_in = jax.random.normal(ks[2], (DEPTH, D_MODEL, IN_W), f32) * D_MODEL ** -0.5
    b_gate = 0.01 * jax.random.normal(ks[3], (DEPTH, N_BRANCH * D_MODEL), f32)
    conv_w = jax.random.normal(ks[4], (DEPTH, CONV_WIDTH, LRU_WIDTH), f32) * CONV_WIDTH ** -0.5
    conv_b = 0.01 * jax.random.normal(ks[5], (DEPTH, LRU_WIDTH), f32)
    u = jax.random.uniform(ks[6], (DEPTH, N_DIR, LRU_WIDTH), f32, minval=0.9, maxval=0.999)
    p = u ** (1.0 / RGLRU_C)
    lru_lambda = jnp.log(p) - jnp.log1p(-p)
    lru_wa = jax.random.normal(ks[7], (DEPTH, N_DIR, LRU_HEADS, LRU_BLOCK, LRU_BLOCK), f32) * LRU_BLOCK ** -0.5
    lru_ba = 0.01 * jax.random.normal(ks[8], (DEPTH, N_DIR, LRU_WIDTH), f32)
    lru_wx = jax.random.normal(ks[9], (DEPTH, N_DIR, LRU_HEADS, LRU_BLOCK, LRU_BLOCK), f32) * LRU_BLOCK ** -0.5
    lru_bx = 0.01 * jax.random.normal(ks[10], (DEPTH, N_DIR, LRU_WIDTH), f32)
    attn_sink = 0.5 * jax.random.normal(ks[11], (DEPTH, N_HEADS), f32)
    w_out = jax.random.normal(ks[12], (DEPTH, D_MODEL, D_MODEL), f32) * D_MODEL ** -0.5
    norm_ffn_g = 1.0 + 0.05 * jax.random.normal(ks[13], (DEPTH, D_MODEL), f32)
    w_ffn_in = jax.random.normal(ks[14], (DEPTH, D_MODEL, 2 * D_FF), f32) * D_MODEL ** -0.5
    w_ffn_out = jax.random.normal(ks[15], (DEPTH, D_FF, D_MODEL), f32) * D_FF ** -0.5
    norm_final_g = 1.0 + 0.05 * jax.random.normal(ks[16], (D_MODEL,), f32)
    return {"x": x, "norm_mix_g": norm_mix_g, "w_in": w_in, "b_gate": b_gate,
            "conv_w": conv_w, "conv_b": conv_b, "lru_lambda": lru_lambda,
            "lru_wa": lru_wa, "lru_ba": lru_ba, "lru_wx": lru_wx, "lru_bx": lru_bx,
            "attn_sink": attn_sink, "w_out": w_out, "norm_ffn_g": norm_ffn_g,
            "w_ffn_in": w_ffn_in, "w_ffn_out": w_ffn_out, "norm_final_g": norm_final_g}


def _fwd_reference(x, norm_mix_g, w_in, b_gate, conv_w, conv_b, lru_lambda, lru_wa, lru_ba,
              lru_wx, lru_bx, attn_sink, w_out, norm_ffn_g, w_ffn_in, w_ffn_out, norm_final_g):
    bsz, s, _ = x.shape
    splits = [LRU_WIDTH, 2 * LRU_WIDTH, 2 * LRU_WIDTH + Q_W, 2 * LRU_WIDTH + Q_W + KV_W,
              2 * LRU_WIDTH + Q_W + 2 * KV_W]
    for l in range(DEPTH):
        xn = rmsnorm(x, norm_mix_g[l])
        proj = xn @ w_in[l]
        u, g_lru, q, k, v, z = jnp.split(proj, splits, axis=-1)

        uc = centred_depthwise_conv(u, conv_w[l], conv_b[l]).astype(jnp.float32)
        h_fwd = rg_lru(uc, lru_lambda[l, 0], lru_wa[l, 0], lru_ba[l, 0], lru_wx[l, 0], lru_bx[l, 0], False)
        h_bwd = rg_lru(uc, lru_lambda[l, 1], lru_wa[l, 1], lru_ba[l, 1], lru_wx[l, 1], lru_bx[l, 1], True)
        y_a = ((h_fwd + h_bwd) * jax.nn.gelu(g_lru.astype(jnp.float32))).astype(x.dtype)

        y_b = banded_alibi_sink_attention(
            q.reshape(bsz, s, N_HEADS, HEAD_DIM),
            k.reshape(bsz, s, N_KV_HEADS, HEAD_DIM),
            v.reshape(bsz, s, N_KV_HEADS, HEAD_DIM),
            attn_sink[l]).astype(x.dtype)

        gates = jax.nn.sigmoid(z + b_gate[l]).reshape(bsz, s, N_BRANCH, D_MODEL)
        merged = gates[:, :, 0] * y_a + gates[:, :, 1] * y_b
        x = x + merged @ w_out[l]

        xn2 = rmsnorm(x, norm_ffn_g[l])
        gu = xn2 @ w_ffn_in[l]
        ff_gate, ff_up = jnp.split(gu, [D_FF], axis=-1)
        x = x + (jax.nn.silu(ff_gate) * ff_up) @ w_ffn_out[l]
    return rmsnorm(x, norm_final_g)


import jax as _jax
import jax.numpy as _jnp

TWIN_FORMAT = 'train_step'
FWD_PARAMS = ['x', 'norm_mix_g', 'w_in', 'b_gate', 'conv_w', 'conv_b', 'lru_lambda', 'lru_wa', 'lru_ba', 'lru_wx', 'lru_bx', 'attn_sink', 'w_out', 'norm_ffn_g', 'w_ffn_in', 'w_ffn_out', 'norm_final_g']
TWIN_WEIGHTS = ['norm_mix_g', 'w_in', 'b_gate', 'conv_w', 'conv_b', 'lru_lambda', 'lru_wa', 'lru_ba', 'lru_wx', 'lru_bx', 'attn_sink', 'w_out', 'norm_ffn_g', 'w_ffn_in', 'w_ffn_out', 'norm_final_g']
TWIN_DIFF_INPUT = 'x'
TWIN_INPUTS = ['x', 'norm_mix_g', 'w_in', 'b_gate', 'conv_w', 'conv_b', 'lru_lambda', 'lru_wa', 'lru_ba', 'lru_wx', 'lru_bx', 'attn_sink', 'w_out', 'norm_ffn_g', 'w_ffn_in', 'w_ffn_out', 'norm_final_g', 'loss_target', 'm_norm_mix_g', 'm_w_in', 'm_b_gate', 'm_conv_w', 'm_conv_b', 'm_lru_lambda', 'm_lru_wa', 'm_lru_ba', 'm_lru_wx', 'm_lru_bx', 'm_attn_sink', 'm_w_out', 'm_norm_ffn_g', 'm_w_ffn_in', 'm_w_ffn_out', 'm_norm_final_g', 'v_norm_mix_g', 'v_w_in', 'v_b_gate', 'v_conv_w', 'v_conv_b', 'v_lru_lambda', 'v_lru_wa', 'v_lru_ba', 'v_lru_wx', 'v_lru_bx', 'v_attn_sink', 'v_w_out', 'v_norm_ffn_g', 'v_w_ffn_in', 'v_w_ffn_out', 'v_norm_final_g']
TWIN_OUTPUTS = ['loss', 'grad_x', 'grad_norm_mix_g', 'grad_w_in', 'grad_b_gate', 'grad_conv_w', 'grad_conv_b', 'grad_lru_lambda', 'grad_lru_wa', 'grad_lru_ba', 'grad_lru_wx', 'grad_lru_bx', 'grad_attn_sink', 'grad_w_out', 'grad_norm_ffn_g', 'grad_w_ffn_in', 'grad_w_ffn_out', 'grad_norm_final_g', 'delta_norm_mix_g', 'delta_w_in', 'delta_b_gate', 'delta_conv_w', 'delta_conv_b', 'delta_lru_lambda', 'delta_lru_wa', 'delta_lru_ba', 'delta_lru_wx', 'delta_lru_bx', 'delta_attn_sink', 'delta_w_out', 'delta_norm_ffn_g', 'delta_w_ffn_in', 'delta_w_ffn_out', 'delta_norm_final_g', 'new_m_norm_mix_g', 'new_m_w_in', 'new_m_b_gate', 'new_m_conv_w', 'new_m_conv_b', 'new_m_lru_lambda', 'new_m_lru_wa', 'new_m_lru_ba', 'new_m_lru_wx', 'new_m_lru_bx', 'new_m_attn_sink', 'new_m_w_out', 'new_m_norm_ffn_g', 'new_m_w_ffn_in', 'new_m_w_ffn_out', 'new_m_norm_final_g', 'new_v_norm_mix_g', 'new_v_w_in', 'new_v_b_gate', 'new_v_conv_w', 'new_v_conv_b', 'new_v_lru_lambda', 'new_v_lru_wa', 'new_v_lru_ba', 'new_v_lru_wx', 'new_v_lru_bx', 'new_v_attn_sink', 'new_v_w_out', 'new_v_norm_ffn_g', 'new_v_w_ffn_in', 'new_v_w_ffn_out', 'new_v_norm_final_g']
TWIN_LEAF_KINDS = {'loss': 'loss', 'grad_x': 'grad_x', 'grad_norm_mix_g': 'grad_w', 'grad_w_in': 'grad_w', 'grad_b_gate': 'grad_w', 'grad_conv_w': 'grad_w', 'grad_conv_b': 'grad_w', 'grad_lru_lambda': 'grad_w', 'grad_lru_wa': 'grad_w', 'grad_lru_ba': 'grad_w', 'grad_lru_wx': 'grad_w', 'grad_lru_bx': 'grad_w', 'grad_attn_sink': 'grad_w', 'grad_w_out': 'grad_w', 'grad_norm_ffn_g': 'grad_w', 'grad_w_ffn_in': 'grad_w', 'grad_w_ffn_out': 'grad_w', 'grad_norm_final_g': 'grad_w', 'delta_norm_mix_g': 'delta_w', 'delta_w_in': 'delta_w', 'delta_b_gate': 'delta_w', 'delta_conv_w': 'delta_w', 'delta_conv_b': 'delta_w', 'delta_lru_lambda': 'delta_w', 'delta_lru_wa': 'delta_w', 'delta_lru_ba': 'delta_w', 'delta_lru_wx': 'delta_w', 'delta_lru_bx': 'delta_w', 'delta_attn_sink': 'delta_w', 'delta_w_out': 'delta_w', 'delta_norm_ffn_g': 'delta_w', 'delta_w_ffn_in': 'delta_w', 'delta_w_ffn_out': 'delta_w', 'delta_norm_final_g': 'delta_w', 'new_m_norm_mix_g': 'new_m', 'new_m_w_in': 'new_m', 'new_m_b_gate': 'new_m', 'new_m_conv_w': 'new_m', 'new_m_conv_b': 'new_m', 'new_m_lru_lambda': 'new_m', 'new_m_lru_wa': 'new_m', 'new_m_lru_ba': 'new_m', 'new_m_lru_wx': 'new_m', 'new_m_lru_bx': 'new_m', 'new_m_attn_sink': 'new_m', 'new_m_w_out': 'new_m', 'new_m_norm_ffn_g': 'new_m', 'new_m_w_ffn_in': 'new_m', 'new_m_w_ffn_out': 'new_m', 'new_m_norm_final_g': 'new_m', 'new_v_norm_mix_g': 'new_v', 'new_v_w_in': 'new_v', 'new_v_b_gate': 'new_v', 'new_v_conv_w': 'new_v', 'new_v_conv_b': 'new_v', 'new_v_lru_lambda': 'new_v', 'new_v_lru_wa': 'new_v', 'new_v_lru_ba': 'new_v', 'new_v_lru_wx': 'new_v', 'new_v_lru_bx': 'new_v', 'new_v_attn_sink': 'new_v', 'new_v_w_out': 'new_v', 'new_v_norm_ffn_g': 'new_v', 'new_v_w_ffn_in': 'new_v', 'new_v_w_ffn_out': 'new_v', 'new_v_norm_final_g': 'new_v'}


def _forward(args):
    return _fwd_reference(*[args[k] for k in FWD_PARAMS])


def _output_shape():
    def fwd():
        inp = _fwd_setup_inputs(0)
        return _fwd_reference(*[inp[k] for k in FWD_PARAMS])
    out = _jax.eval_shape(fwd)
    return out.shape, out.dtype

N_MICROBATCH = 1
ADAM_LR = 0.001
ADAM_B1 = 0.9
ADAM_B2 = 0.999
ADAM_EPS = 1e-08
ADAM_WD = 0.01
ADAM_STEP = 10
PER_EXAMPLE_BATCH_AXIS = {'x': 0, 'loss_target': 0}
SHARED_INPUTS = []
_WEIGHT_DTYPES = {'norm_mix_g': _jnp.float32, 'w_in': _jnp.float32, 'b_gate': _jnp.float32, 'conv_w': _jnp.float32, 'conv_b': _jnp.float32, 'lru_lambda': _jnp.float32, 'lru_wa': _jnp.float32, 'lru_ba': _jnp.float32, 'lru_wx': _jnp.float32, 'lru_bx': _jnp.float32, 'attn_sink': _jnp.float32, 'w_out': _jnp.float32, 'norm_ffn_g': _jnp.float32, 'w_ffn_in': _jnp.float32, 'w_ffn_out': _jnp.float32, 'norm_final_g': _jnp.float32}
MOMENT_SCALE = {'norm_mix_g': 1.438187e-01, 'w_in': 6.406677e-02, 'b_gate': 3.298560e-02, 'conv_w': 1.063136e-01, 'conv_b': 1.451161e+00, 'lru_lambda': 3.930376e-02, 'lru_wa': 2.884475e-02, 'lru_ba': 2.125680e-02, 'lru_wx': 5.421731e-02, 'lru_bx': 2.510324e-02, 'attn_sink': 7.240725e-02, 'w_out': 1.103423e-01, 'norm_ffn_g': 1.911177e-01, 'w_ffn_in': 7.655820e-02, 'w_ffn_out': 1.250622e-01, 'norm_final_g': 6.412987e+01}


def _to_microbatches(a, axis):
    t = _jnp.moveaxis(a, axis, 0)
    t = t.reshape((N_MICROBATCH, t.shape[0] // N_MICROBATCH) + t.shape[1:])
    return _jnp.moveaxis(t, 1, axis + 1)


def setup_inputs(seed: int = 0) -> dict:
    inp = _fwd_setup_inputs(seed)
    key = _jax.random.fold_in(_jax.random.key(seed), 7919)
    shape, _ = _output_shape()
    out = dict(inp)
    out["loss_target"] = _jax.random.normal(_jax.random.fold_in(key, 0), shape, _jnp.float32)
    for i, name in enumerate(TWIN_WEIGHTS):
        w = inp[name].astype(_jnp.float32)
        if MOMENT_SCALE is None:
            s = _jnp.sqrt(_jnp.mean(_jnp.square(w)) + 1e-30)
        else:
            s = MOMENT_SCALE[name]
        km, kv = _jax.random.split(_jax.random.fold_in(key, i + 1))
        out[name] = w
        out["m_" + name] = s * _jax.random.normal(km, w.shape, _jnp.float32)
        out["v_" + name] = (s * s) * _jax.random.uniform(kv, w.shape, _jnp.float32, 0.5, 1.5)
    if N_MICROBATCH > 1:
        for name, axis in PER_EXAMPLE_BATCH_AXIS.items():
            out[name] = _to_microbatches(out[name], axis)
    return {'x': out['x'], 'norm_mix_g': out['norm_mix_g'], 'w_in': out['w_in'], 'b_gate': out['b_gate'], 'conv_w': out['conv_w'], 'conv_b': out['conv_b'], 'lru_lambda': out['lru_lambda'], 'lru_wa': out['lru_wa'], 'lru_ba': out['lru_ba'], 'lru_wx': out['lru_wx'], 'lru_bx': out['lru_bx'], 'attn_sink': out['attn_sink'], 'w_out': out['w_out'], 'norm_ffn_g': out['norm_ffn_g'], 'w_ffn_in': out['w_ffn_in'], 'w_ffn_out': out['w_ffn_out'], 'norm_final_g': out['norm_final_g'], 'loss_target': out['loss_target'], 'm_norm_mix_g': out['m_norm_mix_g'], 'm_w_in': out['m_w_in'], 'm_b_gate': out['m_b_gate'], 'm_conv_w': out['m_conv_w'], 'm_conv_b': out['m_conv_b'], 'm_lru_lambda': out['m_lru_lambda'], 'm_lru_wa': out['m_lru_wa'], 'm_lru_ba': out['m_lru_ba'], 'm_lru_wx': out['m_lru_wx'], 'm_lru_bx': out['m_lru_bx'], 'm_attn_sink': out['m_attn_sink'], 'm_w_out': out['m_w_out'], 'm_norm_ffn_g': out['m_norm_ffn_g'], 'm_w_ffn_in': out['m_w_ffn_in'], 'm_w_ffn_out': out['m_w_ffn_out'], 'm_norm_final_g': out['m_norm_final_g'], 'v_norm_mix_g': out['v_norm_mix_g'], 'v_w_in': out['v_w_in'], 'v_b_gate': out['v_b_gate'], 'v_conv_w': out['v_conv_w'], 'v_conv_b': out['v_conv_b'], 'v_lru_lambda': out['v_lru_lambda'], 'v_lru_wa': out['v_lru_wa'], 'v_lru_ba': out['v_lru_ba'], 'v_lru_wx': out['v_lru_wx'], 'v_lru_bx': out['v_lru_bx'], 'v_attn_sink': out['v_attn_sink'], 'v_w_out': out['v_w_out'], 'v_norm_ffn_g': out['v_norm_ffn_g'], 'v_w_ffn_in': out['v_w_ffn_in'], 'v_w_ffn_out': out['v_w_ffn_out'], 'v_norm_final_g': out['v_norm_final_g']}


def _loss(weights, diff, rest, loss_target):
    with _jax.named_scope("forward"):
        args = {**rest, TWIN_DIFF_INPUT: diff, **{k: w.astype(_WEIGHT_DTYPES[k]) for k, w in weights.items()}}
        y = _forward(args)
    with _jax.named_scope("loss_head"):
        err = _jnp.square(y.astype(_jnp.float32) - loss_target)
        return 0.5 * _jnp.sum(_jnp.mean(err, axis=-1)) if err.ndim else 0.5 * err


def _adamw(w, g, m, v):
    m = ADAM_B1 * m + (1.0 - ADAM_B1) * g
    v = ADAM_B2 * v + (1.0 - ADAM_B2) * _jnp.square(g)
    m_hat = m / (1.0 - ADAM_B1 ** ADAM_STEP)
    v_hat = v / (1.0 - ADAM_B2 ** ADAM_STEP)
    delta = -ADAM_LR * (m_hat / (_jnp.sqrt(v_hat) + ADAM_EPS) + ADAM_WD * w)
    return delta, m, v


def reference(x, norm_mix_g, w_in, b_gate, conv_w, conv_b, lru_lambda, lru_wa, lru_ba, lru_wx, lru_bx, attn_sink, w_out, norm_ffn_g, w_ffn_in, w_ffn_out, norm_final_g, loss_target, m_norm_mix_g, m_w_in, m_b_gate, m_conv_w, m_conv_b, m_lru_lambda, m_lru_wa, m_lru_ba, m_lru_wx, m_lru_bx, m_attn_sink, m_w_out, m_norm_ffn_g, m_w_ffn_in, m_w_ffn_out, m_norm_final_g, v_norm_mix_g, v_w_in, v_b_gate, v_conv_w, v_conv_b, v_lru_lambda, v_lru_wa, v_lru_ba, v_lru_wx, v_lru_bx, v_attn_sink, v_w_out, v_norm_ffn_g, v_w_ffn_in, v_w_ffn_out, v_norm_final_g):
    given = dict(x=x, norm_mix_g=norm_mix_g, w_in=w_in, b_gate=b_gate, conv_w=conv_w, conv_b=conv_b, lru_lambda=lru_lambda, lru_wa=lru_wa, lru_ba=lru_ba, lru_wx=lru_wx, lru_bx=lru_bx, attn_sink=attn_sink, w_out=w_out, norm_ffn_g=norm_ffn_g, w_ffn_in=w_ffn_in, w_ffn_out=w_ffn_out, norm_final_g=norm_final_g, loss_target=loss_target, m_norm_mix_g=m_norm_mix_g, m_w_in=m_w_in, m_b_gate=m_b_gate, m_conv_w=m_conv_w, m_conv_b=m_conv_b, m_lru_lambda=m_lru_lambda, m_lru_wa=m_lru_wa, m_lru_ba=m_lru_ba, m_lru_wx=m_lru_wx, m_lru_bx=m_lru_bx, m_attn_sink=m_attn_sink, m_w_out=m_w_out, m_norm_ffn_g=m_norm_ffn_g, m_w_ffn_in=m_w_ffn_in, m_w_ffn_out=m_w_ffn_out, m_norm_final_g=m_norm_final_g, v_norm_mix_g=v_norm_mix_g, v_w_in=v_w_in, v_b_gate=v_b_gate, v_conv_w=v_conv_w, v_conv_b=v_conv_b, v_lru_lambda=v_lru_lambda, v_lru_wa=v_lru_wa, v_lru_ba=v_lru_ba, v_lru_wx=v_lru_wx, v_lru_bx=v_lru_bx, v_attn_sink=v_attn_sink, v_w_out=v_w_out, v_norm_ffn_g=v_norm_ffn_g, v_w_ffn_in=v_w_ffn_in, v_w_ffn_out=v_w_ffn_out, v_norm_final_g=v_norm_final_g)
    weights = {n: given[n] for n in TWIN_WEIGHTS}
    shared = {n: given[n] for n in SHARED_INPUTS}
    per_example = {n: given[n] for n in ['x']}
    grad_fn = _jax.value_and_grad(_loss, argnums=(0, 1))

    def one_microbatch(ex, loss_target):
        ex = dict(ex)
        diff = ex.pop(TWIN_DIFF_INPUT)
        return grad_fn(weights, diff, {**shared, **ex}, loss_target)

    if N_MICROBATCH == 1:
        loss, (grad_w, grad_x) = one_microbatch(per_example, given["loss_target"])
    else:
        def body(carry, xs):
            loss_sum, grad_sum = carry
            l_k, (gw_k, gx_k) = one_microbatch(xs[0], xs[1])
            with _jax.named_scope("update"):
                return (loss_sum + l_k, _jax.tree.map(_jnp.add, grad_sum, gw_k)), gx_k

        init = (_jnp.zeros((), _jnp.float32), _jax.tree.map(_jnp.zeros_like, weights))
        (loss, grad_w), grad_x = _jax.lax.scan(body, init, (per_example, given["loss_target"]))
    with _jax.named_scope("update"):
        delta_w, new_m, new_v = {}, {}, {}
        for n in TWIN_WEIGHTS:
            delta_w[n], new_m[n], new_v[n] = _adamw(weights[n], grad_w[n], given["m_" + n], given["v_" + n])
    return (loss, grad_x, *[grad_w[n] for n in TWIN_WEIGHTS], *[delta_w[n] for n in TWIN_WEIGHTS],
            *[new_m[n] for n in TWIN_WEIGHTS], *[new_v[n] for n in TWIN_WEIGHTS])
```

```python
import math

import jax
import jax.numpy as jnp
from jax import lax
from jax.experimental import pallas as pl
from jax.experimental.pallas import tpu as pltpu

f32 = jnp.float32
bf16 = jnp.bfloat16

D = 1024
D_FF = 2816
IN_W = 5632
N_HEADS = 16
N_KV = 4
HEAD_DIM = 64
WINDOW = 128
BLK = 128
LRU_HEADS = 16
LRU_BLOCK = 64
LRU_GROUPS = 4
LRU_GW = 256
LRU_CHUNK = 128
RGLRU_C = 8.0
EPS = 1e-6
NEG_INF = -1e30
N_DEV = 8

ADAM_LR = 0.001
ADAM_B1 = 0.9
ADAM_B2 = 0.999
ADAM_EPS = 1e-08
ADAM_WD = 0.01
ADAM_STEP = 10

VMEM_MB = 56

C_U, C_G, C_Q, C_Z0, C_Z1, C_K, C_V = 0, 1024, 2048, 3072, 4096, 5120, 5376


def _cparams(vmem_mb=VMEM_MB):
    return pltpu.CompilerParams(vmem_limit_bytes=vmem_mb << 20)


def _div_tile(n, pref):
    if n <= pref:
        return n
    return max(t for t in range(8, pref + 1, 8) if n % t == 0)


def _perm_cols(w):
    return jnp.concatenate([w[:, :3072], w[:, 3584:5632], w[:, 3072:3584]], axis=1)


def _unperm_cols(w):
    return jnp.concatenate([w[:, :3072], w[:, 5120:5632], w[:, 3072:5120]], axis=1)


def _sigmoid(x):
    return 1.0 / (1.0 + jnp.exp(-x))


def _expm1(x):
    p = x * (1.0 + x * (0.5 + x * (1.0 / 6 + x * (1.0 / 24 + x * (1.0 / 120 + x * (1.0 / 720))))))
    return jnp.where(jnp.abs(x) < 0.3, p, jnp.exp(x) - 1.0)


def _log1p(x):
    u = 1.0 + x
    d = u - 1.0
    return jnp.where(d == 0.0, x, jnp.log(u) * (x / jnp.where(d == 0.0, 1.0, d)))


def _softplus(x):
    return jnp.maximum(x, 0.0) + _log1p(jnp.exp(-jnp.abs(x)))


def _gelu_and_grad(x):
    c = math.sqrt(2.0 / math.pi)
    inner = c * (x + 0.044715 * (x * x * x))
    t = jnp.tanh(inner)
    gelu = 0.5 * x * (1.0 + t)
    dinner = c * (1.0 + 3 * 0.044715 * (x * x))
    dgelu = 0.5 * (1.0 + t) + 0.5 * x * (1.0 - t * t) * dinner
    return gelu, dgelu


def _rms_bwd(dn, xv, g):
    r = lax.rsqrt(jnp.mean(xv * xv, axis=-1, keepdims=True) + EPS)
    xh = xv * r
    dxh = dn * g
    dx = r * (dxh - xh * jnp.mean(dxh * xh, axis=-1, keepdims=True))
    return dx, dn * xh


def _norm_matmul(x, g, w, name, tm=512, tn=1408):
    S, dm = x.shape
    n = w.shape[1]
    tm = min(tm, S)

    def body(x_ref, g_ref, w_ref, xn_ref, o_ref):
        @pl.when(pl.program_id(1) == 0)
        def _():
            xv = x_ref[...]
            r = lax.rsqrt(jnp.mean(xv * xv, axis=-1, keepdims=True) + EPS)
            xn_ref[...] = ((xv * r) * g_ref[...]).astype(bf16)

        o_ref[...] = jnp.dot(xn_ref[...], w_ref[...], preferred_element_type=f32)

    return pl.pallas_call(
        body, name=name, grid=(S // tm, n // tn),
        in_specs=[pl.BlockSpec((tm, dm), lambda i, j: (i, 0)),
                  pl.BlockSpec((1, dm), lambda i, j: (0, 0)),
                  pl.BlockSpec((dm, tn), lambda i, j: (0, j))],
        out_specs=[pl.BlockSpec((tm, dm), lambda i, j: (i, 0)),
                   pl.BlockSpec((tm, tn), lambda i, j: (i, j))],
        out_shape=[jax.ShapeDtypeStruct((S, dm), bf16), jax.ShapeDtypeStruct((S, n), f32)],
        compiler_params=_cparams())(x, g, w)


def _mm_tn(a, b, name, tk, tn, tmc=512):
    m, ka = a.shape
    n = b.shape[1]
    tmc = min(tmc, m)

    def body(a_ref, b_ref, o_ref):
        @pl.when(pl.program_id(2) == 0)
        def _():
            o_ref[...] = jnp.zeros_like(o_ref)

        o_ref[...] += lax.dot_general(a_ref[...], b_ref[...], (((0,), (0,)), ((), ())),
                                      preferred_element_type=f32)

    return pl.pallas_call(
        body, name=name, grid=(ka // tk, n // tn, m // tmc),
        in_specs=[pl.BlockSpec((tmc, tk), lambda i, j, k: (k, i)),
                  pl.BlockSpec((tmc, tn), lambda i, j, k: (k, j))],
        out_specs=pl.BlockSpec((tk, tn), lambda i, j, k: (i, j)),
        out_shape=jax.ShapeDtypeStruct((ka, n), f32),
        compiler_params=_cparams())(a, b)


def _rows_at(ext, o, tc):
    if o == 0:
        return ext[8:8 + tc]
    return pltpu.roll(ext, (-o) % ext.shape[0], 0)[8:8 + tc]


def _halo_specs(tc, S, width, col):
    per = tc // 8
    last = S // 8 - 1
    return (pl.BlockSpec((tc, width), lambda i: (i, col)),
            pl.BlockSpec((8, width), lambda i: (jnp.maximum(i * per - 1, 0), col)),
            pl.BlockSpec((8, width), lambda i: (jnp.minimum((i + 1) * per, last), col)))


def _extended(cur_ref, prev_ref, next_ref, i, nsteps):
    prev = jnp.where(i > 0, prev_ref[...], 0.0)
    nxt = jnp.where(i < nsteps - 1, next_ref[...], 0.0)
    return jnp.concatenate([prev, cur_ref[...], nxt], axis=0)


def _conv_fwd(proj, cw, cb, tc=512):
    S = proj.shape[0]
    tc = min(tc, S)
    nsteps = S // tc

    def body(cur_ref, prev_ref, next_ref, w_ref, b_ref, o_ref):
        ext = _extended(cur_ref, prev_ref, next_ref, pl.program_id(0), nsteps)
        acc = _rows_at(ext, -2, tc) * w_ref[0:1, :]
        for k in range(1, 4):
            acc = acc + _rows_at(ext, k - 2, tc) * w_ref[k:k + 1, :]
        o_ref[...] = acc + b_ref[...]

    return pl.pallas_call(
        body, name="conv_fwd", grid=(nsteps,),
        in_specs=[*_halo_specs(tc, S, D, 0),
                  pl.BlockSpec((4, D), lambda i: (0, 0)), pl.BlockSpec((1, D), lambda i: (0, 0))],
        out_specs=pl.BlockSpec((tc, D), lambda i: (i, 0)),
        out_shape=jax.ShapeDtypeStruct((S, D), f32),
        compiler_params=_cparams())(proj, proj, proj, cw, cb)


def _conv_bwd(duc_f, duc_b, proj, cw, tc=512):
    S = proj.shape[0]
    tc = min(tc, S)
    nsteps = S // tc

    def body(fc, fp, fn, bc, bp, bn, uc_, up, un, w_ref, du_ref, dw_ref, db_ref):
        i = pl.program_id(0)

        @pl.when(i == 0)
        def _():
            dw_ref[...] = jnp.zeros_like(dw_ref)
            db_ref[...] = jnp.zeros_like(db_ref)

        dext = _extended(fc, fp, fn, i, nsteps) + _extended(bc, bp, bn, i, nsteps)
        uext = _extended(uc_, up, un, i, nsteps)
        d = dext[8:8 + tc]
        acc = _rows_at(dext, 2, tc) * w_ref[0:1, :]
        for k in range(1, 4):
            acc = acc + _rows_at(dext, 2 - k, tc) * w_ref[k:k + 1, :]
        du_ref[...] = acc.astype(bf16)
        wrow = lax.broadcasted_iota(jnp.int32, (4, D), 0)
        for k in range(4):
            dw_ref[...] += jnp.where(wrow == k, jnp.sum(d * _rows_at(uext, k - 2, tc), axis=0, keepdims=True), 0.0)
        db_ref[...] += jnp.sum(d, axis=0, keepdims=True)

    return pl.pallas_call(
        body, name="conv_bwd", grid=(nsteps,),
        in_specs=[*_halo_specs(tc, S, D, 0), *_halo_specs(tc, S, D, 0), *_halo_specs(tc, S, D, 0),
                  pl.BlockSpec((4, D), lambda i: (0, 0))],
        out_specs=[pl.BlockSpec((tc, D), lambda i: (i, 0)),
                   pl.BlockSpec((4, D), lambda i: (0, 0)), pl.BlockSpec((1, D), lambda i: (0, 0))],
        out_shape=[jax.ShapeDtypeStruct((S, D), bf16), jax.ShapeDtypeStruct((4, D), f32),
                   jax.ShapeDtypeStruct((1, D), f32)],
        compiler_params=_cparams())(duc_f, duc_f, duc_f, duc_b, duc_b, duc_b, proj, proj, proj, cw)


def _scan_chunk(a, b, carry, reverse):
    tc = a.shape[0]
    row = lax.broadcasted_iota(jnp.int32, a.shape, 0)
    s = 1
    while s < tc:
        shift = tc - s if reverse else s
        keep = (row < tc - s) if reverse else (row >= s)
        a_sh = pltpu.roll(a, shift, 0)
        b_sh = pltpu.roll(b, shift, 0)
        b = jnp.where(keep, a * b_sh + b, b)
        a = jnp.where(keep, a * a_sh, a)
        s *= 2
    return b + a * carry


def _lru_gates(uc, w, p_ref):
    pre = jnp.dot(uc.astype(bf16), w, preferred_element_type=f32)
    r = _sigmoid(pre[:, :LRU_GW] + p_ref[0, 1:2, :])
    gi = _sigmoid(pre[:, LRU_GW:] + p_ref[0, 2:3, :])
    sp = _softplus(-p_ref[0, 0:1, :])
    log_a = -RGLRU_C * r * sp
    a = jnp.exp(log_a)
    beta = jnp.sqrt(jnp.maximum(-_expm1(2.0 * log_a), 0.0))
    return r, gi, sp, a, beta


def _lru_fwd(uc, wg, lp, reverse):
    S = uc.shape[0]
    tc = LRU_CHUNK
    nc = S // tc
    d = 1 if reverse else 0

    def cidx(c):
        return nc - 1 - c if reverse else c

    def body(uc_ref, w_ref, p_ref, h_ref, carry_ref):
        @pl.when(pl.program_id(1) == 0)
        def _():
            carry_ref[...] = jnp.zeros_like(carry_ref)

        ucv = uc_ref[...]
        _, gi, _, a, beta = _lru_gates(ucv, w_ref[0], p_ref)
        h_ref[...] = _scan_chunk(a, beta * (gi * ucv), carry_ref[...], reverse)
        carry_ref[...] = h_ref[0:1, :] if reverse else h_ref[tc - 1:tc, :]

    return pl.pallas_call(
        body, name="lru_fwd_rev" if reverse else "lru_fwd", grid=(LRU_GROUPS, nc),
        in_specs=[pl.BlockSpec((tc, LRU_GW), lambda g, c: (cidx(c), g)),
                  pl.BlockSpec((1, LRU_GW, 2 * LRU_GW), lambda g, c: (g, 0, d)),
                  pl.BlockSpec((1, 8, LRU_GW), lambda g, c: (d, 0, g))],
        out_specs=pl.BlockSpec((tc, LRU_GW), lambda g, c: (cidx(c), g)),
        out_shape=jax.ShapeDtypeStruct((S, D), f32),
        scratch_shapes=[pltpu.VMEM((1, LRU_GW), f32)],
        compiler_params=_cparams())(uc, wg, lp)


def _lru_bwd(uc, dh, h, wg, wgt, lp, reverse):
    S = uc.shape[0]
    tc = LRU_CHUNK
    nc = S // tc
    d = 1 if reverse else 0
    per = tc // 8
    last8 = S // 8 - 1

    def cidx(c):
        return c if reverse else nc - 1 - c

    def halo_idx(c):
        if reverse:
            return jnp.minimum((cidx(c) + 1) * per, last8)
        return jnp.maximum(cidx(c) * per - 1, 0)

    def body(uc_ref, dh_ref, h_ref, halo_ref, w_ref, wt_ref, p_ref, duc_ref, dw_ref, dp_ref, carry_ref, tmp_ref):
        c = pl.program_id(1)
        ci = cidx(c)

        @pl.when(c == 0)
        def _():
            carry_ref[...] = jnp.zeros_like(carry_ref)
            dw_ref[...] = jnp.zeros_like(dw_ref)
            dp_ref[...] = jnp.zeros_like(dp_ref)

        ucv = uc_ref[...]
        ucb = ucv.astype(bf16)
        r, gi, sp, a, beta = _lru_gates(ucv, w_ref[0], p_ref)
        row = lax.broadcasted_iota(jnp.int32, a.shape, 0)
        hv = h_ref[...]
        if reverse:
            alpha = jnp.where(row == 0, 1.0, pltpu.roll(a, 1, 0))
            gsc = _scan_chunk(alpha, dh_ref[...], carry_ref[...], False)
            edge = jnp.where(ci < nc - 1, halo_ref[0:1, :], 0.0)
            h_nb = jnp.where(row == tc - 1, edge, pltpu.roll(hv, tc - 1, 0))
        else:
            alpha = jnp.where(row == tc - 1, 1.0, pltpu.roll(a, tc - 1, 0))
            gsc = _scan_chunk(alpha, dh_ref[...], carry_ref[...], True)
            edge = jnp.where(ci > 0, halo_ref[7:8, :], 0.0)
            h_nb = jnp.where(row == 0, edge, pltpu.roll(hv, 1, 0))
        tmp_ref[...] = a * gsc
        carry_ref[...] = tmp_ref[tc - 1:tc, :] if reverse else tmp_ref[0:1, :]

        da = gsc * h_nb
        iu = gi * ucv
        dbeta = gsc * iu
        dl = da * a - dbeta * (a * a) / beta
        dr = dl * (-RGLRU_C * sp)
        dsp = jnp.sum(dl * (-RGLRU_C * r), axis=0, keepdims=True)
        dgi = gsc * beta * ucv
        dpre_r = dr * r * (1.0 - r)
        dpre_i = dgi * gi * (1.0 - gi)
        dpre = jnp.concatenate([dpre_r, dpre_i], axis=1).astype(bf16)
        duc_ref[...] = gsc * beta * gi + jnp.dot(dpre, wt_ref[0], preferred_element_type=f32)
        dw_ref[0] += lax.dot_general(ucb, dpre, (((0,), (0,)), ((), ())), preferred_element_type=f32)
        dlam = dsp * (-_sigmoid(-p_ref[0, 0:1, :]))
        prow = lax.broadcasted_iota(jnp.int32, (8, LRU_GW), 0)
        dp_ref[...] += (jnp.where(prow == 0, dlam, 0.0)
                        + jnp.where(prow == 1, jnp.sum(dpre_r, axis=0, keepdims=True), 0.0)
                        + jnp.where(prow == 2, jnp.sum(dpre_i, axis=0, keepdims=True), 0.0))

    chunk = pl.BlockSpec((tc, LRU_GW), lambda g, c: (cidx(c), g))
    return pl.pallas_call(
        body, name="lru_bwd_rev" if reverse else "lru_bwd", grid=(LRU_GROUPS, nc),
        in_specs=[chunk, chunk, chunk,
                  pl.BlockSpec((8, LRU_GW), lambda g, c: (halo_idx(c), g)),
                  pl.BlockSpec((1, LRU_GW, 2 * LRU_GW), lambda g, c: (g, 0, d)),
                  pl.BlockSpec((1, 2 * LRU_GW, LRU_GW), lambda g, c: (g, d, 0)),
                  pl.BlockSpec((1, 8, LRU_GW), lambda g, c: (d, 0, g))],
        out_specs=[chunk,
                   pl.BlockSpec((1, LRU_GW, 2 * LRU_GW), lambda g, c: (g, 0, 0)),
                   pl.BlockSpec((8, LRU_GW), lambda g, c: (0, g))],
        out_shape=[jax.ShapeDtypeStruct((S, D), f32),
                   jax.ShapeDtypeStruct((LRU_GROUPS, LRU_GW, 2 * LRU_GW), f32),
                   jax.ShapeDtypeStruct((8, D), f32)],
        scratch_shapes=[pltpu.VMEM((1, LRU_GW), f32), pltpu.VMEM((tc, LRU_GW), f32)],
        compiler_params=_cparams())(uc, dh, h, h, wg, wgt, lp)


def _attn_mask_parts(n, nb):
    q_loc = lax.broadcasted_iota(jnp.int32, (BLK, 3 * BLK), 0)
    k_loc = lax.broadcasted_iota(jnp.int32, (BLK, 3 * BLK), 1)
    dist = q_loc + BLK - k_loc
    adist = jnp.abs(dist)
    kpos = n * BLK - BLK + k_loc
    valid = (adist <= WINDOW) & (kpos >= 0) & (kpos < nb * BLK)
    return valid, adist.astype(f32)


def _attn_probs(qm, k2, valid, adist, slope, sink):
    s = lax.dot_general(qm, k2, (((1,), (1,)), ((), ())), preferred_element_type=f32)
    s = jnp.where(valid, s + (-slope) * adist, NEG_INF)
    m = jnp.maximum(jnp.max(s, axis=-1, keepdims=True), sink)
    p = jnp.exp(s - m)
    ps = jnp.exp(sink - m)
    denom = jnp.sum(p, axis=-1, keepdims=True) + ps
    return p / denom, ps / denom


def _slope(h):
    return 2.0 ** (-8.0 * (h + 1.0) / N_HEADS)


def _window_specs():
    return [pl.BlockSpec((BLK, 512), lambda n: (n, 0)), pl.BlockSpec((BLK, 512), lambda n: (n + 1, 0)),
            pl.BlockSpec((BLK, 512), lambda n: (n + 2, 0))]


def _window(r0, r1, r2, kv):
    cols = slice(kv * 128, (kv + 1) * 128)
    return jnp.concatenate([r0[:, cols], r1[:, cols], r2[:, cols]], axis=0)


def _attn_fwd(proj, k2, v2, sink):
    S = proj.shape[0]
    nb = S // BLK

    def body(q_ref, k0, k1, k2_, v0, v1, v2_, sink_ref, o_ref):
        n = pl.program_id(0)
        valid, adist = _attn_mask_parts(n, nb)
        left = lax.broadcasted_iota(jnp.int32, (BLK, 2 * HEAD_DIM), 1) < HEAD_DIM
        outs = []
        for pair in range(N_HEADS // 2):
            kv = pair // 2
            qp = q_ref[:, pair * 128:(pair + 1) * 128] * (HEAD_DIM ** -0.5)
            kk = _window(k0, k1, k2_, kv)
            vv = _window(v0, v1, v2_, kv)
            halves = []
            for sub in range(2):
                h = 2 * pair + sub
                qm = jnp.where(left if sub == 0 else ~left, qp, 0.0).astype(bf16)
                pn, _ = _attn_probs(qm, kk, valid, adist, _slope(h), sink_ref[0, h])
                halves.append(jnp.dot(pn.astype(bf16), vv, preferred_element_type=f32))
            outs.append(jnp.where(left, halves[0], halves[1]))
        o_ref[...] = jnp.concatenate(outs, axis=1)

    return pl.pallas_call(
        body, name="attn_fwd", grid=(nb,),
        in_specs=[pl.BlockSpec((BLK, D), lambda n: (n, C_Q // D)), *_window_specs(), *_window_specs(),
                  pl.BlockSpec(memory_space=pltpu.SMEM)],
        out_specs=pl.BlockSpec((BLK, D), lambda n: (n, 0)),
        out_shape=jax.ShapeDtypeStruct((S, D), f32),
        compiler_params=_cparams())(proj, k2, k2, k2, v2, v2, v2, sink)


def _attn_bwd(proj, k2, v2, sink, dyb):
    S = proj.shape[0]
    nb = S // BLK

    def body(q_ref, k0, k1, k2_, v0, v1, v2_, sink_ref, do_ref, dq_ref, dk_ref, dv_ref, ds_ref):
        n = pl.program_id(0)

        @pl.when(n == 0)
        def _():
            dk_ref[...] = jnp.zeros_like(dk_ref)
            dv_ref[...] = jnp.zeros_like(dv_ref)
            ds_ref[...] = jnp.zeros_like(ds_ref)

        valid, adist = _attn_mask_parts(n, nb)
        left = lax.broadcasted_iota(jnp.int32, (BLK, 2 * HEAD_DIM), 1) < HEAD_DIM
        lane = lax.broadcasted_iota(jnp.int32, (1, 128), 1)
        start = pl.multiple_of(n * BLK, BLK)
        left3 = lax.broadcasted_iota(jnp.int32, (3 * BLK, 2 * HEAD_DIM), 1) < HEAD_DIM
        dqs = []
        dks, dvs = [], []
        dsink = jnp.zeros((1, 128), f32)
        for kv in range(N_KV):
            kk = _window(k0, k1, k2_, kv)
            vv = _window(v0, v1, v2_, kv)
            ds_rows, pn_rows, q_rows, do_rows = [], [], [], []
            for pp in range(2):
                pair = 2 * kv + pp
                qp = q_ref[:, pair * 128:(pair + 1) * 128] * (HEAD_DIM ** -0.5)
                dop = do_ref[:, pair * 128:(pair + 1) * 128]
                dq_halves = []
                for sub in range(2):
                    h = 2 * pair + sub
                    sel = left if sub == 0 else ~left
                    qm = jnp.where(sel, qp, 0.0).astype(bf16)
                    dom = jnp.where(sel, dop, jnp.zeros_like(dop))
                    pn, psn = _attn_probs(qm, kk, valid, adist, _slope(h), sink_ref[0, h])
                    dp = lax.dot_general(dom, vv, (((1,), (1,)), ((), ())), preferred_element_type=f32)
                    delta = jnp.sum(pn * dp, axis=-1, keepdims=True)
                    dsc = (pn * (dp - delta)).astype(bf16)
                    dsink = dsink + jnp.where(lane == h, -jnp.sum(delta * psn), 0.0)
                    dq_halves.append(jnp.dot(dsc, kk, preferred_element_type=f32))
                    ds_rows.append(dsc)
                    pn_rows.append(pn.astype(bf16))
                    q_rows.append(qm)
                    do_rows.append(dom)
                dqs.append(jnp.where(left, dq_halves[0], dq_halves[1]) * (HEAD_DIM ** -0.5))
            dsa = jnp.concatenate(ds_rows, axis=0)
            pna = jnp.concatenate(pn_rows, axis=0)
            qa = jnp.concatenate(q_rows, axis=0)
            doa = jnp.concatenate(do_rows, axis=0)
            dk = lax.dot_general(dsa, qa, (((0,), (0,)), ((), ())), preferred_element_type=f32)
            dv = lax.dot_general(pna, doa, (((0,), (0,)), ((), ())), preferred_element_type=f32)
            dks.append(dk + pltpu.roll(dk, HEAD_DIM, 1))
            dvs.append(dv + pltpu.roll(dv, HEAD_DIM, 1))
        for jp in range(N_KV // 2):
            cols = slice(jp * 128, (jp + 1) * 128)
            dk_ref[pl.ds(start, 3 * BLK), cols] += jnp.where(left3, dks[2 * jp], dks[2 * jp + 1])
            dv_ref[pl.ds(start, 3 * BLK), cols] += jnp.where(left3, dvs[2 * jp], dvs[2 * jp + 1])
        dq_ref[...] = jnp.concatenate(dqs, axis=1).astype(bf16)
        ds_ref[...] += dsink

    whole = pl.BlockSpec((S + 2 * BLK, N_KV * HEAD_DIM), lambda n: (0, 0))
    return pl.pallas_call(
        body, name="attn_bwd", grid=(nb,),
        in_specs=[pl.BlockSpec((BLK, D), lambda n: (n, C_Q // D)), *_window_specs(), *_window_specs(),
                  pl.BlockSpec(memory_space=pltpu.SMEM),
                  pl.BlockSpec((BLK, D), lambda n: (n, 0))],
        out_specs=[pl.BlockSpec((BLK, D), lambda n: (n, 0)), whole, whole,
                   pl.BlockSpec((1, 128), lambda n: (0, 0))],
        out_shape=[jax.ShapeDtypeStruct((S, D), bf16),
                   jax.ShapeDtypeStruct((S + 2 * BLK, N_KV * HEAD_DIM), f32),
                   jax.ShapeDtypeStruct((S + 2 * BLK, N_KV * HEAD_DIM), f32),
                   jax.ShapeDtypeStruct((1, 128), f32)],
        compiler_params=_cparams())(proj, k2, k2, k2, v2, v2, v2, sink, dyb)


def _merge_parts(hf, hb, g, z0, z1, yb, bg):
    g0 = _sigmoid(z0 + bg[:, :D])
    g1 = _sigmoid(z1 + bg[:, D:])
    gelu, dgelu = _gelu_and_grad(g)
    hs = hf + hb
    ya = hs * gelu
    return g0, g1, gelu, dgelu, hs, ya


def _merge_outproj(x, hf, hb, proj, yb, bg, w_out, tm=512):
    S = x.shape[0]
    tm = min(tm, S)

    def body(x_ref, hf_ref, hb_ref, g_ref, z0_ref, z1_ref, yb_ref, bg_ref, w_ref, mg_ref, x1_ref):
        ybv = yb_ref[...]
        g0, g1, _, _, _, ya = _merge_parts(hf_ref[...], hb_ref[...], g_ref[...], z0_ref[...], z1_ref[...],
                                           ybv, bg_ref[...])
        mg = (g0 * ya + g1 * ybv).astype(bf16)
        mg_ref[...] = mg
        x1_ref[...] = x_ref[...] + jnp.dot(mg, w_ref[...], preferred_element_type=f32)

    row = pl.BlockSpec((tm, D), lambda i: (i, 0))
    return pl.pallas_call(
        body, name="merge_outproj", grid=(S // tm,),
        in_specs=[row, row, row,
                  pl.BlockSpec((tm, D), lambda i: (i, C_G // D)),
                  pl.BlockSpec((tm, D), lambda i: (i, C_Z0 // D)),
                  pl.BlockSpec((tm, D), lambda i: (i, C_Z1 // D)),
                  row, pl.BlockSpec((1, 2 * D), lambda i: (0, 0)), pl.BlockSpec((D, D), lambda i: (0, 0))],
        out_specs=[row, row],
        out_shape=[jax.ShapeDtypeStruct((S, D), bf16), jax.ShapeDtypeStruct((S, D), f32)],
        compiler_params=_cparams())(x, hf, hb, proj, proj, proj, yb, bg, w_out)


def _ffn_out_loss(gu, x1, w_fo, g3, tgt, tm=256):
    S = x1.shape[0]
    tm = min(tm, S)

    def body(gt_ref, up_ref, x1_ref, w_ref, g_ref, t_ref, ff_ref, dx_ref, dxb_ref, loss_ref, dg_ref):
        @pl.when(pl.program_id(0) == 0)
        def _():
            loss_ref[...] = jnp.zeros_like(loss_ref)
            dg_ref[...] = jnp.zeros_like(dg_ref)

        gt = gt_ref[...]
        ff = ((gt * _sigmoid(gt)) * up_ref[...]).astype(bf16)
        ff_ref[...] = ff
        x2 = x1_ref[...] + jnp.dot(ff, w_ref[...], preferred_element_type=f32)
        gv = g_ref[...]
        r = lax.rsqrt(jnp.mean(x2 * x2, axis=-1, keepdims=True) + EPS)
        xh = x2 * r
        diff = xh * gv - t_ref[...]
        loss_ref[...] += (0.5 / D) * jnp.sum(diff * diff)
        dy = diff * (1.0 / D)
        dg_ref[...] += jnp.sum(dy * xh, axis=0, keepdims=True)
        dxh = dy * gv
        dx = r * (dxh - xh * jnp.mean(dxh * xh, axis=-1, keepdims=True))
        dx_ref[...] = dx
        dxb_ref[...] = dx.astype(bf16)

    row = pl.BlockSpec((tm, D), lambda i: (i, 0))
    vec = pl.BlockSpec((1, D), lambda i: (0, 0))
    return pl.pallas_call(
        body, name="ffn_out_loss", grid=(S // tm,),
        in_specs=[pl.BlockSpec((tm, D_FF), lambda i: (i, 0)), pl.BlockSpec((tm, D_FF), lambda i: (i, 1)),
                  row, pl.BlockSpec((D_FF, D), lambda i: (0, 0)), vec, row],
        out_specs=[pl.BlockSpec((tm, D_FF), lambda i: (i, 0)), row, row,
                   pl.BlockSpec((1, 128), lambda i: (0, 0)), vec],
        out_shape=[jax.ShapeDtypeStruct((S, D_FF), bf16), jax.ShapeDtypeStruct((S, D), f32),
                   jax.ShapeDtypeStruct((S, D), bf16), jax.ShapeDtypeStruct((1, 128), f32),
                   jax.ShapeDtypeStruct((1, D), f32)],
        compiler_params=_cparams())(gu, gu, x1, w_fo, g3, tgt)


def _ffn_bwd1(dx2b, w_fot, gu, tm=256):
    S = dx2b.shape[0]
    tm = min(tm, S)

    def body(dx_ref, w_ref, gt_ref, up_ref, dgt_ref, dup_ref):
        dff = jnp.dot(dx_ref[...], w_ref[...], preferred_element_type=f32)
        gt = gt_ref[...]
        sg = _sigmoid(gt)
        dup_ref[...] = (dff * (gt * sg)).astype(bf16)
        dgt_ref[...] = ((dff * up_ref[...]) * (sg * (1.0 + gt * (1.0 - sg)))).astype(bf16)

    wide = pl.BlockSpec((tm, D_FF), lambda i: (i, 0))
    return pl.pallas_call(
        body, name="ffn_bwd1", grid=(S // tm,),
        in_specs=[pl.BlockSpec((tm, D), lambda i: (i, 0)), pl.BlockSpec((D, D_FF), lambda i: (0, 0)),
                  wide, pl.BlockSpec((tm, D_FF), lambda i: (i, 1))],
        out_specs=[wide, wide],
        out_shape=[jax.ShapeDtypeStruct((S, D_FF), bf16), jax.ShapeDtypeStruct((S, D_FF), bf16)],
        compiler_params=_cparams())(dx2b, w_fot, gu, gu)


def _proj_bwd(pieces, wts, xres, g, dres, name, tm=256):
    S = xres.shape[0]
    tm = min(tm, S)
    np_ = len(pieces)

    def body(*refs):
        p_refs = refs[:np_]
        w_refs = refs[np_:2 * np_]
        x_ref, g_ref, dres_ref, dx_ref, dxb_ref, dg_ref = refs[2 * np_:]

        @pl.when(pl.program_id(0) == 0)
        def _():
            dg_ref[...] = jnp.zeros_like(dg_ref)

        dn = jnp.dot(p_refs[0][...], w_refs[0][...], preferred_element_type=f32)
        for pr, wr in zip(p_refs[1:], w_refs[1:]):
            dn = dn + jnp.dot(pr[...], wr[...], preferred_element_type=f32)
        dxn, dgc = _rms_bwd(dn, x_ref[...], g_ref[...])
        dx = dres_ref[...] + dxn
        dx_ref[...] = dx
        dxb_ref[...] = dx.astype(bf16)
        dg_ref[...] += jnp.sum(dgc, axis=0, keepdims=True)

    row = pl.BlockSpec((tm, D), lambda i: (i, 0))
    vec = pl.BlockSpec((1, D), lambda i: (0, 0))
    return pl.pallas_call(
        body, name=name, grid=(S // tm,),
        in_specs=[*[pl.BlockSpec((tm, p.shape[1]), lambda i: (i, 0)) for p in pieces],
                  *[pl.BlockSpec(w.shape, lambda i: (0, 0)) for w in wts],
                  row, vec, row],
        out_specs=[row, row, vec],
        out_shape=[jax.ShapeDtypeStruct((S, D), f32), jax.ShapeDtypeStruct((S, D), bf16),
                   jax.ShapeDtypeStruct((1, D), f32)],
        compiler_params=_cparams())(*pieces, *wts, xres, g, dres)


def _outproj_bwd(dx1b, w_outt, hf, hb, proj, yb, bg, tm=512):
    S = dx1b.shape[0]
    tm = min(tm, S)

    def body(dx_ref, w_ref, hf_ref, hb_ref, g_ref, z0_ref, z1_ref, yb_ref, bg_ref,
             dh_ref, dg_ref, dz_ref, dyb_ref, dbg_ref):
        @pl.when(pl.program_id(0) == 0)
        def _():
            dbg_ref[...] = jnp.zeros_like(dbg_ref)

        dm = jnp.dot(dx_ref[...], w_ref[...], preferred_element_type=f32)
        ybv = yb_ref[...]
        g0, g1, gelu, dgelu, hs, ya = _merge_parts(hf_ref[...], hb_ref[...], g_ref[...], z0_ref[...],
                                                   z1_ref[...], ybv, bg_ref[...])
        dya = dm * g0
        dh_ref[...] = dya * gelu
        dg_ref[...] = (dya * hs * dgelu).astype(bf16)
        dyb_ref[...] = (dm * g1).astype(bf16)
        dz0 = (dm * ya) * (g0 * (1.0 - g0))
        dz1 = (dm * ybv) * (g1 * (1.0 - g1))
        dz = jnp.concatenate([dz0, dz1], axis=1)
        dz_ref[...] = dz.astype(bf16)
        dbg_ref[...] += jnp.sum(dz, axis=0, keepdims=True)

    row = pl.BlockSpec((tm, D), lambda i: (i, 0))
    return pl.pallas_call(
        body, name="outproj_bwd", grid=(S // tm,),
        in_specs=[row, pl.BlockSpec((D, D), lambda i: (0, 0)), row, row,
                  pl.BlockSpec((tm, D), lambda i: (i, C_G // D)),
                  pl.BlockSpec((tm, D), lambda i: (i, C_Z0 // D)),
                  pl.BlockSpec((tm, D), lambda i: (i, C_Z1 // D)),
                  row, pl.BlockSpec((1, 2 * D), lambda i: (0, 0))],
        out_specs=[row, row, pl.BlockSpec((tm, 2 * D), lambda i: (i, 0)), row,
                   pl.BlockSpec((1, 2 * D), lambda i: (0, 0))],
        out_shape=[jax.ShapeDtypeStruct((S, D), f32), jax.ShapeDtypeStruct((S, D), bf16),
                   jax.ShapeDtypeStruct((S, 2 * D), bf16), jax.ShapeDtypeStruct((S, D), bf16),
                   jax.ShapeDtypeStruct((1, 2 * D), f32)],
        compiler_params=_cparams())(dx1b, w_outt, hf, hb, proj, proj, proj, yb, bg)


def _block_diag_groups(w):
    w4 = w.reshape(LRU_GROUPS, 4, LRU_BLOCK, LRU_BLOCK)
    eye = jnp.eye(4, dtype=w.dtype)
    return jnp.einsum("ghij,hk->ghikj", w4, eye).reshape(LRU_GROUPS, LRU_GW, LRU_GW)


def _diag_blocks(dw):
    d5 = dw.reshape(LRU_GROUPS, 4, LRU_BLOCK, 4, LRU_BLOCK)
    return jnp.stack([d5[:, h, :, h, :] for h in range(4)], axis=1).reshape(LRU_HEADS, LRU_BLOCK, LRU_BLOCK)


def _dup_heads(t, S):
    t4 = t.astype(bf16).reshape(S, N_KV, HEAD_DIM)
    t2 = jnp.concatenate([t4, t4], axis=-1).reshape(S, 2 * N_KV * HEAD_DIM)
    return jnp.pad(t2, ((BLK, BLK), (0, 0)))


def _local_step(x, tgt, w_in_p, w_out, w_fi, w_fo, small):
    S = x.shape[0]
    g1, g2, g3 = small["norm_mix_g"], small["norm_ffn_g"], small["norm_final_g"]
    bg, cw, cb = small["b_gate"], small["conv_w"], small["conv_b"]
    sink = small["attn_sink"]

    wg = jnp.concatenate([_block_diag_groups(small["lru_wa"][0]), _block_diag_groups(small["lru_wx"][0]),
                          _block_diag_groups(small["lru_wa"][1]), _block_diag_groups(small["lru_wx"][1])],
                         axis=2).astype(bf16)
    wgt = jnp.swapaxes(wg, 1, 2)
    zeros5 = jnp.zeros((5, D), f32)
    lp = jnp.stack([jnp.concatenate([small["lru_lambda"][d:d + 1], small["lru_ba"][d:d + 1],
                                     small["lru_bx"][d:d + 1], zeros5], axis=0) for d in range(2)])

    xn, proj = _norm_matmul(x, g1, w_in_p, "norm_inproj")
    uc = _conv_fwd(proj, cw, cb)
    hf = _lru_fwd(uc, wg, lp, False)
    hb = _lru_fwd(uc, wg, lp, True)
    k2 = _dup_heads(proj[:, C_K:C_V], S)
    v2 = _dup_heads(proj[:, C_V:], S)
    yb = _attn_fwd(proj, k2, v2, sink)
    merged, x1 = _merge_outproj(x, hf, hb, proj, yb, bg, w_out)
    xn2, gu = _norm_matmul(x1, g2, w_fi, "norm_ffn_in")
    ff, dx2, dx2b, loss, dg3 = _ffn_out_loss(gu, x1, w_fo, g3, tgt)

    dgt, dup = _ffn_bwd1(dx2b, w_fo.T, gu)
    w_fit = w_fi.T
    dx1, dx1b, dg2 = _proj_bwd([dgt, dup], [w_fit[:D_FF], w_fit[D_FF:]], x1, g2, dx2, "ffn_in_bwd")
    dw_fo = _mm_tn(ff, dx2b, "dw_ffn_out", tk=1408, tn=1024)
    dw_fi = jnp.concatenate([_mm_tn(xn2, dgt, "dw_ffn_in_gate", tk=1024, tn=1408),
                             _mm_tn(xn2, dup, "dw_ffn_in_up", tk=1024, tn=1408)], axis=1)
    dh, dgl, dz, dyb, dbg = _outproj_bwd(dx1b, w_out.T, hf, hb, proj, yb, bg)
    dw_out = _mm_tn(merged, dx1b, "dw_out", tk=1024, tn=1024)
    dq, dk2, dv2, dsink = _attn_bwd(proj, k2, v2, sink, dyb)
    dkv = jnp.concatenate([dk2[BLK:BLK + S], dv2[BLK:BLK + S]], axis=1).astype(bf16)
    duc_f, dwg_f, dp_f = _lru_bwd(uc, dh, hf, wg, wgt, lp, False)
    duc_b, dwg_b, dp_b = _lru_bwd(uc, dh, hb, wg, wgt, lp, True)
    du, dcw, dcb = _conv_bwd(duc_f, duc_b, proj, cw)
    w_int = w_in_p.T
    pieces = [du, dgl, dq, dz, dkv]
    bounds = [0, 1024, 2048, 3072, 5120, 5632]
    dx, _, dg1 = _proj_bwd(pieces, [w_int[bounds[i]:bounds[i + 1]] for i in range(5)], x, g1, dx1, "inproj_bwd")
    dw_in_p = jnp.concatenate(
        [_mm_tn(xn, p, "dw_in_%d" % i, tk=1024, tn=min(p.shape[1], 1024)) for i, p in enumerate(pieces)], axis=1)

    grads = {
        "norm_mix_g": dg1, "w_in": _unperm_cols(dw_in_p), "b_gate": dbg, "conv_w": dcw, "conv_b": dcb,
        "lru_lambda": jnp.concatenate([dp_f[0:1], dp_b[0:1]], axis=0),
        "lru_wa": jnp.stack([_diag_blocks(dwg_f[:, :, :LRU_GW]), _diag_blocks(dwg_b[:, :, :LRU_GW])]),
        "lru_ba": jnp.concatenate([dp_f[1:2], dp_b[1:2]], axis=0),
        "lru_wx": jnp.stack([_diag_blocks(dwg_f[:, :, LRU_GW:]), _diag_blocks(dwg_b[:, :, LRU_GW:])]),
        "lru_bx": jnp.concatenate([dp_f[2:3], dp_b[2:3]], axis=0),
        "attn_sink": dsink[:, :N_HEADS], "w_out": dw_out, "norm_ffn_g": dg2, "w_ffn_in": dw_fi,
        "w_ffn_out": dw_fo, "norm_final_g": dg3,
    }
    return loss, dx, grads


def _exchange(src, name, scatter):
    block = src.shape[1:] if scatter else src.shape

    def body(src_ref, out_ref, send_sems, recv_sems, local_sem):
        x, y, c = lax.axis_index("x"), lax.axis_index("y"), lax.axis_index("c")
        me = 4 * x + 2 * y + c
        local = pltpu.make_async_copy(src_ref.at[me] if scatter else src_ref, out_ref.at[me], local_sem)
        local.start()
        copies = []
        for k in range(1, N_DEV):
            px, py, pc = x ^ (k >> 2), y ^ ((k >> 1) & 1), c ^ (k & 1)
            peer = 4 * px + 2 * py + pc
            cp = pltpu.make_async_remote_copy(
                src_ref=src_ref.at[peer] if scatter else src_ref, dst_ref=out_ref.at[me],
                send_sem=send_sems.at[k - 1], recv_sem=recv_sems.at[k - 1],
                device_id=(px, py, pc), device_id_type=pl.DeviceIdType.MESH)
            cp.start()
            copies.append(cp)
        for cp in copies:
            cp.wait()
        local.wait()

    return pl.pallas_call(
        body, name=name,
        in_specs=[pl.BlockSpec(memory_space=pl.ANY)],
        out_specs=pl.BlockSpec(memory_space=pl.ANY),
        out_shape=jax.ShapeDtypeStruct((N_DEV, *block), src.dtype),
        scratch_shapes=[pltpu.SemaphoreType.DMA((N_DEV - 1,)), pltpu.SemaphoreType.DMA((N_DEV - 1,)),
                        pltpu.SemaphoreType.DMA],
    )(src)


def _adamw(gparts, w, m, v, name, tr=256):
    n, rows, cols = gparts.shape
    tr = _div_tile(rows, tr)
    c1 = 1.0 - ADAM_B1 ** ADAM_STEP
    c2 = 1.0 - ADAM_B2 ** ADAM_STEP

    def body(g_ref, w_ref, m_ref, v_ref, go_ref, d_ref, mo_ref, vo_ref):
        g = g_ref[0]
        for j in range(1, n):
            g = g + g_ref[j]
        mn = ADAM_B1 * m_ref[...] + (1.0 - ADAM_B1) * g
        vn = ADAM_B2 * v_ref[...] + (1.0 - ADAM_B2) * (g * g)
        m_hat = mn / c1
        v_hat = vn / c2
        go_ref[...] = g
        d_ref[...] = -ADAM_LR * (m_hat / (jnp.sqrt(v_hat) + ADAM_EPS) + ADAM_WD * w_ref[...])
        mo_ref[...] = mn
        vo_ref[...] = vn

    blk = pl.BlockSpec((tr, cols), lambda i: (i, 0))
    shp = jax.ShapeDtypeStruct((rows, cols), f32)
    return pl.pallas_call(
        body, name=name, grid=(rows // tr,),
        in_specs=[pl.BlockSpec((n, tr, cols), lambda i: (0, i, 0)), blk, blk, blk],
        out_specs=[blk, blk, blk, blk], out_shape=[shp, shp, shp, shp],
        compiler_params=_cparams())(gparts, w, m, v)


def _sum_parts(parts, name):
    n, rows, cols = parts.shape

    def body(p_ref, o_ref):
        acc = p_ref[0]
        for j in range(1, n):
            acc = acc + p_ref[j]
        o_ref[...] = acc

    return pl.pallas_call(
        body, name=name, out_shape=jax.ShapeDtypeStruct((rows, cols), f32),
        compiler_params=_cparams())(parts)


def _pack_rows(arrs):
    rows, spans, at = [], [], 0
    for a in arrs:
        flat = a.reshape(-1)
        nr = -(-flat.shape[0] // 1024)
        rows.append(jnp.pad(flat, (0, nr * 1024 - flat.shape[0])).reshape(nr, 1024))
        spans.append((at, nr))
        at += nr
    pad = (-at) % 8
    if pad:
        rows.append(jnp.zeros((pad, 1024), f32))
    return jnp.concatenate(rows, axis=0), spans


def _unpack_rows(packed, spans, shapes):
    out = []
    for (at, nr), shp in zip(spans, shapes):
        n = math.prod(shp)
        out.append(packed[at:at + nr].reshape(-1)[:n].reshape(shp))
    return out


BIG = ("w_in", "w_out", "w_ffn_in", "w_ffn_out")
SMALL_REPL = ("norm_mix_g", "b_gate", "conv_b", "lru_wa", "lru_wx", "attn_sink", "norm_ffn_g", "norm_final_g")
SMALL_SHARD = ("conv_w", "lru_lambda", "lru_ba", "lru_bx")
ORDER = ("norm_mix_g", "w_in", "b_gate", "conv_w", "conv_b", "lru_lambda", "lru_wa", "lru_ba", "lru_wx",
         "lru_bx", "attn_sink", "w_out", "norm_ffn_g", "w_ffn_in", "w_ffn_out", "norm_final_g")


def kernel(x, norm_mix_g, w_in, b_gate, conv_w, conv_b, lru_lambda, lru_wa, lru_ba, lru_wx, lru_bx, attn_sink, w_out, norm_ffn_g, w_ffn_in, w_ffn_out, norm_final_g, loss_target, m_norm_mix_g, m_w_in, m_b_gate, m_conv_w, m_conv_b, m_lru_lambda, m_lru_wa, m_lru_ba, m_lru_wx, m_lru_bx, m_attn_sink, m_w_out, m_norm_ffn_g, m_w_ffn_in, m_w_ffn_out, m_norm_final_g, v_norm_mix_g, v_w_in, v_b_gate, v_conv_w, v_conv_b, v_lru_lambda, v_lru_wa, v_lru_ba, v_lru_wx, v_lru_bx, v_attn_sink, v_w_out, v_norm_ffn_g, v_w_ffn_in, v_w_ffn_out, v_norm_final_g):
    w = dict(norm_mix_g=norm_mix_g, w_in=w_in, b_gate=b_gate, conv_w=conv_w, conv_b=conv_b, lru_lambda=lru_lambda,
             lru_wa=lru_wa, lru_ba=lru_ba, lru_wx=lru_wx, lru_bx=lru_bx, attn_sink=attn_sink, w_out=w_out,
             norm_ffn_g=norm_ffn_g, w_ffn_in=w_ffn_in, w_ffn_out=w_ffn_out, norm_final_g=norm_final_g)
    m = dict(norm_mix_g=m_norm_mix_g, w_in=m_w_in, b_gate=m_b_gate, conv_w=m_conv_w, conv_b=m_conv_b,
             lru_lambda=m_lru_lambda, lru_wa=m_lru_wa, lru_ba=m_lru_ba, lru_wx=m_lru_wx, lru_bx=m_lru_bx,
             attn_sink=m_attn_sink, w_out=m_w_out, norm_ffn_g=m_norm_ffn_g, w_ffn_in=m_w_ffn_in,
             w_ffn_out=m_w_ffn_out, norm_final_g=m_norm_final_g)
    v = dict(norm_mix_g=v_norm_mix_g, w_in=v_w_in, b_gate=v_b_gate, conv_w=v_conv_w, conv_b=v_conv_b,
             lru_lambda=v_lru_lambda, lru_wa=v_lru_wa, lru_ba=v_lru_ba, lru_wx=v_lru_wx, lru_bx=v_lru_bx,
             attn_sink=v_attn_sink, w_out=v_w_out, norm_ffn_g=v_norm_ffn_g, w_ffn_in=v_w_ffn_in,
             w_ffn_out=v_w_ffn_out, norm_final_g=v_norm_final_g)
    me = 4 * lax.axis_index("x") + 2 * lax.axis_index("y") + lax.axis_index("c")

    def gather_cols(shard):
        got = _exchange(shard.astype(bf16), "gather_cols_%d" % shard.shape[1], scatter=False)
        return jnp.swapaxes(got, 0, 1).reshape(shard.shape[0], -1)

    def gather_rows(shard, name):
        got = _exchange(shard.astype(bf16), name, scatter=False)
        return got.reshape(-1, shard.shape[1])

    w_in_full = _perm_cols(gather_cols(w_in[0]))
    w_fi_full = gather_cols(w_ffn_in[0])
    w_out_full = gather_rows(w_out[0], "gather_w_out")
    w_fo_full = gather_rows(w_ffn_out[0], "gather_w_ffn_out")
    shard_rows = jnp.concatenate([w[n][0] for n in SMALL_SHARD], axis=0)
    got = _exchange(shard_rows, "gather_small_shards", scatter=False)
    full_rows = jnp.swapaxes(got, 0, 1).reshape(shard_rows.shape[0], -1)
    small = {n: w[n] for n in ("norm_mix_g", "b_gate", "conv_b", "attn_sink", "norm_ffn_g")}
    small["lru_wa"], small["lru_wx"] = lru_wa[0], lru_wx[0]
    small["norm_final_g"] = norm_final_g.reshape(1, D)
    small["conv_w"], small["lru_lambda"] = full_rows[0:4], full_rows[4:6]
    small["lru_ba"], small["lru_bx"] = full_rows[6:8], full_rows[8:10]

    loss_part, grad_x, grads = _local_step(x[0], loss_target[0], w_in_full, w_out_full, w_fi_full, w_fo_full, small)

    outs = {}

    def finish_big(name, parts8, shard_shape):
        recv = _exchange(parts8, "scatter_" + name, scatter=True)
        r2 = lambda a: a.reshape(shard_shape)
        res = _adamw(recv, r2(w[name]), r2(m[name]), r2(v[name]), "adamw_" + name)
        outs[name] = [t.reshape(w[name].shape) for t in res]

    for name in ("w_in", "w_ffn_in"):
        g = grads[name]
        parts8 = jnp.swapaxes(g.reshape(g.shape[0], N_DEV, -1), 0, 1)
        finish_big(name, parts8, parts8.shape[1:])
    for name in ("w_out", "w_ffn_out"):
        g = grads[name]
        parts8 = g.reshape(N_DEV, -1, g.shape[1])
        finish_big(name, parts8, parts8.shape[1:])

    small_names = SMALL_REPL + SMALL_SHARD
    packed, spans = _pack_rows([loss_part[:, :1]] + [grads[n] for n in small_names])
    total = _sum_parts(_exchange(packed, "gather_small_grads", scatter=False), "sum_small_grads")
    shapes = [(1, 1)] + [grads[n].shape for n in small_names]
    summed = dict(zip(("loss",) + small_names, _unpack_rows(total, spans, shapes)))
    loss = summed["loss"].reshape(())
    gsm = {n: summed[n].reshape(w[n].shape) for n in SMALL_REPL}
    for n in SMALL_SHARD:
        full = summed[n]
        gsm[n] = lax.dynamic_slice_in_dim(full, me * 128, 128, axis=1).reshape(w[n].shape)
    pk = lambda dct: _pack_rows([dct[n] for n in small_names])[0]
    gp, sp = _pack_rows([gsm[n] for n in small_names])
    res = _adamw(gp[None], pk(w), pk(m), pk(v), "adamw_small")
    sshapes = [w[n].shape for n in small_names]
    for idx, t in enumerate(res):
        for n, a in zip(small_names, _unpack_rows(t, sp, sshapes)):
            outs.setdefault(n, [None] * 4)[idx] = a

    result = [loss, grad_x[None]]
    for idx in range(4):
        result += [outs[n][idx] for n in ORDER]
    return tuple(result)
```

```python
import functools
import math

import jax
import jax.numpy as jnp
from jax import lax
from jax.experimental import pallas as pl
from jax.experimental.pallas import tpu as pltpu

f32 = jnp.float32
bf16 = jnp.bfloat16

D = 1024
D_FF = 2816
IN_W = 5632
N_HEADS = 16
N_KV = 4
HEAD_DIM = 64
WINDOW = 128
BLK = 128
LRU_HEADS = 16
LRU_BLOCK = 64
LRU_GROUPS = 4
LRU_GW = 256
LRU_CHUNK = 128
RGLRU_C = 8.0
EPS = 1e-6
NEG_INF = -1e30
N_DEV = 8

ADAM_LR = 0.001
ADAM_B1 = 0.9
ADAM_B2 = 0.999
ADAM_EPS = 1e-08
ADAM_WD = 0.01
ADAM_STEP = 10

VMEM_MB = 56

C_U, C_G, C_Q, C_Z0, C_Z1, C_K, C_V = 0, 1024, 2048, 3072, 4096, 5120, 5376


def _cparams(vmem_mb=VMEM_MB):
    return pltpu.CompilerParams(vmem_limit_bytes=vmem_mb << 20)


def _div_tile(n, pref):
    if n <= pref:
        return n
    return max(t for t in range(8, pref + 1, 8) if n % t == 0)


def _perm_cols(w):
    return jnp.concatenate([w[:, :3072], w[:, 3584:5632], w[:, 3072:3584]], axis=1)


def _unperm_cols(w):
    return jnp.concatenate([w[:, :3072], w[:, 5120:5632], w[:, 3072:5120]], axis=1)


def _sigmoid(x):
    return 1.0 / (1.0 + jnp.exp(-x))


def _expm1(x):
    p = x * (1.0 + x * (0.5 + x * (1.0 / 6 + x * (1.0 / 24 + x * (1.0 / 120 + x * (1.0 / 720))))))
    return jnp.where(jnp.abs(x) < 0.3, p, jnp.exp(x) - 1.0)


def _log1p(x):
    u = 1.0 + x
    d = u - 1.0
    return jnp.where(d == 0.0, x, jnp.log(u) * (x / jnp.where(d == 0.0, 1.0, d)))


def _softplus(x):
    return jnp.maximum(x, 0.0) + _log1p(jnp.exp(-jnp.abs(x)))


def _gelu_and_grad(x):
    c = math.sqrt(2.0 / math.pi)
    inner = c * (x + 0.044715 * (x * x * x))
    t = jnp.tanh(inner)
    gelu = 0.5 * x * (1.0 + t)
    dinner = c * (1.0 + 3 * 0.044715 * (x * x))
    dgelu = 0.5 * (1.0 + t) + 0.5 * x * (1.0 - t * t) * dinner
    return gelu, dgelu


def _rms_bwd(dn, xv, g):
    r = lax.rsqrt(jnp.mean(xv * xv, axis=-1, keepdims=True) + EPS)
    xh = xv * r
    dxh = dn * g
    dx = r * (dxh - xh * jnp.mean(dxh * xh, axis=-1, keepdims=True))
    return dx, dn * xh


ANY_SPEC = pl.BlockSpec(memory_space=pl.ANY)


def _comm_out_shape(src, scatter):
    return jax.ShapeDtypeStruct((N_DEV, *(src.shape[1:] if scatter else src.shape)), src.dtype)


def _comm_sems():
    return [pltpu.SemaphoreType.DMA((N_DEV - 1,)), pltpu.SemaphoreType.DMA((N_DEV - 1,)), pltpu.SemaphoreType.DMA]


def _comm_descs(src_ref, out_ref, send_sems, recv_sems, local_sem, scatter):
    x, y, c = lax.axis_index("x"), lax.axis_index("y"), lax.axis_index("c")
    me = 4 * x + 2 * y + c
    descs = [pltpu.make_async_copy(src_ref.at[me] if scatter else src_ref, out_ref.at[me], local_sem)]
    for k in range(1, N_DEV):
        px, py, pc = x ^ (k >> 2), y ^ ((k >> 1) & 1), c ^ (k & 1)
        peer = 4 * px + 2 * py + pc
        descs.append(pltpu.make_async_remote_copy(
            src_ref=src_ref.at[peer] if scatter else src_ref, dst_ref=out_ref.at[me],
            send_sem=send_sems.at[k - 1], recv_sem=recv_sems.at[k - 1],
            device_id=(px, py, pc), device_id_type=pl.DeviceIdType.MESH))
    return descs


def _exchange(comm, name):
    nc = len(comm)

    def body(*refs):
        srcs, outs, sems = refs[:nc], refs[nc:2 * nc], refs[2 * nc:]
        descs = [d for i in range(nc) for d in _comm_descs(srcs[i], outs[i], *sems[3 * i:3 * i + 3], comm[i][1])]
        for d in descs:
            d.start()
        for d in descs:
            d.wait()

    return pl.pallas_call(
        body, name=name, in_specs=[ANY_SPEC] * nc, out_specs=[ANY_SPEC] * nc,
        out_shape=[_comm_out_shape(*c) for c in comm],
        scratch_shapes=[s for _ in comm for s in _comm_sems()],
    )(*[c[0] for c in comm])


def _hosted_call(body, *, name, grid, in_specs, out_specs, out_shape, args, scratch_shapes=(), comm=()):
    nin, nout, nscr, nc = len(in_specs), len(out_specs), len(scratch_shapes), len(comm)

    def wrapped(*refs):
        ins = refs[:nin]
        csrc = refs[nin:nin + nc]
        outs = refs[nin + nc:nin + nc + nout]
        cout = refs[nin + nc + nout:nin + 2 * nc + nout]
        scr = refs[nin + 2 * nc + nout:]
        sems = scr[nscr:]

        def descs():
            return [d for i in range(nc) for d in _comm_descs(csrc[i], cout[i], *sems[3 * i:3 * i + 3], comm[i][1])]

        if nc:
            first = functools.reduce(jnp.logical_and, [pl.program_id(a) == 0 for a in range(len(grid))])

            @pl.when(first)
            def _():
                for d in descs():
                    d.start()

        body(*ins, *outs, *scr[:nscr])

        if nc:
            last = functools.reduce(jnp.logical_and, [pl.program_id(a) == grid[a] - 1 for a in range(len(grid))])

            @pl.when(last)
            def _():
                for d in descs():
                    d.wait()

    res = pl.pallas_call(
        wrapped, name=name, grid=grid,
        in_specs=[*in_specs, *[ANY_SPEC] * nc], out_specs=[*out_specs, *[ANY_SPEC] * nc],
        out_shape=[*out_shape, *[_comm_out_shape(*c) for c in comm]],
        scratch_shapes=[*scratch_shapes, *[s for _ in comm for s in _comm_sems()]],
        compiler_params=_cparams())(*args, *[c[0] for c in comm])
    return res[:nout], res[nout:]


def _norm_matmul(x, g, w, name, tm=1024, tn=1408, comm=()):
    S, dm = x.shape
    n = w.shape[1]
    tm = min(tm, S)

    def body(x_ref, g_ref, w_ref, xn_ref, o_ref):
        @pl.when(pl.program_id(1) == 0)
        def _():
            xv = x_ref[...]
            r = lax.rsqrt(jnp.mean(xv * xv, axis=-1, keepdims=True) + EPS)
            xn_ref[...] = ((xv * r) * g_ref[...]).astype(bf16)

        o_ref[...] = jnp.dot(xn_ref[...], w_ref[...], preferred_element_type=f32)

    return _hosted_call(
        body, name=name, grid=(S // tm, n // tn),
        in_specs=[pl.BlockSpec((tm, dm), lambda i, j: (i, 0)),
                  pl.BlockSpec((1, dm), lambda i, j: (0, 0)),
                  pl.BlockSpec((dm, tn), lambda i, j: (0, j))],
        out_specs=[pl.BlockSpec((tm, dm), lambda i, j: (i, 0)),
                   pl.BlockSpec((tm, tn), lambda i, j: (i, j))],
        out_shape=[jax.ShapeDtypeStruct((S, dm), bf16), jax.ShapeDtypeStruct((S, n), f32)],
        args=(x, g, w), comm=comm)


def _mm_tn(a, b, name, tk, tn, tmc=2048):
    m, ka = a.shape
    n = b.shape[1]
    tmc = min(tmc, m)

    def body(a_ref, b_ref, o_ref):
        @pl.when(pl.program_id(2) == 0)
        def _():
            o_ref[...] = jnp.zeros_like(o_ref)

        o_ref[...] += lax.dot_general(a_ref[...], b_ref[...], (((0,), (0,)), ((), ())),
                                      preferred_element_type=f32)

    return pl.pallas_call(
        body, name=name, grid=(ka // tk, n // tn, m // tmc),
        in_specs=[pl.BlockSpec((tmc, tk), lambda i, j, k: (k, i)),
                  pl.BlockSpec((tmc, tn), lambda i, j, k: (k, j))],
        out_specs=pl.BlockSpec((tk, tn), lambda i, j, k: (i, j)),
        out_shape=jax.ShapeDtypeStruct((ka, n), f32),
        compiler_params=_cparams())(a, b)


def _rows_at(ext, o, tc):
    if o == 0:
        return ext[8:8 + tc]
    return pltpu.roll(ext, (-o) % ext.shape[0], 0)[8:8 + tc]


def _halo_specs(tc, S, width, col):
    per = tc // 8
    last = S // 8 - 1
    return (pl.BlockSpec((tc, width), lambda i: (i, col)),
            pl.BlockSpec((8, width), lambda i: (jnp.maximum(i * per - 1, 0), col)),
            pl.BlockSpec((8, width), lambda i: (jnp.minimum((i + 1) * per, last), col)))


def _extended(cur_ref, prev_ref, next_ref, i, nsteps):
    prev = jnp.where(i > 0, prev_ref[...], 0.0)
    nxt = jnp.where(i < nsteps - 1, next_ref[...], 0.0)
    return jnp.concatenate([prev, cur_ref[...], nxt], axis=0)


def _conv_fwd(proj, cw, cb, tc=512):
    S = proj.shape[0]
    tc = min(tc, S)
    nsteps = S // tc

    def body(cur_ref, prev_ref, next_ref, w_ref, b_ref, o_ref):
        ext = _extended(cur_ref, prev_ref, next_ref, pl.program_id(0), nsteps)
        acc = _rows_at(ext, -2, tc) * w_ref[0:1, :]
        for k in range(1, 4):
            acc = acc + _rows_at(ext, k - 2, tc) * w_ref[k:k + 1, :]
        o_ref[...] = acc + b_ref[...]

    return pl.pallas_call(
        body, name="conv_fwd", grid=(nsteps,),
        in_specs=[*_halo_specs(tc, S, D, 0),
                  pl.BlockSpec((4, D), lambda i: (0, 0)), pl.BlockSpec((1, D), lambda i: (0, 0))],
        out_specs=pl.BlockSpec((tc, D), lambda i: (i, 0)),
        out_shape=jax.ShapeDtypeStruct((S, D), f32),
        compiler_params=_cparams())(proj, proj, proj, cw, cb)


def _conv_bwd(duc_f, duc_b, proj, cw, tc=512):
    S = proj.shape[0]
    tc = min(tc, S)
    nsteps = S // tc

    def body(fc, fp, fn, bc, bp, bn, uc_, up, un, w_ref, du_ref, dw_ref, db_ref):
        i = pl.program_id(0)

        @pl.when(i == 0)
        def _():
            dw_ref[...] = jnp.zeros_like(dw_ref)
            db_ref[...] = jnp.zeros_like(db_ref)

        dext = _extended(fc, fp, fn, i, nsteps) + _extended(bc, bp, bn, i, nsteps)
        uext = _extended(uc_, up, un, i, nsteps)
        d = dext[8:8 + tc]
        acc = _rows_at(dext, 2, tc) * w_ref[0:1, :]
        for k in range(1, 4):
            acc = acc + _rows_at(dext, 2 - k, tc) * w_ref[k:k + 1, :]
        du_ref[...] = acc.astype(bf16)
        wrow = lax.broadcasted_iota(jnp.int32, (4, D), 0)
        for k in range(4):
            dw_ref[...] += jnp.where(wrow == k, jnp.sum(d * _rows_at(uext, k - 2, tc), axis=0, keepdims=True), 0.0)
        db_ref[...] += jnp.sum(d, axis=0, keepdims=True)

    return pl.pallas_call(
        body, name="conv_bwd", grid=(nsteps,),
        in_specs=[*_halo_specs(tc, S, D, 0), *_halo_specs(tc, S, D, 0), *_halo_specs(tc, S, D, 0),
                  pl.BlockSpec((4, D), lambda i: (0, 0))],
        out_specs=[pl.BlockSpec((tc, D), lambda i: (i, 0)),
                   pl.BlockSpec((4, D), lambda i: (0, 0)), pl.BlockSpec((1, D), lambda i: (0, 0))],
        out_shape=[jax.ShapeDtypeStruct((S, D), bf16), jax.ShapeDtypeStruct((4, D), f32),
                   jax.ShapeDtypeStruct((1, D), f32)],
        compiler_params=_cparams())(duc_f, duc_f, duc_f, duc_b, duc_b, duc_b, proj, proj, proj, cw)


def _scan_chunk(a, b, carry, reverse):
    tc = a.shape[0]
    row = lax.broadcasted_iota(jnp.int32, a.shape, 0)
    s = 1
    while s < tc:
        shift = tc - s if reverse else s
        keep = (row < tc - s) if reverse else (row >= s)
        a_sh = pltpu.roll(a, shift, 0)
        b_sh = pltpu.roll(b, shift, 0)
        b = jnp.where(keep, a * b_sh + b, b)
        a = jnp.where(keep, a * a_sh, a)
        s *= 2
    return b + a * carry


def _lru_gates(uc, w, p_ref):
    pre = jnp.dot(uc.astype(bf16), w, preferred_element_type=f32)
    r = _sigmoid(pre[:, :LRU_GW] + p_ref[0, 1:2, :])
    gi = _sigmoid(pre[:, LRU_GW:] + p_ref[0, 2:3, :])
    sp = _softplus(-p_ref[0, 0:1, :])
    log_a = -RGLRU_C * r * sp
    a = jnp.exp(log_a)
    beta = jnp.sqrt(jnp.maximum(-_expm1(2.0 * log_a), 0.0))
    return r, gi, sp, a, beta


def _lru_fwd(uc, wg, lp, reverse):
    S = uc.shape[0]
    tc = LRU_CHUNK
    nc = S // tc
    d = 1 if reverse else 0

    def cidx(c):
        return nc - 1 - c if reverse else c

    def body(uc_ref, w_ref, p_ref, h_ref, carry_ref):
        @pl.when(pl.program_id(1) == 0)
        def _():
            carry_ref[...] = jnp.zeros_like(carry_ref)

        ucv = uc_ref[...]
        _, gi, _, a, beta = _lru_gates(ucv, w_ref[0], p_ref)
        h_ref[...] = _scan_chunk(a, beta * (gi * ucv), carry_ref[...], reverse)
        carry_ref[...] = h_ref[0:1, :] if reverse else h_ref[tc - 1:tc, :]

    return pl.pallas_call(
        body, name="lru_fwd_rev" if reverse else "lru_fwd", grid=(LRU_GROUPS, nc),
        in_specs=[pl.BlockSpec((tc, LRU_GW), lambda g, c: (cidx(c), g)),
                  pl.BlockSpec((1, LRU_GW, 2 * LRU_GW), lambda g, c: (g, 0, d)),
                  pl.BlockSpec((1, 8, LRU_GW), lambda g, c: (d, 0, g))],
        out_specs=pl.BlockSpec((tc, LRU_GW), lambda g, c: (cidx(c), g)),
        out_shape=jax.ShapeDtypeStruct((S, D), f32),
        scratch_shapes=[pltpu.VMEM((1, LRU_GW), f32)],
        compiler_params=_cparams())(uc, wg, lp)


def _lru_bwd(uc, dh, h, wg, wgt, lp, reverse, comm=()):
    S = uc.shape[0]
    tc = LRU_CHUNK
    nc = S // tc
    d = 1 if reverse else 0
    per = tc // 8
    last8 = S // 8 - 1

    def cidx(c):
        return c if reverse else nc - 1 - c

    def halo_idx(c):
        if reverse:
            return jnp.minimum((cidx(c) + 1) * per, last8)
        return jnp.maximum(cidx(c) * per - 1, 0)

    def body(uc_ref, dh_ref, h_ref, halo_ref, w_ref, wt_ref, p_ref, duc_ref, dw_ref, dp_ref, carry_ref, tmp_ref):
        c = pl.program_id(1)
        ci = cidx(c)

        @pl.when(c == 0)
        def _():
            carry_ref[...] = jnp.zeros_like(carry_ref)
            dw_ref[...] = jnp.zeros_like(dw_ref)
            dp_ref[...] = jnp.zeros_like(dp_ref)

        ucv = uc_ref[...]
        ucb = ucv.astype(bf16)
        r, gi, sp, a, beta = _lru_gates(ucv, w_ref[0], p_ref)
        row = lax.broadcasted_iota(jnp.int32, a.shape, 0)
        hv = h_ref[...]
        if reverse:
            alpha = jnp.where(row == 0, 1.0, pltpu.roll(a, 1, 0))
            gsc = _scan_chunk(alpha, dh_ref[...], carry_ref[...], False)
            edge = jnp.where(ci < nc - 1, halo_ref[0:1, :], 0.0)
            h_nb = jnp.where(row == tc - 1, edge, pltpu.roll(hv, tc - 1, 0))
        else:
            alpha = jnp.where(row == tc - 1, 1.0, pltpu.roll(a, tc - 1, 0))
            gsc = _scan_chunk(alpha, dh_ref[...], carry_ref[...], True)
            edge = jnp.where(ci > 0, halo_ref[7:8, :], 0.0)
            h_nb = jnp.where(row == 0, edge, pltpu.roll(hv, 1, 0))
        tmp_ref[...] = a * gsc
        carry_ref[...] = tmp_ref[tc - 1:tc, :] if reverse else tmp_ref[0:1, :]

        da = gsc * h_nb
        iu = gi * ucv
        dbeta = gsc * iu
        dl = da * a - dbeta * (a * a) / beta
        dr = dl * (-RGLRU_C * sp)
        dsp = jnp.sum(dl * (-RGLRU_C * r), axis=0, keepdims=True)
        dgi = gsc * beta * ucv
        dpre_r = dr * r * (1.0 - r)
        dpre_i = dgi * gi * (1.0 - gi)
        dpre = jnp.concatenate([dpre_r, dpre_i], axis=1).astype(bf16)
        duc_ref[...] = gsc * beta * gi + jnp.dot(dpre, wt_ref[0], preferred_element_type=f32)
        dw_ref[0] += lax.dot_general(ucb, dpre, (((0,), (0,)), ((), ())), preferred_element_type=f32)
        dlam = dsp * (-_sigmoid(-p_ref[0, 0:1, :]))
        prow = lax.broadcasted_iota(jnp.int32, (8, LRU_GW), 0)
        dp_ref[...] += (jnp.where(prow == 0, dlam, 0.0)
                        + jnp.where(prow == 1, jnp.sum(dpre_r, axis=0, keepdims=True), 0.0)
                        + jnp.where(prow == 2, jnp.sum(dpre_i, axis=0, keepdims=True), 0.0))

    chunk = pl.BlockSpec((tc, LRU_GW), lambda g, c: (cidx(c), g))
    return _hosted_call(
        body, name="lru_bwd_rev" if reverse else "lru_bwd", grid=(LRU_GROUPS, nc),
        in_specs=[chunk, chunk, chunk,
                  pl.BlockSpec((8, LRU_GW), lambda g, c: (halo_idx(c), g)),
                  pl.BlockSpec((1, LRU_GW, 2 * LRU_GW), lambda g, c: (g, 0, d)),
                  pl.BlockSpec((1, 2 * LRU_GW, LRU_GW), lambda g, c: (g, d, 0)),
                  pl.BlockSpec((1, 8, LRU_GW), lambda g, c: (d, 0, g))],
        out_specs=[chunk,
                   pl.BlockSpec((1, LRU_GW, 2 * LRU_GW), lambda g, c: (g, 0, 0)),
                   pl.BlockSpec((8, LRU_GW), lambda g, c: (0, g))],
        out_shape=[jax.ShapeDtypeStruct((S, D), f32),
                   jax.ShapeDtypeStruct((LRU_GROUPS, LRU_GW, 2 * LRU_GW), f32),
                   jax.ShapeDtypeStruct((8, D), f32)],
        scratch_shapes=[pltpu.VMEM((1, LRU_GW), f32), pltpu.VMEM((tc, LRU_GW), f32)],
        args=(uc, dh, h, h, wg, wgt, lp), comm=comm)


def _attn_mask_parts(n, nb):
    q_loc = lax.broadcasted_iota(jnp.int32, (BLK, 3 * BLK), 0)
    k_loc = lax.broadcasted_iota(jnp.int32, (BLK, 3 * BLK), 1)
    dist = q_loc + BLK - k_loc
    adist = jnp.abs(dist)
    kpos = n * BLK - BLK + k_loc
    valid = (adist <= WINDOW) & (kpos >= 0) & (kpos < nb * BLK)
    return valid, adist.astype(f32)


def _attn_probs(qm, k2, valid, adist, slope, sink):
    s = lax.dot_general(qm, k2, (((1,), (1,)), ((), ())), preferred_element_type=f32)
    s = jnp.where(valid, s + (-slope) * adist, NEG_INF)
    m = jnp.maximum(jnp.max(s, axis=-1, keepdims=True), sink)
    p = jnp.exp(s - m)
    ps = jnp.exp(sink - m)
    denom = jnp.sum(p, axis=-1, keepdims=True) + ps
    return p / denom, ps / denom


def _slope(h):
    return 2.0 ** (-8.0 * (h + 1.0) / N_HEADS)


def _window_specs():
    return [pl.BlockSpec((BLK, 512), lambda n: (n, 0)), pl.BlockSpec((BLK, 512), lambda n: (n + 1, 0)),
            pl.BlockSpec((BLK, 512), lambda n: (n + 2, 0))]


def _window(r0, r1, r2, kv):
    cols = slice(kv * 128, (kv + 1) * 128)
    return jnp.concatenate([r0[:, cols], r1[:, cols], r2[:, cols]], axis=0)


def _attn_fwd(proj, k2, v2, sink, comm=()):
    S = proj.shape[0]
    nb = S // BLK

    def body(q_ref, k0, k1, k2_, v0, v1, v2_, sink_ref, o_ref):
        n = pl.program_id(0)
        valid, adist = _attn_mask_parts(n, nb)
        left = lax.broadcasted_iota(jnp.int32, (BLK, 2 * HEAD_DIM), 1) < HEAD_DIM
        outs = []
        for pair in range(N_HEADS // 2):
            kv = pair // 2
            qp = q_ref[:, pair * 128:(pair + 1) * 128] * (HEAD_DIM ** -0.5)
            kk = _window(k0, k1, k2_, kv)
            vv = _window(v0, v1, v2_, kv)
            halves = []
            for sub in range(2):
                h = 2 * pair + sub
                qm = jnp.where(left if sub == 0 else ~left, qp, 0.0).astype(bf16)
                pn, _ = _attn_probs(qm, kk, valid, adist, _slope(h), sink_ref[0, h])
                halves.append(jnp.dot(pn.astype(bf16), vv, preferred_element_type=f32))
            outs.append(jnp.where(left, halves[0], halves[1]))
        o_ref[...] = jnp.concatenate(outs, axis=1)

    return _hosted_call(
        body, name="attn_fwd", grid=(nb,),
        in_specs=[pl.BlockSpec((BLK, D), lambda n: (n, C_Q // D)), *_window_specs(), *_window_specs(),
                  pl.BlockSpec(memory_space=pltpu.SMEM)],
        out_specs=[pl.BlockSpec((BLK, D), lambda n: (n, 0))],
        out_shape=[jax.ShapeDtypeStruct((S, D), f32)],
        args=(proj, k2, k2, k2, v2, v2, v2, sink), comm=comm)


def _attn_bwd(proj, k2, v2, sink, dyb, comm=()):
    S = proj.shape[0]
    nb = S // BLK

    def body(q_ref, k0, k1, k2_, v0, v1, v2_, sink_ref, do_ref, dq_ref, dk_ref, dv_ref, ds_ref):
        n = pl.program_id(0)

        @pl.when(n == 0)
        def _():
            dk_ref[...] = jnp.zeros_like(dk_ref)
            dv_ref[...] = jnp.zeros_like(dv_ref)
            ds_ref[...] = jnp.zeros_like(ds_ref)

        valid, adist = _attn_mask_parts(n, nb)
        left = lax.broadcasted_iota(jnp.int32, (BLK, 2 * HEAD_DIM), 1) < HEAD_DIM
        lane = lax.broadcasted_iota(jnp.int32, (1, 128), 1)
        start = pl.multiple_of(n * BLK, BLK)
        left3 = lax.broadcasted_iota(jnp.int32, (3 * BLK, 2 * HEAD_DIM), 1) < HEAD_DIM
        dqs = []
        dks, dvs = [], []
        dsink = jnp.zeros((1, 128), f32)
        for kv in range(N_KV):
            kk = _window(k0, k1, k2_, kv)
            vv = _window(v0, v1, v2_, kv)
            ds_rows, pn_rows, q_rows, do_rows = [], [], [], []
            for pp in range(2):
                pair = 2 * kv + pp
                qp = q_ref[:, pair * 128:(pair + 1) * 128] * (HEAD_DIM ** -0.5)
                dop = do_ref[:, pair * 128:(pair + 1) * 128]
                dq_halves = []
                for sub in range(2):
                    h = 2 * pair + sub
                    sel = left if sub == 0 else ~left
                    qm = jnp.where(sel, qp, 0.0).astype(bf16)
                    dom = jnp.where(sel, dop, jnp.zeros_like(dop))
                    pn, psn = _attn_probs(qm, kk, valid, adist, _slope(h), sink_ref[0, h])
                    dp = lax.dot_general(dom, vv, (((1,), (1,)), ((), ())), preferred_element_type=f32)
                    delta = jnp.sum(pn * dp, axis=-1, keepdims=True)
                    dsc = (pn * (dp - delta)).astype(bf16)
                    dsink = dsink + jnp.where(lane == h, -jnp.sum(delta * psn), 0.0)
                    dq_halves.append(jnp.dot(dsc, kk, preferred_element_type=f32))
                    ds_rows.append(dsc)
                    pn_rows.append(pn.astype(bf16))
                    q_rows.append(qm)
                    do_rows.append(dom)
                dqs.append(jnp.where(left, dq_halves[0], dq_halves[1]) * (HEAD_DIM ** -0.5))
            dsa = jnp.concatenate(ds_rows, axis=0)
            pna = jnp.concatenate(pn_rows, axis=0)
            qa = jnp.concatenate(q_rows, axis=0)
            doa = jnp.concatenate(do_rows, axis=0)
            dk = lax.dot_general(dsa, qa, (((0,), (0,)), ((), ())), preferred_element_type=f32)
            dv = lax.dot_general(pna, doa, (((0,), (0,)), ((), ())), preferred_element_type=f32)
            dks.append(dk + pltpu.roll(dk, HEAD_DIM, 1))
            dvs.append(dv + pltpu.roll(dv, HEAD_DIM, 1))
        for jp in range(N_KV // 2):
            cols = slice(jp * 128, (jp + 1) * 128)
            dk_ref[pl.ds(start, 3 * BLK), cols] += jnp.where(left3, dks[2 * jp], dks[2 * jp + 1])
            dv_ref[pl.ds(start, 3 * BLK), cols] += jnp.where(left3, dvs[2 * jp], dvs[2 * jp + 1])
        dq_ref[...] = jnp.concatenate(dqs, axis=1).astype(bf16)
        ds_ref[...] += dsink

    whole = pl.BlockSpec((S + 2 * BLK, N_KV * HEAD_DIM), lambda n: (0, 0))
    return _hosted_call(
        body, name="attn_bwd", grid=(nb,),
        in_specs=[pl.BlockSpec((BLK, D), lambda n: (n, C_Q // D)), *_window_specs(), *_window_specs(),
                  pl.BlockSpec(memory_space=pltpu.SMEM),
                  pl.BlockSpec((BLK, D), lambda n: (n, 0))],
        out_specs=[pl.BlockSpec((BLK, D), lambda n: (n, 0)), whole, whole,
                   pl.BlockSpec((1, 128), lambda n: (0, 0))],
        out_shape=[jax.ShapeDtypeStruct((S, D), bf16),
                   jax.ShapeDtypeStruct((S + 2 * BLK, N_KV * HEAD_DIM), f32),
                   jax.ShapeDtypeStruct((S + 2 * BLK, N_KV * HEAD_DIM), f32),
                   jax.ShapeDtypeStruct((1, 128), f32)],
        args=(proj, k2, k2, k2, v2, v2, v2, sink, dyb), comm=comm)


def _merge_parts(hf, hb, g, z0, z1, yb, bg):
    g0 = _sigmoid(z0 + bg[:, :D])
    g1 = _sigmoid(z1 + bg[:, D:])
    gelu, dgelu = _gelu_and_grad(g)
    hs = hf + hb
    ya = hs * gelu
    return g0, g1, gelu, dgelu, hs, ya


def _merge_outproj(x, hf, hb, proj, yb, bg, w_out, tm=512):
    S = x.shape[0]
    tm = min(tm, S)

    def body(x_ref, hf_ref, hb_ref, g_ref, z0_ref, z1_ref, yb_ref, bg_ref, w_ref, mg_ref, x1_ref):
        ybv = yb_ref[...]
        g0, g1, _, _, _, ya = _merge_parts(hf_ref[...], hb_ref[...], g_ref[...], z0_ref[...], z1_ref[...],
                                           ybv, bg_ref[...])
        mg = (g0 * ya + g1 * ybv).astype(bf16)
        mg_ref[...] = mg
        x1_ref[...] = x_ref[...] + jnp.dot(mg, w_ref[...], preferred_element_type=f32)

    row = pl.BlockSpec((tm, D), lambda i: (i, 0))
    return pl.pallas_call(
        body, name="merge_outproj", grid=(S // tm,),
        in_specs=[row, row, row,
                  pl.BlockSpec((tm, D), lambda i: (i, C_G // D)),
                  pl.BlockSpec((tm, D), lambda i: (i, C_Z0 // D)),
                  pl.BlockSpec((tm, D), lambda i: (i, C_Z1 // D)),
                  row, pl.BlockSpec((1, 2 * D), lambda i: (0, 0)), pl.BlockSpec((D, D), lambda i: (0, 0))],
        out_specs=[row, row],
        out_shape=[jax.ShapeDtypeStruct((S, D), bf16), jax.ShapeDtypeStruct((S, D), f32)],
        compiler_params=_cparams())(x, hf, hb, proj, proj, proj, yb, bg, w_out)


def _ffn_out_loss(gu, x1, w_fo, g3, tgt, tm=256):
    S = x1.shape[0]
    tm = min(tm, S)

    def body(gt_ref, up_ref, x1_ref, w_ref, g_ref, t_ref, ff_ref, dx_ref, dxb_ref, loss_ref, dg_ref):
        @pl.when(pl.program_id(0) == 0)
        def _():
            loss_ref[...] = jnp.zeros_like(loss_ref)
            dg_ref[...] = jnp.zeros_like(dg_ref)

        gt = gt_ref[...]
        ff = ((gt * _sigmoid(gt)) * up_ref[...]).astype(bf16)
        ff_ref[...] = ff
        x2 = x1_ref[...] + jnp.dot(ff, w_ref[...], preferred_element_type=f32)
        gv = g_ref[...]
        r = lax.rsqrt(jnp.mean(x2 * x2, axis=-1, keepdims=True) + EPS)
        xh = x2 * r
        diff = xh * gv - t_ref[...]
        loss_ref[...] += (0.5 / D) * jnp.sum(diff * diff)
        dy = diff * (1.0 / D)
        dg_ref[...] += jnp.sum(dy * xh, axis=0, keepdims=True)
        dxh = dy * gv
        dx = r * (dxh - xh * jnp.mean(dxh * xh, axis=-1, keepdims=True))
        dx_ref[...] = dx
        dxb_ref[...] = dx.astype(bf16)

    row = pl.BlockSpec((tm, D), lambda i: (i, 0))
    vec = pl.BlockSpec((1, D), lambda i: (0, 0))
    return pl.pallas_call(
        body, name="ffn_out_loss", grid=(S // tm,),
        in_specs=[pl.BlockSpec((tm, D_FF), lambda i: (i, 0)), pl.BlockSpec((tm, D_FF), lambda i: (i, 1)),
                  row, pl.BlockSpec((D_FF, D), lambda i: (0, 0)), vec, row],
        out_specs=[pl.BlockSpec((tm, D_FF), lambda i: (i, 0)), row, row,
                   pl.BlockSpec((1, 128), lambda i: (0, 0)), vec],
        out_shape=[jax.ShapeDtypeStruct((S, D_FF), bf16), jax.ShapeDtypeStruct((S, D), f32),
                   jax.ShapeDtypeStruct((S, D), bf16), jax.ShapeDtypeStruct((1, 128), f32),
                   jax.ShapeDtypeStruct((1, D), f32)],
        compiler_params=_cparams())(gu, gu, x1, w_fo, g3, tgt)


def _ffn_bwd1(dx2b, w_fot, gu, tm=256, comm=()):
    S = dx2b.shape[0]
    tm = min(tm, S)

    def body(dx_ref, w_ref, gt_ref, up_ref, dgt_ref, dup_ref):
        dff = jnp.dot(dx_ref[...], w_ref[...], preferred_element_type=f32)
        gt = gt_ref[...]
        sg = _sigmoid(gt)
        dup_ref[...] = (dff * (gt * sg)).astype(bf16)
        dgt_ref[...] = ((dff * up_ref[...]) * (sg * (1.0 + gt * (1.0 - sg)))).astype(bf16)

    wide = pl.BlockSpec((tm, D_FF), lambda i: (i, 0))
    return _hosted_call(
        body, name="ffn_bwd1", grid=(S // tm,),
        in_specs=[pl.BlockSpec((tm, D), lambda i: (i, 0)), pl.BlockSpec((D, D_FF), lambda i: (0, 0)),
                  wide, pl.BlockSpec((tm, D_FF), lambda i: (i, 1))],
        out_specs=[wide, wide],
        out_shape=[jax.ShapeDtypeStruct((S, D_FF), bf16), jax.ShapeDtypeStruct((S, D_FF), bf16)],
        args=(dx2b, w_fot, gu, gu), comm=comm)


def _proj_bwd(pieces, wts, xres, g, dres, name, tm=256, comm=()):
    S = xres.shape[0]
    tm = min(tm, S)
    np_ = len(pieces)

    def body(*refs):
        p_refs = refs[:np_]
        w_refs = refs[np_:2 * np_]
        x_ref, g_ref, dres_ref, dx_ref, dxb_ref, dg_ref = refs[2 * np_:]

        @pl.when(pl.program_id(0) == 0)
        def _():
            dg_ref[...] = jnp.zeros_like(dg_ref)

        dn = jnp.dot(p_refs[0][...], w_refs[0][...], preferred_element_type=f32)
        for pr, wr in zip(p_refs[1:], w_refs[1:]):
            dn = dn + jnp.dot(pr[...], wr[...], preferred_element_type=f32)
        dxn, dgc = _rms_bwd(dn, x_ref[...], g_ref[...])
        dx = dres_ref[...] + dxn
        dx_ref[...] = dx
        dxb_ref[...] = dx.astype(bf16)
        dg_ref[...] += jnp.sum(dgc, axis=0, keepdims=True)

    row = pl.BlockSpec((tm, D), lambda i: (i, 0))
    vec = pl.BlockSpec((1, D), lambda i: (0, 0))
    return _hosted_call(
        body, name=name, grid=(S // tm,),
        in_specs=[*[pl.BlockSpec((tm, p.shape[1]), lambda i: (i, 0)) for p in pieces],
                  *[pl.BlockSpec(w.shape, lambda i: (0, 0)) for w in wts],
                  row, vec, row],
        out_specs=[row, row, vec],
        out_shape=[jax.ShapeDtypeStruct((S, D), f32), jax.ShapeDtypeStruct((S, D), bf16),
                   jax.ShapeDtypeStruct((1, D), f32)],
        args=(*pieces, *wts, xres, g, dres), comm=comm)


def _outproj_bwd(dx1b, w_outt, hf, hb, proj, yb, bg, tm=512):
    S = dx1b.shape[0]
    tm = min(tm, S)

    def body(dx_ref, w_ref, hf_ref, hb_ref, g_ref, z0_ref, z1_ref, yb_ref, bg_ref,
             dh_ref, dg_ref, dz_ref, dyb_ref, dbg_ref):
        @pl.when(pl.program_id(0) == 0)
        def _():
            dbg_ref[...] = jnp.zeros_like(dbg_ref)

        dm = jnp.dot(dx_ref[...], w_ref[...], preferred_element_type=f32)
        ybv = yb_ref[...]
        g0, g1, gelu, dgelu, hs, ya = _merge_parts(hf_ref[...], hb_ref[...], g_ref[...], z0_ref[...],
                                                   z1_ref[...], ybv, bg_ref[...])
        dya = dm * g0
        dh_ref[...] = dya * gelu
        dg_ref[...] = (dya * hs * dgelu).astype(bf16)
        dyb_ref[...] = (dm * g1).astype(bf16)
        dz0 = (dm * ya) * (g0 * (1.0 - g0))
        dz1 = (dm * ybv) * (g1 * (1.0 - g1))
        dz = jnp.concatenate([dz0, dz1], axis=1)
        dz_ref[...] = dz.astype(bf16)
        dbg_ref[...] += jnp.sum(dz, axis=0, keepdims=True)

    row = pl.BlockSpec((tm, D), lambda i: (i, 0))
    return pl.pallas_call(
        body, name="outproj_bwd", grid=(S // tm,),
        in_specs=[row, pl.BlockSpec((D, D), lambda i: (0, 0)), row, row,
                  pl.BlockSpec((tm, D), lambda i: (i, C_G // D)),
                  pl.BlockSpec((tm, D), lambda i: (i, C_Z0 // D)),
                  pl.BlockSpec((tm, D), lambda i: (i, C_Z1 // D)),
                  row, pl.BlockSpec((1, 2 * D), lambda i: (0, 0))],
        out_specs=[row, row, pl.BlockSpec((tm, 2 * D), lambda i: (i, 0)), row,
                   pl.BlockSpec((1, 2 * D), lambda i: (0, 0))],
        out_shape=[jax.ShapeDtypeStruct((S, D), f32), jax.ShapeDtypeStruct((S, D), bf16),
                   jax.ShapeDtypeStruct((S, 2 * D), bf16), jax.ShapeDtypeStruct((S, D), bf16),
                   jax.ShapeDtypeStruct((1, 2 * D), f32)],
        compiler_params=_cparams())(dx1b, w_outt, hf, hb, proj, proj, proj, yb, bg)


def _block_diag_groups(w):
    w4 = w.reshape(LRU_GROUPS, 4, LRU_BLOCK, LRU_BLOCK)
    eye = jnp.eye(4, dtype=w.dtype)
    return jnp.einsum("ghij,hk->ghikj", w4, eye).reshape(LRU_GROUPS, LRU_GW, LRU_GW)


def _diag_blocks(dw):
    d5 = dw.reshape(LRU_GROUPS, 4, LRU_BLOCK, 4, LRU_BLOCK)
    return jnp.stack([d5[:, h, :, h, :] for h in range(4)], axis=1).reshape(LRU_HEADS, LRU_BLOCK, LRU_BLOCK)


def _dup_heads(t, S):
    t4 = t.astype(bf16).reshape(S, N_KV, HEAD_DIM)
    t2 = jnp.concatenate([t4, t4], axis=-1).reshape(S, 2 * N_KV * HEAD_DIM)
    return jnp.pad(t2, ((BLK, BLK), (0, 0)))


def _local_step(x, tgt, small, env, before=lambda name: (), after=lambda name, got: None):
    S = x.shape[0]
    g1, g2, g3 = small["norm_mix_g"], small["norm_ffn_g"], small["norm_final_g"]
    bg, cw, cb = small["b_gate"], small["conv_w"], small["conv_b"]
    sink = small["attn_sink"]

    wg = jnp.concatenate([_block_diag_groups(small["lru_wa"][0]), _block_diag_groups(small["lru_wx"][0]),
                          _block_diag_groups(small["lru_wa"][1]), _block_diag_groups(small["lru_wx"][1])],
                         axis=2).astype(bf16)
    wgt = jnp.swapaxes(wg, 1, 2)
    zeros5 = jnp.zeros((5, D), f32)
    lp = jnp.stack([jnp.concatenate([small["lru_lambda"][d:d + 1], small["lru_ba"][d:d + 1],
                                     small["lru_bx"][d:d + 1], zeros5], axis=0) for d in range(2)])

    def hosted(name, fn, *args, **kw):
        outs, got = fn(*args, comm=tuple(before(name)), **kw)
        after(name, got)
        return outs

    xn, proj = hosted("norm_inproj", _norm_matmul, x, g1, env["w_in_p"], "norm_inproj")
    uc = _conv_fwd(proj, cw, cb)
    hf = _lru_fwd(uc, wg, lp, False)
    hb = _lru_fwd(uc, wg, lp, True)
    k2 = _dup_heads(proj[:, C_K:C_V], S)
    v2 = _dup_heads(proj[:, C_V:], S)
    (yb,) = hosted("attn_fwd", _attn_fwd, proj, k2, v2, sink)
    merged, x1 = _merge_outproj(x, hf, hb, proj, yb, bg, env["w_out"])
    (xn2, gu), _ = _norm_matmul(x1, g2, env["w_fi"], "norm_ffn_in")
    ff, dx2, dx2b, loss, dg3 = _ffn_out_loss(gu, x1, env["w_fo"], g3, tgt)

    env["dw_fo"] = _mm_tn(ff, dx2b, "dw_ffn_out", tk=1408, tn=1024)
    dgt, dup = hosted("ffn_bwd1", _ffn_bwd1, dx2b, env["w_fo"].T, gu)
    env["dw_fi"] = jnp.concatenate([_mm_tn(xn2, dgt, "dw_ffn_in_gate", tk=1024, tn=1408),
                                    _mm_tn(xn2, dup, "dw_ffn_in_up", tk=1024, tn=1408)], axis=1)
    w_fit = env["w_fi"].T
    (dx1, dx1b, dg2), _ = _proj_bwd([dgt, dup], [w_fit[:D_FF], w_fit[D_FF:]], x1, g2, dx2, "ffn_in_bwd")
    env["dw_out"] = _mm_tn(merged, dx1b, "dw_out", tk=1024, tn=1024)
    dh, dgl, dz, dyb, dbg = _outproj_bwd(dx1b, env["w_out"].T, hf, hb, proj, yb, bg)
    dq, dk2, dv2, dsink = hosted("attn_bwd", _attn_bwd, proj, k2, v2, sink, dyb)
    dkv = jnp.concatenate([dk2[BLK:BLK + S], dv2[BLK:BLK + S]], axis=1).astype(bf16)
    duc_f, dwg_f, dp_f = hosted("lru_bwd", _lru_bwd, uc, dh, hf, wg, wgt, lp, False)
    (duc_b, dwg_b, dp_b), _ = _lru_bwd(uc, dh, hb, wg, wgt, lp, True)
    du, dcw, dcb = _conv_bwd(duc_f, duc_b, proj, cw)
    pieces = [du, dgl, dq, dz, dkv]
    bounds = [0, 1024, 2048, 3072, 5120, 5632]
    env["dw_in"] = _unperm_cols(jnp.concatenate(
        [_mm_tn(xn, p, "dw_in_%d" % i, tk=1024, tn=min(p.shape[1], 1024)) for i, p in enumerate(pieces)], axis=1))
    w_int = env["w_in_p"].T
    dx, _, dg1 = hosted("inproj_bwd", _proj_bwd, pieces, [w_int[bounds[i]:bounds[i + 1]] for i in range(5)],
                        x, g1, dx1, "inproj_bwd")

    grads = {
        "norm_mix_g": dg1, "b_gate": dbg, "conv_w": dcw, "conv_b": dcb,
        "lru_lambda": jnp.concatenate([dp_f[0:1], dp_b[0:1]], axis=0),
        "lru_wa": jnp.stack([_diag_blocks(dwg_f[:, :, :LRU_GW]), _diag_blocks(dwg_b[:, :, :LRU_GW])]),
        "lru_ba": jnp.concatenate([dp_f[1:2], dp_b[1:2]], axis=0),
        "lru_wx": jnp.stack([_diag_blocks(dwg_f[:, :, LRU_GW:]), _diag_blocks(dwg_b[:, :, LRU_GW:])]),
        "lru_bx": jnp.concatenate([dp_f[2:3], dp_b[2:3]], axis=0),
        "attn_sink": dsink[:, :N_HEADS], "norm_ffn_g": dg2, "norm_final_g": dg3,
    }
    return loss, dx, grads


def _adamw(gparts, w, m, v, name, tr=256):
    n, rows, cols = gparts.shape
    tr = _div_tile(rows, tr)
    c1 = 1.0 - ADAM_B1 ** ADAM_STEP
    c2 = 1.0 - ADAM_B2 ** ADAM_STEP

    def body(g_ref, w_ref, m_ref, v_ref, go_ref, d_ref, mo_ref, vo_ref):
        g = g_ref[0]
        for j in range(1, n):
            g = g + g_ref[j]
        mn = ADAM_B1 * m_ref[...] + (1.0 - ADAM_B1) * g
        vn = ADAM_B2 * v_ref[...] + (1.0 - ADAM_B2) * (g * g)
        m_hat = mn / c1
        v_hat = vn / c2
        go_ref[...] = g
        d_ref[...] = -ADAM_LR * (m_hat / (jnp.sqrt(v_hat) + ADAM_EPS) + ADAM_WD * w_ref[...])
        mo_ref[...] = mn
        vo_ref[...] = vn

    blk = pl.BlockSpec((tr, cols), lambda i: (i, 0))
    shp = jax.ShapeDtypeStruct((rows, cols), f32)
    return pl.pallas_call(
        body, name=name, grid=(rows // tr,),
        in_specs=[pl.BlockSpec((n, tr, cols), lambda i: (0, i, 0)), blk, blk, blk],
        out_specs=[blk, blk, blk, blk], out_shape=[shp, shp, shp, shp],
        compiler_params=_cparams())(gparts, w, m, v)


def _sum_parts(parts, name):
    n, rows, cols = parts.shape

    def body(p_ref, o_ref):
        acc = p_ref[0]
        for j in range(1, n):
            acc = acc + p_ref[j]
        o_ref[...] = acc

    return pl.pallas_call(
        body, name=name, out_shape=jax.ShapeDtypeStruct((rows, cols), f32),
        compiler_params=_cparams())(parts)


def _pack_rows(arrs):
    rows, spans, at = [], [], 0
    for a in arrs:
        flat = a.reshape(-1)
        nr = -(-flat.shape[0] // 1024)
        rows.append(jnp.pad(flat, (0, nr * 1024 - flat.shape[0])).reshape(nr, 1024))
        spans.append((at, nr))
        at += nr
    pad = (-at) % 8
    if pad:
        rows.append(jnp.zeros((pad, 1024), f32))
    return jnp.concatenate(rows, axis=0), spans


def _unpack_rows(packed, spans, shapes):
    out = []
    for (at, nr), shp in zip(spans, shapes):
        n = math.prod(shp)
        out.append(packed[at:at + nr].reshape(-1)[:n].reshape(shp))
    return out


BIG = ("w_in", "w_out", "w_ffn_in", "w_ffn_out")
SMALL_REPL = ("norm_mix_g", "b_gate", "conv_b", "lru_wa", "lru_wx", "attn_sink", "norm_ffn_g", "norm_final_g")
SMALL_SHARD = ("conv_w", "lru_lambda", "lru_ba", "lru_bx")
ORDER = ("norm_mix_g", "w_in", "b_gate", "conv_w", "conv_b", "lru_lambda", "lru_wa", "lru_ba", "lru_wx",
         "lru_bx", "attn_sink", "w_out", "norm_ffn_g", "w_ffn_in", "w_ffn_out", "norm_final_g")


def kernel(x, norm_mix_g, w_in, b_gate, conv_w, conv_b, lru_lambda, lru_wa, lru_ba, lru_wx, lru_bx, attn_sink, w_out, norm_ffn_g, w_ffn_in, w_ffn_out, norm_final_g, loss_target, m_norm_mix_g, m_w_in, m_b_gate, m_conv_w, m_conv_b, m_lru_lambda, m_lru_wa, m_lru_ba, m_lru_wx, m_lru_bx, m_attn_sink, m_w_out, m_norm_ffn_g, m_w_ffn_in, m_w_ffn_out, m_norm_final_g, v_norm_mix_g, v_w_in, v_b_gate, v_conv_w, v_conv_b, v_lru_lambda, v_lru_wa, v_lru_ba, v_lru_wx, v_lru_bx, v_attn_sink, v_w_out, v_norm_ffn_g, v_w_ffn_in, v_w_ffn_out, v_norm_final_g):
    w = dict(norm_mix_g=norm_mix_g, w_in=w_in, b_gate=b_gate, conv_w=conv_w, conv_b=conv_b, lru_lambda=lru_lambda,
             lru_wa=lru_wa, lru_ba=lru_ba, lru_wx=lru_wx, lru_bx=lru_bx, attn_sink=attn_sink, w_out=w_out,
             norm_ffn_g=norm_ffn_g, w_ffn_in=w_ffn_in, w_ffn_out=w_ffn_out, norm_final_g=norm_final_g)
    m = dict(norm_mix_g=m_norm_mix_g, w_in=m_w_in, b_gate=m_b_gate, conv_w=m_conv_w, conv_b=m_conv_b,
             lru_lambda=m_lru_lambda, lru_wa=m_lru_wa, lru_ba=m_lru_ba, lru_wx=m_lru_wx, lru_bx=m_lru_bx,
             attn_sink=m_attn_sink, w_out=m_w_out, norm_ffn_g=m_norm_ffn_g, w_ffn_in=m_w_ffn_in,
             w_ffn_out=m_w_ffn_out, norm_final_g=m_norm_final_g)
    v = dict(norm_mix_g=v_norm_mix_g, w_in=v_w_in, b_gate=v_b_gate, conv_w=v_conv_w, conv_b=v_conv_b,
             lru_lambda=v_lru_lambda, lru_wa=v_lru_wa, lru_ba=v_lru_ba, lru_wx=v_lru_wx, lru_bx=v_lru_bx,
             attn_sink=v_attn_sink, w_out=v_w_out, norm_ffn_g=v_norm_ffn_g, w_ffn_in=v_w_ffn_in,
             w_ffn_out=v_w_ffn_out, norm_final_g=v_norm_final_g)
    me = 4 * lax.axis_index("x") + 2 * lax.axis_index("y") + lax.axis_index("c")

    def cols_full(got):
        return jnp.swapaxes(got, 0, 1).reshape(got.shape[1], -1)

    def cols_parts(g):
        return jnp.swapaxes(g.reshape(g.shape[0], N_DEV, -1), 0, 1)

    def rows_parts(g):
        return g.reshape(N_DEV, -1, g.shape[1])

    shard_rows = jnp.concatenate([w[n][0] for n in SMALL_SHARD], axis=0)
    got_w_in, got_rows = _exchange([(w_in[0].astype(bf16), False), (shard_rows, False)], "gather_w_in")
    full_rows = cols_full(got_rows)
    small = {n: w[n] for n in ("norm_mix_g", "b_gate", "conv_b", "attn_sink", "norm_ffn_g")}
    small["lru_wa"], small["lru_wx"] = lru_wa[0], lru_wx[0]
    small["norm_final_g"] = norm_final_g.reshape(1, D)
    small["conv_w"], small["lru_lambda"] = full_rows[0:4], full_rows[4:6]
    small["lru_ba"], small["lru_bx"] = full_rows[6:8], full_rows[8:10]

    env = {"w_in_p": _perm_cols(cols_full(got_w_in))}
    recv = {}

    def before(name):
        if name == "norm_inproj":
            return [(w_out[0].astype(bf16), False), (w_ffn_out[0].astype(bf16), False)]
        if name == "attn_fwd":
            return [(w_ffn_in[0].astype(bf16), False)]
        if name == "ffn_bwd1":
            return [(rows_parts(env["dw_fo"]), True)]
        if name == "attn_bwd":
            return [(cols_parts(env["dw_fi"]), True)]
        if name == "lru_bwd":
            return [(rows_parts(env["dw_out"]), True)]
        if name == "inproj_bwd":
            return [(cols_parts(env["dw_in"]), True)]
        return []

    def after(name, got):
        if name == "norm_inproj":
            env["w_out"], env["w_fo"] = got[0].reshape(D, D), got[1].reshape(D_FF, D)
        elif name == "attn_fwd":
            env["w_fi"] = cols_full(got[0])
        elif name == "ffn_bwd1":
            recv["w_ffn_out"] = got[0]
        elif name == "attn_bwd":
            recv["w_ffn_in"] = got[0]
        elif name == "lru_bwd":
            recv["w_out"] = got[0]
        elif name == "inproj_bwd":
            recv["w_in"] = got[0]

    loss_part, grad_x, grads = _local_step(x[0], loss_target[0], small, env, before, after)

    outs = {}
    for name in BIG:
        shard_shape = recv[name].shape[1:]
        r2 = lambda a: a.reshape(shard_shape)
        res = _adamw(recv[name], r2(w[name]), r2(m[name]), r2(v[name]), "adamw_" + name)
        outs[name] = [t.reshape(w[name].shape) for t in res]

    small_names = SMALL_REPL + SMALL_SHARD
    packed, spans = _pack_rows([loss_part[:, :1]] + [grads[n] for n in small_names])
    total = _sum_parts(_exchange([(packed, False)], "gather_small_grads")[0], "sum_small_grads")
    shapes = [(1, 1)] + [grads[n].shape for n in small_names]
    summed = dict(zip(("loss",) + small_names, _unpack_rows(total, spans, shapes)))
    loss = summed["loss"].reshape(())
    gsm = {n: summed[n].reshape(w[n].shape) for n in SMALL_REPL}
    for n in SMALL_SHARD:
        full = summed[n]
        gsm[n] = lax.dynamic_slice_in_dim(full, me * 128, 128, axis=1).reshape(w[n].shape)
    pk = lambda dct: _pack_rows([dct[n] for n in small_names])[0]
    gp, sp = _pack_rows([gsm[n] for n in small_names])
    res = _adamw(gp[None], pk(w), pk(m), pk(v), "adamw_small")
    sshapes = [w[n].shape for n in small_names]
    for idx, t in enumerate(res):
        for n, a in zip(small_names, _unpack_rows(t, sp, sshapes)):
            outs.setdefault(n, [None] * 4)[idx] = a

    result = [loss, grad_x[None]]
    for idx in range(4):
        result += [outs[n][idx] for n in ORDER]
    return tuple(result)
```

```python
import functools
import math

import jax
import jax.numpy as jnp
from jax import lax
from jax.experimental import pallas as pl
from jax.experimental.pallas import tpu as pltpu

f32 = jnp.float32
bf16 = jnp.bfloat16

D = 1024
D_FF = 2816
IN_W = 5632
N_HEADS = 16
N_KV = 4
HEAD_DIM = 64
WINDOW = 128
BLK = 128
LRU_HEADS = 16
LRU_BLOCK = 64
LRU_GROUPS = 4
LRU_GW = 256
LRU_CHUNK = 128
LRU_ROWS = 512
RGLRU_C = 8.0
EPS = 1e-6
NEG_INF = -1e30
N_DEV = 8

ADAM_LR = 0.001
ADAM_B1 = 0.9
ADAM_B2 = 0.999
ADAM_EPS = 1e-08
ADAM_WD = 0.01
ADAM_STEP = 10

VMEM_MB = 56

C_U, C_G, C_Q, C_Z0, C_Z1, C_K, C_V = 0, 1024, 2048, 3072, 4096, 5120, 5376


def _cparams(vmem_mb=VMEM_MB):
    return pltpu.CompilerParams(vmem_limit_bytes=vmem_mb << 20)


def _div_tile(n, pref):
    if n <= pref:
        return n
    return max(t for t in range(8, pref + 1, 8) if n % t == 0)


def _perm_cols(w):
    return jnp.concatenate([w[:, :3072], w[:, 3584:5632], w[:, 3072:3584]], axis=1)


def _unperm_cols(w):
    return jnp.concatenate([w[:, :3072], w[:, 5120:5632], w[:, 3072:5120]], axis=1)


def _sigmoid(x):
    return 1.0 / (1.0 + jnp.exp(-x))


def _expm1(x):
    p = x * (1.0 + x * (0.5 + x * (1.0 / 6 + x * (1.0 / 24 + x * (1.0 / 120 + x * (1.0 / 720))))))
    return jnp.where(jnp.abs(x) < 0.3, p, jnp.exp(x) - 1.0)


def _log1p(x):
    u = 1.0 + x
    d = u - 1.0
    return jnp.where(d == 0.0, x, jnp.log(u) * (x / jnp.where(d == 0.0, 1.0, d)))


def _softplus(x):
    return jnp.maximum(x, 0.0) + _log1p(jnp.exp(-jnp.abs(x)))


def _gelu_and_grad(x):
    c = math.sqrt(2.0 / math.pi)
    inner = c * (x + 0.044715 * (x * x * x))
    t = jnp.tanh(inner)
    gelu = 0.5 * x * (1.0 + t)
    dinner = c * (1.0 + 3 * 0.044715 * (x * x))
    dgelu = 0.5 * (1.0 + t) + 0.5 * x * (1.0 - t * t) * dinner
    return gelu, dgelu


def _rms_bwd(dn, xv, g):
    r = lax.rsqrt(jnp.mean(xv * xv, axis=-1, keepdims=True) + EPS)
    xh = xv * r
    dxh = dn * g
    dx = r * (dxh - xh * jnp.mean(dxh * xh, axis=-1, keepdims=True))
    return dx, dn * xh


ANY_SPEC = pl.BlockSpec(memory_space=pl.ANY)


def _comm_out_shape(src, scatter):
    return jax.ShapeDtypeStruct((N_DEV, *(src.shape[1:] if scatter else src.shape)), src.dtype)


def _comm_sems():
    return [pltpu.SemaphoreType.DMA((N_DEV - 1,)), pltpu.SemaphoreType.DMA((N_DEV - 1,)), pltpu.SemaphoreType.DMA]


def _comm_descs(src_ref, out_ref, send_sems, recv_sems, local_sem, scatter):
    x, y, c = lax.axis_index("x"), lax.axis_index("y"), lax.axis_index("c")
    me = 4 * x + 2 * y + c
    descs = [pltpu.make_async_copy(src_ref.at[me] if scatter else src_ref, out_ref.at[me], local_sem)]
    for k in range(1, N_DEV):
        px, py, pc = x ^ (k >> 2), y ^ ((k >> 1) & 1), c ^ (k & 1)
        peer = 4 * px + 2 * py + pc
        descs.append(pltpu.make_async_remote_copy(
            src_ref=src_ref.at[peer] if scatter else src_ref, dst_ref=out_ref.at[me],
            send_sem=send_sems.at[k - 1], recv_sem=recv_sems.at[k - 1],
            device_id=(px, py, pc), device_id_type=pl.DeviceIdType.MESH))
    return descs


def _exchange(comm, name):
    nc = len(comm)

    def body(*refs):
        srcs, outs, sems = refs[:nc], refs[nc:2 * nc], refs[2 * nc:]
        descs = [d for i in range(nc) for d in _comm_descs(srcs[i], outs[i], *sems[3 * i:3 * i + 3], comm[i][1])]
        for d in descs:
            d.start()
        for d in descs:
            d.wait()

    return pl.pallas_call(
        body, name=name, in_specs=[ANY_SPEC] * nc, out_specs=[ANY_SPEC] * nc,
        out_shape=[_comm_out_shape(*c) for c in comm],
        scratch_shapes=[s for _ in comm for s in _comm_sems()],
    )(*[c[0] for c in comm])


def _hosted_call(body, *, name, grid, in_specs, out_specs, out_shape, args, scratch_shapes=(), comm=()):
    nin, nout, nscr, nc = len(in_specs), len(out_specs), len(scratch_shapes), len(comm)

    def wrapped(*refs):
        ins = refs[:nin]
        csrc = refs[nin:nin + nc]
        outs = refs[nin + nc:nin + nc + nout]
        cout = refs[nin + nc + nout:nin + 2 * nc + nout]
        scr = refs[nin + 2 * nc + nout:]
        sems = scr[nscr:]

        def descs():
            return [d for i in range(nc) for d in _comm_descs(csrc[i], cout[i], *sems[3 * i:3 * i + 3], comm[i][1])]

        if nc:
            first = functools.reduce(jnp.logical_and, [pl.program_id(a) == 0 for a in range(len(grid))])

            @pl.when(first)
            def _():
                for d in descs():
                    d.start()

        body(*ins, *outs, *scr[:nscr])

        if nc:
            last = functools.reduce(jnp.logical_and, [pl.program_id(a) == grid[a] - 1 for a in range(len(grid))])

            @pl.when(last)
            def _():
                for d in descs():
                    d.wait()

    res = pl.pallas_call(
        wrapped, name=name, grid=grid,
        in_specs=[*in_specs, *[ANY_SPEC] * nc], out_specs=[*out_specs, *[ANY_SPEC] * nc],
        out_shape=[*out_shape, *[_comm_out_shape(*c) for c in comm]],
        scratch_shapes=[*scratch_shapes, *[s for _ in comm for s in _comm_sems()]],
        compiler_params=_cparams())(*args, *[c[0] for c in comm])
    return res[:nout], res[nout:]


def _norm_matmul(x, g, w, name, tm=1024, tn=1408, comm=()):
    S, dm = x.shape
    n = w.shape[1]
    tm = min(tm, S)

    def body(x_ref, g_ref, w_ref, xn_ref, o_ref):
        @pl.when(pl.program_id(1) == 0)
        def _():
            xv = x_ref[...]
            r = lax.rsqrt(jnp.mean(xv * xv, axis=-1, keepdims=True) + EPS)
            xn_ref[...] = ((xv * r) * g_ref[...]).astype(bf16)

        o_ref[...] = jnp.dot(xn_ref[...], w_ref[...], preferred_element_type=f32)

    return _hosted_call(
        body, name=name, grid=(S // tm, n // tn),
        in_specs=[pl.BlockSpec((tm, dm), lambda i, j: (i, 0)),
                  pl.BlockSpec((1, dm), lambda i, j: (0, 0)),
                  pl.BlockSpec((dm, tn), lambda i, j: (0, j))],
        out_specs=[pl.BlockSpec((tm, dm), lambda i, j: (i, 0)),
                   pl.BlockSpec((tm, tn), lambda i, j: (i, j))],
        out_shape=[jax.ShapeDtypeStruct((S, dm), bf16), jax.ShapeDtypeStruct((S, n), f32)],
        args=(x, g, w), comm=comm)


def _mm_tn(a, b, name, tk, tn, tmc=2048):
    m, ka = a.shape
    n = b.shape[1]
    tmc = min(tmc, m)

    def body(a_ref, b_ref, o_ref):
        @pl.when(pl.program_id(2) == 0)
        def _():
            o_ref[...] = jnp.zeros_like(o_ref)

        o_ref[...] += lax.dot_general(a_ref[...], b_ref[...], (((0,), (0,)), ((), ())),
                                      preferred_element_type=f32)

    return pl.pallas_call(
        body, name=name, grid=(ka // tk, n // tn, m // tmc),
        in_specs=[pl.BlockSpec((tmc, tk), lambda i, j, k: (k, i)),
                  pl.BlockSpec((tmc, tn), lambda i, j, k: (k, j))],
        out_specs=pl.BlockSpec((tk, tn), lambda i, j, k: (i, j)),
        out_shape=jax.ShapeDtypeStruct((ka, n), f32),
        compiler_params=_cparams())(a, b)


def _rows_at(ext, o, tc):
    if o == 0:
        return ext[8:8 + tc]
    return pltpu.roll(ext, (-o) % ext.shape[0], 0)[8:8 + tc]


def _halo_specs(tc, S, width, col):
    per = tc // 8
    last = S // 8 - 1
    return (pl.BlockSpec((tc, width), lambda i: (i, col)),
            pl.BlockSpec((8, width), lambda i: (jnp.maximum(i * per - 1, 0), col)),
            pl.BlockSpec((8, width), lambda i: (jnp.minimum((i + 1) * per, last), col)))


def _extended(cur_ref, prev_ref, next_ref, i, nsteps):
    prev = jnp.where(i > 0, prev_ref[...], 0.0)
    nxt = jnp.where(i < nsteps - 1, next_ref[...], 0.0)
    return jnp.concatenate([prev, cur_ref[...], nxt], axis=0)


def _conv_fwd(proj, cw, cb, tc=512):
    S = proj.shape[0]
    tc = min(tc, S)
    nsteps = S // tc

    def body(cur_ref, prev_ref, next_ref, w_ref, b_ref, o_ref):
        ext = _extended(cur_ref, prev_ref, next_ref, pl.program_id(0), nsteps)
        acc = _rows_at(ext, -2, tc) * w_ref[0:1, :]
        for k in range(1, 4):
            acc = acc + _rows_at(ext, k - 2, tc) * w_ref[k:k + 1, :]
        o_ref[...] = acc + b_ref[...]

    return pl.pallas_call(
        body, name="conv_fwd", grid=(nsteps,),
        in_specs=[*_halo_specs(tc, S, D, 0),
                  pl.BlockSpec((4, D), lambda i: (0, 0)), pl.BlockSpec((1, D), lambda i: (0, 0))],
        out_specs=pl.BlockSpec((tc, D), lambda i: (i, 0)),
        out_shape=jax.ShapeDtypeStruct((S, D), f32),
        compiler_params=_cparams())(proj, proj, proj, cw, cb)


def _conv_bwd(duc_f, duc_b, proj, cw, tc=512, comm=()):
    S = proj.shape[0]
    tc = min(tc, S)
    nsteps = S // tc

    def body(fc, fp, fn, bc, bp, bn, uc_, up, un, w_ref, du_ref, dw_ref, db_ref):
        i = pl.program_id(0)

        @pl.when(i == 0)
        def _():
            dw_ref[...] = jnp.zeros_like(dw_ref)
            db_ref[...] = jnp.zeros_like(db_ref)

        dext = _extended(fc, fp, fn, i, nsteps) + _extended(bc, bp, bn, i, nsteps)
        uext = _extended(uc_, up, un, i, nsteps)
        d = dext[8:8 + tc]
        acc = _rows_at(dext, 2, tc) * w_ref[0:1, :]
        for k in range(1, 4):
            acc = acc + _rows_at(dext, 2 - k, tc) * w_ref[k:k + 1, :]
        du_ref[...] = acc.astype(bf16)
        wrow = lax.broadcasted_iota(jnp.int32, (4, D), 0)
        for k in range(4):
            dw_ref[...] += jnp.where(wrow == k, jnp.sum(d * _rows_at(uext, k - 2, tc), axis=0, keepdims=True), 0.0)
        db_ref[...] += jnp.sum(d, axis=0, keepdims=True)

    return _hosted_call(
        body, name="conv_bwd", grid=(nsteps,),
        in_specs=[*_halo_specs(tc, S, D, 0), *_halo_specs(tc, S, D, 0), *_halo_specs(tc, S, D, 0),
                  pl.BlockSpec((4, D), lambda i: (0, 0))],
        out_specs=[pl.BlockSpec((tc, D), lambda i: (i, 0)),
                   pl.BlockSpec((4, D), lambda i: (0, 0)), pl.BlockSpec((1, D), lambda i: (0, 0))],
        out_shape=[jax.ShapeDtypeStruct((S, D), bf16), jax.ShapeDtypeStruct((4, D), f32),
                   jax.ShapeDtypeStruct((1, D), f32)],
        args=(duc_f, duc_f, duc_f, duc_b, duc_b, duc_b, proj, proj, proj, cw), comm=comm)


def _scan_chunk(a, b, carry, reverse):
    tc = a.shape[0]
    row = lax.broadcasted_iota(jnp.int32, a.shape, 0)
    s = 1
    while s < tc:
        shift = tc - s if reverse else s
        keep = (row < tc - s) if reverse else (row >= s)
        a_sh = pltpu.roll(a, shift, 0)
        b_sh = pltpu.roll(b, shift, 0)
        b = jnp.where(keep, a * b_sh + b, b)
        a = jnp.where(keep, a * a_sh, a)
        s *= 2
    return b + a * carry


def _lru_gates(uc, w, p_ref):
    pre = jnp.dot(uc.astype(bf16), w, preferred_element_type=f32)
    r = _sigmoid(pre[:, :LRU_GW] + p_ref[0, 1:2, :])
    gi = _sigmoid(pre[:, LRU_GW:] + p_ref[0, 2:3, :])
    sp = _softplus(-p_ref[0, 0:1, :])
    log_a = -RGLRU_C * r * sp
    a = jnp.exp(log_a)
    beta = jnp.sqrt(jnp.maximum(-_expm1(2.0 * log_a), 0.0))
    return r, gi, sp, a, beta


def _lru_fwd(uc, wg, lp, reverse, comm=()):
    S = uc.shape[0]
    tc = LRU_CHUNK
    rows = min(LRU_ROWS, S)
    nsub = rows // tc
    nblk = S // rows
    d = 1 if reverse else 0

    def bidx(c):
        return nblk - 1 - c if reverse else c

    def body(uc_ref, w_ref, p_ref, h_ref, carry_ref):
        @pl.when(pl.program_id(1) == 0)
        def _():
            carry_ref[...] = jnp.zeros_like(carry_ref)

        carry = carry_ref[...]
        for j in (reversed(range(nsub)) if reverse else range(nsub)):
            sl = slice(j * tc, (j + 1) * tc)
            ucv = uc_ref[sl, :]
            _, gi, _, a, beta = _lru_gates(ucv, w_ref[0], p_ref)
            h_ref[sl, :] = _scan_chunk(a, beta * (gi * ucv), carry, reverse)
            carry = h_ref[j * tc:j * tc + 1, :] if reverse else h_ref[(j + 1) * tc - 1:(j + 1) * tc, :]
        carry_ref[...] = carry

    return _hosted_call(
        body, name="lru_fwd_rev" if reverse else "lru_fwd", grid=(LRU_GROUPS, nblk),
        in_specs=[pl.BlockSpec((rows, LRU_GW), lambda g, c: (bidx(c), g)),
                  pl.BlockSpec((1, LRU_GW, 2 * LRU_GW), lambda g, c: (g, 0, d)),
                  pl.BlockSpec((1, 8, LRU_GW), lambda g, c: (d, 0, g))],
        out_specs=[pl.BlockSpec((rows, LRU_GW), lambda g, c: (bidx(c), g))],
        out_shape=[jax.ShapeDtypeStruct((S, D), f32)],
        scratch_shapes=[pltpu.VMEM((1, LRU_GW), f32)],
        args=(uc, wg, lp), comm=comm)


def _lru_bwd(uc, dh, h, wg, wgt, lp, reverse, comm=()):
    S = uc.shape[0]
    tc = LRU_CHUNK
    rows = min(LRU_ROWS, S)
    nsub = rows // tc
    nblk = S // rows
    d = 1 if reverse else 0
    per = rows // 8
    last8 = S // 8 - 1

    def bidx(c):
        return c if reverse else nblk - 1 - c

    def halo_idx(c):
        if reverse:
            return jnp.minimum((bidx(c) + 1) * per, last8)
        return jnp.maximum(bidx(c) * per - 1, 0)

    def body(uc_ref, dh_ref, h_ref, halo_ref, w_ref, wt_ref, p_ref, duc_ref, dw_ref, dp_ref, carry_ref, tmp_ref):
        c = pl.program_id(1)
        bi = bidx(c)

        @pl.when(c == 0)
        def _():
            carry_ref[...] = jnp.zeros_like(carry_ref)
            dw_ref[...] = jnp.zeros_like(dw_ref)
            dp_ref[...] = jnp.zeros_like(dp_ref)

        row = lax.broadcasted_iota(jnp.int32, (tc, LRU_GW), 0)
        carry = carry_ref[...]
        dw = jnp.zeros((LRU_GW, 2 * LRU_GW), f32)
        dsp = jnp.zeros((1, LRU_GW), f32)
        dba = jnp.zeros((1, LRU_GW), f32)
        dbx = jnp.zeros((1, LRU_GW), f32)
        for j in (range(nsub) if reverse else reversed(range(nsub))):
            sl = slice(j * tc, (j + 1) * tc)
            ucv = uc_ref[sl, :]
            ucb = ucv.astype(bf16)
            r, gi, sp, a, beta = _lru_gates(ucv, w_ref[0], p_ref)
            hv = h_ref[sl, :]
            if reverse:
                alpha = jnp.where(row == 0, 1.0, pltpu.roll(a, 1, 0))
                gsc = _scan_chunk(alpha, dh_ref[sl, :], carry, False)
                if j < nsub - 1:
                    edge = h_ref[(j + 1) * tc:(j + 1) * tc + 1, :]
                else:
                    edge = jnp.where(bi < nblk - 1, halo_ref[0:1, :], 0.0)
                h_nb = jnp.where(row == tc - 1, edge, pltpu.roll(hv, tc - 1, 0))
            else:
                alpha = jnp.where(row == tc - 1, 1.0, pltpu.roll(a, tc - 1, 0))
                gsc = _scan_chunk(alpha, dh_ref[sl, :], carry, True)
                if j > 0:
                    edge = h_ref[j * tc - 1:j * tc, :]
                else:
                    edge = jnp.where(bi > 0, halo_ref[7:8, :], 0.0)
                h_nb = jnp.where(row == 0, edge, pltpu.roll(hv, 1, 0))
            tmp_ref[...] = a * gsc
            carry = tmp_ref[tc - 1:tc, :] if reverse else tmp_ref[0:1, :]

            da = gsc * h_nb
            dbeta = gsc * (gi * ucv)
            dl = da * a - dbeta * (a * a) / beta
            dr = dl * (-RGLRU_C * sp)
            dsp = dsp + jnp.sum(dl * (-RGLRU_C * r), axis=0, keepdims=True)
            dgi = gsc * beta * ucv
            dpre_r = dr * r * (1.0 - r)
            dpre_i = dgi * gi * (1.0 - gi)
            dba = dba + jnp.sum(dpre_r, axis=0, keepdims=True)
            dbx = dbx + jnp.sum(dpre_i, axis=0, keepdims=True)
            dpre = jnp.concatenate([dpre_r, dpre_i], axis=1).astype(bf16)
            duc_ref[sl, :] = gsc * beta * gi + jnp.dot(dpre, wt_ref[0], preferred_element_type=f32)
            dw = dw + lax.dot_general(ucb, dpre, (((0,), (0,)), ((), ())), preferred_element_type=f32)
        carry_ref[...] = carry
        dw_ref[0] += dw
        dlam = dsp * (-_sigmoid(-p_ref[0, 0:1, :]))
        prow = lax.broadcasted_iota(jnp.int32, (8, LRU_GW), 0)
        dp_ref[...] += (jnp.where(prow == 0, dlam, 0.0) + jnp.where(prow == 1, dba, 0.0)
                        + jnp.where(prow == 2, dbx, 0.0))

    chunk = pl.BlockSpec((rows, LRU_GW), lambda g, c: (bidx(c), g))
    return _hosted_call(
        body, name="lru_bwd_rev" if reverse else "lru_bwd", grid=(LRU_GROUPS, nblk),
        in_specs=[chunk, chunk, chunk,
                  pl.BlockSpec((8, LRU_GW), lambda g, c: (halo_idx(c), g)),
                  pl.BlockSpec((1, LRU_GW, 2 * LRU_GW), lambda g, c: (g, 0, d)),
                  pl.BlockSpec((1, 2 * LRU_GW, LRU_GW), lambda g, c: (g, d, 0)),
                  pl.BlockSpec((1, 8, LRU_GW), lambda g, c: (d, 0, g))],
        out_specs=[chunk,
                   pl.BlockSpec((1, LRU_GW, 2 * LRU_GW), lambda g, c: (g, 0, 0)),
                   pl.BlockSpec((8, LRU_GW), lambda g, c: (0, g))],
        out_shape=[jax.ShapeDtypeStruct((S, D), f32),
                   jax.ShapeDtypeStruct((LRU_GROUPS, LRU_GW, 2 * LRU_GW), f32),
                   jax.ShapeDtypeStruct((8, D), f32)],
        scratch_shapes=[pltpu.VMEM((1, LRU_GW), f32), pltpu.VMEM((tc, LRU_GW), f32)],
        args=(uc, dh, h, h, wg, wgt, lp), comm=comm)


def _slope(h):
    return 2.0 ** (-8.0 * (h + 1.0) / N_HEADS)


def _kv_specs(nb, col):
    return [pl.BlockSpec((BLK, N_KV * HEAD_DIM), lambda n: (jnp.maximum(n - 1, 0), col)),
            pl.BlockSpec((BLK, N_KV * HEAD_DIM), lambda n: (n, col)),
            pl.BlockSpec((BLK, N_KV * HEAD_DIM), lambda n: (jnp.minimum(n + 1, nb - 1), col))]


def _dup_windows(r0, r1, r2):
    left = lax.broadcasted_iota(jnp.int32, (3 * BLK, 128), 1) < HEAD_DIM
    win = jnp.concatenate([r0[...], r1[...], r2[...]], axis=0)
    out = []
    for i in range(N_KV // 2):
        t = win[:, i * 128:(i + 1) * 128]
        r = pltpu.roll(t, HEAD_DIM, 1)
        out += [jnp.where(left, t, r).astype(bf16), jnp.where(left, r, t).astype(bf16)]
    return out


def _attn_bias_init(bias_ref):
    k_loc = lax.broadcasted_iota(jnp.int32, (3 * BLK, BLK), 0)
    q_loc = lax.broadcasted_iota(jnp.int32, (3 * BLK, BLK), 1)
    adist = jnp.abs(q_loc + BLK - k_loc)
    adf = adist.astype(f32)
    for e in range(3):
        ok = adist <= WINDOW
        if e == 0:
            ok = ok & (k_loc >= BLK)
        if e == 2:
            ok = ok & (k_loc < 2 * BLK)
        for kv in range(N_KV):
            bias_ref[e, kv] = jnp.concatenate(
                [jnp.where(ok, (-_slope(4 * kv + j)) * adf, NEG_INF) for j in range(4)], axis=1)


def _stack_heads(ref, kv, scale):
    left = lax.broadcasted_iota(jnp.int32, (BLK, 128), 1) < HEAD_DIM
    rows = []
    for pp in range(2):
        t = ref[:, (2 * kv + pp) * 128:(2 * kv + pp + 1) * 128]
        if scale != 1.0:
            t = t * scale
        zero = jnp.zeros_like(t)
        rows += [jnp.where(left, t, zero).astype(bf16), jnp.where(left, zero, t).astype(bf16)]
    return jnp.concatenate(rows, axis=0)


def _attn_softmax(qs, k2, bias, sink_ref, kv):
    sink = jnp.concatenate([jnp.full((1, BLK), sink_ref[0, 4 * kv + j], f32) for j in range(4)], axis=1)
    s = lax.dot_general(k2, qs, (((1,), (1,)), ((), ())), preferred_element_type=f32) + bias
    m = jnp.maximum(jnp.max(s, axis=0, keepdims=True), sink)
    p = jnp.exp(s - m)
    ps = jnp.exp(sink - m)
    inv = 1.0 / (jnp.sum(p, axis=0, keepdims=True) + ps)
    return p, ps, inv


def _pair_tiles(t):
    return [jnp.concatenate([t[:HEAD_DIM, 256 * pp:256 * pp + 128],
                             t[HEAD_DIM:, 256 * pp + 128:256 * pp + 256]], axis=0).T for pp in range(2)]


def _attn_fwd(proj, sink, comm=()):
    S = proj.shape[0]
    nb = S // BLK
    assert nb >= 2

    def body(q_ref, k0, k1, k2_, v0, v1, v2_, sink_ref, o_ref, bias_ref):
        n = pl.program_id(0)

        @pl.when(n == 0)
        def _():
            _attn_bias_init(bias_ref)

        e = jnp.where(n == 0, 0, jnp.where(n == nb - 1, 2, 1))
        kk = _dup_windows(k0, k1, k2_)
        vv = _dup_windows(v0, v1, v2_)
        tiles = []
        for kv in range(N_KV):
            qs = _stack_heads(q_ref, kv, HEAD_DIM ** -0.5)
            p, _, inv = _attn_softmax(qs, kk[kv], bias_ref[e, kv], sink_ref, kv)
            ot = lax.dot_general(vv[kv], p.astype(bf16), (((0,), (0,)), ((), ())), preferred_element_type=f32)
            tiles += _pair_tiles(ot * inv)
        o_ref[...] = jnp.concatenate(tiles, axis=1)

    return _hosted_call(
        body, name="attn_fwd", grid=(nb,),
        in_specs=[pl.BlockSpec((BLK, D), lambda n: (n, C_Q // D)),
                  *_kv_specs(nb, C_K // (N_KV * HEAD_DIM)), *_kv_specs(nb, C_V // (N_KV * HEAD_DIM)),
                  pl.BlockSpec(memory_space=pltpu.SMEM)],
        out_specs=[pl.BlockSpec((BLK, D), lambda n: (n, 0))],
        out_shape=[jax.ShapeDtypeStruct((S, D), f32)],
        scratch_shapes=[pltpu.VMEM((3, N_KV, 3 * BLK, 4 * BLK), f32)],
        args=(proj, proj, proj, proj, proj, proj, proj, sink), comm=comm)


def _attn_bwd(proj, sink, dyb, comm=()):
    S = proj.shape[0]
    nb = S // BLK
    assert nb >= 2

    def body(q_ref, k0, k1, k2_, v0, v1, v2_, sink_ref, do_ref, dq_ref, dk_out, dv_out, ds_ref,
             bias_ref, dk_ref, dv_ref, dsk_ref):
        n = pl.program_id(0)

        @pl.when(n == 0)
        def _():
            _attn_bias_init(bias_ref)
            dk_ref[...] = jnp.zeros_like(dk_ref)
            dv_ref[...] = jnp.zeros_like(dv_ref)
            dsk_ref[...] = jnp.zeros_like(dsk_ref)

        e = jnp.where(n == 0, 0, jnp.where(n == nb - 1, 2, 1))
        kk = _dup_windows(k0, k1, k2_)
        vv = _dup_windows(v0, v1, v2_)
        left3 = lax.broadcasted_iota(jnp.int32, (3 * BLK, 128), 1) < HEAD_DIM
        start = pl.multiple_of(n * BLK, BLK)
        dq_tiles, dks, dvs = [], [], []
        for kv in range(N_KV):
            qs = _stack_heads(q_ref, kv, HEAD_DIM ** -0.5)
            dos = _stack_heads(do_ref, kv, 1.0)
            p, ps, inv = _attn_softmax(qs, kk[kv], bias_ref[e, kv], sink_ref, kv)
            pn = p * inv
            dp = lax.dot_general(vv[kv], dos, (((1,), (1,)), ((), ())), preferred_element_type=f32)
            delta = jnp.sum(pn * dp, axis=0, keepdims=True)
            dsc = (pn * (dp - delta)).astype(bf16)
            dsk_ref[kv:kv + 1, :] += delta * (ps * inv)
            dqt = lax.dot_general(kk[kv], dsc, (((0,), (0,)), ((), ())), preferred_element_type=f32)
            dq_tiles += _pair_tiles(dqt * (HEAD_DIM ** -0.5))
            dk = jnp.dot(dsc, qs, preferred_element_type=f32)
            dv = jnp.dot(pn.astype(bf16), dos, preferred_element_type=f32)
            dks.append(dk + pltpu.roll(dk, HEAD_DIM, 1))
            dvs.append(dv + pltpu.roll(dv, HEAD_DIM, 1))
        for jp in range(N_KV // 2):
            cols = slice(jp * 128, (jp + 1) * 128)
            dk_ref[pl.ds(start, 3 * BLK), cols] += jnp.where(left3, dks[2 * jp], dks[2 * jp + 1])
            dv_ref[pl.ds(start, 3 * BLK), cols] += jnp.where(left3, dvs[2 * jp], dvs[2 * jp + 1])
        dq_ref[...] = jnp.concatenate(dq_tiles, axis=1).astype(bf16)

        @pl.when(n == nb - 1)
        def _():
            pltpu.sync_copy(dk_ref, dk_out)
            pltpu.sync_copy(dv_ref, dv_out)
            lane = lax.broadcasted_iota(jnp.int32, (1, 128), 1)
            dsink = jnp.zeros((1, 128), f32)
            for h in range(N_HEADS):
                part = dsk_ref[h // 4:h // 4 + 1, (h % 4) * BLK:(h % 4 + 1) * BLK]
                dsink = dsink + jnp.where(lane == h, -jnp.sum(part), 0.0)
            ds_ref[...] = dsink

    acc = jax.ShapeDtypeStruct((S + 2 * BLK, N_KV * HEAD_DIM), f32)
    return _hosted_call(
        body, name="attn_bwd", grid=(nb,),
        in_specs=[pl.BlockSpec((BLK, D), lambda n: (n, C_Q // D)),
                  *_kv_specs(nb, C_K // (N_KV * HEAD_DIM)), *_kv_specs(nb, C_V // (N_KV * HEAD_DIM)),
                  pl.BlockSpec(memory_space=pltpu.SMEM),
                  pl.BlockSpec((BLK, D), lambda n: (n, 0))],
        out_specs=[pl.BlockSpec((BLK, D), lambda n: (n, 0)), ANY_SPEC, ANY_SPEC,
                   pl.BlockSpec((1, 128), lambda n: (0, 0))],
        out_shape=[jax.ShapeDtypeStruct((S, D), bf16), acc, acc, jax.ShapeDtypeStruct((1, 128), f32)],
        scratch_shapes=[pltpu.VMEM((3, N_KV, 3 * BLK, 4 * BLK), f32), pltpu.VMEM(acc.shape, f32),
                        pltpu.VMEM(acc.shape, f32), pltpu.VMEM((8, 4 * BLK), f32)],
        args=(proj, proj, proj, proj, proj, proj, proj, sink, dyb), comm=comm)


def _merge_parts(hf, hb, g, z0, z1, yb, bg):
    g0 = _sigmoid(z0 + bg[:, :D])
    g1 = _sigmoid(z1 + bg[:, D:])
    gelu, dgelu = _gelu_and_grad(g)
    hs = hf + hb
    ya = hs * gelu
    return g0, g1, gelu, dgelu, hs, ya


def _merge_outproj(x, hf, hb, proj, yb, bg, w_out, tm=512):
    S = x.shape[0]
    tm = min(tm, S)

    def body(x_ref, hf_ref, hb_ref, g_ref, z0_ref, z1_ref, yb_ref, bg_ref, w_ref, mg_ref, x1_ref):
        ybv = yb_ref[...]
        g0, g1, _, _, _, ya = _merge_parts(hf_ref[...], hb_ref[...], g_ref[...], z0_ref[...], z1_ref[...],
                                           ybv, bg_ref[...])
        mg = (g0 * ya + g1 * ybv).astype(bf16)
        mg_ref[...] = mg
        x1_ref[...] = x_ref[...] + jnp.dot(mg, w_ref[...], preferred_element_type=f32)

    row = pl.BlockSpec((tm, D), lambda i: (i, 0))
    return pl.pallas_call(
        body, name="merge_outproj", grid=(S // tm,),
        in_specs=[row, row, row,
                  pl.BlockSpec((tm, D), lambda i: (i, C_G // D)),
                  pl.BlockSpec((tm, D), lambda i: (i, C_Z0 // D)),
                  pl.BlockSpec((tm, D), lambda i: (i, C_Z1 // D)),
                  row, pl.BlockSpec((1, 2 * D), lambda i: (0, 0)), pl.BlockSpec((D, D), lambda i: (0, 0))],
        out_specs=[row, row],
        out_shape=[jax.ShapeDtypeStruct((S, D), bf16), jax.ShapeDtypeStruct((S, D), f32)],
        compiler_params=_cparams())(x, hf, hb, proj, proj, proj, yb, bg, w_out)


def _ffn_out_loss(gu, x1, w_fo, g3, tgt, tm=256):
    S = x1.shape[0]
    tm = min(tm, S)

    def body(gt_ref, up_ref, x1_ref, w_ref, g_ref, t_ref, ff_ref, dx_ref, dxb_ref, loss_ref, dg_ref):
        @pl.when(pl.program_id(0) == 0)
        def _():
            loss_ref[...] = jnp.zeros_like(loss_ref)
            dg_ref[...] = jnp.zeros_like(dg_ref)

        gt = gt_ref[...]
        ff = ((gt * _sigmoid(gt)) * up_ref[...]).astype(bf16)
        ff_ref[...] = ff
        x2 = x1_ref[...] + jnp.dot(ff, w_ref[...], preferred_element_type=f32)
        gv = g_ref[...]
        r = lax.rsqrt(jnp.mean(x2 * x2, axis=-1, keepdims=True) + EPS)
        xh = x2 * r
        diff = xh * gv - t_ref[...]
        loss_ref[...] += (0.5 / D) * jnp.sum(diff * diff)
        dy = diff * (1.0 / D)
        dg_ref[...] += jnp.sum(dy * xh, axis=0, keepdims=True)
        dxh = dy * gv
        dx = r * (dxh - xh * jnp.mean(dxh * xh, axis=-1, keepdims=True))
        dx_ref[...] = dx
        dxb_ref[...] = dx.astype(bf16)

    row = pl.BlockSpec((tm, D), lambda i: (i, 0))
    vec = pl.BlockSpec((1, D), lambda i: (0, 0))
    return pl.pallas_call(
        body, name="ffn_out_loss", grid=(S // tm,),
        in_specs=[pl.BlockSpec((tm, D_FF), lambda i: (i, 0)), pl.BlockSpec((tm, D_FF), lambda i: (i, 1)),
                  row, pl.BlockSpec((D_FF, D), lambda i: (0, 0)), vec, row],
        out_specs=[pl.BlockSpec((tm, D_FF), lambda i: (i, 0)), row, row,
                   pl.BlockSpec((1, 128), lambda i: (0, 0)), vec],
        out_shape=[jax.ShapeDtypeStruct((S, D_FF), bf16), jax.ShapeDtypeStruct((S, D), f32),
                   jax.ShapeDtypeStruct((S, D), bf16), jax.ShapeDtypeStruct((1, 128), f32),
                   jax.ShapeDtypeStruct((1, D), f32)],
        compiler_params=_cparams())(gu, gu, x1, w_fo, g3, tgt)


def _ffn_bwd1(dx2b, w_fot, gu, tm=256, comm=()):
    S = dx2b.shape[0]
    tm = min(tm, S)

    def body(dx_ref, w_ref, gt_ref, up_ref, dgt_ref, dup_ref):
        dff = jnp.dot(dx_ref[...], w_ref[...], preferred_element_type=f32)
        gt = gt_ref[...]
        sg = _sigmoid(gt)
        dup_ref[...] = (dff * (gt * sg)).astype(bf16)
        dgt_ref[...] = ((dff * up_ref[...]) * (sg * (1.0 + gt * (1.0 - sg)))).astype(bf16)

    wide = pl.BlockSpec((tm, D_FF), lambda i: (i, 0))
    return _hosted_call(
        body, name="ffn_bwd1", grid=(S // tm,),
        in_specs=[pl.BlockSpec((tm, D), lambda i: (i, 0)), pl.BlockSpec((D, D_FF), lambda i: (0, 0)),
                  wide, pl.BlockSpec((tm, D_FF), lambda i: (i, 1))],
        out_specs=[wide, wide],
        out_shape=[jax.ShapeDtypeStruct((S, D_FF), bf16), jax.ShapeDtypeStruct((S, D_FF), bf16)],
        args=(dx2b, w_fot, gu, gu), comm=comm)


def _proj_bwd(pieces, wts, xres, g, dres, name, tm=256, comm=()):
    S = xres.shape[0]
    tm = min(tm, S)
    np_ = len(pieces)

    def body(*refs):
        p_refs = refs[:np_]
        w_refs = refs[np_:2 * np_]
        x_ref, g_ref, dres_ref, dx_ref, dxb_ref, dg_ref = refs[2 * np_:]

        @pl.when(pl.program_id(0) == 0)
        def _():
            dg_ref[...] = jnp.zeros_like(dg_ref)

        dn = jnp.dot(p_refs[0][...], w_refs[0][...], preferred_element_type=f32)
        for pr, wr in zip(p_refs[1:], w_refs[1:]):
            dn = dn + jnp.dot(pr[...], wr[...], preferred_element_type=f32)
        dxn, dgc = _rms_bwd(dn, x_ref[...], g_ref[...])
        dx = dres_ref[...] + dxn
        dx_ref[...] = dx
        dxb_ref[...] = dx.astype(bf16)
        dg_ref[...] += jnp.sum(dgc, axis=0, keepdims=True)

    row = pl.BlockSpec((tm, D), lambda i: (i, 0))
    vec = pl.BlockSpec((1, D), lambda i: (0, 0))
    return _hosted_call(
        body, name=name, grid=(S // tm,),
        in_specs=[*[pl.BlockSpec((tm, p.shape[1]), lambda i: (i, 0)) for p in pieces],
                  *[pl.BlockSpec(w.shape, lambda i: (0, 0)) for w in wts],
                  row, vec, row],
        out_specs=[row, row, vec],
        out_shape=[jax.ShapeDtypeStruct((S, D), f32), jax.ShapeDtypeStruct((S, D), bf16),
                   jax.ShapeDtypeStruct((1, D), f32)],
        args=(*pieces, *wts, xres, g, dres), comm=comm)


def _outproj_bwd(dx1b, w_outt, hf, hb, proj, yb, bg, tm=512):
    S = dx1b.shape[0]
    tm = min(tm, S)

    def body(dx_ref, w_ref, hf_ref, hb_ref, g_ref, z0_ref, z1_ref, yb_ref, bg_ref,
             dh_ref, dg_ref, dz_ref, dyb_ref, dbg_ref):
        @pl.when(pl.program_id(0) == 0)
        def _():
            dbg_ref[...] = jnp.zeros_like(dbg_ref)

        dm = jnp.dot(dx_ref[...], w_ref[...], preferred_element_type=f32)
        ybv = yb_ref[...]
        g0, g1, gelu, dgelu, hs, ya = _merge_parts(hf_ref[...], hb_ref[...], g_ref[...], z0_ref[...],
                                                   z1_ref[...], ybv, bg_ref[...])
        dya = dm * g0
        dh_ref[...] = dya * gelu
        dg_ref[...] = (dya * hs * dgelu).astype(bf16)
        dyb_ref[...] = (dm * g1).astype(bf16)
        dz0 = (dm * ya) * (g0 * (1.0 - g0))
        dz1 = (dm * ybv) * (g1 * (1.0 - g1))
        dz = jnp.concatenate([dz0, dz1], axis=1)
        dz_ref[...] = dz.astype(bf16)
        dbg_ref[...] += jnp.sum(dz, axis=0, keepdims=True)

    row = pl.BlockSpec((tm, D), lambda i: (i, 0))
    return pl.pallas_call(
        body, name="outproj_bwd", grid=(S // tm,),
        in_specs=[row, pl.BlockSpec((D, D), lambda i: (0, 0)), row, row,
                  pl.BlockSpec((tm, D), lambda i: (i, C_G // D)),
                  pl.BlockSpec((tm, D), lambda i: (i, C_Z0 // D)),
                  pl.BlockSpec((tm, D), lambda i: (i, C_Z1 // D)),
                  row, pl.BlockSpec((1, 2 * D), lambda i: (0, 0))],
        out_specs=[row, row, pl.BlockSpec((tm, 2 * D), lambda i: (i, 0)), row,
                   pl.BlockSpec((1, 2 * D), lambda i: (0, 0))],
        out_shape=[jax.ShapeDtypeStruct((S, D), f32), jax.ShapeDtypeStruct((S, D), bf16),
                   jax.ShapeDtypeStruct((S, 2 * D), bf16), jax.ShapeDtypeStruct((S, D), bf16),
                   jax.ShapeDtypeStruct((1, 2 * D), f32)],
        compiler_params=_cparams())(dx1b, w_outt, hf, hb, proj, proj, proj, yb, bg)


def _block_diag_groups(w):
    w4 = w.reshape(LRU_GROUPS, 4, LRU_BLOCK, LRU_BLOCK)
    eye = jnp.eye(4, dtype=w.dtype)
    return jnp.einsum("ghij,hk->ghikj", w4, eye).reshape(LRU_GROUPS, LRU_GW, LRU_GW)


def _diag_blocks(dw):
    d5 = dw.reshape(LRU_GROUPS, 4, LRU_BLOCK, 4, LRU_BLOCK)
    return jnp.stack([d5[:, h, :, h, :] for h in range(4)], axis=1).reshape(LRU_HEADS, LRU_BLOCK, LRU_BLOCK)


def _local_step(x, tgt, small, env, before=lambda name: (), after=lambda name, got: None):
    S = x.shape[0]
    g1, g2, g3 = small["norm_mix_g"], small["norm_ffn_g"], small["norm_final_g"]
    bg, cw, cb = small["b_gate"], small["conv_w"], small["conv_b"]
    sink = small["attn_sink"]

    wg = jnp.concatenate([_block_diag_groups(small["lru_wa"][0]), _block_diag_groups(small["lru_wx"][0]),
                          _block_diag_groups(small["lru_wa"][1]), _block_diag_groups(small["lru_wx"][1])],
                         axis=2).astype(bf16)
    wgt = jnp.swapaxes(wg, 1, 2)
    zeros5 = jnp.zeros((5, D), f32)
    lp = jnp.stack([jnp.concatenate([small["lru_lambda"][d:d + 1], small["lru_ba"][d:d + 1],
                                     small["lru_bx"][d:d + 1], zeros5], axis=0) for d in range(2)])

    def hosted(name, fn, *args, **kw):
        outs, got = fn(*args, comm=tuple(before(name)), **kw)
        after(name, got)
        return outs

    xn, proj = hosted("norm_inproj", _norm_matmul, x, g1, env["w_in_p"], "norm_inproj")
    uc = _conv_fwd(proj, cw, cb)
    (hf,) = hosted("lru_fwd", _lru_fwd, uc, wg, lp, False)
    (hb,), _ = _lru_fwd(uc, wg, lp, True)
    (yb,) = hosted("attn_fwd", _attn_fwd, proj, sink)
    merged, x1 = _merge_outproj(x, hf, hb, proj, yb, bg, env["w_out"])
    (xn2, gu), _ = _norm_matmul(x1, g2, env["w_fi"], "norm_ffn_in")
    ff, dx2, dx2b, loss, dg3 = _ffn_out_loss(gu, x1, env["w_fo"], g3, tgt)

    env["dw_fo"] = _mm_tn(ff, dx2b, "dw_ffn_out", tk=1408, tn=1024)
    dgt, dup = hosted("ffn_bwd1", _ffn_bwd1, dx2b, env["w_fo"].T, gu)
    env["dw_fi"] = jnp.concatenate([_mm_tn(xn2, dgt, "dw_ffn_in_gate", tk=1024, tn=1408),
                                    _mm_tn(xn2, dup, "dw_ffn_in_up", tk=1024, tn=1408)], axis=1)
    w_fit = env["w_fi"].T
    (dx1, dx1b, dg2), _ = _proj_bwd([dgt, dup], [w_fit[:D_FF], w_fit[D_FF:]], x1, g2, dx2, "ffn_in_bwd")
    env["dw_out"] = _mm_tn(merged, dx1b, "dw_out", tk=1024, tn=1024)
    dh, dgl, dz, dyb, dbg = _outproj_bwd(dx1b, env["w_out"].T, hf, hb, proj, yb, bg)
    dq, dk2, dv2, dsink = hosted("attn_bwd", _attn_bwd, proj, sink, dyb)
    dkv = jnp.concatenate([dk2[BLK:BLK + S], dv2[BLK:BLK + S]], axis=1).astype(bf16)
    duc_f, dwg_f, dp_f = hosted("lru_bwd", _lru_bwd, uc, dh, hf, wg, wgt, lp, False)
    (duc_b, dwg_b, dp_b), _ = _lru_bwd(uc, dh, hb, wg, wgt, lp, True)
    env["grads_early"] = {
        "loss": loss[:, :1], "b_gate": dbg,
        "lru_lambda": jnp.concatenate([dp_f[0:1], dp_b[0:1]], axis=0),
        "lru_wa": jnp.stack([_diag_blocks(dwg_f[:, :, :LRU_GW]), _diag_blocks(dwg_b[:, :, :LRU_GW])]),
        "lru_ba": jnp.concatenate([dp_f[1:2], dp_b[1:2]], axis=0),
        "lru_wx": jnp.stack([_diag_blocks(dwg_f[:, :, LRU_GW:]), _diag_blocks(dwg_b[:, :, LRU_GW:])]),
        "lru_bx": jnp.concatenate([dp_f[2:3], dp_b[2:3]], axis=0),
        "attn_sink": dsink[:, :N_HEADS], "norm_ffn_g": dg2, "norm_final_g": dg3,
    }
    du, dcw, dcb = hosted("conv_bwd", _conv_bwd, duc_f, duc_b, proj, cw)
    pieces = [du, dgl, dq, dz, dkv]
    bounds = [0, 1024, 2048, 3072, 5120, 5632]
    env["dw_in"] = _unperm_cols(jnp.concatenate(
        [_mm_tn(xn, p, "dw_in_%d" % i, tk=1024, tn=min(p.shape[1], 1024)) for i, p in enumerate(pieces)], axis=1))
    w_int = env["w_in_p"].T
    dx, _, dg1 = hosted("inproj_bwd", _proj_bwd, pieces, [w_int[bounds[i]:bounds[i + 1]] for i in range(5)],
                        x, g1, dx1, "inproj_bwd")

    grads = dict(env["grads_early"], norm_mix_g=dg1, conv_w=dcw, conv_b=dcb)
    return dx, grads


def _adamw(gparts, w, m, v, name, tr=256):
    n, rows, cols = gparts.shape
    tr = _div_tile(rows, tr)
    c1 = 1.0 - ADAM_B1 ** ADAM_STEP
    c2 = 1.0 - ADAM_B2 ** ADAM_STEP

    def body(g_ref, w_ref, m_ref, v_ref, go_ref, d_ref, mo_ref, vo_ref):
        g = g_ref[0].astype(f32)
        for j in range(1, n):
            g = g + g_ref[j].astype(f32)
        mn = ADAM_B1 * m_ref[...] + (1.0 - ADAM_B1) * g
        vn = ADAM_B2 * v_ref[...] + (1.0 - ADAM_B2) * (g * g)
        m_hat = mn / c1
        v_hat = vn / c2
        go_ref[...] = g
        d_ref[...] = -ADAM_LR * (m_hat / (jnp.sqrt(v_hat) + ADAM_EPS) + ADAM_WD * w_ref[...])
        mo_ref[...] = mn
        vo_ref[...] = vn

    blk = pl.BlockSpec((tr, cols), lambda i: (i, 0))
    shp = jax.ShapeDtypeStruct((rows, cols), f32)
    return pl.pallas_call(
        body, name=name, grid=(rows // tr,),
        in_specs=[pl.BlockSpec((n, tr, cols), lambda i: (0, i, 0)), blk, blk, blk],
        out_specs=[blk, blk, blk, blk], out_shape=[shp, shp, shp, shp],
        compiler_params=_cparams())(gparts, w, m, v)


def _sum_parts(parts, name):
    n, rows, cols = parts.shape

    def body(p_ref, o_ref):
        acc = p_ref[0].astype(f32)
        for j in range(1, n):
            acc = acc + p_ref[j].astype(f32)
        o_ref[...] = acc

    return pl.pallas_call(
        body, name=name, out_shape=jax.ShapeDtypeStruct((rows, cols), f32),
        compiler_params=_cparams())(parts)


def _pack_rows(arrs, dtype=f32):
    rows, spans, at = [], [], 0
    for a in arrs:
        flat = a.reshape(-1).astype(dtype)
        nr = -(-flat.shape[0] // 1024)
        rows.append(jnp.pad(flat, (0, nr * 1024 - flat.shape[0])).reshape(nr, 1024))
        spans.append((at, nr))
        at += nr
    pad = (-at) % 16
    if pad:
        rows.append(jnp.zeros((pad, 1024), dtype))
    return jnp.concatenate(rows, axis=0), spans


def _unpack_rows(packed, spans, shapes):
    out = []
    for (at, nr), shp in zip(spans, shapes):
        n = math.prod(shp)
        out.append(packed[at:at + nr].reshape(-1)[:n].reshape(shp))
    return out


BIG = ("w_in", "w_out", "w_ffn_in", "w_ffn_out")
SMALL_REPL = ("norm_mix_g", "b_gate", "conv_b", "lru_wa", "lru_wx", "attn_sink", "norm_ffn_g", "norm_final_g")
SMALL_SHARD = ("conv_w", "lru_lambda", "lru_ba", "lru_bx")
ORDER = ("norm_mix_g", "w_in", "b_gate", "conv_w", "conv_b", "lru_lambda", "lru_wa", "lru_ba", "lru_wx",
         "lru_bx", "attn_sink", "w_out", "norm_ffn_g", "w_ffn_in", "w_ffn_out", "norm_final_g")
EARLY_F32 = ("loss", "b_gate", "lru_lambda", "lru_ba", "lru_bx", "attn_sink", "norm_ffn_g", "norm_final_g")
EARLY_BF16 = ("lru_wa", "lru_wx")
LATE = ("norm_mix_g", "conv_w", "conv_b")


def kernel(x, norm_mix_g, w_in, b_gate, conv_w, conv_b, lru_lambda, lru_wa, lru_ba, lru_wx, lru_bx, attn_sink, w_out, norm_ffn_g, w_ffn_in, w_ffn_out, norm_final_g, loss_target, m_norm_mix_g, m_w_in, m_b_gate, m_conv_w, m_conv_b, m_lru_lambda, m_lru_wa, m_lru_ba, m_lru_wx, m_lru_bx, m_attn_sink, m_w_out, m_norm_ffn_g, m_w_ffn_in, m_w_ffn_out, m_norm_final_g, v_norm_mix_g, v_w_in, v_b_gate, v_conv_w, v_conv_b, v_lru_lambda, v_lru_wa, v_lru_ba, v_lru_wx, v_lru_bx, v_attn_sink, v_w_out, v_norm_ffn_g, v_w_ffn_in, v_w_ffn_out, v_norm_final_g):
    w = dict(norm_mix_g=norm_mix_g, w_in=w_in, b_gate=b_gate, conv_w=conv_w, conv_b=conv_b, lru_lambda=lru_lambda,
             lru_wa=lru_wa, lru_ba=lru_ba, lru_wx=lru_wx, lru_bx=lru_bx, attn_sink=attn_sink, w_out=w_out,
             norm_ffn_g=norm_ffn_g, w_ffn_in=w_ffn_in, w_ffn_out=w_ffn_out, norm_final_g=norm_final_g)
    m = dict(norm_mix_g=m_norm_mix_g, w_in=m_w_in, b_gate=m_b_gate, conv_w=m_conv_w, conv_b=m_conv_b,
             lru_lambda=m_lru_lambda, lru_wa=m_lru_wa, lru_ba=m_lru_ba, lru_wx=m_lru_wx, lru_bx=m_lru_bx,
             attn_sink=m_attn_sink, w_out=m_w_out, norm_ffn_g=m_norm_ffn_g, w_ffn_in=m_w_ffn_in,
             w_ffn_out=m_w_ffn_out, norm_final_g=m_norm_final_g)
    v = dict(norm_mix_g=v_norm_mix_g, w_in=v_w_in, b_gate=v_b_gate, conv_w=v_conv_w, conv_b=v_conv_b,
             lru_lambda=v_lru_lambda, lru_wa=v_lru_wa, lru_ba=v_lru_ba, lru_wx=v_lru_wx, lru_bx=v_lru_bx,
             attn_sink=v_attn_sink, w_out=v_w_out, norm_ffn_g=v_norm_ffn_g, w_ffn_in=v_w_ffn_in,
             w_ffn_out=v_w_ffn_out, norm_final_g=v_norm_final_g)
    me = 4 * lax.axis_index("x") + 2 * lax.axis_index("y") + lax.axis_index("c")

    def cols_full(got):
        return jnp.swapaxes(got, 0, 1).reshape(got.shape[1], -1)

    def cols_parts(g):
        return jnp.swapaxes(g.reshape(g.shape[0], N_DEV, -1), 0, 1)

    def rows_parts(g):
        return g.reshape(N_DEV, -1, g.shape[1])

    shard_rows = jnp.concatenate([w[n][0] for n in SMALL_SHARD], axis=0)
    got_w_in, got_rows = _exchange([(w_in[0].astype(bf16), False), (shard_rows, False)], "gather_w_in")
    full_rows = cols_full(got_rows)
    small = {n: w[n] for n in ("norm_mix_g", "b_gate", "conv_b", "attn_sink", "norm_ffn_g")}
    small["lru_wa"], small["lru_wx"] = lru_wa[0], lru_wx[0]
    small["norm_final_g"] = norm_final_g.reshape(1, D)
    small["conv_w"], small["lru_lambda"] = full_rows[0:4], full_rows[4:6]
    small["lru_ba"], small["lru_bx"] = full_rows[6:8], full_rows[8:10]

    env = {"w_in_p": _perm_cols(cols_full(got_w_in))}
    recv = {}

    def before(name):
        if name == "norm_inproj":
            return [(w_out[0].astype(bf16), False), (w_ffn_out[0].astype(bf16), False)]
        if name == "lru_fwd":
            return [(w_ffn_in[0].astype(bf16), False)]
        if name == "ffn_bwd1":
            return [(rows_parts(env["dw_fo"]).astype(bf16), True)]
        if name == "attn_bwd":
            return [(rows_parts(env["dw_out"]).astype(bf16), True)]
        if name == "lru_bwd":
            return [(cols_parts(env["dw_fi"]).astype(bf16), True)]
        if name == "conv_bwd":
            ge = env["grads_early"]
            p32, env["early_f32_spans"] = _pack_rows([ge[n] for n in EARLY_F32])
            p16, env["early_bf16_spans"] = _pack_rows([ge[n] for n in EARLY_BF16], bf16)
            return [(p32, False), (p16, False)]
        if name == "inproj_bwd":
            return [(cols_parts(env["dw_in"]).astype(bf16), True)]
        return []

    def after(name, got):
        if name == "norm_inproj":
            env["w_out"], env["w_fo"] = got[0].reshape(D, D), got[1].reshape(D_FF, D)
        elif name == "lru_fwd":
            env["w_fi"] = cols_full(got[0])
        elif name == "ffn_bwd1":
            recv["w_ffn_out"] = got[0]
        elif name == "attn_bwd":
            recv["w_out"] = got[0]
        elif name == "lru_bwd":
            recv["w_ffn_in"] = got[0]
        elif name == "conv_bwd":
            recv["early_f32"], recv["early_bf16"] = got
        elif name == "inproj_bwd":
            recv["w_in"] = got[0]

    grad_x, grads = _local_step(x[0], loss_target[0], small, env, before, after)

    outs = {}
    for name in BIG:
        shard_shape = recv[name].shape[1:]
        r2 = lambda a: a.reshape(shard_shape)
        res = _adamw(recv[name], r2(w[name]), r2(m[name]), r2(v[name]), "adamw_" + name)
        outs[name] = [t.reshape(w[name].shape) for t in res]

    small_names = SMALL_REPL + SMALL_SHARD
    late_packed, late_spans = _pack_rows([grads[n] for n in LATE])
    (got_late,) = _exchange([(late_packed, False)], "gather_late_grads")
    summed = {}
    for names, got, spans, tag in ((EARLY_F32, recv["early_f32"], env["early_f32_spans"], "early_f32"),
                                   (EARLY_BF16, recv["early_bf16"], env["early_bf16_spans"], "early_bf16"),
                                   (LATE, got_late, late_spans, "late")):
        total = _sum_parts(got, "sum_small_" + tag)
        summed.update(zip(names, _unpack_rows(total, spans, [grads[n].shape for n in names])))
    loss = summed["loss"].reshape(())
    gsm = {n: summed[n].reshape(w[n].shape) for n in SMALL_REPL}
    for n in SMALL_SHARD:
        full = summed[n]
        gsm[n] = lax.dynamic_slice_in_dim(full, me * 128, 128, axis=1).reshape(w[n].shape)
    pk = lambda dct: _pack_rows([dct[n] for n in small_names])[0]
    gp, sp = _pack_rows([gsm[n] for n in small_names])
    res = _adamw(gp[None], pk(w), pk(m), pk(v), "adamw_small")
    sshapes = [w[n].shape for n in small_names]
    for idx, t in enumerate(res):
        for n, a in zip(small_names, _unpack_rows(t, sp, sshapes)):
            outs.setdefault(n, [None] * 4)[idx] = a

    result = [loss, grad_x[None]]
    for idx in range(4):
        result += [outs[n][idx] for n in ORDER]
    return tuple(result)
```

```python
import functools
import math

import jax
import jax.numpy as jnp
from jax import lax
from jax.experimental import pallas as pl
from jax.experimental.pallas import tpu as pltpu

f32 = jnp.float32
bf16 = jnp.bfloat16

D = 1024
D_FF = 2816
IN_W = 5632
N_HEADS = 16
N_KV = 4
HEAD_DIM = 64
WINDOW = 128
BLK = 128
LRU_HEADS = 16
LRU_BLOCK = 64
LRU_GROUPS = 4
LRU_GW = 256
LRU_CHUNK = 128
LRU_ROWS = 512
RGLRU_C = 8.0
EPS = 1e-6
NEG_INF = -1e30
N_DEV = 8

ADAM_LR = 0.001
ADAM_B1 = 0.9
ADAM_B2 = 0.999
ADAM_EPS = 1e-08
ADAM_WD = 0.01
ADAM_STEP = 10

VMEM_MB = 56

C_U, C_G, C_Q, C_Z0, C_Z1, C_K, C_V = 0, 1024, 2048, 3072, 4096, 5120, 5376


def _cparams(vmem_mb=VMEM_MB):
    return pltpu.CompilerParams(vmem_limit_bytes=vmem_mb << 20)


def _div_tile(n, pref):
    if n <= pref:
        return n
    return max(t for t in range(8, pref + 1, 8) if n % t == 0)


def _perm_cols(w):
    return jnp.concatenate([w[:, :3072], w[:, 3584:5632], w[:, 3072:3584]], axis=1)


def _unperm_cols(w):
    return jnp.concatenate([w[:, :3072], w[:, 5120:5632], w[:, 3072:5120]], axis=1)


def _sigmoid(x):
    return 1.0 / (1.0 + jnp.exp(-x))


def _expm1(x):
    p = x * (1.0 + x * (0.5 + x * (1.0 / 6 + x * (1.0 / 24 + x * (1.0 / 120 + x * (1.0 / 720))))))
    return jnp.where(jnp.abs(x) < 0.3, p, jnp.exp(x) - 1.0)


def _log1p(x):
    u = 1.0 + x
    d = u - 1.0
    return jnp.where(d == 0.0, x, jnp.log(u) * (x / jnp.where(d == 0.0, 1.0, d)))


def _softplus(x):
    return jnp.maximum(x, 0.0) + _log1p(jnp.exp(-jnp.abs(x)))


def _gelu_and_grad(x):
    c = math.sqrt(2.0 / math.pi)
    inner = c * (x + 0.044715 * (x * x * x))
    t = jnp.tanh(inner)
    gelu = 0.5 * x * (1.0 + t)
    dinner = c * (1.0 + 3 * 0.044715 * (x * x))
    dgelu = 0.5 * (1.0 + t) + 0.5 * x * (1.0 - t * t) * dinner
    return gelu, dgelu


def _rms_bwd(dn, xv, g):
    r = lax.rsqrt(jnp.mean(xv * xv, axis=-1, keepdims=True) + EPS)
    xh = xv * r
    dxh = dn * g
    dx = r * (dxh - xh * jnp.mean(dxh * xh, axis=-1, keepdims=True))
    return dx, dn * xh


ANY_SPEC = pl.BlockSpec(memory_space=pl.ANY)


def _comm_out_shape(src, scatter):
    return jax.ShapeDtypeStruct((N_DEV, *(src.shape[1:] if scatter else src.shape)), src.dtype)


def _comm_sems():
    return [pltpu.SemaphoreType.DMA((N_DEV - 1,)), pltpu.SemaphoreType.DMA((N_DEV - 1,)), pltpu.SemaphoreType.DMA]


def _comm_descs(src_ref, out_ref, send_sems, recv_sems, local_sem, scatter):
    x, y, c = lax.axis_index("x"), lax.axis_index("y"), lax.axis_index("c")
    me = 4 * x + 2 * y + c
    descs = [pltpu.make_async_copy(src_ref.at[me] if scatter else src_ref, out_ref.at[me], local_sem)]
    for k in range(1, N_DEV):
        px, py, pc = x ^ (k >> 2), y ^ ((k >> 1) & 1), c ^ (k & 1)
        peer = 4 * px + 2 * py + pc
        descs.append(pltpu.make_async_remote_copy(
            src_ref=src_ref.at[peer] if scatter else src_ref, dst_ref=out_ref.at[me],
            send_sem=send_sems.at[k - 1], recv_sem=recv_sems.at[k - 1],
            device_id=(px, py, pc), device_id_type=pl.DeviceIdType.MESH))
    return descs


def _exchange(comm, name):
    nc = len(comm)

    def body(*refs):
        srcs, outs, sems = refs[:nc], refs[nc:2 * nc], refs[2 * nc:]
        descs = [d for i in range(nc) for d in _comm_descs(srcs[i], outs[i], *sems[3 * i:3 * i + 3], comm[i][1])]
        for d in descs:
            d.start()
        for d in descs:
            d.wait()

    return pl.pallas_call(
        body, name=name, in_specs=[ANY_SPEC] * nc, out_specs=[ANY_SPEC] * nc,
        out_shape=[_comm_out_shape(*c) for c in comm],
        scratch_shapes=[s for _ in comm for s in _comm_sems()],
    )(*[c[0] for c in comm])


def _hosted_call(body, *, name, grid, in_specs, out_specs, out_shape, args, scratch_shapes=(), comm=()):
    nin, nout, nscr, nc = len(in_specs), len(out_specs), len(scratch_shapes), len(comm)

    def wrapped(*refs):
        ins = refs[:nin]
        csrc = refs[nin:nin + nc]
        outs = refs[nin + nc:nin + nc + nout]
        cout = refs[nin + nc + nout:nin + 2 * nc + nout]
        scr = refs[nin + 2 * nc + nout:]
        sems = scr[nscr:]

        def descs():
            return [d for i in range(nc) for d in _comm_descs(csrc[i], cout[i], *sems[3 * i:3 * i + 3], comm[i][1])]

        if nc:
            first = functools.reduce(jnp.logical_and, [pl.program_id(a) == 0 for a in range(len(grid))])

            @pl.when(first)
            def _():
                for d in descs():
                    d.start()

        body(*ins, *outs, *scr[:nscr])

        if nc:
            last = functools.reduce(jnp.logical_and, [pl.program_id(a) == grid[a] - 1 for a in range(len(grid))])

            @pl.when(last)
            def _():
                for d in descs():
                    d.wait()

    res = pl.pallas_call(
        wrapped, name=name, grid=grid,
        in_specs=[*in_specs, *[ANY_SPEC] * nc], out_specs=[*out_specs, *[ANY_SPEC] * nc],
        out_shape=[*out_shape, *[_comm_out_shape(*c) for c in comm]],
        scratch_shapes=[*scratch_shapes, *[s for _ in comm for s in _comm_sems()]],
        compiler_params=_cparams())(*args, *[c[0] for c in comm])
    return res[:nout], res[nout:]


def _norm_matmul(x, g, w, name, tm=1024, tn=1408, comm=()):
    S, dm = x.shape
    n = w.shape[1]
    tm = min(tm, S)

    def body(x_ref, g_ref, w_ref, xn_ref, o_ref):
        @pl.when(pl.program_id(1) == 0)
        def _():
            xv = x_ref[...]
            r = lax.rsqrt(jnp.mean(xv * xv, axis=-1, keepdims=True) + EPS)
            xn_ref[...] = ((xv * r) * g_ref[...]).astype(bf16)

        o_ref[...] = jnp.dot(xn_ref[...], w_ref[...], preferred_element_type=f32).astype(bf16)

    return _hosted_call(
        body, name=name, grid=(S // tm, n // tn),
        in_specs=[pl.BlockSpec((tm, dm), lambda i, j: (i, 0)),
                  pl.BlockSpec((1, dm), lambda i, j: (0, 0)),
                  pl.BlockSpec((dm, tn), lambda i, j: (0, j))],
        out_specs=[pl.BlockSpec((tm, dm), lambda i, j: (i, 0)),
                   pl.BlockSpec((tm, tn), lambda i, j: (i, j))],
        out_shape=[jax.ShapeDtypeStruct((S, dm), bf16), jax.ShapeDtypeStruct((S, n), bf16)],
        args=(x, g, w), comm=comm)


def _mm_tn(a, b, name, tk, tn, tmc=2048):
    m, ka = a.shape
    n = b.shape[1]
    tmc = min(tmc, m)

    def body(a_ref, b_ref, o_ref):
        @pl.when(pl.program_id(2) == 0)
        def _():
            o_ref[...] = jnp.zeros_like(o_ref)

        o_ref[...] += lax.dot_general(a_ref[...], b_ref[...], (((0,), (0,)), ((), ())),
                                      preferred_element_type=f32)

    return pl.pallas_call(
        body, name=name, grid=(ka // tk, n // tn, m // tmc),
        in_specs=[pl.BlockSpec((tmc, tk), lambda i, j, k: (k, i)),
                  pl.BlockSpec((tmc, tn), lambda i, j, k: (k, j))],
        out_specs=pl.BlockSpec((tk, tn), lambda i, j, k: (i, j)),
        out_shape=jax.ShapeDtypeStruct((ka, n), f32),
        compiler_params=_cparams())(a, b)


HALO = 16


def _rows_at(ext, o, tc):
    if o == 0:
        return ext[HALO:HALO + tc]
    return pltpu.roll(ext, (-o) % ext.shape[0], 0)[HALO:HALO + tc]


def _halo_specs(tc, S, width, col):
    per = tc // HALO
    last = S // HALO - 1
    return (pl.BlockSpec((tc, width), lambda i: (i, col)),
            pl.BlockSpec((HALO, width), lambda i: (jnp.maximum(i * per - 1, 0), col)),
            pl.BlockSpec((HALO, width), lambda i: (jnp.minimum((i + 1) * per, last), col)))


def _extended(cur_ref, prev_ref, next_ref, i, nsteps):
    prev = jnp.where(i > 0, prev_ref[...].astype(f32), 0.0)
    nxt = jnp.where(i < nsteps - 1, next_ref[...].astype(f32), 0.0)
    return jnp.concatenate([prev, cur_ref[...].astype(f32), nxt], axis=0)


def _conv_fwd(proj, cw, cb, tc=512):
    S = proj.shape[0]
    tc = min(tc, S)
    nsteps = S // tc

    def body(cur_ref, prev_ref, next_ref, w_ref, b_ref, o_ref):
        ext = _extended(cur_ref, prev_ref, next_ref, pl.program_id(0), nsteps)
        acc = _rows_at(ext, -2, tc) * w_ref[0:1, :]
        for k in range(1, 4):
            acc = acc + _rows_at(ext, k - 2, tc) * w_ref[k:k + 1, :]
        o_ref[...] = acc + b_ref[...]

    return pl.pallas_call(
        body, name="conv_fwd", grid=(nsteps,),
        in_specs=[*_halo_specs(tc, S, D, 0),
                  pl.BlockSpec((4, D), lambda i: (0, 0)), pl.BlockSpec((1, D), lambda i: (0, 0))],
        out_specs=pl.BlockSpec((tc, D), lambda i: (i, 0)),
        out_shape=jax.ShapeDtypeStruct((S, D), f32),
        compiler_params=_cparams())(proj, proj, proj, cw, cb)


def _conv_bwd(duc_f, duc_b, proj, cw, tc=512, comm=()):
    S = proj.shape[0]
    tc = min(tc, S)
    nsteps = S // tc

    def body(fc, fp, fn, bc, bp, bn, uc_, up, un, w_ref, du_ref, dw_ref, db_ref):
        i = pl.program_id(0)

        @pl.when(i == 0)
        def _():
            dw_ref[...] = jnp.zeros_like(dw_ref)
            db_ref[...] = jnp.zeros_like(db_ref)

        dext = _extended(fc, fp, fn, i, nsteps) + _extended(bc, bp, bn, i, nsteps)
        uext = _extended(uc_, up, un, i, nsteps)
        d = dext[HALO:HALO + tc]
        acc = _rows_at(dext, 2, tc) * w_ref[0:1, :]
        for k in range(1, 4):
            acc = acc + _rows_at(dext, 2 - k, tc) * w_ref[k:k + 1, :]
        du_ref[...] = acc.astype(bf16)
        wrow = lax.broadcasted_iota(jnp.int32, (4, D), 0)
        for k in range(4):
            dw_ref[...] += jnp.where(wrow == k, jnp.sum(d * _rows_at(uext, k - 2, tc), axis=0, keepdims=True), 0.0)
        db_ref[...] += jnp.sum(d, axis=0, keepdims=True)

    return _hosted_call(
        body, name="conv_bwd", grid=(nsteps,),
        in_specs=[*_halo_specs(tc, S, D, 0), *_halo_specs(tc, S, D, 0), *_halo_specs(tc, S, D, 0),
                  pl.BlockSpec((4, D), lambda i: (0, 0))],
        out_specs=[pl.BlockSpec((tc, D), lambda i: (i, 0)),
                   pl.BlockSpec((4, D), lambda i: (0, 0)), pl.BlockSpec((1, D), lambda i: (0, 0))],
        out_shape=[jax.ShapeDtypeStruct((S, D), bf16), jax.ShapeDtypeStruct((4, D), f32),
                   jax.ShapeDtypeStruct((1, D), f32)],
        args=(duc_f, duc_f, duc_f, duc_b, duc_b, duc_b, proj, proj, proj, cw), comm=comm)


def _scan_scratch():
    halves = [pltpu.VMEM((LRU_CHUNK, 128), f32) for _ in range(2 * (LRU_GW // 128))]
    return [*halves, pltpu.VMEM((LRU_CHUNK // 8, LRU_GW), f32)]


def _log_scan(a, b, row, n, reverse, steps):
    for s in steps:
        shift = a.shape[0] - s if reverse else s
        keep = (row < n - s) if reverse else (row >= s)
        a_sh = pltpu.roll(a, shift, 0)
        b_sh = pltpu.roll(b, shift, 0)
        b = jnp.where(keep, a * b_sh + b, b)
        a = jnp.where(keep, a * a_sh, a)
    return a, b


def _scan_chunk(a, b, carry, reverse, *scratch):
    tc, w = a.shape
    ng = tc // 8
    nl = w // 128
    sa_refs, sb_refs, sc_ref = scratch[:nl], scratch[nl:2 * nl], scratch[2 * nl]
    sub = lax.broadcasted_iota(jnp.int32, (8, w), 0)
    ag, bg = [], []
    for k in range(ng):
        ak, bk = _log_scan(a[8 * k:8 * k + 8], b[8 * k:8 * k + 8], sub, 8, reverse, (1, 2, 4))
        ag.append(ak)
        bg.append(bk)
    a = jnp.concatenate(ag, axis=0)
    b = jnp.concatenate(bg, axis=0)
    edge = 0 if reverse else 7
    for i in range(nl):
        sa_refs[i][...] = a[:, 128 * i:128 * (i + 1)]
        sb_refs[i][...] = b[:, 128 * i:128 * (i + 1)]
    ta = jnp.concatenate([r[pl.ds(edge, ng, stride=8), :] for r in sa_refs], axis=1)
    tb = jnp.concatenate([r[pl.ds(edge, ng, stride=8), :] for r in sb_refs], axis=1)
    grow = lax.broadcasted_iota(jnp.int32, (ng, w), 0)
    ta, tb = _log_scan(ta, tb, grow, ng, reverse, [1 << i for i in range(ng.bit_length() - 1)])
    state = tb + ta * carry
    if reverse:
        sc_ref[...] = jnp.where(grow == ng - 1, carry, pltpu.roll(state, ng - 1, 0))
    else:
        sc_ref[...] = jnp.where(grow == 0, carry, pltpu.roll(state, 1, 0))
    return jnp.concatenate([bg[k] + ag[k] * sc_ref[k:k + 1, :] for k in range(ng)], axis=0)


def _lru_gates(uc, w, p_ref):
    pre = jnp.dot(uc.astype(bf16), w, preferred_element_type=f32)
    r = _sigmoid(pre[:, :LRU_GW] + p_ref[0, 1:2, :])
    gi = _sigmoid(pre[:, LRU_GW:] + p_ref[0, 2:3, :])
    sp = _softplus(-p_ref[0, 0:1, :])
    log_a = -RGLRU_C * r * sp
    a = jnp.exp(log_a)
    beta = jnp.sqrt(jnp.maximum(-_expm1(2.0 * log_a), 0.0))
    return r, gi, sp, a, beta


def _lru_fwd(uc, wg, lp, reverse, comm=()):
    S = uc.shape[0]
    tc = LRU_CHUNK
    rows = min(LRU_ROWS, S)
    nsub = rows // tc
    nblk = S // rows
    d = 1 if reverse else 0

    def bidx(c):
        return nblk - 1 - c if reverse else c

    def body(uc_ref, w_ref, p_ref, h_ref, carry_ref, *scan_scratch):
        @pl.when(pl.program_id(1) == 0)
        def _():
            carry_ref[...] = jnp.zeros_like(carry_ref)

        carry = carry_ref[...]
        for j in (reversed(range(nsub)) if reverse else range(nsub)):
            sl = slice(j * tc, (j + 1) * tc)
            ucv = uc_ref[sl, :]
            _, gi, _, a, beta = _lru_gates(ucv, w_ref[0], p_ref)
            h_ref[sl, :] = _scan_chunk(a, beta * (gi * ucv), carry, reverse, *scan_scratch)
            carry = h_ref[j * tc:j * tc + 1, :] if reverse else h_ref[(j + 1) * tc - 1:(j + 1) * tc, :]
        carry_ref[...] = carry

    return _hosted_call(
        body, name="lru_fwd_rev" if reverse else "lru_fwd", grid=(LRU_GROUPS, nblk),
        in_specs=[pl.BlockSpec((rows, LRU_GW), lambda g, c: (bidx(c), g)),
                  pl.BlockSpec((1, LRU_GW, 2 * LRU_GW), lambda g, c: (g, 0, d)),
                  pl.BlockSpec((1, 8, LRU_GW), lambda g, c: (d, 0, g))],
        out_specs=[pl.BlockSpec((rows, LRU_GW), lambda g, c: (bidx(c), g))],
        out_shape=[jax.ShapeDtypeStruct((S, D), f32)],
        scratch_shapes=[pltpu.VMEM((1, LRU_GW), f32), *_scan_scratch()],
        args=(uc, wg, lp), comm=comm)


def _lru_bwd(uc, dh, h, wg, wgt, lp, reverse, comm=()):
    S = uc.shape[0]
    tc = LRU_CHUNK
    rows = min(LRU_ROWS, S)
    nsub = rows // tc
    nblk = S // rows
    d = 1 if reverse else 0
    per = rows // 8
    last8 = S // 8 - 1

    def bidx(c):
        return c if reverse else nblk - 1 - c

    def halo_idx(c):
        if reverse:
            return jnp.minimum((bidx(c) + 1) * per, last8)
        return jnp.maximum(bidx(c) * per - 1, 0)

    def body(uc_ref, dh_ref, h_ref, halo_ref, w_ref, wt_ref, p_ref, duc_ref, dw_ref, dp_ref, carry_ref, tmp_ref,
             *scan_scratch):
        c = pl.program_id(1)
        bi = bidx(c)

        @pl.when(c == 0)
        def _():
            carry_ref[...] = jnp.zeros_like(carry_ref)
            dw_ref[...] = jnp.zeros_like(dw_ref)
            dp_ref[...] = jnp.zeros_like(dp_ref)

        row = lax.broadcasted_iota(jnp.int32, (tc, LRU_GW), 0)
        carry = carry_ref[...]
        dw = jnp.zeros((LRU_GW, 2 * LRU_GW), f32)
        dsp = jnp.zeros((1, LRU_GW), f32)
        dba = jnp.zeros((1, LRU_GW), f32)
        dbx = jnp.zeros((1, LRU_GW), f32)
        for j in (range(nsub) if reverse else reversed(range(nsub))):
            sl = slice(j * tc, (j + 1) * tc)
            ucv = uc_ref[sl, :]
            ucb = ucv.astype(bf16)
            r, gi, sp, a, beta = _lru_gates(ucv, w_ref[0], p_ref)
            hv = h_ref[sl, :]
            if reverse:
                alpha = jnp.where(row == 0, 1.0, pltpu.roll(a, 1, 0))
                gsc = _scan_chunk(alpha, dh_ref[sl, :], carry, False, *scan_scratch)
                if j < nsub - 1:
                    edge = h_ref[(j + 1) * tc:(j + 1) * tc + 1, :]
                else:
                    edge = jnp.where(bi < nblk - 1, halo_ref[0:1, :], 0.0)
                h_nb = jnp.where(row == tc - 1, edge, pltpu.roll(hv, tc - 1, 0))
            else:
                alpha = jnp.where(row == tc - 1, 1.0, pltpu.roll(a, tc - 1, 0))
                gsc = _scan_chunk(alpha, dh_ref[sl, :], carry, True, *scan_scratch)
                if j > 0:
                    edge = h_ref[j * tc - 1:j * tc, :]
                else:
                    edge = jnp.where(bi > 0, halo_ref[7:8, :], 0.0)
                h_nb = jnp.where(row == 0, edge, pltpu.roll(hv, 1, 0))
            tmp_ref[...] = a * gsc
            carry = tmp_ref[tc - 1:tc, :] if reverse else tmp_ref[0:1, :]

            da = gsc * h_nb
            dbeta = gsc * (gi * ucv)
            dl = da * a - dbeta * (a * a) / beta
            dr = dl * (-RGLRU_C * sp)
            dsp = dsp + jnp.sum(dl * (-RGLRU_C * r), axis=0, keepdims=True)
            dgi = gsc * beta * ucv
            dpre_r = dr * r * (1.0 - r)
            dpre_i = dgi * gi * (1.0 - gi)
            dba = dba + jnp.sum(dpre_r, axis=0, keepdims=True)
            dbx = dbx + jnp.sum(dpre_i, axis=0, keepdims=True)
            dpre = jnp.concatenate([dpre_r, dpre_i], axis=1).astype(bf16)
            duc_ref[sl, :] = gsc * beta * gi + jnp.dot(dpre, wt_ref[0], preferred_element_type=f32)
            dw = dw + lax.dot_general(ucb, dpre, (((0,), (0,)), ((), ())), preferred_element_type=f32)
        carry_ref[...] = carry
        dw_ref[0] += dw
        dlam = dsp * (-_sigmoid(-p_ref[0, 0:1, :]))
        prow = lax.broadcasted_iota(jnp.int32, (8, LRU_GW), 0)
        dp_ref[...] += (jnp.where(prow == 0, dlam, 0.0) + jnp.where(prow == 1, dba, 0.0)
                        + jnp.where(prow == 2, dbx, 0.0))

    chunk = pl.BlockSpec((rows, LRU_GW), lambda g, c: (bidx(c), g))
    return _hosted_call(
        body, name="lru_bwd_rev" if reverse else "lru_bwd", grid=(LRU_GROUPS, nblk),
        in_specs=[chunk, chunk, chunk,
                  pl.BlockSpec((8, LRU_GW), lambda g, c: (halo_idx(c), g)),
                  pl.BlockSpec((1, LRU_GW, 2 * LRU_GW), lambda g, c: (g, 0, d)),
                  pl.BlockSpec((1, 2 * LRU_GW, LRU_GW), lambda g, c: (g, d, 0)),
                  pl.BlockSpec((1, 8, LRU_GW), lambda g, c: (d, 0, g))],
        out_specs=[chunk,
                   pl.BlockSpec((1, LRU_GW, 2 * LRU_GW), lambda g, c: (g, 0, 0)),
                   pl.BlockSpec((8, LRU_GW), lambda g, c: (0, g))],
        out_shape=[jax.ShapeDtypeStruct((S, D), f32),
                   jax.ShapeDtypeStruct((LRU_GROUPS, LRU_GW, 2 * LRU_GW), f32),
                   jax.ShapeDtypeStruct((8, D), f32)],
        scratch_shapes=[pltpu.VMEM((1, LRU_GW), f32), pltpu.VMEM((tc, LRU_GW), f32), *_scan_scratch()],
        args=(uc, dh, h, h, wg, wgt, lp), comm=comm)


def _slope(h):
    return 2.0 ** (-8.0 * (h + 1.0) / N_HEADS)


def _kv_specs(nb, col):
    return [pl.BlockSpec((BLK, N_KV * HEAD_DIM), lambda n: (jnp.maximum(n - 1, 0), col)),
            pl.BlockSpec((BLK, N_KV * HEAD_DIM), lambda n: (n, col)),
            pl.BlockSpec((BLK, N_KV * HEAD_DIM), lambda n: (jnp.minimum(n + 1, nb - 1), col))]


def _dup_windows(r0, r1, r2):
    left = lax.broadcasted_iota(jnp.int32, (3 * BLK, 128), 1) < HEAD_DIM
    win = jnp.concatenate([r0[...], r1[...], r2[...]], axis=0)
    out = []
    for i in range(N_KV // 2):
        t = win[:, i * 128:(i + 1) * 128]
        r = pltpu.roll(t, HEAD_DIM, 1)
        out += [jnp.where(left, t, r).astype(bf16), jnp.where(left, r, t).astype(bf16)]
    return out


def _attn_bias_init(bias_ref):
    k_loc = lax.broadcasted_iota(jnp.int32, (3 * BLK, BLK), 0)
    q_loc = lax.broadcasted_iota(jnp.int32, (3 * BLK, BLK), 1)
    adist = jnp.abs(q_loc + BLK - k_loc)
    adf = adist.astype(f32)
    for e in range(3):
        ok = adist <= WINDOW
        if e == 0:
            ok = ok & (k_loc >= BLK)
        if e == 2:
            ok = ok & (k_loc < 2 * BLK)
        for kv in range(N_KV):
            bias_ref[e, kv] = jnp.concatenate(
                [jnp.where(ok, (-_slope(4 * kv + j)) * adf, NEG_INF) for j in range(4)], axis=1)


def _stack_heads(ref, kv, scale):
    left = lax.broadcasted_iota(jnp.int32, (BLK, 128), 1) < HEAD_DIM
    rows = []
    for pp in range(2):
        t = ref[:, (2 * kv + pp) * 128:(2 * kv + pp + 1) * 128]
        if scale != 1.0:
            t = t * scale
        zero = jnp.zeros_like(t)
        rows += [jnp.where(left, t, zero).astype(bf16), jnp.where(left, zero, t).astype(bf16)]
    return jnp.concatenate(rows, axis=0)


def _attn_softmax(qs, k2, bias, sink_ref, kv):
    sink = jnp.concatenate([jnp.full((1, BLK), sink_ref[0, 4 * kv + j], f32) for j in range(4)], axis=1)
    s = lax.dot_general(k2, qs, (((1,), (1,)), ((), ())), preferred_element_type=f32) + bias
    m = jnp.maximum(jnp.max(s, axis=0, keepdims=True), sink)
    p = jnp.exp(s - m)
    ps = jnp.exp(sink - m)
    inv = 1.0 / (jnp.sum(p, axis=0, keepdims=True) + ps)
    return p, ps, inv


def _pair_tiles(t):
    return [jnp.concatenate([t[:HEAD_DIM, 256 * pp:256 * pp + 128],
                             t[HEAD_DIM:, 256 * pp + 128:256 * pp + 256]], axis=0).T for pp in range(2)]


def _attn_fwd(proj, sink, comm=()):
    S = proj.shape[0]
    nb = S // BLK
    assert nb >= 2

    def body(q_ref, k0, k1, k2_, v0, v1, v2_, sink_ref, o_ref, bias_ref):
        n = pl.program_id(0)

        @pl.when(n == 0)
        def _():
            _attn_bias_init(bias_ref)

        e = jnp.where(n == 0, 0, jnp.where(n == nb - 1, 2, 1))
        kk = _dup_windows(k0, k1, k2_)
        vv = _dup_windows(v0, v1, v2_)
        tiles = []
        for kv in range(N_KV):
            qs = _stack_heads(q_ref, kv, HEAD_DIM ** -0.5)
            p, _, inv = _attn_softmax(qs, kk[kv], bias_ref[e, kv], sink_ref, kv)
            ot = lax.dot_general(vv[kv], p.astype(bf16), (((0,), (0,)), ((), ())), preferred_element_type=f32)
            tiles += _pair_tiles(ot * inv)
        o_ref[...] = jnp.concatenate(tiles, axis=1)

    return _hosted_call(
        body, name="attn_fwd", grid=(nb,),
        in_specs=[pl.BlockSpec((BLK, D), lambda n: (n, C_Q // D)),
                  *_kv_specs(nb, C_K // (N_KV * HEAD_DIM)), *_kv_specs(nb, C_V // (N_KV * HEAD_DIM)),
                  pl.BlockSpec(memory_space=pltpu.SMEM)],
        out_specs=[pl.BlockSpec((BLK, D), lambda n: (n, 0))],
        out_shape=[jax.ShapeDtypeStruct((S, D), f32)],
        scratch_shapes=[pltpu.VMEM((3, N_KV, 3 * BLK, 4 * BLK), f32)],
        args=(proj, proj, proj, proj, proj, proj, proj, sink), comm=comm)


def _attn_bwd(proj, sink, dyb, comm=()):
    S = proj.shape[0]
    nb = S // BLK
    assert nb >= 2

    def body(q_ref, k0, k1, k2_, v0, v1, v2_, sink_ref, do_ref, dq_ref, dk_out, dv_out, ds_ref,
             bias_ref, dk_ref, dv_ref, dsk_ref):
        n = pl.program_id(0)

        @pl.when(n == 0)
        def _():
            _attn_bias_init(bias_ref)
            dk_ref[...] = jnp.zeros_like(dk_ref)
            dv_ref[...] = jnp.zeros_like(dv_ref)
            dsk_ref[...] = jnp.zeros_like(dsk_ref)

        e = jnp.where(n == 0, 0, jnp.where(n == nb - 1, 2, 1))
        kk = _dup_windows(k0, k1, k2_)
        vv = _dup_windows(v0, v1, v2_)
        left3 = lax.broadcasted_iota(jnp.int32, (3 * BLK, 128), 1) < HEAD_DIM
        start = pl.multiple_of(n * BLK, BLK)
        dq_tiles, dks, dvs = [], [], []
        for kv in range(N_KV):
            qs = _stack_heads(q_ref, kv, HEAD_DIM ** -0.5)
            dos = _stack_heads(do_ref, kv, 1.0)
            p, ps, inv = _attn_softmax(qs, kk[kv], bias_ref[e, kv], sink_ref, kv)
            pn = p * inv
            dp = lax.dot_general(vv[kv], dos, (((1,), (1,)), ((), ())), preferred_element_type=f32)
            delta = jnp.sum(pn * dp, axis=0, keepdims=True)
            dsc = (pn * (dp - delta)).astype(bf16)
            dsk_ref[kv:kv + 1, :] += delta * (ps * inv)
            dqt = lax.dot_general(kk[kv], dsc, (((0,), (0,)), ((), ())), preferred_element_type=f32)
            dq_tiles += _pair_tiles(dqt * (HEAD_DIM ** -0.5))
            dk = jnp.dot(dsc, qs, preferred_element_type=f32)
            dv = jnp.dot(pn.astype(bf16), dos, preferred_element_type=f32)
            dks.append(dk + pltpu.roll(dk, HEAD_DIM, 1))
            dvs.append(dv + pltpu.roll(dv, HEAD_DIM, 1))
        for jp in range(N_KV // 2):
            cols = slice(jp * 128, (jp + 1) * 128)
            dk_ref[pl.ds(start, 3 * BLK), cols] += jnp.where(left3, dks[2 * jp], dks[2 * jp + 1])
            dv_ref[pl.ds(start, 3 * BLK), cols] += jnp.where(left3, dvs[2 * jp], dvs[2 * jp + 1])
        dq_ref[...] = jnp.concatenate(dq_tiles, axis=1).astype(bf16)

        @pl.when(n == nb - 1)
        def _():
            pltpu.sync_copy(dk_ref, dk_out)
            pltpu.sync_copy(dv_ref, dv_out)
            lane = lax.broadcasted_iota(jnp.int32, (1, 128), 1)
            dsink = jnp.zeros((1, 128), f32)
            for h in range(N_HEADS):
                part = dsk_ref[h // 4:h // 4 + 1, (h % 4) * BLK:(h % 4 + 1) * BLK]
                dsink = dsink + jnp.where(lane == h, -jnp.sum(part), 0.0)
            ds_ref[...] = dsink

    acc = jax.ShapeDtypeStruct((S + 2 * BLK, N_KV * HEAD_DIM), f32)
    return _hosted_call(
        body, name="attn_bwd", grid=(nb,),
        in_specs=[pl.BlockSpec((BLK, D), lambda n: (n, C_Q // D)),
                  *_kv_specs(nb, C_K // (N_KV * HEAD_DIM)), *_kv_specs(nb, C_V // (N_KV * HEAD_DIM)),
                  pl.BlockSpec(memory_space=pltpu.SMEM),
                  pl.BlockSpec((BLK, D), lambda n: (n, 0))],
        out_specs=[pl.BlockSpec((BLK, D), lambda n: (n, 0)), ANY_SPEC, ANY_SPEC,
                   pl.BlockSpec((1, 128), lambda n: (0, 0))],
        out_shape=[jax.ShapeDtypeStruct((S, D), bf16), acc, acc, jax.ShapeDtypeStruct((1, 128), f32)],
        scratch_shapes=[pltpu.VMEM((3, N_KV, 3 * BLK, 4 * BLK), f32), pltpu.VMEM(acc.shape, f32),
                        pltpu.VMEM(acc.shape, f32), pltpu.VMEM((8, 4 * BLK), f32)],
        args=(proj, proj, proj, proj, proj, proj, proj, sink, dyb), comm=comm)


def _merge_parts(hf, hb, g, z0, z1, yb, bg):
    g0 = _sigmoid(z0.astype(f32) + bg[:, :D])
    g1 = _sigmoid(z1.astype(f32) + bg[:, D:])
    gelu, dgelu = _gelu_and_grad(g.astype(f32))
    hs = hf + hb
    ya = hs * gelu
    return g0, g1, gelu, dgelu, hs, ya


def _merge_outproj(x, hf, hb, proj, yb, bg, w_out, tm=512):
    S = x.shape[0]
    tm = min(tm, S)

    def body(x_ref, hf_ref, hb_ref, g_ref, z0_ref, z1_ref, yb_ref, bg_ref, w_ref, mg_ref, x1_ref):
        ybv = yb_ref[...]
        g0, g1, _, _, _, ya = _merge_parts(hf_ref[...], hb_ref[...], g_ref[...], z0_ref[...], z1_ref[...],
                                           ybv, bg_ref[...])
        mg = (g0 * ya + g1 * ybv).astype(bf16)
        mg_ref[...] = mg
        x1_ref[...] = x_ref[...] + jnp.dot(mg, w_ref[...], preferred_element_type=f32)

    row = pl.BlockSpec((tm, D), lambda i: (i, 0))
    return pl.pallas_call(
        body, name="merge_outproj", grid=(S // tm,),
        in_specs=[row, row, row,
                  pl.BlockSpec((tm, D), lambda i: (i, C_G // D)),
                  pl.BlockSpec((tm, D), lambda i: (i, C_Z0 // D)),
                  pl.BlockSpec((tm, D), lambda i: (i, C_Z1 // D)),
                  row, pl.BlockSpec((1, 2 * D), lambda i: (0, 0)), pl.BlockSpec((D, D), lambda i: (0, 0))],
        out_specs=[row, row],
        out_shape=[jax.ShapeDtypeStruct((S, D), bf16), jax.ShapeDtypeStruct((S, D), f32)],
        compiler_params=_cparams())(x, hf, hb, proj, proj, proj, yb, bg, w_out)


def _ffn_out_loss(gu, x1, w_fo, g3, tgt, tm=256):
    S = x1.shape[0]
    tm = min(tm, S)

    def body(gt_ref, up_ref, x1_ref, w_ref, g_ref, t_ref, ff_ref, dx_ref, dxb_ref, loss_ref, dg_ref):
        @pl.when(pl.program_id(0) == 0)
        def _():
            loss_ref[...] = jnp.zeros_like(loss_ref)
            dg_ref[...] = jnp.zeros_like(dg_ref)

        gt = gt_ref[...].astype(f32)
        ff = ((gt * _sigmoid(gt)) * up_ref[...].astype(f32)).astype(bf16)
        ff_ref[...] = ff
        x2 = x1_ref[...] + jnp.dot(ff, w_ref[...], preferred_element_type=f32)
        gv = g_ref[...]
        r = lax.rsqrt(jnp.mean(x2 * x2, axis=-1, keepdims=True) + EPS)
        xh = x2 * r
        diff = xh * gv - t_ref[...]
        loss_ref[...] += (0.5 / D) * jnp.sum(diff * diff)
        dy = diff * (1.0 / D)
        dg_ref[...] += jnp.sum(dy * xh, axis=0, keepdims=True)
        dxh = dy * gv
        dx = r * (dxh - xh * jnp.mean(dxh * xh, axis=-1, keepdims=True))
        dx_ref[...] = dx
        dxb_ref[...] = dx.astype(bf16)

    row = pl.BlockSpec((tm, D), lambda i: (i, 0))
    vec = pl.BlockSpec((1, D), lambda i: (0, 0))
    return pl.pallas_call(
        body, name="ffn_out_loss", grid=(S // tm,),
        in_specs=[pl.BlockSpec((tm, D_FF), lambda i: (i, 0)), pl.BlockSpec((tm, D_FF), lambda i: (i, 1)),
                  row, pl.BlockSpec((D_FF, D), lambda i: (0, 0)), vec, row],
        out_specs=[pl.BlockSpec((tm, D_FF), lambda i: (i, 0)), row, row,
                   pl.BlockSpec((1, 128), lambda i: (0, 0)), vec],
        out_shape=[jax.ShapeDtypeStruct((S, D_FF), bf16), jax.ShapeDtypeStruct((S, D), f32),
                   jax.ShapeDtypeStruct((S, D), bf16), jax.ShapeDtypeStruct((1, 128), f32),
                   jax.ShapeDtypeStruct((1, D), f32)],
        compiler_params=_cparams())(gu, gu, x1, w_fo, g3, tgt)


def _ffn_bwd1(dx2b, w_fot, gu, tm=256, comm=()):
    S = dx2b.shape[0]
    tm = min(tm, S)

    def body(dx_ref, w_ref, gt_ref, up_ref, dgt_ref, dup_ref):
        dff = jnp.dot(dx_ref[...], w_ref[...], preferred_element_type=f32)
        gt = gt_ref[...].astype(f32)
        sg = _sigmoid(gt)
        dup_ref[...] = (dff * (gt * sg)).astype(bf16)
        dgt_ref[...] = ((dff * up_ref[...].astype(f32)) * (sg * (1.0 + gt * (1.0 - sg)))).astype(bf16)

    wide = pl.BlockSpec((tm, D_FF), lambda i: (i, 0))
    return _hosted_call(
        body, name="ffn_bwd1", grid=(S // tm,),
        in_specs=[pl.BlockSpec((tm, D), lambda i: (i, 0)), pl.BlockSpec((D, D_FF), lambda i: (0, 0)),
                  wide, pl.BlockSpec((tm, D_FF), lambda i: (i, 1))],
        out_specs=[wide, wide],
        out_shape=[jax.ShapeDtypeStruct((S, D_FF), bf16), jax.ShapeDtypeStruct((S, D_FF), bf16)],
        args=(dx2b, w_fot, gu, gu), comm=comm)


def _proj_bwd(pieces, wts, xres, g, dres, name, tm=256, comm=()):
    S = xres.shape[0]
    tm = min(tm, S)
    np_ = len(pieces)

    def body(*refs):
        p_refs = refs[:np_]
        w_refs = refs[np_:2 * np_]
        x_ref, g_ref, dres_ref, dx_ref, dxb_ref, dg_ref = refs[2 * np_:]

        @pl.when(pl.program_id(0) == 0)
        def _():
            dg_ref[...] = jnp.zeros_like(dg_ref)

        dn = jnp.dot(p_refs[0][...], w_refs[0][...], preferred_element_type=f32)
        for pr, wr in zip(p_refs[1:], w_refs[1:]):
            dn = dn + jnp.dot(pr[...], wr[...], preferred_element_type=f32)
        dxn, dgc = _rms_bwd(dn, x_ref[...], g_ref[...])
        dx = dres_ref[...] + dxn
        dx_ref[...] = dx
        dxb_ref[...] = dx.astype(bf16)
        dg_ref[...] += jnp.sum(dgc, axis=0, keepdims=True)

    row = pl.BlockSpec((tm, D), lambda i: (i, 0))
    vec = pl.BlockSpec((1, D), lambda i: (0, 0))
    return _hosted_call(
        body, name=name, grid=(S // tm,),
        in_specs=[*[pl.BlockSpec((tm, p.shape[1]), lambda i: (i, 0)) for p in pieces],
                  *[pl.BlockSpec(w.shape, lambda i: (0, 0)) for w in wts],
                  row, vec, row],
        out_specs=[row, row, vec],
        out_shape=[jax.ShapeDtypeStruct((S, D), f32), jax.ShapeDtypeStruct((S, D), bf16),
                   jax.ShapeDtypeStruct((1, D), f32)],
        args=(*pieces, *wts, xres, g, dres), comm=comm)


def _outproj_bwd(dx1b, w_outt, hf, hb, proj, yb, bg, tm=512):
    S = dx1b.shape[0]
    tm = min(tm, S)

    def body(dx_ref, w_ref, hf_ref, hb_ref, g_ref, z0_ref, z1_ref, yb_ref, bg_ref,
             dh_ref, dg_ref, dz_ref, dyb_ref, dbg_ref):
        @pl.when(pl.program_id(0) == 0)
        def _():
            dbg_ref[...] = jnp.zeros_like(dbg_ref)

        dm = jnp.dot(dx_ref[...], w_ref[...], preferred_element_type=f32)
        ybv = yb_ref[...]
        g0, g1, gelu, dgelu, hs, ya = _merge_parts(hf_ref[...], hb_ref[...], g_ref[...], z0_ref[...],
                                                   z1_ref[...], ybv, bg_ref[...])
        dya = dm * g0
        dh_ref[...] = dya * gelu
        dg_ref[...] = (dya * hs * dgelu).astype(bf16)
        dyb_ref[...] = (dm * g1).astype(bf16)
        dz0 = (dm * ya) * (g0 * (1.0 - g0))
        dz1 = (dm * ybv) * (g1 * (1.0 - g1))
        dz = jnp.concatenate([dz0, dz1], axis=1)
        dz_ref[...] = dz.astype(bf16)
        dbg_ref[...] += jnp.sum(dz, axis=0, keepdims=True)

    row = pl.BlockSpec((tm, D), lambda i: (i, 0))
    return pl.pallas_call(
        body, name="outproj_bwd", grid=(S // tm,),
        in_specs=[row, pl.BlockSpec((D, D), lambda i: (0, 0)), row, row,
                  pl.BlockSpec((tm, D), lambda i: (i, C_G // D)),
                  pl.BlockSpec((tm, D), lambda i: (i, C_Z0 // D)),
                  pl.BlockSpec((tm, D), lambda i: (i, C_Z1 // D)),
                  row, pl.BlockSpec((1, 2 * D), lambda i: (0, 0))],
        out_specs=[row, row, pl.BlockSpec((tm, 2 * D), lambda i: (i, 0)), row,
                   pl.BlockSpec((1, 2 * D), lambda i: (0, 0))],
        out_shape=[jax.ShapeDtypeStruct((S, D), f32), jax.ShapeDtypeStruct((S, D), bf16),
                   jax.ShapeDtypeStruct((S, 2 * D), bf16), jax.ShapeDtypeStruct((S, D), bf16),
                   jax.ShapeDtypeStruct((1, 2 * D), f32)],
        compiler_params=_cparams())(dx1b, w_outt, hf, hb, proj, proj, proj, yb, bg)


def _block_diag_groups(w):
    w4 = w.reshape(LRU_GROUPS, 4, LRU_BLOCK, LRU_BLOCK)
    eye = jnp.eye(4, dtype=w.dtype)
    return jnp.einsum("ghij,hk->ghikj", w4, eye).reshape(LRU_GROUPS, LRU_GW, LRU_GW)


def _diag_blocks(dw):
    d5 = dw.reshape(LRU_GROUPS, 4, LRU_BLOCK, 4, LRU_BLOCK)
    return jnp.stack([d5[:, h, :, h, :] for h in range(4)], axis=1).reshape(LRU_HEADS, LRU_BLOCK, LRU_BLOCK)


def _local_step(x, tgt, small, env, before=lambda name: (), after=lambda name, got: None):
    S = x.shape[0]
    g1, g2, g3 = small["norm_mix_g"], small["norm_ffn_g"], small["norm_final_g"]
    bg, cw, cb = small["b_gate"], small["conv_w"], small["conv_b"]
    sink = small["attn_sink"]

    wg = jnp.concatenate([_block_diag_groups(small["lru_wa"][0]), _block_diag_groups(small["lru_wx"][0]),
                          _block_diag_groups(small["lru_wa"][1]), _block_diag_groups(small["lru_wx"][1])],
                         axis=2).astype(bf16)
    wgt = jnp.swapaxes(wg, 1, 2)
    zeros5 = jnp.zeros((5, D), f32)
    lp = jnp.stack([jnp.concatenate([small["lru_lambda"][d:d + 1], small["lru_ba"][d:d + 1],
                                     small["lru_bx"][d:d + 1], zeros5], axis=0) for d in range(2)])

    def hosted(name, fn, *args, **kw):
        outs, got = fn(*args, comm=tuple(before(name)), **kw)
        after(name, got)
        return outs

    xn, proj = hosted("norm_inproj", _norm_matmul, x, g1, env["w_in_p"], "norm_inproj")
    uc = _conv_fwd(proj, cw, cb)
    (hf,) = hosted("lru_fwd", _lru_fwd, uc, wg, lp, False)
    (hb,), _ = _lru_fwd(uc, wg, lp, True)
    (yb,) = hosted("attn_fwd", _attn_fwd, proj, sink)
    merged, x1 = _merge_outproj(x, hf, hb, proj, yb, bg, env["w_out"])
    (xn2, gu), _ = _norm_matmul(x1, g2, env["w_fi"], "norm_ffn_in")
    ff, dx2, dx2b, loss, dg3 = _ffn_out_loss(gu, x1, env["w_fo"], g3, tgt)

    env["dw_fo"] = _mm_tn(ff, dx2b, "dw_ffn_out", tk=1408, tn=1024)
    dgt, dup = hosted("ffn_bwd1", _ffn_bwd1, dx2b, env["w_fo"].T, gu)
    env["dw_fi"] = jnp.concatenate([_mm_tn(xn2, dgt, "dw_ffn_in_gate", tk=1024, tn=1408),
                                    _mm_tn(xn2, dup, "dw_ffn_in_up", tk=1024, tn=1408)], axis=1)
    w_fit = env["w_fi"].T
    (dx1, dx1b, dg2), _ = _proj_bwd([dgt, dup], [w_fit[:D_FF], w_fit[D_FF:]], x1, g2, dx2, "ffn_in_bwd")
    env["dw_out"] = _mm_tn(merged, dx1b, "dw_out", tk=1024, tn=1024)
    dh, dgl, dz, dyb, dbg = _outproj_bwd(dx1b, env["w_out"].T, hf, hb, proj, yb, bg)
    dq, dk2, dv2, dsink = hosted("attn_bwd", _attn_bwd, proj, sink, dyb)
    dkv = jnp.concatenate([dk2[BLK:BLK + S], dv2[BLK:BLK + S]], axis=1).astype(bf16)
    duc_f, dwg_f, dp_f = hosted("lru_bwd", _lru_bwd, uc, dh, hf, wg, wgt, lp, False)
    (duc_b, dwg_b, dp_b), _ = _lru_bwd(uc, dh, hb, wg, wgt, lp, True)
    env["grads_early"] = {
        "loss": loss[:, :1], "b_gate": dbg,
        "lru_lambda": jnp.concatenate([dp_f[0:1], dp_b[0:1]], axis=0),
        "lru_wa": jnp.stack([_diag_blocks(dwg_f[:, :, :LRU_GW]), _diag_blocks(dwg_b[:, :, :LRU_GW])]),
        "lru_ba": jnp.concatenate([dp_f[1:2], dp_b[1:2]], axis=0),
        "lru_wx": jnp.stack([_diag_blocks(dwg_f[:, :, LRU_GW:]), _diag_blocks(dwg_b[:, :, LRU_GW:])]),
        "lru_bx": jnp.concatenate([dp_f[2:3], dp_b[2:3]], axis=0),
        "attn_sink": dsink[:, :N_HEADS], "norm_ffn_g": dg2, "norm_final_g": dg3,
    }
    du, dcw, dcb = hosted("conv_bwd", _conv_bwd, duc_f, duc_b, proj, cw)
    pieces = [du, dgl, dq, dz, dkv]
    bounds = [0, 1024, 2048, 3072, 5120, 5632]
    env["dw_in"] = _unperm_cols(jnp.concatenate(
        [_mm_tn(xn, p, "dw_in_%d" % i, tk=1024, tn=min(p.shape[1], 1024)) for i, p in enumerate(pieces)], axis=1))
    w_int = env["w_in_p"].T
    dx, _, dg1 = hosted("inproj_bwd", _proj_bwd, pieces, [w_int[bounds[i]:bounds[i + 1]] for i in range(5)],
                        x, g1, dx1, "inproj_bwd")

    grads = dict(env["grads_early"], norm_mix_g=dg1, conv_w=dcw, conv_b=dcb)
    return dx, grads


def _adamw(gparts, w, m, v, name, tr=256):
    n, rows, cols = gparts.shape
    tr = _div_tile(rows, tr)
    c1 = 1.0 - ADAM_B1 ** ADAM_STEP
    c2 = 1.0 - ADAM_B2 ** ADAM_STEP

    def body(g_ref, w_ref, m_ref, v_ref, go_ref, d_ref, mo_ref, vo_ref):
        g = g_ref[0].astype(f32)
        for j in range(1, n):
            g = g + g_ref[j].astype(f32)
        mn = ADAM_B1 * m_ref[...] + (1.0 - ADAM_B1) * g
        vn = ADAM_B2 * v_ref[...] + (1.0 - ADAM_B2) * (g * g)
        m_hat = mn / c1
        v_hat = vn / c2
        go_ref[...] = g
        d_ref[...] = -ADAM_LR * (m_hat / (jnp.sqrt(v_hat) + ADAM_EPS) + ADAM_WD * w_ref[...])
        mo_ref[...] = mn
        vo_ref[...] = vn

    blk = pl.BlockSpec((tr, cols), lambda i: (i, 0))
    shp = jax.ShapeDtypeStruct((rows, cols), f32)
    return pl.pallas_call(
        body, name=name, grid=(rows // tr,),
        in_specs=[pl.BlockSpec((n, tr, cols), lambda i: (0, i, 0)), blk, blk, blk],
        out_specs=[blk, blk, blk, blk], out_shape=[shp, shp, shp, shp],
        compiler_params=_cparams())(gparts, w, m, v)


def _sum_parts(parts, name):
    n, rows, cols = parts.shape

    def body(p_ref, o_ref):
        acc = p_ref[0].astype(f32)
        for j in range(1, n):
            acc = acc + p_ref[j].astype(f32)
        o_ref[...] = acc

    return pl.pallas_call(
        body, name=name, out_shape=jax.ShapeDtypeStruct((rows, cols), f32),
        compiler_params=_cparams())(parts)


def _pack_rows(arrs, dtype=f32):
    rows, spans, at = [], [], 0
    for a in arrs:
        flat = a.reshape(-1).astype(dtype)
        nr = -(-flat.shape[0] // 1024)
        rows.append(jnp.pad(flat, (0, nr * 1024 - flat.shape[0])).reshape(nr, 1024))
        spans.append((at, nr))
        at += nr
    pad = (-at) % 16
    if pad:
        rows.append(jnp.zeros((pad, 1024), dtype))
    return jnp.concatenate(rows, axis=0), spans


def _unpack_rows(packed, spans, shapes):
    out = []
    for (at, nr), shp in zip(spans, shapes):
        n = math.prod(shp)
        out.append(packed[at:at + nr].reshape(-1)[:n].reshape(shp))
    return out


BIG = ("w_in", "w_out", "w_ffn_in", "w_ffn_out")
SMALL_REPL = ("norm_mix_g", "b_gate", "conv_b", "lru_wa", "lru_wx", "attn_sink", "norm_ffn_g", "norm_final_g")
SMALL_SHARD = ("conv_w", "lru_lambda", "lru_ba", "lru_bx")
ORDER = ("norm_mix_g", "w_in", "b_gate", "conv_w", "conv_b", "lru_lambda", "lru_wa", "lru_ba", "lru_wx",
         "lru_bx", "attn_sink", "w_out", "norm_ffn_g", "w_ffn_in", "w_ffn_out", "norm_final_g")
EARLY_F32 = ("loss", "b_gate", "lru_lambda", "lru_ba", "lru_bx", "attn_sink", "norm_ffn_g", "norm_final_g")
EARLY_BF16 = ("lru_wa", "lru_wx")
LATE = ("norm_mix_g", "conv_w", "conv_b")


def kernel(x, norm_mix_g, w_in, b_gate, conv_w, conv_b, lru_lambda, lru_wa, lru_ba, lru_wx, lru_bx, attn_sink, w_out, norm_ffn_g, w_ffn_in, w_ffn_out, norm_final_g, loss_target, m_norm_mix_g, m_w_in, m_b_gate, m_conv_w, m_conv_b, m_lru_lambda, m_lru_wa, m_lru_ba, m_lru_wx, m_lru_bx, m_attn_sink, m_w_out, m_norm_ffn_g, m_w_ffn_in, m_w_ffn_out, m_norm_final_g, v_norm_mix_g, v_w_in, v_b_gate, v_conv_w, v_conv_b, v_lru_lambda, v_lru_wa, v_lru_ba, v_lru_wx, v_lru_bx, v_attn_sink, v_w_out, v_norm_ffn_g, v_w_ffn_in, v_w_ffn_out, v_norm_final_g):
    w = dict(norm_mix_g=norm_mix_g, w_in=w_in, b_gate=b_gate, conv_w=conv_w, conv_b=conv_b, lru_lambda=lru_lambda,
             lru_wa=lru_wa, lru_ba=lru_ba, lru_wx=lru_wx, lru_bx=lru_bx, attn_sink=attn_sink, w_out=w_out,
             norm_ffn_g=norm_ffn_g, w_ffn_in=w_ffn_in, w_ffn_out=w_ffn_out, norm_final_g=norm_final_g)
    m = dict(norm_mix_g=m_norm_mix_g, w_in=m_w_in, b_gate=m_b_gate, conv_w=m_conv_w, conv_b=m_conv_b,
             lru_lambda=m_lru_lambda, lru_wa=m_lru_wa, lru_ba=m_lru_ba, lru_wx=m_lru_wx, lru_bx=m_lru_bx,
             attn_sink=m_attn_sink, w_out=m_w_out, norm_ffn_g=m_norm_ffn_g, w_ffn_in=m_w_ffn_in,
             w_ffn_out=m_w_ffn_out, norm_final_g=m_norm_final_g)
    v = dict(norm_mix_g=v_norm_mix_g, w_in=v_w_in, b_gate=v_b_gate, conv_w=v_conv_w, conv_b=v_conv_b,
             lru_lambda=v_lru_lambda, lru_wa=v_lru_wa, lru_ba=v_lru_ba, lru_wx=v_lru_wx, lru_bx=v_lru_bx,
             attn_sink=v_attn_sink, w_out=v_w_out, norm_ffn_g=v_norm_ffn_g, w_ffn_in=v_w_ffn_in,
             w_ffn_out=v_w_ffn_out, norm_final_g=v_norm_final_g)
    me = 4 * lax.axis_index("x") + 2 * lax.axis_index("y") + lax.axis_index("c")

    def cols_full(got):
        return jnp.swapaxes(got, 0, 1).reshape(got.shape[1], -1)

    def cols_parts(g):
        return jnp.swapaxes(g.reshape(g.shape[0], N_DEV, -1), 0, 1)

    def rows_parts(g):
        return g.reshape(N_DEV, -1, g.shape[1])

    shard_rows = jnp.concatenate([w[n][0] for n in SMALL_SHARD], axis=0)
    got_w_in, got_rows = _exchange([(w_in[0].astype(bf16), False), (shard_rows, False)], "gather_w_in")
    full_rows = cols_full(got_rows)
    small = {n: w[n] for n in ("norm_mix_g", "b_gate", "conv_b", "attn_sink", "norm_ffn_g")}
    small["lru_wa"], small["lru_wx"] = lru_wa[0], lru_wx[0]
    small["norm_final_g"] = norm_final_g.reshape(1, D)
    small["conv_w"], small["lru_lambda"] = full_rows[0:4], full_rows[4:6]
    small["lru_ba"], small["lru_bx"] = full_rows[6:8], full_rows[8:10]

    env = {"w_in_p": _perm_cols(cols_full(got_w_in))}
    recv = {}

    def before(name):
        if name == "norm_inproj":
            return [(w_out[0].astype(bf16), False), (w_ffn_out[0].astype(bf16), False)]
        if name == "lru_fwd":
            return [(w_ffn_in[0].astype(bf16), False)]
        if name == "ffn_bwd1":
            return [(rows_parts(env["dw_fo"]).astype(bf16), True)]
        if name == "attn_bwd":
            return [(rows_parts(env["dw_out"]).astype(bf16), True)]
        if name == "lru_bwd":
            return [(cols_parts(env["dw_fi"]).astype(bf16), True)]
        if name == "conv_bwd":
            ge = env["grads_early"]
            p32, env["early_f32_spans"] = _pack_rows([ge[n] for n in EARLY_F32])
            p16, env["early_bf16_spans"] = _pack_rows([ge[n] for n in EARLY_BF16], bf16)
            return [(p32, False), (p16, False)]
        if name == "inproj_bwd":
            return [(cols_parts(env["dw_in"]).astype(bf16), True)]
        return []

    def after(name, got):
        if name == "norm_inproj":
            env["w_out"], env["w_fo"] = got[0].reshape(D, D), got[1].reshape(D_FF, D)
        elif name == "lru_fwd":
            env["w_fi"] = cols_full(got[0])
        elif name == "ffn_bwd1":
            recv["w_ffn_out"] = got[0]
        elif name == "attn_bwd":
            recv["w_out"] = got[0]
        elif name == "lru_bwd":
            recv["w_ffn_in"] = got[0]
        elif name == "conv_bwd":
            recv["early_f32"], recv["early_bf16"] = got
        elif name == "inproj_bwd":
            recv["w_in"] = got[0]

    grad_x, grads = _local_step(x[0], loss_target[0], small, env, before, after)

    outs = {}
    for name in BIG:
        shard_shape = recv[name].shape[1:]
        r2 = lambda a: a.reshape(shard_shape)
        res = _adamw(recv[name], r2(w[name]), r2(m[name]), r2(v[name]), "adamw_" + name)
        outs[name] = [t.reshape(w[name].shape) for t in res]

    small_names = SMALL_REPL + SMALL_SHARD
    late_packed, late_spans = _pack_rows([grads[n] for n in LATE])
    (got_late,) = _exchange([(late_packed, False)], "gather_late_grads")
    summed = {}
    for names, got, spans, tag in ((EARLY_F32, recv["early_f32"], env["early_f32_spans"], "early_f32"),
                                   (EARLY_BF16, recv["early_bf16"], env["early_bf16_spans"], "early_bf16"),
                                   (LATE, got_late, late_spans, "late")):
        total = _sum_parts(got, "sum_small_" + tag)
        summed.update(zip(names, _unpack_rows(total, spans, [grads[n].shape for n in names])))
    loss = summed["loss"].reshape(())
    gsm = {n: summed[n].reshape(w[n].shape) for n in SMALL_REPL}
    for n in SMALL_SHARD:
        full = summed[n]
        gsm[n] = lax.dynamic_slice_in_dim(full, me * 128, 128, axis=1).reshape(w[n].shape)
    pk = lambda dct: _pack_rows([dct[n] for n in small_names])[0]
    gp, sp = _pack_rows([gsm[n] for n in small_names])
    res = _adamw(gp[None], pk(w), pk(m), pk(v), "adamw_small")
    sshapes = [w[n].shape for n in small_names]
    for idx, t in enumerate(res):
        for n, a in zip(small_names, _unpack_rows(t, sp, sshapes)):
            outs.setdefault(n, [None] * 4)[idx] = a

    result = [loss, grad_x[None]]
    for idx in range(4):
        result += [outs[n][idx] for n in ORDER]
    return tuple(result)
```

```python
import functools
import math

import jax
import jax.numpy as jnp
from jax import lax
from jax.experimental import pallas as pl
from jax.experimental.pallas import tpu as pltpu

f32 = jnp.float32
bf16 = jnp.bfloat16

D = 1024
D_FF = 2816
IN_W = 5632
N_HEADS = 16
N_KV = 4
HEAD_DIM = 64
WINDOW = 128
BLK = 128
LRU_HEADS = 16
LRU_BLOCK = 64
LRU_GROUPS = 4
LRU_GW = 256
LRU_CHUNK = 128
LRU_ROWS = 512
RGLRU_C = 8.0
EPS = 1e-6
NEG_INF = -1e30
N_DEV = 8

ADAM_LR = 0.001
ADAM_B1 = 0.9
ADAM_B2 = 0.999
ADAM_EPS = 1e-08
ADAM_WD = 0.01
ADAM_STEP = 10

VMEM_MB = 56

C_U, C_G, C_Q, C_Z0, C_Z1, C_K, C_V = 0, 1024, 2048, 3072, 4096, 5120, 5376


def _cparams(vmem_mb=VMEM_MB):
    return pltpu.CompilerParams(vmem_limit_bytes=vmem_mb << 20)


def _div_tile(n, pref):
    if n <= pref:
        return n
    return max(t for t in range(8, pref + 1, 8) if n % t == 0)


def _perm_cols(w):
    return jnp.concatenate([w[:, :3072], w[:, 3584:5632], w[:, 3072:3584]], axis=1)


def _unperm_cols(w):
    return jnp.concatenate([w[:, :3072], w[:, 5120:5632], w[:, 3072:5120]], axis=1)


def _sigmoid(x):
    return 1.0 / (1.0 + jnp.exp(-x))


def _sigmoid_t(x):
    return 0.5 * jnp.tanh(0.5 * x) + 0.5


def _log1p(x):
    u = 1.0 + x
    d = u - 1.0
    return jnp.where(d == 0.0, x, jnp.log(u) * (x / jnp.where(d == 0.0, 1.0, d)))


def _softplus(x):
    return jnp.maximum(x, 0.0) + _log1p(jnp.exp(-jnp.abs(x)))


def _gelu_and_grad(x):
    c = math.sqrt(2.0 / math.pi)
    inner = c * (x + 0.044715 * (x * x * x))
    t = jnp.tanh(inner)
    gelu = 0.5 * x * (1.0 + t)
    dinner = c * (1.0 + 3 * 0.044715 * (x * x))
    dgelu = 0.5 * (1.0 + t) + 0.5 * x * (1.0 - t * t) * dinner
    return gelu, dgelu


def _rms_bwd(dn, xv, g):
    r = lax.rsqrt(jnp.mean(xv * xv, axis=-1, keepdims=True) + EPS)
    xh = xv * r
    dxh = dn * g
    dx = r * (dxh - xh * jnp.mean(dxh * xh, axis=-1, keepdims=True))
    return dx, dn * xh


ANY_SPEC = pl.BlockSpec(memory_space=pl.ANY)


def _comm_out_shape(src, scatter):
    return jax.ShapeDtypeStruct((N_DEV, *(src.shape[1:] if scatter else src.shape)), src.dtype)


def _comm_sems():
    return [pltpu.SemaphoreType.DMA((N_DEV - 1,)), pltpu.SemaphoreType.DMA((N_DEV - 1,)), pltpu.SemaphoreType.DMA]


def _scatter_descs(src_ref, out_ref, send_sems, recv_sems, local_sem):
    x, y, c = lax.axis_index("x"), lax.axis_index("y"), lax.axis_index("c")
    me = 4 * x + 2 * y + c
    descs = [pltpu.make_async_copy(src_ref.at[me], out_ref.at[me], local_sem)]
    for k in range(1, N_DEV):
        px, py, pc = x ^ (k >> 2), y ^ ((k >> 1) & 1), c ^ (k & 1)
        descs.append(pltpu.make_async_remote_copy(
            src_ref=src_ref.at[4 * px + 2 * py + pc], dst_ref=out_ref.at[me],
            send_sem=send_sems.at[k - 1], recv_sem=recv_sems.at[k - 1],
            device_id=(px, py, pc), device_id_type=pl.DeviceIdType.MESH))
    return descs


def _gather_copies(src_ref, out_ref, send_sems, recv_sems, local_sem):
    x, y, c = lax.axis_index("x"), lax.axis_index("y"), lax.axis_index("c")
    me, sibling = (x, y, c), (x, y, 1 - c)
    chips = [(1 - x, y), (x, 1 - y), (1 - x, 1 - y)]

    def slot(px, py, pc):
        return out_ref.at[4 * px + 2 * py + pc]

    def copy(k, block, to, src=None):
        return pltpu.make_async_remote_copy(
            src_ref=slot(*block) if src is None else src, dst_ref=slot(*block),
            send_sem=send_sems.at[k], recv_sem=recv_sems.at[k], device_id=to, device_id_type=pl.DeviceIdType.MESH)

    local = pltpu.make_async_copy(src_ref, slot(*me), local_sem)
    first = [copy(0, me, sibling, src=src_ref)] + [copy(1 + j, me, (*chip, c), src=src_ref)
                                                    for j, chip in enumerate(chips)]
    passed = [copy(4 + j, (*chip, c), sibling) for j, chip in enumerate(chips)]
    landed = [copy(1 + j, (*chip, c), me) for j, chip in enumerate(chips)]
    later = [copy(0, sibling, me)] + [copy(4 + j, (*chip, 1 - c), me) for j, chip in enumerate(chips)]
    return local, first, passed, landed, later


def _comm_start(src_ref, out_ref, sems, scatter):
    if scatter:
        for d in _scatter_descs(src_ref, out_ref, *sems):
            d.start()
    else:
        local, first, _, _, _ = _gather_copies(src_ref, out_ref, *sems)
        local.start()
        for cp in first:
            cp.start()


def _comm_finish(src_ref, out_ref, sems, scatter):
    if scatter:
        for d in _scatter_descs(src_ref, out_ref, *sems):
            d.wait()
    else:
        local, first, passed, landed, later = _gather_copies(src_ref, out_ref, *sems)
        for arrived, onward in zip(landed, passed):
            arrived.wait_recv()
            onward.start()
        for cp in later:
            cp.wait_recv()
        for cp in first + passed:
            cp.wait_send()
        local.wait()


def _exchange(comm, name):
    nc = len(comm)

    def body(*refs):
        srcs, outs, sems = refs[:nc], refs[nc:2 * nc], refs[2 * nc:]
        for i in range(nc):
            _comm_start(srcs[i], outs[i], sems[3 * i:3 * i + 3], comm[i][1])
        for i in range(nc):
            _comm_finish(srcs[i], outs[i], sems[3 * i:3 * i + 3], comm[i][1])

    return pl.pallas_call(
        body, name=name, in_specs=[ANY_SPEC] * nc, out_specs=[ANY_SPEC] * nc,
        out_shape=[_comm_out_shape(*c) for c in comm],
        scratch_shapes=[s for _ in comm for s in _comm_sems()],
    )(*[c[0] for c in comm])


def _hosted_call(body, *, name, grid, in_specs, out_specs, out_shape, args, scratch_shapes=(), comm=()):
    nin, nout, nscr, nc = len(in_specs), len(out_specs), len(scratch_shapes), len(comm)

    def wrapped(*refs):
        ins = refs[:nin]
        csrc = refs[nin:nin + nc]
        outs = refs[nin + nc:nin + nc + nout]
        cout = refs[nin + nc + nout:nin + 2 * nc + nout]
        scr = refs[nin + 2 * nc + nout:]
        sems = scr[nscr:]

        if nc:
            first = functools.reduce(jnp.logical_and, [pl.program_id(a) == 0 for a in range(len(grid))])

            @pl.when(first)
            def _():
                for i in range(nc):
                    _comm_start(csrc[i], cout[i], sems[3 * i:3 * i + 3], comm[i][1])

        body(*ins, *outs, *scr[:nscr])

        if nc:
            last = functools.reduce(jnp.logical_and, [pl.program_id(a) == grid[a] - 1 for a in range(len(grid))])

            @pl.when(last)
            def _():
                for i in range(nc):
                    _comm_finish(csrc[i], cout[i], sems[3 * i:3 * i + 3], comm[i][1])

    res = pl.pallas_call(
        wrapped, name=name, grid=grid,
        in_specs=[*in_specs, *[ANY_SPEC] * nc], out_specs=[*out_specs, *[ANY_SPEC] * nc],
        out_shape=[*out_shape, *[_comm_out_shape(*c) for c in comm]],
        scratch_shapes=[*scratch_shapes, *[s for _ in comm for s in _comm_sems()]],
        compiler_params=_cparams())(*args, *[c[0] for c in comm])
    return res[:nout], res[nout:]


def _norm_matmul(x, g, w, name, tm=1024, tn=1408, comm=()):
    S, dm = x.shape
    n = w.shape[1]
    tm = min(tm, S)

    def body(x_ref, g_ref, w_ref, xn_ref, o_ref):
        @pl.when(pl.program_id(1) == 0)
        def _():
            xv = x_ref[...]
            r = lax.rsqrt(jnp.mean(xv * xv, axis=-1, keepdims=True) + EPS)
            xn_ref[...] = ((xv * r) * g_ref[...]).astype(bf16)

        o_ref[...] = jnp.dot(xn_ref[...], w_ref[...], preferred_element_type=f32).astype(bf16)

    return _hosted_call(
        body, name=name, grid=(S // tm, n // tn),
        in_specs=[pl.BlockSpec((tm, dm), lambda i, j: (i, 0)),
                  pl.BlockSpec((1, dm), lambda i, j: (0, 0)),
                  pl.BlockSpec((dm, tn), lambda i, j: (0, j))],
        out_specs=[pl.BlockSpec((tm, dm), lambda i, j: (i, 0)),
                   pl.BlockSpec((tm, tn), lambda i, j: (i, j))],
        out_shape=[jax.ShapeDtypeStruct((S, dm), bf16), jax.ShapeDtypeStruct((S, n), bf16)],
        args=(x, g, w), comm=comm)


def _mm_tn(a, b, name, tk, tn, tmc=2048):
    m, ka = a.shape
    n = b.shape[1]
    tmc = min(tmc, m)

    def body(a_ref, b_ref, o_ref):
        @pl.when(pl.program_id(2) == 0)
        def _():
            o_ref[...] = jnp.zeros_like(o_ref)

        o_ref[...] += lax.dot_general(a_ref[...], b_ref[...], (((0,), (0,)), ((), ())),
                                      preferred_element_type=f32)

    return pl.pallas_call(
        body, name=name, grid=(ka // tk, n // tn, m // tmc),
        in_specs=[pl.BlockSpec((tmc, tk), lambda i, j, k: (k, i)),
                  pl.BlockSpec((tmc, tn), lambda i, j, k: (k, j))],
        out_specs=pl.BlockSpec((tk, tn), lambda i, j, k: (i, j)),
        out_shape=jax.ShapeDtypeStruct((ka, n), f32),
        compiler_params=_cparams())(a, b)


HALO = 16


def _rows_at(ext, o, tc):
    if o == 0:
        return ext[HALO:HALO + tc]
    return pltpu.roll(ext, (-o) % ext.shape[0], 0)[HALO:HALO + tc]


def _halo_specs(tc, S, width, col):
    per = tc // HALO
    last = S // HALO - 1
    return (pl.BlockSpec((tc, width), lambda i: (i, col)),
            pl.BlockSpec((HALO, width), lambda i: (jnp.maximum(i * per - 1, 0), col)),
            pl.BlockSpec((HALO, width), lambda i: (jnp.minimum((i + 1) * per, last), col)))


def _extended(cur_ref, prev_ref, next_ref, i, nsteps):
    prev = jnp.where(i > 0, prev_ref[...].astype(f32), 0.0)
    nxt = jnp.where(i < nsteps - 1, next_ref[...].astype(f32), 0.0)
    return jnp.concatenate([prev, cur_ref[...].astype(f32), nxt], axis=0)


def _conv_fwd(proj, cw, cb, tc=512):
    S = proj.shape[0]
    tc = min(tc, S)
    nsteps = S // tc

    def body(cur_ref, prev_ref, next_ref, w_ref, b_ref, o_ref):
        ext = _extended(cur_ref, prev_ref, next_ref, pl.program_id(0), nsteps)
        acc = _rows_at(ext, -2, tc) * w_ref[0:1, :]
        for k in range(1, 4):
            acc = acc + _rows_at(ext, k - 2, tc) * w_ref[k:k + 1, :]
        o_ref[...] = acc + b_ref[...]

    return pl.pallas_call(
        body, name="conv_fwd", grid=(nsteps,),
        in_specs=[*_halo_specs(tc, S, D, 0),
                  pl.BlockSpec((4, D), lambda i: (0, 0)), pl.BlockSpec((1, D), lambda i: (0, 0))],
        out_specs=pl.BlockSpec((tc, D), lambda i: (i, 0)),
        out_shape=jax.ShapeDtypeStruct((S, D), f32),
        compiler_params=_cparams())(proj, proj, proj, cw, cb)


def _conv_bwd(duc_f, duc_b, proj, cw, tc=512, comm=()):
    S = proj.shape[0]
    tc = min(tc, S)
    nsteps = S // tc

    def body(fc, fp, fn, bc, bp, bn, uc_, up, un, w_ref, du_ref, dw_ref, db_ref):
        i = pl.program_id(0)

        @pl.when(i == 0)
        def _():
            dw_ref[...] = jnp.zeros_like(dw_ref)
            db_ref[...] = jnp.zeros_like(db_ref)

        dext = _extended(fc, fp, fn, i, nsteps) + _extended(bc, bp, bn, i, nsteps)
        uext = _extended(uc_, up, un, i, nsteps)
        d = dext[HALO:HALO + tc]
        acc = _rows_at(dext, 2, tc) * w_ref[0:1, :]
        for k in range(1, 4):
            acc = acc + _rows_at(dext, 2 - k, tc) * w_ref[k:k + 1, :]
        du_ref[...] = acc.astype(bf16)
        wrow = lax.broadcasted_iota(jnp.int32, (4, D), 0)
        for k in range(4):
            dw_ref[...] += jnp.where(wrow == k, jnp.sum(d * _rows_at(uext, k - 2, tc), axis=0, keepdims=True), 0.0)
        db_ref[...] += jnp.sum(d, axis=0, keepdims=True)

    return _hosted_call(
        body, name="conv_bwd", grid=(nsteps,),
        in_specs=[*_halo_specs(tc, S, D, 0), *_halo_specs(tc, S, D, 0), *_halo_specs(tc, S, D, 0),
                  pl.BlockSpec((4, D), lambda i: (0, 0))],
        out_specs=[pl.BlockSpec((tc, D), lambda i: (i, 0)),
                   pl.BlockSpec((4, D), lambda i: (0, 0)), pl.BlockSpec((1, D), lambda i: (0, 0))],
        out_shape=[jax.ShapeDtypeStruct((S, D), bf16), jax.ShapeDtypeStruct((4, D), f32),
                   jax.ShapeDtypeStruct((1, D), f32)],
        args=(duc_f, duc_f, duc_f, duc_b, duc_b, duc_b, proj, proj, proj, cw), comm=comm)


def _scan_scratch():
    halves = [pltpu.VMEM((LRU_CHUNK, 128), f32) for _ in range(2 * (LRU_GW // 128))]
    return [*halves, pltpu.VMEM((LRU_CHUNK // 8, LRU_GW), f32), pltpu.VMEM((LRU_CHUNK // 8, LRU_GW), f32)]


def _log_scan(a, b, row, n, reverse, steps):
    for s in steps:
        shift = a.shape[0] - s if reverse else s
        keep = (row < n - s) if reverse else (row >= s)
        a_sh = pltpu.roll(a, shift, 0)
        b_sh = pltpu.roll(b, shift, 0)
        b = jnp.where(keep, a * b_sh + b, b)
        a = jnp.where(keep, a * a_sh, a)
    return a, b


def _scan_chunk(a, b, carry, reverse, *scratch):
    tc, w = a.shape
    ng = tc // 8
    nl = w // 128
    sa_refs, sb_refs, sc_ref, st_ref = scratch[:nl], scratch[nl:2 * nl], scratch[2 * nl], scratch[2 * nl + 1]
    sub = lax.broadcasted_iota(jnp.int32, (8, w), 0)
    ag, bg = [], []
    for k in range(ng):
        ak, bk = _log_scan(a[8 * k:8 * k + 8], b[8 * k:8 * k + 8], sub, 8, reverse, (1, 2, 4))
        ag.append(ak)
        bg.append(bk)
    a = jnp.concatenate(ag, axis=0)
    b = jnp.concatenate(bg, axis=0)
    edge = 0 if reverse else 7
    for i in range(nl):
        sa_refs[i][...] = a[:, 128 * i:128 * (i + 1)]
        sb_refs[i][...] = b[:, 128 * i:128 * (i + 1)]
    ta = jnp.concatenate([r[pl.ds(edge, ng, stride=8), :] for r in sa_refs], axis=1)
    tb = jnp.concatenate([r[pl.ds(edge, ng, stride=8), :] for r in sb_refs], axis=1)
    grow = lax.broadcasted_iota(jnp.int32, (ng, w), 0)
    ta, tb = _log_scan(ta, tb, grow, ng, reverse, [1 << i for i in range(ng.bit_length() - 1)])
    state = tb + ta * carry
    st_ref[...] = state
    if reverse:
        sc_ref[...] = jnp.where(grow == ng - 1, carry, pltpu.roll(state, ng - 1, 0))
    else:
        sc_ref[...] = jnp.where(grow == 0, carry, pltpu.roll(state, 1, 0))
    h = jnp.concatenate([bg[k] + ag[k] * sc_ref[k:k + 1, :] for k in range(ng)], axis=0)
    return h, (st_ref[0:1, :] if reverse else st_ref[ng - 1:ng, :])


def _lru_gates(uc, w, p_ref):
    pre = jnp.dot(uc.astype(bf16), w, preferred_element_type=f32)
    r = _sigmoid_t(pre[:, :LRU_GW] + p_ref[0, 1:2, :])
    gi = _sigmoid_t(pre[:, LRU_GW:] + p_ref[0, 2:3, :])
    sp = _softplus(-p_ref[0, 0:1, :])
    log_a = -RGLRU_C * r * sp
    a = jnp.exp(log_a)
    x = 2.0 * log_a
    series = -x * (1.0 + x * (0.5 + x * (1.0 / 6 + x * (1.0 / 24))))
    beta = jnp.sqrt(jnp.maximum(jnp.where(x > -0.0625, series, 1.0 - a * a), 0.0))
    return r, gi, sp, a, beta


def _lru_fwd(uc, wg, lp, reverse, comm=()):
    S = uc.shape[0]
    tc = LRU_CHUNK
    rows = min(LRU_ROWS, S)
    nsub = rows // tc
    nblk = S // rows
    d = 1 if reverse else 0

    def bidx(c):
        return nblk - 1 - c if reverse else c

    def body(uc_ref, w_ref, p_ref, h_ref, carry_ref, *scan_scratch):
        @pl.when(pl.program_id(1) == 0)
        def _():
            carry_ref[...] = jnp.zeros_like(carry_ref)

        carry = carry_ref[...]
        for j in (reversed(range(nsub)) if reverse else range(nsub)):
            sl = slice(j * tc, (j + 1) * tc)
            ucv = uc_ref[sl, :]
            _, gi, _, a, beta = _lru_gates(ucv, w_ref[0], p_ref)
            h, carry = _scan_chunk(a, beta * (gi * ucv), carry, reverse, *scan_scratch)
            h_ref[sl, :] = h.astype(bf16)
        carry_ref[...] = carry

    return _hosted_call(
        body, name="lru_fwd_rev" if reverse else "lru_fwd", grid=(LRU_GROUPS, nblk),
        in_specs=[pl.BlockSpec((rows, LRU_GW), lambda g, c: (bidx(c), g)),
                  pl.BlockSpec((1, LRU_GW, 2 * LRU_GW), lambda g, c: (g, 0, d)),
                  pl.BlockSpec((1, 8, LRU_GW), lambda g, c: (d, 0, g))],
        out_specs=[pl.BlockSpec((rows, LRU_GW), lambda g, c: (bidx(c), g))],
        out_shape=[jax.ShapeDtypeStruct((S, D), bf16)],
        scratch_shapes=[pltpu.VMEM((1, LRU_GW), f32), *_scan_scratch()],
        args=(uc, wg, lp), comm=comm)


def _lru_bwd(uc, dh, h, wg, wgt, lp, reverse, comm=()):
    S = uc.shape[0]
    tc = LRU_CHUNK
    rows = min(LRU_ROWS, S)
    nsub = rows // tc
    nblk = S // rows
    d = 1 if reverse else 0
    per = rows // HALO
    last8 = S // HALO - 1

    def bidx(c):
        return c if reverse else nblk - 1 - c

    def halo_idx(c):
        if reverse:
            return jnp.minimum((bidx(c) + 1) * per, last8)
        return jnp.maximum(bidx(c) * per - 1, 0)

    def body(uc_ref, dh_ref, h_ref, halo_ref, w_ref, wt_ref, p_ref, duc_ref, dw_ref, dp_ref, carry_ref, tmp_ref,
             *scan_scratch):
        c = pl.program_id(1)
        bi = bidx(c)

        @pl.when(c == 0)
        def _():
            carry_ref[...] = jnp.zeros_like(carry_ref)
            dw_ref[...] = jnp.zeros_like(dw_ref)
            dp_ref[...] = jnp.zeros_like(dp_ref)

        row = lax.broadcasted_iota(jnp.int32, (tc, LRU_GW), 0)
        carry = carry_ref[...]
        dw = jnp.zeros((LRU_GW, 2 * LRU_GW), f32)
        dsp = jnp.zeros((1, LRU_GW), f32)
        dba = jnp.zeros((1, LRU_GW), f32)
        dbx = jnp.zeros((1, LRU_GW), f32)
        for j in (range(nsub) if reverse else reversed(range(nsub))):
            sl = slice(j * tc, (j + 1) * tc)
            ucv = uc_ref[sl, :]
            ucb = ucv.astype(bf16)
            r, gi, sp, a, beta = _lru_gates(ucv, w_ref[0], p_ref)
            hv = h_ref[sl, :].astype(f32)
            dhv = dh_ref[sl, :].astype(f32)
            if reverse:
                alpha = jnp.where(row == 0, 1.0, pltpu.roll(a, 1, 0))
                gsc, _ = _scan_chunk(alpha, dhv, carry, False, *scan_scratch)
                if j < nsub - 1:
                    edge = h_ref[(j + 1) * tc:(j + 1) * tc + HALO, :].astype(f32)[0:1, :]
                else:
                    edge = jnp.where(bi < nblk - 1, halo_ref[...].astype(f32)[0:1, :], 0.0)
                h_nb = jnp.where(row == tc - 1, edge, pltpu.roll(hv, tc - 1, 0))
            else:
                alpha = jnp.where(row == tc - 1, 1.0, pltpu.roll(a, tc - 1, 0))
                gsc, _ = _scan_chunk(alpha, dhv, carry, True, *scan_scratch)
                if j > 0:
                    edge = h_ref[j * tc - HALO:j * tc, :].astype(f32)[HALO - 1:HALO, :]
                else:
                    edge = jnp.where(bi > 0, halo_ref[...].astype(f32)[HALO - 1:HALO, :], 0.0)
                h_nb = jnp.where(row == 0, edge, pltpu.roll(hv, 1, 0))
            tmp_ref[...] = a * gsc
            carry = tmp_ref[tc - 1:tc, :] if reverse else tmp_ref[0:1, :]

            da = gsc * h_nb
            dbeta = gsc * (gi * ucv)
            dl = da * a - dbeta * (a * a) / beta
            dr = dl * (-RGLRU_C * sp)
            dsp = dsp + jnp.sum(dl * (-RGLRU_C * r), axis=0, keepdims=True)
            dgi = gsc * beta * ucv
            dpre_r = dr * r * (1.0 - r)
            dpre_i = dgi * gi * (1.0 - gi)
            dba = dba + jnp.sum(dpre_r, axis=0, keepdims=True)
            dbx = dbx + jnp.sum(dpre_i, axis=0, keepdims=True)
            dpre = jnp.concatenate([dpre_r, dpre_i], axis=1).astype(bf16)
            duc_ref[sl, :] = (gsc * beta * gi + jnp.dot(dpre, wt_ref[0], preferred_element_type=f32)).astype(bf16)
            dw = dw + lax.dot_general(ucb, dpre, (((0,), (0,)), ((), ())), preferred_element_type=f32)
        carry_ref[...] = carry
        dw_ref[0] += dw
        dlam = dsp * (-_sigmoid(-p_ref[0, 0:1, :]))
        prow = lax.broadcasted_iota(jnp.int32, (8, LRU_GW), 0)
        dp_ref[...] += (jnp.where(prow == 0, dlam, 0.0) + jnp.where(prow == 1, dba, 0.0)
                        + jnp.where(prow == 2, dbx, 0.0))

    chunk = pl.BlockSpec((rows, LRU_GW), lambda g, c: (bidx(c), g))
    return _hosted_call(
        body, name="lru_bwd_rev" if reverse else "lru_bwd", grid=(LRU_GROUPS, nblk),
        in_specs=[chunk, chunk, chunk,
                  pl.BlockSpec((HALO, LRU_GW), lambda g, c: (halo_idx(c), g)),
                  pl.BlockSpec((1, LRU_GW, 2 * LRU_GW), lambda g, c: (g, 0, d)),
                  pl.BlockSpec((1, 2 * LRU_GW, LRU_GW), lambda g, c: (g, d, 0)),
                  pl.BlockSpec((1, 8, LRU_GW), lambda g, c: (d, 0, g))],
        out_specs=[chunk,
                   pl.BlockSpec((1, LRU_GW, 2 * LRU_GW), lambda g, c: (g, 0, 0)),
                   pl.BlockSpec((8, LRU_GW), lambda g, c: (0, g))],
        out_shape=[jax.ShapeDtypeStruct((S, D), bf16),
                   jax.ShapeDtypeStruct((LRU_GROUPS, LRU_GW, 2 * LRU_GW), f32),
                   jax.ShapeDtypeStruct((8, D), f32)],
        scratch_shapes=[pltpu.VMEM((1, LRU_GW), f32), pltpu.VMEM((tc, LRU_GW), f32), *_scan_scratch()],
        args=(uc, dh, h, h, wg, wgt, lp), comm=comm)


def _slope(h):
    return 2.0 ** (-8.0 * (h + 1.0) / N_HEADS)


def _kv_specs(nb, col):
    return [pl.BlockSpec((BLK, N_KV * HEAD_DIM), lambda n: (jnp.maximum(n - 1, 0), col)),
            pl.BlockSpec((BLK, N_KV * HEAD_DIM), lambda n: (n, col)),
            pl.BlockSpec((BLK, N_KV * HEAD_DIM), lambda n: (jnp.minimum(n + 1, nb - 1), col))]


def _dup_windows(r0, r1, r2):
    left = lax.broadcasted_iota(jnp.int32, (3 * BLK, 128), 1) < HEAD_DIM
    win = jnp.concatenate([r0[...], r1[...], r2[...]], axis=0)
    out = []
    for i in range(N_KV // 2):
        t = win[:, i * 128:(i + 1) * 128]
        r = pltpu.roll(t, HEAD_DIM, 1)
        out += [jnp.where(left, t, r).astype(bf16), jnp.where(left, r, t).astype(bf16)]
    return out


def _attn_bias_init(bias_ref):
    k_loc = lax.broadcasted_iota(jnp.int32, (3 * BLK, BLK), 0)
    q_loc = lax.broadcasted_iota(jnp.int32, (3 * BLK, BLK), 1)
    adist = jnp.abs(q_loc + BLK - k_loc)
    adf = adist.astype(f32)
    for e in range(3):
        ok = adist <= WINDOW
        if e == 0:
            ok = ok & (k_loc >= BLK)
        if e == 2:
            ok = ok & (k_loc < 2 * BLK)
        for kv in range(N_KV):
            bias_ref[e, kv] = jnp.concatenate(
                [jnp.where(ok, (-_slope(4 * kv + j)) * adf, NEG_INF) for j in range(4)], axis=1)


def _stack_heads(ref, kv, scale):
    left = lax.broadcasted_iota(jnp.int32, (BLK, 128), 1) < HEAD_DIM
    rows = []
    for pp in range(2):
        t = ref[:, (2 * kv + pp) * 128:(2 * kv + pp + 1) * 128]
        if scale != 1.0:
            t = t * scale
        zero = jnp.zeros_like(t)
        rows += [jnp.where(left, t, zero).astype(bf16), jnp.where(left, zero, t).astype(bf16)]
    return jnp.concatenate(rows, axis=0)


def _attn_softmax(qs, k2, bias, sink_ref, kv):
    sink = jnp.concatenate([jnp.full((1, BLK), sink_ref[0, 4 * kv + j], f32) for j in range(4)], axis=1)
    s = lax.dot_general(k2, qs, (((1,), (1,)), ((), ())), preferred_element_type=f32) + bias
    m = jnp.maximum(jnp.max(s, axis=0, keepdims=True), sink)
    p = jnp.exp(s - m)
    ps = jnp.exp(sink - m)
    inv = 1.0 / (jnp.sum(p, axis=0, keepdims=True) + ps)
    return p, ps, inv


def _pair_tiles(t):
    return [jnp.concatenate([t[:HEAD_DIM, 256 * pp:256 * pp + 128],
                             t[HEAD_DIM:, 256 * pp + 128:256 * pp + 256]], axis=0).T for pp in range(2)]


def _attn_fwd(proj, sink, comm=()):
    S = proj.shape[0]
    nb = S // BLK
    assert nb >= 2

    def body(q_ref, k0, k1, k2_, v0, v1, v2_, sink_ref, o_ref, bias_ref):
        n = pl.program_id(0)

        @pl.when(n == 0)
        def _():
            _attn_bias_init(bias_ref)

        e = jnp.where(n == 0, 0, jnp.where(n == nb - 1, 2, 1))
        kk = _dup_windows(k0, k1, k2_)
        vv = _dup_windows(v0, v1, v2_)
        tiles = []
        for kv in range(N_KV):
            qs = _stack_heads(q_ref, kv, HEAD_DIM ** -0.5)
            p, _, inv = _attn_softmax(qs, kk[kv], bias_ref[e, kv], sink_ref, kv)
            ot = lax.dot_general(vv[kv], p.astype(bf16), (((0,), (0,)), ((), ())), preferred_element_type=f32)
            tiles += _pair_tiles(ot * inv)
        o_ref[...] = jnp.concatenate(tiles, axis=1).astype(bf16)

    return _hosted_call(
        body, name="attn_fwd", grid=(nb,),
        in_specs=[pl.BlockSpec((BLK, D), lambda n: (n, C_Q // D)),
                  *_kv_specs(nb, C_K // (N_KV * HEAD_DIM)), *_kv_specs(nb, C_V // (N_KV * HEAD_DIM)),
                  pl.BlockSpec(memory_space=pltpu.SMEM)],
        out_specs=[pl.BlockSpec((BLK, D), lambda n: (n, 0))],
        out_shape=[jax.ShapeDtypeStruct((S, D), bf16)],
        scratch_shapes=[pltpu.VMEM((3, N_KV, 3 * BLK, 4 * BLK), f32)],
        args=(proj, proj, proj, proj, proj, proj, proj, sink), comm=comm)


def _attn_bwd(proj, sink, dyb, comm=()):
    S = proj.shape[0]
    nb = S // BLK
    assert nb >= 2

    def body(q_ref, k0, k1, k2_, v0, v1, v2_, sink_ref, do_ref, dq_ref, dk_out, dv_out, ds_ref,
             bias_ref, dk_ref, dv_ref, dsk_ref):
        n = pl.program_id(0)

        @pl.when(n == 0)
        def _():
            _attn_bias_init(bias_ref)
            dk_ref[...] = jnp.zeros_like(dk_ref)
            dv_ref[...] = jnp.zeros_like(dv_ref)
            dsk_ref[...] = jnp.zeros_like(dsk_ref)

        e = jnp.where(n == 0, 0, jnp.where(n == nb - 1, 2, 1))
        kk = _dup_windows(k0, k1, k2_)
        vv = _dup_windows(v0, v1, v2_)
        left3 = lax.broadcasted_iota(jnp.int32, (3 * BLK, 128), 1) < HEAD_DIM
        start = pl.multiple_of(n * BLK, BLK)
        dq_tiles, dks, dvs = [], [], []
        for kv in range(N_KV):
            qs = _stack_heads(q_ref, kv, HEAD_DIM ** -0.5)
            dos = _stack_heads(do_ref, kv, 1.0)
            p, ps, inv = _attn_softmax(qs, kk[kv], bias_ref[e, kv], sink_ref, kv)
            pn = p * inv
            dp = lax.dot_general(vv[kv], dos, (((1,), (1,)), ((), ())), preferred_element_type=f32)
            delta = jnp.sum(pn * dp, axis=0, keepdims=True)
            dsc = (pn * (dp - delta)).astype(bf16)
            dsk_ref[kv:kv + 1, :] += delta * (ps * inv)
            dqt = lax.dot_general(kk[kv], dsc, (((0,), (0,)), ((), ())), preferred_element_type=f32)
            dq_tiles += _pair_tiles(dqt * (HEAD_DIM ** -0.5))
            dk = jnp.dot(dsc, qs, preferred_element_type=f32)
            dv = jnp.dot(pn.astype(bf16), dos, preferred_element_type=f32)
            dks.append(dk + pltpu.roll(dk, HEAD_DIM, 1))
            dvs.append(dv + pltpu.roll(dv, HEAD_DIM, 1))
        for jp in range(N_KV // 2):
            cols = slice(jp * 128, (jp + 1) * 128)
            dk_ref[pl.ds(start, 3 * BLK), cols] += jnp.where(left3, dks[2 * jp], dks[2 * jp + 1])
            dv_ref[pl.ds(start, 3 * BLK), cols] += jnp.where(left3, dvs[2 * jp], dvs[2 * jp + 1])
        dq_ref[...] = jnp.concatenate(dq_tiles, axis=1).astype(bf16)

        @pl.when(n == nb - 1)
        def _():
            pltpu.sync_copy(dk_ref, dk_out)
            pltpu.sync_copy(dv_ref, dv_out)
            lane = lax.broadcasted_iota(jnp.int32, (1, 128), 1)
            dsink = jnp.zeros((1, 128), f32)
            for h in range(N_HEADS):
                part = dsk_ref[h // 4:h // 4 + 1, (h % 4) * BLK:(h % 4 + 1) * BLK]
                dsink = dsink + jnp.where(lane == h, -jnp.sum(part), 0.0)
            ds_ref[...] = dsink

    acc = jax.ShapeDtypeStruct((S + 2 * BLK, N_KV * HEAD_DIM), f32)
    return _hosted_call(
        body, name="attn_bwd", grid=(nb,),
        in_specs=[pl.BlockSpec((BLK, D), lambda n: (n, C_Q // D)),
                  *_kv_specs(nb, C_K // (N_KV * HEAD_DIM)), *_kv_specs(nb, C_V // (N_KV * HEAD_DIM)),
                  pl.BlockSpec(memory_space=pltpu.SMEM),
                  pl.BlockSpec((BLK, D), lambda n: (n, 0))],
        out_specs=[pl.BlockSpec((BLK, D), lambda n: (n, 0)), ANY_SPEC, ANY_SPEC,
                   pl.BlockSpec((1, 128), lambda n: (0, 0))],
        out_shape=[jax.ShapeDtypeStruct((S, D), bf16), acc, acc, jax.ShapeDtypeStruct((1, 128), f32)],
        scratch_shapes=[pltpu.VMEM((3, N_KV, 3 * BLK, 4 * BLK), f32), pltpu.VMEM(acc.shape, f32),
                        pltpu.VMEM(acc.shape, f32), pltpu.VMEM((8, 4 * BLK), f32)],
        args=(proj, proj, proj, proj, proj, proj, proj, sink, dyb), comm=comm)


def _merge_parts(hf, hb, g, z0, z1, yb, bg):
    g0 = _sigmoid(z0.astype(f32) + bg[:, :D])
    g1 = _sigmoid(z1.astype(f32) + bg[:, D:])
    gelu, dgelu = _gelu_and_grad(g.astype(f32))
    hs = hf.astype(f32) + hb.astype(f32)
    ya = hs * gelu
    return g0, g1, gelu, dgelu, hs, ya


def _merge_outproj(x, hf, hb, proj, yb, bg, w_out, tm=512):
    S = x.shape[0]
    tm = min(tm, S)

    def body(x_ref, hf_ref, hb_ref, g_ref, z0_ref, z1_ref, yb_ref, bg_ref, w_ref, mg_ref, x1_ref):
        ybv = yb_ref[...].astype(f32)
        g0, g1, _, _, _, ya = _merge_parts(hf_ref[...], hb_ref[...], g_ref[...], z0_ref[...], z1_ref[...],
                                           ybv, bg_ref[...])
        mg = (g0 * ya + g1 * ybv).astype(bf16)
        mg_ref[...] = mg
        x1_ref[...] = x_ref[...] + jnp.dot(mg, w_ref[...], preferred_element_type=f32)

    row = pl.BlockSpec((tm, D), lambda i: (i, 0))
    return pl.pallas_call(
        body, name="merge_outproj", grid=(S // tm,),
        in_specs=[row, row, row,
                  pl.BlockSpec((tm, D), lambda i: (i, C_G // D)),
                  pl.BlockSpec((tm, D), lambda i: (i, C_Z0 // D)),
                  pl.BlockSpec((tm, D), lambda i: (i, C_Z1 // D)),
                  row, pl.BlockSpec((1, 2 * D), lambda i: (0, 0)), pl.BlockSpec((D, D), lambda i: (0, 0))],
        out_specs=[row, row],
        out_shape=[jax.ShapeDtypeStruct((S, D), bf16), jax.ShapeDtypeStruct((S, D), f32)],
        compiler_params=_cparams())(x, hf, hb, proj, proj, proj, yb, bg, w_out)


def _ffn_out_loss(gu, x1, w_fo, g3, tgt, tm=256):
    S = x1.shape[0]
    tm = min(tm, S)

    def body(gt_ref, up_ref, x1_ref, w_ref, g_ref, t_ref, ff_ref, dx_ref, dxb_ref, loss_ref, dg_ref):
        @pl.when(pl.program_id(0) == 0)
        def _():
            loss_ref[...] = jnp.zeros_like(loss_ref)
            dg_ref[...] = jnp.zeros_like(dg_ref)

        gt = gt_ref[...].astype(f32)
        ff = ((gt * _sigmoid(gt)) * up_ref[...].astype(f32)).astype(bf16)
        ff_ref[...] = ff
        x2 = x1_ref[...] + jnp.dot(ff, w_ref[...], preferred_element_type=f32)
        gv = g_ref[...]
        r = lax.rsqrt(jnp.mean(x2 * x2, axis=-1, keepdims=True) + EPS)
        xh = x2 * r
        diff = xh * gv - t_ref[...]
        loss_ref[...] += (0.5 / D) * jnp.sum(diff * diff)
        dy = diff * (1.0 / D)
        dg_ref[...] += jnp.sum(dy * xh, axis=0, keepdims=True)
        dxh = dy * gv
        dx = r * (dxh - xh * jnp.mean(dxh * xh, axis=-1, keepdims=True))
        dx_ref[...] = dx
        dxb_ref[...] = dx.astype(bf16)

    row = pl.BlockSpec((tm, D), lambda i: (i, 0))
    vec = pl.BlockSpec((1, D), lambda i: (0, 0))
    return pl.pallas_call(
        body, name="ffn_out_loss", grid=(S // tm,),
        in_specs=[pl.BlockSpec((tm, D_FF), lambda i: (i, 0)), pl.BlockSpec((tm, D_FF), lambda i: (i, 1)),
                  row, pl.BlockSpec((D_FF, D), lambda i: (0, 0)), vec, row],
        out_specs=[pl.BlockSpec((tm, D_FF), lambda i: (i, 0)), row, row,
                   pl.BlockSpec((1, 128), lambda i: (0, 0)), vec],
        out_shape=[jax.ShapeDtypeStruct((S, D_FF), bf16), jax.ShapeDtypeStruct((S, D), f32),
                   jax.ShapeDtypeStruct((S, D), bf16), jax.ShapeDtypeStruct((1, 128), f32),
                   jax.ShapeDtypeStruct((1, D), f32)],
        compiler_params=_cparams())(gu, gu, x1, w_fo, g3, tgt)


def _ffn_bwd1(dx2b, w_fot, gu, tm=256, comm=()):
    S = dx2b.shape[0]
    tm = min(tm, S)

    def body(dx_ref, w_ref, gt_ref, up_ref, dgt_ref, dup_ref):
        dff = jnp.dot(dx_ref[...], w_ref[...], preferred_element_type=f32)
        gt = gt_ref[...].astype(f32)
        sg = _sigmoid(gt)
        dup_ref[...] = (dff * (gt * sg)).astype(bf16)
        dgt_ref[...] = ((dff * up_ref[...].astype(f32)) * (sg * (1.0 + gt * (1.0 - sg)))).astype(bf16)

    wide = pl.BlockSpec((tm, D_FF), lambda i: (i, 0))
    return _hosted_call(
        body, name="ffn_bwd1", grid=(S // tm,),
        in_specs=[pl.BlockSpec((tm, D), lambda i: (i, 0)), pl.BlockSpec((D, D_FF), lambda i: (0, 0)),
                  wide, pl.BlockSpec((tm, D_FF), lambda i: (i, 1))],
        out_specs=[wide, wide],
        out_shape=[jax.ShapeDtypeStruct((S, D_FF), bf16), jax.ShapeDtypeStruct((S, D_FF), bf16)],
        args=(dx2b, w_fot, gu, gu), comm=comm)


def _proj_bwd(pieces, wts, xres, g, dres, name, tm=256, comm=()):
    S = xres.shape[0]
    tm = min(tm, S)
    np_ = len(pieces)

    def body(*refs):
        p_refs = refs[:np_]
        w_refs = refs[np_:2 * np_]
        x_ref, g_ref, dres_ref, dx_ref, dxb_ref, dg_ref = refs[2 * np_:]

        @pl.when(pl.program_id(0) == 0)
        def _():
            dg_ref[...] = jnp.zeros_like(dg_ref)

        dn = jnp.dot(p_refs[0][...], w_refs[0][...], preferred_element_type=f32)
        for pr, wr in zip(p_refs[1:], w_refs[1:]):
            dn = dn + jnp.dot(pr[...], wr[...], preferred_element_type=f32)
        dxn, dgc = _rms_bwd(dn, x_ref[...], g_ref[...])
        dx = dres_ref[...] + dxn
        dx_ref[...] = dx
        dxb_ref[...] = dx.astype(bf16)
        dg_ref[...] += jnp.sum(dgc, axis=0, keepdims=True)

    row = pl.BlockSpec((tm, D), lambda i: (i, 0))
    vec = pl.BlockSpec((1, D), lambda i: (0, 0))
    return _hosted_call(
        body, name=name, grid=(S // tm,),
        in_specs=[*[pl.BlockSpec((tm, p.shape[1]), lambda i: (i, 0)) for p in pieces],
                  *[pl.BlockSpec(w.shape, lambda i: (0, 0)) for w in wts],
                  row, vec, row],
        out_specs=[row, row, vec],
        out_shape=[jax.ShapeDtypeStruct((S, D), f32), jax.ShapeDtypeStruct((S, D), bf16),
                   jax.ShapeDtypeStruct((1, D), f32)],
        args=(*pieces, *wts, xres, g, dres), comm=comm)


def _outproj_bwd(dx1b, w_outt, hf, hb, proj, yb, bg, tm=512):
    S = dx1b.shape[0]
    tm = min(tm, S)

    def body(dx_ref, w_ref, hf_ref, hb_ref, g_ref, z0_ref, z1_ref, yb_ref, bg_ref,
             dh_ref, dg_ref, dz_ref, dyb_ref, dbg_ref):
        @pl.when(pl.program_id(0) == 0)
        def _():
            dbg_ref[...] = jnp.zeros_like(dbg_ref)

        dm = jnp.dot(dx_ref[...], w_ref[...], preferred_element_type=f32)
        ybv = yb_ref[...].astype(f32)
        g0, g1, gelu, dgelu, hs, ya = _merge_parts(hf_ref[...], hb_ref[...], g_ref[...], z0_ref[...],
                                                   z1_ref[...], ybv, bg_ref[...])
        dya = dm * g0
        dh_ref[...] = (dya * gelu).astype(bf16)
        dg_ref[...] = (dya * hs * dgelu).astype(bf16)
        dyb_ref[...] = (dm * g1).astype(bf16)
        dz0 = (dm * ya) * (g0 * (1.0 - g0))
        dz1 = (dm * ybv) * (g1 * (1.0 - g1))
        dz = jnp.concatenate([dz0, dz1], axis=1)
        dz_ref[...] = dz.astype(bf16)
        dbg_ref[...] += jnp.sum(dz, axis=0, keepdims=True)

    row = pl.BlockSpec((tm, D), lambda i: (i, 0))
    return pl.pallas_call(
        body, name="outproj_bwd", grid=(S // tm,),
        in_specs=[row, pl.BlockSpec((D, D), lambda i: (0, 0)), row, row,
                  pl.BlockSpec((tm, D), lambda i: (i, C_G // D)),
                  pl.BlockSpec((tm, D), lambda i: (i, C_Z0 // D)),
                  pl.BlockSpec((tm, D), lambda i: (i, C_Z1 // D)),
                  row, pl.BlockSpec((1, 2 * D), lambda i: (0, 0))],
        out_specs=[row, row, pl.BlockSpec((tm, 2 * D), lambda i: (i, 0)), row,
                   pl.BlockSpec((1, 2 * D), lambda i: (0, 0))],
        out_shape=[jax.ShapeDtypeStruct((S, D), bf16), jax.ShapeDtypeStruct((S, D), bf16),
                   jax.ShapeDtypeStruct((S, 2 * D), bf16), jax.ShapeDtypeStruct((S, D), bf16),
                   jax.ShapeDtypeStruct((1, 2 * D), f32)],
        compiler_params=_cparams())(dx1b, w_outt, hf, hb, proj, proj, proj, yb, bg)


def _block_diag_groups(w):
    w4 = w.reshape(LRU_GROUPS, 4, LRU_BLOCK, LRU_BLOCK)
    eye = jnp.eye(4, dtype=w.dtype)
    return jnp.einsum("ghij,hk->ghikj", w4, eye).reshape(LRU_GROUPS, LRU_GW, LRU_GW)


def _diag_blocks(dw):
    d5 = dw.reshape(LRU_GROUPS, 4, LRU_BLOCK, 4, LRU_BLOCK)
    return jnp.stack([d5[:, h, :, h, :] for h in range(4)], axis=1).reshape(LRU_HEADS, LRU_BLOCK, LRU_BLOCK)


def _local_step(x, tgt, small, env, before=lambda name: (), after=lambda name, got: None):
    S = x.shape[0]
    g1, g2, g3 = small["norm_mix_g"], small["norm_ffn_g"], small["norm_final_g"]
    bg, cw, cb = small["b_gate"], small["conv_w"], small["conv_b"]
    sink = small["attn_sink"]

    wg = jnp.concatenate([_block_diag_groups(small["lru_wa"][0]), _block_diag_groups(small["lru_wx"][0]),
                          _block_diag_groups(small["lru_wa"][1]), _block_diag_groups(small["lru_wx"][1])],
                         axis=2).astype(bf16)
    wgt = jnp.swapaxes(wg, 1, 2)
    zeros5 = jnp.zeros((5, D), f32)
    lp = jnp.stack([jnp.concatenate([small["lru_lambda"][d:d + 1], small["lru_ba"][d:d + 1],
                                     small["lru_bx"][d:d + 1], zeros5], axis=0) for d in range(2)])

    def hosted(name, fn, *args, **kw):
        outs, got = fn(*args, comm=tuple(before(name)), **kw)
        after(name, got)
        return outs

    xn, proj = hosted("norm_inproj", _norm_matmul, x, g1, env["w_in_p"], "norm_inproj")
    uc = _conv_fwd(proj, cw, cb)
    (hf,) = hosted("lru_fwd", _lru_fwd, uc, wg, lp, False)
    (hb,), _ = _lru_fwd(uc, wg, lp, True)
    (yb,) = hosted("attn_fwd", _attn_fwd, proj, sink)
    merged, x1 = _merge_outproj(x, hf, hb, proj, yb, bg, env["w_out"])
    (xn2, gu), _ = _norm_matmul(x1, g2, env["w_fi"], "norm_ffn_in")
    ff, dx2, dx2b, loss, dg3 = _ffn_out_loss(gu, x1, env["w_fo"], g3, tgt)

    env["dw_fo"] = _mm_tn(ff, dx2b, "dw_ffn_out", tk=1408, tn=1024)
    dgt, dup = hosted("ffn_bwd1", _ffn_bwd1, dx2b, env["w_fo"].T, gu)
    env["dw_fi"] = jnp.concatenate([_mm_tn(xn2, dgt, "dw_ffn_in_gate", tk=1024, tn=1408),
                                    _mm_tn(xn2, dup, "dw_ffn_in_up", tk=1024, tn=1408)], axis=1)
    w_fit = env["w_fi"].T
    (dx1, dx1b, dg2), _ = _proj_bwd([dgt, dup], [w_fit[:D_FF], w_fit[D_FF:]], x1, g2, dx2, "ffn_in_bwd")
    env["dw_out"] = _mm_tn(merged, dx1b, "dw_out", tk=1024, tn=1024)
    dh, dgl, dz, dyb, dbg = _outproj_bwd(dx1b, env["w_out"].T, hf, hb, proj, yb, bg)
    dq, dk2, dv2, dsink = hosted("attn_bwd", _attn_bwd, proj, sink, dyb)
    dkv = jnp.concatenate([dk2[BLK:BLK + S], dv2[BLK:BLK + S]], axis=1).astype(bf16)
    duc_f, dwg_f, dp_f = hosted("lru_bwd", _lru_bwd, uc, dh, hf, wg, wgt, lp, False)
    (duc_b, dwg_b, dp_b), _ = _lru_bwd(uc, dh, hb, wg, wgt, lp, True)
    env["grads_early"] = {
        "loss": loss[:, :1], "b_gate": dbg,
        "lru_lambda": jnp.concatenate([dp_f[0:1], dp_b[0:1]], axis=0),
        "lru_wa": jnp.stack([_diag_blocks(dwg_f[:, :, :LRU_GW]), _diag_blocks(dwg_b[:, :, :LRU_GW])]),
        "lru_ba": jnp.concatenate([dp_f[1:2], dp_b[1:2]], axis=0),
        "lru_wx": jnp.stack([_diag_blocks(dwg_f[:, :, LRU_GW:]), _diag_blocks(dwg_b[:, :, LRU_GW:])]),
        "lru_bx": jnp.concatenate([dp_f[2:3], dp_b[2:3]], axis=0),
        "attn_sink": dsink[:, :N_HEADS], "norm_ffn_g": dg2, "norm_final_g": dg3,
    }
    du, dcw, dcb = hosted("conv_bwd", _conv_bwd, duc_f, duc_b, proj, cw)
    pieces = [du, dgl, dq, dz, dkv]
    bounds = [0, 1024, 2048, 3072, 5120, 5632]
    env["dw_in"] = _unperm_cols(jnp.concatenate(
        [_mm_tn(xn, p, "dw_in_%d" % i, tk=1024, tn=min(p.shape[1], 1024)) for i, p in enumerate(pieces)], axis=1))
    w_int = env["w_in_p"].T
    dx, _, dg1 = hosted("inproj_bwd", _proj_bwd, pieces, [w_int[bounds[i]:bounds[i + 1]] for i in range(5)],
                        x, g1, dx1, "inproj_bwd")

    grads = dict(env["grads_early"], norm_mix_g=dg1, conv_w=dcw, conv_b=dcb)
    return dx, grads


def _adamw(gparts, w, m, v, name, tr=256):
    n, rows, cols = gparts.shape
    tr = _div_tile(rows, tr)
    c1 = 1.0 - ADAM_B1 ** ADAM_STEP
    c2 = 1.0 - ADAM_B2 ** ADAM_STEP

    def body(g_ref, w_ref, m_ref, v_ref, go_ref, d_ref, mo_ref, vo_ref):
        g = g_ref[0].astype(f32)
        for j in range(1, n):
            g = g + g_ref[j].astype(f32)
        mn = ADAM_B1 * m_ref[...] + (1.0 - ADAM_B1) * g
        vn = ADAM_B2 * v_ref[...] + (1.0 - ADAM_B2) * (g * g)
        m_hat = mn / c1
        v_hat = vn / c2
        go_ref[...] = g
        d_ref[...] = -ADAM_LR * (m_hat / (jnp.sqrt(v_hat) + ADAM_EPS) + ADAM_WD * w_ref[...])
        mo_ref[...] = mn
        vo_ref[...] = vn

    blk = pl.BlockSpec((tr, cols), lambda i: (i, 0))
    shp = jax.ShapeDtypeStruct((rows, cols), f32)
    return pl.pallas_call(
        body, name=name, grid=(rows // tr,),
        in_specs=[pl.BlockSpec((n, tr, cols), lambda i: (0, i, 0)), blk, blk, blk],
        out_specs=[blk, blk, blk, blk], out_shape=[shp, shp, shp, shp],
        compiler_params=_cparams())(gparts, w, m, v)


def _sum_parts(parts, name):
    n, rows, cols = parts.shape

    def body(p_ref, o_ref):
        acc = p_ref[0].astype(f32)
        for j in range(1, n):
            acc = acc + p_ref[j].astype(f32)
        o_ref[...] = acc

    return pl.pallas_call(
        body, name=name, out_shape=jax.ShapeDtypeStruct((rows, cols), f32),
        compiler_params=_cparams())(parts)


def _pack_rows(arrs, dtype=f32):
    rows, spans, at = [], [], 0
    for a in arrs:
        flat = a.reshape(-1).astype(dtype)
        nr = -(-flat.shape[0] // 1024)
        rows.append(jnp.pad(flat, (0, nr * 1024 - flat.shape[0])).reshape(nr, 1024))
        spans.append((at, nr))
        at += nr
    pad = (-at) % 16
    if pad:
        rows.append(jnp.zeros((pad, 1024), dtype))
    return jnp.concatenate(rows, axis=0), spans


def _unpack_rows(packed, spans, shapes):
    out = []
    for (at, nr), shp in zip(spans, shapes):
        n = math.prod(shp)
        out.append(packed[at:at + nr].reshape(-1)[:n].reshape(shp))
    return out


BIG = ("w_in", "w_out", "w_ffn_in", "w_ffn_out")
SMALL_REPL = ("norm_mix_g", "b_gate", "conv_b", "lru_wa", "lru_wx", "attn_sink", "norm_ffn_g", "norm_final_g")
SMALL_SHARD = ("conv_w", "lru_lambda", "lru_ba", "lru_bx")
ORDER = ("norm_mix_g", "w_in", "b_gate", "conv_w", "conv_b", "lru_lambda", "lru_wa", "lru_ba", "lru_wx",
         "lru_bx", "attn_sink", "w_out", "norm_ffn_g", "w_ffn_in", "w_ffn_out", "norm_final_g")
EARLY_F32 = ("loss", "b_gate", "lru_lambda", "lru_ba", "lru_bx", "attn_sink", "norm_ffn_g", "norm_final_g")
EARLY_BF16 = ("lru_wa", "lru_wx")
LATE = ("norm_mix_g", "conv_w", "conv_b")


def kernel(x, norm_mix_g, w_in, b_gate, conv_w, conv_b, lru_lambda, lru_wa, lru_ba, lru_wx, lru_bx, attn_sink, w_out, norm_ffn_g, w_ffn_in, w_ffn_out, norm_final_g, loss_target, m_norm_mix_g, m_w_in, m_b_gate, m_conv_w, m_conv_b, m_lru_lambda, m_lru_wa, m_lru_ba, m_lru_wx, m_lru_bx, m_attn_sink, m_w_out, m_norm_ffn_g, m_w_ffn_in, m_w_ffn_out, m_norm_final_g, v_norm_mix_g, v_w_in, v_b_gate, v_conv_w, v_conv_b, v_lru_lambda, v_lru_wa, v_lru_ba, v_lru_wx, v_lru_bx, v_attn_sink, v_w_out, v_norm_ffn_g, v_w_ffn_in, v_w_ffn_out, v_norm_final_g):
    w = dict(norm_mix_g=norm_mix_g, w_in=w_in, b_gate=b_gate, conv_w=conv_w, conv_b=conv_b, lru_lambda=lru_lambda,
             lru_wa=lru_wa, lru_ba=lru_ba, lru_wx=lru_wx, lru_bx=lru_bx, attn_sink=attn_sink, w_out=w_out,
             norm_ffn_g=norm_ffn_g, w_ffn_in=w_ffn_in, w_ffn_out=w_ffn_out, norm_final_g=norm_final_g)
    m = dict(norm_mix_g=m_norm_mix_g, w_in=m_w_in, b_gate=m_b_gate, conv_w=m_conv_w, conv_b=m_conv_b,
             lru_lambda=m_lru_lambda, lru_wa=m_lru_wa, lru_ba=m_lru_ba, lru_wx=m_lru_wx, lru_bx=m_lru_bx,
             attn_sink=m_attn_sink, w_out=m_w_out, norm_ffn_g=m_norm_ffn_g, w_ffn_in=m_w_ffn_in,
             w_ffn_out=m_w_ffn_out, norm_final_g=m_norm_final_g)
    v = dict(norm_mix_g=v_norm_mix_g, w_in=v_w_in, b_gate=v_b_gate, conv_w=v_conv_w, conv_b=v_conv_b,
             lru_lambda=v_lru_lambda, lru_wa=v_lru_wa, lru_ba=v_lru_ba, lru_wx=v_lru_wx, lru_bx=v_lru_bx,
             attn_sink=v_attn_sink, w_out=v_w_out, norm_ffn_g=v_norm_ffn_g, w_ffn_in=v_w_ffn_in,
             w_ffn_out=v_w_ffn_out, norm_final_g=v_norm_final_g)
    me = 4 * lax.axis_index("x") + 2 * lax.axis_index("y") + lax.axis_index("c")

    def cols_full(got):
        return jnp.swapaxes(got, 0, 1).reshape(got.shape[1], -1)

    def cols_parts(g):
        return jnp.swapaxes(g.reshape(g.shape[0], N_DEV, -1), 0, 1)

    def rows_parts(g):
        return g.reshape(N_DEV, -1, g.shape[1])

    shard_rows = jnp.concatenate([w[n][0] for n in SMALL_SHARD], axis=0)
    got_w_in, got_rows = _exchange([(w_in[0].astype(bf16), False), (shard_rows, False)], "gather_w_in")
    full_rows = cols_full(got_rows)
    small = {n: w[n] for n in ("norm_mix_g", "b_gate", "conv_b", "attn_sink", "norm_ffn_g")}
    small["lru_wa"], small["lru_wx"] = lru_wa[0], lru_wx[0]
    small["norm_final_g"] = norm_final_g.reshape(1, D)
    small["conv_w"], small["lru_lambda"] = full_rows[0:4], full_rows[4:6]
    small["lru_ba"], small["lru_bx"] = full_rows[6:8], full_rows[8:10]

    env = {"w_in_p": _perm_cols(cols_full(got_w_in))}
    recv = {}

    def before(name):
        if name == "norm_inproj":
            return [(w_out[0].astype(bf16), False), (w_ffn_out[0].astype(bf16), False)]
        if name == "lru_fwd":
            return [(w_ffn_in[0].astype(bf16), False)]
        if name == "ffn_bwd1":
            return [(rows_parts(env["dw_fo"]).astype(bf16), True)]
        if name == "attn_bwd":
            return [(rows_parts(env["dw_out"]).astype(bf16), True)]
        if name == "lru_bwd":
            return [(cols_parts(env["dw_fi"]).astype(bf16), True)]
        if name == "conv_bwd":
            ge = env["grads_early"]
            p32, env["early_f32_spans"] = _pack_rows([ge[n] for n in EARLY_F32])
            p16, env["early_bf16_spans"] = _pack_rows([ge[n] for n in EARLY_BF16], bf16)
            return [(p32, False), (p16, False)]
        if name == "inproj_bwd":
            return [(cols_parts(env["dw_in"]).astype(bf16), True)]
        return []

    def after(name, got):
        if name == "norm_inproj":
            env["w_out"], env["w_fo"] = got[0].reshape(D, D), got[1].reshape(D_FF, D)
        elif name == "lru_fwd":
            env["w_fi"] = cols_full(got[0])
        elif name == "ffn_bwd1":
            recv["w_ffn_out"] = got[0]
        elif name == "attn_bwd":
            recv["w_out"] = got[0]
        elif name == "lru_bwd":
            recv["w_ffn_in"] = got[0]
        elif name == "conv_bwd":
            recv["early_f32"], recv["early_bf16"] = got
        elif name == "inproj_bwd":
            recv["w_in"] = got[0]

    grad_x, grads = _local_step(x[0], loss_target[0], small, env, before, after)

    outs = {}
    for name in BIG:
        shard_shape = recv[name].shape[1:]
        r2 = lambda a: a.reshape(shard_shape)
        res = _adamw(recv[name], r2(w[name]), r2(m[name]), r2(v[name]), "adamw_" + name)
        outs[name] = [t.reshape(w[name].shape) for t in res]

    small_names = SMALL_REPL + SMALL_SHARD
    late_packed, late_spans = _pack_rows([grads[n] for n in LATE])
    (got_late,) = _exchange([(late_packed, False)], "gather_late_grads")
    summed = {}
    for names, got, spans, tag in ((EARLY_F32, recv["early_f32"], env["early_f32_spans"], "early_f32"),
                                   (EARLY_BF16, recv["early_bf16"], env["early_bf16_spans"], "early_bf16"),
                                   (LATE, got_late, late_spans, "late")):
        total = _sum_parts(got, "sum_small_" + tag)
        summed.update(zip(names, _unpack_rows(total, spans, [grads[n].shape for n in names])))
    loss = summed["loss"].reshape(())
    gsm = {n: summed[n].reshape(w[n].shape) for n in SMALL_REPL}
    for n in SMALL_SHARD:
        full = summed[n]
        gsm[n] = lax.dynamic_slice_in_dim(full, me * 128, 128, axis=1).reshape(w[n].shape)
    pk = lambda dct: _pack_rows([dct[n] for n in small_names])[0]
    gp, sp = _pack_rows([gsm[n] for n in small_names])
    res = _adamw(gp[None], pk(w), pk(m), pk(v), "adamw_small")
    sshapes = [w[n].shape for n in small_names]
    for idx, t in enumerate(res):
        for n, a in zip(small_names, _unpack_rows(t, sp, sshapes)):
            outs.setdefault(n, [None] * 4)[idx] = a

    result = [loss, grad_x[None]]
    for idx in range(4):
        result += [outs[n][idx] for n in ORDER]
    return tuple(result)
```

```python
import functools
import math

import jax
import jax.numpy as jnp
from jax import lax
from jax.experimental import pallas as pl
from jax.experimental.pallas import tpu as pltpu

f32 = jnp.float32
bf16 = jnp.bfloat16

D = 1024
D_FF = 2816
IN_W = 5632
N_HEADS = 16
N_KV = 4
HEAD_DIM = 64
WINDOW = 128
BLK = 128
LRU_HEADS = 16
LRU_BLOCK = 64
LRU_GROUPS = 4
LRU_GW = 256
LRU_CHUNK = 128
LRU_ROWS = 512
RGLRU_C = 8.0
EPS = 1e-6
NEG_INF = -1e30
N_DEV = 8

ADAM_LR = 0.001
ADAM_B1 = 0.9
ADAM_B2 = 0.999
ADAM_EPS = 1e-08
ADAM_WD = 0.01
ADAM_STEP = 10

VMEM_MB = 56

C_U, C_G, C_Q, C_Z0, C_Z1, C_K, C_V = 0, 1024, 2048, 3072, 4096, 5120, 5376


def _cparams(vmem_mb=VMEM_MB):
    return pltpu.CompilerParams(vmem_limit_bytes=vmem_mb << 20)


def _div_tile(n, pref):
    if n <= pref:
        return n
    return max(t for t in range(8, pref + 1, 8) if n % t == 0)


def _perm_cols(w):
    return jnp.concatenate([w[:, :3072], w[:, 3584:5632], w[:, 3072:3584]], axis=1)


def _unperm_cols(w):
    return jnp.concatenate([w[:, :3072], w[:, 5120:5632], w[:, 3072:5120]], axis=1)


def _sigmoid(x):
    return 1.0 / (1.0 + jnp.exp(-x))


def _sigmoid_t(x):
    return 0.5 * jnp.tanh(0.5 * x) + 0.5


def _log1p(x):
    u = 1.0 + x
    d = u - 1.0
    return jnp.where(d == 0.0, x, jnp.log(u) * (x / jnp.where(d == 0.0, 1.0, d)))


def _softplus(x):
    return jnp.maximum(x, 0.0) + _log1p(jnp.exp(-jnp.abs(x)))


def _gelu_and_grad(x):
    c = math.sqrt(2.0 / math.pi)
    inner = c * (x + 0.044715 * (x * x * x))
    t = jnp.tanh(inner)
    gelu = 0.5 * x * (1.0 + t)
    dinner = c * (1.0 + 3 * 0.044715 * (x * x))
    dgelu = 0.5 * (1.0 + t) + 0.5 * x * (1.0 - t * t) * dinner
    return gelu, dgelu


def _rms_bwd(dn, xv, g):
    r = lax.rsqrt(jnp.mean(xv * xv, axis=-1, keepdims=True) + EPS)
    xh = xv * r
    dxh = dn * g
    dx = r * (dxh - xh * jnp.mean(dxh * xh, axis=-1, keepdims=True))
    return dx, dn * xh


ANY_SPEC = pl.BlockSpec(memory_space=pl.ANY)


def _comm_out_shape(src, scatter):
    return jax.ShapeDtypeStruct((N_DEV, *(src.shape[1:] if scatter else src.shape)), src.dtype)


def _comm_sems():
    return [pltpu.SemaphoreType.DMA((N_DEV - 1,)), pltpu.SemaphoreType.DMA((N_DEV - 1,)), pltpu.SemaphoreType.DMA]


def _scatter_descs(src_ref, out_ref, send_sems, recv_sems, local_sem):
    x, y, c = lax.axis_index("x"), lax.axis_index("y"), lax.axis_index("c")
    me = 4 * x + 2 * y + c
    descs = [pltpu.make_async_copy(src_ref.at[me], out_ref.at[me], local_sem)]
    for k in range(1, N_DEV):
        px, py, pc = x ^ (k >> 2), y ^ ((k >> 1) & 1), c ^ (k & 1)
        descs.append(pltpu.make_async_remote_copy(
            src_ref=src_ref.at[4 * px + 2 * py + pc], dst_ref=out_ref.at[me],
            send_sem=send_sems.at[k - 1], recv_sem=recv_sems.at[k - 1],
            device_id=(px, py, pc), device_id_type=pl.DeviceIdType.MESH))
    return descs


def _gather_copies(src_ref, out_ref, send_sems, recv_sems, local_sem, starting):
    x, y, c = lax.axis_index("x"), lax.axis_index("y"), lax.axis_index("c")
    me, sibling = (x, y, c), (x, y, 1 - c)
    chips = [(1 - x, y), (x, 1 - y), (1 - x, 1 - y)]

    def slot(px, py, pc):
        return out_ref.at[4 * px + 2 * py + pc]

    def copy(k, block, to, src=None):
        return pltpu.make_async_remote_copy(
            src_ref=slot(*block) if src is None else src, dst_ref=slot(*block),
            send_sem=send_sems.at[k], recv_sem=recv_sems.at[k], device_id=to, device_id_type=pl.DeviceIdType.MESH)

    local = pltpu.make_async_copy(src_ref, slot(*me), local_sem)
    first = [copy(0, me, sibling, src=src_ref)] + [copy(1 + j, me, (*chip, c), src=src_ref)
                                                    for j, chip in enumerate(chips)]
    if starting:
        return local, first
    passed = [copy(4 + j, (*chip, c), sibling) for j, chip in enumerate(chips)]
    landed = [copy(1 + j, (*chip, c), me) for j, chip in enumerate(chips)]
    later = [copy(0, sibling, me)] + [copy(4 + j, (*chip, 1 - c), me) for j, chip in enumerate(chips)]
    return local, first, passed, landed, later


def _comm_start(src_ref, out_ref, sems, scatter):
    if scatter:
        for d in _scatter_descs(src_ref, out_ref, *sems):
            d.start()
    else:
        local, first = _gather_copies(src_ref, out_ref, *sems, starting=True)
        local.start()
        for cp in first:
            cp.start()


def _comm_finish(src_ref, out_ref, sems, scatter):
    if scatter:
        for d in _scatter_descs(src_ref, out_ref, *sems):
            d.wait()
    else:
        local, first, passed, landed, later = _gather_copies(src_ref, out_ref, *sems, starting=False)
        for arrived, onward in zip(landed, passed):
            arrived.wait_recv()
            onward.start()
        for cp in later:
            cp.wait_recv()
        for cp in first + passed:
            cp.wait_send()
        local.wait()


def _exchange(comm, name):
    nc = len(comm)

    def body(*refs):
        srcs, outs, sems = refs[:nc], refs[nc:2 * nc], refs[2 * nc:]
        for i in range(nc):
            _comm_start(srcs[i], outs[i], sems[3 * i:3 * i + 3], comm[i][1])
        for i in range(nc):
            _comm_finish(srcs[i], outs[i], sems[3 * i:3 * i + 3], comm[i][1])

    return pl.pallas_call(
        body, name=name, in_specs=[ANY_SPEC] * nc, out_specs=[ANY_SPEC] * nc,
        out_shape=[_comm_out_shape(*c) for c in comm],
        scratch_shapes=[s for _ in comm for s in _comm_sems()],
    )(*[c[0] for c in comm])


def _hosted_call(body, *, name, grid, in_specs, out_specs, out_shape, args, scratch_shapes=(), comm=()):
    nin, nout, nscr, nc = len(in_specs), len(out_specs), len(scratch_shapes), len(comm)

    def wrapped(*refs):
        ins = refs[:nin]
        csrc = refs[nin:nin + nc]
        outs = refs[nin + nc:nin + nc + nout]
        cout = refs[nin + nc + nout:nin + 2 * nc + nout]
        scr = refs[nin + 2 * nc + nout:]
        sems = scr[nscr:]

        if nc:
            first = functools.reduce(jnp.logical_and, [pl.program_id(a) == 0 for a in range(len(grid))])

            @pl.when(first)
            def _():
                for i in range(nc):
                    _comm_start(csrc[i], cout[i], sems[3 * i:3 * i + 3], comm[i][1])

        body(*ins, *outs, *scr[:nscr])

        if nc:
            last = functools.reduce(jnp.logical_and, [pl.program_id(a) == grid[a] - 1 for a in range(len(grid))])

            @pl.when(last)
            def _():
                for i in range(nc):
                    _comm_finish(csrc[i], cout[i], sems[3 * i:3 * i + 3], comm[i][1])

    res = pl.pallas_call(
        wrapped, name=name, grid=grid,
        in_specs=[*in_specs, *[ANY_SPEC] * nc], out_specs=[*out_specs, *[ANY_SPEC] * nc],
        out_shape=[*out_shape, *[_comm_out_shape(*c) for c in comm]],
        scratch_shapes=[*scratch_shapes, *[s for _ in comm for s in _comm_sems()]],
        compiler_params=_cparams())(*args, *[c[0] for c in comm])
    return res[:nout], res[nout:]


def _norm_matmul(x, g, w, name, tm=1024, tn=1408, comm=()):
    S, dm = x.shape
    n = w.shape[1]
    tm = min(tm, S)

    def body(x_ref, g_ref, w_ref, xn_ref, o_ref):
        @pl.when(pl.program_id(1) == 0)
        def _():
            xv = x_ref[...]
            r = lax.rsqrt(jnp.mean(xv * xv, axis=-1, keepdims=True) + EPS)
            xn_ref[...] = ((xv * r) * g_ref[...]).astype(bf16)

        o_ref[...] = jnp.dot(xn_ref[...], w_ref[...], preferred_element_type=f32).astype(bf16)

    return _hosted_call(
        body, name=name, grid=(S // tm, n // tn),
        in_specs=[pl.BlockSpec((tm, dm), lambda i, j: (i, 0)),
                  pl.BlockSpec((1, dm), lambda i, j: (0, 0)),
                  pl.BlockSpec((dm, tn), lambda i, j: (0, j))],
        out_specs=[pl.BlockSpec((tm, dm), lambda i, j: (i, 0)),
                   pl.BlockSpec((tm, tn), lambda i, j: (i, j))],
        out_shape=[jax.ShapeDtypeStruct((S, dm), bf16), jax.ShapeDtypeStruct((S, n), bf16)],
        args=(x, g, w), comm=comm)


def _mm_tn(a, b, name, tk, tn, tmc=2048):
    m, ka = a.shape
    n = b.shape[1]
    tmc = min(tmc, m)
    nk = m // tmc

    def body(a_ref, b_ref, o_ref, acc_ref):
        k = pl.program_id(2)
        part = lax.dot_general(a_ref[...], b_ref[...], (((0,), (0,)), ((), ())), preferred_element_type=f32)

        @pl.when(k == 0)
        def _():
            acc_ref[...] = part

        @pl.when(k > 0)
        def _():
            acc_ref[...] += part

        @pl.when(k == nk - 1)
        def _():
            o_ref[...] = acc_ref[...].astype(bf16)

    return pl.pallas_call(
        body, name=name, grid=(ka // tk, n // tn, nk),
        in_specs=[pl.BlockSpec((tmc, tk), lambda i, j, k: (k, i)),
                  pl.BlockSpec((tmc, tn), lambda i, j, k: (k, j))],
        out_specs=pl.BlockSpec((tk, tn), lambda i, j, k: (i, j)),
        out_shape=jax.ShapeDtypeStruct((ka, n), bf16),
        scratch_shapes=[pltpu.VMEM((tk, tn), f32)],
        compiler_params=_cparams())(a, b)


HALO = 16


def _rows_at(ext, o, tc):
    if o == 0:
        return ext[HALO:HALO + tc]
    return pltpu.roll(ext, (-o) % ext.shape[0], 0)[HALO:HALO + tc]


def _halo_specs(tc, S, width, col):
    per = tc // HALO
    last = S // HALO - 1
    return (pl.BlockSpec((tc, width), lambda i: (i, col)),
            pl.BlockSpec((HALO, width), lambda i: (jnp.maximum(i * per - 1, 0), col)),
            pl.BlockSpec((HALO, width), lambda i: (jnp.minimum((i + 1) * per, last), col)))


def _extended(cur_ref, prev_ref, next_ref, i, nsteps):
    prev = jnp.where(i > 0, prev_ref[...].astype(f32), 0.0)
    nxt = jnp.where(i < nsteps - 1, next_ref[...].astype(f32), 0.0)
    return jnp.concatenate([prev, cur_ref[...].astype(f32), nxt], axis=0)


def _conv_fwd(proj, cw, cb, tc=512):
    S = proj.shape[0]
    tc = min(tc, S)
    nsteps = S // tc

    def body(cur_ref, prev_ref, next_ref, w_ref, b_ref, o_ref):
        ext = _extended(cur_ref, prev_ref, next_ref, pl.program_id(0), nsteps)
        acc = _rows_at(ext, -2, tc) * w_ref[0:1, :]
        for k in range(1, 4):
            acc = acc + _rows_at(ext, k - 2, tc) * w_ref[k:k + 1, :]
        o_ref[...] = acc + b_ref[...]

    return pl.pallas_call(
        body, name="conv_fwd", grid=(nsteps,),
        in_specs=[*_halo_specs(tc, S, D, 0),
                  pl.BlockSpec((4, D), lambda i: (0, 0)), pl.BlockSpec((1, D), lambda i: (0, 0))],
        out_specs=pl.BlockSpec((tc, D), lambda i: (i, 0)),
        out_shape=jax.ShapeDtypeStruct((S, D), f32),
        compiler_params=_cparams())(proj, proj, proj, cw, cb)


def _conv_bwd(duc_f, duc_b, proj, cw, tc=512, comm=()):
    S = proj.shape[0]
    tc = min(tc, S)
    nsteps = S // tc

    def body(fc, fp, fn, bc, bp, bn, uc_, up, un, w_ref, du_ref, dw_ref, db_ref):
        i = pl.program_id(0)

        @pl.when(i == 0)
        def _():
            dw_ref[...] = jnp.zeros_like(dw_ref)
            db_ref[...] = jnp.zeros_like(db_ref)

        dext = _extended(fc, fp, fn, i, nsteps) + _extended(bc, bp, bn, i, nsteps)
        uext = _extended(uc_, up, un, i, nsteps)
        d = dext[HALO:HALO + tc]
        acc = _rows_at(dext, 2, tc) * w_ref[0:1, :]
        for k in range(1, 4):
            acc = acc + _rows_at(dext, 2 - k, tc) * w_ref[k:k + 1, :]
        du_ref[...] = acc.astype(bf16)
        wrow = lax.broadcasted_iota(jnp.int32, (4, D), 0)
        for k in range(4):
            dw_ref[...] += jnp.where(wrow == k, jnp.sum(d * _rows_at(uext, k - 2, tc), axis=0, keepdims=True), 0.0)
        db_ref[...] += jnp.sum(d, axis=0, keepdims=True)

    return _hosted_call(
        body, name="conv_bwd", grid=(nsteps,),
        in_specs=[*_halo_specs(tc, S, D, 0), *_halo_specs(tc, S, D, 0), *_halo_specs(tc, S, D, 0),
                  pl.BlockSpec((4, D), lambda i: (0, 0))],
        out_specs=[pl.BlockSpec((tc, D), lambda i: (i, 0)),
                   pl.BlockSpec((4, D), lambda i: (0, 0)), pl.BlockSpec((1, D), lambda i: (0, 0))],
        out_shape=[jax.ShapeDtypeStruct((S, D), bf16), jax.ShapeDtypeStruct((4, D), f32),
                   jax.ShapeDtypeStruct((1, D), f32)],
        args=(duc_f, duc_f, duc_f, duc_b, duc_b, duc_b, proj, proj, proj, cw), comm=comm)


def _scan_scratch():
    halves = [pltpu.VMEM((LRU_CHUNK, 128), f32) for _ in range(2 * (LRU_GW // 128))]
    return [*halves, pltpu.VMEM((LRU_CHUNK // 8, LRU_GW), f32), pltpu.VMEM((LRU_CHUNK // 8, LRU_GW), f32)]


def _log_scan(a, b, row, n, reverse, steps):
    for s in steps:
        shift = a.shape[0] - s if reverse else s
        keep = (row < n - s) if reverse else (row >= s)
        a_sh = pltpu.roll(a, shift, 0)
        b_sh = pltpu.roll(b, shift, 0)
        b = jnp.where(keep, a * b_sh + b, b)
        a = jnp.where(keep, a * a_sh, a)
    return a, b


def _scan_chunk(a, b, carry, reverse, *scratch):
    tc, w = a.shape
    ng = tc // 8
    nl = w // 128
    sa_refs, sb_refs, sc_ref, st_ref = scratch[:nl], scratch[nl:2 * nl], scratch[2 * nl], scratch[2 * nl + 1]
    sub = lax.broadcasted_iota(jnp.int32, (8, w), 0)
    ag, bg = [], []
    for k in range(ng):
        ak, bk = _log_scan(a[8 * k:8 * k + 8], b[8 * k:8 * k + 8], sub, 8, reverse, (1, 2, 4))
        ag.append(ak)
        bg.append(bk)
    a = jnp.concatenate(ag, axis=0)
    b = jnp.concatenate(bg, axis=0)
    edge = 0 if reverse else 7
    for i in range(nl):
        sa_refs[i][...] = a[:, 128 * i:128 * (i + 1)]
        sb_refs[i][...] = b[:, 128 * i:128 * (i + 1)]
    ta = jnp.concatenate([r[pl.ds(edge, ng, stride=8), :] for r in sa_refs], axis=1)
    tb = jnp.concatenate([r[pl.ds(edge, ng, stride=8), :] for r in sb_refs], axis=1)
    grow = lax.broadcasted_iota(jnp.int32, (ng, w), 0)
    ta, tb = _log_scan(ta, tb, grow, ng, reverse, [1 << i for i in range(ng.bit_length() - 1)])
    state = tb + ta * carry
    st_ref[...] = state
    if reverse:
        sc_ref[...] = jnp.where(grow == ng - 1, carry, pltpu.roll(state, ng - 1, 0))
    else:
        sc_ref[...] = jnp.where(grow == 0, carry, pltpu.roll(state, 1, 0))
    h = jnp.concatenate([bg[k] + ag[k] * sc_ref[k:k + 1, :] for k in range(ng)], axis=0)
    return h, (st_ref[0:1, :] if reverse else st_ref[ng - 1:ng, :])


def _lru_gates(uc, w, p_ref):
    pre = jnp.dot(uc.astype(bf16), w, preferred_element_type=f32)
    r = _sigmoid_t(pre[:, :LRU_GW] + p_ref[0, 1:2, :])
    gi = _sigmoid_t(pre[:, LRU_GW:] + p_ref[0, 2:3, :])
    sp = _softplus(-p_ref[0, 0:1, :])
    log_a = -RGLRU_C * r * sp
    a = jnp.exp(log_a)
    x = 2.0 * log_a
    series = -x * (1.0 + x * (0.5 + x * (1.0 / 6 + x * (1.0 / 24))))
    beta = jnp.sqrt(jnp.maximum(jnp.where(x > -0.0625, series, 1.0 - a * a), 0.0))
    return r, gi, sp, a, beta


def _lru_fwd(uc, wg, lp, reverse, comm=()):
    S = uc.shape[0]
    tc = LRU_CHUNK
    rows = min(LRU_ROWS, S)
    nsub = rows // tc
    nblk = S // rows
    d = 1 if reverse else 0

    def bidx(c):
        return nblk - 1 - c if reverse else c

    def body(uc_ref, w_ref, p_ref, h_ref, carry_ref, *scan_scratch):
        @pl.when(pl.program_id(1) == 0)
        def _():
            carry_ref[...] = jnp.zeros_like(carry_ref)

        carry = carry_ref[...]
        for j in (reversed(range(nsub)) if reverse else range(nsub)):
            sl = slice(j * tc, (j + 1) * tc)
            ucv = uc_ref[sl, :]
            _, gi, _, a, beta = _lru_gates(ucv, w_ref[0], p_ref)
            h, carry = _scan_chunk(a, beta * (gi * ucv), carry, reverse, *scan_scratch)
            h_ref[sl, :] = h.astype(bf16)
        carry_ref[...] = carry

    return _hosted_call(
        body, name="lru_fwd_rev" if reverse else "lru_fwd", grid=(LRU_GROUPS, nblk),
        in_specs=[pl.BlockSpec((rows, LRU_GW), lambda g, c: (bidx(c), g)),
                  pl.BlockSpec((1, LRU_GW, 2 * LRU_GW), lambda g, c: (g, 0, d)),
                  pl.BlockSpec((1, 8, LRU_GW), lambda g, c: (d, 0, g))],
        out_specs=[pl.BlockSpec((rows, LRU_GW), lambda g, c: (bidx(c), g))],
        out_shape=[jax.ShapeDtypeStruct((S, D), bf16)],
        scratch_shapes=[pltpu.VMEM((1, LRU_GW), f32), *_scan_scratch()],
        args=(uc, wg, lp), comm=comm)


def _lru_bwd(uc, dh, h, wg, wgt, lp, reverse, comm=()):
    S = uc.shape[0]
    tc = LRU_CHUNK
    rows = min(LRU_ROWS, S)
    nsub = rows // tc
    nblk = S // rows
    d = 1 if reverse else 0
    per = rows // HALO
    last8 = S // HALO - 1

    def bidx(c):
        return c if reverse else nblk - 1 - c

    def halo_idx(c):
        if reverse:
            return jnp.minimum((bidx(c) + 1) * per, last8)
        return jnp.maximum(bidx(c) * per - 1, 0)

    def body(uc_ref, dh_ref, h_ref, halo_ref, w_ref, wt_ref, p_ref, duc_ref, dw_ref, dp_ref, carry_ref, tmp_ref,
             *scan_scratch):
        c = pl.program_id(1)
        bi = bidx(c)

        @pl.when(c == 0)
        def _():
            carry_ref[...] = jnp.zeros_like(carry_ref)
            dw_ref[...] = jnp.zeros_like(dw_ref)
            dp_ref[...] = jnp.zeros_like(dp_ref)

        row = lax.broadcasted_iota(jnp.int32, (tc, LRU_GW), 0)
        carry = carry_ref[...]
        dw = jnp.zeros((LRU_GW, 2 * LRU_GW), f32)
        dsp = jnp.zeros((1, LRU_GW), f32)
        dba = jnp.zeros((1, LRU_GW), f32)
        dbx = jnp.zeros((1, LRU_GW), f32)
        for j in (range(nsub) if reverse else reversed(range(nsub))):
            sl = slice(j * tc, (j + 1) * tc)
            ucv = uc_ref[sl, :]
            ucb = ucv.astype(bf16)
            r, gi, sp, a, beta = _lru_gates(ucv, w_ref[0], p_ref)
            hv = h_ref[sl, :].astype(f32)
            dhv = dh_ref[sl, :].astype(f32)
            if reverse:
                alpha = jnp.where(row == 0, 1.0, pltpu.roll(a, 1, 0))
                gsc, _ = _scan_chunk(alpha, dhv, carry, False, *scan_scratch)
                if j < nsub - 1:
                    edge = h_ref[(j + 1) * tc:(j + 1) * tc + HALO, :].astype(f32)[0:1, :]
                else:
                    edge = jnp.where(bi < nblk - 1, halo_ref[...].astype(f32)[0:1, :], 0.0)
                h_nb = jnp.where(row == tc - 1, edge, pltpu.roll(hv, tc - 1, 0))
            else:
                alpha = jnp.where(row == tc - 1, 1.0, pltpu.roll(a, tc - 1, 0))
                gsc, _ = _scan_chunk(alpha, dhv, carry, True, *scan_scratch)
                if j > 0:
                    edge = h_ref[j * tc - HALO:j * tc, :].astype(f32)[HALO - 1:HALO, :]
                else:
                    edge = jnp.where(bi > 0, halo_ref[...].astype(f32)[HALO - 1:HALO, :], 0.0)
                h_nb = jnp.where(row == 0, edge, pltpu.roll(hv, 1, 0))
            tmp_ref[...] = a * gsc
            carry = tmp_ref[tc - 1:tc, :] if reverse else tmp_ref[0:1, :]

            da = gsc * h_nb
            dbeta = gsc * (gi * ucv)
            dl = da * a - dbeta * (a * a) / beta
            dr = dl * (-RGLRU_C * sp)
            dsp = dsp + jnp.sum(dl * (-RGLRU_C * r), axis=0, keepdims=True)
            dgi = gsc * beta * ucv
            dpre_r = dr * r * (1.0 - r)
            dpre_i = dgi * gi * (1.0 - gi)
            dba = dba + jnp.sum(dpre_r, axis=0, keepdims=True)
            dbx = dbx + jnp.sum(dpre_i, axis=0, keepdims=True)
            dpre = jnp.concatenate([dpre_r, dpre_i], axis=1).astype(bf16)
            duc_ref[sl, :] = (gsc * beta * gi + jnp.dot(dpre, wt_ref[0], preferred_element_type=f32)).astype(bf16)
            dw = dw + lax.dot_general(ucb, dpre, (((0,), (0,)), ((), ())), preferred_element_type=f32)
        carry_ref[...] = carry
        dw_ref[0] += dw
        dlam = dsp * (-_sigmoid(-p_ref[0, 0:1, :]))
        prow = lax.broadcasted_iota(jnp.int32, (8, LRU_GW), 0)
        dp_ref[...] += (jnp.where(prow == 0, dlam, 0.0) + jnp.where(prow == 1, dba, 0.0)
                        + jnp.where(prow == 2, dbx, 0.0))

    chunk = pl.BlockSpec((rows, LRU_GW), lambda g, c: (bidx(c), g))
    return _hosted_call(
        body, name="lru_bwd_rev" if reverse else "lru_bwd", grid=(LRU_GROUPS, nblk),
        in_specs=[chunk, chunk, chunk,
                  pl.BlockSpec((HALO, LRU_GW), lambda g, c: (halo_idx(c), g)),
                  pl.BlockSpec((1, LRU_GW, 2 * LRU_GW), lambda g, c: (g, 0, d)),
                  pl.BlockSpec((1, 2 * LRU_GW, LRU_GW), lambda g, c: (g, d, 0)),
                  pl.BlockSpec((1, 8, LRU_GW), lambda g, c: (d, 0, g))],
        out_specs=[chunk,
                   pl.BlockSpec((1, LRU_GW, 2 * LRU_GW), lambda g, c: (g, 0, 0)),
                   pl.BlockSpec((8, LRU_GW), lambda g, c: (0, g))],
        out_shape=[jax.ShapeDtypeStruct((S, D), bf16),
                   jax.ShapeDtypeStruct((LRU_GROUPS, LRU_GW, 2 * LRU_GW), f32),
                   jax.ShapeDtypeStruct((8, D), f32)],
        scratch_shapes=[pltpu.VMEM((1, LRU_GW), f32), pltpu.VMEM((tc, LRU_GW), f32), *_scan_scratch()],
        args=(uc, dh, h, h, wg, wgt, lp), comm=comm)


def _slope(h):
    return 2.0 ** (-8.0 * (h + 1.0) / N_HEADS)


def _kv_specs(nb, col):
    return [pl.BlockSpec((BLK, N_KV * HEAD_DIM), lambda n: (jnp.maximum(n - 1, 0), col)),
            pl.BlockSpec((BLK, N_KV * HEAD_DIM), lambda n: (n, col)),
            pl.BlockSpec((BLK, N_KV * HEAD_DIM), lambda n: (jnp.minimum(n + 1, nb - 1), col))]


def _dup_windows(r0, r1, r2):
    left = lax.broadcasted_iota(jnp.int32, (3 * BLK, 128), 1) < HEAD_DIM
    win = jnp.concatenate([r0[...], r1[...], r2[...]], axis=0)
    out = []
    for i in range(N_KV // 2):
        t = win[:, i * 128:(i + 1) * 128]
        r = pltpu.roll(t, HEAD_DIM, 1)
        out += [jnp.where(left, t, r).astype(bf16), jnp.where(left, r, t).astype(bf16)]
    return out


def _attn_bias_init(bias_ref):
    k_loc = lax.broadcasted_iota(jnp.int32, (3 * BLK, BLK), 0)
    q_loc = lax.broadcasted_iota(jnp.int32, (3 * BLK, BLK), 1)
    adist = jnp.abs(q_loc + BLK - k_loc)
    adf = adist.astype(f32)
    for e in range(3):
        ok = adist <= WINDOW
        if e == 0:
            ok = ok & (k_loc >= BLK)
        if e == 2:
            ok = ok & (k_loc < 2 * BLK)
        for kv in range(N_KV):
            bias_ref[e, kv] = jnp.concatenate(
                [jnp.where(ok, (-_slope(4 * kv + j)) * adf, NEG_INF) for j in range(4)], axis=1)


def _stack_heads(ref, kv, scale):
    left = lax.broadcasted_iota(jnp.int32, (BLK, 128), 1) < HEAD_DIM
    rows = []
    for pp in range(2):
        t = ref[:, (2 * kv + pp) * 128:(2 * kv + pp + 1) * 128]
        if scale != 1.0:
            t = t * scale
        zero = jnp.zeros_like(t)
        rows += [jnp.where(left, t, zero).astype(bf16), jnp.where(left, zero, t).astype(bf16)]
    return jnp.concatenate(rows, axis=0)


def _attn_softmax(qs, k2, bias, sink_ref, kv):
    sink = jnp.concatenate([jnp.full((1, BLK), sink_ref[0, 4 * kv + j], f32) for j in range(4)], axis=1)
    s = lax.dot_general(k2, qs, (((1,), (1,)), ((), ())), preferred_element_type=f32) + bias
    m = jnp.maximum(jnp.max(s, axis=0, keepdims=True), sink)
    p = jnp.exp(s - m)
    ps = jnp.exp(sink - m)
    inv = 1.0 / (jnp.sum(p, axis=0, keepdims=True) + ps)
    return p, ps, inv


def _pair_tiles(t):
    return [jnp.concatenate([t[:HEAD_DIM, 256 * pp:256 * pp + 128],
                             t[HEAD_DIM:, 256 * pp + 128:256 * pp + 256]], axis=0).T for pp in range(2)]


def _attn_fwd(proj, sink, comm=()):
    S = proj.shape[0]
    nb = S // BLK
    assert nb >= 2

    def body(q_ref, k0, k1, k2_, v0, v1, v2_, sink_ref, o_ref, bias_ref):
        n = pl.program_id(0)

        @pl.when(n == 0)
        def _():
            _attn_bias_init(bias_ref)

        e = jnp.where(n == 0, 0, jnp.where(n == nb - 1, 2, 1))
        kk = _dup_windows(k0, k1, k2_)
        vv = _dup_windows(v0, v1, v2_)
        tiles = []
        for kv in range(N_KV):
            qs = _stack_heads(q_ref, kv, HEAD_DIM ** -0.5)
            p, _, inv = _attn_softmax(qs, kk[kv], bias_ref[e, kv], sink_ref, kv)
            ot = lax.dot_general(vv[kv], p.astype(bf16), (((0,), (0,)), ((), ())), preferred_element_type=f32)
            tiles += _pair_tiles(ot * inv)
        o_ref[...] = jnp.concatenate(tiles, axis=1).astype(bf16)

    return _hosted_call(
        body, name="attn_fwd", grid=(nb,),
        in_specs=[pl.BlockSpec((BLK, D), lambda n: (n, C_Q // D)),
                  *_kv_specs(nb, C_K // (N_KV * HEAD_DIM)), *_kv_specs(nb, C_V // (N_KV * HEAD_DIM)),
                  pl.BlockSpec(memory_space=pltpu.SMEM)],
        out_specs=[pl.BlockSpec((BLK, D), lambda n: (n, 0))],
        out_shape=[jax.ShapeDtypeStruct((S, D), bf16)],
        scratch_shapes=[pltpu.VMEM((3, N_KV, 3 * BLK, 4 * BLK), f32)],
        args=(proj, proj, proj, proj, proj, proj, proj, sink), comm=comm)


def _attn_bwd(proj, sink, dyb, comm=()):
    S = proj.shape[0]
    nb = S // BLK
    assert nb >= 2

    def body(q_ref, k0, k1, k2_, v0, v1, v2_, sink_ref, do_ref, dq_ref, dk_out, dv_out, ds_ref,
             bias_ref, dk_ref, dv_ref, dsk_ref):
        n = pl.program_id(0)

        @pl.when(n == 0)
        def _():
            _attn_bias_init(bias_ref)
            dk_ref[...] = jnp.zeros_like(dk_ref)
            dv_ref[...] = jnp.zeros_like(dv_ref)
            dsk_ref[...] = jnp.zeros_like(dsk_ref)

        e = jnp.where(n == 0, 0, jnp.where(n == nb - 1, 2, 1))
        kk = _dup_windows(k0, k1, k2_)
        vv = _dup_windows(v0, v1, v2_)
        left3 = lax.broadcasted_iota(jnp.int32, (3 * BLK, 128), 1) < HEAD_DIM
        start = pl.multiple_of(n * BLK, BLK)
        dq_tiles, dks, dvs = [], [], []
        for kv in range(N_KV):
            qs = _stack_heads(q_ref, kv, HEAD_DIM ** -0.5)
            dos = _stack_heads(do_ref, kv, 1.0)
            p, ps, inv = _attn_softmax(qs, kk[kv], bias_ref[e, kv], sink_ref, kv)
            pn = p * inv
            dp = lax.dot_general(vv[kv], dos, (((1,), (1,)), ((), ())), preferred_element_type=f32)
            delta = jnp.sum(pn * dp, axis=0, keepdims=True)
            dsc = (pn * (dp - delta)).astype(bf16)
            dsk_ref[kv:kv + 1, :] += delta * (ps * inv)
            dqt = lax.dot_general(kk[kv], dsc, (((0,), (0,)), ((), ())), preferred_element_type=f32)
            dq_tiles += _pair_tiles(dqt * (HEAD_DIM ** -0.5))
            dk = jnp.dot(dsc, qs, preferred_element_type=f32)
            dv = jnp.dot(pn.astype(bf16), dos, preferred_element_type=f32)
            dks.append(dk + pltpu.roll(dk, HEAD_DIM, 1))
            dvs.append(dv + pltpu.roll(dv, HEAD_DIM, 1))
        for jp in range(N_KV // 2):
            cols = slice(jp * 128, (jp + 1) * 128)
            dk_ref[pl.ds(start, 3 * BLK), cols] += jnp.where(left3, dks[2 * jp], dks[2 * jp + 1])
            dv_ref[pl.ds(start, 3 * BLK), cols] += jnp.where(left3, dvs[2 * jp], dvs[2 * jp + 1])
        dq_ref[...] = jnp.concatenate(dq_tiles, axis=1).astype(bf16)

        @pl.when(n == nb - 1)
        def _():
            pltpu.sync_copy(dk_ref, dk_out)
            pltpu.sync_copy(dv_ref, dv_out)
            lane = lax.broadcasted_iota(jnp.int32, (1, 128), 1)
            dsink = jnp.zeros((1, 128), f32)
            for h in range(N_HEADS):
                part = dsk_ref[h // 4:h // 4 + 1, (h % 4) * BLK:(h % 4 + 1) * BLK]
                dsink = dsink + jnp.where(lane == h, -jnp.sum(part), 0.0)
            ds_ref[...] = dsink

    acc = jax.ShapeDtypeStruct((S + 2 * BLK, N_KV * HEAD_DIM), f32)
    return _hosted_call(
        body, name="attn_bwd", grid=(nb,),
        in_specs=[pl.BlockSpec((BLK, D), lambda n: (n, C_Q // D)),
                  *_kv_specs(nb, C_K // (N_KV * HEAD_DIM)), *_kv_specs(nb, C_V // (N_KV * HEAD_DIM)),
                  pl.BlockSpec(memory_space=pltpu.SMEM),
                  pl.BlockSpec((BLK, D), lambda n: (n, 0))],
        out_specs=[pl.BlockSpec((BLK, D), lambda n: (n, 0)), ANY_SPEC, ANY_SPEC,
                   pl.BlockSpec((1, 128), lambda n: (0, 0))],
        out_shape=[jax.ShapeDtypeStruct((S, D), bf16), acc, acc, jax.ShapeDtypeStruct((1, 128), f32)],
        scratch_shapes=[pltpu.VMEM((3, N_KV, 3 * BLK, 4 * BLK), f32), pltpu.VMEM(acc.shape, f32),
                        pltpu.VMEM(acc.shape, f32), pltpu.VMEM((8, 4 * BLK), f32)],
        args=(proj, proj, proj, proj, proj, proj, proj, sink, dyb), comm=comm)


def _merge_parts(hf, hb, g, z0, z1, yb, bg):
    g0 = _sigmoid(z0.astype(f32) + bg[:, :D])
    g1 = _sigmoid(z1.astype(f32) + bg[:, D:])
    gelu, dgelu = _gelu_and_grad(g.astype(f32))
    hs = hf.astype(f32) + hb.astype(f32)
    ya = hs * gelu
    return g0, g1, gelu, dgelu, hs, ya


def _merge_outproj(x, hf, hb, proj, yb, bg, w_out, tm=512):
    S = x.shape[0]
    tm = min(tm, S)

    def body(x_ref, hf_ref, hb_ref, g_ref, z0_ref, z1_ref, yb_ref, bg_ref, w_ref, mg_ref, x1_ref):
        ybv = yb_ref[...].astype(f32)
        g0, g1, _, _, _, ya = _merge_parts(hf_ref[...], hb_ref[...], g_ref[...], z0_ref[...], z1_ref[...],
                                           ybv, bg_ref[...])
        mg = (g0 * ya + g1 * ybv).astype(bf16)
        mg_ref[...] = mg
        x1_ref[...] = x_ref[...] + jnp.dot(mg, w_ref[...], preferred_element_type=f32)

    row = pl.BlockSpec((tm, D), lambda i: (i, 0))
    return pl.pallas_call(
        body, name="merge_outproj", grid=(S // tm,),
        in_specs=[row, row, row,
                  pl.BlockSpec((tm, D), lambda i: (i, C_G // D)),
                  pl.BlockSpec((tm, D), lambda i: (i, C_Z0 // D)),
                  pl.BlockSpec((tm, D), lambda i: (i, C_Z1 // D)),
                  row, pl.BlockSpec((1, 2 * D), lambda i: (0, 0)), pl.BlockSpec((D, D), lambda i: (0, 0))],
        out_specs=[row, row],
        out_shape=[jax.ShapeDtypeStruct((S, D), bf16), jax.ShapeDtypeStruct((S, D), f32)],
        compiler_params=_cparams())(x, hf, hb, proj, proj, proj, yb, bg, w_out)


def _ffn_out_loss(gu, x1, w_fo, g3, tgt, tm=256):
    S = x1.shape[0]
    tm = min(tm, S)

    def body(gt_ref, up_ref, x1_ref, w_ref, g_ref, t_ref, ff_ref, dx_ref, dxb_ref, loss_ref, dg_ref):
        @pl.when(pl.program_id(0) == 0)
        def _():
            loss_ref[...] = jnp.zeros_like(loss_ref)
            dg_ref[...] = jnp.zeros_like(dg_ref)

        gt = gt_ref[...].astype(f32)
        ff = ((gt * _sigmoid(gt)) * up_ref[...].astype(f32)).astype(bf16)
        ff_ref[...] = ff
        x2 = x1_ref[...] + jnp.dot(ff, w_ref[...], preferred_element_type=f32)
        gv = g_ref[...]
        r = lax.rsqrt(jnp.mean(x2 * x2, axis=-1, keepdims=True) + EPS)
        xh = x2 * r
        diff = xh * gv - t_ref[...]
        loss_ref[...] += (0.5 / D) * jnp.sum(diff * diff)
        dy = diff * (1.0 / D)
        dg_ref[...] += jnp.sum(dy * xh, axis=0, keepdims=True)
        dxh = dy * gv
        dx = r * (dxh - xh * jnp.mean(dxh * xh, axis=-1, keepdims=True))
        dx_ref[...] = dx
        dxb_ref[...] = dx.astype(bf16)

    row = pl.BlockSpec((tm, D), lambda i: (i, 0))
    vec = pl.BlockSpec((1, D), lambda i: (0, 0))
    return pl.pallas_call(
        body, name="ffn_out_loss", grid=(S // tm,),
        in_specs=[pl.BlockSpec((tm, D_FF), lambda i: (i, 0)), pl.BlockSpec((tm, D_FF), lambda i: (i, 1)),
                  row, pl.BlockSpec((D_FF, D), lambda i: (0, 0)), vec, row],
        out_specs=[pl.BlockSpec((tm, D_FF), lambda i: (i, 0)), row, row,
                   pl.BlockSpec((1, 128), lambda i: (0, 0)), vec],
        out_shape=[jax.ShapeDtypeStruct((S, D_FF), bf16), jax.ShapeDtypeStruct((S, D), f32),
                   jax.ShapeDtypeStruct((S, D), bf16), jax.ShapeDtypeStruct((1, 128), f32),
                   jax.ShapeDtypeStruct((1, D), f32)],
        compiler_params=_cparams())(gu, gu, x1, w_fo, g3, tgt)


def _ffn_bwd1(dx2b, w_fo, gu, tm=256, comm=()):
    S = dx2b.shape[0]
    tm = min(tm, S)

    def body(dx_ref, w_ref, gt_ref, up_ref, dgt_ref, dup_ref):
        dff = lax.dot_general(dx_ref[...], w_ref[...], (((1,), (1,)), ((), ())), preferred_element_type=f32)
        gt = gt_ref[...].astype(f32)
        sg = _sigmoid(gt)
        dup_ref[...] = (dff * (gt * sg)).astype(bf16)
        dgt_ref[...] = ((dff * up_ref[...].astype(f32)) * (sg * (1.0 + gt * (1.0 - sg)))).astype(bf16)

    wide = pl.BlockSpec((tm, D_FF), lambda i: (i, 0))
    return _hosted_call(
        body, name="ffn_bwd1", grid=(S // tm,),
        in_specs=[pl.BlockSpec((tm, D), lambda i: (i, 0)), pl.BlockSpec((D_FF, D), lambda i: (0, 0)),
                  wide, pl.BlockSpec((tm, D_FF), lambda i: (i, 1))],
        out_specs=[wide, wide],
        out_shape=[jax.ShapeDtypeStruct((S, D_FF), bf16), jax.ShapeDtypeStruct((S, D_FF), bf16)],
        args=(dx2b, w_fo, gu, gu), comm=comm)


def _proj_bwd(pieces, w, xres, g, dres, name, tm=256, comm=()):
    S = xres.shape[0]
    tm = min(tm, S)
    np_ = len(pieces)

    def body(*refs):
        p_refs = refs[:np_]
        w_refs = refs[np_:2 * np_]
        x_ref, g_ref, dres_ref, dx_ref, dxb_ref, dg_ref = refs[2 * np_:]

        @pl.when(pl.program_id(0) == 0)
        def _():
            dg_ref[...] = jnp.zeros_like(dg_ref)

        nt = (((1,), (1,)), ((), ()))
        dn = lax.dot_general(p_refs[0][...], w_refs[0][...], nt, preferred_element_type=f32)
        for pr, wr in zip(p_refs[1:], w_refs[1:]):
            dn = dn + lax.dot_general(pr[...], wr[...], nt, preferred_element_type=f32)
        dxn, dgc = _rms_bwd(dn, x_ref[...], g_ref[...])
        dx = dres_ref[...] + dxn
        dx_ref[...] = dx
        dxb_ref[...] = dx.astype(bf16)
        dg_ref[...] += jnp.sum(dgc, axis=0, keepdims=True)

    row = pl.BlockSpec((tm, D), lambda i: (i, 0))
    vec = pl.BlockSpec((1, D), lambda i: (0, 0))
    return _hosted_call(
        body, name=name, grid=(S // tm,),
        in_specs=[*[pl.BlockSpec((tm, wd), functools.partial(lambda i, cb: (i, cb), cb=acb))
                    for _, acb, _, wd in pieces],
                  *[pl.BlockSpec((D, wd), functools.partial(lambda i, cb: (0, cb), cb=wcb))
                    for _, _, wcb, wd in pieces],
                  row, vec, row],
        out_specs=[row, row, vec],
        out_shape=[jax.ShapeDtypeStruct((S, D), f32), jax.ShapeDtypeStruct((S, D), bf16),
                   jax.ShapeDtypeStruct((1, D), f32)],
        args=(*[p[0] for p in pieces], *[w] * np_, xres, g, dres), comm=comm)


def _outproj_bwd(dx1b, w_out, hf, hb, proj, yb, bg, tm=512):
    S = dx1b.shape[0]
    tm = min(tm, S)

    def body(dx_ref, w_ref, hf_ref, hb_ref, g_ref, z0_ref, z1_ref, yb_ref, bg_ref,
             dh_ref, dg_ref, dz_ref, dyb_ref, dbg_ref):
        @pl.when(pl.program_id(0) == 0)
        def _():
            dbg_ref[...] = jnp.zeros_like(dbg_ref)

        dm = lax.dot_general(dx_ref[...], w_ref[...], (((1,), (1,)), ((), ())), preferred_element_type=f32)
        ybv = yb_ref[...].astype(f32)
        g0, g1, gelu, dgelu, hs, ya = _merge_parts(hf_ref[...], hb_ref[...], g_ref[...], z0_ref[...],
                                                   z1_ref[...], ybv, bg_ref[...])
        dya = dm * g0
        dh_ref[...] = (dya * gelu).astype(bf16)
        dg_ref[...] = (dya * hs * dgelu).astype(bf16)
        dyb_ref[...] = (dm * g1).astype(bf16)
        dz0 = (dm * ya) * (g0 * (1.0 - g0))
        dz1 = (dm * ybv) * (g1 * (1.0 - g1))
        dz = jnp.concatenate([dz0, dz1], axis=1)
        dz_ref[...] = dz.astype(bf16)
        dbg_ref[...] += jnp.sum(dz, axis=0, keepdims=True)

    row = pl.BlockSpec((tm, D), lambda i: (i, 0))
    return pl.pallas_call(
        body, name="outproj_bwd", grid=(S // tm,),
        in_specs=[row, pl.BlockSpec((D, D), lambda i: (0, 0)), row, row,
                  pl.BlockSpec((tm, D), lambda i: (i, C_G // D)),
                  pl.BlockSpec((tm, D), lambda i: (i, C_Z0 // D)),
                  pl.BlockSpec((tm, D), lambda i: (i, C_Z1 // D)),
                  row, pl.BlockSpec((1, 2 * D), lambda i: (0, 0))],
        out_specs=[row, row, pl.BlockSpec((tm, 2 * D), lambda i: (i, 0)), row,
                   pl.BlockSpec((1, 2 * D), lambda i: (0, 0))],
        out_shape=[jax.ShapeDtypeStruct((S, D), bf16), jax.ShapeDtypeStruct((S, D), bf16),
                   jax.ShapeDtypeStruct((S, 2 * D), bf16), jax.ShapeDtypeStruct((S, D), bf16),
                   jax.ShapeDtypeStruct((1, 2 * D), f32)],
        compiler_params=_cparams())(dx1b, w_out, hf, hb, proj, proj, proj, yb, bg)


def _block_diag_groups(w):
    w4 = w.reshape(LRU_GROUPS, 4, LRU_BLOCK, LRU_BLOCK)
    eye = jnp.eye(4, dtype=w.dtype)
    return jnp.einsum("ghij,hk->ghikj", w4, eye).reshape(LRU_GROUPS, LRU_GW, LRU_GW)


def _diag_blocks(dw):
    d5 = dw.reshape(LRU_GROUPS, 4, LRU_BLOCK, 4, LRU_BLOCK)
    return jnp.stack([d5[:, h, :, h, :] for h in range(4)], axis=1).reshape(LRU_HEADS, LRU_BLOCK, LRU_BLOCK)


def _local_step(x, tgt, small, env, before=lambda name: (), after=lambda name, got: None):
    S = x.shape[0]
    g1, g2, g3 = small["norm_mix_g"], small["norm_ffn_g"], small["norm_final_g"]
    bg, cw, cb = small["b_gate"], small["conv_w"], small["conv_b"]
    sink = small["attn_sink"]

    wg = jnp.concatenate([_block_diag_groups(small["lru_wa"][0]), _block_diag_groups(small["lru_wx"][0]),
                          _block_diag_groups(small["lru_wa"][1]), _block_diag_groups(small["lru_wx"][1])],
                         axis=2).astype(bf16)
    wgt = jnp.swapaxes(wg, 1, 2)
    zeros5 = jnp.zeros((5, D), f32)
    lp = jnp.stack([jnp.concatenate([small["lru_lambda"][d:d + 1], small["lru_ba"][d:d + 1],
                                     small["lru_bx"][d:d + 1], zeros5], axis=0) for d in range(2)])

    def hosted(name, fn, *args, **kw):
        outs, got = fn(*args, comm=tuple(before(name)), **kw)
        after(name, got)
        return outs

    xn, proj = hosted("norm_inproj", _norm_matmul, x, g1, env["w_in_p"], "norm_inproj")
    uc = _conv_fwd(proj, cw, cb)
    (hf,) = hosted("lru_fwd", _lru_fwd, uc, wg, lp, False)
    (hb,), _ = _lru_fwd(uc, wg, lp, True)
    (yb,) = hosted("attn_fwd", _attn_fwd, proj, sink)
    merged, x1 = _merge_outproj(x, hf, hb, proj, yb, bg, env["w_out"])
    (xn2, gu), _ = _norm_matmul(x1, g2, env["w_fi"], "norm_ffn_in")
    ff, dx2, dx2b, loss, dg3 = _ffn_out_loss(gu, x1, env["w_fo"], g3, tgt)

    env["dw_fo"] = _mm_tn(ff, dx2b, "dw_ffn_out", tk=1408, tn=1024)
    dgt, dup = hosted("ffn_bwd1", _ffn_bwd1, dx2b, env["w_fo"], gu)
    env["dw_fi"] = jnp.concatenate([_mm_tn(xn2, dgt, "dw_ffn_in_gate", tk=1024, tn=1408),
                                    _mm_tn(xn2, dup, "dw_ffn_in_up", tk=1024, tn=1408)], axis=1)
    (dx1, dx1b, dg2), _ = _proj_bwd([(dgt, 0, 0, D_FF), (dup, 0, 1, D_FF)], env["w_fi"], x1, g2, dx2, "ffn_in_bwd")
    env["dw_out"] = _mm_tn(merged, dx1b, "dw_out", tk=1024, tn=1024)
    dh, dgl, dz, dyb, dbg = _outproj_bwd(dx1b, env["w_out"], hf, hb, proj, yb, bg)
    dq, dk2, dv2, dsink = hosted("attn_bwd", _attn_bwd, proj, sink, dyb)
    dkv = jnp.concatenate([dk2[BLK:BLK + S], dv2[BLK:BLK + S]], axis=1).astype(bf16)
    duc_f, dwg_f, dp_f = hosted("lru_bwd", _lru_bwd, uc, dh, hf, wg, wgt, lp, False)
    (duc_b, dwg_b, dp_b), _ = _lru_bwd(uc, dh, hb, wg, wgt, lp, True)
    env["grads_early"] = {
        "loss": loss[:, :1], "b_gate": dbg,
        "lru_lambda": jnp.concatenate([dp_f[0:1], dp_b[0:1]], axis=0),
        "lru_wa": jnp.stack([_diag_blocks(dwg_f[:, :, :LRU_GW]), _diag_blocks(dwg_b[:, :, :LRU_GW])]),
        "lru_ba": jnp.concatenate([dp_f[1:2], dp_b[1:2]], axis=0),
        "lru_wx": jnp.stack([_diag_blocks(dwg_f[:, :, LRU_GW:]), _diag_blocks(dwg_b[:, :, LRU_GW:])]),
        "lru_bx": jnp.concatenate([dp_f[2:3], dp_b[2:3]], axis=0),
        "attn_sink": dsink[:, :N_HEADS], "norm_ffn_g": dg2, "norm_final_g": dg3,
    }
    du, dcw, dcb = hosted("conv_bwd", _conv_bwd, duc_f, duc_b, proj, cw)
    pieces = [du, dgl, dq, dz, dkv]
    env["dw_in"] = _unperm_cols(jnp.concatenate(
        [_mm_tn(xn, p, "dw_in_%d" % i, tk=1024, tn=min(p.shape[1], 1024)) for i, p in enumerate(pieces)], axis=1))
    col_pieces = [(du, 0, C_U // D, D), (dgl, 0, C_G // D, D), (dq, 0, C_Q // D, D), (dz, 0, C_Z0 // D, D),
                  (dz, 1, C_Z1 // D, D), (dkv, 0, C_K // 512, 512)]
    dx, _, dg1 = hosted("inproj_bwd", _proj_bwd, col_pieces, env["w_in_p"], x, g1, dx1, "inproj_bwd")

    grads = dict(env["grads_early"], norm_mix_g=dg1, conv_w=dcw, conv_b=dcb)
    return dx, grads


def _adamw(gparts, w, m, v, name, tr=256):
    n, rows, cols = gparts.shape
    tr = _div_tile(rows, tr)
    c1 = 1.0 - ADAM_B1 ** ADAM_STEP
    c2 = 1.0 - ADAM_B2 ** ADAM_STEP

    def body(g_ref, w_ref, m_ref, v_ref, go_ref, d_ref, mo_ref, vo_ref):
        g = g_ref[0].astype(f32)
        for j in range(1, n):
            g = g + g_ref[j].astype(f32)
        mn = ADAM_B1 * m_ref[0] + (1.0 - ADAM_B1) * g
        vn = ADAM_B2 * v_ref[0] + (1.0 - ADAM_B2) * (g * g)
        m_hat = mn / c1
        v_hat = vn / c2
        go_ref[0] = g
        d_ref[0] = -ADAM_LR * (m_hat / (jnp.sqrt(v_hat) + ADAM_EPS) + ADAM_WD * w_ref[0])
        mo_ref[0] = mn
        vo_ref[0] = vn

    blk = pl.BlockSpec((1, tr, cols), lambda i: (0, i, 0))
    shp = jax.ShapeDtypeStruct((1, rows, cols), f32)
    return pl.pallas_call(
        body, name=name, grid=(rows // tr,),
        in_specs=[pl.BlockSpec((n, tr, cols), lambda i: (0, i, 0)), blk, blk, blk],
        out_specs=[blk, blk, blk, blk], out_shape=[shp, shp, shp, shp],
        compiler_params=_cparams())(gparts, w, m, v)


def _sum_parts(parts, name):
    n, rows, cols = parts.shape

    def body(p_ref, o_ref):
        acc = p_ref[0].astype(f32)
        for j in range(1, n):
            acc = acc + p_ref[j].astype(f32)
        o_ref[...] = acc

    return pl.pallas_call(
        body, name=name, out_shape=jax.ShapeDtypeStruct((rows, cols), f32),
        compiler_params=_cparams())(parts)


def _pack_rows(arrs, dtype=f32):
    rows, spans, at = [], [], 0
    for a in arrs:
        flat = a.reshape(-1).astype(dtype)
        nr = -(-flat.shape[0] // 1024)
        rows.append(jnp.pad(flat, (0, nr * 1024 - flat.shape[0])).reshape(nr, 1024))
        spans.append((at, nr))
        at += nr
    pad = (-at) % 16
    if pad:
        rows.append(jnp.zeros((pad, 1024), dtype))
    return jnp.concatenate(rows, axis=0), spans


def _unpack_rows(packed, spans, shapes):
    out = []
    for (at, nr), shp in zip(spans, shapes):
        n = math.prod(shp)
        out.append(packed[at:at + nr].reshape(-1)[:n].reshape(shp))
    return out


BIG = ("w_in", "w_out", "w_ffn_in", "w_ffn_out")
SMALL_REPL = ("norm_mix_g", "b_gate", "conv_b", "lru_wa", "lru_wx", "attn_sink", "norm_ffn_g", "norm_final_g")
SMALL_SHARD = ("conv_w", "lru_lambda", "lru_ba", "lru_bx")
ORDER = ("norm_mix_g", "w_in", "b_gate", "conv_w", "conv_b", "lru_lambda", "lru_wa", "lru_ba", "lru_wx",
         "lru_bx", "attn_sink", "w_out", "norm_ffn_g", "w_ffn_in", "w_ffn_out", "norm_final_g")
EARLY_F32 = ("loss", "b_gate", "lru_lambda", "lru_ba", "lru_bx", "attn_sink", "norm_ffn_g", "norm_final_g")
EARLY_BF16 = ("lru_wa", "lru_wx")
LATE = ("norm_mix_g", "conv_w", "conv_b")


def kernel(x, norm_mix_g, w_in, b_gate, conv_w, conv_b, lru_lambda, lru_wa, lru_ba, lru_wx, lru_bx, attn_sink, w_out, norm_ffn_g, w_ffn_in, w_ffn_out, norm_final_g, loss_target, m_norm_mix_g, m_w_in, m_b_gate, m_conv_w, m_conv_b, m_lru_lambda, m_lru_wa, m_lru_ba, m_lru_wx, m_lru_bx, m_attn_sink, m_w_out, m_norm_ffn_g, m_w_ffn_in, m_w_ffn_out, m_norm_final_g, v_norm_mix_g, v_w_in, v_b_gate, v_conv_w, v_conv_b, v_lru_lambda, v_lru_wa, v_lru_ba, v_lru_wx, v_lru_bx, v_attn_sink, v_w_out, v_norm_ffn_g, v_w_ffn_in, v_w_ffn_out, v_norm_final_g):
    w = dict(norm_mix_g=norm_mix_g, w_in=w_in, b_gate=b_gate, conv_w=conv_w, conv_b=conv_b, lru_lambda=lru_lambda,
             lru_wa=lru_wa, lru_ba=lru_ba, lru_wx=lru_wx, lru_bx=lru_bx, attn_sink=attn_sink, w_out=w_out,
             norm_ffn_g=norm_ffn_g, w_ffn_in=w_ffn_in, w_ffn_out=w_ffn_out, norm_final_g=norm_final_g)
    m = dict(norm_mix_g=m_norm_mix_g, w_in=m_w_in, b_gate=m_b_gate, conv_w=m_conv_w, conv_b=m_conv_b,
             lru_lambda=m_lru_lambda, lru_wa=m_lru_wa, lru_ba=m_lru_ba, lru_wx=m_lru_wx, lru_bx=m_lru_bx,
             attn_sink=m_attn_sink, w_out=m_w_out, norm_ffn_g=m_norm_ffn_g, w_ffn_in=m_w_ffn_in,
             w_ffn_out=m_w_ffn_out, norm_final_g=m_norm_final_g)
    v = dict(norm_mix_g=v_norm_mix_g, w_in=v_w_in, b_gate=v_b_gate, conv_w=v_conv_w, conv_b=v_conv_b,
             lru_lambda=v_lru_lambda, lru_wa=v_lru_wa, lru_ba=v_lru_ba, lru_wx=v_lru_wx, lru_bx=v_lru_bx,
             attn_sink=v_attn_sink, w_out=v_w_out, norm_ffn_g=v_norm_ffn_g, w_ffn_in=v_w_ffn_in,
             w_ffn_out=v_w_ffn_out, norm_final_g=v_norm_final_g)
    me = 4 * lax.axis_index("x") + 2 * lax.axis_index("y") + lax.axis_index("c")

    def cols_full(got):
        return jnp.swapaxes(got, 0, 1).reshape(got.shape[1], -1)

    def cols_parts(g):
        return jnp.swapaxes(g.reshape(g.shape[0], N_DEV, -1), 0, 1)

    def rows_parts(g):
        return g.reshape(N_DEV, -1, g.shape[1])

    shard_rows = jnp.concatenate([w[n][0] for n in SMALL_SHARD], axis=0)
    got_w_in, got_rows = _exchange([(w_in[0].astype(bf16), False), (shard_rows, False)], "gather_w_in")
    full_rows = cols_full(got_rows)
    small = {n: w[n] for n in ("norm_mix_g", "b_gate", "conv_b", "attn_sink", "norm_ffn_g")}
    small["lru_wa"], small["lru_wx"] = lru_wa[0], lru_wx[0]
    small["norm_final_g"] = norm_final_g.reshape(1, D)
    small["conv_w"], small["lru_lambda"] = full_rows[0:4], full_rows[4:6]
    small["lru_ba"], small["lru_bx"] = full_rows[6:8], full_rows[8:10]

    env = {"w_in_p": _perm_cols(cols_full(got_w_in))}
    recv = {}

    def before(name):
        if name == "norm_inproj":
            return [(w_out[0].astype(bf16), False), (w_ffn_out[0].astype(bf16), False)]
        if name == "lru_fwd":
            return [(w_ffn_in[0].astype(bf16), False)]
        if name == "ffn_bwd1":
            return [(rows_parts(env["dw_fo"]).astype(bf16), True)]
        if name == "attn_bwd":
            return [(rows_parts(env["dw_out"]).astype(bf16), True)]
        if name == "lru_bwd":
            return [(cols_parts(env["dw_fi"]).astype(bf16), True)]
        if name == "conv_bwd":
            ge = env["grads_early"]
            p32, env["early_f32_spans"] = _pack_rows([ge[n] for n in EARLY_F32])
            p16, env["early_bf16_spans"] = _pack_rows([ge[n] for n in EARLY_BF16], bf16)
            return [(p32, False), (p16, False)]
        if name == "inproj_bwd":
            return [(cols_parts(env["dw_in"]).astype(bf16), True)]
        return []

    def after(name, got):
        if name == "norm_inproj":
            env["w_out"], env["w_fo"] = got[0].reshape(D, D), got[1].reshape(D_FF, D)
        elif name == "lru_fwd":
            env["w_fi"] = cols_full(got[0])
        elif name == "ffn_bwd1":
            recv["w_ffn_out"] = got[0]
        elif name == "attn_bwd":
            recv["w_out"] = got[0]
        elif name == "lru_bwd":
            recv["w_ffn_in"] = got[0]
        elif name == "conv_bwd":
            recv["early_f32"], recv["early_bf16"] = got
        elif name == "inproj_bwd":
            recv["w_in"] = got[0]

    grad_x, grads = _local_step(x[0], loss_target[0], small, env, before, after)

    outs = {}
    for name in BIG:
        outs[name] = _adamw(recv[name], w[name], m[name], v[name], "adamw_" + name)

    small_names = SMALL_REPL + SMALL_SHARD
    late_packed, late_spans = _pack_rows([grads[n] for n in LATE])
    (got_late,) = _exchange([(late_packed, False)], "gather_late_grads")
    summed = {}
    for names, got, spans, tag in ((EARLY_F32, recv["early_f32"], env["early_f32_spans"], "early_f32"),
                                   (EARLY_BF16, recv["early_bf16"], env["early_bf16_spans"], "early_bf16"),
                                   (LATE, got_late, late_spans, "late")):
        total = _sum_parts(got, "sum_small_" + tag)
        summed.update(zip(names, _unpack_rows(total, spans, [grads[n].shape for n in names])))
    loss = summed["loss"].reshape(())
    gsm = {n: summed[n].reshape(w[n].shape) for n in SMALL_REPL}
    for n in SMALL_SHARD:
        full = summed[n]
        gsm[n] = lax.dynamic_slice_in_dim(full, me * 128, 128, axis=1).reshape(w[n].shape)
    pk = lambda dct: _pack_rows([dct[n] for n in small_names])[0]
    gp, sp = _pack_rows([gsm[n] for n in small_names])
    res = _adamw(gp[None], pk(w)[None], pk(m)[None], pk(v)[None], "adamw_small")
    sshapes = [w[n].shape for n in small_names]
    for idx, t in enumerate(res):
        for n, a in zip(small_names, _unpack_rows(t[0], sp, sshapes)):
            outs.setdefault(n, [None] * 4)[idx] = a

    result = [loss, grad_x[None]]
    for idx in range(4):
        result += [outs[n][idx] for n in ORDER]
    return tuple(result)
```

```python
import functools
import math

import jax
import jax.numpy as jnp
from jax import lax
from jax.experimental import pallas as pl
from jax.experimental.pallas import tpu as pltpu

f32 = jnp.float32
bf16 = jnp.bfloat16

D = 1024
D_FF = 2816
IN_W = 5632
N_HEADS = 16
N_KV = 4
HEAD_DIM = 64
WINDOW = 128
BLK = 128
LRU_HEADS = 16
LRU_BLOCK = 64
LRU_GROUPS = 4
LRU_GW = 256
LRU_CHUNK = 128
LRU_ROWS = 512
RGLRU_C = 8.0
EPS = 1e-6
NEG_INF = -1e30
N_DEV = 8

ADAM_LR = 0.001
ADAM_B1 = 0.9
ADAM_B2 = 0.999
ADAM_EPS = 1e-08
ADAM_WD = 0.01
ADAM_STEP = 10

VMEM_MB = 56

C_U, C_G, C_Q, C_Z0, C_Z1, C_K, C_V = 0, 1024, 2048, 3072, 4096, 5120, 5376


def _cparams(vmem_mb=VMEM_MB):
    return pltpu.CompilerParams(vmem_limit_bytes=vmem_mb << 20)


def _div_tile(n, pref):
    if n <= pref:
        return n
    return max(t for t in range(8, pref + 1, 8) if n % t == 0)


def _perm_cols(w):
    return jnp.concatenate([w[:, :3072], w[:, 3584:5632], w[:, 3072:3584]], axis=1)


def _sigmoid(x):
    return 1.0 / (1.0 + jnp.exp(-x))


def _sigmoid_t(x):
    return 0.5 * jnp.tanh(0.5 * x) + 0.5


def _log1p(x):
    u = 1.0 + x
    d = u - 1.0
    return jnp.where(d == 0.0, x, jnp.log(u) * (x / jnp.where(d == 0.0, 1.0, d)))


def _softplus(x):
    return jnp.maximum(x, 0.0) + _log1p(jnp.exp(-jnp.abs(x)))


def _gelu_and_grad(x):
    c = math.sqrt(2.0 / math.pi)
    inner = c * (x + 0.044715 * (x * x * x))
    t = jnp.tanh(inner)
    gelu = 0.5 * x * (1.0 + t)
    dinner = c * (1.0 + 3 * 0.044715 * (x * x))
    dgelu = 0.5 * (1.0 + t) + 0.5 * x * (1.0 - t * t) * dinner
    return gelu, dgelu


def _rms_bwd(dn, xv, g):
    r = lax.rsqrt(jnp.mean(xv * xv, axis=-1, keepdims=True) + EPS)
    xh = xv * r
    dxh = dn * g
    dx = r * (dxh - xh * jnp.mean(dxh * xh, axis=-1, keepdims=True))
    return dx, dn * xh


ANY_SPEC = pl.BlockSpec(memory_space=pl.ANY)


def _comm_out_shape(src, scatter):
    return jax.ShapeDtypeStruct((N_DEV, *(src.shape[1:] if scatter else src.shape)), src.dtype)


def _comm_sems():
    return [pltpu.SemaphoreType.DMA((N_DEV - 1,)), pltpu.SemaphoreType.DMA((N_DEV - 1,)), pltpu.SemaphoreType.DMA]


def _scatter_descs(src_ref, out_ref, send_sems, recv_sems, local_sem):
    x, y, c = lax.axis_index("x"), lax.axis_index("y"), lax.axis_index("c")
    me = 4 * x + 2 * y + c
    descs = [pltpu.make_async_copy(src_ref.at[me], out_ref.at[me], local_sem)]
    for k in range(1, N_DEV):
        px, py, pc = x ^ (k >> 2), y ^ ((k >> 1) & 1), c ^ (k & 1)
        descs.append(pltpu.make_async_remote_copy(
            src_ref=src_ref.at[4 * px + 2 * py + pc], dst_ref=out_ref.at[me],
            send_sem=send_sems.at[k - 1], recv_sem=recv_sems.at[k - 1],
            device_id=(px, py, pc), device_id_type=pl.DeviceIdType.MESH))
    return descs


def _gather_copies(src_ref, out_ref, send_sems, recv_sems, local_sem, starting):
    x, y, c = lax.axis_index("x"), lax.axis_index("y"), lax.axis_index("c")
    me, sibling = (x, y, c), (x, y, 1 - c)
    chips = [(1 - x, y), (x, 1 - y), (1 - x, 1 - y)]

    def slot(px, py, pc):
        return out_ref.at[4 * px + 2 * py + pc]

    def copy(k, block, to, src=None):
        return pltpu.make_async_remote_copy(
            src_ref=slot(*block) if src is None else src, dst_ref=slot(*block),
            send_sem=send_sems.at[k], recv_sem=recv_sems.at[k], device_id=to, device_id_type=pl.DeviceIdType.MESH)

    local = pltpu.make_async_copy(src_ref, slot(*me), local_sem)
    first = [copy(0, me, sibling, src=src_ref)] + [copy(1 + j, me, (*chip, c), src=src_ref)
                                                    for j, chip in enumerate(chips)]
    if starting:
        return local, first
    passed = [copy(4 + j, (*chip, c), sibling) for j, chip in enumerate(chips)]
    landed = [copy(1 + j, (*chip, c), me) for j, chip in enumerate(chips)]
    later = [copy(0, sibling, me)] + [copy(4 + j, (*chip, 1 - c), me) for j, chip in enumerate(chips)]
    return local, first, passed, landed, later


def _comm_start(src_ref, out_ref, sems, scatter):
    if scatter:
        for d in _scatter_descs(src_ref, out_ref, *sems):
            d.start()
    else:
        local, first = _gather_copies(src_ref, out_ref, *sems, starting=True)
        local.start()
        for cp in first:
            cp.start()


def _comm_finish(src_ref, out_ref, sems, scatter):
    if scatter:
        for d in _scatter_descs(src_ref, out_ref, *sems):
            d.wait()
    else:
        local, first, passed, landed, later = _gather_copies(src_ref, out_ref, *sems, starting=False)
        for arrived, onward in zip(landed, passed):
            arrived.wait_recv()
            onward.start()
        for cp in later:
            cp.wait_recv()
        for cp in first + passed:
            cp.wait_send()
        local.wait()


def _exchange(comm, name):
    nc = len(comm)

    def body(*refs):
        srcs, outs, sems = refs[:nc], refs[nc:2 * nc], refs[2 * nc:]
        for i in range(nc):
            _comm_start(srcs[i], outs[i], sems[3 * i:3 * i + 3], comm[i][1])
        for i in range(nc):
            _comm_finish(srcs[i], outs[i], sems[3 * i:3 * i + 3], comm[i][1])

    return pl.pallas_call(
        body, name=name, in_specs=[ANY_SPEC] * nc, out_specs=[ANY_SPEC] * nc,
        out_shape=[_comm_out_shape(*c) for c in comm],
        scratch_shapes=[s for _ in comm for s in _comm_sems()],
    )(*[c[0] for c in comm])


def _hosted_call(body, *, name, grid, in_specs, out_specs, out_shape, args, scratch_shapes=(), comm=()):
    nin, nout, nscr, nc = len(in_specs), len(out_specs), len(scratch_shapes), len(comm)

    def wrapped(*refs):
        ins = refs[:nin]
        csrc = refs[nin:nin + nc]
        outs = refs[nin + nc:nin + nc + nout]
        cout = refs[nin + nc + nout:nin + 2 * nc + nout]
        scr = refs[nin + 2 * nc + nout:]
        sems = scr[nscr:]

        if nc:
            first = functools.reduce(jnp.logical_and, [pl.program_id(a) == 0 for a in range(len(grid))])

            @pl.when(first)
            def _():
                for i in range(nc):
                    _comm_start(csrc[i], cout[i], sems[3 * i:3 * i + 3], comm[i][1])

        body(*ins, *outs, *scr[:nscr])

        if nc:
            last = functools.reduce(jnp.logical_and, [pl.program_id(a) == grid[a] - 1 for a in range(len(grid))])

            @pl.when(last)
            def _():
                for i in range(nc):
                    _comm_finish(csrc[i], cout[i], sems[3 * i:3 * i + 3], comm[i][1])

    res = pl.pallas_call(
        wrapped, name=name, grid=grid,
        in_specs=[*in_specs, *[ANY_SPEC] * nc], out_specs=[*out_specs, *[ANY_SPEC] * nc],
        out_shape=[*out_shape, *[_comm_out_shape(*c) for c in comm]],
        scratch_shapes=[*scratch_shapes, *[s for _ in comm for s in _comm_sems()]],
        compiler_params=_cparams())(*args, *[c[0] for c in comm])
    return res[:nout], res[nout:]


def _norm_matmul(x, g, w, name, tm=1024, tn=1408, comm=()):
    S, dm = x.shape
    n = w.shape[1]
    tm = min(tm, S)

    def body(x_ref, g_ref, w_ref, xn_ref, o_ref):
        @pl.when(pl.program_id(1) == 0)
        def _():
            xv = x_ref[...]
            r = lax.rsqrt(jnp.mean(xv * xv, axis=-1, keepdims=True) + EPS)
            xn_ref[...] = ((xv * r) * g_ref[...]).astype(bf16)

        o_ref[...] = jnp.dot(xn_ref[...], w_ref[...], preferred_element_type=f32).astype(bf16)

    return _hosted_call(
        body, name=name, grid=(S // tm, n // tn),
        in_specs=[pl.BlockSpec((tm, dm), lambda i, j: (i, 0)),
                  pl.BlockSpec((1, dm), lambda i, j: (0, 0)),
                  pl.BlockSpec((dm, tn), lambda i, j: (0, j))],
        out_specs=[pl.BlockSpec((tm, dm), lambda i, j: (i, 0)),
                   pl.BlockSpec((tm, tn), lambda i, j: (i, j))],
        out_shape=[jax.ShapeDtypeStruct((S, dm), bf16), jax.ShapeDtypeStruct((S, n), bf16)],
        args=(x, g, w), comm=comm)


def _mm_tn(a, b, name, tk, tn, tmc=2048, into=None, col=0, out_cols=None):
    m, ka = a.shape
    n = b.shape[1]
    tmc = min(tmc, m)
    nk = m // tmc

    def body(a_ref, b_ref, *rest):
        o_ref, acc_ref = rest[-2:]
        k = pl.program_id(2)
        part = lax.dot_general(a_ref[...], b_ref[...], (((0,), (0,)), ((), ())), preferred_element_type=f32)

        @pl.when(k == 0)
        def _():
            acc_ref[...] = part

        @pl.when(k > 0)
        def _():
            acc_ref[...] += part

        @pl.when(k == nk - 1)
        def _():
            o_ref[...] = acc_ref[...].astype(bf16)

    in_specs = [pl.BlockSpec((tmc, tk), lambda i, j, k: (k, i)), pl.BlockSpec((tmc, tn), lambda i, j, k: (k, j))]
    if into is None:
        return pl.pallas_call(
            body, name=name, grid=(ka // tk, n // tn, nk), in_specs=in_specs,
            out_specs=pl.BlockSpec((tk, tn), lambda i, j, k: (i, j + col)),
            out_shape=jax.ShapeDtypeStruct((ka, out_cols or n), bf16),
            scratch_shapes=[pltpu.VMEM((tk, tn), f32)],
            compiler_params=_cparams())(a, b)
    return pl.pallas_call(
        body, name=name, grid=(ka // tk, n // tn, nk), in_specs=[*in_specs, ANY_SPEC],
        out_specs=pl.BlockSpec((tk, tn), lambda i, j, k: (i, j + col)),
        out_shape=jax.ShapeDtypeStruct(into.shape, into.dtype),
        scratch_shapes=[pltpu.VMEM((tk, tn), f32)], input_output_aliases={2: 0},
        compiler_params=_cparams())(a, b, into)


HALO = 16


def _rows_at(ext, o, tc):
    if o == 0:
        return ext[HALO:HALO + tc]
    return pltpu.roll(ext, (-o) % ext.shape[0], 0)[HALO:HALO + tc]


def _halo_specs(tc, S, width, col):
    per = tc // HALO
    last = S // HALO - 1
    return (pl.BlockSpec((tc, width), lambda i: (i, col)),
            pl.BlockSpec((HALO, width), lambda i: (jnp.maximum(i * per - 1, 0), col)),
            pl.BlockSpec((HALO, width), lambda i: (jnp.minimum((i + 1) * per, last), col)))


def _extended(cur_ref, prev_ref, next_ref, i, nsteps):
    prev = jnp.where(i > 0, prev_ref[...].astype(f32), 0.0)
    nxt = jnp.where(i < nsteps - 1, next_ref[...].astype(f32), 0.0)
    return jnp.concatenate([prev, cur_ref[...].astype(f32), nxt], axis=0)


def _conv_fwd(proj, cw, cb, tc=512):
    S = proj.shape[0]
    tc = min(tc, S)
    nsteps = S // tc

    def body(cur_ref, prev_ref, next_ref, w_ref, b_ref, o_ref):
        ext = _extended(cur_ref, prev_ref, next_ref, pl.program_id(0), nsteps)
        acc = _rows_at(ext, -2, tc) * w_ref[0:1, :]
        for k in range(1, 4):
            acc = acc + _rows_at(ext, k - 2, tc) * w_ref[k:k + 1, :]
        o_ref[...] = acc + b_ref[...]

    return pl.pallas_call(
        body, name="conv_fwd", grid=(nsteps,),
        in_specs=[*_halo_specs(tc, S, D, 0),
                  pl.BlockSpec((4, D), lambda i: (0, 0)), pl.BlockSpec((1, D), lambda i: (0, 0))],
        out_specs=pl.BlockSpec((tc, D), lambda i: (i, 0)),
        out_shape=jax.ShapeDtypeStruct((S, D), f32),
        compiler_params=_cparams())(proj, proj, proj, cw, cb)


def _conv_bwd(duc_f, duc_b, proj, cw, tc=512, comm=()):
    S = proj.shape[0]
    tc = min(tc, S)
    nsteps = S // tc

    def body(fc, fp, fn, bc, bp, bn, uc_, up, un, w_ref, du_ref, dw_ref, db_ref):
        i = pl.program_id(0)

        @pl.when(i == 0)
        def _():
            dw_ref[...] = jnp.zeros_like(dw_ref)
            db_ref[...] = jnp.zeros_like(db_ref)

        dext = _extended(fc, fp, fn, i, nsteps) + _extended(bc, bp, bn, i, nsteps)
        uext = _extended(uc_, up, un, i, nsteps)
        d = dext[HALO:HALO + tc]
        acc = _rows_at(dext, 2, tc) * w_ref[0:1, :]
        for k in range(1, 4):
            acc = acc + _rows_at(dext, 2 - k, tc) * w_ref[k:k + 1, :]
        du_ref[...] = acc.astype(bf16)
        wrow = lax.broadcasted_iota(jnp.int32, (4, D), 0)
        for k in range(4):
            dw_ref[...] += jnp.where(wrow == k, jnp.sum(d * _rows_at(uext, k - 2, tc), axis=0, keepdims=True), 0.0)
        db_ref[...] += jnp.sum(d, axis=0, keepdims=True)

    return _hosted_call(
        body, name="conv_bwd", grid=(nsteps,),
        in_specs=[*_halo_specs(tc, S, D, 0), *_halo_specs(tc, S, D, 0), *_halo_specs(tc, S, D, 0),
                  pl.BlockSpec((4, D), lambda i: (0, 0))],
        out_specs=[pl.BlockSpec((tc, D), lambda i: (i, 0)),
                   pl.BlockSpec((4, D), lambda i: (0, 0)), pl.BlockSpec((1, D), lambda i: (0, 0))],
        out_shape=[jax.ShapeDtypeStruct((S, D), bf16), jax.ShapeDtypeStruct((4, D), f32),
                   jax.ShapeDtypeStruct((1, D), f32)],
        args=(duc_f, duc_f, duc_f, duc_b, duc_b, duc_b, proj, proj, proj, cw), comm=comm)


def _scan_scratch():
    halves = [pltpu.VMEM((LRU_CHUNK, 128), f32) for _ in range(2 * (LRU_GW // 128))]
    return [*halves, pltpu.VMEM((LRU_CHUNK // 8, LRU_GW), f32), pltpu.VMEM((LRU_CHUNK // 8, LRU_GW), f32)]


def _log_scan(a, b, row, n, reverse, steps):
    for s in steps:
        shift = a.shape[0] - s if reverse else s
        keep = (row < n - s) if reverse else (row >= s)
        a_sh = pltpu.roll(a, shift, 0)
        b_sh = pltpu.roll(b, shift, 0)
        b = jnp.where(keep, a * b_sh + b, b)
        a = jnp.where(keep, a * a_sh, a)
    return a, b


def _scan_chunk(a, b, carry, reverse, *scratch):
    tc, w = a.shape
    ng = tc // 8
    nl = w // 128
    sa_refs, sb_refs, sc_ref, st_ref = scratch[:nl], scratch[nl:2 * nl], scratch[2 * nl], scratch[2 * nl + 1]
    sub = lax.broadcasted_iota(jnp.int32, (8, w), 0)
    ag, bg = [], []
    for k in range(ng):
        ak, bk = _log_scan(a[8 * k:8 * k + 8], b[8 * k:8 * k + 8], sub, 8, reverse, (1, 2, 4))
        ag.append(ak)
        bg.append(bk)
    a = jnp.concatenate(ag, axis=0)
    b = jnp.concatenate(bg, axis=0)
    edge = 0 if reverse else 7
    for i in range(nl):
        sa_refs[i][...] = a[:, 128 * i:128 * (i + 1)]
        sb_refs[i][...] = b[:, 128 * i:128 * (i + 1)]
    ta = jnp.concatenate([r[pl.ds(edge, ng, stride=8), :] for r in sa_refs], axis=1)
    tb = jnp.concatenate([r[pl.ds(edge, ng, stride=8), :] for r in sb_refs], axis=1)
    grow = lax.broadcasted_iota(jnp.int32, (ng, w), 0)
    ta, tb = _log_scan(ta, tb, grow, ng, reverse, [1 << i for i in range(ng.bit_length() - 1)])
    state = tb + ta * carry
    st_ref[...] = state
    if reverse:
        sc_ref[...] = jnp.where(grow == ng - 1, carry, pltpu.roll(state, ng - 1, 0))
    else:
        sc_ref[...] = jnp.where(grow == 0, carry, pltpu.roll(state, 1, 0))
    h = jnp.concatenate([bg[k] + ag[k] * sc_ref[k:k + 1, :] for k in range(ng)], axis=0)
    return h, (st_ref[0:1, :] if reverse else st_ref[ng - 1:ng, :])


def _lru_gates(uc, w, p_ref):
    pre = jnp.dot(uc.astype(bf16), w, preferred_element_type=f32)
    r = _sigmoid_t(pre[:, :LRU_GW] + p_ref[0, 1:2, :])
    gi = _sigmoid_t(pre[:, LRU_GW:] + p_ref[0, 2:3, :])
    sp = _softplus(-p_ref[0, 0:1, :])
    log_a = -RGLRU_C * r * sp
    a = jnp.exp(log_a)
    x = 2.0 * log_a
    series = -x * (1.0 + x * (0.5 + x * (1.0 / 6 + x * (1.0 / 24))))
    beta = jnp.sqrt(jnp.maximum(jnp.where(x > -0.0625, series, 1.0 - a * a), 0.0))
    return r, gi, sp, a, beta


def _lru_fwd(uc, wg, lp, reverse, comm=()):
    S = uc.shape[0]
    tc = LRU_CHUNK
    rows = min(LRU_ROWS, S)
    nsub = rows // tc
    nblk = S // rows
    d = 1 if reverse else 0

    def bidx(c):
        return nblk - 1 - c if reverse else c

    def body(uc_ref, w_ref, p_ref, h_ref, carry_ref, *scan_scratch):
        @pl.when(pl.program_id(1) == 0)
        def _():
            carry_ref[...] = jnp.zeros_like(carry_ref)

        carry = carry_ref[...]
        for j in (reversed(range(nsub)) if reverse else range(nsub)):
            sl = slice(j * tc, (j + 1) * tc)
            ucv = uc_ref[sl, :]
            _, gi, _, a, beta = _lru_gates(ucv, w_ref[0], p_ref)
            h, carry = _scan_chunk(a, beta * (gi * ucv), carry, reverse, *scan_scratch)
            h_ref[sl, :] = h.astype(bf16)
        carry_ref[...] = carry

    return _hosted_call(
        body, name="lru_fwd_rev" if reverse else "lru_fwd", grid=(LRU_GROUPS, nblk),
        in_specs=[pl.BlockSpec((rows, LRU_GW), lambda g, c: (bidx(c), g)),
                  pl.BlockSpec((1, LRU_GW, 2 * LRU_GW), lambda g, c: (g, 0, d)),
                  pl.BlockSpec((1, 8, LRU_GW), lambda g, c: (d, 0, g))],
        out_specs=[pl.BlockSpec((rows, LRU_GW), lambda g, c: (bidx(c), g))],
        out_shape=[jax.ShapeDtypeStruct((S, D), bf16)],
        scratch_shapes=[pltpu.VMEM((1, LRU_GW), f32), *_scan_scratch()],
        args=(uc, wg, lp), comm=comm)


def _lru_bwd(uc, dh, h, wg, lp, reverse, comm=()):
    S = uc.shape[0]
    tc = LRU_CHUNK
    rows = min(LRU_ROWS, S)
    nsub = rows // tc
    nblk = S // rows
    d = 1 if reverse else 0
    per = rows // HALO
    last8 = S // HALO - 1

    def bidx(c):
        return c if reverse else nblk - 1 - c

    def halo_idx(c):
        if reverse:
            return jnp.minimum((bidx(c) + 1) * per, last8)
        return jnp.maximum(bidx(c) * per - 1, 0)

    def body(uc_ref, dh_ref, h_ref, halo_ref, w_ref, p_ref, duc_ref, dw_ref, dp_ref, carry_ref, tmp_ref,
             *scan_scratch):
        c = pl.program_id(1)
        bi = bidx(c)

        @pl.when(c == 0)
        def _():
            carry_ref[...] = jnp.zeros_like(carry_ref)
            dw_ref[...] = jnp.zeros_like(dw_ref)
            dp_ref[...] = jnp.zeros_like(dp_ref)

        row = lax.broadcasted_iota(jnp.int32, (tc, LRU_GW), 0)
        carry = carry_ref[...]
        dw = jnp.zeros((LRU_GW, 2 * LRU_GW), f32)
        dsp = jnp.zeros((1, LRU_GW), f32)
        dba = jnp.zeros((1, LRU_GW), f32)
        dbx = jnp.zeros((1, LRU_GW), f32)
        for j in (range(nsub) if reverse else reversed(range(nsub))):
            sl = slice(j * tc, (j + 1) * tc)
            ucv = uc_ref[sl, :]
            ucb = ucv.astype(bf16)
            r, gi, sp, a, beta = _lru_gates(ucv, w_ref[0], p_ref)
            hv = h_ref[sl, :].astype(f32)
            dhv = dh_ref[sl, :].astype(f32)
            if reverse:
                alpha = jnp.where(row == 0, 1.0, pltpu.roll(a, 1, 0))
                gsc, _ = _scan_chunk(alpha, dhv, carry, False, *scan_scratch)
                if j < nsub - 1:
                    edge = h_ref[(j + 1) * tc:(j + 1) * tc + HALO, :].astype(f32)[0:1, :]
                else:
                    edge = jnp.where(bi < nblk - 1, halo_ref[...].astype(f32)[0:1, :], 0.0)
                h_nb = jnp.where(row == tc - 1, edge, pltpu.roll(hv, tc - 1, 0))
            else:
                alpha = jnp.where(row == tc - 1, 1.0, pltpu.roll(a, tc - 1, 0))
                gsc, _ = _scan_chunk(alpha, dhv, carry, True, *scan_scratch)
                if j > 0:
                    edge = h_ref[j * tc - HALO:j * tc, :].astype(f32)[HALO - 1:HALO, :]
                else:
                    edge = jnp.where(bi > 0, halo_ref[...].astype(f32)[HALO - 1:HALO, :], 0.0)
                h_nb = jnp.where(row == 0, edge, pltpu.roll(hv, 1, 0))
            tmp_ref[...] = a * gsc
            carry = tmp_ref[tc - 1:tc, :] if reverse else tmp_ref[0:1, :]

            da = gsc * h_nb
            dbeta = gsc * (gi * ucv)
            dl = da * a - dbeta * (a * a) / beta
            dr = dl * (-RGLRU_C * sp)
            dsp = dsp + jnp.sum(dl * (-RGLRU_C * r), axis=0, keepdims=True)
            dgi = gsc * beta * ucv
            dpre_r = dr * r * (1.0 - r)
            dpre_i = dgi * gi * (1.0 - gi)
            dba = dba + jnp.sum(dpre_r, axis=0, keepdims=True)
            dbx = dbx + jnp.sum(dpre_i, axis=0, keepdims=True)
            dpre = jnp.concatenate([dpre_r, dpre_i], axis=1).astype(bf16)
            back = lax.dot_general(dpre, w_ref[0], (((1,), (1,)), ((), ())), preferred_element_type=f32)
            duc_ref[sl, :] = (gsc * beta * gi + back).astype(bf16)
            dw = dw + lax.dot_general(ucb, dpre, (((0,), (0,)), ((), ())), preferred_element_type=f32)
        carry_ref[...] = carry
        dw_ref[0] += dw
        dlam = dsp * (-_sigmoid(-p_ref[0, 0:1, :]))
        prow = lax.broadcasted_iota(jnp.int32, (8, LRU_GW), 0)
        dp_ref[...] += (jnp.where(prow == 0, dlam, 0.0) + jnp.where(prow == 1, dba, 0.0)
                        + jnp.where(prow == 2, dbx, 0.0))

    chunk = pl.BlockSpec((rows, LRU_GW), lambda g, c: (bidx(c), g))
    return _hosted_call(
        body, name="lru_bwd_rev" if reverse else "lru_bwd", grid=(LRU_GROUPS, nblk),
        in_specs=[chunk, chunk, chunk,
                  pl.BlockSpec((HALO, LRU_GW), lambda g, c: (halo_idx(c), g)),
                  pl.BlockSpec((1, LRU_GW, 2 * LRU_GW), lambda g, c: (g, 0, d)),
                  pl.BlockSpec((1, 8, LRU_GW), lambda g, c: (d, 0, g))],
        out_specs=[chunk,
                   pl.BlockSpec((1, LRU_GW, 2 * LRU_GW), lambda g, c: (g, 0, 0)),
                   pl.BlockSpec((8, LRU_GW), lambda g, c: (0, g))],
        out_shape=[jax.ShapeDtypeStruct((S, D), bf16),
                   jax.ShapeDtypeStruct((LRU_GROUPS, LRU_GW, 2 * LRU_GW), f32),
                   jax.ShapeDtypeStruct((8, D), f32)],
        scratch_shapes=[pltpu.VMEM((1, LRU_GW), f32), pltpu.VMEM((tc, LRU_GW), f32), *_scan_scratch()],
        args=(uc, dh, h, h, wg, lp), comm=comm)


def _slope(h):
    return 2.0 ** (-8.0 * (h + 1.0) / N_HEADS)


def _kv_specs(nb, col):
    return [pl.BlockSpec((BLK, N_KV * HEAD_DIM), lambda n: (jnp.maximum(n - 1, 0), col)),
            pl.BlockSpec((BLK, N_KV * HEAD_DIM), lambda n: (n, col)),
            pl.BlockSpec((BLK, N_KV * HEAD_DIM), lambda n: (jnp.minimum(n + 1, nb - 1), col))]


def _dup_windows(r0, r1, r2):
    left = lax.broadcasted_iota(jnp.int32, (3 * BLK, 128), 1) < HEAD_DIM
    win = jnp.concatenate([r0[...], r1[...], r2[...]], axis=0)
    out = []
    for i in range(N_KV // 2):
        t = win[:, i * 128:(i + 1) * 128]
        r = pltpu.roll(t, HEAD_DIM, 1)
        out += [jnp.where(left, t, r).astype(bf16), jnp.where(left, r, t).astype(bf16)]
    return out


def _attn_bias_init(bias_ref):
    k_loc = lax.broadcasted_iota(jnp.int32, (3 * BLK, BLK), 0)
    q_loc = lax.broadcasted_iota(jnp.int32, (3 * BLK, BLK), 1)
    adist = jnp.abs(q_loc + BLK - k_loc)
    adf = adist.astype(f32)
    for e in range(3):
        ok = adist <= WINDOW
        if e == 0:
            ok = ok & (k_loc >= BLK)
        if e == 2:
            ok = ok & (k_loc < 2 * BLK)
        for kv in range(N_KV):
            bias_ref[e, kv] = jnp.concatenate(
                [jnp.where(ok, (-_slope(4 * kv + j)) * adf, NEG_INF) for j in range(4)], axis=1)


def _stack_heads(ref, kv, scale):
    left = lax.broadcasted_iota(jnp.int32, (BLK, 128), 1) < HEAD_DIM
    rows = []
    for pp in range(2):
        t = ref[:, (2 * kv + pp) * 128:(2 * kv + pp + 1) * 128]
        if scale != 1.0:
            t = t * scale
        zero = jnp.zeros_like(t)
        rows += [jnp.where(left, t, zero).astype(bf16), jnp.where(left, zero, t).astype(bf16)]
    return jnp.concatenate(rows, axis=0)


def _attn_softmax(qs, k2, bias, sink_ref, kv, stats=None):
    sink = jnp.concatenate([jnp.full((1, BLK), sink_ref[0, 4 * kv + j], f32) for j in range(4)], axis=1)
    s = lax.dot_general(k2, qs, (((1,), (1,)), ((), ())), preferred_element_type=f32) + bias
    m = jnp.maximum(jnp.max(s, axis=0, keepdims=True), sink) if stats is None else stats[0]
    p = jnp.exp(s - m)
    ps = jnp.exp(sink - m)
    inv = 1.0 / (jnp.sum(p, axis=0, keepdims=True) + ps) if stats is None else stats[1]
    return p, ps, m, inv


def _pair_tiles(t):
    return [jnp.concatenate([t[:HEAD_DIM, 256 * pp:256 * pp + 128],
                             t[HEAD_DIM:, 256 * pp + 128:256 * pp + 256]], axis=0).T for pp in range(2)]


def _attn_fwd(proj, sink, comm=()):
    S = proj.shape[0]
    nb = S // BLK
    assert nb >= 2

    def body(q_ref, k0, k1, k2_, v0, v1, v2_, sink_ref, o_ref, st_ref, bias_ref):
        n = pl.program_id(0)

        @pl.when(n == 0)
        def _():
            _attn_bias_init(bias_ref)

        e = jnp.where(n == 0, 0, jnp.where(n == nb - 1, 2, 1))
        kk = _dup_windows(k0, k1, k2_)
        vv = _dup_windows(v0, v1, v2_)
        tiles = []
        for kv in range(N_KV):
            qs = _stack_heads(q_ref, kv, HEAD_DIM ** -0.5)
            p, _, m, inv = _attn_softmax(qs, kk[kv], bias_ref[e, kv], sink_ref, kv)
            st_ref[0, kv:kv + 1, :] = m
            st_ref[0, N_KV + kv:N_KV + kv + 1, :] = inv
            ot = lax.dot_general(vv[kv], p.astype(bf16), (((0,), (0,)), ((), ())), preferred_element_type=f32)
            tiles += _pair_tiles(ot * inv)
        o_ref[...] = jnp.concatenate(tiles, axis=1).astype(bf16)

    return _hosted_call(
        body, name="attn_fwd", grid=(nb,),
        in_specs=[pl.BlockSpec((BLK, D), lambda n: (n, C_Q // D)),
                  *_kv_specs(nb, C_K // (N_KV * HEAD_DIM)), *_kv_specs(nb, C_V // (N_KV * HEAD_DIM)),
                  pl.BlockSpec(memory_space=pltpu.SMEM)],
        out_specs=[pl.BlockSpec((BLK, D), lambda n: (n, 0)), pl.BlockSpec((1, 2 * N_KV, 4 * BLK), lambda n: (n, 0, 0))],
        out_shape=[jax.ShapeDtypeStruct((S, D), bf16), jax.ShapeDtypeStruct((nb, 2 * N_KV, 4 * BLK), f32)],
        scratch_shapes=[pltpu.VMEM((3, N_KV, 3 * BLK, 4 * BLK), f32)],
        args=(proj, proj, proj, proj, proj, proj, proj, sink), comm=comm)


def _attn_bwd(proj, sink, dyb, stats, comm=()):
    S = proj.shape[0]
    nb = S // BLK
    assert nb >= 2

    def body(q_ref, k0, k1, k2_, v0, v1, v2_, sink_ref, do_ref, st_ref, dq_ref, dk_out, dv_out, ds_ref,
             bias_ref, dk_ref, dv_ref, dsk_ref):
        n = pl.program_id(0)

        @pl.when(n == 0)
        def _():
            _attn_bias_init(bias_ref)
            dk_ref[...] = jnp.zeros_like(dk_ref)
            dv_ref[...] = jnp.zeros_like(dv_ref)
            dsk_ref[...] = jnp.zeros_like(dsk_ref)

        e = jnp.where(n == 0, 0, jnp.where(n == nb - 1, 2, 1))
        kk = _dup_windows(k0, k1, k2_)
        vv = _dup_windows(v0, v1, v2_)
        left3 = lax.broadcasted_iota(jnp.int32, (3 * BLK, 128), 1) < HEAD_DIM
        start = pl.multiple_of(n * BLK, BLK)
        dq_tiles, dks, dvs = [], [], []
        for kv in range(N_KV):
            qs = _stack_heads(q_ref, kv, HEAD_DIM ** -0.5)
            dos = _stack_heads(do_ref, kv, 1.0)
            stats = (st_ref[0, kv:kv + 1, :], st_ref[0, N_KV + kv:N_KV + kv + 1, :])
            p, ps, _, inv = _attn_softmax(qs, kk[kv], bias_ref[e, kv], sink_ref, kv, stats)
            pn = p * inv
            dp = lax.dot_general(vv[kv], dos, (((1,), (1,)), ((), ())), preferred_element_type=f32)
            delta = jnp.sum(pn * dp, axis=0, keepdims=True)
            dsc = (pn * (dp - delta)).astype(bf16)
            dsk_ref[kv:kv + 1, :] += delta * (ps * inv)
            dqt = lax.dot_general(kk[kv], dsc, (((0,), (0,)), ((), ())), preferred_element_type=f32)
            dq_tiles += _pair_tiles(dqt * (HEAD_DIM ** -0.5))
            dk = jnp.dot(dsc, qs, preferred_element_type=f32)
            dv = jnp.dot(pn.astype(bf16), dos, preferred_element_type=f32)
            dks.append(dk + pltpu.roll(dk, HEAD_DIM, 1))
            dvs.append(dv + pltpu.roll(dv, HEAD_DIM, 1))
        for jp in range(N_KV // 2):
            cols = slice(jp * 128, (jp + 1) * 128)
            dk_ref[pl.ds(start, 3 * BLK), cols] += jnp.where(left3, dks[2 * jp], dks[2 * jp + 1])
            dv_ref[pl.ds(start, 3 * BLK), cols] += jnp.where(left3, dvs[2 * jp], dvs[2 * jp + 1])
        dq_ref[...] = jnp.concatenate(dq_tiles, axis=1).astype(bf16)

        @pl.when(n == nb - 1)
        def _():
            pltpu.sync_copy(dk_ref, dk_out)
            pltpu.sync_copy(dv_ref, dv_out)
            lane = lax.broadcasted_iota(jnp.int32, (1, 128), 1)
            dsink = jnp.zeros((1, 128), f32)
            for h in range(N_HEADS):
                part = dsk_ref[h // 4:h // 4 + 1, (h % 4) * BLK:(h % 4 + 1) * BLK]
                dsink = dsink + jnp.where(lane == h, -jnp.sum(part), 0.0)
            ds_ref[...] = dsink

    acc = jax.ShapeDtypeStruct((S + 2 * BLK, N_KV * HEAD_DIM), f32)
    return _hosted_call(
        body, name="attn_bwd", grid=(nb,),
        in_specs=[pl.BlockSpec((BLK, D), lambda n: (n, C_Q // D)),
                  *_kv_specs(nb, C_K // (N_KV * HEAD_DIM)), *_kv_specs(nb, C_V // (N_KV * HEAD_DIM)),
                  pl.BlockSpec(memory_space=pltpu.SMEM),
                  pl.BlockSpec((BLK, D), lambda n: (n, 0)),
                  pl.BlockSpec((1, 2 * N_KV, 4 * BLK), lambda n: (n, 0, 0))],
        out_specs=[pl.BlockSpec((BLK, D), lambda n: (n, 0)), ANY_SPEC, ANY_SPEC,
                   pl.BlockSpec((1, 128), lambda n: (0, 0))],
        out_shape=[jax.ShapeDtypeStruct((S, D), bf16), acc, acc, jax.ShapeDtypeStruct((1, 128), f32)],
        scratch_shapes=[pltpu.VMEM((3, N_KV, 3 * BLK, 4 * BLK), f32), pltpu.VMEM(acc.shape, f32),
                        pltpu.VMEM(acc.shape, f32), pltpu.VMEM((8, 4 * BLK), f32)],
        args=(proj, proj, proj, proj, proj, proj, proj, sink, dyb, stats), comm=comm)


def _merge_parts(hf, hb, g, z0, z1, yb, bg):
    g0 = _sigmoid(z0.astype(f32) + bg[:, :D])
    g1 = _sigmoid(z1.astype(f32) + bg[:, D:])
    gelu, dgelu = _gelu_and_grad(g.astype(f32))
    hs = hf.astype(f32) + hb.astype(f32)
    ya = hs * gelu
    return g0, g1, gelu, dgelu, hs, ya


def _merge_outproj(x, hf, hb, proj, yb, bg, w_out, tm=512):
    S = x.shape[0]
    tm = min(tm, S)

    def body(x_ref, hf_ref, hb_ref, g_ref, z0_ref, z1_ref, yb_ref, bg_ref, w_ref, mg_ref, x1_ref):
        ybv = yb_ref[...].astype(f32)
        g0, g1, _, _, _, ya = _merge_parts(hf_ref[...], hb_ref[...], g_ref[...], z0_ref[...], z1_ref[...],
                                           ybv, bg_ref[...])
        mg = (g0 * ya + g1 * ybv).astype(bf16)
        mg_ref[...] = mg
        x1_ref[...] = x_ref[...] + jnp.dot(mg, w_ref[...], preferred_element_type=f32)

    row = pl.BlockSpec((tm, D), lambda i: (i, 0))
    return pl.pallas_call(
        body, name="merge_outproj", grid=(S // tm,),
        in_specs=[row, row, row,
                  pl.BlockSpec((tm, D), lambda i: (i, C_G // D)),
                  pl.BlockSpec((tm, D), lambda i: (i, C_Z0 // D)),
                  pl.BlockSpec((tm, D), lambda i: (i, C_Z1 // D)),
                  row, pl.BlockSpec((1, 2 * D), lambda i: (0, 0)), pl.BlockSpec((D, D), lambda i: (0, 0))],
        out_specs=[row, row],
        out_shape=[jax.ShapeDtypeStruct((S, D), bf16), jax.ShapeDtypeStruct((S, D), f32)],
        compiler_params=_cparams())(x, hf, hb, proj, proj, proj, yb, bg, w_out)


def _ffn_out_loss(gu, x1, w_fo, g3, tgt, tm=256):
    S = x1.shape[0]
    tm = min(tm, S)

    def body(gt_ref, up_ref, x1_ref, w_ref, g_ref, t_ref, ff_ref, dx_ref, dxb_ref, loss_ref, dg_ref):
        @pl.when(pl.program_id(0) == 0)
        def _():
            loss_ref[...] = jnp.zeros_like(loss_ref)
            dg_ref[...] = jnp.zeros_like(dg_ref)

        gt = gt_ref[...].astype(f32)
        ff = ((gt * _sigmoid(gt)) * up_ref[...].astype(f32)).astype(bf16)
        ff_ref[...] = ff
        x2 = x1_ref[...] + jnp.dot(ff, w_ref[...], preferred_element_type=f32)
        gv = g_ref[...]
        r = lax.rsqrt(jnp.mean(x2 * x2, axis=-1, keepdims=True) + EPS)
        xh = x2 * r
        diff = xh * gv - t_ref[...]
        loss_ref[...] += (0.5 / D) * jnp.sum(diff * diff)
        dy = diff * (1.0 / D)
        dg_ref[...] += jnp.sum(dy * xh, axis=0, keepdims=True)
        dxh = dy * gv
        dx = r * (dxh - xh * jnp.mean(dxh * xh, axis=-1, keepdims=True))
        dx_ref[...] = dx
        dxb_ref[...] = dx.astype(bf16)

    row = pl.BlockSpec((tm, D), lambda i: (i, 0))
    vec = pl.BlockSpec((1, D), lambda i: (0, 0))
    return pl.pallas_call(
        body, name="ffn_out_loss", grid=(S // tm,),
        in_specs=[pl.BlockSpec((tm, D_FF), lambda i: (i, 0)), pl.BlockSpec((tm, D_FF), lambda i: (i, 1)),
                  row, pl.BlockSpec((D_FF, D), lambda i: (0, 0)), vec, row],
        out_specs=[pl.BlockSpec((tm, D_FF), lambda i: (i, 0)), row, row,
                   pl.BlockSpec((1, 128), lambda i: (0, 0)), vec],
        out_shape=[jax.ShapeDtypeStruct((S, D_FF), bf16), jax.ShapeDtypeStruct((S, D), f32),
                   jax.ShapeDtypeStruct((S, D), bf16), jax.ShapeDtypeStruct((1, 128), f32),
                   jax.ShapeDtypeStruct((1, D), f32)],
        compiler_params=_cparams())(gu, gu, x1, w_fo, g3, tgt)


def _ffn_bwd1(dx2b, w_fo, gu, tm=256, comm=()):
    S = dx2b.shape[0]
    tm = min(tm, S)

    def body(dx_ref, w_ref, gt_ref, up_ref, dgt_ref, dup_ref):
        dff = lax.dot_general(dx_ref[...], w_ref[...], (((1,), (1,)), ((), ())), preferred_element_type=f32)
        gt = gt_ref[...].astype(f32)
        sg = _sigmoid(gt)
        dup_ref[...] = (dff * (gt * sg)).astype(bf16)
        dgt_ref[...] = ((dff * up_ref[...].astype(f32)) * (sg * (1.0 + gt * (1.0 - sg)))).astype(bf16)

    wide = pl.BlockSpec((tm, D_FF), lambda i: (i, 0))
    return _hosted_call(
        body, name="ffn_bwd1", grid=(S // tm,),
        in_specs=[pl.BlockSpec((tm, D), lambda i: (i, 0)), pl.BlockSpec((D_FF, D), lambda i: (0, 0)),
                  wide, pl.BlockSpec((tm, D_FF), lambda i: (i, 1))],
        out_specs=[wide, wide],
        out_shape=[jax.ShapeDtypeStruct((S, D_FF), bf16), jax.ShapeDtypeStruct((S, D_FF), bf16)],
        args=(dx2b, w_fo, gu, gu), comm=comm)


def _proj_bwd(pieces, w, xres, g, dres, name, tm=256, comm=()):
    S = xres.shape[0]
    tm = min(tm, S)
    np_ = len(pieces)

    def body(*refs):
        p_refs = refs[:np_]
        w_refs = refs[np_:2 * np_]
        x_ref, g_ref, dres_ref, dx_ref, dxb_ref, dg_ref = refs[2 * np_:]

        @pl.when(pl.program_id(0) == 0)
        def _():
            dg_ref[...] = jnp.zeros_like(dg_ref)

        nt = (((1,), (1,)), ((), ()))
        dn = lax.dot_general(p_refs[0][...], w_refs[0][...], nt, preferred_element_type=f32)
        for pr, wr in zip(p_refs[1:], w_refs[1:]):
            dn = dn + lax.dot_general(pr[...], wr[...], nt, preferred_element_type=f32)
        dxn, dgc = _rms_bwd(dn, x_ref[...], g_ref[...])
        dx = dres_ref[...] + dxn
        dx_ref[...] = dx
        dxb_ref[...] = dx.astype(bf16)
        dg_ref[...] += jnp.sum(dgc, axis=0, keepdims=True)

    row = pl.BlockSpec((tm, D), lambda i: (i, 0))
    vec = pl.BlockSpec((1, D), lambda i: (0, 0))
    return _hosted_call(
        body, name=name, grid=(S // tm,),
        in_specs=[*[pl.BlockSpec((tm, wd), functools.partial(lambda i, cb: (i, cb), cb=acb))
                    for _, acb, _, wd in pieces],
                  *[pl.BlockSpec((D, wd), functools.partial(lambda i, cb: (0, cb), cb=wcb))
                    for _, _, wcb, wd in pieces],
                  row, vec, row],
        out_specs=[row, row, vec],
        out_shape=[jax.ShapeDtypeStruct((S, D), f32), jax.ShapeDtypeStruct((S, D), bf16),
                   jax.ShapeDtypeStruct((1, D), f32)],
        args=(*[p[0] for p in pieces], *[w] * np_, xres, g, dres), comm=comm)


def _outproj_bwd(dx1b, w_out, hf, hb, proj, yb, bg, tm=512):
    S = dx1b.shape[0]
    tm = min(tm, S)

    def body(dx_ref, w_ref, hf_ref, hb_ref, g_ref, z0_ref, z1_ref, yb_ref, bg_ref,
             dh_ref, dg_ref, dz_ref, dyb_ref, dbg_ref):
        @pl.when(pl.program_id(0) == 0)
        def _():
            dbg_ref[...] = jnp.zeros_like(dbg_ref)

        dm = lax.dot_general(dx_ref[...], w_ref[...], (((1,), (1,)), ((), ())), preferred_element_type=f32)
        ybv = yb_ref[...].astype(f32)
        g0, g1, gelu, dgelu, hs, ya = _merge_parts(hf_ref[...], hb_ref[...], g_ref[...], z0_ref[...],
                                                   z1_ref[...], ybv, bg_ref[...])
        dya = dm * g0
        dh_ref[...] = (dya * gelu).astype(bf16)
        dg_ref[...] = (dya * hs * dgelu).astype(bf16)
        dyb_ref[...] = (dm * g1).astype(bf16)
        dz0 = (dm * ya) * (g0 * (1.0 - g0))
        dz1 = (dm * ybv) * (g1 * (1.0 - g1))
        dz = jnp.concatenate([dz0, dz1], axis=1)
        dz_ref[...] = dz.astype(bf16)
        dbg_ref[...] += jnp.sum(dz, axis=0, keepdims=True)

    row = pl.BlockSpec((tm, D), lambda i: (i, 0))
    return pl.pallas_call(
        body, name="outproj_bwd", grid=(S // tm,),
        in_specs=[row, pl.BlockSpec((D, D), lambda i: (0, 0)), row, row,
                  pl.BlockSpec((tm, D), lambda i: (i, C_G // D)),
                  pl.BlockSpec((tm, D), lambda i: (i, C_Z0 // D)),
                  pl.BlockSpec((tm, D), lambda i: (i, C_Z1 // D)),
                  row, pl.BlockSpec((1, 2 * D), lambda i: (0, 0))],
        out_specs=[row, row, pl.BlockSpec((tm, 2 * D), lambda i: (i, 0)), row,
                   pl.BlockSpec((1, 2 * D), lambda i: (0, 0))],
        out_shape=[jax.ShapeDtypeStruct((S, D), bf16), jax.ShapeDtypeStruct((S, D), bf16),
                   jax.ShapeDtypeStruct((S, 2 * D), bf16), jax.ShapeDtypeStruct((S, D), bf16),
                   jax.ShapeDtypeStruct((1, 2 * D), f32)],
        compiler_params=_cparams())(dx1b, w_out, hf, hb, proj, proj, proj, yb, bg)


def _block_diag_groups(w):
    w4 = w.reshape(LRU_GROUPS, 4, LRU_BLOCK, LRU_BLOCK)
    eye = jnp.eye(4, dtype=w.dtype)
    return jnp.einsum("ghij,hk->ghikj", w4, eye).reshape(LRU_GROUPS, LRU_GW, LRU_GW)


def _diag_blocks(dw):
    d5 = dw.reshape(LRU_GROUPS, 4, LRU_BLOCK, 4, LRU_BLOCK)
    return jnp.stack([d5[:, h, :, h, :] for h in range(4)], axis=1).reshape(LRU_HEADS, LRU_BLOCK, LRU_BLOCK)


def _local_step(x, tgt, small, env, before=lambda name: (), after=lambda name, got: None):
    S = x.shape[0]
    g1, g2, g3 = small["norm_mix_g"], small["norm_ffn_g"], small["norm_final_g"]
    bg, cw, cb = small["b_gate"], small["conv_w"], small["conv_b"]
    sink = small["attn_sink"]

    wg = jnp.concatenate([_block_diag_groups(small["lru_wa"][0]), _block_diag_groups(small["lru_wx"][0]),
                          _block_diag_groups(small["lru_wa"][1]), _block_diag_groups(small["lru_wx"][1])],
                         axis=2).astype(bf16)
    zeros5 = jnp.zeros((5, D), f32)
    lp = jnp.stack([jnp.concatenate([small["lru_lambda"][d:d + 1], small["lru_ba"][d:d + 1],
                                     small["lru_bx"][d:d + 1], zeros5], axis=0) for d in range(2)])

    def hosted(name, fn, *args, **kw):
        outs, got = fn(*args, comm=tuple(before(name)), **kw)
        after(name, got)
        return outs

    xn, proj = hosted("norm_inproj", _norm_matmul, x, g1, env["w_in_p"], "norm_inproj")
    uc = _conv_fwd(proj, cw, cb)
    (hf,) = hosted("lru_fwd", _lru_fwd, uc, wg, lp, False)
    (hb,), _ = _lru_fwd(uc, wg, lp, True)
    yb, attn_stats = hosted("attn_fwd", _attn_fwd, proj, sink)
    merged, x1 = _merge_outproj(x, hf, hb, proj, yb, bg, env["w_out"])
    (xn2, gu), _ = _norm_matmul(x1, g2, env["w_fi"], "norm_ffn_in")
    ff, dx2, dx2b, loss, dg3 = _ffn_out_loss(gu, x1, env["w_fo"], g3, tgt)

    env["dw_fo"] = _mm_tn(ff, dx2b, "dw_ffn_out", tk=1408, tn=1024)
    dgt, dup = hosted("ffn_bwd1", _ffn_bwd1, dx2b, env["w_fo"], gu)
    dw_gate = _mm_tn(xn2, dgt, "dw_ffn_in_gate", tk=1024, tn=1408, out_cols=2 * D_FF)
    env["dw_fi"] = _mm_tn(xn2, dup, "dw_ffn_in_up", tk=1024, tn=1408, into=dw_gate, col=D_FF // 1408)
    (dx1, dx1b, dg2), _ = _proj_bwd([(dgt, 0, 0, D_FF), (dup, 0, 1, D_FF)], env["w_fi"], x1, g2, dx2, "ffn_in_bwd")
    env["dw_out"] = _mm_tn(merged, dx1b, "dw_out", tk=1024, tn=1024)
    dh, dgl, dz, dyb, dbg = _outproj_bwd(dx1b, env["w_out"], hf, hb, proj, yb, bg)
    dq, dk2, dv2, dsink = hosted("attn_bwd", _attn_bwd, proj, sink, dyb, attn_stats)
    dkv = jnp.concatenate([dk2[BLK:BLK + S], dv2[BLK:BLK + S]], axis=1).astype(bf16)
    duc_f, dwg_f, dp_f = hosted("lru_bwd", _lru_bwd, uc, dh, hf, wg, lp, False)
    (duc_b, dwg_b, dp_b), _ = _lru_bwd(uc, dh, hb, wg, lp, True)
    env["grads_early"] = {
        "loss": loss[:, :1], "b_gate": dbg,
        "lru_lambda": jnp.concatenate([dp_f[0:1], dp_b[0:1]], axis=0),
        "lru_wa": jnp.stack([_diag_blocks(dwg_f[:, :, :LRU_GW]), _diag_blocks(dwg_b[:, :, :LRU_GW])]),
        "lru_ba": jnp.concatenate([dp_f[1:2], dp_b[1:2]], axis=0),
        "lru_wx": jnp.stack([_diag_blocks(dwg_f[:, :, LRU_GW:]), _diag_blocks(dwg_b[:, :, LRU_GW:])]),
        "lru_bx": jnp.concatenate([dp_f[2:3], dp_b[2:3]], axis=0),
        "attn_sink": dsink[:, :N_HEADS], "norm_ffn_g": dg2, "norm_final_g": dg3,
    }
    du, dcw, dcb = hosted("conv_bwd", _conv_bwd, duc_f, duc_b, proj, cw)
    dw_in = _mm_tn(xn, du, "dw_in_u", tk=1024, tn=1024, out_cols=IN_W)
    dw_in = _mm_tn(xn, dgl, "dw_in_g", tk=1024, tn=1024, into=dw_in, col=1)
    dw_in = _mm_tn(xn, dq, "dw_in_q", tk=1024, tn=1024, into=dw_in, col=2)
    dw_in = _mm_tn(xn, dkv, "dw_in_kv", tk=1024, tn=512, into=dw_in, col=3072 // 512)
    env["dw_in"] = _mm_tn(xn, dz, "dw_in_z", tk=1024, tn=512, into=dw_in, col=3584 // 512)
    col_pieces = [(du, 0, C_U // D, D), (dgl, 0, C_G // D, D), (dq, 0, C_Q // D, D), (dz, 0, C_Z0 // D, D),
                  (dz, 1, C_Z1 // D, D), (dkv, 0, C_K // 512, 512)]
    dx, _, dg1 = hosted("inproj_bwd", _proj_bwd, col_pieces, env["w_in_p"], x, g1, dx1, "inproj_bwd")

    grads = dict(env["grads_early"], norm_mix_g=dg1, conv_w=dcw, conv_b=dcb)
    return dx, grads


def _adamw(gparts, w, m, v, name, tr=256):
    n, rows, cols = gparts.shape
    tr = _div_tile(rows, tr)
    c1 = 1.0 - ADAM_B1 ** ADAM_STEP
    c2 = 1.0 - ADAM_B2 ** ADAM_STEP

    def body(g_ref, w_ref, m_ref, v_ref, go_ref, d_ref, mo_ref, vo_ref):
        g = g_ref[0].astype(f32)
        for j in range(1, n):
            g = g + g_ref[j].astype(f32)
        mn = ADAM_B1 * m_ref[0] + (1.0 - ADAM_B1) * g
        vn = ADAM_B2 * v_ref[0] + (1.0 - ADAM_B2) * (g * g)
        m_hat = mn / c1
        v_hat = vn / c2
        go_ref[0] = g
        d_ref[0] = -ADAM_LR * (m_hat / (jnp.sqrt(v_hat) + ADAM_EPS) + ADAM_WD * w_ref[0])
        mo_ref[0] = mn
        vo_ref[0] = vn

    blk = pl.BlockSpec((1, tr, cols), lambda i: (0, i, 0))
    shp = jax.ShapeDtypeStruct((1, rows, cols), f32)
    return pl.pallas_call(
        body, name=name, grid=(rows // tr,),
        in_specs=[pl.BlockSpec((n, tr, cols), lambda i: (0, i, 0)), blk, blk, blk],
        out_specs=[blk, blk, blk, blk], out_shape=[shp, shp, shp, shp],
        compiler_params=_cparams())(gparts, w, m, v)


def _sum_parts(parts, name):
    n, rows, cols = parts.shape

    def body(p_ref, o_ref):
        acc = p_ref[0].astype(f32)
        for j in range(1, n):
            acc = acc + p_ref[j].astype(f32)
        o_ref[...] = acc

    return pl.pallas_call(
        body, name=name, out_shape=jax.ShapeDtypeStruct((rows, cols), f32),
        compiler_params=_cparams())(parts)


def _pack_rows(arrs, dtype=f32):
    rows, spans, at = [], [], 0
    for a in arrs:
        flat = a.reshape(-1).astype(dtype)
        nr = -(-flat.shape[0] // 1024)
        rows.append(jnp.pad(flat, (0, nr * 1024 - flat.shape[0])).reshape(nr, 1024))
        spans.append((at, nr))
        at += nr
    pad = (-at) % 16
    if pad:
        rows.append(jnp.zeros((pad, 1024), dtype))
    return jnp.concatenate(rows, axis=0), spans


def _unpack_rows(packed, spans, shapes):
    out = []
    for (at, nr), shp in zip(spans, shapes):
        n = math.prod(shp)
        out.append(packed[at:at + nr].reshape(-1)[:n].reshape(shp))
    return out


BIG = ("w_in", "w_out", "w_ffn_in", "w_ffn_out")
SMALL_REPL = ("norm_mix_g", "b_gate", "conv_b", "lru_wa", "lru_wx", "attn_sink", "norm_ffn_g", "norm_final_g")
SMALL_SHARD = ("conv_w", "lru_lambda", "lru_ba", "lru_bx")
ORDER = ("norm_mix_g", "w_in", "b_gate", "conv_w", "conv_b", "lru_lambda", "lru_wa", "lru_ba", "lru_wx",
         "lru_bx", "attn_sink", "w_out", "norm_ffn_g", "w_ffn_in", "w_ffn_out", "norm_final_g")
EARLY_F32 = ("loss", "b_gate", "lru_lambda", "lru_ba", "lru_bx", "attn_sink", "norm_ffn_g", "norm_final_g")
EARLY_BF16 = ("lru_wa", "lru_wx")
LATE = ("norm_mix_g", "conv_w", "conv_b")


def kernel(x, norm_mix_g, w_in, b_gate, conv_w, conv_b, lru_lambda, lru_wa, lru_ba, lru_wx, lru_bx, attn_sink, w_out, norm_ffn_g, w_ffn_in, w_ffn_out, norm_final_g, loss_target, m_norm_mix_g, m_w_in, m_b_gate, m_conv_w, m_conv_b, m_lru_lambda, m_lru_wa, m_lru_ba, m_lru_wx, m_lru_bx, m_attn_sink, m_w_out, m_norm_ffn_g, m_w_ffn_in, m_w_ffn_out, m_norm_final_g, v_norm_mix_g, v_w_in, v_b_gate, v_conv_w, v_conv_b, v_lru_lambda, v_lru_wa, v_lru_ba, v_lru_wx, v_lru_bx, v_attn_sink, v_w_out, v_norm_ffn_g, v_w_ffn_in, v_w_ffn_out, v_norm_final_g):
    w = dict(norm_mix_g=norm_mix_g, w_in=w_in, b_gate=b_gate, conv_w=conv_w, conv_b=conv_b, lru_lambda=lru_lambda,
             lru_wa=lru_wa, lru_ba=lru_ba, lru_wx=lru_wx, lru_bx=lru_bx, attn_sink=attn_sink, w_out=w_out,
             norm_ffn_g=norm_ffn_g, w_ffn_in=w_ffn_in, w_ffn_out=w_ffn_out, norm_final_g=norm_final_g)
    m = dict(norm_mix_g=m_norm_mix_g, w_in=m_w_in, b_gate=m_b_gate, conv_w=m_conv_w, conv_b=m_conv_b,
             lru_lambda=m_lru_lambda, lru_wa=m_lru_wa, lru_ba=m_lru_ba, lru_wx=m_lru_wx, lru_bx=m_lru_bx,
             attn_sink=m_attn_sink, w_out=m_w_out, norm_ffn_g=m_norm_ffn_g, w_ffn_in=m_w_ffn_in,
             w_ffn_out=m_w_ffn_out, norm_final_g=m_norm_final_g)
    v = dict(norm_mix_g=v_norm_mix_g, w_in=v_w_in, b_gate=v_b_gate, conv_w=v_conv_w, conv_b=v_conv_b,
             lru_lambda=v_lru_lambda, lru_wa=v_lru_wa, lru_ba=v_lru_ba, lru_wx=v_lru_wx, lru_bx=v_lru_bx,
             attn_sink=v_attn_sink, w_out=v_w_out, norm_ffn_g=v_norm_ffn_g, w_ffn_in=v_w_ffn_in,
             w_ffn_out=v_w_ffn_out, norm_final_g=v_norm_final_g)
    me = 4 * lax.axis_index("x") + 2 * lax.axis_index("y") + lax.axis_index("c")

    def cols_full(got):
        return jnp.swapaxes(got, 0, 1).reshape(got.shape[1], -1)

    def cols_parts(g):
        return jnp.swapaxes(g.reshape(g.shape[0], N_DEV, -1), 0, 1)

    def rows_parts(g):
        return g.reshape(N_DEV, -1, g.shape[1])

    shard_rows = jnp.concatenate([w[n][0] for n in SMALL_SHARD], axis=0)
    got_w_in, got_rows = _exchange([(w_in[0].astype(bf16), False), (shard_rows, False)], "gather_w_in")
    full_rows = cols_full(got_rows)
    small = {n: w[n] for n in ("norm_mix_g", "b_gate", "conv_b", "attn_sink", "norm_ffn_g")}
    small["lru_wa"], small["lru_wx"] = lru_wa[0], lru_wx[0]
    small["norm_final_g"] = norm_final_g.reshape(1, D)
    small["conv_w"], small["lru_lambda"] = full_rows[0:4], full_rows[4:6]
    small["lru_ba"], small["lru_bx"] = full_rows[6:8], full_rows[8:10]

    env = {"w_in_p": _perm_cols(cols_full(got_w_in))}
    recv = {}

    def before(name):
        if name == "norm_inproj":
            return [(w_out[0].astype(bf16), False), (w_ffn_out[0].astype(bf16), False)]
        if name == "lru_fwd":
            return [(w_ffn_in[0].astype(bf16), False)]
        if name == "ffn_bwd1":
            return [(rows_parts(env["dw_fo"]).astype(bf16), True)]
        if name == "attn_bwd":
            return [(rows_parts(env["dw_out"]).astype(bf16), True)]
        if name == "lru_bwd":
            return [(cols_parts(env["dw_fi"]).astype(bf16), True)]
        if name == "conv_bwd":
            ge = env["grads_early"]
            p32, env["early_f32_spans"] = _pack_rows([ge[n] for n in EARLY_F32])
            p16, env["early_bf16_spans"] = _pack_rows([ge[n] for n in EARLY_BF16], bf16)
            return [(p32, False), (p16, False)]
        if name == "inproj_bwd":
            return [(cols_parts(env["dw_in"]).astype(bf16), True)]
        return []

    def after(name, got):
        if name == "norm_inproj":
            env["w_out"], env["w_fo"] = got[0].reshape(D, D), got[1].reshape(D_FF, D)
        elif name == "lru_fwd":
            env["w_fi"] = cols_full(got[0])
        elif name == "ffn_bwd1":
            recv["w_ffn_out"] = got[0]
        elif name == "attn_bwd":
            recv["w_out"] = got[0]
        elif name == "lru_bwd":
            recv["w_ffn_in"] = got[0]
        elif name == "conv_bwd":
            recv["early_f32"], recv["early_bf16"] = got
        elif name == "inproj_bwd":
            recv["w_in"] = got[0]

    grad_x, grads = _local_step(x[0], loss_target[0], small, env, before, after)

    outs = {}
    for name in BIG:
        outs[name] = _adamw(recv[name], w[name], m[name], v[name], "adamw_" + name)

    small_names = SMALL_REPL + SMALL_SHARD
    late_packed, late_spans = _pack_rows([grads[n] for n in LATE])
    (got_late,) = _exchange([(late_packed, False)], "gather_late_grads")
    summed = {}
    for names, got, spans, tag in ((EARLY_F32, recv["early_f32"], env["early_f32_spans"], "early_f32"),
                                   (EARLY_BF16, recv["early_bf16"], env["early_bf16_spans"], "early_bf16"),
                                   (LATE, got_late, late_spans, "late")):
        total = _sum_parts(got, "sum_small_" + tag)
        summed.update(zip(names, _unpack_rows(total, spans, [grads[n].shape for n in names])))
    loss = summed["loss"].reshape(())
    gsm = {n: summed[n].reshape(w[n].shape) for n in SMALL_REPL}
    for n in SMALL_SHARD:
        full = summed[n]
        gsm[n] = lax.dynamic_slice_in_dim(full, me * 128, 128, axis=1).reshape(w[n].shape)
    pk = lambda dct: _pack_rows([dct[n] for n in small_names])[0]
    gp, sp = _pack_rows([gsm[n] for n in small_names])
    res = _adamw(gp[None], pk(w)[None], pk(m)[None], pk(v)[None], "adamw_small")
    sshapes = [w[n].shape for n in small_names]
    for idx, t in enumerate(res):
        for n, a in zip(small_names, _unpack_rows(t[0], sp, sshapes)):
            outs.setdefault(n, [None] * 4)[idx] = a

    result = [loss, grad_x[None]]
    for idx in range(4):
        result += [outs[n][idx] for n in ORDER]
    return tuple(result)
```

```python
import functools
import math

import jax
import jax.numpy as jnp
from jax import lax
from jax.experimental import pallas as pl
from jax.experimental.pallas import tpu as pltpu

f32 = jnp.float32
bf16 = jnp.bfloat16

D = 1024
D_FF = 2816
IN_W = 5632
N_HEADS = 16
N_KV = 4
HEAD_DIM = 64
WINDOW = 128
BLK = 128
LRU_HEADS = 16
LRU_BLOCK = 64
LRU_GROUPS = 4
LRU_GW = 256
LRU_CHUNK = 128
LRU_ROWS = 1024
RGLRU_C = 8.0
EPS = 1e-6
NEG_INF = -1e30
N_DEV = 8

ADAM_LR = 0.001
ADAM_B1 = 0.9
ADAM_B2 = 0.999
ADAM_EPS = 1e-08
ADAM_WD = 0.01
ADAM_STEP = 10

VMEM_MB = 56

C_U, C_G, C_Q, C_Z0, C_Z1, C_K, C_V = 0, 1024, 2048, 3072, 4096, 5120, 5376


def _cparams(vmem_mb=VMEM_MB):
    return pltpu.CompilerParams(vmem_limit_bytes=vmem_mb << 20)


def _div_tile(n, pref):
    if n <= pref:
        return n
    return max(t for t in range(8, pref + 1, 8) if n % t == 0)


def _perm_cols(w):
    return jnp.concatenate([w[:, :3072], w[:, 3584:5632], w[:, 3072:3584]], axis=1)


def _sigmoid(x):
    return 1.0 / (1.0 + jnp.exp(-x))


def _sigmoid_t(x):
    return 0.5 * jnp.tanh(0.5 * x) + 0.5


def _log1p(x):
    u = 1.0 + x
    d = u - 1.0
    return jnp.where(d == 0.0, x, jnp.log(u) * (x / jnp.where(d == 0.0, 1.0, d)))


def _softplus(x):
    return jnp.maximum(x, 0.0) + _log1p(jnp.exp(-jnp.abs(x)))


def _gelu_and_grad(x):
    c = math.sqrt(2.0 / math.pi)
    inner = c * (x + 0.044715 * (x * x * x))
    t = jnp.tanh(inner)
    gelu = 0.5 * x * (1.0 + t)
    dinner = c * (1.0 + 3 * 0.044715 * (x * x))
    dgelu = 0.5 * (1.0 + t) + 0.5 * x * (1.0 - t * t) * dinner
    return gelu, dgelu


def _rms_bwd(dn, xv, g):
    r = lax.rsqrt(jnp.mean(xv * xv, axis=-1, keepdims=True) + EPS)
    xh = xv * r
    dxh = dn * g
    dx = r * (dxh - xh * jnp.mean(dxh * xh, axis=-1, keepdims=True))
    return dx, dn * xh


ANY_SPEC = pl.BlockSpec(memory_space=pl.ANY)


def _comm_out_shape(src, scatter):
    return jax.ShapeDtypeStruct((N_DEV, *(src.shape[1:] if scatter else src.shape)), src.dtype)


def _comm_sems():
    return [pltpu.SemaphoreType.DMA((N_DEV - 1,)), pltpu.SemaphoreType.DMA((N_DEV - 1,)), pltpu.SemaphoreType.DMA]


def _scatter_descs(src_ref, out_ref, send_sems, recv_sems, local_sem):
    x, y, c = lax.axis_index("x"), lax.axis_index("y"), lax.axis_index("c")
    me = 4 * x + 2 * y + c
    descs = [pltpu.make_async_copy(src_ref.at[me], out_ref.at[me], local_sem)]
    for k in range(1, N_DEV):
        px, py, pc = x ^ (k >> 2), y ^ ((k >> 1) & 1), c ^ (k & 1)
        descs.append(pltpu.make_async_remote_copy(
            src_ref=src_ref.at[4 * px + 2 * py + pc], dst_ref=out_ref.at[me],
            send_sem=send_sems.at[k - 1], recv_sem=recv_sems.at[k - 1],
            device_id=(px, py, pc), device_id_type=pl.DeviceIdType.MESH))
    return descs


def _gather_copies(src_ref, out_ref, send_sems, recv_sems, local_sem, starting):
    x, y, c = lax.axis_index("x"), lax.axis_index("y"), lax.axis_index("c")
    me, sibling = (x, y, c), (x, y, 1 - c)
    chips = [(1 - x, y), (x, 1 - y), (1 - x, 1 - y)]

    def slot(px, py, pc):
        return out_ref.at[4 * px + 2 * py + pc]

    def copy(k, block, to, src=None):
        return pltpu.make_async_remote_copy(
            src_ref=slot(*block) if src is None else src, dst_ref=slot(*block),
            send_sem=send_sems.at[k], recv_sem=recv_sems.at[k], device_id=to, device_id_type=pl.DeviceIdType.MESH)

    local = pltpu.make_async_copy(src_ref, slot(*me), local_sem)
    first = [copy(0, me, sibling, src=src_ref)] + [copy(1 + j, me, (*chip, c), src=src_ref)
                                                    for j, chip in enumerate(chips)]
    if starting:
        return local, first
    passed = [copy(4 + j, (*chip, c), sibling) for j, chip in enumerate(chips)]
    landed = [copy(1 + j, (*chip, c), me) for j, chip in enumerate(chips)]
    later = [copy(0, sibling, me)] + [copy(4 + j, (*chip, 1 - c), me) for j, chip in enumerate(chips)]
    return local, first, passed, landed, later


def _comm_start(src_ref, out_ref, sems, scatter):
    if scatter:
        for d in _scatter_descs(src_ref, out_ref, *sems):
            d.start()
    else:
        local, first = _gather_copies(src_ref, out_ref, *sems, starting=True)
        local.start()
        for cp in first:
            cp.start()


def _comm_finish(src_ref, out_ref, sems, scatter):
    if scatter:
        for d in _scatter_descs(src_ref, out_ref, *sems):
            d.wait()
    else:
        local, first, passed, landed, later = _gather_copies(src_ref, out_ref, *sems, starting=False)
        for arrived, onward in zip(landed, passed):
            arrived.wait_recv()
            onward.start()
        for cp in later:
            cp.wait_recv()
        for cp in first + passed:
            cp.wait_send()
        local.wait()


def _exchange(comm, name):
    nc = len(comm)

    def body(*refs):
        srcs, outs, sems = refs[:nc], refs[nc:2 * nc], refs[2 * nc:]
        for i in range(nc):
            _comm_start(srcs[i], outs[i], sems[3 * i:3 * i + 3], comm[i][1])
        for i in range(nc):
            _comm_finish(srcs[i], outs[i], sems[3 * i:3 * i + 3], comm[i][1])

    return pl.pallas_call(
        body, name=name, in_specs=[ANY_SPEC] * nc, out_specs=[ANY_SPEC] * nc,
        out_shape=[_comm_out_shape(*c) for c in comm],
        scratch_shapes=[s for _ in comm for s in _comm_sems()],
    )(*[c[0] for c in comm])


def _hosted_call(body, *, name, grid, in_specs, out_specs, out_shape, args, scratch_shapes=(), comm=()):
    nin, nout, nscr, nc = len(in_specs), len(out_specs), len(scratch_shapes), len(comm)

    def wrapped(*refs):
        ins = refs[:nin]
        csrc = refs[nin:nin + nc]
        outs = refs[nin + nc:nin + nc + nout]
        cout = refs[nin + nc + nout:nin + 2 * nc + nout]
        scr = refs[nin + 2 * nc + nout:]
        sems = scr[nscr:]

        if nc:
            first = functools.reduce(jnp.logical_and, [pl.program_id(a) == 0 for a in range(len(grid))])

            @pl.when(first)
            def _():
                for i in range(nc):
                    _comm_start(csrc[i], cout[i], sems[3 * i:3 * i + 3], comm[i][1])

        body(*ins, *outs, *scr[:nscr])

        if nc:
            last = functools.reduce(jnp.logical_and, [pl.program_id(a) == grid[a] - 1 for a in range(len(grid))])

            @pl.when(last)
            def _():
                for i in range(nc):
                    _comm_finish(csrc[i], cout[i], sems[3 * i:3 * i + 3], comm[i][1])

    res = pl.pallas_call(
        wrapped, name=name, grid=grid,
        in_specs=[*in_specs, *[ANY_SPEC] * nc], out_specs=[*out_specs, *[ANY_SPEC] * nc],
        out_shape=[*out_shape, *[_comm_out_shape(*c) for c in comm]],
        scratch_shapes=[*scratch_shapes, *[s for _ in comm for s in _comm_sems()]],
        compiler_params=_cparams())(*args, *[c[0] for c in comm])
    return res[:nout], res[nout:]


def _norm_matmul(x, g, w, name, tm=1024, tn=1408, comm=()):
    S, dm = x.shape
    n = w.shape[1]
    tm = min(tm, S)

    def body(x_ref, g_ref, w_ref, xn_ref, o_ref):
        @pl.when(pl.program_id(1) == 0)
        def _():
            xv = x_ref[...]
            r = lax.rsqrt(jnp.mean(xv * xv, axis=-1, keepdims=True) + EPS)
            xn_ref[...] = ((xv * r) * g_ref[...]).astype(bf16)

        o_ref[...] = jnp.dot(xn_ref[...], w_ref[...], preferred_element_type=f32).astype(bf16)

    return _hosted_call(
        body, name=name, grid=(S // tm, n // tn),
        in_specs=[pl.BlockSpec((tm, dm), lambda i, j: (i, 0)),
                  pl.BlockSpec((1, dm), lambda i, j: (0, 0)),
                  pl.BlockSpec((dm, tn), lambda i, j: (0, j))],
        out_specs=[pl.BlockSpec((tm, dm), lambda i, j: (i, 0)),
                   pl.BlockSpec((tm, tn), lambda i, j: (i, j))],
        out_shape=[jax.ShapeDtypeStruct((S, dm), bf16), jax.ShapeDtypeStruct((S, n), bf16)],
        args=(x, g, w), comm=comm)


def _mm_tn(a, b, name, tk, tn, tmc=2048, into=None, col=0, out_cols=None):
    m, ka = a.shape
    n = b.shape[1]
    tmc = min(tmc, m)
    nk = m // tmc

    def body(a_ref, b_ref, *rest):
        o_ref, acc_ref = rest[-2:]
        k = pl.program_id(2)
        part = lax.dot_general(a_ref[...], b_ref[...], (((0,), (0,)), ((), ())), preferred_element_type=f32)

        @pl.when(k == 0)
        def _():
            acc_ref[...] = part

        @pl.when(k > 0)
        def _():
            acc_ref[...] += part

        @pl.when(k == nk - 1)
        def _():
            o_ref[...] = acc_ref[...].astype(bf16)

    in_specs = [pl.BlockSpec((tmc, tk), lambda i, j, k: (k, i)), pl.BlockSpec((tmc, tn), lambda i, j, k: (k, j))]
    if into is None:
        return pl.pallas_call(
            body, name=name, grid=(ka // tk, n // tn, nk), in_specs=in_specs,
            out_specs=pl.BlockSpec((tk, tn), lambda i, j, k: (i, j + col)),
            out_shape=jax.ShapeDtypeStruct((ka, out_cols or n), bf16),
            scratch_shapes=[pltpu.VMEM((tk, tn), f32)],
            compiler_params=_cparams())(a, b)
    return pl.pallas_call(
        body, name=name, grid=(ka // tk, n // tn, nk), in_specs=[*in_specs, ANY_SPEC],
        out_specs=pl.BlockSpec((tk, tn), lambda i, j, k: (i, j + col)),
        out_shape=jax.ShapeDtypeStruct(into.shape, into.dtype),
        scratch_shapes=[pltpu.VMEM((tk, tn), f32)], input_output_aliases={2: 0},
        compiler_params=_cparams())(a, b, into)


HALO = 16


def _rows_at(ext, o, tc):
    if o == 0:
        return ext[HALO:HALO + tc]
    return pltpu.roll(ext, (-o) % ext.shape[0], 0)[HALO:HALO + tc]


def _halo_specs(tc, S, width, col):
    per = tc // HALO
    last = S // HALO - 1
    return (pl.BlockSpec((tc, width), lambda i: (i, col)),
            pl.BlockSpec((HALO, width), lambda i: (jnp.maximum(i * per - 1, 0), col)),
            pl.BlockSpec((HALO, width), lambda i: (jnp.minimum((i + 1) * per, last), col)))


def _extended(cur_ref, prev_ref, next_ref, i, nsteps):
    prev = jnp.where(i > 0, prev_ref[...].astype(f32), 0.0)
    nxt = jnp.where(i < nsteps - 1, next_ref[...].astype(f32), 0.0)
    return jnp.concatenate([prev, cur_ref[...].astype(f32), nxt], axis=0)


def _conv_fwd(proj, cw, cb, tc=512):
    S = proj.shape[0]
    tc = min(tc, S)
    nsteps = S // tc

    def body(cur_ref, prev_ref, next_ref, w_ref, b_ref, o_ref):
        ext = _extended(cur_ref, prev_ref, next_ref, pl.program_id(0), nsteps)
        acc = _rows_at(ext, -2, tc) * w_ref[0:1, :]
        for k in range(1, 4):
            acc = acc + _rows_at(ext, k - 2, tc) * w_ref[k:k + 1, :]
        o_ref[...] = acc + b_ref[...]

    return pl.pallas_call(
        body, name="conv_fwd", grid=(nsteps,),
        in_specs=[*_halo_specs(tc, S, D, 0),
                  pl.BlockSpec((4, D), lambda i: (0, 0)), pl.BlockSpec((1, D), lambda i: (0, 0))],
        out_specs=pl.BlockSpec((tc, D), lambda i: (i, 0)),
        out_shape=jax.ShapeDtypeStruct((S, D), f32),
        compiler_params=_cparams())(proj, proj, proj, cw, cb)


def _conv_bwd(duc_f, duc_b, proj, cw, tc=512, comm=()):
    S = proj.shape[0]
    tc = min(tc, S)
    nsteps = S // tc

    def body(fc, fp, fn, bc, bp, bn, uc_, up, un, w_ref, du_ref, dw_ref, db_ref):
        i = pl.program_id(0)

        @pl.when(i == 0)
        def _():
            dw_ref[...] = jnp.zeros_like(dw_ref)
            db_ref[...] = jnp.zeros_like(db_ref)

        dext = _extended(fc, fp, fn, i, nsteps) + _extended(bc, bp, bn, i, nsteps)
        uext = _extended(uc_, up, un, i, nsteps)
        d = dext[HALO:HALO + tc]
        acc = _rows_at(dext, 2, tc) * w_ref[0:1, :]
        for k in range(1, 4):
            acc = acc + _rows_at(dext, 2 - k, tc) * w_ref[k:k + 1, :]
        du_ref[...] = acc.astype(bf16)
        wrow = lax.broadcasted_iota(jnp.int32, (4, D), 0)
        for k in range(4):
            dw_ref[...] += jnp.where(wrow == k, jnp.sum(d * _rows_at(uext, k - 2, tc), axis=0, keepdims=True), 0.0)
        db_ref[...] += jnp.sum(d, axis=0, keepdims=True)

    return _hosted_call(
        body, name="conv_bwd", grid=(nsteps,),
        in_specs=[*_halo_specs(tc, S, D, 0), *_halo_specs(tc, S, D, 0), *_halo_specs(tc, S, D, 0),
                  pl.BlockSpec((4, D), lambda i: (0, 0))],
        out_specs=[pl.BlockSpec((tc, D), lambda i: (i, 0)),
                   pl.BlockSpec((4, D), lambda i: (0, 0)), pl.BlockSpec((1, D), lambda i: (0, 0))],
        out_shape=[jax.ShapeDtypeStruct((S, D), bf16), jax.ShapeDtypeStruct((4, D), f32),
                   jax.ShapeDtypeStruct((1, D), f32)],
        args=(duc_f, duc_f, duc_f, duc_b, duc_b, duc_b, proj, proj, proj, cw), comm=comm)


def _scan_scratch():
    halves = [pltpu.VMEM((LRU_CHUNK, 128), f32) for _ in range(2 * (LRU_GW // 128))]
    return [*halves, pltpu.VMEM((LRU_CHUNK // 8, LRU_GW), f32), pltpu.VMEM((LRU_CHUNK // 8, LRU_GW), f32)]


def _log_scan(a, b, row, n, reverse, steps):
    for s in steps:
        shift = a.shape[0] - s if reverse else s
        keep = (row < n - s) if reverse else (row >= s)
        a_sh = pltpu.roll(a, shift, 0)
        b_sh = pltpu.roll(b, shift, 0)
        b = jnp.where(keep, a * b_sh + b, b)
        a = jnp.where(keep, a * a_sh, a)
    return a, b


def _scan_chunk(a, b, carry, reverse, *scratch):
    tc, w = a.shape
    ng = tc // 8
    nl = w // 128
    sa_refs, sb_refs, sc_ref, st_ref = scratch[:nl], scratch[nl:2 * nl], scratch[2 * nl], scratch[2 * nl + 1]
    sub = lax.broadcasted_iota(jnp.int32, (8, w), 0)
    ag, bg = [], []
    for k in range(ng):
        ak, bk = _log_scan(a[8 * k:8 * k + 8], b[8 * k:8 * k + 8], sub, 8, reverse, (1, 2, 4))
        ag.append(ak)
        bg.append(bk)
    a = jnp.concatenate(ag, axis=0)
    b = jnp.concatenate(bg, axis=0)
    edge = 0 if reverse else 7
    for i in range(nl):
        sa_refs[i][...] = a[:, 128 * i:128 * (i + 1)]
        sb_refs[i][...] = b[:, 128 * i:128 * (i + 1)]
    ta = jnp.concatenate([r[pl.ds(edge, ng, stride=8), :] for r in sa_refs], axis=1)
    tb = jnp.concatenate([r[pl.ds(edge, ng, stride=8), :] for r in sb_refs], axis=1)
    grow = lax.broadcasted_iota(jnp.int32, (ng, w), 0)
    ta, tb = _log_scan(ta, tb, grow, ng, reverse, [1 << i for i in range(ng.bit_length() - 1)])
    state = tb + ta * carry
    st_ref[...] = state
    if reverse:
        sc_ref[...] = jnp.where(grow == ng - 1, carry, pltpu.roll(state, ng - 1, 0))
    else:
        sc_ref[...] = jnp.where(grow == 0, carry, pltpu.roll(state, 1, 0))
    h = jnp.concatenate([bg[k] + ag[k] * sc_ref[k:k + 1, :] for k in range(ng)], axis=0)
    return h, (st_ref[0:1, :] if reverse else st_ref[ng - 1:ng, :])


def _lru_gates(uc, w, p_ref):
    pre = jnp.dot(uc.astype(bf16), w, preferred_element_type=f32)
    r = _sigmoid_t(pre[:, :LRU_GW] + p_ref[0, 1:2, :])
    gi = _sigmoid_t(pre[:, LRU_GW:] + p_ref[0, 2:3, :])
    sp = _softplus(-p_ref[0, 0:1, :])
    log_a = -RGLRU_C * r * sp
    a = jnp.exp(log_a)
    x = 2.0 * log_a
    series = -x * (1.0 + x * (0.5 + x * (1.0 / 6 + x * (1.0 / 24))))
    beta = jnp.sqrt(jnp.maximum(jnp.where(x > -0.0625, series, 1.0 - a * a), 0.0))
    return r, gi, sp, a, beta


def _lru_fwd(uc, wg, lp, reverse, comm=()):
    S = uc.shape[0]
    tc = LRU_CHUNK
    rows = min(LRU_ROWS, S)
    nsub = rows // tc
    nblk = S // rows
    d = 1 if reverse else 0

    def bidx(c):
        return nblk - 1 - c if reverse else c

    def body(uc_ref, w_ref, p_ref, h_ref, carry_ref, *scan_scratch):
        @pl.when(pl.program_id(1) == 0)
        def _():
            carry_ref[...] = jnp.zeros_like(carry_ref)

        carry = carry_ref[...]
        for j in (reversed(range(nsub)) if reverse else range(nsub)):
            sl = slice(j * tc, (j + 1) * tc)
            ucv = uc_ref[sl, :]
            _, gi, _, a, beta = _lru_gates(ucv, w_ref[0], p_ref)
            h, carry = _scan_chunk(a, beta * (gi * ucv), carry, reverse, *scan_scratch)
            h_ref[sl, :] = h.astype(bf16)
        carry_ref[...] = carry

    return _hosted_call(
        body, name="lru_fwd_rev" if reverse else "lru_fwd", grid=(LRU_GROUPS, nblk),
        in_specs=[pl.BlockSpec((rows, LRU_GW), lambda g, c: (bidx(c), g)),
                  pl.BlockSpec((1, LRU_GW, 2 * LRU_GW), lambda g, c: (g, 0, d)),
                  pl.BlockSpec((1, 8, LRU_GW), lambda g, c: (d, 0, g))],
        out_specs=[pl.BlockSpec((rows, LRU_GW), lambda g, c: (bidx(c), g))],
        out_shape=[jax.ShapeDtypeStruct((S, D), bf16)],
        scratch_shapes=[pltpu.VMEM((1, LRU_GW), f32), *_scan_scratch()],
        args=(uc, wg, lp), comm=comm)


def _lru_bwd(uc, dh, h, wg, lp, reverse, comm=()):
    S = uc.shape[0]
    tc = LRU_CHUNK
    rows = min(LRU_ROWS, S)
    nsub = rows // tc
    nblk = S // rows
    d = 1 if reverse else 0
    per = rows // HALO
    last8 = S // HALO - 1

    def bidx(c):
        return c if reverse else nblk - 1 - c

    def halo_idx(c):
        if reverse:
            return jnp.minimum((bidx(c) + 1) * per, last8)
        return jnp.maximum(bidx(c) * per - 1, 0)

    def body(uc_ref, dh_ref, h_ref, halo_ref, w_ref, p_ref, duc_ref, dw_ref, dp_ref, carry_ref, tmp_ref,
             *scan_scratch):
        c = pl.program_id(1)
        bi = bidx(c)

        @pl.when(c == 0)
        def _():
            carry_ref[...] = jnp.zeros_like(carry_ref)
            dw_ref[...] = jnp.zeros_like(dw_ref)
            dp_ref[...] = jnp.zeros_like(dp_ref)

        row = lax.broadcasted_iota(jnp.int32, (tc, LRU_GW), 0)
        carry = carry_ref[...]
        dw = jnp.zeros((LRU_GW, 2 * LRU_GW), f32)
        dsp = jnp.zeros((1, LRU_GW), f32)
        dba = jnp.zeros((1, LRU_GW), f32)
        dbx = jnp.zeros((1, LRU_GW), f32)
        for j in (range(nsub) if reverse else reversed(range(nsub))):
            sl = slice(j * tc, (j + 1) * tc)
            ucv = uc_ref[sl, :]
            ucb = ucv.astype(bf16)
            r, gi, sp, a, beta = _lru_gates(ucv, w_ref[0], p_ref)
            hv = h_ref[sl, :].astype(f32)
            dhv = dh_ref[sl, :].astype(f32)
            if reverse:
                alpha = jnp.where(row == 0, 1.0, pltpu.roll(a, 1, 0))
                gsc, _ = _scan_chunk(alpha, dhv, carry, False, *scan_scratch)
                if j < nsub - 1:
                    edge = h_ref[(j + 1) * tc:(j + 1) * tc + HALO, :].astype(f32)[0:1, :]
                else:
                    edge = jnp.where(bi < nblk - 1, halo_ref[...].astype(f32)[0:1, :], 0.0)
                h_nb = jnp.where(row == tc - 1, edge, pltpu.roll(hv, tc - 1, 0))
            else:
                alpha = jnp.where(row == tc - 1, 1.0, pltpu.roll(a, tc - 1, 0))
                gsc, _ = _scan_chunk(alpha, dhv, carry, True, *scan_scratch)
                if j > 0:
                    edge = h_ref[j * tc - HALO:j * tc, :].astype(f32)[HALO - 1:HALO, :]
                else:
                    edge = jnp.where(bi > 0, halo_ref[...].astype(f32)[HALO - 1:HALO, :], 0.0)
                h_nb = jnp.where(row == 0, edge, pltpu.roll(hv, 1, 0))
            tmp_ref[...] = a * gsc
            carry = tmp_ref[tc - 1:tc, :] if reverse else tmp_ref[0:1, :]

            da = gsc * h_nb
            dbeta = gsc * (gi * ucv)
            dl = da * a - dbeta * (a * a) / beta
            dr = dl * (-RGLRU_C * sp)
            dsp = dsp + jnp.sum(dl * (-RGLRU_C * r), axis=0, keepdims=True)
            dgi = gsc * beta * ucv
            dpre_r = dr * r * (1.0 - r)
            dpre_i = dgi * gi * (1.0 - gi)
            dba = dba + jnp.sum(dpre_r, axis=0, keepdims=True)
            dbx = dbx + jnp.sum(dpre_i, axis=0, keepdims=True)
            dpre = jnp.concatenate([dpre_r, dpre_i], axis=1).astype(bf16)
            back = lax.dot_general(dpre, w_ref[0], (((1,), (1,)), ((), ())), preferred_element_type=f32)
            duc_ref[sl, :] = (gsc * beta * gi + back).astype(bf16)
            dw = dw + lax.dot_general(ucb, dpre, (((0,), (0,)), ((), ())), preferred_element_type=f32)
        carry_ref[...] = carry
        dw_ref[0] += dw
        dlam = dsp * (-_sigmoid(-p_ref[0, 0:1, :]))
        prow = lax.broadcasted_iota(jnp.int32, (8, LRU_GW), 0)
        dp_ref[...] += (jnp.where(prow == 0, dlam, 0.0) + jnp.where(prow == 1, dba, 0.0)
                        + jnp.where(prow == 2, dbx, 0.0))

    chunk = pl.BlockSpec((rows, LRU_GW), lambda g, c: (bidx(c), g))
    return _hosted_call(
        body, name="lru_bwd_rev" if reverse else "lru_bwd", grid=(LRU_GROUPS, nblk),
        in_specs=[chunk, chunk, chunk,
                  pl.BlockSpec((HALO, LRU_GW), lambda g, c: (halo_idx(c), g)),
                  pl.BlockSpec((1, LRU_GW, 2 * LRU_GW), lambda g, c: (g, 0, d)),
                  pl.BlockSpec((1, 8, LRU_GW), lambda g, c: (d, 0, g))],
        out_specs=[chunk,
                   pl.BlockSpec((1, LRU_GW, 2 * LRU_GW), lambda g, c: (g, 0, 0)),
                   pl.BlockSpec((8, LRU_GW), lambda g, c: (0, g))],
        out_shape=[jax.ShapeDtypeStruct((S, D), bf16),
                   jax.ShapeDtypeStruct((LRU_GROUPS, LRU_GW, 2 * LRU_GW), f32),
                   jax.ShapeDtypeStruct((8, D), f32)],
        scratch_shapes=[pltpu.VMEM((1, LRU_GW), f32), pltpu.VMEM((tc, LRU_GW), f32), *_scan_scratch()],
        args=(uc, dh, h, h, wg, lp), comm=comm)


def _slope(h):
    return 2.0 ** (-8.0 * (h + 1.0) / N_HEADS)


def _kv_specs(nb, col):
    return [pl.BlockSpec((BLK, N_KV * HEAD_DIM), lambda n: (jnp.maximum(n - 1, 0), col)),
            pl.BlockSpec((BLK, N_KV * HEAD_DIM), lambda n: (n, col)),
            pl.BlockSpec((BLK, N_KV * HEAD_DIM), lambda n: (jnp.minimum(n + 1, nb - 1), col))]


def _dup_windows(r0, r1, r2):
    left = lax.broadcasted_iota(jnp.int32, (3 * BLK, 128), 1) < HEAD_DIM
    win = jnp.concatenate([r0[...], r1[...], r2[...]], axis=0)
    out = []
    for i in range(N_KV // 2):
        t = win[:, i * 128:(i + 1) * 128]
        r = pltpu.roll(t, HEAD_DIM, 1)
        out += [jnp.where(left, t, r).astype(bf16), jnp.where(left, r, t).astype(bf16)]
    return out


def _attn_bias_init(bias_ref):
    k_loc = lax.broadcasted_iota(jnp.int32, (3 * BLK, BLK), 0)
    q_loc = lax.broadcasted_iota(jnp.int32, (3 * BLK, BLK), 1)
    adist = jnp.abs(q_loc + BLK - k_loc)
    adf = adist.astype(f32)
    for e in range(3):
        ok = adist <= WINDOW
        if e == 0:
            ok = ok & (k_loc >= BLK)
        if e == 2:
            ok = ok & (k_loc < 2 * BLK)
        for kv in range(N_KV):
            bias_ref[e, kv] = jnp.concatenate(
                [jnp.where(ok, (-_slope(4 * kv + j)) * adf, NEG_INF) for j in range(4)], axis=1)


def _stack_heads(ref, kv, scale):
    left = lax.broadcasted_iota(jnp.int32, (BLK, 128), 1) < HEAD_DIM
    rows = []
    for pp in range(2):
        t = ref[:, (2 * kv + pp) * 128:(2 * kv + pp + 1) * 128]
        if scale != 1.0:
            t = t * scale
        zero = jnp.zeros_like(t)
        rows += [jnp.where(left, t, zero).astype(bf16), jnp.where(left, zero, t).astype(bf16)]
    return jnp.concatenate(rows, axis=0)


def _attn_softmax(qs, k2, bias, sink_ref, kv, stats=None):
    sink = jnp.concatenate([jnp.full((1, BLK), sink_ref[0, 4 * kv + j], f32) for j in range(4)], axis=1)
    s = lax.dot_general(k2, qs, (((1,), (1,)), ((), ())), preferred_element_type=f32) + bias
    m = jnp.maximum(jnp.max(s, axis=0, keepdims=True), sink) if stats is None else stats[0]
    p = jnp.exp(s - m)
    ps = jnp.exp(sink - m)
    inv = 1.0 / (jnp.sum(p, axis=0, keepdims=True) + ps) if stats is None else stats[1]
    return p, ps, m, inv


def _pair_tiles(t):
    return [jnp.concatenate([t[:HEAD_DIM, 256 * pp:256 * pp + 128],
                             t[HEAD_DIM:, 256 * pp + 128:256 * pp + 256]], axis=0).T for pp in range(2)]


def _attn_fwd(proj, sink, comm=()):
    S = proj.shape[0]
    nb = S // BLK
    assert nb >= 2

    def body(q_ref, k0, k1, k2_, v0, v1, v2_, sink_ref, o_ref, st_ref, bias_ref):
        n = pl.program_id(0)

        @pl.when(n == 0)
        def _():
            _attn_bias_init(bias_ref)

        e = jnp.where(n == 0, 0, jnp.where(n == nb - 1, 2, 1))
        kk = _dup_windows(k0, k1, k2_)
        vv = _dup_windows(v0, v1, v2_)
        tiles = []
        for kv in range(N_KV):
            qs = _stack_heads(q_ref, kv, HEAD_DIM ** -0.5)
            p, _, m, inv = _attn_softmax(qs, kk[kv], bias_ref[e, kv], sink_ref, kv)
            st_ref[0, kv:kv + 1, :] = m
            st_ref[0, N_KV + kv:N_KV + kv + 1, :] = inv
            ot = lax.dot_general(vv[kv], p.astype(bf16), (((0,), (0,)), ((), ())), preferred_element_type=f32)
            tiles += _pair_tiles(ot * inv)
        o_ref[...] = jnp.concatenate(tiles, axis=1).astype(bf16)

    return _hosted_call(
        body, name="attn_fwd", grid=(nb,),
        in_specs=[pl.BlockSpec((BLK, D), lambda n: (n, C_Q // D)),
                  *_kv_specs(nb, C_K // (N_KV * HEAD_DIM)), *_kv_specs(nb, C_V // (N_KV * HEAD_DIM)),
                  pl.BlockSpec(memory_space=pltpu.SMEM)],
        out_specs=[pl.BlockSpec((BLK, D), lambda n: (n, 0)), pl.BlockSpec((1, 2 * N_KV, 4 * BLK), lambda n: (n, 0, 0))],
        out_shape=[jax.ShapeDtypeStruct((S, D), bf16), jax.ShapeDtypeStruct((nb, 2 * N_KV, 4 * BLK), f32)],
        scratch_shapes=[pltpu.VMEM((3, N_KV, 3 * BLK, 4 * BLK), f32)],
        args=(proj, proj, proj, proj, proj, proj, proj, sink), comm=comm)


def _attn_bwd(proj, sink, dyb, stats, comm=()):
    S = proj.shape[0]
    nb = S // BLK
    assert nb >= 2

    def body(q_ref, k0, k1, k2_, v0, v1, v2_, sink_ref, do_ref, st_ref, dq_ref, dk_out, dv_out, ds_ref,
             bias_ref, dk_ref, dv_ref, dsk_ref):
        n = pl.program_id(0)

        @pl.when(n == 0)
        def _():
            _attn_bias_init(bias_ref)
            dk_ref[...] = jnp.zeros_like(dk_ref)
            dv_ref[...] = jnp.zeros_like(dv_ref)
            dsk_ref[...] = jnp.zeros_like(dsk_ref)

        e = jnp.where(n == 0, 0, jnp.where(n == nb - 1, 2, 1))
        kk = _dup_windows(k0, k1, k2_)
        vv = _dup_windows(v0, v1, v2_)
        left3 = lax.broadcasted_iota(jnp.int32, (3 * BLK, 128), 1) < HEAD_DIM
        start = pl.multiple_of(n * BLK, BLK)
        dq_tiles, dks, dvs = [], [], []
        for kv in range(N_KV):
            qs = _stack_heads(q_ref, kv, HEAD_DIM ** -0.5)
            dos = _stack_heads(do_ref, kv, 1.0)
            stats = (st_ref[0, kv:kv + 1, :], st_ref[0, N_KV + kv:N_KV + kv + 1, :])
            p, ps, _, inv = _attn_softmax(qs, kk[kv], bias_ref[e, kv], sink_ref, kv, stats)
            pn = p * inv
            dp = lax.dot_general(vv[kv], dos, (((1,), (1,)), ((), ())), preferred_element_type=f32)
            delta = jnp.sum(pn * dp, axis=0, keepdims=True)
            dsc = (pn * (dp - delta)).astype(bf16)
            dsk_ref[kv:kv + 1, :] += delta * (ps * inv)
            dqt = lax.dot_general(kk[kv], dsc, (((0,), (0,)), ((), ())), preferred_element_type=f32)
            dq_tiles += _pair_tiles(dqt * (HEAD_DIM ** -0.5))
            dk = jnp.dot(dsc, qs, preferred_element_type=f32)
            dv = jnp.dot(pn.astype(bf16), dos, preferred_element_type=f32)
            dks.append(dk + pltpu.roll(dk, HEAD_DIM, 1))
            dvs.append(dv + pltpu.roll(dv, HEAD_DIM, 1))
        for jp in range(N_KV // 2):
            cols = slice(jp * 128, (jp + 1) * 128)
            dk_ref[pl.ds(start, 3 * BLK), cols] += jnp.where(left3, dks[2 * jp], dks[2 * jp + 1])
            dv_ref[pl.ds(start, 3 * BLK), cols] += jnp.where(left3, dvs[2 * jp], dvs[2 * jp + 1])
        dq_ref[...] = jnp.concatenate(dq_tiles, axis=1).astype(bf16)

        @pl.when(n == nb - 1)
        def _():
            pltpu.sync_copy(dk_ref, dk_out)
            pltpu.sync_copy(dv_ref, dv_out)
            lane = lax.broadcasted_iota(jnp.int32, (1, 128), 1)
            dsink = jnp.zeros((1, 128), f32)
            for h in range(N_HEADS):
                part = dsk_ref[h // 4:h // 4 + 1, (h % 4) * BLK:(h % 4 + 1) * BLK]
                dsink = dsink + jnp.where(lane == h, -jnp.sum(part), 0.0)
            ds_ref[...] = dsink

    acc = jax.ShapeDtypeStruct((S + 2 * BLK, N_KV * HEAD_DIM), f32)
    return _hosted_call(
        body, name="attn_bwd", grid=(nb,),
        in_specs=[pl.BlockSpec((BLK, D), lambda n: (n, C_Q // D)),
                  *_kv_specs(nb, C_K // (N_KV * HEAD_DIM)), *_kv_specs(nb, C_V // (N_KV * HEAD_DIM)),
                  pl.BlockSpec(memory_space=pltpu.SMEM),
                  pl.BlockSpec((BLK, D), lambda n: (n, 0)),
                  pl.BlockSpec((1, 2 * N_KV, 4 * BLK), lambda n: (n, 0, 0))],
        out_specs=[pl.BlockSpec((BLK, D), lambda n: (n, 0)), ANY_SPEC, ANY_SPEC,
                   pl.BlockSpec((1, 128), lambda n: (0, 0))],
        out_shape=[jax.ShapeDtypeStruct((S, D), bf16), acc, acc, jax.ShapeDtypeStruct((1, 128), f32)],
        scratch_shapes=[pltpu.VMEM((3, N_KV, 3 * BLK, 4 * BLK), f32), pltpu.VMEM(acc.shape, f32),
                        pltpu.VMEM(acc.shape, f32), pltpu.VMEM((8, 4 * BLK), f32)],
        args=(proj, proj, proj, proj, proj, proj, proj, sink, dyb, stats), comm=comm)


def _merge_parts(hf, hb, g, z0, z1, yb, bg):
    g0 = _sigmoid(z0.astype(f32) + bg[:, :D])
    g1 = _sigmoid(z1.astype(f32) + bg[:, D:])
    gelu, dgelu = _gelu_and_grad(g.astype(f32))
    hs = hf.astype(f32) + hb.astype(f32)
    ya = hs * gelu
    return g0, g1, gelu, dgelu, hs, ya


def _merge_outproj(x, hf, hb, proj, yb, bg, w_out, tm=512):
    S = x.shape[0]
    tm = min(tm, S)

    def body(x_ref, hf_ref, hb_ref, g_ref, z0_ref, z1_ref, yb_ref, bg_ref, w_ref, mg_ref, x1_ref):
        ybv = yb_ref[...].astype(f32)
        g0, g1, _, _, _, ya = _merge_parts(hf_ref[...], hb_ref[...], g_ref[...], z0_ref[...], z1_ref[...],
                                           ybv, bg_ref[...])
        mg = (g0 * ya + g1 * ybv).astype(bf16)
        mg_ref[...] = mg
        x1_ref[...] = x_ref[...] + jnp.dot(mg, w_ref[...], preferred_element_type=f32)

    row = pl.BlockSpec((tm, D), lambda i: (i, 0))
    return pl.pallas_call(
        body, name="merge_outproj", grid=(S // tm,),
        in_specs=[row, row, row,
                  pl.BlockSpec((tm, D), lambda i: (i, C_G // D)),
                  pl.BlockSpec((tm, D), lambda i: (i, C_Z0 // D)),
                  pl.BlockSpec((tm, D), lambda i: (i, C_Z1 // D)),
                  row, pl.BlockSpec((1, 2 * D), lambda i: (0, 0)), pl.BlockSpec((D, D), lambda i: (0, 0))],
        out_specs=[row, row],
        out_shape=[jax.ShapeDtypeStruct((S, D), bf16), jax.ShapeDtypeStruct((S, D), f32)],
        compiler_params=_cparams())(x, hf, hb, proj, proj, proj, yb, bg, w_out)


def _ffn_out_loss(gu, x1, w_fo, g3, tgt, tm=512):
    S = x1.shape[0]
    tm = min(tm, S)

    def body(gt_ref, up_ref, x1_ref, w_ref, g_ref, t_ref, ff_ref, dx_ref, dxb_ref, loss_ref, dg_ref):
        @pl.when(pl.program_id(0) == 0)
        def _():
            loss_ref[...] = jnp.zeros_like(loss_ref)
            dg_ref[...] = jnp.zeros_like(dg_ref)

        gt = gt_ref[...].astype(f32)
        ff = ((gt * _sigmoid(gt)) * up_ref[...].astype(f32)).astype(bf16)
        ff_ref[...] = ff
        x2 = x1_ref[...] + jnp.dot(ff, w_ref[...], preferred_element_type=f32)
        gv = g_ref[...]
        r = lax.rsqrt(jnp.mean(x2 * x2, axis=-1, keepdims=True) + EPS)
        xh = x2 * r
        diff = xh * gv - t_ref[...]
        loss_ref[...] += (0.5 / D) * jnp.sum(diff * diff)
        dy = diff * (1.0 / D)
        dg_ref[...] += jnp.sum(dy * xh, axis=0, keepdims=True)
        dxh = dy * gv
        dx = r * (dxh - xh * jnp.mean(dxh * xh, axis=-1, keepdims=True))
        dx_ref[...] = dx
        dxb_ref[...] = dx.astype(bf16)

    row = pl.BlockSpec((tm, D), lambda i: (i, 0))
    vec = pl.BlockSpec((1, D), lambda i: (0, 0))
    return pl.pallas_call(
        body, name="ffn_out_loss", grid=(S // tm,),
        in_specs=[pl.BlockSpec((tm, D_FF), lambda i: (i, 0)), pl.BlockSpec((tm, D_FF), lambda i: (i, 1)),
                  row, pl.BlockSpec((D_FF, D), lambda i: (0, 0)), vec, row],
        out_specs=[pl.BlockSpec((tm, D_FF), lambda i: (i, 0)), row, row,
                   pl.BlockSpec((1, 128), lambda i: (0, 0)), vec],
        out_shape=[jax.ShapeDtypeStruct((S, D_FF), bf16), jax.ShapeDtypeStruct((S, D), f32),
                   jax.ShapeDtypeStruct((S, D), bf16), jax.ShapeDtypeStruct((1, 128), f32),
                   jax.ShapeDtypeStruct((1, D), f32)],
        compiler_params=_cparams())(gu, gu, x1, w_fo, g3, tgt)


def _ffn_bwd1(dx2b, w_fo, gu, tm=512, comm=()):
    S = dx2b.shape[0]
    tm = min(tm, S)

    def body(dx_ref, w_ref, gt_ref, up_ref, dgt_ref, dup_ref):
        dff = lax.dot_general(dx_ref[...], w_ref[...], (((1,), (1,)), ((), ())), preferred_element_type=f32)
        gt = gt_ref[...].astype(f32)
        sg = _sigmoid(gt)
        dup_ref[...] = (dff * (gt * sg)).astype(bf16)
        dgt_ref[...] = ((dff * up_ref[...].astype(f32)) * (sg * (1.0 + gt * (1.0 - sg)))).astype(bf16)

    wide = pl.BlockSpec((tm, D_FF), lambda i: (i, 0))
    return _hosted_call(
        body, name="ffn_bwd1", grid=(S // tm,),
        in_specs=[pl.BlockSpec((tm, D), lambda i: (i, 0)), pl.BlockSpec((D_FF, D), lambda i: (0, 0)),
                  wide, pl.BlockSpec((tm, D_FF), lambda i: (i, 1))],
        out_specs=[wide, wide],
        out_shape=[jax.ShapeDtypeStruct((S, D_FF), bf16), jax.ShapeDtypeStruct((S, D_FF), bf16)],
        args=(dx2b, w_fo, gu, gu), comm=comm)


def _proj_bwd(pieces, w, xres, g, dres, name, tm=256, comm=()):
    S = xres.shape[0]
    tm = min(tm, S)
    np_ = len(pieces)

    def body(*refs):
        p_refs = refs[:np_]
        w_refs = refs[np_:2 * np_]
        x_ref, g_ref, dres_ref, dx_ref, dxb_ref, dg_ref = refs[2 * np_:]

        @pl.when(pl.program_id(0) == 0)
        def _():
            dg_ref[...] = jnp.zeros_like(dg_ref)

        nt = (((1,), (1,)), ((), ()))
        dn = lax.dot_general(p_refs[0][...], w_refs[0][...], nt, preferred_element_type=f32)
        for pr, wr in zip(p_refs[1:], w_refs[1:]):
            dn = dn + lax.dot_general(pr[...], wr[...], nt, preferred_element_type=f32)
        dxn, dgc = _rms_bwd(dn, x_ref[...], g_ref[...])
        dx = dres_ref[...] + dxn
        dx_ref[...] = dx
        dxb_ref[...] = dx.astype(bf16)
        dg_ref[...] += jnp.sum(dgc, axis=0, keepdims=True)

    row = pl.BlockSpec((tm, D), lambda i: (i, 0))
    vec = pl.BlockSpec((1, D), lambda i: (0, 0))
    return _hosted_call(
        body, name=name, grid=(S // tm,),
        in_specs=[*[pl.BlockSpec((tm, wd), functools.partial(lambda i, cb: (i, cb), cb=acb))
                    for _, acb, _, wd in pieces],
                  *[pl.BlockSpec((D, wd), functools.partial(lambda i, cb: (0, cb), cb=wcb))
                    for _, _, wcb, wd in pieces],
                  row, vec, row],
        out_specs=[row, row, vec],
        out_shape=[jax.ShapeDtypeStruct((S, D), f32), jax.ShapeDtypeStruct((S, D), bf16),
                   jax.ShapeDtypeStruct((1, D), f32)],
        args=(*[p[0] for p in pieces], *[w] * np_, xres, g, dres), comm=comm)


def _outproj_bwd(dx1b, w_out, hf, hb, proj, yb, bg, tm=512):
    S = dx1b.shape[0]
    tm = min(tm, S)

    def body(dx_ref, w_ref, hf_ref, hb_ref, g_ref, z0_ref, z1_ref, yb_ref, bg_ref,
             dh_ref, dg_ref, dz_ref, dyb_ref, dbg_ref):
        @pl.when(pl.program_id(0) == 0)
        def _():
            dbg_ref[...] = jnp.zeros_like(dbg_ref)

        dm = lax.dot_general(dx_ref[...], w_ref[...], (((1,), (1,)), ((), ())), preferred_element_type=f32)
        ybv = yb_ref[...].astype(f32)
        g0, g1, gelu, dgelu, hs, ya = _merge_parts(hf_ref[...], hb_ref[...], g_ref[...], z0_ref[...],
                                                   z1_ref[...], ybv, bg_ref[...])
        dya = dm * g0
        dh_ref[...] = (dya * gelu).astype(bf16)
        dg_ref[...] = (dya * hs * dgelu).astype(bf16)
        dyb_ref[...] = (dm * g1).astype(bf16)
        dz0 = (dm * ya) * (g0 * (1.0 - g0))
        dz1 = (dm * ybv) * (g1 * (1.0 - g1))
        dz = jnp.concatenate([dz0, dz1], axis=1)
        dz_ref[...] = dz.astype(bf16)
        dbg_ref[...] += jnp.sum(dz, axis=0, keepdims=True)

    row = pl.BlockSpec((tm, D), lambda i: (i, 0))
    return pl.pallas_call(
        body, name="outproj_bwd", grid=(S // tm,),
        in_specs=[row, pl.BlockSpec((D, D), lambda i: (0, 0)), row, row,
                  pl.BlockSpec((tm, D), lambda i: (i, C_G // D)),
                  pl.BlockSpec((tm, D), lambda i: (i, C_Z0 // D)),
                  pl.BlockSpec((tm, D), lambda i: (i, C_Z1 // D)),
                  row, pl.BlockSpec((1, 2 * D), lambda i: (0, 0))],
        out_specs=[row, row, pl.BlockSpec((tm, 2 * D), lambda i: (i, 0)), row,
                   pl.BlockSpec((1, 2 * D), lambda i: (0, 0))],
        out_shape=[jax.ShapeDtypeStruct((S, D), bf16), jax.ShapeDtypeStruct((S, D), bf16),
                   jax.ShapeDtypeStruct((S, 2 * D), bf16), jax.ShapeDtypeStruct((S, D), bf16),
                   jax.ShapeDtypeStruct((1, 2 * D), f32)],
        compiler_params=_cparams())(dx1b, w_out, hf, hb, proj, proj, proj, yb, bg)


def _block_diag_groups(w):
    w4 = w.reshape(LRU_GROUPS, 4, LRU_BLOCK, LRU_BLOCK)
    eye = jnp.eye(4, dtype=w.dtype)
    return jnp.einsum("ghij,hk->ghikj", w4, eye).reshape(LRU_GROUPS, LRU_GW, LRU_GW)


def _diag_blocks(dw):
    d5 = dw.reshape(LRU_GROUPS, 4, LRU_BLOCK, 4, LRU_BLOCK)
    return jnp.stack([d5[:, h, :, h, :] for h in range(4)], axis=1).reshape(LRU_HEADS, LRU_BLOCK, LRU_BLOCK)


def _local_step(x, tgt, small, env, before=lambda name: (), after=lambda name, got: None):
    S = x.shape[0]
    g1, g2, g3 = small["norm_mix_g"], small["norm_ffn_g"], small["norm_final_g"]
    bg, cw, cb = small["b_gate"], small["conv_w"], small["conv_b"]
    sink = small["attn_sink"]

    wg = jnp.concatenate([_block_diag_groups(small["lru_wa"][0]), _block_diag_groups(small["lru_wx"][0]),
                          _block_diag_groups(small["lru_wa"][1]), _block_diag_groups(small["lru_wx"][1])],
                         axis=2).astype(bf16)
    zeros5 = jnp.zeros((5, D), f32)
    lp = jnp.stack([jnp.concatenate([small["lru_lambda"][d:d + 1], small["lru_ba"][d:d + 1],
                                     small["lru_bx"][d:d + 1], zeros5], axis=0) for d in range(2)])

    def hosted(name, fn, *args, **kw):
        outs, got = fn(*args, comm=tuple(before(name)), **kw)
        after(name, got)
        return outs

    xn, proj = hosted("norm_inproj", _norm_matmul, x, g1, env["w_in_p"], "norm_inproj")
    uc = _conv_fwd(proj, cw, cb)
    (hf,) = hosted("lru_fwd", _lru_fwd, uc, wg, lp, False)
    (hb,), _ = _lru_fwd(uc, wg, lp, True)
    yb, attn_stats = hosted("attn_fwd", _attn_fwd, proj, sink)
    merged, x1 = _merge_outproj(x, hf, hb, proj, yb, bg, env["w_out"])
    (xn2, gu), _ = _norm_matmul(x1, g2, env["w_fi"], "norm_ffn_in")
    ff, dx2, dx2b, loss, dg3 = _ffn_out_loss(gu, x1, env["w_fo"], g3, tgt)

    env["dw_fo"] = _mm_tn(ff, dx2b, "dw_ffn_out", tk=1408, tn=1024)
    dgt, dup = hosted("ffn_bwd1", _ffn_bwd1, dx2b, env["w_fo"], gu)
    dw_gate = _mm_tn(xn2, dgt, "dw_ffn_in_gate", tk=1024, tn=1408, out_cols=2 * D_FF)
    env["dw_fi"] = _mm_tn(xn2, dup, "dw_ffn_in_up", tk=1024, tn=1408, into=dw_gate, col=D_FF // 1408)
    (dx1, dx1b, dg2), _ = _proj_bwd([(dgt, 0, 0, D_FF), (dup, 0, 1, D_FF)], env["w_fi"], x1, g2, dx2, "ffn_in_bwd")
    env["dw_out"] = _mm_tn(merged, dx1b, "dw_out", tk=1024, tn=1024)
    dh, dgl, dz, dyb, dbg = _outproj_bwd(dx1b, env["w_out"], hf, hb, proj, yb, bg)
    dq, dk2, dv2, dsink = hosted("attn_bwd", _attn_bwd, proj, sink, dyb, attn_stats)
    dkv = jnp.concatenate([dk2[BLK:BLK + S], dv2[BLK:BLK + S]], axis=1).astype(bf16)
    duc_f, dwg_f, dp_f = hosted("lru_bwd", _lru_bwd, uc, dh, hf, wg, lp, False)
    (duc_b, dwg_b, dp_b), _ = _lru_bwd(uc, dh, hb, wg, lp, True)
    env["grads_early"] = {
        "loss": loss[:, :1], "b_gate": dbg,
        "lru_lambda": jnp.concatenate([dp_f[0:1], dp_b[0:1]], axis=0),
        "lru_wa": jnp.stack([_diag_blocks(dwg_f[:, :, :LRU_GW]), _diag_blocks(dwg_b[:, :, :LRU_GW])]),
        "lru_ba": jnp.concatenate([dp_f[1:2], dp_b[1:2]], axis=0),
        "lru_wx": jnp.stack([_diag_blocks(dwg_f[:, :, LRU_GW:]), _diag_blocks(dwg_b[:, :, LRU_GW:])]),
        "lru_bx": jnp.concatenate([dp_f[2:3], dp_b[2:3]], axis=0),
        "attn_sink": dsink[:, :N_HEADS], "norm_ffn_g": dg2, "norm_final_g": dg3,
    }
    du, dcw, dcb = hosted("conv_bwd", _conv_bwd, duc_f, duc_b, proj, cw)
    dw_in = _mm_tn(xn, du, "dw_in_u", tk=1024, tn=1024, out_cols=IN_W)
    dw_in = _mm_tn(xn, dgl, "dw_in_g", tk=1024, tn=1024, into=dw_in, col=1)
    dw_in = _mm_tn(xn, dq, "dw_in_q", tk=1024, tn=1024, into=dw_in, col=2)
    dw_in = _mm_tn(xn, dkv, "dw_in_kv", tk=1024, tn=512, into=dw_in, col=3072 // 512)
    env["dw_in"] = _mm_tn(xn, dz, "dw_in_z", tk=1024, tn=512, into=dw_in, col=3584 // 512)
    col_pieces = [(du, 0, C_U // D, D), (dgl, 0, C_G // D, D), (dq, 0, C_Q // D, D), (dz, 0, C_Z0 // D, D),
                  (dz, 1, C_Z1 // D, D), (dkv, 0, C_K // 512, 512)]
    dx, _, dg1 = hosted("inproj_bwd", _proj_bwd, col_pieces, env["w_in_p"], x, g1, dx1, "inproj_bwd")

    grads = dict(env["grads_early"], norm_mix_g=dg1, conv_w=dcw, conv_b=dcb)
    return dx, grads


def _adamw(gparts, w, m, v, name, tr=256):
    n, rows, cols = gparts.shape
    tr = _div_tile(rows, tr)
    c1 = 1.0 - ADAM_B1 ** ADAM_STEP
    c2 = 1.0 - ADAM_B2 ** ADAM_STEP

    def body(g_ref, w_ref, m_ref, v_ref, go_ref, d_ref, mo_ref, vo_ref):
        g = g_ref[0].astype(f32)
        for j in range(1, n):
            g = g + g_ref[j].astype(f32)
        mn = ADAM_B1 * m_ref[0] + (1.0 - ADAM_B1) * g
        vn = ADAM_B2 * v_ref[0] + (1.0 - ADAM_B2) * (g * g)
        m_hat = mn / c1
        v_hat = vn / c2
        go_ref[0] = g
        d_ref[0] = -ADAM_LR * (m_hat / (jnp.sqrt(v_hat) + ADAM_EPS) + ADAM_WD * w_ref[0])
        mo_ref[0] = mn
        vo_ref[0] = vn

    blk = pl.BlockSpec((1, tr, cols), lambda i: (0, i, 0))
    shp = jax.ShapeDtypeStruct((1, rows, cols), f32)
    return pl.pallas_call(
        body, name=name, grid=(rows // tr,),
        in_specs=[pl.BlockSpec((n, tr, cols), lambda i: (0, i, 0)), blk, blk, blk],
        out_specs=[blk, blk, blk, blk], out_shape=[shp, shp, shp, shp],
        compiler_params=_cparams())(gparts, w, m, v)


def _sum_parts(parts, name):
    n, rows, cols = parts.shape

    def body(p_ref, o_ref):
        acc = p_ref[0].astype(f32)
        for j in range(1, n):
            acc = acc + p_ref[j].astype(f32)
        o_ref[...] = acc

    return pl.pallas_call(
        body, name=name, out_shape=jax.ShapeDtypeStruct((rows, cols), f32),
        compiler_params=_cparams())(parts)


def _pack_rows(arrs, dtype=f32):
    rows, spans, at = [], [], 0
    for a in arrs:
        flat = a.reshape(-1).astype(dtype)
        nr = -(-flat.shape[0] // 1024)
        rows.append(jnp.pad(flat, (0, nr * 1024 - flat.shape[0])).reshape(nr, 1024))
        spans.append((at, nr))
        at += nr
    pad = (-at) % 16
    if pad:
        rows.append(jnp.zeros((pad, 1024), dtype))
    return jnp.concatenate(rows, axis=0), spans


def _unpack_rows(packed, spans, shapes):
    out = []
    for (at, nr), shp in zip(spans, shapes):
        n = math.prod(shp)
        out.append(packed[at:at + nr].reshape(-1)[:n].reshape(shp))
    return out


BIG = ("w_in", "w_out", "w_ffn_in", "w_ffn_out")
SMALL_REPL = ("norm_mix_g", "b_gate", "conv_b", "lru_wa", "lru_wx", "attn_sink", "norm_ffn_g", "norm_final_g")
SMALL_SHARD = ("conv_w", "lru_lambda", "lru_ba", "lru_bx")
ORDER = ("norm_mix_g", "w_in", "b_gate", "conv_w", "conv_b", "lru_lambda", "lru_wa", "lru_ba", "lru_wx",
         "lru_bx", "attn_sink", "w_out", "norm_ffn_g", "w_ffn_in", "w_ffn_out", "norm_final_g")
EARLY_F32 = ("loss", "b_gate", "lru_lambda", "lru_ba", "lru_bx", "attn_sink", "norm_ffn_g", "norm_final_g")
EARLY_BF16 = ("lru_wa", "lru_wx")
LATE = ("norm_mix_g", "conv_w", "conv_b")


def kernel(x, norm_mix_g, w_in, b_gate, conv_w, conv_b, lru_lambda, lru_wa, lru_ba, lru_wx, lru_bx, attn_sink, w_out, norm_ffn_g, w_ffn_in, w_ffn_out, norm_final_g, loss_target, m_norm_mix_g, m_w_in, m_b_gate, m_conv_w, m_conv_b, m_lru_lambda, m_lru_wa, m_lru_ba, m_lru_wx, m_lru_bx, m_attn_sink, m_w_out, m_norm_ffn_g, m_w_ffn_in, m_w_ffn_out, m_norm_final_g, v_norm_mix_g, v_w_in, v_b_gate, v_conv_w, v_conv_b, v_lru_lambda, v_lru_wa, v_lru_ba, v_lru_wx, v_lru_bx, v_attn_sink, v_w_out, v_norm_ffn_g, v_w_ffn_in, v_w_ffn_out, v_norm_final_g):
    w = dict(norm_mix_g=norm_mix_g, w_in=w_in, b_gate=b_gate, conv_w=conv_w, conv_b=conv_b, lru_lambda=lru_lambda,
             lru_wa=lru_wa, lru_ba=lru_ba, lru_wx=lru_wx, lru_bx=lru_bx, attn_sink=attn_sink, w_out=w_out,
             norm_ffn_g=norm_ffn_g, w_ffn_in=w_ffn_in, w_ffn_out=w_ffn_out, norm_final_g=norm_final_g)
    m = dict(norm_mix_g=m_norm_mix_g, w_in=m_w_in, b_gate=m_b_gate, conv_w=m_conv_w, conv_b=m_conv_b,
             lru_lambda=m_lru_lambda, lru_wa=m_lru_wa, lru_ba=m_lru_ba, lru_wx=m_lru_wx, lru_bx=m_lru_bx,
             attn_sink=m_attn_sink, w_out=m_w_out, norm_ffn_g=m_norm_ffn_g, w_ffn_in=m_w_ffn_in,
             w_ffn_out=m_w_ffn_out, norm_final_g=m_norm_final_g)
    v = dict(norm_mix_g=v_norm_mix_g, w_in=v_w_in, b_gate=v_b_gate, conv_w=v_conv_w, conv_b=v_conv_b,
             lru_lambda=v_lru_lambda, lru_wa=v_lru_wa, lru_ba=v_lru_ba, lru_wx=v_lru_wx, lru_bx=v_lru_bx,
             attn_sink=v_attn_sink, w_out=v_w_out, norm_ffn_g=v_norm_ffn_g, w_ffn_in=v_w_ffn_in,
             w_ffn_out=v_w_ffn_out, norm_final_g=v_norm_final_g)
    me = 4 * lax.axis_index("x") + 2 * lax.axis_index("y") + lax.axis_index("c")

    def cols_full(got):
        return jnp.swapaxes(got, 0, 1).reshape(got.shape[1], -1)

    def cols_parts(g):
        return jnp.swapaxes(g.reshape(g.shape[0], N_DEV, -1), 0, 1)

    def rows_parts(g):
        return g.reshape(N_DEV, -1, g.shape[1])

    shard_rows = jnp.concatenate([w[n][0] for n in SMALL_SHARD], axis=0)
    got_w_in, got_rows = _exchange([(w_in[0].astype(bf16), False), (shard_rows, False)], "gather_w_in")
    full_rows = cols_full(got_rows)
    small = {n: w[n] for n in ("norm_mix_g", "b_gate", "conv_b", "attn_sink", "norm_ffn_g")}
    small["lru_wa"], small["lru_wx"] = lru_wa[0], lru_wx[0]
    small["norm_final_g"] = norm_final_g.reshape(1, D)
    small["conv_w"], small["lru_lambda"] = full_rows[0:4], full_rows[4:6]
    small["lru_ba"], small["lru_bx"] = full_rows[6:8], full_rows[8:10]

    env = {"w_in_p": _perm_cols(cols_full(got_w_in))}
    recv = {}

    def before(name):
        if name == "norm_inproj":
            return [(w_out[0].astype(bf16), False), (w_ffn_out[0].astype(bf16), False)]
        if name == "lru_fwd":
            return [(w_ffn_in[0].astype(bf16), False)]
        if name == "ffn_bwd1":
            return [(rows_parts(env["dw_fo"]).astype(bf16), True)]
        if name == "attn_bwd":
            return [(rows_parts(env["dw_out"]).astype(bf16), True)]
        if name == "lru_bwd":
            return [(cols_parts(env["dw_fi"]).astype(bf16), True)]
        if name == "conv_bwd":
            ge = env["grads_early"]
            p32, env["early_f32_spans"] = _pack_rows([ge[n] for n in EARLY_F32])
            p16, env["early_bf16_spans"] = _pack_rows([ge[n] for n in EARLY_BF16], bf16)
            return [(p32, False), (p16, False)]
        if name == "inproj_bwd":
            return [(cols_parts(env["dw_in"]).astype(bf16), True)]
        return []

    def after(name, got):
        if name == "norm_inproj":
            env["w_out"], env["w_fo"] = got[0].reshape(D, D), got[1].reshape(D_FF, D)
        elif name == "lru_fwd":
            env["w_fi"] = cols_full(got[0])
        elif name == "ffn_bwd1":
            recv["w_ffn_out"] = got[0]
        elif name == "attn_bwd":
            recv["w_out"] = got[0]
        elif name == "lru_bwd":
            recv["w_ffn_in"] = got[0]
        elif name == "conv_bwd":
            recv["early_f32"], recv["early_bf16"] = got
        elif name == "inproj_bwd":
            recv["w_in"] = got[0]

    grad_x, grads = _local_step(x[0], loss_target[0], small, env, before, after)

    outs = {}
    for name in BIG:
        outs[name] = _adamw(recv[name], w[name], m[name], v[name], "adamw_" + name)

    small_names = SMALL_REPL + SMALL_SHARD
    late_packed, late_spans = _pack_rows([grads[n] for n in LATE])
    (got_late,) = _exchange([(late_packed, False)], "gather_late_grads")
    summed = {}
    for names, got, spans, tag in ((EARLY_F32, recv["early_f32"], env["early_f32_spans"], "early_f32"),
                                   (EARLY_BF16, recv["early_bf16"], env["early_bf16_spans"], "early_bf16"),
                                   (LATE, got_late, late_spans, "late")):
        total = _sum_parts(got, "sum_small_" + tag)
        summed.update(zip(names, _unpack_rows(total, spans, [grads[n].shape for n in names])))
    loss = summed["loss"].reshape(())
    gsm = {n: summed[n].reshape(w[n].shape) for n in SMALL_REPL}
    for n in SMALL_SHARD:
        full = summed[n]
        gsm[n] = lax.dynamic_slice_in_dim(full, me * 128, 128, axis=1).reshape(w[n].shape)
    pk = lambda dct: _pack_rows([dct[n] for n in small_names])[0]
    gp, sp = _pack_rows([gsm[n] for n in small_names])
    res = _adamw(gp[None], pk(w)[None], pk(m)[None], pk(v)[None], "adamw_small")
    sshapes = [w[n].shape for n in small_names]
    for idx, t in enumerate(res):
        for n, a in zip(small_names, _unpack_rows(t[0], sp, sshapes)):
            outs.setdefault(n, [None] * 4)[idx] = a

    result = [loss, grad_x[None]]
    for idx in range(4):
        result += [outs[n][idx] for n in ORDER]
    return tuple(result)
```

```python
import functools
import math

import jax
import jax.numpy as jnp
from jax import lax
from jax.experimental import pallas as pl
from jax.experimental.pallas import tpu as pltpu

f32 = jnp.float32
bf16 = jnp.bfloat16

D = 1024
D_FF = 2816
IN_W = 5632
N_HEADS = 16
N_KV = 4
HEAD_DIM = 64
WINDOW = 128
BLK = 128
LRU_HEADS = 16
LRU_BLOCK = 64
LRU_GROUPS = 4
LRU_GW = 256
LRU_CHUNK = 128
LRU_ROWS = 1024
RGLRU_C = 8.0
EPS = 1e-6
NEG_INF = -1e30
N_DEV = 8

ADAM_LR = 0.001
ADAM_B1 = 0.9
ADAM_B2 = 0.999
ADAM_EPS = 1e-08
ADAM_WD = 0.01
ADAM_STEP = 10

VMEM_MB = 56

C_U, C_G, C_Q, C_Z0, C_Z1, C_K, C_V = 0, 1024, 2048, 3072, 4096, 5120, 5376


def _cparams(vmem_mb=VMEM_MB):
    return pltpu.CompilerParams(vmem_limit_bytes=vmem_mb << 20)


def _div_tile(n, pref):
    if n <= pref:
        return n
    return max(t for t in range(8, pref + 1, 8) if n % t == 0)


def _perm_cols(w):
    return jnp.concatenate([w[:, :3072], w[:, 3584:5632], w[:, 3072:3584]], axis=1)


def _sigmoid(x):
    return 0.5 * jnp.tanh(0.5 * x) + 0.5


def _log1p(x):
    u = 1.0 + x
    d = u - 1.0
    return jnp.where(d == 0.0, x, jnp.log(u) * (x / jnp.where(d == 0.0, 1.0, d)))


def _softplus(x):
    return jnp.maximum(x, 0.0) + _log1p(jnp.exp(-jnp.abs(x)))


def _gelu_and_grad(x):
    c = math.sqrt(2.0 / math.pi)
    inner = c * (x + 0.044715 * (x * x * x))
    t = jnp.tanh(inner)
    gelu = 0.5 * x * (1.0 + t)
    dinner = c * (1.0 + 3 * 0.044715 * (x * x))
    dgelu = 0.5 * (1.0 + t) + 0.5 * x * (1.0 - t * t) * dinner
    return gelu, dgelu


def _rms_bwd(dn, xv, g):
    r = lax.rsqrt(jnp.mean(xv * xv, axis=-1, keepdims=True) + EPS)
    xh = xv * r
    dxh = dn * g
    dx = r * (dxh - xh * jnp.mean(dxh * xh, axis=-1, keepdims=True))
    return dx, dn * xh


ANY_SPEC = pl.BlockSpec(memory_space=pl.ANY)


def _comm_out_shape(src, scatter):
    return jax.ShapeDtypeStruct((N_DEV, *(src.shape[1:] if scatter else src.shape)), src.dtype)


def _comm_sems():
    return [pltpu.SemaphoreType.DMA((N_DEV - 1,)), pltpu.SemaphoreType.DMA((N_DEV - 1,)), pltpu.SemaphoreType.DMA]


def _scatter_descs(src_ref, out_ref, send_sems, recv_sems, local_sem):
    x, y, c = lax.axis_index("x"), lax.axis_index("y"), lax.axis_index("c")
    me = 4 * x + 2 * y + c
    descs = [pltpu.make_async_copy(src_ref.at[me], out_ref.at[me], local_sem)]
    for k in range(1, N_DEV):
        px, py, pc = x ^ (k >> 2), y ^ ((k >> 1) & 1), c ^ (k & 1)
        descs.append(pltpu.make_async_remote_copy(
            src_ref=src_ref.at[4 * px + 2 * py + pc], dst_ref=out_ref.at[me],
            send_sem=send_sems.at[k - 1], recv_sem=recv_sems.at[k - 1],
            device_id=(px, py, pc), device_id_type=pl.DeviceIdType.MESH))
    return descs


def _gather_copies(src_ref, out_ref, send_sems, recv_sems, local_sem, starting):
    x, y, c = lax.axis_index("x"), lax.axis_index("y"), lax.axis_index("c")
    me, sibling = (x, y, c), (x, y, 1 - c)
    chips = [(1 - x, y), (x, 1 - y), (1 - x, 1 - y)]

    def slot(px, py, pc):
        return out_ref.at[4 * px + 2 * py + pc]

    def copy(k, block, to, src=None):
        return pltpu.make_async_remote_copy(
            src_ref=slot(*block) if src is None else src, dst_ref=slot(*block),
            send_sem=send_sems.at[k], recv_sem=recv_sems.at[k], device_id=to, device_id_type=pl.DeviceIdType.MESH)

    local = pltpu.make_async_copy(src_ref, slot(*me), local_sem)
    first = [copy(0, me, sibling, src=src_ref)] + [copy(1 + j, me, (*chip, c), src=src_ref)
                                                    for j, chip in enumerate(chips)]
    if starting:
        return local, first
    passed = [copy(4 + j, (*chip, c), sibling) for j, chip in enumerate(chips)]
    landed = [copy(1 + j, (*chip, c), me) for j, chip in enumerate(chips)]
    later = [copy(0, sibling, me)] + [copy(4 + j, (*chip, 1 - c), me) for j, chip in enumerate(chips)]
    return local, first, passed, landed, later


def _comm_start(src_ref, out_ref, sems, scatter):
    if scatter:
        for d in _scatter_descs(src_ref, out_ref, *sems):
            d.start()
    else:
        local, first = _gather_copies(src_ref, out_ref, *sems, starting=True)
        local.start()
        for cp in first:
            cp.start()


def _comm_finish(src_ref, out_ref, sems, scatter):
    if scatter:
        for d in _scatter_descs(src_ref, out_ref, *sems):
            d.wait()
    else:
        local, first, passed, landed, later = _gather_copies(src_ref, out_ref, *sems, starting=False)
        for arrived, onward in zip(landed, passed):
            arrived.wait_recv()
            onward.start()
        for cp in later:
            cp.wait_recv()
        for cp in first + passed:
            cp.wait_send()
        local.wait()


def _exchange(comm, name):
    nc = len(comm)

    def body(*refs):
        srcs, outs, sems = refs[:nc], refs[nc:2 * nc], refs[2 * nc:]
        for i in range(nc):
            _comm_start(srcs[i], outs[i], sems[3 * i:3 * i + 3], comm[i][1])
        for i in range(nc):
            _comm_finish(srcs[i], outs[i], sems[3 * i:3 * i + 3], comm[i][1])

    return pl.pallas_call(
        body, name=name, in_specs=[ANY_SPEC] * nc, out_specs=[ANY_SPEC] * nc,
        out_shape=[_comm_out_shape(*c) for c in comm],
        scratch_shapes=[s for _ in comm for s in _comm_sems()],
    )(*[c[0] for c in comm])


def _hosted_call(body, *, name, grid, in_specs, out_specs, out_shape, args, scratch_shapes=(), comm=()):
    nin, nout, nscr, nc = len(in_specs), len(out_specs), len(scratch_shapes), len(comm)

    def wrapped(*refs):
        ins = refs[:nin]
        csrc = refs[nin:nin + nc]
        outs = refs[nin + nc:nin + nc + nout]
        cout = refs[nin + nc + nout:nin + 2 * nc + nout]
        scr = refs[nin + 2 * nc + nout:]
        sems = scr[nscr:]

        if nc:
            first = functools.reduce(jnp.logical_and, [pl.program_id(a) == 0 for a in range(len(grid))])

            @pl.when(first)
            def _():
                for i in range(nc):
                    _comm_start(csrc[i], cout[i], sems[3 * i:3 * i + 3], comm[i][1])

        body(*ins, *outs, *scr[:nscr])

        if nc:
            last = functools.reduce(jnp.logical_and, [pl.program_id(a) == grid[a] - 1 for a in range(len(grid))])

            @pl.when(last)
            def _():
                for i in range(nc):
                    _comm_finish(csrc[i], cout[i], sems[3 * i:3 * i + 3], comm[i][1])

    res = pl.pallas_call(
        wrapped, name=name, grid=grid,
        in_specs=[*in_specs, *[ANY_SPEC] * nc], out_specs=[*out_specs, *[ANY_SPEC] * nc],
        out_shape=[*out_shape, *[_comm_out_shape(*c) for c in comm]],
        scratch_shapes=[*scratch_shapes, *[s for _ in comm for s in _comm_sems()]],
        compiler_params=_cparams())(*args, *[c[0] for c in comm])
    return res[:nout], res[nout:]


def _norm_matmul(x, g, w, name, tm=1024, tn=1408, comm=()):
    S, dm = x.shape
    n = w.shape[1]
    tm = min(tm, S)

    def body(x_ref, g_ref, w_ref, xn_ref, o_ref):
        @pl.when(pl.program_id(1) == 0)
        def _():
            xv = x_ref[...]
            r = lax.rsqrt(jnp.mean(xv * xv, axis=-1, keepdims=True) + EPS)
            xn_ref[...] = ((xv * r) * g_ref[...]).astype(bf16)

        o_ref[...] = jnp.dot(xn_ref[...], w_ref[...], preferred_element_type=f32).astype(bf16)

    return _hosted_call(
        body, name=name, grid=(S // tm, n // tn),
        in_specs=[pl.BlockSpec((tm, dm), lambda i, j: (i, 0)),
                  pl.BlockSpec((1, dm), lambda i, j: (0, 0)),
                  pl.BlockSpec((dm, tn), lambda i, j: (0, j))],
        out_specs=[pl.BlockSpec((tm, dm), lambda i, j: (i, 0)),
                   pl.BlockSpec((tm, tn), lambda i, j: (i, j))],
        out_shape=[jax.ShapeDtypeStruct((S, dm), bf16), jax.ShapeDtypeStruct((S, n), bf16)],
        args=(x, g, w), comm=comm)


def _mm_tn(a, b, name, tk, tn, tmc=2048, into=None, col=0, out_cols=None):
    m, ka = a.shape
    n = b.shape[1]
    tmc = min(tmc, m)
    nk = m // tmc

    def body(a_ref, b_ref, *rest):
        o_ref, acc_ref = rest[-2:]
        k = pl.program_id(2)
        part = lax.dot_general(a_ref[...], b_ref[...], (((0,), (0,)), ((), ())), preferred_element_type=f32)

        @pl.when(k == 0)
        def _():
            acc_ref[...] = part

        @pl.when(k > 0)
        def _():
            acc_ref[...] += part

        @pl.when(k == nk - 1)
        def _():
            o_ref[...] = acc_ref[...].astype(bf16)

    in_specs = [pl.BlockSpec((tmc, tk), lambda i, j, k: (k, i)), pl.BlockSpec((tmc, tn), lambda i, j, k: (k, j))]
    if into is None:
        return pl.pallas_call(
            body, name=name, grid=(ka // tk, n // tn, nk), in_specs=in_specs,
            out_specs=pl.BlockSpec((tk, tn), lambda i, j, k: (i, j + col)),
            out_shape=jax.ShapeDtypeStruct((ka, out_cols or n), bf16),
            scratch_shapes=[pltpu.VMEM((tk, tn), f32)],
            compiler_params=_cparams())(a, b)
    return pl.pallas_call(
        body, name=name, grid=(ka // tk, n // tn, nk), in_specs=[*in_specs, ANY_SPEC],
        out_specs=pl.BlockSpec((tk, tn), lambda i, j, k: (i, j + col)),
        out_shape=jax.ShapeDtypeStruct(into.shape, into.dtype),
        scratch_shapes=[pltpu.VMEM((tk, tn), f32)], input_output_aliases={2: 0},
        compiler_params=_cparams())(a, b, into)


HALO = 16


def _rows_at(ext, o, tc):
    if o == 0:
        return ext[HALO:HALO + tc]
    return pltpu.roll(ext, (-o) % ext.shape[0], 0)[HALO:HALO + tc]


def _halo_specs(tc, S, width, col):
    per = tc // HALO
    last = S // HALO - 1
    return (pl.BlockSpec((tc, width), lambda i: (i, col)),
            pl.BlockSpec((HALO, width), lambda i: (jnp.maximum(i * per - 1, 0), col)),
            pl.BlockSpec((HALO, width), lambda i: (jnp.minimum((i + 1) * per, last), col)))


def _extended(cur_ref, prev_ref, next_ref, i, nsteps):
    prev = jnp.where(i > 0, prev_ref[...].astype(f32), 0.0)
    nxt = jnp.where(i < nsteps - 1, next_ref[...].astype(f32), 0.0)
    return jnp.concatenate([prev, cur_ref[...].astype(f32), nxt], axis=0)


def _conv_fwd(proj, cw, cb, tc=1024):
    S = proj.shape[0]
    tc = min(tc, S)
    nsteps = S // tc

    def body(cur_ref, prev_ref, next_ref, w_ref, b_ref, o_ref):
        ext = _extended(cur_ref, prev_ref, next_ref, pl.program_id(0), nsteps)
        acc = _rows_at(ext, -2, tc) * w_ref[0:1, :]
        for k in range(1, 4):
            acc = acc + _rows_at(ext, k - 2, tc) * w_ref[k:k + 1, :]
        o_ref[...] = acc + b_ref[...]

    return pl.pallas_call(
        body, name="conv_fwd", grid=(nsteps,),
        in_specs=[*_halo_specs(tc, S, D, 0),
                  pl.BlockSpec((4, D), lambda i: (0, 0)), pl.BlockSpec((1, D), lambda i: (0, 0))],
        out_specs=pl.BlockSpec((tc, D), lambda i: (i, 0)),
        out_shape=jax.ShapeDtypeStruct((S, D), f32),
        compiler_params=_cparams())(proj, proj, proj, cw, cb)


def _conv_bwd(duc_f, duc_b, proj, cw, tc=1024, comm=()):
    S = proj.shape[0]
    tc = min(tc, S)
    nsteps = S // tc

    def body(fc, fp, fn, bc, bp, bn, uc_, up, un, w_ref, du_ref, dw_ref, db_ref):
        i = pl.program_id(0)

        @pl.when(i == 0)
        def _():
            dw_ref[...] = jnp.zeros_like(dw_ref)
            db_ref[...] = jnp.zeros_like(db_ref)

        dext = _extended(fc, fp, fn, i, nsteps) + _extended(bc, bp, bn, i, nsteps)
        uext = _extended(uc_, up, un, i, nsteps)
        d = dext[HALO:HALO + tc]
        acc = _rows_at(dext, 2, tc) * w_ref[0:1, :]
        for k in range(1, 4):
            acc = acc + _rows_at(dext, 2 - k, tc) * w_ref[k:k + 1, :]
        du_ref[...] = acc.astype(bf16)
        wrow = lax.broadcasted_iota(jnp.int32, (4, D), 0)
        for k in range(4):
            dw_ref[...] += jnp.where(wrow == k, jnp.sum(d * _rows_at(uext, k - 2, tc), axis=0, keepdims=True), 0.0)
        db_ref[...] += jnp.sum(d, axis=0, keepdims=True)

    return _hosted_call(
        body, name="conv_bwd", grid=(nsteps,),
        in_specs=[*_halo_specs(tc, S, D, 0), *_halo_specs(tc, S, D, 0), *_halo_specs(tc, S, D, 0),
                  pl.BlockSpec((4, D), lambda i: (0, 0))],
        out_specs=[pl.BlockSpec((tc, D), lambda i: (i, 0)),
                   pl.BlockSpec((4, D), lambda i: (0, 0)), pl.BlockSpec((1, D), lambda i: (0, 0))],
        out_shape=[jax.ShapeDtypeStruct((S, D), bf16), jax.ShapeDtypeStruct((4, D), f32),
                   jax.ShapeDtypeStruct((1, D), f32)],
        args=(duc_f, duc_f, duc_f, duc_b, duc_b, duc_b, proj, proj, proj, cw), comm=comm)


def _scan_scratch():
    halves = [pltpu.VMEM((LRU_CHUNK, 128), f32) for _ in range(2 * (LRU_GW // 128))]
    return [*halves, pltpu.VMEM((LRU_CHUNK // 8, LRU_GW), f32), pltpu.VMEM((LRU_CHUNK // 8, LRU_GW), f32)]


def _log_scan(a, b, row, n, reverse, steps):
    for s in steps:
        shift = a.shape[0] - s if reverse else s
        keep = (row < n - s) if reverse else (row >= s)
        a_sh = pltpu.roll(a, shift, 0)
        b_sh = pltpu.roll(b, shift, 0)
        b = jnp.where(keep, a * b_sh + b, b)
        a = jnp.where(keep, a * a_sh, a)
    return a, b


def _scan_chunk(a, b, carry, reverse, *scratch):
    tc, w = a.shape
    ng = tc // 8
    nl = w // 128
    sa_refs, sb_refs, sc_ref, st_ref = scratch[:nl], scratch[nl:2 * nl], scratch[2 * nl], scratch[2 * nl + 1]
    sub = lax.broadcasted_iota(jnp.int32, (8, w), 0)
    ag, bg = [], []
    for k in range(ng):
        ak, bk = _log_scan(a[8 * k:8 * k + 8], b[8 * k:8 * k + 8], sub, 8, reverse, (1, 2, 4))
        ag.append(ak)
        bg.append(bk)
    a = jnp.concatenate(ag, axis=0)
    b = jnp.concatenate(bg, axis=0)
    edge = 0 if reverse else 7
    for i in range(nl):
        sa_refs[i][...] = a[:, 128 * i:128 * (i + 1)]
        sb_refs[i][...] = b[:, 128 * i:128 * (i + 1)]
    ta = jnp.concatenate([r[pl.ds(edge, ng, stride=8), :] for r in sa_refs], axis=1)
    tb = jnp.concatenate([r[pl.ds(edge, ng, stride=8), :] for r in sb_refs], axis=1)
    grow = lax.broadcasted_iota(jnp.int32, (ng, w), 0)
    ta, tb = _log_scan(ta, tb, grow, ng, reverse, [1 << i for i in range(ng.bit_length() - 1)])
    state = tb + ta * carry
    st_ref[...] = state
    if reverse:
        sc_ref[...] = jnp.where(grow == ng - 1, carry, pltpu.roll(state, ng - 1, 0))
    else:
        sc_ref[...] = jnp.where(grow == 0, carry, pltpu.roll(state, 1, 0))
    h = jnp.concatenate([bg[k] + ag[k] * sc_ref[k:k + 1, :] for k in range(ng)], axis=0)
    return h, (st_ref[0:1, :] if reverse else st_ref[ng - 1:ng, :])


def _lru_gates(uc, w, p_ref):
    pre = jnp.dot(uc.astype(bf16), w, preferred_element_type=f32)
    r = _sigmoid(pre[:, :LRU_GW] + p_ref[0, 1:2, :])
    gi = _sigmoid(pre[:, LRU_GW:] + p_ref[0, 2:3, :])
    sp = _softplus(-p_ref[0, 0:1, :])
    log_a = -RGLRU_C * r * sp
    a = jnp.exp(log_a)
    x = 2.0 * log_a
    series = -x * (1.0 + x * (0.5 + x * (1.0 / 6 + x * (1.0 / 24))))
    beta = jnp.sqrt(jnp.maximum(jnp.where(x > -0.0625, series, 1.0 - a * a), 0.0))
    return r, gi, sp, a, beta


def _lru_fwd(uc, wg, lp, reverse, comm=()):
    S = uc.shape[0]
    tc = LRU_CHUNK
    rows = min(LRU_ROWS, S)
    nsub = rows // tc
    nblk = S // rows
    d = 1 if reverse else 0

    def bidx(c):
        return nblk - 1 - c if reverse else c

    def body(uc_ref, w_ref, p_ref, h_ref, carry_ref, *scan_scratch):
        @pl.when(pl.program_id(1) == 0)
        def _():
            carry_ref[...] = jnp.zeros_like(carry_ref)

        carry = carry_ref[...]
        for j in (reversed(range(nsub)) if reverse else range(nsub)):
            sl = slice(j * tc, (j + 1) * tc)
            ucv = uc_ref[sl, :]
            _, gi, _, a, beta = _lru_gates(ucv, w_ref[0], p_ref)
            h, carry = _scan_chunk(a, beta * (gi * ucv), carry, reverse, *scan_scratch)
            h_ref[sl, :] = h.astype(bf16)
        carry_ref[...] = carry

    return _hosted_call(
        body, name="lru_fwd_rev" if reverse else "lru_fwd", grid=(LRU_GROUPS, nblk),
        in_specs=[pl.BlockSpec((rows, LRU_GW), lambda g, c: (bidx(c), g)),
                  pl.BlockSpec((1, LRU_GW, 2 * LRU_GW), lambda g, c: (g, 0, d)),
                  pl.BlockSpec((1, 8, LRU_GW), lambda g, c: (d, 0, g))],
        out_specs=[pl.BlockSpec((rows, LRU_GW), lambda g, c: (bidx(c), g))],
        out_shape=[jax.ShapeDtypeStruct((S, D), bf16)],
        scratch_shapes=[pltpu.VMEM((1, LRU_GW), f32), *_scan_scratch()],
        args=(uc, wg, lp), comm=comm)


def _lru_bwd(uc, dh, h, wg, lp, reverse, comm=()):
    S = uc.shape[0]
    tc = LRU_CHUNK
    rows = min(LRU_ROWS, S)
    nsub = rows // tc
    nblk = S // rows
    d = 1 if reverse else 0
    per = rows // HALO
    last8 = S // HALO - 1

    def bidx(c):
        return c if reverse else nblk - 1 - c

    def halo_idx(c):
        if reverse:
            return jnp.minimum((bidx(c) + 1) * per, last8)
        return jnp.maximum(bidx(c) * per - 1, 0)

    def body(uc_ref, dh_ref, h_ref, halo_ref, w_ref, p_ref, duc_ref, dw_ref, dp_ref, carry_ref, tmp_ref,
             *scan_scratch):
        c = pl.program_id(1)
        bi = bidx(c)

        @pl.when(c == 0)
        def _():
            carry_ref[...] = jnp.zeros_like(carry_ref)
            dw_ref[...] = jnp.zeros_like(dw_ref)
            dp_ref[...] = jnp.zeros_like(dp_ref)

        row = lax.broadcasted_iota(jnp.int32, (tc, LRU_GW), 0)
        carry = carry_ref[...]
        dw = jnp.zeros((LRU_GW, 2 * LRU_GW), f32)
        dsp = jnp.zeros((1, LRU_GW), f32)
        dba = jnp.zeros((1, LRU_GW), f32)
        dbx = jnp.zeros((1, LRU_GW), f32)
        for j in (range(nsub) if reverse else reversed(range(nsub))):
            sl = slice(j * tc, (j + 1) * tc)
            ucv = uc_ref[sl, :]
            ucb = ucv.astype(bf16)
            r, gi, sp, a, beta = _lru_gates(ucv, w_ref[0], p_ref)
            hv = h_ref[sl, :].astype(f32)
            dhv = dh_ref[sl, :].astype(f32)
            if reverse:
                alpha = jnp.where(row == 0, 1.0, pltpu.roll(a, 1, 0))
                gsc, _ = _scan_chunk(alpha, dhv, carry, False, *scan_scratch)
                if j < nsub - 1:
                    edge = h_ref[(j + 1) * tc:(j + 1) * tc + HALO, :].astype(f32)[0:1, :]
                else:
                    edge = jnp.where(bi < nblk - 1, halo_ref[...].astype(f32)[0:1, :], 0.0)
                h_nb = jnp.where(row == tc - 1, edge, pltpu.roll(hv, tc - 1, 0))
            else:
                alpha = jnp.where(row == tc - 1, 1.0, pltpu.roll(a, tc - 1, 0))
                gsc, _ = _scan_chunk(alpha, dhv, carry, True, *scan_scratch)
                if j > 0:
                    edge = h_ref[j * tc - HALO:j * tc, :].astype(f32)[HALO - 1:HALO, :]
                else:
                    edge = jnp.where(bi > 0, halo_ref[...].astype(f32)[HALO - 1:HALO, :], 0.0)
                h_nb = jnp.where(row == 0, edge, pltpu.roll(hv, 1, 0))
            tmp_ref[...] = a * gsc
            carry = tmp_ref[tc - 1:tc, :] if reverse else tmp_ref[0:1, :]

            da = gsc * h_nb
            dbeta = gsc * (gi * ucv)
            dl = da * a - dbeta * (a * a) / beta
            dr = dl * (-RGLRU_C * sp)
            dsp = dsp + jnp.sum(dl * (-RGLRU_C * r), axis=0, keepdims=True)
            dgi = gsc * beta * ucv
            dpre_r = dr * r * (1.0 - r)
            dpre_i = dgi * gi * (1.0 - gi)
            dba = dba + jnp.sum(dpre_r, axis=0, keepdims=True)
            dbx = dbx + jnp.sum(dpre_i, axis=0, keepdims=True)
            dpre = jnp.concatenate([dpre_r, dpre_i], axis=1).astype(bf16)
            back = lax.dot_general(dpre, w_ref[0], (((1,), (1,)), ((), ())), preferred_element_type=f32)
            duc_ref[sl, :] = (gsc * beta * gi + back).astype(bf16)
            dw = dw + lax.dot_general(ucb, dpre, (((0,), (0,)), ((), ())), preferred_element_type=f32)
        carry_ref[...] = carry
        dw_ref[0] += dw
        dlam = -dsp / (1.0 + jnp.exp(p_ref[0, 0:1, :]))
        prow = lax.broadcasted_iota(jnp.int32, (8, LRU_GW), 0)
        dp_ref[...] += (jnp.where(prow == 0, dlam, 0.0) + jnp.where(prow == 1, dba, 0.0)
                        + jnp.where(prow == 2, dbx, 0.0))

    chunk = pl.BlockSpec((rows, LRU_GW), lambda g, c: (bidx(c), g))
    return _hosted_call(
        body, name="lru_bwd_rev" if reverse else "lru_bwd", grid=(LRU_GROUPS, nblk),
        in_specs=[chunk, chunk, chunk,
                  pl.BlockSpec((HALO, LRU_GW), lambda g, c: (halo_idx(c), g)),
                  pl.BlockSpec((1, LRU_GW, 2 * LRU_GW), lambda g, c: (g, 0, d)),
                  pl.BlockSpec((1, 8, LRU_GW), lambda g, c: (d, 0, g))],
        out_specs=[chunk,
                   pl.BlockSpec((1, LRU_GW, 2 * LRU_GW), lambda g, c: (g, 0, 0)),
                   pl.BlockSpec((8, LRU_GW), lambda g, c: (0, g))],
        out_shape=[jax.ShapeDtypeStruct((S, D), bf16),
                   jax.ShapeDtypeStruct((LRU_GROUPS, LRU_GW, 2 * LRU_GW), f32),
                   jax.ShapeDtypeStruct((8, D), f32)],
        scratch_shapes=[pltpu.VMEM((1, LRU_GW), f32), pltpu.VMEM((tc, LRU_GW), f32), *_scan_scratch()],
        args=(uc, dh, h, h, wg, lp), comm=comm)


def _slope(h):
    return 2.0 ** (-8.0 * (h + 1.0) / N_HEADS)


def _kv_specs(nb, col):
    return [pl.BlockSpec((BLK, N_KV * HEAD_DIM), lambda n: (jnp.maximum(n - 1, 0), col)),
            pl.BlockSpec((BLK, N_KV * HEAD_DIM), lambda n: (n, col)),
            pl.BlockSpec((BLK, N_KV * HEAD_DIM), lambda n: (jnp.minimum(n + 1, nb - 1), col))]


def _dup_windows(r0, r1, r2):
    left = lax.broadcasted_iota(jnp.int32, (3 * BLK, 128), 1) < HEAD_DIM
    win = jnp.concatenate([r0[...], r1[...], r2[...]], axis=0)
    out = []
    for i in range(N_KV // 2):
        t = win[:, i * 128:(i + 1) * 128]
        r = pltpu.roll(t, HEAD_DIM, 1)
        out += [jnp.where(left, t, r).astype(bf16), jnp.where(left, r, t).astype(bf16)]
    return out


def _attn_bias_init(bias_ref):
    k_loc = lax.broadcasted_iota(jnp.int32, (3 * BLK, BLK), 0)
    q_loc = lax.broadcasted_iota(jnp.int32, (3 * BLK, BLK), 1)
    adist = jnp.abs(q_loc + BLK - k_loc)
    adf = adist.astype(f32)
    for e in range(3):
        ok = adist <= WINDOW
        if e == 0:
            ok = ok & (k_loc >= BLK)
        if e == 2:
            ok = ok & (k_loc < 2 * BLK)
        for kv in range(N_KV):
            bias_ref[e, kv] = jnp.concatenate(
                [jnp.where(ok, (-_slope(4 * kv + j)) * adf, NEG_INF) for j in range(4)], axis=1)


def _stack_heads(ref, kv, scale):
    left = lax.broadcasted_iota(jnp.int32, (BLK, 128), 1) < HEAD_DIM
    rows = []
    for pp in range(2):
        t = ref[:, (2 * kv + pp) * 128:(2 * kv + pp + 1) * 128]
        if scale != 1.0:
            t = t * scale
        zero = jnp.zeros_like(t)
        rows += [jnp.where(left, t, zero).astype(bf16), jnp.where(left, zero, t).astype(bf16)]
    return jnp.concatenate(rows, axis=0)


def _attn_softmax(qs, k2, bias, sink_ref, kv, stats=None):
    sink = jnp.concatenate([jnp.full((1, BLK), sink_ref[0, 4 * kv + j], f32) for j in range(4)], axis=1)
    s = lax.dot_general(k2, qs, (((1,), (1,)), ((), ())), preferred_element_type=f32) + bias
    m = jnp.maximum(jnp.max(s, axis=0, keepdims=True), sink) if stats is None else stats[0]
    p = jnp.exp(s - m)
    ps = jnp.exp(sink - m)
    inv = 1.0 / (jnp.sum(p, axis=0, keepdims=True) + ps) if stats is None else stats[1]
    return p, ps, m, inv


def _pair_tiles(t):
    return [jnp.concatenate([t[:HEAD_DIM, 256 * pp:256 * pp + 128],
                             t[HEAD_DIM:, 256 * pp + 128:256 * pp + 256]], axis=0).T for pp in range(2)]


def _attn_fwd(proj, sink, comm=()):
    S = proj.shape[0]
    nb = S // BLK
    assert nb >= 2

    def body(q_ref, k0, k1, k2_, v0, v1, v2_, sink_ref, o_ref, st_ref, bias_ref):
        n = pl.program_id(0)

        @pl.when(n == 0)
        def _():
            _attn_bias_init(bias_ref)

        e = jnp.where(n == 0, 0, jnp.where(n == nb - 1, 2, 1))
        kk = _dup_windows(k0, k1, k2_)
        vv = _dup_windows(v0, v1, v2_)
        tiles = []
        for kv in range(N_KV):
            qs = _stack_heads(q_ref, kv, HEAD_DIM ** -0.5)
            p, _, m, inv = _attn_softmax(qs, kk[kv], bias_ref[e, kv], sink_ref, kv)
            st_ref[0, kv:kv + 1, :] = m
            st_ref[0, N_KV + kv:N_KV + kv + 1, :] = inv
            ot = lax.dot_general(vv[kv], p.astype(bf16), (((0,), (0,)), ((), ())), preferred_element_type=f32)
            tiles += _pair_tiles(ot * inv)
        o_ref[...] = jnp.concatenate(tiles, axis=1).astype(bf16)

    return _hosted_call(
        body, name="attn_fwd", grid=(nb,),
        in_specs=[pl.BlockSpec((BLK, D), lambda n: (n, C_Q // D)),
                  *_kv_specs(nb, C_K // (N_KV * HEAD_DIM)), *_kv_specs(nb, C_V // (N_KV * HEAD_DIM)),
                  pl.BlockSpec(memory_space=pltpu.SMEM)],
        out_specs=[pl.BlockSpec((BLK, D), lambda n: (n, 0)), pl.BlockSpec((1, 2 * N_KV, 4 * BLK), lambda n: (n, 0, 0))],
        out_shape=[jax.ShapeDtypeStruct((S, D), bf16), jax.ShapeDtypeStruct((nb, 2 * N_KV, 4 * BLK), f32)],
        scratch_shapes=[pltpu.VMEM((3, N_KV, 3 * BLK, 4 * BLK), f32)],
        args=(proj, proj, proj, proj, proj, proj, proj, sink), comm=comm)


def _attn_bwd(proj, sink, dyb, stats, comm=()):
    S = proj.shape[0]
    nb = S // BLK
    assert nb >= 2

    def body(q_ref, k0, k1, k2_, v0, v1, v2_, sink_ref, do_ref, st_ref, dq_ref, dk_out, dv_out, ds_ref,
             bias_ref, dk_ref, dv_ref, dsk_ref):
        n = pl.program_id(0)

        @pl.when(n == 0)
        def _():
            _attn_bias_init(bias_ref)
            dk_ref[...] = jnp.zeros_like(dk_ref)
            dv_ref[...] = jnp.zeros_like(dv_ref)
            dsk_ref[...] = jnp.zeros_like(dsk_ref)

        e = jnp.where(n == 0, 0, jnp.where(n == nb - 1, 2, 1))
        kk = _dup_windows(k0, k1, k2_)
        vv = _dup_windows(v0, v1, v2_)
        left3 = lax.broadcasted_iota(jnp.int32, (3 * BLK, 128), 1) < HEAD_DIM
        start = pl.multiple_of(n * BLK, BLK)
        dq_tiles, dks, dvs = [], [], []
        for kv in range(N_KV):
            qs = _stack_heads(q_ref, kv, HEAD_DIM ** -0.5)
            dos = _stack_heads(do_ref, kv, 1.0)
            stats = (st_ref[0, kv:kv + 1, :], st_ref[0, N_KV + kv:N_KV + kv + 1, :])
            p, ps, _, inv = _attn_softmax(qs, kk[kv], bias_ref[e, kv], sink_ref, kv, stats)
            pn = p * inv
            dp = lax.dot_general(vv[kv], dos, (((1,), (1,)), ((), ())), preferred_element_type=f32)
            delta = jnp.sum(pn * dp, axis=0, keepdims=True)
            dsc = (pn * (dp - delta)).astype(bf16)
            dsk_ref[kv:kv + 1, :] += delta * (ps * inv)
            dqt = lax.dot_general(kk[kv], dsc, (((0,), (0,)), ((), ())), preferred_element_type=f32)
            dq_tiles += _pair_tiles(dqt * (HEAD_DIM ** -0.5))
            dk = jnp.dot(dsc, qs, preferred_element_type=f32)
            dv = jnp.dot(pn.astype(bf16), dos, preferred_element_type=f32)
            dks.append(dk + pltpu.roll(dk, HEAD_DIM, 1))
            dvs.append(dv + pltpu.roll(dv, HEAD_DIM, 1))
        for jp in range(N_KV // 2):
            cols = slice(jp * 128, (jp + 1) * 128)
            dk_ref[pl.ds(start, 3 * BLK), cols] += jnp.where(left3, dks[2 * jp], dks[2 * jp + 1])
            dv_ref[pl.ds(start, 3 * BLK), cols] += jnp.where(left3, dvs[2 * jp], dvs[2 * jp + 1])
        dq_ref[...] = jnp.concatenate(dq_tiles, axis=1).astype(bf16)

        @pl.when(n == nb - 1)
        def _():
            pltpu.sync_copy(dk_ref, dk_out)
            pltpu.sync_copy(dv_ref, dv_out)
            lane = lax.broadcasted_iota(jnp.int32, (1, 128), 1)
            dsink = jnp.zeros((1, 128), f32)
            for h in range(N_HEADS):
                part = dsk_ref[h // 4:h // 4 + 1, (h % 4) * BLK:(h % 4 + 1) * BLK]
                dsink = dsink + jnp.where(lane == h, -jnp.sum(part), 0.0)
            ds_ref[...] = dsink

    acc = jax.ShapeDtypeStruct((S + 2 * BLK, N_KV * HEAD_DIM), f32)
    return _hosted_call(
        body, name="attn_bwd", grid=(nb,),
        in_specs=[pl.BlockSpec((BLK, D), lambda n: (n, C_Q // D)),
                  *_kv_specs(nb, C_K // (N_KV * HEAD_DIM)), *_kv_specs(nb, C_V // (N_KV * HEAD_DIM)),
                  pl.BlockSpec(memory_space=pltpu.SMEM),
                  pl.BlockSpec((BLK, D), lambda n: (n, 0)),
                  pl.BlockSpec((1, 2 * N_KV, 4 * BLK), lambda n: (n, 0, 0))],
        out_specs=[pl.BlockSpec((BLK, D), lambda n: (n, 0)), ANY_SPEC, ANY_SPEC,
                   pl.BlockSpec((1, 128), lambda n: (0, 0))],
        out_shape=[jax.ShapeDtypeStruct((S, D), bf16), acc, acc, jax.ShapeDtypeStruct((1, 128), f32)],
        scratch_shapes=[pltpu.VMEM((3, N_KV, 3 * BLK, 4 * BLK), f32), pltpu.VMEM(acc.shape, f32),
                        pltpu.VMEM(acc.shape, f32), pltpu.VMEM((8, 4 * BLK), f32)],
        args=(proj, proj, proj, proj, proj, proj, proj, sink, dyb, stats), comm=comm)


def _merge_parts(hf, hb, g, z0, z1, yb, bg):
    g0 = _sigmoid(z0.astype(f32) + bg[:, :D])
    g1 = _sigmoid(z1.astype(f32) + bg[:, D:])
    gelu, dgelu = _gelu_and_grad(g.astype(f32))
    hs = hf.astype(f32) + hb.astype(f32)
    ya = hs * gelu
    return g0, g1, gelu, dgelu, hs, ya


def _merge_outproj(x, hf, hb, proj, yb, bg, w_out, tm=1024):
    S = x.shape[0]
    tm = min(tm, S)

    def body(x_ref, hf_ref, hb_ref, g_ref, z0_ref, z1_ref, yb_ref, bg_ref, w_ref, mg_ref, x1_ref):
        ybv = yb_ref[...].astype(f32)
        g0, g1, _, _, _, ya = _merge_parts(hf_ref[...], hb_ref[...], g_ref[...], z0_ref[...], z1_ref[...],
                                           ybv, bg_ref[...])
        mg = (g0 * ya + g1 * ybv).astype(bf16)
        mg_ref[...] = mg
        x1_ref[...] = x_ref[...] + jnp.dot(mg, w_ref[...], preferred_element_type=f32)

    row = pl.BlockSpec((tm, D), lambda i: (i, 0))
    return pl.pallas_call(
        body, name="merge_outproj", grid=(S // tm,),
        in_specs=[row, row, row,
                  pl.BlockSpec((tm, D), lambda i: (i, C_G // D)),
                  pl.BlockSpec((tm, D), lambda i: (i, C_Z0 // D)),
                  pl.BlockSpec((tm, D), lambda i: (i, C_Z1 // D)),
                  row, pl.BlockSpec((1, 2 * D), lambda i: (0, 0)), pl.BlockSpec((D, D), lambda i: (0, 0))],
        out_specs=[row, row],
        out_shape=[jax.ShapeDtypeStruct((S, D), bf16), jax.ShapeDtypeStruct((S, D), f32)],
        compiler_params=_cparams())(x, hf, hb, proj, proj, proj, yb, bg, w_out)


def _ffn_out_loss(gu, x1, w_fo, g3, tgt, tm=256):
    S = x1.shape[0]
    tm = min(tm, S)

    def body(gt_ref, up_ref, x1_ref, w_ref, g_ref, t_ref, ff_ref, dx_ref, dxb_ref, loss_ref, dg_ref,
             dgt_ref, dup_ref):
        @pl.when(pl.program_id(0) == 0)
        def _():
            loss_ref[...] = jnp.zeros_like(loss_ref)
            dg_ref[...] = jnp.zeros_like(dg_ref)

        gt = gt_ref[...].astype(f32)
        up = up_ref[...].astype(f32)
        sg = _sigmoid(gt)
        silu = gt * sg
        ff = (silu * up).astype(bf16)
        ff_ref[...] = ff
        x2 = x1_ref[...] + jnp.dot(ff, w_ref[...], preferred_element_type=f32)
        gv = g_ref[...]
        r = lax.rsqrt(jnp.mean(x2 * x2, axis=-1, keepdims=True) + EPS)
        xh = x2 * r
        diff = xh * gv - t_ref[...]
        loss_ref[...] += (0.5 / D) * jnp.sum(diff * diff)
        dy = diff * (1.0 / D)
        dg_ref[...] += jnp.sum(dy * xh, axis=0, keepdims=True)
        dxh = dy * gv
        dx = r * (dxh - xh * jnp.mean(dxh * xh, axis=-1, keepdims=True))
        dx_ref[...] = dx
        dxb = dx.astype(bf16)
        dxb_ref[...] = dxb
        dff = lax.dot_general(dxb, w_ref[...], (((1,), (1,)), ((), ())), preferred_element_type=f32)
        dup_ref[...] = (dff * silu).astype(bf16)
        dgt_ref[...] = ((dff * up) * (sg * (1.0 + gt * (1.0 - sg)))).astype(bf16)

    row = pl.BlockSpec((tm, D), lambda i: (i, 0))
    vec = pl.BlockSpec((1, D), lambda i: (0, 0))
    wide = pl.BlockSpec((tm, D_FF), lambda i: (i, 0))
    wide_shape = jax.ShapeDtypeStruct((S, D_FF), bf16)
    return pl.pallas_call(
        body, name="ffn_out_loss", grid=(S // tm,),
        in_specs=[wide, pl.BlockSpec((tm, D_FF), lambda i: (i, 1)),
                  row, pl.BlockSpec((D_FF, D), lambda i: (0, 0)), vec, row],
        out_specs=[wide, row, row, pl.BlockSpec((1, 128), lambda i: (0, 0)), vec, wide, wide],
        out_shape=[wide_shape, jax.ShapeDtypeStruct((S, D), f32), jax.ShapeDtypeStruct((S, D), bf16),
                   jax.ShapeDtypeStruct((1, 128), f32), jax.ShapeDtypeStruct((1, D), f32), wide_shape, wide_shape],
        compiler_params=_cparams())(gu, gu, x1, w_fo, g3, tgt)


def _proj_bwd(pieces, w, xres, g, dres, name, tm=512, comm=()):
    S = xres.shape[0]
    tm = min(tm, S)
    np_ = len(pieces)

    def body(*refs):
        p_refs = refs[:np_]
        w_refs = refs[np_:2 * np_]
        x_ref, g_ref, dres_ref, dx_ref, dxb_ref, dg_ref = refs[2 * np_:]

        @pl.when(pl.program_id(0) == 0)
        def _():
            dg_ref[...] = jnp.zeros_like(dg_ref)

        nt = (((1,), (1,)), ((), ()))
        dn = lax.dot_general(p_refs[0][...], w_refs[0][...], nt, preferred_element_type=f32)
        for pr, wr in zip(p_refs[1:], w_refs[1:]):
            dn = dn + lax.dot_general(pr[...], wr[...], nt, preferred_element_type=f32)
        dxn, dgc = _rms_bwd(dn, x_ref[...], g_ref[...])
        dx = dres_ref[...] + dxn
        dx_ref[...] = dx
        dxb_ref[...] = dx.astype(bf16)
        dg_ref[...] += jnp.sum(dgc, axis=0, keepdims=True)

    row = pl.BlockSpec((tm, D), lambda i: (i, 0))
    vec = pl.BlockSpec((1, D), lambda i: (0, 0))
    return _hosted_call(
        body, name=name, grid=(S // tm,),
        in_specs=[*[pl.BlockSpec((tm, wd), functools.partial(lambda i, cb: (i, cb), cb=acb))
                    for _, acb, _, wd in pieces],
                  *[pl.BlockSpec((D, wd), functools.partial(lambda i, cb: (0, cb), cb=wcb))
                    for _, _, wcb, wd in pieces],
                  row, vec, row],
        out_specs=[row, row, vec],
        out_shape=[jax.ShapeDtypeStruct((S, D), f32), jax.ShapeDtypeStruct((S, D), bf16),
                   jax.ShapeDtypeStruct((1, D), f32)],
        args=(*[p[0] for p in pieces], *[w] * np_, xres, g, dres), comm=comm)


def _outproj_bwd(dx1b, w_out, hf, hb, proj, yb, bg, tm=1024):
    S = dx1b.shape[0]
    tm = min(tm, S)

    def body(dx_ref, w_ref, hf_ref, hb_ref, g_ref, z0_ref, z1_ref, yb_ref, bg_ref,
             dh_ref, dg_ref, dz_ref, dyb_ref, dbg_ref):
        @pl.when(pl.program_id(0) == 0)
        def _():
            dbg_ref[...] = jnp.zeros_like(dbg_ref)

        dm = lax.dot_general(dx_ref[...], w_ref[...], (((1,), (1,)), ((), ())), preferred_element_type=f32)
        ybv = yb_ref[...].astype(f32)
        g0, g1, gelu, dgelu, hs, ya = _merge_parts(hf_ref[...], hb_ref[...], g_ref[...], z0_ref[...],
                                                   z1_ref[...], ybv, bg_ref[...])
        dya = dm * g0
        dh_ref[...] = (dya * gelu).astype(bf16)
        dg_ref[...] = (dya * hs * dgelu).astype(bf16)
        dyb_ref[...] = (dm * g1).astype(bf16)
        dz0 = (dm * ya) * (g0 * (1.0 - g0))
        dz1 = (dm * ybv) * (g1 * (1.0 - g1))
        dz = jnp.concatenate([dz0, dz1], axis=1)
        dz_ref[...] = dz.astype(bf16)
        dbg_ref[...] += jnp.sum(dz, axis=0, keepdims=True)

    row = pl.BlockSpec((tm, D), lambda i: (i, 0))
    return pl.pallas_call(
        body, name="outproj_bwd", grid=(S // tm,),
        in_specs=[row, pl.BlockSpec((D, D), lambda i: (0, 0)), row, row,
                  pl.BlockSpec((tm, D), lambda i: (i, C_G // D)),
                  pl.BlockSpec((tm, D), lambda i: (i, C_Z0 // D)),
                  pl.BlockSpec((tm, D), lambda i: (i, C_Z1 // D)),
                  row, pl.BlockSpec((1, 2 * D), lambda i: (0, 0))],
        out_specs=[row, row, pl.BlockSpec((tm, 2 * D), lambda i: (i, 0)), row,
                   pl.BlockSpec((1, 2 * D), lambda i: (0, 0))],
        out_shape=[jax.ShapeDtypeStruct((S, D), bf16), jax.ShapeDtypeStruct((S, D), bf16),
                   jax.ShapeDtypeStruct((S, 2 * D), bf16), jax.ShapeDtypeStruct((S, D), bf16),
                   jax.ShapeDtypeStruct((1, 2 * D), f32)],
        compiler_params=_cparams())(dx1b, w_out, hf, hb, proj, proj, proj, yb, bg)


def _block_diag_groups(w):
    w4 = w.reshape(LRU_GROUPS, 4, LRU_BLOCK, LRU_BLOCK)
    eye = jnp.eye(4, dtype=w.dtype)
    return jnp.einsum("ghij,hk->ghikj", w4, eye).reshape(LRU_GROUPS, LRU_GW, LRU_GW)


def _diag_blocks(dw):
    d5 = dw.reshape(LRU_GROUPS, 4, LRU_BLOCK, 4, LRU_BLOCK)
    return jnp.stack([d5[:, h, :, h, :] for h in range(4)], axis=1).reshape(LRU_HEADS, LRU_BLOCK, LRU_BLOCK)


def _local_step(x, tgt, small, env, before=lambda name: (), after=lambda name, got: None):
    S = x.shape[0]
    g1, g2, g3 = small["norm_mix_g"], small["norm_ffn_g"], small["norm_final_g"]
    bg, cw, cb = small["b_gate"], small["conv_w"], small["conv_b"]
    sink = small["attn_sink"]

    wg = jnp.concatenate([_block_diag_groups(small["lru_wa"][0]), _block_diag_groups(small["lru_wx"][0]),
                          _block_diag_groups(small["lru_wa"][1]), _block_diag_groups(small["lru_wx"][1])],
                         axis=2).astype(bf16)
    zeros5 = jnp.zeros((5, D), f32)
    lp = jnp.stack([jnp.concatenate([small["lru_lambda"][d:d + 1], small["lru_ba"][d:d + 1],
                                     small["lru_bx"][d:d + 1], zeros5], axis=0) for d in range(2)])

    def hosted(name, fn, *args, **kw):
        outs, got = fn(*args, comm=tuple(before(name)), **kw)
        after(name, got)
        return outs

    xn, proj = hosted("norm_inproj", _norm_matmul, x, g1, env["w_in_p"], "norm_inproj")
    uc = _conv_fwd(proj, cw, cb)
    (hf,), _ = _lru_fwd(uc, wg, lp, False)
    (hb,), _ = _lru_fwd(uc, wg, lp, True)
    yb, attn_stats = hosted("attn_fwd", _attn_fwd, proj, sink)
    merged, x1 = _merge_outproj(x, hf, hb, proj, yb, bg, env["w_out"])
    (xn2, gu), _ = _norm_matmul(x1, g2, env["w_fi"], "norm_ffn_in")
    ff, dx2, dx2b, loss, dg3, dgt, dup = _ffn_out_loss(gu, x1, env["w_fo"], g3, tgt)

    env["dw_fo"] = _mm_tn(ff, dx2b, "dw_ffn_out", tk=1408, tn=1024)
    dx1, dx1b, dg2 = hosted("ffn_in_bwd", _proj_bwd, [(dgt, 0, 0, D_FF), (dup, 0, 1, D_FF)], env["w_fi"],
                            x1, g2, dx2, "ffn_in_bwd")
    dw_gate = _mm_tn(xn2, dgt, "dw_ffn_in_gate", tk=1024, tn=1408, out_cols=2 * D_FF)
    env["dw_fi"] = _mm_tn(xn2, dup, "dw_ffn_in_up", tk=1024, tn=1408, into=dw_gate, col=D_FF // 1408)
    env["dw_out"] = _mm_tn(merged, dx1b, "dw_out", tk=1024, tn=1024)
    dh, dgl, dz, dyb, dbg = _outproj_bwd(dx1b, env["w_out"], hf, hb, proj, yb, bg)
    dq, dk2, dv2, dsink = hosted("attn_bwd", _attn_bwd, proj, sink, dyb, attn_stats)
    dkv = jnp.concatenate([dk2[BLK:BLK + S], dv2[BLK:BLK + S]], axis=1).astype(bf16)
    duc_f, dwg_f, dp_f = hosted("lru_bwd", _lru_bwd, uc, dh, hf, wg, lp, False)
    (duc_b, dwg_b, dp_b), _ = _lru_bwd(uc, dh, hb, wg, lp, True)
    env["grads_early"] = {
        "loss": loss[:, :1], "b_gate": dbg,
        "lru_lambda": jnp.concatenate([dp_f[0:1], dp_b[0:1]], axis=0),
        "lru_wa": jnp.stack([_diag_blocks(dwg_f[:, :, :LRU_GW]), _diag_blocks(dwg_b[:, :, :LRU_GW])]),
        "lru_ba": jnp.concatenate([dp_f[1:2], dp_b[1:2]], axis=0),
        "lru_wx": jnp.stack([_diag_blocks(dwg_f[:, :, LRU_GW:]), _diag_blocks(dwg_b[:, :, LRU_GW:])]),
        "lru_bx": jnp.concatenate([dp_f[2:3], dp_b[2:3]], axis=0),
        "attn_sink": dsink[:, :N_HEADS], "norm_ffn_g": dg2, "norm_final_g": dg3,
    }
    du, dcw, dcb = hosted("conv_bwd", _conv_bwd, duc_f, duc_b, proj, cw)
    dw_in = _mm_tn(xn, du, "dw_in_u", tk=1024, tn=1024, out_cols=IN_W)
    dw_in = _mm_tn(xn, dgl, "dw_in_g", tk=1024, tn=1024, into=dw_in, col=1)
    dw_in = _mm_tn(xn, dq, "dw_in_q", tk=1024, tn=1024, into=dw_in, col=2)
    dw_in = _mm_tn(xn, dkv, "dw_in_kv", tk=1024, tn=512, into=dw_in, col=3072 // 512)
    env["dw_in"] = _mm_tn(xn, dz, "dw_in_z", tk=1024, tn=512, into=dw_in, col=3584 // 512)
    col_pieces = [(du, 0, C_U // D, D), (dgl, 0, C_G // D, D), (dq, 0, C_Q // D, D), (dz, 0, C_Z0 // D, D),
                  (dz, 1, C_Z1 // D, D), (dkv, 0, C_K // 512, 512)]
    dx, _, dg1 = hosted("inproj_bwd", _proj_bwd, col_pieces, env["w_in_p"], x, g1, dx1, "inproj_bwd")

    grads = dict(env["grads_early"], norm_mix_g=dg1, conv_w=dcw, conv_b=dcb)
    return dx, grads


def _adamw(gparts, w, m, v, name, tr=256):
    n, rows, cols = gparts.shape
    tr = _div_tile(rows, tr)
    c1 = 1.0 - ADAM_B1 ** ADAM_STEP
    c2 = 1.0 - ADAM_B2 ** ADAM_STEP

    def body(g_ref, w_ref, m_ref, v_ref, go_ref, d_ref, mo_ref, vo_ref):
        g = g_ref[0].astype(f32)
        for j in range(1, n):
            g = g + g_ref[j].astype(f32)
        mn = ADAM_B1 * m_ref[0] + (1.0 - ADAM_B1) * g
        vn = ADAM_B2 * v_ref[0] + (1.0 - ADAM_B2) * (g * g)
        m_hat = mn / c1
        v_hat = vn / c2
        go_ref[0] = g
        d_ref[0] = -ADAM_LR * (m_hat / (jnp.sqrt(v_hat) + ADAM_EPS) + ADAM_WD * w_ref[0])
        mo_ref[0] = mn
        vo_ref[0] = vn

    blk = pl.BlockSpec((1, tr, cols), lambda i: (0, i, 0))
    shp = jax.ShapeDtypeStruct((1, rows, cols), f32)
    return pl.pallas_call(
        body, name=name, grid=(rows // tr,),
        in_specs=[pl.BlockSpec((n, tr, cols), lambda i: (0, i, 0)), blk, blk, blk],
        out_specs=[blk, blk, blk, blk], out_shape=[shp, shp, shp, shp],
        compiler_params=_cparams())(gparts, w, m, v)


def _sum_parts(parts, name):
    n, rows, cols = parts.shape

    def body(p_ref, o_ref):
        acc = p_ref[0].astype(f32)
        for j in range(1, n):
            acc = acc + p_ref[j].astype(f32)
        o_ref[...] = acc

    return pl.pallas_call(
        body, name=name, out_shape=jax.ShapeDtypeStruct((rows, cols), f32),
        compiler_params=_cparams())(parts)


def _pack_rows(arrs, dtype=f32):
    rows, spans, at = [], [], 0
    for a in arrs:
        flat = a.reshape(-1).astype(dtype)
        nr = -(-flat.shape[0] // 1024)
        rows.append(jnp.pad(flat, (0, nr * 1024 - flat.shape[0])).reshape(nr, 1024))
        spans.append((at, nr))
        at += nr
    pad = (-at) % 16
    if pad:
        rows.append(jnp.zeros((pad, 1024), dtype))
    return jnp.concatenate(rows, axis=0), spans


def _unpack_rows(packed, spans, shapes):
    out = []
    for (at, nr), shp in zip(spans, shapes):
        n = math.prod(shp)
        out.append(packed[at:at + nr].reshape(-1)[:n].reshape(shp))
    return out


BIG = ("w_in", "w_out", "w_ffn_in", "w_ffn_out")
SMALL_REPL = ("norm_mix_g", "b_gate", "conv_b", "lru_wa", "lru_wx", "attn_sink", "norm_ffn_g", "norm_final_g")
SMALL_SHARD = ("conv_w", "lru_lambda", "lru_ba", "lru_bx")
ORDER = ("norm_mix_g", "w_in", "b_gate", "conv_w", "conv_b", "lru_lambda", "lru_wa", "lru_ba", "lru_wx",
         "lru_bx", "attn_sink", "w_out", "norm_ffn_g", "w_ffn_in", "w_ffn_out", "norm_final_g")
EARLY_F32 = ("loss", "b_gate", "lru_lambda", "lru_ba", "lru_bx", "attn_sink", "norm_ffn_g", "norm_final_g")
EARLY_BF16 = ("lru_wa", "lru_wx")
LATE = ("norm_mix_g", "conv_w", "conv_b")


def kernel(x, norm_mix_g, w_in, b_gate, conv_w, conv_b, lru_lambda, lru_wa, lru_ba, lru_wx, lru_bx, attn_sink, w_out, norm_ffn_g, w_ffn_in, w_ffn_out, norm_final_g, loss_target, m_norm_mix_g, m_w_in, m_b_gate, m_conv_w, m_conv_b, m_lru_lambda, m_lru_wa, m_lru_ba, m_lru_wx, m_lru_bx, m_attn_sink, m_w_out, m_norm_ffn_g, m_w_ffn_in, m_w_ffn_out, m_norm_final_g, v_norm_mix_g, v_w_in, v_b_gate, v_conv_w, v_conv_b, v_lru_lambda, v_lru_wa, v_lru_ba, v_lru_wx, v_lru_bx, v_attn_sink, v_w_out, v_norm_ffn_g, v_w_ffn_in, v_w_ffn_out, v_norm_final_g):
    w = dict(norm_mix_g=norm_mix_g, w_in=w_in, b_gate=b_gate, conv_w=conv_w, conv_b=conv_b, lru_lambda=lru_lambda,
             lru_wa=lru_wa, lru_ba=lru_ba, lru_wx=lru_wx, lru_bx=lru_bx, attn_sink=attn_sink, w_out=w_out,
             norm_ffn_g=norm_ffn_g, w_ffn_in=w_ffn_in, w_ffn_out=w_ffn_out, norm_final_g=norm_final_g)
    m = dict(norm_mix_g=m_norm_mix_g, w_in=m_w_in, b_gate=m_b_gate, conv_w=m_conv_w, conv_b=m_conv_b,
             lru_lambda=m_lru_lambda, lru_wa=m_lru_wa, lru_ba=m_lru_ba, lru_wx=m_lru_wx, lru_bx=m_lru_bx,
             attn_sink=m_attn_sink, w_out=m_w_out, norm_ffn_g=m_norm_ffn_g, w_ffn_in=m_w_ffn_in,
             w_ffn_out=m_w_ffn_out, norm_final_g=m_norm_final_g)
    v = dict(norm_mix_g=v_norm_mix_g, w_in=v_w_in, b_gate=v_b_gate, conv_w=v_conv_w, conv_b=v_conv_b,
             lru_lambda=v_lru_lambda, lru_wa=v_lru_wa, lru_ba=v_lru_ba, lru_wx=v_lru_wx, lru_bx=v_lru_bx,
             attn_sink=v_attn_sink, w_out=v_w_out, norm_ffn_g=v_norm_ffn_g, w_ffn_in=v_w_ffn_in,
             w_ffn_out=v_w_ffn_out, norm_final_g=v_norm_final_g)
    me = 4 * lax.axis_index("x") + 2 * lax.axis_index("y") + lax.axis_index("c")

    def cols_full(got):
        return jnp.swapaxes(got, 0, 1).reshape(got.shape[1], -1)

    def cols_parts(g):
        return jnp.swapaxes(g.reshape(g.shape[0], N_DEV, -1), 0, 1)

    def rows_parts(g):
        return g.reshape(N_DEV, -1, g.shape[1])

    shard_rows = jnp.concatenate([w[n][0] for n in SMALL_SHARD], axis=0)
    got_w_in, got_rows = _exchange([(w_in[0].astype(bf16), False), (shard_rows, False)], "gather_w_in")
    full_rows = cols_full(got_rows)
    small = {n: w[n] for n in ("norm_mix_g", "b_gate", "conv_b", "attn_sink", "norm_ffn_g")}
    small["lru_wa"], small["lru_wx"] = lru_wa[0], lru_wx[0]
    small["norm_final_g"] = norm_final_g.reshape(1, D)
    small["conv_w"], small["lru_lambda"] = full_rows[0:4], full_rows[4:6]
    small["lru_ba"], small["lru_bx"] = full_rows[6:8], full_rows[8:10]

    env = {"w_in_p": _perm_cols(cols_full(got_w_in))}
    recv = {}

    def before(name):
        if name == "norm_inproj":
            return [(w_out[0].astype(bf16), False), (w_ffn_out[0].astype(bf16), False)]
        if name == "attn_fwd":
            return [(w_ffn_in[0].astype(bf16), False)]
        if name == "ffn_in_bwd":
            return [(rows_parts(env["dw_fo"]).astype(bf16), True)]
        if name == "attn_bwd":
            return [(rows_parts(env["dw_out"]).astype(bf16), True)]
        if name == "lru_bwd":
            return [(cols_parts(env["dw_fi"]).astype(bf16), True)]
        if name == "conv_bwd":
            ge = env["grads_early"]
            p32, env["early_f32_spans"] = _pack_rows([ge[n] for n in EARLY_F32])
            p16, env["early_bf16_spans"] = _pack_rows([ge[n] for n in EARLY_BF16], bf16)
            return [(p32, False), (p16, False)]
        if name == "inproj_bwd":
            return [(cols_parts(env["dw_in"]).astype(bf16), True)]
        return []

    def after(name, got):
        if name == "norm_inproj":
            env["w_out"], env["w_fo"] = got[0].reshape(D, D), got[1].reshape(D_FF, D)
        elif name == "attn_fwd":
            env["w_fi"] = cols_full(got[0])
        elif name == "ffn_in_bwd":
            recv["w_ffn_out"] = got[0]
        elif name == "attn_bwd":
            recv["w_out"] = got[0]
        elif name == "lru_bwd":
            recv["w_ffn_in"] = got[0]
        elif name == "conv_bwd":
            recv["early_f32"], recv["early_bf16"] = got
        elif name == "inproj_bwd":
            recv["w_in"] = got[0]

    grad_x, grads = _local_step(x[0], loss_target[0], small, env, before, after)

    outs = {}
    for name in BIG:
        outs[name] = _adamw(recv[name], w[name], m[name], v[name], "adamw_" + name)

    small_names = SMALL_REPL + SMALL_SHARD
    late_packed, late_spans = _pack_rows([grads[n] for n in LATE])
    (got_late,) = _exchange([(late_packed, False)], "gather_late_grads")
    summed = {}
    for names, got, spans, tag in ((EARLY_F32, recv["early_f32"], env["early_f32_spans"], "early_f32"),
                                   (EARLY_BF16, recv["early_bf16"], env["early_bf16_spans"], "early_bf16"),
                                   (LATE, got_late, late_spans, "late")):
        total = _sum_parts(got, "sum_small_" + tag)
        summed.update(zip(names, _unpack_rows(total, spans, [grads[n].shape for n in names])))
    loss = summed["loss"].reshape(())
    gsm = {n: summed[n].reshape(w[n].shape) for n in SMALL_REPL}
    for n in SMALL_SHARD:
        full = summed[n]
        gsm[n] = lax.dynamic_slice_in_dim(full, me * 128, 128, axis=1).reshape(w[n].shape)
    pk = lambda dct: _pack_rows([dct[n] for n in small_names])[0]
    gp, sp = _pack_rows([gsm[n] for n in small_names])
    res = _adamw(gp[None], pk(w)[None], pk(m)[None], pk(v)[None], "adamw_small")
    sshapes = [w[n].shape for n in small_names]
    for idx, t in enumerate(res):
        for n, a in zip(small_names, _unpack_rows(t[0], sp, sshapes)):
            outs.setdefault(n, [None] * 4)[idx] = a

    result = [loss, grad_x[None]]
    for idx in range(4):
        result += [outs[n][idx] for n in ORDER]
    return tuple(result)
```

```python
import functools
import math

import jax
import jax.numpy as jnp
from jax import lax
from jax.experimental import pallas as pl
from jax.experimental.pallas import tpu as pltpu

f32 = jnp.float32
bf16 = jnp.bfloat16

D = 1024
D_FF = 2816
IN_W = 5632
N_HEADS = 16
N_KV = 4
HEAD_DIM = 64
WINDOW = 128
BLK = 128
LRU_HEADS = 16
LRU_BLOCK = 64
LRU_GROUPS = 4
LRU_GW = 256
LRU_CHUNK = 128
LRU_ROWS = 1024
RGLRU_C = 8.0
EPS = 1e-6
NEG_INF = -1e30
N_DEV = 8

ADAM_LR = 0.001
ADAM_B1 = 0.9
ADAM_B2 = 0.999
ADAM_EPS = 1e-08
ADAM_WD = 0.01
ADAM_STEP = 10

VMEM_MB = 56

C_U, C_G, C_Q, C_Z0, C_Z1, C_K, C_V = 0, 1024, 2048, 3072, 4096, 5120, 5376


def _cparams(vmem_mb=VMEM_MB):
    return pltpu.CompilerParams(vmem_limit_bytes=vmem_mb << 20)


def _div_tile(n, pref):
    if n <= pref:
        return n
    return max(t for t in range(8, pref + 1, 8) if n % t == 0)


def _perm_cols(w):
    return jnp.concatenate([w[:, :3072], w[:, 3584:5632], w[:, 3072:3584]], axis=1)


def _sigmoid(x):
    return 0.5 * jnp.tanh(0.5 * x) + 0.5


def _log1p(x):
    u = 1.0 + x
    d = u - 1.0
    return jnp.where(d == 0.0, x, jnp.log(u) * (x / jnp.where(d == 0.0, 1.0, d)))


def _softplus(x):
    return jnp.maximum(x, 0.0) + _log1p(jnp.exp(-jnp.abs(x)))


def _gelu_and_grad(x):
    c = math.sqrt(2.0 / math.pi)
    inner = c * (x + 0.044715 * (x * x * x))
    t = jnp.tanh(inner)
    gelu = 0.5 * x * (1.0 + t)
    dinner = c * (1.0 + 3 * 0.044715 * (x * x))
    dgelu = 0.5 * (1.0 + t) + 0.5 * x * (1.0 - t * t) * dinner
    return gelu, dgelu


def _rms_bwd(dn, xv, g):
    r = lax.rsqrt(jnp.mean(xv * xv, axis=-1, keepdims=True) + EPS)
    xh = xv * r
    dxh = dn * g
    dx = r * (dxh - xh * jnp.mean(dxh * xh, axis=-1, keepdims=True))
    return dx, dn * xh


ANY_SPEC = pl.BlockSpec(memory_space=pl.ANY)


def _comm_out_shape(src, scatter):
    return jax.ShapeDtypeStruct((N_DEV, *(src.shape[1:] if scatter else src.shape)), src.dtype)


def _comm_sems():
    return [pltpu.SemaphoreType.DMA((N_DEV - 1,)), pltpu.SemaphoreType.DMA((N_DEV - 1,)), pltpu.SemaphoreType.DMA]


def _scatter_descs(src_ref, out_ref, send_sems, recv_sems, local_sem):
    x, y, c = lax.axis_index("x"), lax.axis_index("y"), lax.axis_index("c")
    me = 4 * x + 2 * y + c
    descs = [pltpu.make_async_copy(src_ref.at[me], out_ref.at[me], local_sem)]
    for k in range(1, N_DEV):
        px, py, pc = x ^ (k >> 2), y ^ ((k >> 1) & 1), c ^ (k & 1)
        descs.append(pltpu.make_async_remote_copy(
            src_ref=src_ref.at[4 * px + 2 * py + pc], dst_ref=out_ref.at[me],
            send_sem=send_sems.at[k - 1], recv_sem=recv_sems.at[k - 1],
            device_id=(px, py, pc), device_id_type=pl.DeviceIdType.MESH))
    return descs


def _gather_copies(src_ref, out_ref, send_sems, recv_sems, local_sem, starting):
    x, y, c = lax.axis_index("x"), lax.axis_index("y"), lax.axis_index("c")
    me, sibling = (x, y, c), (x, y, 1 - c)
    chips = [(1 - x, y), (x, 1 - y), (1 - x, 1 - y)]

    def slot(px, py, pc):
        return out_ref.at[4 * px + 2 * py + pc]

    def copy(k, block, to, src=None):
        return pltpu.make_async_remote_copy(
            src_ref=slot(*block) if src is None else src, dst_ref=slot(*block),
            send_sem=send_sems.at[k], recv_sem=recv_sems.at[k], device_id=to, device_id_type=pl.DeviceIdType.MESH)

    local = pltpu.make_async_copy(src_ref, slot(*me), local_sem)
    first = [copy(0, me, sibling, src=src_ref)] + [copy(1 + j, me, (*chip, c), src=src_ref)
                                                    for j, chip in enumerate(chips)]
    if starting:
        return local, first
    passed = [copy(4 + j, (*chip, c), sibling) for j, chip in enumerate(chips)]
    landed = [copy(1 + j, (*chip, c), me) for j, chip in enumerate(chips)]
    later = [copy(0, sibling, me)] + [copy(4 + j, (*chip, 1 - c), me) for j, chip in enumerate(chips)]
    return local, first, passed, landed, later


def _comm_start(src_ref, out_ref, sems, scatter):
    if scatter:
        for d in _scatter_descs(src_ref, out_ref, *sems):
            d.start()
    else:
        local, first = _gather_copies(src_ref, out_ref, *sems, starting=True)
        local.start()
        for cp in first:
            cp.start()


def _comm_pass_on(src_ref, out_ref, sems, scatter):
    if not scatter:
        _, _, passed, landed, _ = _gather_copies(src_ref, out_ref, *sems, starting=False)
        for arrived, onward in zip(landed, passed):
            arrived.wait_recv()
            onward.start()


def _comm_finish(src_ref, out_ref, sems, scatter):
    if scatter:
        for d in _scatter_descs(src_ref, out_ref, *sems):
            d.wait()
    else:
        local, first, passed, _, later = _gather_copies(src_ref, out_ref, *sems, starting=False)
        for cp in later:
            cp.wait_recv()
        for cp in first + passed:
            cp.wait_send()
        local.wait()


def _exchange(comm, name):
    nc = len(comm)

    def body(*refs):
        srcs, outs, sems = refs[:nc], refs[nc:2 * nc], refs[2 * nc:]
        for stage in (_comm_start, _comm_pass_on, _comm_finish):
            for i in range(nc):
                stage(srcs[i], outs[i], sems[3 * i:3 * i + 3], comm[i][1])

    return pl.pallas_call(
        body, name=name, in_specs=[ANY_SPEC] * nc, out_specs=[ANY_SPEC] * nc,
        out_shape=[_comm_out_shape(*c) for c in comm],
        scratch_shapes=[s for _ in comm for s in _comm_sems()],
    )(*[c[0] for c in comm])


def _hosted_call(body, *, name, grid, in_specs, out_specs, out_shape, args, scratch_shapes=(), comm=()):
    nin, nout, nscr, nc = len(in_specs), len(out_specs), len(scratch_shapes), len(comm)
    steps = math.prod(grid)

    def wrapped(*refs):
        ins = refs[:nin]
        csrc = refs[nin:nin + nc]
        outs = refs[nin + nc:nin + nc + nout]
        cout = refs[nin + nc + nout:nin + 2 * nc + nout]
        scr = refs[nin + 2 * nc + nout:]
        sems = scr[nscr:]

        def at(step, stage):
            lin = 0
            for a in range(len(grid)):
                lin = lin * grid[a] + pl.program_id(a)

            @pl.when(lin == step)
            def _():
                for i in range(nc):
                    stage(csrc[i], cout[i], sems[3 * i:3 * i + 3], comm[i][1])

        if nc:
            at(0, _comm_start)

        body(*ins, *outs, *scr[:nscr])

        if nc:
            at((3 * (steps - 1)) // 4, _comm_pass_on)
            at(steps - 1, _comm_finish)

    res = pl.pallas_call(
        wrapped, name=name, grid=grid,
        in_specs=[*in_specs, *[ANY_SPEC] * nc], out_specs=[*out_specs, *[ANY_SPEC] * nc],
        out_shape=[*out_shape, *[_comm_out_shape(*c) for c in comm]],
        scratch_shapes=[*scratch_shapes, *[s for _ in comm for s in _comm_sems()]],
        compiler_params=_cparams())(*args, *[c[0] for c in comm])
    return res[:nout], res[nout:]


def _norm_matmul(x, g, w, name, tm=1024, tn=1408, comm=()):
    S, dm = x.shape
    n = w.shape[1]
    tm = min(tm, S)

    def body(x_ref, g_ref, w_ref, xn_ref, o_ref):
        @pl.when(pl.program_id(1) == 0)
        def _():
            xv = x_ref[...]
            r = lax.rsqrt(jnp.mean(xv * xv, axis=-1, keepdims=True) + EPS)
            xn_ref[...] = ((xv * r) * g_ref[...]).astype(bf16)

        o_ref[...] = jnp.dot(xn_ref[...], w_ref[...], preferred_element_type=f32).astype(bf16)

    return _hosted_call(
        body, name=name, grid=(S // tm, n // tn),
        in_specs=[pl.BlockSpec((tm, dm), lambda i, j: (i, 0)),
                  pl.BlockSpec((1, dm), lambda i, j: (0, 0)),
                  pl.BlockSpec((dm, tn), lambda i, j: (0, j))],
        out_specs=[pl.BlockSpec((tm, dm), lambda i, j: (i, 0)),
                   pl.BlockSpec((tm, tn), lambda i, j: (i, j))],
        out_shape=[jax.ShapeDtypeStruct((S, dm), bf16), jax.ShapeDtypeStruct((S, n), bf16)],
        args=(x, g, w), comm=comm)


def _mm_tn(a, b, name, tk, tn, tmc=2048, into=None, col=0, out_cols=None):
    m, ka = a.shape
    n = b.shape[1]
    tmc = min(tmc, m)
    nk = m // tmc

    def body(a_ref, b_ref, *rest):
        o_ref, acc_ref = rest[-2:]
        k = pl.program_id(2)
        part = lax.dot_general(a_ref[...], b_ref[...], (((0,), (0,)), ((), ())), preferred_element_type=f32)

        @pl.when(k == 0)
        def _():
            acc_ref[...] = part

        @pl.when(k > 0)
        def _():
            acc_ref[...] += part

        @pl.when(k == nk - 1)
        def _():
            o_ref[...] = acc_ref[...].astype(bf16)

    in_specs = [pl.BlockSpec((tmc, tk), lambda i, j, k: (k, i)), pl.BlockSpec((tmc, tn), lambda i, j, k: (k, j))]
    if into is None:
        return pl.pallas_call(
            body, name=name, grid=(ka // tk, n // tn, nk), in_specs=in_specs,
            out_specs=pl.BlockSpec((tk, tn), lambda i, j, k: (i, j + col)),
            out_shape=jax.ShapeDtypeStruct((ka, out_cols or n), bf16),
            scratch_shapes=[pltpu.VMEM((tk, tn), f32)],
            compiler_params=_cparams())(a, b)
    return pl.pallas_call(
        body, name=name, grid=(ka // tk, n // tn, nk), in_specs=[*in_specs, ANY_SPEC],
        out_specs=pl.BlockSpec((tk, tn), lambda i, j, k: (i, j + col)),
        out_shape=jax.ShapeDtypeStruct(into.shape, into.dtype),
        scratch_shapes=[pltpu.VMEM((tk, tn), f32)], input_output_aliases={2: 0},
        compiler_params=_cparams())(a, b, into)


HALO = 16


def _rows_at(ext, o, tc):
    if o == 0:
        return ext[HALO:HALO + tc]
    return pltpu.roll(ext, (-o) % ext.shape[0], 0)[HALO:HALO + tc]


def _halo_specs(tc, S, width, col):
    per = tc // HALO
    last = S // HALO - 1
    return (pl.BlockSpec((tc, width), lambda i: (i, col)),
            pl.BlockSpec((HALO, width), lambda i: (jnp.maximum(i * per - 1, 0), col)),
            pl.BlockSpec((HALO, width), lambda i: (jnp.minimum((i + 1) * per, last), col)))


def _extended(cur_ref, prev_ref, next_ref, i, nsteps):
    prev = jnp.where(i > 0, prev_ref[...].astype(f32), 0.0)
    nxt = jnp.where(i < nsteps - 1, next_ref[...].astype(f32), 0.0)
    return jnp.concatenate([prev, cur_ref[...].astype(f32), nxt], axis=0)


def _conv_fwd(proj, cw, cb, tc=1024):
    S = proj.shape[0]
    tc = min(tc, S)
    nsteps = S // tc

    def body(cur_ref, prev_ref, next_ref, w_ref, b_ref, o_ref):
        ext = _extended(cur_ref, prev_ref, next_ref, pl.program_id(0), nsteps)
        acc = _rows_at(ext, -2, tc) * w_ref[0:1, :]
        for k in range(1, 4):
            acc = acc + _rows_at(ext, k - 2, tc) * w_ref[k:k + 1, :]
        o_ref[...] = acc + b_ref[...]

    return pl.pallas_call(
        body, name="conv_fwd", grid=(nsteps,),
        in_specs=[*_halo_specs(tc, S, D, 0),
                  pl.BlockSpec((4, D), lambda i: (0, 0)), pl.BlockSpec((1, D), lambda i: (0, 0))],
        out_specs=pl.BlockSpec((tc, D), lambda i: (i, 0)),
        out_shape=jax.ShapeDtypeStruct((S, D), f32),
        compiler_params=_cparams())(proj, proj, proj, cw, cb)


def _conv_bwd(duc_f, duc_b, proj, cw, tc=1024, comm=()):
    S = proj.shape[0]
    tc = min(tc, S)
    nsteps = S // tc

    def body(fc, fp, fn, bc, bp, bn, uc_, up, un, w_ref, du_ref, dw_ref, db_ref):
        i = pl.program_id(0)

        @pl.when(i == 0)
        def _():
            dw_ref[...] = jnp.zeros_like(dw_ref)
            db_ref[...] = jnp.zeros_like(db_ref)

        dext = _extended(fc, fp, fn, i, nsteps) + _extended(bc, bp, bn, i, nsteps)
        uext = _extended(uc_, up, un, i, nsteps)
        d = dext[HALO:HALO + tc]
        acc = _rows_at(dext, 2, tc) * w_ref[0:1, :]
        for k in range(1, 4):
            acc = acc + _rows_at(dext, 2 - k, tc) * w_ref[k:k + 1, :]
        du_ref[...] = acc.astype(bf16)
        wrow = lax.broadcasted_iota(jnp.int32, (4, D), 0)
        for k in range(4):
            dw_ref[...] += jnp.where(wrow == k, jnp.sum(d * _rows_at(uext, k - 2, tc), axis=0, keepdims=True), 0.0)
        db_ref[...] += jnp.sum(d, axis=0, keepdims=True)

    return _hosted_call(
        body, name="conv_bwd", grid=(nsteps,),
        in_specs=[*_halo_specs(tc, S, D, 0), *_halo_specs(tc, S, D, 0), *_halo_specs(tc, S, D, 0),
                  pl.BlockSpec((4, D), lambda i: (0, 0))],
        out_specs=[pl.BlockSpec((tc, D), lambda i: (i, 0)),
                   pl.BlockSpec((4, D), lambda i: (0, 0)), pl.BlockSpec((1, D), lambda i: (0, 0))],
        out_shape=[jax.ShapeDtypeStruct((S, D), bf16), jax.ShapeDtypeStruct((4, D), f32),
                   jax.ShapeDtypeStruct((1, D), f32)],
        args=(duc_f, duc_f, duc_f, duc_b, duc_b, duc_b, proj, proj, proj, cw), comm=comm)


def _scan_scratch():
    halves = [pltpu.VMEM((LRU_CHUNK, 128), f32) for _ in range(2 * (LRU_GW // 128))]
    return [*halves, pltpu.VMEM((LRU_CHUNK // 8, LRU_GW), f32), pltpu.VMEM((LRU_CHUNK // 8, LRU_GW), f32)]


def _log_scan(a, b, row, n, reverse, steps):
    for s in steps:
        shift = a.shape[0] - s if reverse else s
        keep = (row < n - s) if reverse else (row >= s)
        a_sh = pltpu.roll(a, shift, 0)
        b_sh = pltpu.roll(b, shift, 0)
        b = jnp.where(keep, a * b_sh + b, b)
        a = jnp.where(keep, a * a_sh, a)
    return a, b


def _scan_chunk(a, b, carry, reverse, *scratch):
    tc, w = a.shape
    ng = tc // 8
    nl = w // 128
    sa_refs, sb_refs, sc_ref, st_ref = scratch[:nl], scratch[nl:2 * nl], scratch[2 * nl], scratch[2 * nl + 1]
    sub = lax.broadcasted_iota(jnp.int32, (8, w), 0)
    ag, bg = [], []
    for k in range(ng):
        ak, bk = _log_scan(a[8 * k:8 * k + 8], b[8 * k:8 * k + 8], sub, 8, reverse, (1, 2, 4))
        ag.append(ak)
        bg.append(bk)
    a = jnp.concatenate(ag, axis=0)
    b = jnp.concatenate(bg, axis=0)
    edge = 0 if reverse else 7
    for i in range(nl):
        sa_refs[i][...] = a[:, 128 * i:128 * (i + 1)]
        sb_refs[i][...] = b[:, 128 * i:128 * (i + 1)]
    ta = jnp.concatenate([r[pl.ds(edge, ng, stride=8), :] for r in sa_refs], axis=1)
    tb = jnp.concatenate([r[pl.ds(edge, ng, stride=8), :] for r in sb_refs], axis=1)
    grow = lax.broadcasted_iota(jnp.int32, (ng, w), 0)
    ta, tb = _log_scan(ta, tb, grow, ng, reverse, [1 << i for i in range(ng.bit_length() - 1)])
    state = tb + ta * carry
    st_ref[...] = state
    if reverse:
        sc_ref[...] = jnp.where(grow == ng - 1, carry, pltpu.roll(state, ng - 1, 0))
    else:
        sc_ref[...] = jnp.where(grow == 0, carry, pltpu.roll(state, 1, 0))
    h = jnp.concatenate([bg[k] + ag[k] * sc_ref[k:k + 1, :] for k in range(ng)], axis=0)
    return h, (st_ref[0:1, :] if reverse else st_ref[ng - 1:ng, :])


def _lru_gates(uc, w, p_ref):
    pre = jnp.dot(uc.astype(bf16), w, preferred_element_type=f32)
    r = _sigmoid(pre[:, :LRU_GW] + p_ref[0, 1:2, :])
    gi = _sigmoid(pre[:, LRU_GW:] + p_ref[0, 2:3, :])
    sp = _softplus(-p_ref[0, 0:1, :])
    log_a = -RGLRU_C * r * sp
    a = jnp.exp(log_a)
    x = 2.0 * log_a
    series = -x * (1.0 + x * (0.5 + x * (1.0 / 6 + x * (1.0 / 24))))
    beta = jnp.sqrt(jnp.maximum(jnp.where(x > -0.0625, series, 1.0 - a * a), 0.0))
    return r, gi, sp, a, beta


def _lru_fwd(uc, wg, lp, reverse, comm=()):
    S = uc.shape[0]
    tc = LRU_CHUNK
    rows = min(LRU_ROWS, S)
    nsub = rows // tc
    nblk = S // rows
    d = 1 if reverse else 0

    def bidx(c):
        return nblk - 1 - c if reverse else c

    def body(uc_ref, w_ref, p_ref, h_ref, carry_ref, *scan_scratch):
        @pl.when(pl.program_id(1) == 0)
        def _():
            carry_ref[...] = jnp.zeros_like(carry_ref)

        carry = carry_ref[...]
        for j in (reversed(range(nsub)) if reverse else range(nsub)):
            sl = slice(j * tc, (j + 1) * tc)
            ucv = uc_ref[sl, :]
            _, gi, _, a, beta = _lru_gates(ucv, w_ref[0], p_ref)
            h, carry = _scan_chunk(a, beta * (gi * ucv), carry, reverse, *scan_scratch)
            h_ref[sl, :] = h.astype(bf16)
        carry_ref[...] = carry

    return _hosted_call(
        body, name="lru_fwd_rev" if reverse else "lru_fwd", grid=(LRU_GROUPS, nblk),
        in_specs=[pl.BlockSpec((rows, LRU_GW), lambda g, c: (bidx(c), g)),
                  pl.BlockSpec((1, LRU_GW, 2 * LRU_GW), lambda g, c: (g, 0, d)),
                  pl.BlockSpec((1, 8, LRU_GW), lambda g, c: (d, 0, g))],
        out_specs=[pl.BlockSpec((rows, LRU_GW), lambda g, c: (bidx(c), g))],
        out_shape=[jax.ShapeDtypeStruct((S, D), bf16)],
        scratch_shapes=[pltpu.VMEM((1, LRU_GW), f32), *_scan_scratch()],
        args=(uc, wg, lp), comm=comm)


def _lru_bwd(uc, dh, h, wg, lp, reverse, comm=()):
    S = uc.shape[0]
    tc = LRU_CHUNK
    rows = min(LRU_ROWS, S)
    nsub = rows // tc
    nblk = S // rows
    d = 1 if reverse else 0
    per = rows // HALO
    last8 = S // HALO - 1

    def bidx(c):
        return c if reverse else nblk - 1 - c

    def halo_idx(c):
        if reverse:
            return jnp.minimum((bidx(c) + 1) * per, last8)
        return jnp.maximum(bidx(c) * per - 1, 0)

    def body(uc_ref, dh_ref, h_ref, halo_ref, w_ref, p_ref, duc_ref, dw_ref, dp_ref, carry_ref, tmp_ref,
             *scan_scratch):
        c = pl.program_id(1)
        bi = bidx(c)

        @pl.when(c == 0)
        def _():
            carry_ref[...] = jnp.zeros_like(carry_ref)
            dw_ref[...] = jnp.zeros_like(dw_ref)
            dp_ref[...] = jnp.zeros_like(dp_ref)

        row = lax.broadcasted_iota(jnp.int32, (tc, LRU_GW), 0)
        carry = carry_ref[...]
        dw = jnp.zeros((LRU_GW, 2 * LRU_GW), f32)
        dsp = jnp.zeros((1, LRU_GW), f32)
        dba = jnp.zeros((1, LRU_GW), f32)
        dbx = jnp.zeros((1, LRU_GW), f32)
        for j in (range(nsub) if reverse else reversed(range(nsub))):
            sl = slice(j * tc, (j + 1) * tc)
            ucv = uc_ref[sl, :]
            ucb = ucv.astype(bf16)
            r, gi, sp, a, beta = _lru_gates(ucv, w_ref[0], p_ref)
            hv = h_ref[sl, :].astype(f32)
            dhv = dh_ref[sl, :].astype(f32)
            if reverse:
                alpha = jnp.where(row == 0, 1.0, pltpu.roll(a, 1, 0))
                gsc, _ = _scan_chunk(alpha, dhv, carry, False, *scan_scratch)
                if j < nsub - 1:
                    edge = h_ref[(j + 1) * tc:(j + 1) * tc + HALO, :].astype(f32)[0:1, :]
                else:
                    edge = jnp.where(bi < nblk - 1, halo_ref[...].astype(f32)[0:1, :], 0.0)
                h_nb = jnp.where(row == tc - 1, edge, pltpu.roll(hv, tc - 1, 0))
            else:
                alpha = jnp.where(row == tc - 1, 1.0, pltpu.roll(a, tc - 1, 0))
                gsc, _ = _scan_chunk(alpha, dhv, carry, True, *scan_scratch)
                if j > 0:
                    edge = h_ref[j * tc - HALO:j * tc, :].astype(f32)[HALO - 1:HALO, :]
                else:
                    edge = jnp.where(bi > 0, halo_ref[...].astype(f32)[HALO - 1:HALO, :], 0.0)
                h_nb = jnp.where(row == 0, edge, pltpu.roll(hv, 1, 0))
            tmp_ref[...] = a * gsc
            carry = tmp_ref[tc - 1:tc, :] if reverse else tmp_ref[0:1, :]

            da = gsc * h_nb
            dbeta = gsc * (gi * ucv)
            dl = da * a - dbeta * (a * a) / beta
            dr = dl * (-RGLRU_C * sp)
            dsp = dsp + jnp.sum(dl * (-RGLRU_C * r), axis=0, keepdims=True)
            dgi = gsc * beta * ucv
            dpre_r = dr * r * (1.0 - r)
            dpre_i = dgi * gi * (1.0 - gi)
            dba = dba + jnp.sum(dpre_r, axis=0, keepdims=True)
            dbx = dbx + jnp.sum(dpre_i, axis=0, keepdims=True)
            dpre = jnp.concatenate([dpre_r, dpre_i], axis=1).astype(bf16)
            back = lax.dot_general(dpre, w_ref[0], (((1,), (1,)), ((), ())), preferred_element_type=f32)
            duc_ref[sl, :] = (gsc * beta * gi + back).astype(bf16)
            dw = dw + lax.dot_general(ucb, dpre, (((0,), (0,)), ((), ())), preferred_element_type=f32)
        carry_ref[...] = carry
        dw_ref[0] += dw
        dlam = -dsp / (1.0 + jnp.exp(p_ref[0, 0:1, :]))
        prow = lax.broadcasted_iota(jnp.int32, (8, LRU_GW), 0)
        dp_ref[...] += (jnp.where(prow == 0, dlam, 0.0) + jnp.where(prow == 1, dba, 0.0)
                        + jnp.where(prow == 2, dbx, 0.0))

    chunk = pl.BlockSpec((rows, LRU_GW), lambda g, c: (bidx(c), g))
    return _hosted_call(
        body, name="lru_bwd_rev" if reverse else "lru_bwd", grid=(LRU_GROUPS, nblk),
        in_specs=[chunk, chunk, chunk,
                  pl.BlockSpec((HALO, LRU_GW), lambda g, c: (halo_idx(c), g)),
                  pl.BlockSpec((1, LRU_GW, 2 * LRU_GW), lambda g, c: (g, 0, d)),
                  pl.BlockSpec((1, 8, LRU_GW), lambda g, c: (d, 0, g))],
        out_specs=[chunk,
                   pl.BlockSpec((1, LRU_GW, 2 * LRU_GW), lambda g, c: (g, 0, 0)),
                   pl.BlockSpec((8, LRU_GW), lambda g, c: (0, g))],
        out_shape=[jax.ShapeDtypeStruct((S, D), bf16),
                   jax.ShapeDtypeStruct((LRU_GROUPS, LRU_GW, 2 * LRU_GW), f32),
                   jax.ShapeDtypeStruct((8, D), f32)],
        scratch_shapes=[pltpu.VMEM((1, LRU_GW), f32), pltpu.VMEM((tc, LRU_GW), f32), *_scan_scratch()],
        args=(uc, dh, h, h, wg, lp), comm=comm)


def _slope(h):
    return 2.0 ** (-8.0 * (h + 1.0) / N_HEADS)


def _kv_specs(nb, col):
    return [pl.BlockSpec((BLK, N_KV * HEAD_DIM), lambda n: (jnp.maximum(n - 1, 0), col)),
            pl.BlockSpec((BLK, N_KV * HEAD_DIM), lambda n: (n, col)),
            pl.BlockSpec((BLK, N_KV * HEAD_DIM), lambda n: (jnp.minimum(n + 1, nb - 1), col))]


def _dup_windows(r0, r1, r2):
    left = lax.broadcasted_iota(jnp.int32, (3 * BLK, 128), 1) < HEAD_DIM
    win = jnp.concatenate([r0[...], r1[...], r2[...]], axis=0)
    out = []
    for i in range(N_KV // 2):
        t = win[:, i * 128:(i + 1) * 128]
        r = pltpu.roll(t, HEAD_DIM, 1)
        out += [jnp.where(left, t, r).astype(bf16), jnp.where(left, r, t).astype(bf16)]
    return out


def _attn_bias_init(bias_ref):
    k_loc = lax.broadcasted_iota(jnp.int32, (3 * BLK, BLK), 0)
    q_loc = lax.broadcasted_iota(jnp.int32, (3 * BLK, BLK), 1)
    adist = jnp.abs(q_loc + BLK - k_loc)
    adf = adist.astype(f32)
    for e in range(3):
        ok = adist <= WINDOW
        if e == 0:
            ok = ok & (k_loc >= BLK)
        if e == 2:
            ok = ok & (k_loc < 2 * BLK)
        for kv in range(N_KV):
            bias_ref[e, kv] = jnp.concatenate(
                [jnp.where(ok, (-_slope(4 * kv + j)) * adf, NEG_INF) for j in range(4)], axis=1)


def _stack_heads(ref, kv, scale):
    left = lax.broadcasted_iota(jnp.int32, (BLK, 128), 1) < HEAD_DIM
    rows = []
    for pp in range(2):
        t = ref[:, (2 * kv + pp) * 128:(2 * kv + pp + 1) * 128]
        if scale != 1.0:
            t = t * scale
        zero = jnp.zeros_like(t)
        rows += [jnp.where(left, t, zero).astype(bf16), jnp.where(left, zero, t).astype(bf16)]
    return jnp.concatenate(rows, axis=0)


def _attn_softmax(qs, k2, bias, sink_ref, kv, stats=None):
    sink = jnp.concatenate([jnp.full((1, BLK), sink_ref[0, 4 * kv + j], f32) for j in range(4)], axis=1)
    s = lax.dot_general(k2, qs, (((1,), (1,)), ((), ())), preferred_element_type=f32) + bias
    m = jnp.maximum(jnp.max(s, axis=0, keepdims=True), sink) if stats is None else stats[0]
    p = jnp.exp(s - m)
    ps = jnp.exp(sink - m)
    inv = 1.0 / (jnp.sum(p, axis=0, keepdims=True) + ps) if stats is None else stats[1]
    return p, ps, m, inv


def _pair_tiles(t):
    return [jnp.concatenate([t[:HEAD_DIM, 256 * pp:256 * pp + 128],
                             t[HEAD_DIM:, 256 * pp + 128:256 * pp + 256]], axis=0).T for pp in range(2)]


def _attn_fwd(proj, sink, comm=()):
    S = proj.shape[0]
    nb = S // BLK
    assert nb >= 2

    def body(q_ref, k0, k1, k2_, v0, v1, v2_, sink_ref, o_ref, st_ref, bias_ref):
        n = pl.program_id(0)

        @pl.when(n == 0)
        def _():
            _attn_bias_init(bias_ref)

        e = jnp.where(n == 0, 0, jnp.where(n == nb - 1, 2, 1))
        kk = _dup_windows(k0, k1, k2_)
        vv = _dup_windows(v0, v1, v2_)
        tiles = []
        for kv in range(N_KV):
            qs = _stack_heads(q_ref, kv, HEAD_DIM ** -0.5)
            p, _, m, inv = _attn_softmax(qs, kk[kv], bias_ref[e, kv], sink_ref, kv)
            st_ref[0, kv:kv + 1, :] = m
            st_ref[0, N_KV + kv:N_KV + kv + 1, :] = inv
            ot = lax.dot_general(vv[kv], p.astype(bf16), (((0,), (0,)), ((), ())), preferred_element_type=f32)
            tiles += _pair_tiles(ot * inv)
        o_ref[...] = jnp.concatenate(tiles, axis=1).astype(bf16)

    return _hosted_call(
        body, name="attn_fwd", grid=(nb,),
        in_specs=[pl.BlockSpec((BLK, D), lambda n: (n, C_Q // D)),
                  *_kv_specs(nb, C_K // (N_KV * HEAD_DIM)), *_kv_specs(nb, C_V // (N_KV * HEAD_DIM)),
                  pl.BlockSpec(memory_space=pltpu.SMEM)],
        out_specs=[pl.BlockSpec((BLK, D), lambda n: (n, 0)), pl.BlockSpec((1, 2 * N_KV, 4 * BLK), lambda n: (n, 0, 0))],
        out_shape=[jax.ShapeDtypeStruct((S, D), bf16), jax.ShapeDtypeStruct((nb, 2 * N_KV, 4 * BLK), f32)],
        scratch_shapes=[pltpu.VMEM((3, N_KV, 3 * BLK, 4 * BLK), f32)],
        args=(proj, proj, proj, proj, proj, proj, proj, sink), comm=comm)


def _attn_bwd(proj, sink, dyb, stats, comm=()):
    S = proj.shape[0]
    nb = S // BLK
    assert nb >= 2

    def body(q_ref, k0, k1, k2_, v0, v1, v2_, sink_ref, do_ref, st_ref, dq_ref, dk_out, dv_out, ds_ref,
             bias_ref, dk_ref, dv_ref, dsk_ref):
        n = pl.program_id(0)

        @pl.when(n == 0)
        def _():
            _attn_bias_init(bias_ref)
            dk_ref[...] = jnp.zeros_like(dk_ref)
            dv_ref[...] = jnp.zeros_like(dv_ref)
            dsk_ref[...] = jnp.zeros_like(dsk_ref)

        e = jnp.where(n == 0, 0, jnp.where(n == nb - 1, 2, 1))
        kk = _dup_windows(k0, k1, k2_)
        vv = _dup_windows(v0, v1, v2_)
        left3 = lax.broadcasted_iota(jnp.int32, (3 * BLK, 128), 1) < HEAD_DIM
        start = pl.multiple_of(n * BLK, BLK)
        dq_tiles, dks, dvs = [], [], []
        for kv in range(N_KV):
            qs = _stack_heads(q_ref, kv, HEAD_DIM ** -0.5)
            dos = _stack_heads(do_ref, kv, 1.0)
            stats = (st_ref[0, kv:kv + 1, :], st_ref[0, N_KV + kv:N_KV + kv + 1, :])
            p, ps, _, inv = _attn_softmax(qs, kk[kv], bias_ref[e, kv], sink_ref, kv, stats)
            pn = p * inv
            dp = lax.dot_general(vv[kv], dos, (((1,), (1,)), ((), ())), preferred_element_type=f32)
            delta = jnp.sum(pn * dp, axis=0, keepdims=True)
            dsc = (pn * (dp - delta)).astype(bf16)
            dsk_ref[kv:kv + 1, :] += delta * (ps * inv)
            dqt = lax.dot_general(kk[kv], dsc, (((0,), (0,)), ((), ())), preferred_element_type=f32)
            dq_tiles += _pair_tiles(dqt * (HEAD_DIM ** -0.5))
            dk = jnp.dot(dsc, qs, preferred_element_type=f32)
            dv = jnp.dot(pn.astype(bf16), dos, preferred_element_type=f32)
            dks.append(dk + pltpu.roll(dk, HEAD_DIM, 1))
            dvs.append(dv + pltpu.roll(dv, HEAD_DIM, 1))
        for jp in range(N_KV // 2):
            cols = slice(jp * 128, (jp + 1) * 128)
            dk_ref[pl.ds(start, 3 * BLK), cols] += jnp.where(left3, dks[2 * jp], dks[2 * jp + 1])
            dv_ref[pl.ds(start, 3 * BLK), cols] += jnp.where(left3, dvs[2 * jp], dvs[2 * jp + 1])
        dq_ref[...] = jnp.concatenate(dq_tiles, axis=1).astype(bf16)

        @pl.when(n == nb - 1)
        def _():
            pltpu.sync_copy(dk_ref, dk_out)
            pltpu.sync_copy(dv_ref, dv_out)
            lane = lax.broadcasted_iota(jnp.int32, (1, 128), 1)
            dsink = jnp.zeros((1, 128), f32)
            for h in range(N_HEADS):
                part = dsk_ref[h // 4:h // 4 + 1, (h % 4) * BLK:(h % 4 + 1) * BLK]
                dsink = dsink + jnp.where(lane == h, -jnp.sum(part), 0.0)
            ds_ref[...] = dsink

    acc = jax.ShapeDtypeStruct((S + 2 * BLK, N_KV * HEAD_DIM), f32)
    return _hosted_call(
        body, name="attn_bwd", grid=(nb,),
        in_specs=[pl.BlockSpec((BLK, D), lambda n: (n, C_Q // D)),
                  *_kv_specs(nb, C_K // (N_KV * HEAD_DIM)), *_kv_specs(nb, C_V // (N_KV * HEAD_DIM)),
                  pl.BlockSpec(memory_space=pltpu.SMEM),
                  pl.BlockSpec((BLK, D), lambda n: (n, 0)),
                  pl.BlockSpec((1, 2 * N_KV, 4 * BLK), lambda n: (n, 0, 0))],
        out_specs=[pl.BlockSpec((BLK, D), lambda n: (n, 0)), ANY_SPEC, ANY_SPEC,
                   pl.BlockSpec((1, 128), lambda n: (0, 0))],
        out_shape=[jax.ShapeDtypeStruct((S, D), bf16), acc, acc, jax.ShapeDtypeStruct((1, 128), f32)],
        scratch_shapes=[pltpu.VMEM((3, N_KV, 3 * BLK, 4 * BLK), f32), pltpu.VMEM(acc.shape, f32),
                        pltpu.VMEM(acc.shape, f32), pltpu.VMEM((8, 4 * BLK), f32)],
        args=(proj, proj, proj, proj, proj, proj, proj, sink, dyb, stats), comm=comm)


def _merge_parts(hf, hb, g, z0, z1, yb, bg):
    g0 = _sigmoid(z0.astype(f32) + bg[:, :D])
    g1 = _sigmoid(z1.astype(f32) + bg[:, D:])
    gelu, dgelu = _gelu_and_grad(g.astype(f32))
    hs = hf.astype(f32) + hb.astype(f32)
    ya = hs * gelu
    return g0, g1, gelu, dgelu, hs, ya


def _merge_outproj(x, hf, hb, proj, yb, bg, w_out, tm=1024):
    S = x.shape[0]
    tm = min(tm, S)

    def body(x_ref, hf_ref, hb_ref, g_ref, z0_ref, z1_ref, yb_ref, bg_ref, w_ref, mg_ref, x1_ref):
        ybv = yb_ref[...].astype(f32)
        g0, g1, _, _, _, ya = _merge_parts(hf_ref[...], hb_ref[...], g_ref[...], z0_ref[...], z1_ref[...],
                                           ybv, bg_ref[...])
        mg = (g0 * ya + g1 * ybv).astype(bf16)
        mg_ref[...] = mg
        x1_ref[...] = x_ref[...] + jnp.dot(mg, w_ref[...], preferred_element_type=f32)

    row = pl.BlockSpec((tm, D), lambda i: (i, 0))
    return pl.pallas_call(
        body, name="merge_outproj", grid=(S // tm,),
        in_specs=[row, row, row,
                  pl.BlockSpec((tm, D), lambda i: (i, C_G // D)),
                  pl.BlockSpec((tm, D), lambda i: (i, C_Z0 // D)),
                  pl.BlockSpec((tm, D), lambda i: (i, C_Z1 // D)),
                  row, pl.BlockSpec((1, 2 * D), lambda i: (0, 0)), pl.BlockSpec((D, D), lambda i: (0, 0))],
        out_specs=[row, row],
        out_shape=[jax.ShapeDtypeStruct((S, D), bf16), jax.ShapeDtypeStruct((S, D), f32)],
        compiler_params=_cparams())(x, hf, hb, proj, proj, proj, yb, bg, w_out)


def _ffn_out_loss(gu, x1, w_fo, g3, tgt, tm=256):
    S = x1.shape[0]
    tm = min(tm, S)

    def body(gt_ref, up_ref, x1_ref, w_ref, g_ref, t_ref, ff_ref, dx_ref, dxb_ref, loss_ref, dg_ref,
             dgt_ref, dup_ref):
        @pl.when(pl.program_id(0) == 0)
        def _():
            loss_ref[...] = jnp.zeros_like(loss_ref)
            dg_ref[...] = jnp.zeros_like(dg_ref)

        gt = gt_ref[...].astype(f32)
        up = up_ref[...].astype(f32)
        sg = _sigmoid(gt)
        silu = gt * sg
        ff = (silu * up).astype(bf16)
        ff_ref[...] = ff
        x2 = x1_ref[...] + jnp.dot(ff, w_ref[...], preferred_element_type=f32)
        gv = g_ref[...]
        r = lax.rsqrt(jnp.mean(x2 * x2, axis=-1, keepdims=True) + EPS)
        xh = x2 * r
        diff = xh * gv - t_ref[...]
        loss_ref[...] += (0.5 / D) * jnp.sum(diff * diff)
        dy = diff * (1.0 / D)
        dg_ref[...] += jnp.sum(dy * xh, axis=0, keepdims=True)
        dxh = dy * gv
        dx = r * (dxh - xh * jnp.mean(dxh * xh, axis=-1, keepdims=True))
        dx_ref[...] = dx
        dxb = dx.astype(bf16)
        dxb_ref[...] = dxb
        dff = lax.dot_general(dxb, w_ref[...], (((1,), (1,)), ((), ())), preferred_element_type=f32)
        dup_ref[...] = (dff * silu).astype(bf16)
        dgt_ref[...] = ((dff * up) * (sg * (1.0 + gt * (1.0 - sg)))).astype(bf16)

    row = pl.BlockSpec((tm, D), lambda i: (i, 0))
    vec = pl.BlockSpec((1, D), lambda i: (0, 0))
    wide = pl.BlockSpec((tm, D_FF), lambda i: (i, 0))
    wide_shape = jax.ShapeDtypeStruct((S, D_FF), bf16)
    return pl.pallas_call(
        body, name="ffn_out_loss", grid=(S // tm,),
        in_specs=[wide, pl.BlockSpec((tm, D_FF), lambda i: (i, 1)),
                  row, pl.BlockSpec((D_FF, D), lambda i: (0, 0)), vec, row],
        out_specs=[wide, row, row, pl.BlockSpec((1, 128), lambda i: (0, 0)), vec, wide, wide],
        out_shape=[wide_shape, jax.ShapeDtypeStruct((S, D), f32), jax.ShapeDtypeStruct((S, D), bf16),
                   jax.ShapeDtypeStruct((1, 128), f32), jax.ShapeDtypeStruct((1, D), f32), wide_shape, wide_shape],
        compiler_params=_cparams())(gu, gu, x1, w_fo, g3, tgt)


def _proj_bwd(pieces, w, xres, g, dres, name, tm=512, comm=()):
    S = xres.shape[0]
    tm = min(tm, S)
    np_ = len(pieces)

    def body(*refs):
        p_refs = refs[:np_]
        w_refs = refs[np_:2 * np_]
        x_ref, g_ref, dres_ref, dx_ref, dxb_ref, dg_ref = refs[2 * np_:]

        @pl.when(pl.program_id(0) == 0)
        def _():
            dg_ref[...] = jnp.zeros_like(dg_ref)

        nt = (((1,), (1,)), ((), ()))
        dn = lax.dot_general(p_refs[0][...], w_refs[0][...], nt, preferred_element_type=f32)
        for pr, wr in zip(p_refs[1:], w_refs[1:]):
            dn = dn + lax.dot_general(pr[...], wr[...], nt, preferred_element_type=f32)
        dxn, dgc = _rms_bwd(dn, x_ref[...], g_ref[...])
        dx = dres_ref[...] + dxn
        dx_ref[...] = dx
        dxb_ref[...] = dx.astype(bf16)
        dg_ref[...] += jnp.sum(dgc, axis=0, keepdims=True)

    row = pl.BlockSpec((tm, D), lambda i: (i, 0))
    vec = pl.BlockSpec((1, D), lambda i: (0, 0))
    return _hosted_call(
        body, name=name, grid=(S // tm,),
        in_specs=[*[pl.BlockSpec((tm, wd), functools.partial(lambda i, cb: (i, cb), cb=acb))
                    for _, acb, _, wd in pieces],
                  *[pl.BlockSpec((D, wd), functools.partial(lambda i, cb: (0, cb), cb=wcb))
                    for _, _, wcb, wd in pieces],
                  row, vec, row],
        out_specs=[row, row, vec],
        out_shape=[jax.ShapeDtypeStruct((S, D), f32), jax.ShapeDtypeStruct((S, D), bf16),
                   jax.ShapeDtypeStruct((1, D), f32)],
        args=(*[p[0] for p in pieces], *[w] * np_, xres, g, dres), comm=comm)


def _outproj_bwd(dx1b, w_out, hf, hb, proj, yb, bg, tm=1024):
    S = dx1b.shape[0]
    tm = min(tm, S)

    def body(dx_ref, w_ref, hf_ref, hb_ref, g_ref, z0_ref, z1_ref, yb_ref, bg_ref,
             dh_ref, dg_ref, dz_ref, dyb_ref, dbg_ref):
        @pl.when(pl.program_id(0) == 0)
        def _():
            dbg_ref[...] = jnp.zeros_like(dbg_ref)

        dm = lax.dot_general(dx_ref[...], w_ref[...], (((1,), (1,)), ((), ())), preferred_element_type=f32)
        ybv = yb_ref[...].astype(f32)
        g0, g1, gelu, dgelu, hs, ya = _merge_parts(hf_ref[...], hb_ref[...], g_ref[...], z0_ref[...],
                                                   z1_ref[...], ybv, bg_ref[...])
        dya = dm * g0
        dh_ref[...] = (dya * gelu).astype(bf16)
        dg_ref[...] = (dya * hs * dgelu).astype(bf16)
        dyb_ref[...] = (dm * g1).astype(bf16)
        dz0 = (dm * ya) * (g0 * (1.0 - g0))
        dz1 = (dm * ybv) * (g1 * (1.0 - g1))
        dz = jnp.concatenate([dz0, dz1], axis=1)
        dz_ref[...] = dz.astype(bf16)
        dbg_ref[...] += jnp.sum(dz, axis=0, keepdims=True)

    row = pl.BlockSpec((tm, D), lambda i: (i, 0))
    return pl.pallas_call(
        body, name="outproj_bwd", grid=(S // tm,),
        in_specs=[row, pl.BlockSpec((D, D), lambda i: (0, 0)), row, row,
                  pl.BlockSpec((tm, D), lambda i: (i, C_G // D)),
                  pl.BlockSpec((tm, D), lambda i: (i, C_Z0 // D)),
                  pl.BlockSpec((tm, D), lambda i: (i, C_Z1 // D)),
                  row, pl.BlockSpec((1, 2 * D), lambda i: (0, 0))],
        out_specs=[row, row, pl.BlockSpec((tm, 2 * D), lambda i: (i, 0)), row,
                   pl.BlockSpec((1, 2 * D), lambda i: (0, 0))],
        out_shape=[jax.ShapeDtypeStruct((S, D), bf16), jax.ShapeDtypeStruct((S, D), bf16),
                   jax.ShapeDtypeStruct((S, 2 * D), bf16), jax.ShapeDtypeStruct((S, D), bf16),
                   jax.ShapeDtypeStruct((1, 2 * D), f32)],
        compiler_params=_cparams())(dx1b, w_out, hf, hb, proj, proj, proj, yb, bg)


def _block_diag_groups(w):
    w4 = w.reshape(LRU_GROUPS, 4, LRU_BLOCK, LRU_BLOCK)
    eye = jnp.eye(4, dtype=w.dtype)
    return jnp.einsum("ghij,hk->ghikj", w4, eye).reshape(LRU_GROUPS, LRU_GW, LRU_GW)


def _diag_blocks(dw):
    d5 = dw.reshape(LRU_GROUPS, 4, LRU_BLOCK, 4, LRU_BLOCK)
    return jnp.stack([d5[:, h, :, h, :] for h in range(4)], axis=1).reshape(LRU_HEADS, LRU_BLOCK, LRU_BLOCK)


def _local_step(x, tgt, small, env, before=lambda name: (), after=lambda name, got: None):
    S = x.shape[0]
    g1, g2, g3 = small["norm_mix_g"], small["norm_ffn_g"], small["norm_final_g"]
    bg, cw, cb = small["b_gate"], small["conv_w"], small["conv_b"]
    sink = small["attn_sink"]

    wg = jnp.concatenate([_block_diag_groups(small["lru_wa"][0]), _block_diag_groups(small["lru_wx"][0]),
                          _block_diag_groups(small["lru_wa"][1]), _block_diag_groups(small["lru_wx"][1])],
                         axis=2).astype(bf16)
    zeros5 = jnp.zeros((5, D), f32)
    lp = jnp.stack([jnp.concatenate([small["lru_lambda"][d:d + 1], small["lru_ba"][d:d + 1],
                                     small["lru_bx"][d:d + 1], zeros5], axis=0) for d in range(2)])

    def hosted(name, fn, *args, **kw):
        outs, got = fn(*args, comm=tuple(before(name)), **kw)
        after(name, got)
        return outs

    xn, proj = hosted("norm_inproj", _norm_matmul, x, g1, env["w_in_p"], "norm_inproj")
    uc = _conv_fwd(proj, cw, cb)
    (hf,), _ = _lru_fwd(uc, wg, lp, False)
    (hb,), _ = _lru_fwd(uc, wg, lp, True)
    yb, attn_stats = hosted("attn_fwd", _attn_fwd, proj, sink)
    merged, x1 = _merge_outproj(x, hf, hb, proj, yb, bg, env["w_out"])
    (xn2, gu), _ = _norm_matmul(x1, g2, env["w_fi"], "norm_ffn_in")
    ff, dx2, dx2b, loss, dg3, dgt, dup = _ffn_out_loss(gu, x1, env["w_fo"], g3, tgt)

    env["dw_fo"] = _mm_tn(ff, dx2b, "dw_ffn_out", tk=1408, tn=1024)
    dx1, dx1b, dg2 = hosted("ffn_in_bwd", _proj_bwd, [(dgt, 0, 0, D_FF), (dup, 0, 1, D_FF)], env["w_fi"],
                            x1, g2, dx2, "ffn_in_bwd")
    dw_gate = _mm_tn(xn2, dgt, "dw_ffn_in_gate", tk=1024, tn=1408, out_cols=2 * D_FF)
    env["dw_fi"] = _mm_tn(xn2, dup, "dw_ffn_in_up", tk=1024, tn=1408, into=dw_gate, col=D_FF // 1408)
    env["dw_out"] = _mm_tn(merged, dx1b, "dw_out", tk=1024, tn=1024)
    dh, dgl, dz, dyb, dbg = _outproj_bwd(dx1b, env["w_out"], hf, hb, proj, yb, bg)
    dq, dk2, dv2, dsink = hosted("attn_bwd", _attn_bwd, proj, sink, dyb, attn_stats)
    dkv = jnp.concatenate([dk2[BLK:BLK + S], dv2[BLK:BLK + S]], axis=1).astype(bf16)
    duc_f, dwg_f, dp_f = hosted("lru_bwd", _lru_bwd, uc, dh, hf, wg, lp, False)
    (duc_b, dwg_b, dp_b), _ = _lru_bwd(uc, dh, hb, wg, lp, True)
    env["grads_early"] = {
        "loss": loss[:, :1], "b_gate": dbg,
        "lru_lambda": jnp.concatenate([dp_f[0:1], dp_b[0:1]], axis=0),
        "lru_wa": jnp.stack([_diag_blocks(dwg_f[:, :, :LRU_GW]), _diag_blocks(dwg_b[:, :, :LRU_GW])]),
        "lru_ba": jnp.concatenate([dp_f[1:2], dp_b[1:2]], axis=0),
        "lru_wx": jnp.stack([_diag_blocks(dwg_f[:, :, LRU_GW:]), _diag_blocks(dwg_b[:, :, LRU_GW:])]),
        "lru_bx": jnp.concatenate([dp_f[2:3], dp_b[2:3]], axis=0),
        "attn_sink": dsink[:, :N_HEADS], "norm_ffn_g": dg2, "norm_final_g": dg3,
    }
    du, dcw, dcb = hosted("conv_bwd", _conv_bwd, duc_f, duc_b, proj, cw)
    dw_in = _mm_tn(xn, du, "dw_in_u", tk=1024, tn=1024, out_cols=IN_W)
    dw_in = _mm_tn(xn, dgl, "dw_in_g", tk=1024, tn=1024, into=dw_in, col=1)
    dw_in = _mm_tn(xn, dq, "dw_in_q", tk=1024, tn=1024, into=dw_in, col=2)
    dw_in = _mm_tn(xn, dkv, "dw_in_kv", tk=1024, tn=512, into=dw_in, col=3072 // 512)
    env["dw_in"] = _mm_tn(xn, dz, "dw_in_z", tk=1024, tn=512, into=dw_in, col=3584 // 512)
    col_pieces = [(du, 0, C_U // D, D), (dgl, 0, C_G // D, D), (dq, 0, C_Q // D, D), (dz, 0, C_Z0 // D, D),
                  (dz, 1, C_Z1 // D, D), (dkv, 0, C_K // 512, 512)]
    dx, _, dg1 = hosted("inproj_bwd", _proj_bwd, col_pieces, env["w_in_p"], x, g1, dx1, "inproj_bwd")

    grads = dict(env["grads_early"], norm_mix_g=dg1, conv_w=dcw, conv_b=dcb)
    return dx, grads


def _adamw(gparts, w, m, v, name, tr=256):
    n, rows, cols = gparts.shape
    tr = _div_tile(rows, tr)
    c1 = 1.0 - ADAM_B1 ** ADAM_STEP
    c2 = 1.0 - ADAM_B2 ** ADAM_STEP

    def body(g_ref, w_ref, m_ref, v_ref, go_ref, d_ref, mo_ref, vo_ref):
        g = g_ref[0].astype(f32)
        for j in range(1, n):
            g = g + g_ref[j].astype(f32)
        mn = ADAM_B1 * m_ref[0] + (1.0 - ADAM_B1) * g
        vn = ADAM_B2 * v_ref[0] + (1.0 - ADAM_B2) * (g * g)
        m_hat = mn / c1
        v_hat = vn / c2
        go_ref[0] = g
        d_ref[0] = -ADAM_LR * (m_hat / (jnp.sqrt(v_hat) + ADAM_EPS) + ADAM_WD * w_ref[0])
        mo_ref[0] = mn
        vo_ref[0] = vn

    blk = pl.BlockSpec((1, tr, cols), lambda i: (0, i, 0))
    shp = jax.ShapeDtypeStruct((1, rows, cols), f32)
    return pl.pallas_call(
        body, name=name, grid=(rows // tr,),
        in_specs=[pl.BlockSpec((n, tr, cols), lambda i: (0, i, 0)), blk, blk, blk],
        out_specs=[blk, blk, blk, blk], out_shape=[shp, shp, shp, shp],
        compiler_params=_cparams())(gparts, w, m, v)


def _sum_parts(parts, name):
    n, rows, cols = parts.shape

    def body(p_ref, o_ref):
        acc = p_ref[0].astype(f32)
        for j in range(1, n):
            acc = acc + p_ref[j].astype(f32)
        o_ref[...] = acc

    return pl.pallas_call(
        body, name=name, out_shape=jax.ShapeDtypeStruct((rows, cols), f32),
        compiler_params=_cparams())(parts)


def _pack_rows(arrs, dtype=f32):
    rows, spans, at = [], [], 0
    for a in arrs:
        flat = a.reshape(-1).astype(dtype)
        nr = -(-flat.shape[0] // 1024)
        rows.append(jnp.pad(flat, (0, nr * 1024 - flat.shape[0])).reshape(nr, 1024))
        spans.append((at, nr))
        at += nr
    pad = (-at) % 16
    if pad:
        rows.append(jnp.zeros((pad, 1024), dtype))
    return jnp.concatenate(rows, axis=0), spans


def _unpack_rows(packed, spans, shapes):
    out = []
    for (at, nr), shp in zip(spans, shapes):
        n = math.prod(shp)
        out.append(packed[at:at + nr].reshape(-1)[:n].reshape(shp))
    return out


BIG = ("w_in", "w_out", "w_ffn_in", "w_ffn_out")
SMALL_REPL = ("norm_mix_g", "b_gate", "conv_b", "lru_wa", "lru_wx", "attn_sink", "norm_ffn_g", "norm_final_g")
SMALL_SHARD = ("conv_w", "lru_lambda", "lru_ba", "lru_bx")
ORDER = ("norm_mix_g", "w_in", "b_gate", "conv_w", "conv_b", "lru_lambda", "lru_wa", "lru_ba", "lru_wx",
         "lru_bx", "attn_sink", "w_out", "norm_ffn_g", "w_ffn_in", "w_ffn_out", "norm_final_g")
EARLY_F32 = ("loss", "b_gate", "lru_lambda", "lru_ba", "lru_bx", "attn_sink", "norm_ffn_g", "norm_final_g")
EARLY_BF16 = ("lru_wa", "lru_wx")
LATE = ("norm_mix_g", "conv_w", "conv_b")


def kernel(x, norm_mix_g, w_in, b_gate, conv_w, conv_b, lru_lambda, lru_wa, lru_ba, lru_wx, lru_bx, attn_sink, w_out, norm_ffn_g, w_ffn_in, w_ffn_out, norm_final_g, loss_target, m_norm_mix_g, m_w_in, m_b_gate, m_conv_w, m_conv_b, m_lru_lambda, m_lru_wa, m_lru_ba, m_lru_wx, m_lru_bx, m_attn_sink, m_w_out, m_norm_ffn_g, m_w_ffn_in, m_w_ffn_out, m_norm_final_g, v_norm_mix_g, v_w_in, v_b_gate, v_conv_w, v_conv_b, v_lru_lambda, v_lru_wa, v_lru_ba, v_lru_wx, v_lru_bx, v_attn_sink, v_w_out, v_norm_ffn_g, v_w_ffn_in, v_w_ffn_out, v_norm_final_g):
    w = dict(norm_mix_g=norm_mix_g, w_in=w_in, b_gate=b_gate, conv_w=conv_w, conv_b=conv_b, lru_lambda=lru_lambda,
             lru_wa=lru_wa, lru_ba=lru_ba, lru_wx=lru_wx, lru_bx=lru_bx, attn_sink=attn_sink, w_out=w_out,
             norm_ffn_g=norm_ffn_g, w_ffn_in=w_ffn_in, w_ffn_out=w_ffn_out, norm_final_g=norm_final_g)
    m = dict(norm_mix_g=m_norm_mix_g, w_in=m_w_in, b_gate=m_b_gate, conv_w=m_conv_w, conv_b=m_conv_b,
             lru_lambda=m_lru_lambda, lru_wa=m_lru_wa, lru_ba=m_lru_ba, lru_wx=m_lru_wx, lru_bx=m_lru_bx,
             attn_sink=m_attn_sink, w_out=m_w_out, norm_ffn_g=m_norm_ffn_g, w_ffn_in=m_w_ffn_in,
             w_ffn_out=m_w_ffn_out, norm_final_g=m_norm_final_g)
    v = dict(norm_mix_g=v_norm_mix_g, w_in=v_w_in, b_gate=v_b_gate, conv_w=v_conv_w, conv_b=v_conv_b,
             lru_lambda=v_lru_lambda, lru_wa=v_lru_wa, lru_ba=v_lru_ba, lru_wx=v_lru_wx, lru_bx=v_lru_bx,
             attn_sink=v_attn_sink, w_out=v_w_out, norm_ffn_g=v_norm_ffn_g, w_ffn_in=v_w_ffn_in,
             w_ffn_out=v_w_ffn_out, norm_final_g=v_norm_final_g)
    me = 4 * lax.axis_index("x") + 2 * lax.axis_index("y") + lax.axis_index("c")

    def cols_full(got):
        return jnp.swapaxes(got, 0, 1).reshape(got.shape[1], -1)

    def cols_parts(g):
        return jnp.swapaxes(g.reshape(g.shape[0], N_DEV, -1), 0, 1)

    def rows_parts(g):
        return g.reshape(N_DEV, -1, g.shape[1])

    shard_rows = jnp.concatenate([w[n][0] for n in SMALL_SHARD], axis=0)
    got_w_in, got_rows = _exchange([(w_in[0].astype(bf16), False), (shard_rows, False)], "gather_w_in")
    full_rows = cols_full(got_rows)
    small = {n: w[n] for n in ("norm_mix_g", "b_gate", "conv_b", "attn_sink", "norm_ffn_g")}
    small["lru_wa"], small["lru_wx"] = lru_wa[0], lru_wx[0]
    small["norm_final_g"] = norm_final_g.reshape(1, D)
    small["conv_w"], small["lru_lambda"] = full_rows[0:4], full_rows[4:6]
    small["lru_ba"], small["lru_bx"] = full_rows[6:8], full_rows[8:10]

    env = {"w_in_p": _perm_cols(cols_full(got_w_in))}
    recv = {}

    def before(name):
        if name == "norm_inproj":
            return [(w_out[0].astype(bf16), False), (w_ffn_out[0].astype(bf16), False)]
        if name == "attn_fwd":
            return [(w_ffn_in[0].astype(bf16), False)]
        if name == "ffn_in_bwd":
            return [(rows_parts(env["dw_fo"]).astype(bf16), True)]
        if name == "attn_bwd":
            return [(rows_parts(env["dw_out"]).astype(bf16), True)]
        if name == "lru_bwd":
            return [(cols_parts(env["dw_fi"]).astype(bf16), True)]
        if name == "conv_bwd":
            ge = env["grads_early"]
            p32, env["early_f32_spans"] = _pack_rows([ge[n] for n in EARLY_F32])
            p16, env["early_bf16_spans"] = _pack_rows([ge[n] for n in EARLY_BF16], bf16)
            return [(p32, False), (p16, False)]
        if name == "inproj_bwd":
            return [(cols_parts(env["dw_in"]).astype(bf16), True)]
        return []

    def after(name, got):
        if name == "norm_inproj":
            env["w_out"], env["w_fo"] = got[0].reshape(D, D), got[1].reshape(D_FF, D)
        elif name == "attn_fwd":
            env["w_fi"] = cols_full(got[0])
        elif name == "ffn_in_bwd":
            recv["w_ffn_out"] = got[0]
        elif name == "attn_bwd":
            recv["w_out"] = got[0]
        elif name == "lru_bwd":
            recv["w_ffn_in"] = got[0]
        elif name == "conv_bwd":
            recv["early_f32"], recv["early_bf16"] = got
        elif name == "inproj_bwd":
            recv["w_in"] = got[0]

    grad_x, grads = _local_step(x[0], loss_target[0], small, env, before, after)

    outs = {}
    for name in BIG:
        outs[name] = _adamw(recv[name], w[name], m[name], v[name], "adamw_" + name)

    small_names = SMALL_REPL + SMALL_SHARD
    late_packed, late_spans = _pack_rows([grads[n] for n in LATE])
    (got_late,) = _exchange([(late_packed, False)], "gather_late_grads")
    summed = {}
    for names, got, spans, tag in ((EARLY_F32, recv["early_f32"], env["early_f32_spans"], "early_f32"),
                                   (EARLY_BF16, recv["early_bf16"], env["early_bf16_spans"], "early_bf16"),
                                   (LATE, got_late, late_spans, "late")):
        total = _sum_parts(got, "sum_small_" + tag)
        summed.update(zip(names, _unpack_rows(total, spans, [grads[n].shape for n in names])))
    loss = summed["loss"].reshape(())
    gsm = {n: summed[n].reshape(w[n].shape) for n in SMALL_REPL}
    for n in SMALL_SHARD:
        full = summed[n]
        gsm[n] = lax.dynamic_slice_in_dim(full, me * 128, 128, axis=1).reshape(w[n].shape)
    pk = lambda dct: _pack_rows([dct[n] for n in small_names])[0]
    gp, sp = _pack_rows([gsm[n] for n in small_names])
    res = _adamw(gp[None], pk(w)[None], pk(m)[None], pk(v)[None], "adamw_small")
    sshapes = [w[n].shape for n in small_names]
    for idx, t in enumerate(res):
        for n, a in zip(small_names, _unpack_rows(t[0], sp, sshapes)):
            outs.setdefault(n, [None] * 4)[idx] = a

    result = [loss, grad_x[None]]
    for idx in range(4):
        result += [outs[n][idx] for n in ORDER]
    return tuple(result)
```

```python
import functools
import math

import jax
import jax.numpy as jnp
from jax import lax
from jax.experimental import pallas as pl
from jax.experimental.pallas import tpu as pltpu

f32 = jnp.float32
bf16 = jnp.bfloat16

D = 1024
D_FF = 2816
IN_W = 5632
N_HEADS = 16
N_KV = 4
HEAD_DIM = 64
WINDOW = 128
BLK = 128
LRU_HEADS = 16
LRU_BLOCK = 64
LRU_GROUPS = 4
LRU_GW = 256
LRU_CHUNK = 128
LRU_ROWS = 1024
RGLRU_C = 8.0
EPS = 1e-6
NEG_INF = -1e30
N_DEV = 8

ADAM_LR = 0.001
ADAM_B1 = 0.9
ADAM_B2 = 0.999
ADAM_EPS = 1e-08
ADAM_WD = 0.01
ADAM_STEP = 10

VMEM_MB = 56

C_U, C_G, C_Q, C_Z0, C_Z1, C_K, C_V = 0, 1024, 2048, 3072, 4096, 5120, 5376


def _cparams(vmem_mb=VMEM_MB):
    return pltpu.CompilerParams(vmem_limit_bytes=vmem_mb << 20)


def _div_tile(n, pref):
    if n <= pref:
        return n
    return max(t for t in range(8, pref + 1, 8) if n % t == 0)


def _sigmoid(x):
    return 0.5 * jnp.tanh(0.5 * x) + 0.5


def _log1p(x):
    u = 1.0 + x
    d = u - 1.0
    return jnp.where(d == 0.0, x, jnp.log(u) * (x / jnp.where(d == 0.0, 1.0, d)))


def _softplus(x):
    return jnp.maximum(x, 0.0) + _log1p(jnp.exp(-jnp.abs(x)))


def _gelu_and_grad(x):
    c = math.sqrt(2.0 / math.pi)
    inner = c * (x + 0.044715 * (x * x * x))
    t = jnp.tanh(inner)
    gelu = 0.5 * x * (1.0 + t)
    dinner = c * (1.0 + 3 * 0.044715 * (x * x))
    dgelu = 0.5 * (1.0 + t) + 0.5 * x * (1.0 - t * t) * dinner
    return gelu, dgelu


def _rms_bwd(dn, xv, g):
    r = lax.rsqrt(jnp.mean(xv * xv, axis=-1, keepdims=True) + EPS)
    xh = xv * r
    dxh = dn * g
    dx = r * (dxh - xh * jnp.mean(dxh * xh, axis=-1, keepdims=True))
    return dx, dn * xh


ANY_SPEC = pl.BlockSpec(memory_space=pl.ANY)


def _comm_out_shape(src, scatter):
    return jax.ShapeDtypeStruct((N_DEV, *(src.shape[1:] if scatter else src.shape)), src.dtype)


def _comm_sems():
    return [pltpu.SemaphoreType.DMA((N_DEV - 1,)), pltpu.SemaphoreType.DMA((N_DEV - 1,)), pltpu.SemaphoreType.DMA]


def _scatter_descs(src_ref, out_ref, send_sems, recv_sems, local_sem):
    x, y, c = lax.axis_index("x"), lax.axis_index("y"), lax.axis_index("c")
    me = 4 * x + 2 * y + c
    descs = [pltpu.make_async_copy(src_ref.at[me], out_ref.at[me], local_sem)]
    for k in range(1, N_DEV):
        px, py, pc = x ^ (k >> 2), y ^ ((k >> 1) & 1), c ^ (k & 1)
        descs.append(pltpu.make_async_remote_copy(
            src_ref=src_ref.at[4 * px + 2 * py + pc], dst_ref=out_ref.at[me],
            send_sem=send_sems.at[k - 1], recv_sem=recv_sems.at[k - 1],
            device_id=(px, py, pc), device_id_type=pl.DeviceIdType.MESH))
    return descs


def _gather_copies(src_ref, out_ref, send_sems, recv_sems, local_sem, starting):
    x, y, c = lax.axis_index("x"), lax.axis_index("y"), lax.axis_index("c")
    me, sibling = (x, y, c), (x, y, 1 - c)
    chips = [(1 - x, y), (x, 1 - y), (1 - x, 1 - y)]

    def slot(px, py, pc):
        return out_ref.at[4 * px + 2 * py + pc]

    def copy(k, block, to, src=None):
        return pltpu.make_async_remote_copy(
            src_ref=slot(*block) if src is None else src, dst_ref=slot(*block),
            send_sem=send_sems.at[k], recv_sem=recv_sems.at[k], device_id=to, device_id_type=pl.DeviceIdType.MESH)

    local = pltpu.make_async_copy(src_ref, slot(*me), local_sem)
    first = [copy(0, me, sibling, src=src_ref)] + [copy(1 + j, me, (*chip, c), src=src_ref)
                                                    for j, chip in enumerate(chips)]
    if starting:
        return local, first
    passed = [copy(4 + j, (*chip, c), sibling) for j, chip in enumerate(chips)]
    landed = [copy(1 + j, (*chip, c), me) for j, chip in enumerate(chips)]
    later = [copy(0, sibling, me)] + [copy(4 + j, (*chip, 1 - c), me) for j, chip in enumerate(chips)]
    return local, first, passed, landed, later


def _comm_start(src_ref, out_ref, sems, scatter):
    if scatter:
        for d in _scatter_descs(src_ref, out_ref, *sems):
            d.start()
    else:
        local, first = _gather_copies(src_ref, out_ref, *sems, starting=True)
        local.start()
        for cp in first:
            cp.start()


def _comm_pass_on(src_ref, out_ref, sems, scatter):
    if not scatter:
        _, _, passed, landed, _ = _gather_copies(src_ref, out_ref, *sems, starting=False)
        for arrived, onward in zip(landed, passed):
            arrived.wait_recv()
            onward.start()


def _comm_finish(src_ref, out_ref, sems, scatter):
    if scatter:
        for d in _scatter_descs(src_ref, out_ref, *sems):
            d.wait()
    else:
        local, first, passed, _, later = _gather_copies(src_ref, out_ref, *sems, starting=False)
        for cp in later:
            cp.wait_recv()
        for cp in first + passed:
            cp.wait_send()
        local.wait()


def _exchange(comm, name):
    nc = len(comm)

    def body(*refs):
        srcs, outs, sems = refs[:nc], refs[nc:2 * nc], refs[2 * nc:]
        for stage in (_comm_start, _comm_pass_on, _comm_finish):
            for i in range(nc):
                stage(srcs[i], outs[i], sems[3 * i:3 * i + 3], comm[i][1])

    return pl.pallas_call(
        body, name=name, in_specs=[ANY_SPEC] * nc, out_specs=[ANY_SPEC] * nc,
        out_shape=[_comm_out_shape(*c) for c in comm],
        scratch_shapes=[s for _ in comm for s in _comm_sems()],
    )(*[c[0] for c in comm])


def _hosted_call(body, *, name, grid, in_specs, out_specs, out_shape, args, scratch_shapes=(), comm=()):
    nin, nout, nscr, nc = len(in_specs), len(out_specs), len(scratch_shapes), len(comm)
    steps = math.prod(grid)

    def wrapped(*refs):
        ins = refs[:nin]
        csrc = refs[nin:nin + nc]
        outs = refs[nin + nc:nin + nc + nout]
        cout = refs[nin + nc + nout:nin + 2 * nc + nout]
        scr = refs[nin + 2 * nc + nout:]
        sems = scr[nscr:]

        def at(step, stage):
            lin = 0
            for a in range(len(grid)):
                lin = lin * grid[a] + pl.program_id(a)

            @pl.when(lin == step)
            def _():
                for i in range(nc):
                    stage(csrc[i], cout[i], sems[3 * i:3 * i + 3], comm[i][1])

        if nc:
            at(0, _comm_start)

        body(*ins, *outs, *scr[:nscr])

        if nc:
            at((3 * (steps - 1)) // 4, _comm_pass_on)
            at(steps - 1, _comm_finish)

    res = pl.pallas_call(
        wrapped, name=name, grid=grid,
        in_specs=[*in_specs, *[ANY_SPEC] * nc], out_specs=[*out_specs, *[ANY_SPEC] * nc],
        out_shape=[*out_shape, *[_comm_out_shape(*c) for c in comm]],
        scratch_shapes=[*scratch_shapes, *[s for _ in comm for s in _comm_sems()]],
        compiler_params=_cparams())(*args, *[c[0] for c in comm])
    return res[:nout], res[nout:]


def _norm_matmul(x, g, wt, name, tm=1024, tn=1408, row_block=lambda j: j, comm=()):
    S, dm = x.shape
    n = wt.shape[0]
    tm = min(tm, S)

    def body(x_ref, g_ref, w_ref, xn_ref, o_ref):
        @pl.when(pl.program_id(1) == 0)
        def _():
            xv = x_ref[...]
            r = lax.rsqrt(jnp.mean(xv * xv, axis=-1, keepdims=True) + EPS)
            xn_ref[...] = ((xv * r) * g_ref[...]).astype(bf16)

        o_ref[...] = lax.dot_general(xn_ref[...], w_ref[...], (((1,), (1,)), ((), ())),
                                     preferred_element_type=f32).astype(bf16)

    return _hosted_call(
        body, name=name, grid=(S // tm, n // tn),
        in_specs=[pl.BlockSpec((tm, dm), lambda i, j: (i, 0)),
                  pl.BlockSpec((1, dm), lambda i, j: (0, 0)),
                  pl.BlockSpec((tn, dm), lambda i, j: (row_block(j), 0))],
        out_specs=[pl.BlockSpec((tm, dm), lambda i, j: (i, 0)),
                   pl.BlockSpec((tm, tn), lambda i, j: (i, j))],
        out_shape=[jax.ShapeDtypeStruct((S, dm), bf16), jax.ShapeDtypeStruct((S, n), bf16)],
        args=(x, g, wt), comm=comm)


def _mm_tn(a, b, name, tk, tn, tmc=2048, into=None, row=0, out_rows=None):
    m, ka = a.shape
    n = b.shape[1]
    tmc = min(tmc, m)
    nk = m // tmc

    def body(a_ref, b_ref, *rest):
        o_ref, acc_ref = rest[-2:]
        k = pl.program_id(2)
        part = lax.dot_general(a_ref[...], b_ref[...], (((0,), (0,)), ((), ())), preferred_element_type=f32)

        @pl.when(k == 0)
        def _():
            acc_ref[...] = part

        @pl.when(k > 0)
        def _():
            acc_ref[...] += part

        @pl.when(k == nk - 1)
        def _():
            o_ref[...] = acc_ref[...].astype(bf16)

    in_specs = [pl.BlockSpec((tmc, tk), lambda i, j, k: (k, i)), pl.BlockSpec((tmc, tn), lambda i, j, k: (k, j))]
    if into is None:
        return pl.pallas_call(
            body, name=name, grid=(ka // tk, n // tn, nk), in_specs=in_specs,
            out_specs=pl.BlockSpec((tk, tn), lambda i, j, k: (i + row, j)),
            out_shape=jax.ShapeDtypeStruct((out_rows or ka, n), bf16),
            scratch_shapes=[pltpu.VMEM((tk, tn), f32)],
            compiler_params=_cparams())(a, b)
    return pl.pallas_call(
        body, name=name, grid=(ka // tk, n // tn, nk), in_specs=[*in_specs, ANY_SPEC],
        out_specs=pl.BlockSpec((tk, tn), lambda i, j, k: (i + row, j)),
        out_shape=jax.ShapeDtypeStruct(into.shape, into.dtype),
        scratch_shapes=[pltpu.VMEM((tk, tn), f32)], input_output_aliases={2: 0},
        compiler_params=_cparams())(a, b, into)


HALO = 16


def _rows_at(ext, o, tc):
    if o == 0:
        return ext[HALO:HALO + tc]
    return pltpu.roll(ext, (-o) % ext.shape[0], 0)[HALO:HALO + tc]


def _halo_specs(tc, S, width, col):
    per = tc // HALO
    last = S // HALO - 1
    return (pl.BlockSpec((tc, width), lambda i: (i, col)),
            pl.BlockSpec((HALO, width), lambda i: (jnp.maximum(i * per - 1, 0), col)),
            pl.BlockSpec((HALO, width), lambda i: (jnp.minimum((i + 1) * per, last), col)))


def _extended(cur_ref, prev_ref, next_ref, i, nsteps):
    prev = jnp.where(i > 0, prev_ref[...].astype(f32), 0.0)
    nxt = jnp.where(i < nsteps - 1, next_ref[...].astype(f32), 0.0)
    return jnp.concatenate([prev, cur_ref[...].astype(f32), nxt], axis=0)


def _conv_fwd(proj, cw, cb, tc=1024):
    S = proj.shape[0]
    tc = min(tc, S)
    nsteps = S // tc

    def body(cur_ref, prev_ref, next_ref, w_ref, b_ref, o_ref):
        ext = _extended(cur_ref, prev_ref, next_ref, pl.program_id(0), nsteps)
        acc = _rows_at(ext, -2, tc) * w_ref[0:1, :]
        for k in range(1, 4):
            acc = acc + _rows_at(ext, k - 2, tc) * w_ref[k:k + 1, :]
        o_ref[...] = acc + b_ref[...]

    return pl.pallas_call(
        body, name="conv_fwd", grid=(nsteps,),
        in_specs=[*_halo_specs(tc, S, D, 0),
                  pl.BlockSpec((4, D), lambda i: (0, 0)), pl.BlockSpec((1, D), lambda i: (0, 0))],
        out_specs=pl.BlockSpec((tc, D), lambda i: (i, 0)),
        out_shape=jax.ShapeDtypeStruct((S, D), f32),
        compiler_params=_cparams())(proj, proj, proj, cw, cb)


def _conv_bwd(duc_f, duc_b, proj, cw, tc=1024, comm=()):
    S = proj.shape[0]
    tc = min(tc, S)
    nsteps = S // tc

    def body(fc, fp, fn, bc, bp, bn, uc_, up, un, w_ref, du_ref, dw_ref, db_ref):
        i = pl.program_id(0)

        @pl.when(i == 0)
        def _():
            dw_ref[...] = jnp.zeros_like(dw_ref)
            db_ref[...] = jnp.zeros_like(db_ref)

        dext = _extended(fc, fp, fn, i, nsteps) + _extended(bc, bp, bn, i, nsteps)
        uext = _extended(uc_, up, un, i, nsteps)
        d = dext[HALO:HALO + tc]
        acc = _rows_at(dext, 2, tc) * w_ref[0:1, :]
        for k in range(1, 4):
            acc = acc + _rows_at(dext, 2 - k, tc) * w_ref[k:k + 1, :]
        du_ref[...] = acc.astype(bf16)
        wrow = lax.broadcasted_iota(jnp.int32, (4, D), 0)
        for k in range(4):
            dw_ref[...] += jnp.where(wrow == k, jnp.sum(d * _rows_at(uext, k - 2, tc), axis=0, keepdims=True), 0.0)
        db_ref[...] += jnp.sum(d, axis=0, keepdims=True)

    return _hosted_call(
        body, name="conv_bwd", grid=(nsteps,),
        in_specs=[*_halo_specs(tc, S, D, 0), *_halo_specs(tc, S, D, 0), *_halo_specs(tc, S, D, 0),
                  pl.BlockSpec((4, D), lambda i: (0, 0))],
        out_specs=[pl.BlockSpec((tc, D), lambda i: (i, 0)),
                   pl.BlockSpec((4, D), lambda i: (0, 0)), pl.BlockSpec((1, D), lambda i: (0, 0))],
        out_shape=[jax.ShapeDtypeStruct((S, D), bf16), jax.ShapeDtypeStruct((4, D), f32),
                   jax.ShapeDtypeStruct((1, D), f32)],
        args=(duc_f, duc_f, duc_f, duc_b, duc_b, duc_b, proj, proj, proj, cw), comm=comm)


def _scan_scratch():
    halves = [pltpu.VMEM((LRU_CHUNK, 128), f32) for _ in range(2 * (LRU_GW // 128))]
    return [*halves, pltpu.VMEM((LRU_CHUNK // 8, LRU_GW), f32), pltpu.VMEM((LRU_CHUNK // 8, LRU_GW), f32)]


def _log_scan(a, b, row, n, reverse, steps):
    for s in steps:
        shift = a.shape[0] - s if reverse else s
        keep = (row < n - s) if reverse else (row >= s)
        a_sh = pltpu.roll(a, shift, 0)
        b_sh = pltpu.roll(b, shift, 0)
        b = jnp.where(keep, a * b_sh + b, b)
        a = jnp.where(keep, a * a_sh, a)
    return a, b


def _scan_chunk(a, b, carry, reverse, *scratch):
    tc, w = a.shape
    ng = tc // 8
    nl = w // 128
    sa_refs, sb_refs, sc_ref, st_ref = scratch[:nl], scratch[nl:2 * nl], scratch[2 * nl], scratch[2 * nl + 1]
    sub = lax.broadcasted_iota(jnp.int32, (8, w), 0)
    ag, bg = [], []
    for k in range(ng):
        ak, bk = _log_scan(a[8 * k:8 * k + 8], b[8 * k:8 * k + 8], sub, 8, reverse, (1, 2, 4))
        ag.append(ak)
        bg.append(bk)
    a = jnp.concatenate(ag, axis=0)
    b = jnp.concatenate(bg, axis=0)
    edge = 0 if reverse else 7
    for i in range(nl):
        sa_refs[i][...] = a[:, 128 * i:128 * (i + 1)]
        sb_refs[i][...] = b[:, 128 * i:128 * (i + 1)]
    ta = jnp.concatenate([r[pl.ds(edge, ng, stride=8), :] for r in sa_refs], axis=1)
    tb = jnp.concatenate([r[pl.ds(edge, ng, stride=8), :] for r in sb_refs], axis=1)
    grow = lax.broadcasted_iota(jnp.int32, (ng, w), 0)
    ta, tb = _log_scan(ta, tb, grow, ng, reverse, [1 << i for i in range(ng.bit_length() - 1)])
    state = tb + ta * carry
    st_ref[...] = state
    if reverse:
        sc_ref[...] = jnp.where(grow == ng - 1, carry, pltpu.roll(state, ng - 1, 0))
    else:
        sc_ref[...] = jnp.where(grow == 0, carry, pltpu.roll(state, 1, 0))
    h = jnp.concatenate([bg[k] + ag[k] * sc_ref[k:k + 1, :] for k in range(ng)], axis=0)
    return h, (st_ref[0:1, :] if reverse else st_ref[ng - 1:ng, :])


def _lru_gates(uc, w, p_ref):
    pre = jnp.dot(uc.astype(bf16), w, preferred_element_type=f32)
    r = _sigmoid(pre[:, :LRU_GW] + p_ref[0, 1:2, :])
    gi = _sigmoid(pre[:, LRU_GW:] + p_ref[0, 2:3, :])
    sp = _softplus(-p_ref[0, 0:1, :])
    log_a = -RGLRU_C * r * sp
    a = jnp.exp(log_a)
    x = 2.0 * log_a
    series = -x * (1.0 + x * (0.5 + x * (1.0 / 6 + x * (1.0 / 24))))
    beta = jnp.sqrt(jnp.maximum(jnp.where(x > -0.0625, series, 1.0 - a * a), 0.0))
    return r, gi, sp, a, beta


def _lru_fwd(uc, wg, lp, reverse, comm=()):
    S = uc.shape[0]
    tc = LRU_CHUNK
    rows = min(LRU_ROWS, S)
    nsub = rows // tc
    nblk = S // rows
    d = 1 if reverse else 0

    def bidx(c):
        return nblk - 1 - c if reverse else c

    def body(uc_ref, w_ref, p_ref, h_ref, carry_ref, *scan_scratch):
        @pl.when(pl.program_id(1) == 0)
        def _():
            carry_ref[...] = jnp.zeros_like(carry_ref)

        carry = carry_ref[...]
        for j in (reversed(range(nsub)) if reverse else range(nsub)):
            sl = slice(j * tc, (j + 1) * tc)
            ucv = uc_ref[sl, :]
            _, gi, _, a, beta = _lru_gates(ucv, w_ref[0], p_ref)
            h, carry = _scan_chunk(a, beta * (gi * ucv), carry, reverse, *scan_scratch)
            h_ref[sl, :] = h.astype(bf16)
        carry_ref[...] = carry

    return _hosted_call(
        body, name="lru_fwd_rev" if reverse else "lru_fwd", grid=(LRU_GROUPS, nblk),
        in_specs=[pl.BlockSpec((rows, LRU_GW), lambda g, c: (bidx(c), g)),
                  pl.BlockSpec((1, LRU_GW, 2 * LRU_GW), lambda g, c: (g, 0, d)),
                  pl.BlockSpec((1, 8, LRU_GW), lambda g, c: (d, 0, g))],
        out_specs=[pl.BlockSpec((rows, LRU_GW), lambda g, c: (bidx(c), g))],
        out_shape=[jax.ShapeDtypeStruct((S, D), bf16)],
        scratch_shapes=[pltpu.VMEM((1, LRU_GW), f32), *_scan_scratch()],
        args=(uc, wg, lp), comm=comm)


def _lru_bwd(uc, dh, h, wg, lp, reverse, comm=()):
    S = uc.shape[0]
    tc = LRU_CHUNK
    rows = min(LRU_ROWS, S)
    nsub = rows // tc
    nblk = S // rows
    d = 1 if reverse else 0
    per = rows // HALO
    last8 = S // HALO - 1

    def bidx(c):
        return c if reverse else nblk - 1 - c

    def halo_idx(c):
        if reverse:
            return jnp.minimum((bidx(c) + 1) * per, last8)
        return jnp.maximum(bidx(c) * per - 1, 0)

    def body(uc_ref, dh_ref, h_ref, halo_ref, w_ref, p_ref, duc_ref, dw_ref, dp_ref, carry_ref, tmp_ref,
             *scan_scratch):
        c = pl.program_id(1)
        bi = bidx(c)

        @pl.when(c == 0)
        def _():
            carry_ref[...] = jnp.zeros_like(carry_ref)
            dw_ref[...] = jnp.zeros_like(dw_ref)
            dp_ref[...] = jnp.zeros_like(dp_ref)

        row = lax.broadcasted_iota(jnp.int32, (tc, LRU_GW), 0)
        carry = carry_ref[...]
        dw = jnp.zeros((LRU_GW, 2 * LRU_GW), f32)
        dsp = jnp.zeros((1, LRU_GW), f32)
        dba = jnp.zeros((1, LRU_GW), f32)
        dbx = jnp.zeros((1, LRU_GW), f32)
        for j in (range(nsub) if reverse else reversed(range(nsub))):
            sl = slice(j * tc, (j + 1) * tc)
            ucv = uc_ref[sl, :]
            ucb = ucv.astype(bf16)
            r, gi, sp, a, beta = _lru_gates(ucv, w_ref[0], p_ref)
            hv = h_ref[sl, :].astype(f32)
            dhv = dh_ref[sl, :].astype(f32)
            if reverse:
                alpha = jnp.where(row == 0, 1.0, pltpu.roll(a, 1, 0))
                gsc, _ = _scan_chunk(alpha, dhv, carry, False, *scan_scratch)
                if j < nsub - 1:
                    edge = h_ref[(j + 1) * tc:(j + 1) * tc + HALO, :].astype(f32)[0:1, :]
                else:
                    edge = jnp.where(bi < nblk - 1, halo_ref[...].astype(f32)[0:1, :], 0.0)
                h_nb = jnp.where(row == tc - 1, edge, pltpu.roll(hv, tc - 1, 0))
            else:
                alpha = jnp.where(row == tc - 1, 1.0, pltpu.roll(a, tc - 1, 0))
                gsc, _ = _scan_chunk(alpha, dhv, carry, True, *scan_scratch)
                if j > 0:
                    edge = h_ref[j * tc - HALO:j * tc, :].astype(f32)[HALO - 1:HALO, :]
                else:
                    edge = jnp.where(bi > 0, halo_ref[...].astype(f32)[HALO - 1:HALO, :], 0.0)
                h_nb = jnp.where(row == 0, edge, pltpu.roll(hv, 1, 0))
            tmp_ref[...] = a * gsc
            carry = tmp_ref[tc - 1:tc, :] if reverse else tmp_ref[0:1, :]

            da = gsc * h_nb
            dbeta = gsc * (gi * ucv)
            dl = da * a - dbeta * (a * a) / beta
            dr = dl * (-RGLRU_C * sp)
            dsp = dsp + jnp.sum(dl * (-RGLRU_C * r), axis=0, keepdims=True)
            dgi = gsc * beta * ucv
            dpre_r = dr * r * (1.0 - r)
            dpre_i = dgi * gi * (1.0 - gi)
            dba = dba + jnp.sum(dpre_r, axis=0, keepdims=True)
            dbx = dbx + jnp.sum(dpre_i, axis=0, keepdims=True)
            dpre = jnp.concatenate([dpre_r, dpre_i], axis=1).astype(bf16)
            back = lax.dot_general(dpre, w_ref[0], (((1,), (1,)), ((), ())), preferred_element_type=f32)
            duc_ref[sl, :] = (gsc * beta * gi + back).astype(bf16)
            dw = dw + lax.dot_general(ucb, dpre, (((0,), (0,)), ((), ())), preferred_element_type=f32)
        carry_ref[...] = carry
        dw_ref[0] += dw
        dlam = -dsp / (1.0 + jnp.exp(p_ref[0, 0:1, :]))
        prow = lax.broadcasted_iota(jnp.int32, (8, LRU_GW), 0)
        dp_ref[...] += (jnp.where(prow == 0, dlam, 0.0) + jnp.where(prow == 1, dba, 0.0)
                        + jnp.where(prow == 2, dbx, 0.0))

    chunk = pl.BlockSpec((rows, LRU_GW), lambda g, c: (bidx(c), g))
    return _hosted_call(
        body, name="lru_bwd_rev" if reverse else "lru_bwd", grid=(LRU_GROUPS, nblk),
        in_specs=[chunk, chunk, chunk,
                  pl.BlockSpec((HALO, LRU_GW), lambda g, c: (halo_idx(c), g)),
                  pl.BlockSpec((1, LRU_GW, 2 * LRU_GW), lambda g, c: (g, 0, d)),
                  pl.BlockSpec((1, 8, LRU_GW), lambda g, c: (d, 0, g))],
        out_specs=[chunk,
                   pl.BlockSpec((1, LRU_GW, 2 * LRU_GW), lambda g, c: (g, 0, 0)),
                   pl.BlockSpec((8, LRU_GW), lambda g, c: (0, g))],
        out_shape=[jax.ShapeDtypeStruct((S, D), bf16),
                   jax.ShapeDtypeStruct((LRU_GROUPS, LRU_GW, 2 * LRU_GW), f32),
                   jax.ShapeDtypeStruct((8, D), f32)],
        scratch_shapes=[pltpu.VMEM((1, LRU_GW), f32), pltpu.VMEM((tc, LRU_GW), f32), *_scan_scratch()],
        args=(uc, dh, h, h, wg, lp), comm=comm)


def _slope(h):
    return 2.0 ** (-8.0 * (h + 1.0) / N_HEADS)


def _kv_specs(nb, col):
    return [pl.BlockSpec((BLK, N_KV * HEAD_DIM), lambda n: (jnp.maximum(n - 1, 0), col)),
            pl.BlockSpec((BLK, N_KV * HEAD_DIM), lambda n: (n, col)),
            pl.BlockSpec((BLK, N_KV * HEAD_DIM), lambda n: (jnp.minimum(n + 1, nb - 1), col))]


def _dup_windows(r0, r1, r2):
    left = lax.broadcasted_iota(jnp.int32, (3 * BLK, 128), 1) < HEAD_DIM
    win = jnp.concatenate([r0[...], r1[...], r2[...]], axis=0)
    out = []
    for i in range(N_KV // 2):
        t = win[:, i * 128:(i + 1) * 128]
        r = pltpu.roll(t, HEAD_DIM, 1)
        out += [jnp.where(left, t, r).astype(bf16), jnp.where(left, r, t).astype(bf16)]
    return out


def _attn_bias_init(bias_ref):
    k_loc = lax.broadcasted_iota(jnp.int32, (3 * BLK, BLK), 0)
    q_loc = lax.broadcasted_iota(jnp.int32, (3 * BLK, BLK), 1)
    adist = jnp.abs(q_loc + BLK - k_loc)
    adf = adist.astype(f32)
    for e in range(3):
        ok = adist <= WINDOW
        if e == 0:
            ok = ok & (k_loc >= BLK)
        if e == 2:
            ok = ok & (k_loc < 2 * BLK)
        for kv in range(N_KV):
            bias_ref[e, kv] = jnp.concatenate(
                [jnp.where(ok, (-_slope(4 * kv + j)) * adf, NEG_INF) for j in range(4)], axis=1)


def _stack_heads(ref, kv, scale):
    left = lax.broadcasted_iota(jnp.int32, (BLK, 128), 1) < HEAD_DIM
    rows = []
    for pp in range(2):
        t = ref[:, (2 * kv + pp) * 128:(2 * kv + pp + 1) * 128]
        if scale != 1.0:
            t = t * scale
        zero = jnp.zeros_like(t)
        rows += [jnp.where(left, t, zero).astype(bf16), jnp.where(left, zero, t).astype(bf16)]
    return jnp.concatenate(rows, axis=0)


def _attn_softmax(qs, k2, bias, sink_ref, kv, stats=None):
    sink = jnp.concatenate([jnp.full((1, BLK), sink_ref[0, 4 * kv + j], f32) for j in range(4)], axis=1)
    s = lax.dot_general(k2, qs, (((1,), (1,)), ((), ())), preferred_element_type=f32) + bias
    m = jnp.maximum(jnp.max(s, axis=0, keepdims=True), sink) if stats is None else stats[0]
    p = jnp.exp(s - m)
    ps = jnp.exp(sink - m)
    inv = 1.0 / (jnp.sum(p, axis=0, keepdims=True) + ps) if stats is None else stats[1]
    return p, ps, m, inv


def _pair_tiles(t):
    return [jnp.concatenate([t[:HEAD_DIM, 256 * pp:256 * pp + 128],
                             t[HEAD_DIM:, 256 * pp + 128:256 * pp + 256]], axis=0).T for pp in range(2)]


def _attn_fwd(proj, sink, comm=()):
    S = proj.shape[0]
    nb = S // BLK
    assert nb >= 2

    def body(q_ref, k0, k1, k2_, v0, v1, v2_, sink_ref, o_ref, st_ref, bias_ref):
        n = pl.program_id(0)

        @pl.when(n == 0)
        def _():
            _attn_bias_init(bias_ref)

        e = jnp.where(n == 0, 0, jnp.where(n == nb - 1, 2, 1))
        kk = _dup_windows(k0, k1, k2_)
        vv = _dup_windows(v0, v1, v2_)
        tiles = []
        for kv in range(N_KV):
            qs = _stack_heads(q_ref, kv, HEAD_DIM ** -0.5)
            p, _, m, inv = _attn_softmax(qs, kk[kv], bias_ref[e, kv], sink_ref, kv)
            st_ref[0, kv:kv + 1, :] = m
            st_ref[0, N_KV + kv:N_KV + kv + 1, :] = inv
            ot = lax.dot_general(vv[kv], p.astype(bf16), (((0,), (0,)), ((), ())), preferred_element_type=f32)
            tiles += _pair_tiles(ot * inv)
        o_ref[...] = jnp.concatenate(tiles, axis=1).astype(bf16)

    return _hosted_call(
        body, name="attn_fwd", grid=(nb,),
        in_specs=[pl.BlockSpec((BLK, D), lambda n: (n, C_Q // D)),
                  *_kv_specs(nb, C_K // (N_KV * HEAD_DIM)), *_kv_specs(nb, C_V // (N_KV * HEAD_DIM)),
                  pl.BlockSpec(memory_space=pltpu.SMEM)],
        out_specs=[pl.BlockSpec((BLK, D), lambda n: (n, 0)), pl.BlockSpec((1, 2 * N_KV, 4 * BLK), lambda n: (n, 0, 0))],
        out_shape=[jax.ShapeDtypeStruct((S, D), bf16), jax.ShapeDtypeStruct((nb, 2 * N_KV, 4 * BLK), f32)],
        scratch_shapes=[pltpu.VMEM((3, N_KV, 3 * BLK, 4 * BLK), f32)],
        args=(proj, proj, proj, proj, proj, proj, proj, sink), comm=comm)


def _attn_bwd(proj, sink, dyb, stats, comm=()):
    S = proj.shape[0]
    nb = S // BLK
    assert nb >= 2

    def body(q_ref, k0, k1, k2_, v0, v1, v2_, sink_ref, do_ref, st_ref, dq_ref, dk_out, dv_out, ds_ref,
             bias_ref, dk_ref, dv_ref, dsk_ref):
        n = pl.program_id(0)

        @pl.when(n == 0)
        def _():
            _attn_bias_init(bias_ref)
            dk_ref[...] = jnp.zeros_like(dk_ref)
            dv_ref[...] = jnp.zeros_like(dv_ref)
            dsk_ref[...] = jnp.zeros_like(dsk_ref)

        e = jnp.where(n == 0, 0, jnp.where(n == nb - 1, 2, 1))
        kk = _dup_windows(k0, k1, k2_)
        vv = _dup_windows(v0, v1, v2_)
        left3 = lax.broadcasted_iota(jnp.int32, (3 * BLK, 128), 1) < HEAD_DIM
        start = pl.multiple_of(n * BLK, BLK)
        dq_tiles, dks, dvs = [], [], []
        for kv in range(N_KV):
            qs = _stack_heads(q_ref, kv, HEAD_DIM ** -0.5)
            dos = _stack_heads(do_ref, kv, 1.0)
            stats = (st_ref[0, kv:kv + 1, :], st_ref[0, N_KV + kv:N_KV + kv + 1, :])
            p, ps, _, inv = _attn_softmax(qs, kk[kv], bias_ref[e, kv], sink_ref, kv, stats)
            pn = p * inv
            dp = lax.dot_general(vv[kv], dos, (((1,), (1,)), ((), ())), preferred_element_type=f32)
            delta = jnp.sum(pn * dp, axis=0, keepdims=True)
            dsc = (pn * (dp - delta)).astype(bf16)
            dsk_ref[kv:kv + 1, :] += delta * (ps * inv)
            dqt = lax.dot_general(kk[kv], dsc, (((0,), (0,)), ((), ())), preferred_element_type=f32)
            dq_tiles += _pair_tiles(dqt * (HEAD_DIM ** -0.5))
            dk = jnp.dot(dsc, qs, preferred_element_type=f32)
            dv = jnp.dot(pn.astype(bf16), dos, preferred_element_type=f32)
            dks.append(dk + pltpu.roll(dk, HEAD_DIM, 1))
            dvs.append(dv + pltpu.roll(dv, HEAD_DIM, 1))
        for jp in range(N_KV // 2):
            cols = slice(jp * 128, (jp + 1) * 128)
            dk_ref[pl.ds(start, 3 * BLK), cols] += jnp.where(left3, dks[2 * jp], dks[2 * jp + 1])
            dv_ref[pl.ds(start, 3 * BLK), cols] += jnp.where(left3, dvs[2 * jp], dvs[2 * jp + 1])
        dq_ref[...] = jnp.concatenate(dq_tiles, axis=1).astype(bf16)

        @pl.when(n == nb - 1)
        def _():
            pltpu.sync_copy(dk_ref, dk_out)
            pltpu.sync_copy(dv_ref, dv_out)
            lane = lax.broadcasted_iota(jnp.int32, (1, 128), 1)
            dsink = jnp.zeros((1, 128), f32)
            for h in range(N_HEADS):
                part = dsk_ref[h // 4:h // 4 + 1, (h % 4) * BLK:(h % 4 + 1) * BLK]
                dsink = dsink + jnp.where(lane == h, -jnp.sum(part), 0.0)
            ds_ref[...] = dsink

    acc = jax.ShapeDtypeStruct((S + 2 * BLK, N_KV * HEAD_DIM), f32)
    return _hosted_call(
        body, name="attn_bwd", grid=(nb,),
        in_specs=[pl.BlockSpec((BLK, D), lambda n: (n, C_Q // D)),
                  *_kv_specs(nb, C_K // (N_KV * HEAD_DIM)), *_kv_specs(nb, C_V // (N_KV * HEAD_DIM)),
                  pl.BlockSpec(memory_space=pltpu.SMEM),
                  pl.BlockSpec((BLK, D), lambda n: (n, 0)),
                  pl.BlockSpec((1, 2 * N_KV, 4 * BLK), lambda n: (n, 0, 0))],
        out_specs=[pl.BlockSpec((BLK, D), lambda n: (n, 0)), ANY_SPEC, ANY_SPEC,
                   pl.BlockSpec((1, 128), lambda n: (0, 0))],
        out_shape=[jax.ShapeDtypeStruct((S, D), bf16), acc, acc, jax.ShapeDtypeStruct((1, 128), f32)],
        scratch_shapes=[pltpu.VMEM((3, N_KV, 3 * BLK, 4 * BLK), f32), pltpu.VMEM(acc.shape, f32),
                        pltpu.VMEM(acc.shape, f32), pltpu.VMEM((8, 4 * BLK), f32)],
        args=(proj, proj, proj, proj, proj, proj, proj, sink, dyb, stats), comm=comm)


def _merge_parts(hf, hb, g, z0, z1, yb, bg):
    g0 = _sigmoid(z0.astype(f32) + bg[:, :D])
    g1 = _sigmoid(z1.astype(f32) + bg[:, D:])
    gelu, dgelu = _gelu_and_grad(g.astype(f32))
    hs = hf.astype(f32) + hb.astype(f32)
    ya = hs * gelu
    return g0, g1, gelu, dgelu, hs, ya


def _merge_outproj(x, hf, hb, proj, yb, bg, w_out, tm=1024):
    S = x.shape[0]
    tm = min(tm, S)

    def body(x_ref, hf_ref, hb_ref, g_ref, z0_ref, z1_ref, yb_ref, bg_ref, w_ref, mg_ref, x1_ref):
        ybv = yb_ref[...].astype(f32)
        g0, g1, _, _, _, ya = _merge_parts(hf_ref[...], hb_ref[...], g_ref[...], z0_ref[...], z1_ref[...],
                                           ybv, bg_ref[...])
        mg = (g0 * ya + g1 * ybv).astype(bf16)
        mg_ref[...] = mg
        x1_ref[...] = x_ref[...] + jnp.dot(mg, w_ref[...], preferred_element_type=f32)

    row = pl.BlockSpec((tm, D), lambda i: (i, 0))
    return pl.pallas_call(
        body, name="merge_outproj", grid=(S // tm,),
        in_specs=[row, row, row,
                  pl.BlockSpec((tm, D), lambda i: (i, C_G // D)),
                  pl.BlockSpec((tm, D), lambda i: (i, C_Z0 // D)),
                  pl.BlockSpec((tm, D), lambda i: (i, C_Z1 // D)),
                  row, pl.BlockSpec((1, 2 * D), lambda i: (0, 0)), pl.BlockSpec((D, D), lambda i: (0, 0))],
        out_specs=[row, row],
        out_shape=[jax.ShapeDtypeStruct((S, D), bf16), jax.ShapeDtypeStruct((S, D), f32)],
        compiler_params=_cparams())(x, hf, hb, proj, proj, proj, yb, bg, w_out)


def _ffn_out_loss(gu, x1, w_fo, g3, tgt, tm=256):
    S = x1.shape[0]
    tm = min(tm, S)

    def body(gt_ref, up_ref, x1_ref, w_ref, g_ref, t_ref, ff_ref, dx_ref, dxb_ref, loss_ref, dg_ref,
             dgt_ref, dup_ref):
        @pl.when(pl.program_id(0) == 0)
        def _():
            loss_ref[...] = jnp.zeros_like(loss_ref)
            dg_ref[...] = jnp.zeros_like(dg_ref)

        gt = gt_ref[...].astype(f32)
        up = up_ref[...].astype(f32)
        sg = _sigmoid(gt)
        silu = gt * sg
        ff = (silu * up).astype(bf16)
        ff_ref[...] = ff
        x2 = x1_ref[...] + jnp.dot(ff, w_ref[...], preferred_element_type=f32)
        gv = g_ref[...]
        r = lax.rsqrt(jnp.mean(x2 * x2, axis=-1, keepdims=True) + EPS)
        xh = x2 * r
        diff = xh * gv - t_ref[...]
        loss_ref[...] += (0.5 / D) * jnp.sum(diff * diff)
        dy = diff * (1.0 / D)
        dg_ref[...] += jnp.sum(dy * xh, axis=0, keepdims=True)
        dxh = dy * gv
        dx = r * (dxh - xh * jnp.mean(dxh * xh, axis=-1, keepdims=True))
        dx_ref[...] = dx
        dxb = dx.astype(bf16)
        dxb_ref[...] = dxb
        dff = lax.dot_general(dxb, w_ref[...], (((1,), (1,)), ((), ())), preferred_element_type=f32)
        dup_ref[...] = (dff * silu).astype(bf16)
        dgt_ref[...] = ((dff * up) * (sg * (1.0 + gt * (1.0 - sg)))).astype(bf16)

    row = pl.BlockSpec((tm, D), lambda i: (i, 0))
    vec = pl.BlockSpec((1, D), lambda i: (0, 0))
    wide = pl.BlockSpec((tm, D_FF), lambda i: (i, 0))
    wide_shape = jax.ShapeDtypeStruct((S, D_FF), bf16)
    return pl.pallas_call(
        body, name="ffn_out_loss", grid=(S // tm,),
        in_specs=[wide, pl.BlockSpec((tm, D_FF), lambda i: (i, 1)),
                  row, pl.BlockSpec((D_FF, D), lambda i: (0, 0)), vec, row],
        out_specs=[wide, row, row, pl.BlockSpec((1, 128), lambda i: (0, 0)), vec, wide, wide],
        out_shape=[wide_shape, jax.ShapeDtypeStruct((S, D), f32), jax.ShapeDtypeStruct((S, D), bf16),
                   jax.ShapeDtypeStruct((1, 128), f32), jax.ShapeDtypeStruct((1, D), f32), wide_shape, wide_shape],
        compiler_params=_cparams())(gu, gu, x1, w_fo, g3, tgt)


def _proj_bwd(pieces, wt, xres, g, dres, name, tm=512, comm=()):
    S = xres.shape[0]
    tm = min(tm, S)
    np_ = len(pieces)

    def body(*refs):
        p_refs = refs[:np_]
        w_refs = refs[np_:2 * np_]
        x_ref, g_ref, dres_ref, dx_ref, dxb_ref, dg_ref = refs[2 * np_:]

        @pl.when(pl.program_id(0) == 0)
        def _():
            dg_ref[...] = jnp.zeros_like(dg_ref)

        dn = jnp.dot(p_refs[0][...], w_refs[0][...], preferred_element_type=f32)
        for pr, wr in zip(p_refs[1:], w_refs[1:]):
            dn = dn + jnp.dot(pr[...], wr[...], preferred_element_type=f32)
        dxn, dgc = _rms_bwd(dn, x_ref[...], g_ref[...])
        dx = dres_ref[...] + dxn
        dx_ref[...] = dx
        dxb_ref[...] = dx.astype(bf16)
        dg_ref[...] += jnp.sum(dgc, axis=0, keepdims=True)

    row = pl.BlockSpec((tm, D), lambda i: (i, 0))
    vec = pl.BlockSpec((1, D), lambda i: (0, 0))
    return _hosted_call(
        body, name=name, grid=(S // tm,),
        in_specs=[*[pl.BlockSpec((tm, wd), functools.partial(lambda i, cb: (i, cb), cb=acb))
                    for _, acb, _, wd in pieces],
                  *[pl.BlockSpec((wd, D), functools.partial(lambda i, rb: (rb, 0), rb=wrb))
                    for _, _, wrb, wd in pieces],
                  row, vec, row],
        out_specs=[row, row, vec],
        out_shape=[jax.ShapeDtypeStruct((S, D), f32), jax.ShapeDtypeStruct((S, D), bf16),
                   jax.ShapeDtypeStruct((1, D), f32)],
        args=(*[p[0] for p in pieces], *[wt] * np_, xres, g, dres), comm=comm)


def _outproj_bwd(dx1b, w_out, hf, hb, proj, yb, bg, tm=1024):
    S = dx1b.shape[0]
    tm = min(tm, S)

    def body(dx_ref, w_ref, hf_ref, hb_ref, g_ref, z0_ref, z1_ref, yb_ref, bg_ref,
             dh_ref, dg_ref, dz_ref, dyb_ref, dbg_ref):
        @pl.when(pl.program_id(0) == 0)
        def _():
            dbg_ref[...] = jnp.zeros_like(dbg_ref)

        dm = lax.dot_general(dx_ref[...], w_ref[...], (((1,), (1,)), ((), ())), preferred_element_type=f32)
        ybv = yb_ref[...].astype(f32)
        g0, g1, gelu, dgelu, hs, ya = _merge_parts(hf_ref[...], hb_ref[...], g_ref[...], z0_ref[...],
                                                   z1_ref[...], ybv, bg_ref[...])
        dya = dm * g0
        dh_ref[...] = (dya * gelu).astype(bf16)
        dg_ref[...] = (dya * hs * dgelu).astype(bf16)
        dyb_ref[...] = (dm * g1).astype(bf16)
        dz0 = (dm * ya) * (g0 * (1.0 - g0))
        dz1 = (dm * ybv) * (g1 * (1.0 - g1))
        dz = jnp.concatenate([dz0, dz1], axis=1)
        dz_ref[...] = dz.astype(bf16)
        dbg_ref[...] += jnp.sum(dz, axis=0, keepdims=True)

    row = pl.BlockSpec((tm, D), lambda i: (i, 0))
    return pl.pallas_call(
        body, name="outproj_bwd", grid=(S // tm,),
        in_specs=[row, pl.BlockSpec((D, D), lambda i: (0, 0)), row, row,
                  pl.BlockSpec((tm, D), lambda i: (i, C_G // D)),
                  pl.BlockSpec((tm, D), lambda i: (i, C_Z0 // D)),
                  pl.BlockSpec((tm, D), lambda i: (i, C_Z1 // D)),
                  row, pl.BlockSpec((1, 2 * D), lambda i: (0, 0))],
        out_specs=[row, row, pl.BlockSpec((tm, 2 * D), lambda i: (i, 0)), row,
                   pl.BlockSpec((1, 2 * D), lambda i: (0, 0))],
        out_shape=[jax.ShapeDtypeStruct((S, D), bf16), jax.ShapeDtypeStruct((S, D), bf16),
                   jax.ShapeDtypeStruct((S, 2 * D), bf16), jax.ShapeDtypeStruct((S, D), bf16),
                   jax.ShapeDtypeStruct((1, 2 * D), f32)],
        compiler_params=_cparams())(dx1b, w_out, hf, hb, proj, proj, proj, yb, bg)


def _block_diag_groups(w):
    w4 = w.reshape(LRU_GROUPS, 4, LRU_BLOCK, LRU_BLOCK)
    eye = jnp.eye(4, dtype=w.dtype)
    return jnp.einsum("ghij,hk->ghikj", w4, eye).reshape(LRU_GROUPS, LRU_GW, LRU_GW)


def _diag_blocks(dw):
    d5 = dw.reshape(LRU_GROUPS, 4, LRU_BLOCK, 4, LRU_BLOCK)
    return jnp.stack([d5[:, h, :, h, :] for h in range(4)], axis=1).reshape(LRU_HEADS, LRU_BLOCK, LRU_BLOCK)


def _local_step(x, tgt, small, env, before=lambda name: (), after=lambda name, got: None):
    S = x.shape[0]
    g1, g2, g3 = small["norm_mix_g"], small["norm_ffn_g"], small["norm_final_g"]
    bg, cw, cb = small["b_gate"], small["conv_w"], small["conv_b"]
    sink = small["attn_sink"]

    wg = jnp.concatenate([_block_diag_groups(small["lru_wa"][0]), _block_diag_groups(small["lru_wx"][0]),
                          _block_diag_groups(small["lru_wa"][1]), _block_diag_groups(small["lru_wx"][1])],
                         axis=2).astype(bf16)
    zeros5 = jnp.zeros((5, D), f32)
    lp = jnp.stack([jnp.concatenate([small["lru_lambda"][d:d + 1], small["lru_ba"][d:d + 1],
                                     small["lru_bx"][d:d + 1], zeros5], axis=0) for d in range(2)])

    def hosted(name, fn, *args, **kw):
        outs, got = fn(*args, comm=tuple(before(name)), **kw)
        after(name, got)
        return outs

    xn, proj = hosted("norm_inproj", _norm_matmul, x, g1, env["w_in_t"], "norm_inproj", tm=2048, tn=512,
                      row_block=lambda j: jnp.where(j < 6, j, jnp.where(j < 10, j + 1, 6)))
    uc = _conv_fwd(proj, cw, cb)
    (hf,), _ = _lru_fwd(uc, wg, lp, False)
    (hb,), _ = _lru_fwd(uc, wg, lp, True)
    yb, attn_stats = hosted("attn_fwd", _attn_fwd, proj, sink)
    merged, x1 = _merge_outproj(x, hf, hb, proj, yb, bg, env["w_out"])
    (xn2, gu), _ = _norm_matmul(x1, g2, env["w_fi_t"], "norm_ffn_in")
    ff, dx2, dx2b, loss, dg3, dgt, dup = _ffn_out_loss(gu, x1, env["w_fo"], g3, tgt)

    env["dw_fo"] = _mm_tn(ff, dx2b, "dw_ffn_out", tk=1408, tn=1024)
    dx1, dx1b, dg2 = hosted("ffn_in_bwd", _proj_bwd, [(dgt, 0, 0, D_FF), (dup, 0, 1, D_FF)], env["w_fi_t"],
                            x1, g2, dx2, "ffn_in_bwd")
    dw_gate = _mm_tn(dgt, xn2, "dw_ffn_in_gate", tk=1408, tn=1024, out_rows=2 * D_FF)
    env["dw_fi_t"] = _mm_tn(dup, xn2, "dw_ffn_in_up", tk=1408, tn=1024, into=dw_gate, row=D_FF // 1408)
    env["dw_out"] = _mm_tn(merged, dx1b, "dw_out", tk=1024, tn=1024)
    dh, dgl, dz, dyb, dbg = _outproj_bwd(dx1b, env["w_out"], hf, hb, proj, yb, bg)
    dq, dk2, dv2, dsink = hosted("attn_bwd", _attn_bwd, proj, sink, dyb, attn_stats)
    dkv = jnp.concatenate([dk2[BLK:BLK + S], dv2[BLK:BLK + S]], axis=1).astype(bf16)
    duc_f, dwg_f, dp_f = hosted("lru_bwd", _lru_bwd, uc, dh, hf, wg, lp, False)
    (duc_b, dwg_b, dp_b), _ = _lru_bwd(uc, dh, hb, wg, lp, True)
    env["grads_early"] = {
        "loss": loss[:, :1], "b_gate": dbg,
        "lru_lambda": jnp.concatenate([dp_f[0:1], dp_b[0:1]], axis=0),
        "lru_wa": jnp.stack([_diag_blocks(dwg_f[:, :, :LRU_GW]), _diag_blocks(dwg_b[:, :, :LRU_GW])]),
        "lru_ba": jnp.concatenate([dp_f[1:2], dp_b[1:2]], axis=0),
        "lru_wx": jnp.stack([_diag_blocks(dwg_f[:, :, LRU_GW:]), _diag_blocks(dwg_b[:, :, LRU_GW:])]),
        "lru_bx": jnp.concatenate([dp_f[2:3], dp_b[2:3]], axis=0),
        "attn_sink": dsink[:, :N_HEADS], "norm_ffn_g": dg2, "norm_final_g": dg3,
    }
    du, dcw, dcb = hosted("conv_bwd", _conv_bwd, duc_f, duc_b, proj, cw)
    dw_in = _mm_tn(du, xn, "dw_in_u", tk=1024, tn=1024, out_rows=IN_W)
    dw_in = _mm_tn(dgl, xn, "dw_in_g", tk=1024, tn=1024, into=dw_in, row=1)
    dw_in = _mm_tn(dq, xn, "dw_in_q", tk=1024, tn=1024, into=dw_in, row=2)
    dw_in = _mm_tn(dkv, xn, "dw_in_kv", tk=512, tn=1024, into=dw_in, row=3072 // 512)
    env["dw_in_t"] = _mm_tn(dz, xn, "dw_in_z", tk=512, tn=1024, into=dw_in, row=3584 // 512)
    col_pieces = [(du, 0, 0, D), (dgl, 0, 1, D), (dq, 0, 2, D), (dkv, 0, 3072 // 512, 512),
                  *[(dz, i, 3584 // 512 + i, 512) for i in range(4)]]
    dx, _, dg1 = hosted("inproj_bwd", _proj_bwd, col_pieces, env["w_in_t"], x, g1, dx1, "inproj_bwd")

    grads = dict(env["grads_early"], norm_mix_g=dg1, conv_w=dcw, conv_b=dcb)
    return dx, grads


def _adamw(gparts, w, m, v, name, tr=256):
    n, rows, cols = gparts.shape
    tr = _div_tile(rows, tr)
    c1 = 1.0 - ADAM_B1 ** ADAM_STEP
    c2 = 1.0 - ADAM_B2 ** ADAM_STEP

    def body(g_ref, w_ref, m_ref, v_ref, go_ref, d_ref, mo_ref, vo_ref):
        g = g_ref[0].astype(f32)
        for j in range(1, n):
            g = g + g_ref[j].astype(f32)
        mn = ADAM_B1 * m_ref[0] + (1.0 - ADAM_B1) * g
        vn = ADAM_B2 * v_ref[0] + (1.0 - ADAM_B2) * (g * g)
        m_hat = mn / c1
        v_hat = vn / c2
        go_ref[0] = g
        d_ref[0] = -ADAM_LR * (m_hat / (jnp.sqrt(v_hat) + ADAM_EPS) + ADAM_WD * w_ref[0])
        mo_ref[0] = mn
        vo_ref[0] = vn

    blk = pl.BlockSpec((1, tr, cols), lambda i: (0, i, 0))
    shp = jax.ShapeDtypeStruct((1, rows, cols), f32)
    return pl.pallas_call(
        body, name=name, grid=(rows // tr,),
        in_specs=[pl.BlockSpec((n, tr, cols), lambda i: (0, i, 0)), blk, blk, blk],
        out_specs=[blk, blk, blk, blk], out_shape=[shp, shp, shp, shp],
        compiler_params=_cparams())(gparts, w, m, v)


def _sum_parts(parts, name):
    n, rows, cols = parts.shape

    def body(p_ref, o_ref):
        acc = p_ref[0].astype(f32)
        for j in range(1, n):
            acc = acc + p_ref[j].astype(f32)
        o_ref[...] = acc

    return pl.pallas_call(
        body, name=name, out_shape=jax.ShapeDtypeStruct((rows, cols), f32),
        compiler_params=_cparams())(parts)


def _pack_rows(arrs, dtype=f32):
    rows, spans, at = [], [], 0
    for a in arrs:
        flat = a.reshape(-1).astype(dtype)
        nr = -(-flat.shape[0] // 1024)
        rows.append(jnp.pad(flat, (0, nr * 1024 - flat.shape[0])).reshape(nr, 1024))
        spans.append((at, nr))
        at += nr
    pad = (-at) % 16
    if pad:
        rows.append(jnp.zeros((pad, 1024), dtype))
    return jnp.concatenate(rows, axis=0), spans


def _unpack_rows(packed, spans, shapes):
    out = []
    for (at, nr), shp in zip(spans, shapes):
        n = math.prod(shp)
        out.append(packed[at:at + nr].reshape(-1)[:n].reshape(shp))
    return out


BIG = ("w_in", "w_out", "w_ffn_in", "w_ffn_out")
SMALL_REPL = ("norm_mix_g", "b_gate", "conv_b", "lru_wa", "lru_wx", "attn_sink", "norm_ffn_g", "norm_final_g")
SMALL_SHARD = ("conv_w", "lru_lambda", "lru_ba", "lru_bx")
ORDER = ("norm_mix_g", "w_in", "b_gate", "conv_w", "conv_b", "lru_lambda", "lru_wa", "lru_ba", "lru_wx",
         "lru_bx", "attn_sink", "w_out", "norm_ffn_g", "w_ffn_in", "w_ffn_out", "norm_final_g")
EARLY_F32 = ("loss", "b_gate", "lru_lambda", "lru_ba", "lru_bx", "attn_sink", "norm_ffn_g", "norm_final_g")
EARLY_BF16 = ("lru_wa", "lru_wx")
LATE = ("norm_mix_g", "conv_w", "conv_b")


def kernel(x, norm_mix_g, w_in, b_gate, conv_w, conv_b, lru_lambda, lru_wa, lru_ba, lru_wx, lru_bx, attn_sink, w_out, norm_ffn_g, w_ffn_in, w_ffn_out, norm_final_g, loss_target, m_norm_mix_g, m_w_in, m_b_gate, m_conv_w, m_conv_b, m_lru_lambda, m_lru_wa, m_lru_ba, m_lru_wx, m_lru_bx, m_attn_sink, m_w_out, m_norm_ffn_g, m_w_ffn_in, m_w_ffn_out, m_norm_final_g, v_norm_mix_g, v_w_in, v_b_gate, v_conv_w, v_conv_b, v_lru_lambda, v_lru_wa, v_lru_ba, v_lru_wx, v_lru_bx, v_attn_sink, v_w_out, v_norm_ffn_g, v_w_ffn_in, v_w_ffn_out, v_norm_final_g):
    w = dict(norm_mix_g=norm_mix_g, w_in=w_in, b_gate=b_gate, conv_w=conv_w, conv_b=conv_b, lru_lambda=lru_lambda,
             lru_wa=lru_wa, lru_ba=lru_ba, lru_wx=lru_wx, lru_bx=lru_bx, attn_sink=attn_sink, w_out=w_out,
             norm_ffn_g=norm_ffn_g, w_ffn_in=w_ffn_in, w_ffn_out=w_ffn_out, norm_final_g=norm_final_g)
    m = dict(norm_mix_g=m_norm_mix_g, w_in=m_w_in, b_gate=m_b_gate, conv_w=m_conv_w, conv_b=m_conv_b,
             lru_lambda=m_lru_lambda, lru_wa=m_lru_wa, lru_ba=m_lru_ba, lru_wx=m_lru_wx, lru_bx=m_lru_bx,
             attn_sink=m_attn_sink, w_out=m_w_out, norm_ffn_g=m_norm_ffn_g, w_ffn_in=m_w_ffn_in,
             w_ffn_out=m_w_ffn_out, norm_final_g=m_norm_final_g)
    v = dict(norm_mix_g=v_norm_mix_g, w_in=v_w_in, b_gate=v_b_gate, conv_w=v_conv_w, conv_b=v_conv_b,
             lru_lambda=v_lru_lambda, lru_wa=v_lru_wa, lru_ba=v_lru_ba, lru_wx=v_lru_wx, lru_bx=v_lru_bx,
             attn_sink=v_attn_sink, w_out=v_w_out, norm_ffn_g=v_norm_ffn_g, w_ffn_in=v_w_ffn_in,
             w_ffn_out=v_w_ffn_out, norm_final_g=v_norm_final_g)
    me = 4 * lax.axis_index("x") + 2 * lax.axis_index("y") + lax.axis_index("c")

    def shard_t(a):
        return jnp.swapaxes(a[0], 0, 1)

    def rows_parts(g):
        return g.reshape(N_DEV, -1, g.shape[1])

    shard_rows = jnp.concatenate([w[n][0] for n in SMALL_SHARD], axis=0)
    got_w_in, got_rows = _exchange([(shard_t(w_in).astype(bf16), False), (shard_rows, False)], "gather_w_in")
    full_rows = jnp.swapaxes(got_rows, 0, 1).reshape(shard_rows.shape[0], -1)
    small = {n: w[n] for n in ("norm_mix_g", "b_gate", "conv_b", "attn_sink", "norm_ffn_g")}
    small["lru_wa"], small["lru_wx"] = lru_wa[0], lru_wx[0]
    small["norm_final_g"] = norm_final_g.reshape(1, D)
    small["conv_w"], small["lru_lambda"] = full_rows[0:4], full_rows[4:6]
    small["lru_ba"], small["lru_bx"] = full_rows[6:8], full_rows[8:10]

    env = {"w_in_t": got_w_in.reshape(IN_W, D)}
    recv = {}

    def before(name):
        if name == "norm_inproj":
            return [(w_out[0].astype(bf16), False), (w_ffn_out[0].astype(bf16), False)]
        if name == "attn_fwd":
            return [(shard_t(w_ffn_in).astype(bf16), False)]
        if name == "ffn_in_bwd":
            return [(rows_parts(env["dw_fo"]), True)]
        if name == "attn_bwd":
            return [(rows_parts(env["dw_out"]), True)]
        if name == "lru_bwd":
            return [(rows_parts(env["dw_fi_t"]), True)]
        if name == "conv_bwd":
            ge = env["grads_early"]
            p32, env["early_f32_spans"] = _pack_rows([ge[n] for n in EARLY_F32])
            p16, env["early_bf16_spans"] = _pack_rows([ge[n] for n in EARLY_BF16], bf16)
            return [(p32, False), (p16, False)]
        if name == "inproj_bwd":
            return [(rows_parts(env["dw_in_t"]), True)]
        return []

    def after(name, got):
        if name == "norm_inproj":
            env["w_out"], env["w_fo"] = got[0].reshape(D, D), got[1].reshape(D_FF, D)
        elif name == "attn_fwd":
            env["w_fi_t"] = got[0].reshape(2 * D_FF, D)
        elif name == "ffn_in_bwd":
            recv["w_ffn_out"] = got[0]
        elif name == "attn_bwd":
            recv["w_out"] = got[0]
        elif name == "lru_bwd":
            recv["w_ffn_in"] = got[0]
        elif name == "conv_bwd":
            recv["early_f32"], recv["early_bf16"] = got
        elif name == "inproj_bwd":
            recv["w_in"] = got[0]

    grad_x, grads = _local_step(x[0], loss_target[0], small, env, before, after)

    outs = {}
    for name in ("w_out", "w_ffn_out"):
        outs[name] = _adamw(recv[name], w[name], m[name], v[name], "adamw_" + name)
    for name in ("w_in", "w_ffn_in"):
        t = lambda a: jnp.swapaxes(a, 1, 2)
        outs[name] = [t(r) for r in _adamw(recv[name], t(w[name]), t(m[name]), t(v[name]), "adamw_" + name)]

    small_names = SMALL_REPL + SMALL_SHARD
    late_packed, late_spans = _pack_rows([grads[n] for n in LATE])
    (got_late,) = _exchange([(late_packed, False)], "gather_late_grads")
    summed = {}
    for names, got, spans, tag in ((EARLY_F32, recv["early_f32"], env["early_f32_spans"], "early_f32"),
                                   (EARLY_BF16, recv["early_bf16"], env["early_bf16_spans"], "early_bf16"),
                                   (LATE, got_late, late_spans, "late")):
        total = _sum_parts(got, "sum_small_" + tag)
        summed.update(zip(names, _unpack_rows(total, spans, [grads[n].shape for n in names])))
    loss = summed["loss"].reshape(())
    gsm = {n: summed[n].reshape(w[n].shape) for n in SMALL_REPL}
    for n in SMALL_SHARD:
        full = summed[n]
        gsm[n] = lax.dynamic_slice_in_dim(full, me * 128, 128, axis=1).reshape(w[n].shape)
    pk = lambda dct: _pack_rows([dct[n] for n in small_names])[0]
    gp, sp = _pack_rows([gsm[n] for n in small_names])
    res = _adamw(gp[None], pk(w)[None], pk(m)[None], pk(v)[None], "adamw_small")
    sshapes = [w[n].shape for n in small_names]
    for idx, t in enumerate(res):
        for n, a in zip(small_names, _unpack_rows(t[0], sp, sshapes)):
            outs.setdefault(n, [None] * 4)[idx] = a

    result = [loss, grad_x[None]]
    for idx in range(4):
        result += [outs[n][idx] for n in ORDER]
    return tuple(result)
```

```python
import functools
import math

import jax
import jax.numpy as jnp
from jax import lax
from jax.experimental import pallas as pl
from jax.experimental.pallas import tpu as pltpu

f32 = jnp.float32
bf16 = jnp.bfloat16

D = 1024
D_FF = 2816
IN_W = 5632
N_HEADS = 16
N_KV = 4
HEAD_DIM = 64
WINDOW = 128
BLK = 128
LRU_HEADS = 16
LRU_BLOCK = 64
LRU_GROUPS = 4
LRU_GW = 256
LRU_CHUNK = 128
LRU_ROWS = 2048
RGLRU_C = 8.0
EPS = 1e-6
NEG_INF = -1e30
N_DEV = 8

ADAM_LR = 0.001
ADAM_B1 = 0.9
ADAM_B2 = 0.999
ADAM_EPS = 1e-08
ADAM_WD = 0.01
ADAM_STEP = 10

VMEM_MB = 56

C_U, C_G, C_Q, C_Z0, C_Z1, C_K, C_V = 0, 1024, 2048, 3072, 4096, 5120, 5376


def _cparams(vmem_mb=VMEM_MB):
    return pltpu.CompilerParams(vmem_limit_bytes=vmem_mb << 20)


def _div_tile(n, pref):
    if n <= pref:
        return n
    return max(t for t in range(8, pref + 1, 8) if n % t == 0)


def _sigmoid(x):
    return 0.5 * jnp.tanh(0.5 * x) + 0.5


def _log1p(x):
    u = 1.0 + x
    d = u - 1.0
    return jnp.where(d == 0.0, x, jnp.log(u) * (x / jnp.where(d == 0.0, 1.0, d)))


def _softplus(x):
    return jnp.maximum(x, 0.0) + _log1p(jnp.exp(-jnp.abs(x)))


def _gelu_and_grad(x):
    c = math.sqrt(2.0 / math.pi)
    inner = c * (x + 0.044715 * (x * x * x))
    t = jnp.tanh(inner)
    gelu = 0.5 * x * (1.0 + t)
    dinner = c * (1.0 + 3 * 0.044715 * (x * x))
    dgelu = 0.5 * (1.0 + t) + 0.5 * x * (1.0 - t * t) * dinner
    return gelu, dgelu


def _rms_bwd(dn, xv, g):
    r = lax.rsqrt(jnp.mean(xv * xv, axis=-1, keepdims=True) + EPS)
    xh = xv * r
    dxh = dn * g
    dx = r * (dxh - xh * jnp.mean(dxh * xh, axis=-1, keepdims=True))
    return dx, dn * xh


ANY_SPEC = pl.BlockSpec(memory_space=pl.ANY)


def _comm_out_shape(src, scatter):
    return jax.ShapeDtypeStruct((N_DEV, *(src.shape[1:] if scatter else src.shape)), src.dtype)


def _comm_sems():
    return [pltpu.SemaphoreType.DMA((N_DEV - 1,)), pltpu.SemaphoreType.DMA((N_DEV - 1,)), pltpu.SemaphoreType.DMA]


def _scatter_descs(src_ref, out_ref, send_sems, recv_sems, local_sem):
    x, y, c = lax.axis_index("x"), lax.axis_index("y"), lax.axis_index("c")
    me = 4 * x + 2 * y + c
    descs = [pltpu.make_async_copy(src_ref.at[me], out_ref.at[me], local_sem)]
    for k in range(1, N_DEV):
        px, py, pc = x ^ (k >> 2), y ^ ((k >> 1) & 1), c ^ (k & 1)
        descs.append(pltpu.make_async_remote_copy(
            src_ref=src_ref.at[4 * px + 2 * py + pc], dst_ref=out_ref.at[me],
            send_sem=send_sems.at[k - 1], recv_sem=recv_sems.at[k - 1],
            device_id=(px, py, pc), device_id_type=pl.DeviceIdType.MESH))
    return descs


def _gather_copies(src_ref, out_ref, send_sems, recv_sems, local_sem, which):
    x, y, c = lax.axis_index("x"), lax.axis_index("y"), lax.axis_index("c")
    me, sibling = (x, y, c), (x, y, 1 - c)
    chips = [(1 - x, y), (x, 1 - y), (1 - x, 1 - y)]

    def slot(px, py, pc):
        return out_ref.at[4 * px + 2 * py + pc]

    def copy(k, block, to, src=None):
        return pltpu.make_async_remote_copy(
            src_ref=slot(*block) if src is None else src, dst_ref=slot(*block),
            send_sem=send_sems.at[k], recv_sem=recv_sems.at[k], device_id=to, device_id_type=pl.DeviceIdType.MESH)

    make = {
        "local": lambda: pltpu.make_async_copy(src_ref, slot(*me), local_sem),
        "first": lambda: [copy(0, me, sibling, src=src_ref)] + [copy(1 + j, me, (*chip, c), src=src_ref)
                                                                 for j, chip in enumerate(chips)],
        "passed": lambda: [copy(4 + j, (*chip, c), sibling) for j, chip in enumerate(chips)],
        "landed": lambda: [copy(1 + j, (*chip, c), me) for j, chip in enumerate(chips)],
        "later": lambda: [copy(0, sibling, me)] + [copy(4 + j, (*chip, 1 - c), me) for j, chip in enumerate(chips)],
    }
    return [make[name]() for name in which]


def _comm_start(src_ref, out_ref, sems, scatter):
    if scatter:
        for d in _scatter_descs(src_ref, out_ref, *sems):
            d.start()
    else:
        local, first = _gather_copies(src_ref, out_ref, *sems, which=("local", "first"))
        local.start()
        for cp in first:
            cp.start()


def _comm_pass_on(src_ref, out_ref, sems, scatter):
    if not scatter:
        landed, passed = _gather_copies(src_ref, out_ref, *sems, which=("landed", "passed"))
        for arrived, onward in zip(landed, passed):
            arrived.wait_recv()
            onward.start()


def _comm_finish(src_ref, out_ref, sems, scatter):
    if scatter:
        for d in _scatter_descs(src_ref, out_ref, *sems):
            d.wait()
    else:
        later, first, passed, local = _gather_copies(src_ref, out_ref, *sems,
                                                     which=("later", "first", "passed", "local"))
        for cp in later:
            cp.wait_recv()
        for cp in first + passed:
            cp.wait_send()
        local.wait()


def _exchange(comm, name):
    nc = len(comm)

    def body(*refs):
        srcs, outs, sems = refs[:nc], refs[nc:2 * nc], refs[2 * nc:]
        for stage in (_comm_start, _comm_pass_on, _comm_finish):
            for i in range(nc):
                stage(srcs[i], outs[i], sems[3 * i:3 * i + 3], comm[i][1])

    return pl.pallas_call(
        body, name=name, in_specs=[ANY_SPEC] * nc, out_specs=[ANY_SPEC] * nc,
        out_shape=[_comm_out_shape(*c) for c in comm],
        scratch_shapes=[s for _ in comm for s in _comm_sems()],
    )(*[c[0] for c in comm])


def _hosted_call(body, *, name, grid, in_specs, out_specs, out_shape, args, scratch_shapes=(), comm=()):
    nin, nout, nscr, nc = len(in_specs), len(out_specs), len(scratch_shapes), len(comm)
    steps = math.prod(grid)

    def wrapped(*refs):
        ins = refs[:nin]
        csrc = refs[nin:nin + nc]
        outs = refs[nin + nc:nin + nc + nout]
        cout = refs[nin + nc + nout:nin + 2 * nc + nout]
        scr = refs[nin + 2 * nc + nout:]
        sems = scr[nscr:]

        def at(step, stage):
            lin = 0
            for a in range(len(grid)):
                lin = lin * grid[a] + pl.program_id(a)

            @pl.when(lin == step)
            def _():
                for i in range(nc):
                    stage(csrc[i], cout[i], sems[3 * i:3 * i + 3], comm[i][1])

        if nc:
            at(0, _comm_start)

        body(*ins, *outs, *scr[:nscr])

        if nc:
            at((3 * (steps - 1)) // 4, _comm_pass_on)
            at(steps - 1, _comm_finish)

    res = pl.pallas_call(
        wrapped, name=name, grid=grid,
        in_specs=[*in_specs, *[ANY_SPEC] * nc], out_specs=[*out_specs, *[ANY_SPEC] * nc],
        out_shape=[*out_shape, *[_comm_out_shape(*c) for c in comm]],
        scratch_shapes=[*scratch_shapes, *[s for _ in comm for s in _comm_sems()]],
        compiler_params=_cparams())(*args, *[c[0] for c in comm])
    return res[:nout], res[nout:]


def _rmsnorm_bf16(x, g, name, tm=1024, comm=()):
    S, dm = x.shape
    tm = min(tm, S)

    def body(x_ref, g_ref, xn_ref):
        xv = x_ref[...]
        r = lax.rsqrt(jnp.mean(xv * xv, axis=-1, keepdims=True) + EPS)
        xn_ref[...] = ((xv * r) * g_ref[...]).astype(bf16)

    row = pl.BlockSpec((tm, dm), lambda i: (i, 0))
    return _hosted_call(
        body, name=name, grid=(S // tm,), in_specs=[row, pl.BlockSpec((1, dm), lambda i: (0, 0))],
        out_specs=[row], out_shape=[jax.ShapeDtypeStruct((S, dm), bf16)], args=(x, g), comm=comm)


def _matmul_t(a, wt, name, tm=2048, tn=512, row_block=lambda j: j, comm=()):
    S, dm = a.shape
    n = wt.shape[0]
    tm = min(tm, S)

    def body(a_ref, w_ref, o_ref):
        o_ref[...] = lax.dot_general(a_ref[...], w_ref[...], (((1,), (1,)), ((), ())),
                                     preferred_element_type=f32).astype(bf16)

    return _hosted_call(
        body, name=name, grid=(S // tm, n // tn),
        in_specs=[pl.BlockSpec((tm, dm), lambda i, j: (i, 0)),
                  pl.BlockSpec((tn, dm), lambda i, j: (row_block(j), 0))],
        out_specs=[pl.BlockSpec((tm, tn), lambda i, j: (i, j))],
        out_shape=[jax.ShapeDtypeStruct((S, n), bf16)], args=(a, wt), comm=comm)


def _norm_matmul(x, g, wt, name, tm=1024, tn=1408, row_block=lambda j: j, comm=()):
    S, dm = x.shape
    n = wt.shape[0]
    tm = min(tm, S)

    def body(x_ref, g_ref, w_ref, xn_ref, o_ref):
        @pl.when(pl.program_id(1) == 0)
        def _():
            xv = x_ref[...]
            r = lax.rsqrt(jnp.mean(xv * xv, axis=-1, keepdims=True) + EPS)
            xn_ref[...] = ((xv * r) * g_ref[...]).astype(bf16)

        o_ref[...] = lax.dot_general(xn_ref[...], w_ref[...], (((1,), (1,)), ((), ())),
                                     preferred_element_type=f32).astype(bf16)

    return _hosted_call(
        body, name=name, grid=(S // tm, n // tn),
        in_specs=[pl.BlockSpec((tm, dm), lambda i, j: (i, 0)),
                  pl.BlockSpec((1, dm), lambda i, j: (0, 0)),
                  pl.BlockSpec((tn, dm), lambda i, j: (row_block(j), 0))],
        out_specs=[pl.BlockSpec((tm, dm), lambda i, j: (i, 0)),
                   pl.BlockSpec((tm, tn), lambda i, j: (i, j))],
        out_shape=[jax.ShapeDtypeStruct((S, dm), bf16), jax.ShapeDtypeStruct((S, n), bf16)],
        args=(x, g, wt), comm=comm)


def _mm_tn(a, b, name, tk, tn, tmc=2048, into=None, row=0, out_rows=None):
    m, ka = a.shape
    n = b.shape[1]
    tmc = min(tmc, m)
    nk = m // tmc

    def body(a_ref, b_ref, *rest):
        o_ref, acc_ref = rest[-2:]
        k = pl.program_id(2)
        part = lax.dot_general(a_ref[...], b_ref[...], (((0,), (0,)), ((), ())), preferred_element_type=f32)

        @pl.when(k == 0)
        def _():
            acc_ref[...] = part

        @pl.when(k > 0)
        def _():
            acc_ref[...] += part

        @pl.when(k == nk - 1)
        def _():
            o_ref[...] = acc_ref[...].astype(bf16)

    in_specs = [pl.BlockSpec((tmc, tk), lambda i, j, k: (k, i)), pl.BlockSpec((tmc, tn), lambda i, j, k: (k, j))]
    if into is None:
        return pl.pallas_call(
            body, name=name, grid=(ka // tk, n // tn, nk), in_specs=in_specs,
            out_specs=pl.BlockSpec((tk, tn), lambda i, j, k: (i + row, j)),
            out_shape=jax.ShapeDtypeStruct((out_rows or ka, n), bf16),
            scratch_shapes=[pltpu.VMEM((tk, tn), f32)],
            compiler_params=_cparams())(a, b)
    return pl.pallas_call(
        body, name=name, grid=(ka // tk, n // tn, nk), in_specs=[*in_specs, ANY_SPEC],
        out_specs=pl.BlockSpec((tk, tn), lambda i, j, k: (i + row, j)),
        out_shape=jax.ShapeDtypeStruct(into.shape, into.dtype),
        scratch_shapes=[pltpu.VMEM((tk, tn), f32)], input_output_aliases={2: 0},
        compiler_params=_cparams())(a, b, into)


HALO = 16


def _rows_at(ext, o, tc):
    if o == 0:
        return ext[HALO:HALO + tc]
    return pltpu.roll(ext, (-o) % ext.shape[0], 0)[HALO:HALO + tc]


def _halo_specs(tc, S, width, col):
    per = tc // HALO
    last = S // HALO - 1
    return (pl.BlockSpec((tc, width), lambda i: (i, col)),
            pl.BlockSpec((HALO, width), lambda i: (jnp.maximum(i * per - 1, 0), col)),
            pl.BlockSpec((HALO, width), lambda i: (jnp.minimum((i + 1) * per, last), col)))


def _extended(cur_ref, prev_ref, next_ref, i, nsteps):
    prev = jnp.where(i > 0, prev_ref[...].astype(f32), 0.0)
    nxt = jnp.where(i < nsteps - 1, next_ref[...].astype(f32), 0.0)
    return jnp.concatenate([prev, cur_ref[...].astype(f32), nxt], axis=0)


def _conv_fwd(proj, cw, cb, tc=1024):
    S = proj.shape[0]
    tc = min(tc, S)
    nsteps = S // tc

    def body(cur_ref, prev_ref, next_ref, w_ref, b_ref, o_ref):
        ext = _extended(cur_ref, prev_ref, next_ref, pl.program_id(0), nsteps)
        acc = _rows_at(ext, -2, tc) * w_ref[0:1, :]
        for k in range(1, 4):
            acc = acc + _rows_at(ext, k - 2, tc) * w_ref[k:k + 1, :]
        o_ref[...] = acc + b_ref[...]

    return pl.pallas_call(
        body, name="conv_fwd", grid=(nsteps,),
        in_specs=[*_halo_specs(tc, S, D, 0),
                  pl.BlockSpec((4, D), lambda i: (0, 0)), pl.BlockSpec((1, D), lambda i: (0, 0))],
        out_specs=pl.BlockSpec((tc, D), lambda i: (i, 0)),
        out_shape=jax.ShapeDtypeStruct((S, D), f32),
        compiler_params=_cparams())(proj, proj, proj, cw, cb)


def _conv_bwd(duc_f, duc_b, proj, cw, tc=1024, comm=()):
    S = proj.shape[0]
    tc = min(tc, S)
    nsteps = S // tc

    def body(fc, fp, fn, bc, bp, bn, uc_, up, un, w_ref, du_ref, dw_ref, db_ref):
        i = pl.program_id(0)

        @pl.when(i == 0)
        def _():
            dw_ref[...] = jnp.zeros_like(dw_ref)
            db_ref[...] = jnp.zeros_like(db_ref)

        dext = _extended(fc, fp, fn, i, nsteps) + _extended(bc, bp, bn, i, nsteps)
        uext = _extended(uc_, up, un, i, nsteps)
        d = dext[HALO:HALO + tc]
        acc = _rows_at(dext, 2, tc) * w_ref[0:1, :]
        for k in range(1, 4):
            acc = acc + _rows_at(dext, 2 - k, tc) * w_ref[k:k + 1, :]
        du_ref[...] = acc.astype(bf16)
        wrow = lax.broadcasted_iota(jnp.int32, (4, D), 0)
        for k in range(4):
            dw_ref[...] += jnp.where(wrow == k, jnp.sum(d * _rows_at(uext, k - 2, tc), axis=0, keepdims=True), 0.0)
        db_ref[...] += jnp.sum(d, axis=0, keepdims=True)

    return _hosted_call(
        body, name="conv_bwd", grid=(nsteps,),
        in_specs=[*_halo_specs(tc, S, D, 0), *_halo_specs(tc, S, D, 0), *_halo_specs(tc, S, D, 0),
                  pl.BlockSpec((4, D), lambda i: (0, 0))],
        out_specs=[pl.BlockSpec((tc, D), lambda i: (i, 0)),
                   pl.BlockSpec((4, D), lambda i: (0, 0)), pl.BlockSpec((1, D), lambda i: (0, 0))],
        out_shape=[jax.ShapeDtypeStruct((S, D), bf16), jax.ShapeDtypeStruct((4, D), f32),
                   jax.ShapeDtypeStruct((1, D), f32)],
        args=(duc_f, duc_f, duc_f, duc_b, duc_b, duc_b, proj, proj, proj, cw), comm=comm)


def _scan_scratch():
    halves = [pltpu.VMEM((LRU_CHUNK, 128), f32) for _ in range(2 * (LRU_GW // 128))]
    return [*halves, pltpu.VMEM((LRU_CHUNK // 8, LRU_GW), f32), pltpu.VMEM((LRU_CHUNK // 8, LRU_GW), f32)]


def _log_scan(a, b, row, n, reverse, steps):
    for s in steps:
        shift = a.shape[0] - s if reverse else s
        keep = (row < n - s) if reverse else (row >= s)
        a_sh = pltpu.roll(a, shift, 0)
        b_sh = pltpu.roll(b, shift, 0)
        b = jnp.where(keep, a * b_sh + b, b)
        a = jnp.where(keep, a * a_sh, a)
    return a, b


def _scan_chunk(a, b, carry, reverse, *scratch):
    tc, w = a.shape
    ng = tc // 8
    nl = w // 128
    sa_refs, sb_refs, sc_ref, st_ref = scratch[:nl], scratch[nl:2 * nl], scratch[2 * nl], scratch[2 * nl + 1]
    sub = lax.broadcasted_iota(jnp.int32, (8, w), 0)
    ag, bg = [], []
    for k in range(ng):
        ak, bk = _log_scan(a[8 * k:8 * k + 8], b[8 * k:8 * k + 8], sub, 8, reverse, (1, 2, 4))
        ag.append(ak)
        bg.append(bk)
    a = jnp.concatenate(ag, axis=0)
    b = jnp.concatenate(bg, axis=0)
    edge = 0 if reverse else 7
    for i in range(nl):
        sa_refs[i][...] = a[:, 128 * i:128 * (i + 1)]
        sb_refs[i][...] = b[:, 128 * i:128 * (i + 1)]
    ta = jnp.concatenate([r[pl.ds(edge, ng, stride=8), :] for r in sa_refs], axis=1)
    tb = jnp.concatenate([r[pl.ds(edge, ng, stride=8), :] for r in sb_refs], axis=1)
    grow = lax.broadcasted_iota(jnp.int32, (ng, w), 0)
    ta, tb = _log_scan(ta, tb, grow, ng, reverse, [1 << i for i in range(ng.bit_length() - 1)])
    state = tb + ta * carry
    st_ref[...] = state
    if reverse:
        sc_ref[...] = jnp.where(grow == ng - 1, carry, pltpu.roll(state, ng - 1, 0))
    else:
        sc_ref[...] = jnp.where(grow == 0, carry, pltpu.roll(state, 1, 0))
    h = jnp.concatenate([bg[k] + ag[k] * sc_ref[k:k + 1, :] for k in range(ng)], axis=0)
    return h, (st_ref[0:1, :] if reverse else st_ref[ng - 1:ng, :])


def _lru_gates(uc, w, p_ref):
    pre = jnp.dot(uc.astype(bf16), w, preferred_element_type=f32)
    r = _sigmoid(pre[:, :LRU_GW] + p_ref[0, 1:2, :])
    gi = _sigmoid(pre[:, LRU_GW:] + p_ref[0, 2:3, :])
    sp = _softplus(-p_ref[0, 0:1, :])
    log_a = -RGLRU_C * r * sp
    a = jnp.exp(log_a)
    x = 2.0 * log_a
    series = -x * (1.0 + x * (0.5 + x * (1.0 / 6 + x * (1.0 / 24))))
    beta = jnp.sqrt(jnp.maximum(jnp.where(x > -0.0625, series, 1.0 - a * a), 0.0))
    return r, gi, sp, a, beta


def _lru_fwd(uc, wg, lp, reverse, comm=()):
    S = uc.shape[0]
    tc = LRU_CHUNK
    rows = min(LRU_ROWS, S)
    nsub = rows // tc
    nblk = S // rows
    d = 1 if reverse else 0

    def bidx(c):
        return nblk - 1 - c if reverse else c

    def body(uc_ref, w_ref, p_ref, h_ref, carry_ref, *scan_scratch):
        @pl.when(pl.program_id(1) == 0)
        def _():
            carry_ref[...] = jnp.zeros_like(carry_ref)

        carry = carry_ref[...]
        for j in (reversed(range(nsub)) if reverse else range(nsub)):
            sl = slice(j * tc, (j + 1) * tc)
            ucv = uc_ref[sl, :]
            _, gi, _, a, beta = _lru_gates(ucv, w_ref[0], p_ref)
            h, carry = _scan_chunk(a, beta * (gi * ucv), carry, reverse, *scan_scratch)
            h_ref[sl, :] = h.astype(bf16)
        carry_ref[...] = carry

    return _hosted_call(
        body, name="lru_fwd_rev" if reverse else "lru_fwd", grid=(LRU_GROUPS, nblk),
        in_specs=[pl.BlockSpec((rows, LRU_GW), lambda g, c: (bidx(c), g)),
                  pl.BlockSpec((1, LRU_GW, 2 * LRU_GW), lambda g, c: (g, 0, d)),
                  pl.BlockSpec((1, 8, LRU_GW), lambda g, c: (d, 0, g))],
        out_specs=[pl.BlockSpec((rows, LRU_GW), lambda g, c: (bidx(c), g))],
        out_shape=[jax.ShapeDtypeStruct((S, D), bf16)],
        scratch_shapes=[pltpu.VMEM((1, LRU_GW), f32), *_scan_scratch()],
        args=(uc, wg, lp), comm=comm)


def _lru_bwd(uc, dh, h, wg, lp, reverse, comm=()):
    S = uc.shape[0]
    tc = LRU_CHUNK
    rows = min(LRU_ROWS, S)
    nsub = rows // tc
    nblk = S // rows
    d = 1 if reverse else 0
    per = rows // HALO
    last8 = S // HALO - 1

    def bidx(c):
        return c if reverse else nblk - 1 - c

    def halo_idx(c):
        if reverse:
            return jnp.minimum((bidx(c) + 1) * per, last8)
        return jnp.maximum(bidx(c) * per - 1, 0)

    def body(uc_ref, dh_ref, h_ref, halo_ref, w_ref, p_ref, duc_ref, dw_ref, dp_ref, carry_ref, tmp_ref,
             *scan_scratch):
        c = pl.program_id(1)
        bi = bidx(c)

        @pl.when(c == 0)
        def _():
            carry_ref[...] = jnp.zeros_like(carry_ref)
            dw_ref[...] = jnp.zeros_like(dw_ref)
            dp_ref[...] = jnp.zeros_like(dp_ref)

        row = lax.broadcasted_iota(jnp.int32, (tc, LRU_GW), 0)
        carry = carry_ref[...]
        dw = jnp.zeros((LRU_GW, 2 * LRU_GW), f32)
        dsp = jnp.zeros((1, LRU_GW), f32)
        dba = jnp.zeros((1, LRU_GW), f32)
        dbx = jnp.zeros((1, LRU_GW), f32)
        for j in (range(nsub) if reverse else reversed(range(nsub))):
            sl = slice(j * tc, (j + 1) * tc)
            ucv = uc_ref[sl, :]
            ucb = ucv.astype(bf16)
            r, gi, sp, a, beta = _lru_gates(ucv, w_ref[0], p_ref)
            hv = h_ref[sl, :].astype(f32)
            dhv = dh_ref[sl, :].astype(f32)
            if reverse:
                alpha = jnp.where(row == 0, 1.0, pltpu.roll(a, 1, 0))
                gsc, _ = _scan_chunk(alpha, dhv, carry, False, *scan_scratch)
                if j < nsub - 1:
                    edge = h_ref[(j + 1) * tc:(j + 1) * tc + HALO, :].astype(f32)[0:1, :]
                else:
                    edge = jnp.where(bi < nblk - 1, halo_ref[...].astype(f32)[0:1, :], 0.0)
                h_nb = jnp.where(row == tc - 1, edge, pltpu.roll(hv, tc - 1, 0))
            else:
                alpha = jnp.where(row == tc - 1, 1.0, pltpu.roll(a, tc - 1, 0))
                gsc, _ = _scan_chunk(alpha, dhv, carry, True, *scan_scratch)
                if j > 0:
                    edge = h_ref[j * tc - HALO:j * tc, :].astype(f32)[HALO - 1:HALO, :]
                else:
                    edge = jnp.where(bi > 0, halo_ref[...].astype(f32)[HALO - 1:HALO, :], 0.0)
                h_nb = jnp.where(row == 0, edge, pltpu.roll(hv, 1, 0))
            tmp_ref[...] = a * gsc
            carry = tmp_ref[tc - 1:tc, :] if reverse else tmp_ref[0:1, :]

            da = gsc * h_nb
            dbeta = gsc * (gi * ucv)
            dl = da * a - dbeta * (a * a) / beta
            dr = dl * (-RGLRU_C * sp)
            dsp = dsp + jnp.sum(dl * (-RGLRU_C * r), axis=0, keepdims=True)
            dgi = gsc * beta * ucv
            dpre_r = dr * r * (1.0 - r)
            dpre_i = dgi * gi * (1.0 - gi)
            dba = dba + jnp.sum(dpre_r, axis=0, keepdims=True)
            dbx = dbx + jnp.sum(dpre_i, axis=0, keepdims=True)
            dpre = jnp.concatenate([dpre_r, dpre_i], axis=1).astype(bf16)
            back = lax.dot_general(dpre, w_ref[0], (((1,), (1,)), ((), ())), preferred_element_type=f32)
            duc_ref[sl, :] = (gsc * beta * gi + back).astype(bf16)
            dw = dw + lax.dot_general(ucb, dpre, (((0,), (0,)), ((), ())), preferred_element_type=f32)
        carry_ref[...] = carry
        dw_ref[0] += dw
        dlam = -dsp / (1.0 + jnp.exp(p_ref[0, 0:1, :]))
        prow = lax.broadcasted_iota(jnp.int32, (8, LRU_GW), 0)
        dp_ref[...] += (jnp.where(prow == 0, dlam, 0.0) + jnp.where(prow == 1, dba, 0.0)
                        + jnp.where(prow == 2, dbx, 0.0))

    chunk = pl.BlockSpec((rows, LRU_GW), lambda g, c: (bidx(c), g))
    return _hosted_call(
        body, name="lru_bwd_rev" if reverse else "lru_bwd", grid=(LRU_GROUPS, nblk),
        in_specs=[chunk, chunk, chunk,
                  pl.BlockSpec((HALO, LRU_GW), lambda g, c: (halo_idx(c), g)),
                  pl.BlockSpec((1, LRU_GW, 2 * LRU_GW), lambda g, c: (g, 0, d)),
                  pl.BlockSpec((1, 8, LRU_GW), lambda g, c: (d, 0, g))],
        out_specs=[chunk,
                   pl.BlockSpec((1, LRU_GW, 2 * LRU_GW), lambda g, c: (g, 0, 0)),
                   pl.BlockSpec((8, LRU_GW), lambda g, c: (0, g))],
        out_shape=[jax.ShapeDtypeStruct((S, D), bf16),
                   jax.ShapeDtypeStruct((LRU_GROUPS, LRU_GW, 2 * LRU_GW), f32),
                   jax.ShapeDtypeStruct((8, D), f32)],
        scratch_shapes=[pltpu.VMEM((1, LRU_GW), f32), pltpu.VMEM((tc, LRU_GW), f32), *_scan_scratch()],
        args=(uc, dh, h, h, wg, lp), comm=comm)


def _slope(h):
    return 2.0 ** (-8.0 * (h + 1.0) / N_HEADS)


def _kv_specs(nb, col):
    return [pl.BlockSpec((BLK, N_KV * HEAD_DIM), lambda n: (jnp.maximum(n - 1, 0), col)),
            pl.BlockSpec((BLK, N_KV * HEAD_DIM), lambda n: (n, col)),
            pl.BlockSpec((BLK, N_KV * HEAD_DIM), lambda n: (jnp.minimum(n + 1, nb - 1), col))]


def _dup_windows(r0, r1, r2):
    left = lax.broadcasted_iota(jnp.int32, (3 * BLK, 128), 1) < HEAD_DIM
    win = jnp.concatenate([r0[...], r1[...], r2[...]], axis=0)
    out = []
    for i in range(N_KV // 2):
        t = win[:, i * 128:(i + 1) * 128]
        r = pltpu.roll(t, HEAD_DIM, 1)
        out += [jnp.where(left, t, r).astype(bf16), jnp.where(left, r, t).astype(bf16)]
    return out


def _attn_bias_init(bias_ref):
    k_loc = lax.broadcasted_iota(jnp.int32, (3 * BLK, BLK), 0)
    q_loc = lax.broadcasted_iota(jnp.int32, (3 * BLK, BLK), 1)
    adist = jnp.abs(q_loc + BLK - k_loc)
    adf = adist.astype(f32)
    for e in range(3):
        ok = adist <= WINDOW
        if e == 0:
            ok = ok & (k_loc >= BLK)
        if e == 2:
            ok = ok & (k_loc < 2 * BLK)
        for kv in range(N_KV):
            bias_ref[e, kv] = jnp.concatenate(
                [jnp.where(ok, (-_slope(4 * kv + j)) * adf, NEG_INF) for j in range(4)], axis=1)


def _stack_heads(ref, kv, scale):
    left = lax.broadcasted_iota(jnp.int32, (BLK, 128), 1) < HEAD_DIM
    rows = []
    for pp in range(2):
        t = ref[:, (2 * kv + pp) * 128:(2 * kv + pp + 1) * 128]
        if scale != 1.0:
            t = t * scale
        zero = jnp.zeros_like(t)
        rows += [jnp.where(left, t, zero).astype(bf16), jnp.where(left, zero, t).astype(bf16)]
    return jnp.concatenate(rows, axis=0)


def _attn_softmax(qs, k2, bias, sink_ref, kv, stats=None):
    sink = jnp.concatenate([jnp.full((1, BLK), sink_ref[0, 4 * kv + j], f32) for j in range(4)], axis=1)
    s = lax.dot_general(k2, qs, (((1,), (1,)), ((), ())), preferred_element_type=f32) + bias
    m = jnp.maximum(jnp.max(s, axis=0, keepdims=True), sink) if stats is None else stats[0]
    p = jnp.exp(s - m)
    ps = jnp.exp(sink - m)
    inv = 1.0 / (jnp.sum(p, axis=0, keepdims=True) + ps) if stats is None else stats[1]
    return p, ps, m, inv


def _pair_tiles(t):
    return [jnp.concatenate([t[:HEAD_DIM, 256 * pp:256 * pp + 128],
                             t[HEAD_DIM:, 256 * pp + 128:256 * pp + 256]], axis=0).T for pp in range(2)]


def _attn_fwd(proj, sink, comm=()):
    S = proj.shape[0]
    nb = S // BLK
    assert nb >= 2

    def body(q_ref, k0, k1, k2_, v0, v1, v2_, sink_ref, o_ref, st_ref, bias_ref):
        n = pl.program_id(0)

        @pl.when(n == 0)
        def _():
            _attn_bias_init(bias_ref)

        e = jnp.where(n == 0, 0, jnp.where(n == nb - 1, 2, 1))
        kk = _dup_windows(k0, k1, k2_)
        vv = _dup_windows(v0, v1, v2_)
        tiles = []
        for kv in range(N_KV):
            qs = _stack_heads(q_ref, kv, HEAD_DIM ** -0.5)
            p, _, m, inv = _attn_softmax(qs, kk[kv], bias_ref[e, kv], sink_ref, kv)
            st_ref[0, kv:kv + 1, :] = m
            st_ref[0, N_KV + kv:N_KV + kv + 1, :] = inv
            ot = lax.dot_general(vv[kv], p.astype(bf16), (((0,), (0,)), ((), ())), preferred_element_type=f32)
            tiles += _pair_tiles(ot * inv)
        o_ref[...] = jnp.concatenate(tiles, axis=1).astype(bf16)

    return _hosted_call(
        body, name="attn_fwd", grid=(nb,),
        in_specs=[pl.BlockSpec((BLK, D), lambda n: (n, C_Q // D)),
                  *_kv_specs(nb, C_K // (N_KV * HEAD_DIM)), *_kv_specs(nb, C_V // (N_KV * HEAD_DIM)),
                  pl.BlockSpec(memory_space=pltpu.SMEM)],
        out_specs=[pl.BlockSpec((BLK, D), lambda n: (n, 0)), pl.BlockSpec((1, 2 * N_KV, 4 * BLK), lambda n: (n, 0, 0))],
        out_shape=[jax.ShapeDtypeStruct((S, D), bf16), jax.ShapeDtypeStruct((nb, 2 * N_KV, 4 * BLK), f32)],
        scratch_shapes=[pltpu.VMEM((3, N_KV, 3 * BLK, 4 * BLK), f32)],
        args=(proj, proj, proj, proj, proj, proj, proj, sink), comm=comm)


def _attn_bwd(proj, sink, dyb, stats, comm=()):
    S = proj.shape[0]
    nb = S // BLK
    assert nb >= 2

    def body(q_ref, k0, k1, k2_, v0, v1, v2_, sink_ref, do_ref, st_ref, dq_ref, dk_out, dv_out, ds_ref,
             bias_ref, dk_ref, dv_ref, dsk_ref):
        n = pl.program_id(0)

        @pl.when(n == 0)
        def _():
            _attn_bias_init(bias_ref)
            dk_ref[...] = jnp.zeros_like(dk_ref)
            dv_ref[...] = jnp.zeros_like(dv_ref)
            dsk_ref[...] = jnp.zeros_like(dsk_ref)

        e = jnp.where(n == 0, 0, jnp.where(n == nb - 1, 2, 1))
        kk = _dup_windows(k0, k1, k2_)
        vv = _dup_windows(v0, v1, v2_)
        left3 = lax.broadcasted_iota(jnp.int32, (3 * BLK, 128), 1) < HEAD_DIM
        start = pl.multiple_of(n * BLK, BLK)
        dq_tiles, dks, dvs = [], [], []
        for kv in range(N_KV):
            qs = _stack_heads(q_ref, kv, HEAD_DIM ** -0.5)
            dos = _stack_heads(do_ref, kv, 1.0)
            stats = (st_ref[0, kv:kv + 1, :], st_ref[0, N_KV + kv:N_KV + kv + 1, :])
            p, ps, _, inv = _attn_softmax(qs, kk[kv], bias_ref[e, kv], sink_ref, kv, stats)
            pn = p * inv
            dp = lax.dot_general(vv[kv], dos, (((1,), (1,)), ((), ())), preferred_element_type=f32)
            delta = jnp.sum(pn * dp, axis=0, keepdims=True)
            dsc = (pn * (dp - delta)).astype(bf16)
            dsk_ref[kv:kv + 1, :] += delta * (ps * inv)
            dqt = lax.dot_general(kk[kv], dsc, (((0,), (0,)), ((), ())), preferred_element_type=f32)
            dq_tiles += _pair_tiles(dqt * (HEAD_DIM ** -0.5))
            dk = jnp.dot(dsc, qs, preferred_element_type=f32)
            dv = jnp.dot(pn.astype(bf16), dos, preferred_element_type=f32)
            dks.append(dk + pltpu.roll(dk, HEAD_DIM, 1))
            dvs.append(dv + pltpu.roll(dv, HEAD_DIM, 1))
        for jp in range(N_KV // 2):
            cols = slice(jp * 128, (jp + 1) * 128)
            dk_ref[pl.ds(start, 3 * BLK), cols] += jnp.where(left3, dks[2 * jp], dks[2 * jp + 1])
            dv_ref[pl.ds(start, 3 * BLK), cols] += jnp.where(left3, dvs[2 * jp], dvs[2 * jp + 1])
        dq_ref[...] = jnp.concatenate(dq_tiles, axis=1).astype(bf16)

        @pl.when(n == nb - 1)
        def _():
            pltpu.sync_copy(dk_ref, dk_out)
            pltpu.sync_copy(dv_ref, dv_out)
            lane = lax.broadcasted_iota(jnp.int32, (1, 128), 1)
            dsink = jnp.zeros((1, 128), f32)
            for h in range(N_HEADS):
                part = dsk_ref[h // 4:h // 4 + 1, (h % 4) * BLK:(h % 4 + 1) * BLK]
                dsink = dsink + jnp.where(lane == h, -jnp.sum(part), 0.0)
            ds_ref[...] = dsink

    acc = jax.ShapeDtypeStruct((S + 2 * BLK, N_KV * HEAD_DIM), f32)
    return _hosted_call(
        body, name="attn_bwd", grid=(nb,),
        in_specs=[pl.BlockSpec((BLK, D), lambda n: (n, C_Q // D)),
                  *_kv_specs(nb, C_K // (N_KV * HEAD_DIM)), *_kv_specs(nb, C_V // (N_KV * HEAD_DIM)),
                  pl.BlockSpec(memory_space=pltpu.SMEM),
                  pl.BlockSpec((BLK, D), lambda n: (n, 0)),
                  pl.BlockSpec((1, 2 * N_KV, 4 * BLK), lambda n: (n, 0, 0))],
        out_specs=[pl.BlockSpec((BLK, D), lambda n: (n, 0)), ANY_SPEC, ANY_SPEC,
                   pl.BlockSpec((1, 128), lambda n: (0, 0))],
        out_shape=[jax.ShapeDtypeStruct((S, D), bf16), acc, acc, jax.ShapeDtypeStruct((1, 128), f32)],
        scratch_shapes=[pltpu.VMEM((3, N_KV, 3 * BLK, 4 * BLK), f32), pltpu.VMEM(acc.shape, f32),
                        pltpu.VMEM(acc.shape, f32), pltpu.VMEM((8, 4 * BLK), f32)],
        args=(proj, proj, proj, proj, proj, proj, proj, sink, dyb, stats), comm=comm)


def _merge_parts(hf, hb, g, z0, z1, yb, bg):
    g0 = _sigmoid(z0.astype(f32) + bg[:, :D])
    g1 = _sigmoid(z1.astype(f32) + bg[:, D:])
    gelu, dgelu = _gelu_and_grad(g.astype(f32))
    hs = hf.astype(f32) + hb.astype(f32)
    ya = hs * gelu
    return g0, g1, gelu, dgelu, hs, ya


def _merge_outproj(x, hf, hb, proj, yb, bg, w_out, tm=1024):
    S = x.shape[0]
    tm = min(tm, S)

    def body(x_ref, hf_ref, hb_ref, g_ref, z0_ref, z1_ref, yb_ref, bg_ref, w_ref, mg_ref, x1_ref):
        ybv = yb_ref[...].astype(f32)
        g0, g1, _, _, _, ya = _merge_parts(hf_ref[...], hb_ref[...], g_ref[...], z0_ref[...], z1_ref[...],
                                           ybv, bg_ref[...])
        mg = (g0 * ya + g1 * ybv).astype(bf16)
        mg_ref[...] = mg
        x1_ref[...] = x_ref[...] + jnp.dot(mg, w_ref[...], preferred_element_type=f32)

    row = pl.BlockSpec((tm, D), lambda i: (i, 0))
    return pl.pallas_call(
        body, name="merge_outproj", grid=(S // tm,),
        in_specs=[row, row, row,
                  pl.BlockSpec((tm, D), lambda i: (i, C_G // D)),
                  pl.BlockSpec((tm, D), lambda i: (i, C_Z0 // D)),
                  pl.BlockSpec((tm, D), lambda i: (i, C_Z1 // D)),
                  row, pl.BlockSpec((1, 2 * D), lambda i: (0, 0)), pl.BlockSpec((D, D), lambda i: (0, 0))],
        out_specs=[row, row],
        out_shape=[jax.ShapeDtypeStruct((S, D), bf16), jax.ShapeDtypeStruct((S, D), f32)],
        compiler_params=_cparams())(x, hf, hb, proj, proj, proj, yb, bg, w_out)


def _ffn_out_loss(gu, x1, w_fo, g3, tgt, tm=256):
    S = x1.shape[0]
    tm = min(tm, S)

    def body(gt_ref, up_ref, x1_ref, w_ref, g_ref, t_ref, ff_ref, dx_ref, dxb_ref, loss_ref, dg_ref,
             dgt_ref, dup_ref):
        @pl.when(pl.program_id(0) == 0)
        def _():
            loss_ref[...] = jnp.zeros_like(loss_ref)
            dg_ref[...] = jnp.zeros_like(dg_ref)

        gt = gt_ref[...].astype(f32)
        up = up_ref[...].astype(f32)
        sg = _sigmoid(gt)
        silu = gt * sg
        ff = (silu * up).astype(bf16)
        ff_ref[...] = ff
        x2 = x1_ref[...] + jnp.dot(ff, w_ref[...], preferred_element_type=f32)
        gv = g_ref[...]
        r = lax.rsqrt(jnp.mean(x2 * x2, axis=-1, keepdims=True) + EPS)
        xh = x2 * r
        diff = xh * gv - t_ref[...]
        loss_ref[...] += (0.5 / D) * jnp.sum(diff * diff)
        dy = diff * (1.0 / D)
        dg_ref[...] += jnp.sum(dy * xh, axis=0, keepdims=True)
        dxh = dy * gv
        dx = r * (dxh - xh * jnp.mean(dxh * xh, axis=-1, keepdims=True))
        dx_ref[...] = dx
        dxb = dx.astype(bf16)
        dxb_ref[...] = dxb
        dff = lax.dot_general(dxb, w_ref[...], (((1,), (1,)), ((), ())), preferred_element_type=f32)
        dup_ref[...] = (dff * silu).astype(bf16)
        dgt_ref[...] = ((dff * up) * (sg * (1.0 + gt * (1.0 - sg)))).astype(bf16)

    row = pl.BlockSpec((tm, D), lambda i: (i, 0))
    vec = pl.BlockSpec((1, D), lambda i: (0, 0))
    wide = pl.BlockSpec((tm, D_FF), lambda i: (i, 0))
    wide_shape = jax.ShapeDtypeStruct((S, D_FF), bf16)
    return pl.pallas_call(
        body, name="ffn_out_loss", grid=(S // tm,),
        in_specs=[wide, pl.BlockSpec((tm, D_FF), lambda i: (i, 1)),
                  row, pl.BlockSpec((D_FF, D), lambda i: (0, 0)), vec, row],
        out_specs=[wide, row, row, pl.BlockSpec((1, 128), lambda i: (0, 0)), vec, wide, wide],
        out_shape=[wide_shape, jax.ShapeDtypeStruct((S, D), f32), jax.ShapeDtypeStruct((S, D), bf16),
                   jax.ShapeDtypeStruct((1, 128), f32), jax.ShapeDtypeStruct((1, D), f32), wide_shape, wide_shape],
        compiler_params=_cparams())(gu, gu, x1, w_fo, g3, tgt)


def _proj_bwd(pieces, wt, xres, g, dres, name, tm=512, comm=()):
    S = xres.shape[0]
    tm = min(tm, S)
    np_ = len(pieces)

    def body(*refs):
        p_refs = refs[:np_]
        w_refs = refs[np_:2 * np_]
        x_ref, g_ref, dres_ref, dx_ref, dxb_ref, dg_ref = refs[2 * np_:]

        @pl.when(pl.program_id(0) == 0)
        def _():
            dg_ref[...] = jnp.zeros_like(dg_ref)

        dn = jnp.dot(p_refs[0][...], w_refs[0][...], preferred_element_type=f32)
        for pr, wr in zip(p_refs[1:], w_refs[1:]):
            dn = dn + jnp.dot(pr[...], wr[...], preferred_element_type=f32)
        dxn, dgc = _rms_bwd(dn, x_ref[...], g_ref[...])
        dx = dres_ref[...] + dxn
        dx_ref[...] = dx
        dxb_ref[...] = dx.astype(bf16)
        dg_ref[...] += jnp.sum(dgc, axis=0, keepdims=True)

    row = pl.BlockSpec((tm, D), lambda i: (i, 0))
    vec = pl.BlockSpec((1, D), lambda i: (0, 0))
    return _hosted_call(
        body, name=name, grid=(S // tm,),
        in_specs=[*[pl.BlockSpec((tm, wd), functools.partial(lambda i, cb: (i, cb), cb=acb))
                    for _, acb, _, wd in pieces],
                  *[pl.BlockSpec((wd, D), functools.partial(lambda i, rb: (rb, 0), rb=wrb))
                    for _, _, wrb, wd in pieces],
                  row, vec, row],
        out_specs=[row, row, vec],
        out_shape=[jax.ShapeDtypeStruct((S, D), f32), jax.ShapeDtypeStruct((S, D), bf16),
                   jax.ShapeDtypeStruct((1, D), f32)],
        args=(*[p[0] for p in pieces], *[wt] * np_, xres, g, dres), comm=comm)


def _outproj_bwd(dx1b, w_out, hf, hb, proj, yb, bg, tm=1024):
    S = dx1b.shape[0]
    tm = min(tm, S)

    def body(dx_ref, w_ref, hf_ref, hb_ref, g_ref, z0_ref, z1_ref, yb_ref, bg_ref,
             dh_ref, dg_ref, dz_ref, dyb_ref, dbg_ref):
        @pl.when(pl.program_id(0) == 0)
        def _():
            dbg_ref[...] = jnp.zeros_like(dbg_ref)

        dm = lax.dot_general(dx_ref[...], w_ref[...], (((1,), (1,)), ((), ())), preferred_element_type=f32)
        ybv = yb_ref[...].astype(f32)
        g0, g1, gelu, dgelu, hs, ya = _merge_parts(hf_ref[...], hb_ref[...], g_ref[...], z0_ref[...],
                                                   z1_ref[...], ybv, bg_ref[...])
        dya = dm * g0
        dh_ref[...] = (dya * gelu).astype(bf16)
        dg_ref[...] = (dya * hs * dgelu).astype(bf16)
        dyb_ref[...] = (dm * g1).astype(bf16)
        dz0 = (dm * ya) * (g0 * (1.0 - g0))
        dz1 = (dm * ybv) * (g1 * (1.0 - g1))
        dz = jnp.concatenate([dz0, dz1], axis=1)
        dz_ref[...] = dz.astype(bf16)
        dbg_ref[...] += jnp.sum(dz, axis=0, keepdims=True)

    row = pl.BlockSpec((tm, D), lambda i: (i, 0))
    return pl.pallas_call(
        body, name="outproj_bwd", grid=(S // tm,),
        in_specs=[row, pl.BlockSpec((D, D), lambda i: (0, 0)), row, row,
                  pl.BlockSpec((tm, D), lambda i: (i, C_G // D)),
                  pl.BlockSpec((tm, D), lambda i: (i, C_Z0 // D)),
                  pl.BlockSpec((tm, D), lambda i: (i, C_Z1 // D)),
                  row, pl.BlockSpec((1, 2 * D), lambda i: (0, 0))],
        out_specs=[row, row, pl.BlockSpec((tm, 2 * D), lambda i: (i, 0)), row,
                   pl.BlockSpec((1, 2 * D), lambda i: (0, 0))],
        out_shape=[jax.ShapeDtypeStruct((S, D), bf16), jax.ShapeDtypeStruct((S, D), bf16),
                   jax.ShapeDtypeStruct((S, 2 * D), bf16), jax.ShapeDtypeStruct((S, D), bf16),
                   jax.ShapeDtypeStruct((1, 2 * D), f32)],
        compiler_params=_cparams())(dx1b, w_out, hf, hb, proj, proj, proj, yb, bg)


def _block_diag_groups(w):
    w4 = w.reshape(LRU_GROUPS, 4, LRU_BLOCK, LRU_BLOCK)
    eye = jnp.eye(4, dtype=w.dtype)
    return jnp.einsum("ghij,hk->ghikj", w4, eye).reshape(LRU_GROUPS, LRU_GW, LRU_GW)


def _diag_blocks(dw):
    d5 = dw.reshape(LRU_GROUPS, 4, LRU_BLOCK, 4, LRU_BLOCK)
    return jnp.stack([d5[:, h, :, h, :] for h in range(4)], axis=1).reshape(LRU_HEADS, LRU_BLOCK, LRU_BLOCK)


def _local_step(x, tgt, small, env, before=lambda name: (), after=lambda name, got: None):
    S = x.shape[0]
    g1, g2, g3 = small["norm_mix_g"], small["norm_ffn_g"], small["norm_final_g"]
    bg, cb, sink = small["b_gate"], small["conv_b"], small["attn_sink"]

    def hosted(name, fn, *args, **kw):
        outs, got = fn(*args, comm=tuple(before(name)), **kw)
        after(name, got)
        return outs

    (xn,) = hosted("norm_x", _rmsnorm_bf16, x, g1, "norm_x")
    cw = small["conv_w"]
    wg = jnp.concatenate([_block_diag_groups(small["lru_wa"][0]), _block_diag_groups(small["lru_wx"][0]),
                          _block_diag_groups(small["lru_wa"][1]), _block_diag_groups(small["lru_wx"][1])],
                         axis=2).astype(bf16)
    zeros5 = jnp.zeros((5, D), f32)
    lp = jnp.stack([jnp.concatenate([small["lru_lambda"][d:d + 1], small["lru_ba"][d:d + 1],
                                     small["lru_bx"][d:d + 1], zeros5], axis=0) for d in range(2)])
    (proj,) = hosted("inproj", _matmul_t, xn, env["w_in_t"], "inproj", tm=2048, tn=512,
                     row_block=lambda j: jnp.where(j < 6, j, jnp.where(j < 10, j + 1, 6)))
    uc = _conv_fwd(proj, cw, cb)
    (hf,), _ = _lru_fwd(uc, wg, lp, False)
    (hb,), _ = _lru_fwd(uc, wg, lp, True)
    yb, attn_stats = hosted("attn_fwd", _attn_fwd, proj, sink)
    merged, x1 = _merge_outproj(x, hf, hb, proj, yb, bg, env["w_out"])
    (xn2, gu), _ = _norm_matmul(x1, g2, env["w_fi_t"], "norm_ffn_in")
    ff, dx2, dx2b, loss, dg3, dgt, dup = _ffn_out_loss(gu, x1, env["w_fo"], g3, tgt)

    env["dw_fo"] = _mm_tn(ff, dx2b, "dw_ffn_out", tk=1408, tn=1024)
    dx1, dx1b, dg2 = hosted("ffn_in_bwd", _proj_bwd, [(dgt, 0, 0, D_FF), (dup, 0, 1, D_FF)], env["w_fi_t"],
                            x1, g2, dx2, "ffn_in_bwd")
    dw_gate = _mm_tn(dgt, xn2, "dw_ffn_in_gate", tk=1408, tn=1024, out_rows=2 * D_FF)
    env["dw_fi_t"] = _mm_tn(dup, xn2, "dw_ffn_in_up", tk=1408, tn=1024, into=dw_gate, row=D_FF // 1408)
    env["dw_out"] = _mm_tn(merged, dx1b, "dw_out", tk=1024, tn=1024)
    dh, dgl, dz, dyb, dbg = _outproj_bwd(dx1b, env["w_out"], hf, hb, proj, yb, bg)
    dq, dk2, dv2, dsink = hosted("attn_bwd", _attn_bwd, proj, sink, dyb, attn_stats)
    dkv = jnp.concatenate([dk2[BLK:BLK + S], dv2[BLK:BLK + S]], axis=1).astype(bf16)
    duc_f, dwg_f, dp_f = hosted("lru_bwd", _lru_bwd, uc, dh, hf, wg, lp, False)
    (duc_b, dwg_b, dp_b), _ = _lru_bwd(uc, dh, hb, wg, lp, True)
    env["grads_early"] = {
        "loss": loss[:, :1], "b_gate": dbg,
        "lru_lambda": jnp.concatenate([dp_f[0:1], dp_b[0:1]], axis=0),
        "lru_wa": jnp.stack([_diag_blocks(dwg_f[:, :, :LRU_GW]), _diag_blocks(dwg_b[:, :, :LRU_GW])]),
        "lru_ba": jnp.concatenate([dp_f[1:2], dp_b[1:2]], axis=0),
        "lru_wx": jnp.stack([_diag_blocks(dwg_f[:, :, LRU_GW:]), _diag_blocks(dwg_b[:, :, LRU_GW:])]),
        "lru_bx": jnp.concatenate([dp_f[2:3], dp_b[2:3]], axis=0),
        "attn_sink": dsink[:, :N_HEADS], "norm_ffn_g": dg2, "norm_final_g": dg3,
    }
    du, dcw, dcb = hosted("conv_bwd", _conv_bwd, duc_f, duc_b, proj, cw)
    dw_in = _mm_tn(du, xn, "dw_in_u", tk=1024, tn=1024, out_rows=IN_W)
    dw_in = _mm_tn(dgl, xn, "dw_in_g", tk=1024, tn=1024, into=dw_in, row=1)
    dw_in = _mm_tn(dq, xn, "dw_in_q", tk=1024, tn=1024, into=dw_in, row=2)
    dw_in = _mm_tn(dkv, xn, "dw_in_kv", tk=512, tn=1024, into=dw_in, row=3072 // 512)
    env["dw_in_t"] = _mm_tn(dz, xn, "dw_in_z", tk=512, tn=1024, into=dw_in, row=3584 // 512)
    col_pieces = [(du, 0, 0, D), (dgl, 0, 1, D), (dq, 0, 2, D), (dkv, 0, 3072 // 512, 512),
                  *[(dz, i, 3584 // 512 + i, 512) for i in range(4)]]
    dx, _, dg1 = hosted("inproj_bwd", _proj_bwd, col_pieces, env["w_in_t"], x, g1, dx1, "inproj_bwd")

    grads = dict(env["grads_early"], norm_mix_g=dg1, conv_w=dcw, conv_b=dcb)
    return dx, grads


def _adamw(gparts, w, m, v, name, tr=256):
    n, rows, cols = gparts.shape
    tr = _div_tile(rows, tr)
    c1 = 1.0 - ADAM_B1 ** ADAM_STEP
    c2 = 1.0 - ADAM_B2 ** ADAM_STEP

    def body(g_ref, w_ref, m_ref, v_ref, go_ref, d_ref, mo_ref, vo_ref):
        g = g_ref[0].astype(f32)
        for j in range(1, n):
            g = g + g_ref[j].astype(f32)
        mn = ADAM_B1 * m_ref[0] + (1.0 - ADAM_B1) * g
        vn = ADAM_B2 * v_ref[0] + (1.0 - ADAM_B2) * (g * g)
        m_hat = mn / c1
        v_hat = vn / c2
        go_ref[0] = g
        d_ref[0] = -ADAM_LR * (m_hat / (jnp.sqrt(v_hat) + ADAM_EPS) + ADAM_WD * w_ref[0])
        mo_ref[0] = mn
        vo_ref[0] = vn

    blk = pl.BlockSpec((1, tr, cols), lambda i: (0, i, 0))
    shp = jax.ShapeDtypeStruct((1, rows, cols), f32)
    return pl.pallas_call(
        body, name=name, grid=(rows // tr,),
        in_specs=[pl.BlockSpec((n, tr, cols), lambda i: (0, i, 0)), blk, blk, blk],
        out_specs=[blk, blk, blk, blk], out_shape=[shp, shp, shp, shp],
        compiler_params=_cparams())(gparts, w, m, v)


def _sum_parts(parts, name):
    n, rows, cols = parts.shape

    def body(p_ref, o_ref):
        acc = p_ref[0].astype(f32)
        for j in range(1, n):
            acc = acc + p_ref[j].astype(f32)
        o_ref[...] = acc

    return pl.pallas_call(
        body, name=name, out_shape=jax.ShapeDtypeStruct((rows, cols), f32),
        compiler_params=_cparams())(parts)


def _pack_rows(arrs, dtype=f32):
    rows, spans, at = [], [], 0
    for a in arrs:
        flat = a.reshape(-1).astype(dtype)
        nr = -(-flat.shape[0] // 1024)
        rows.append(jnp.pad(flat, (0, nr * 1024 - flat.shape[0])).reshape(nr, 1024))
        spans.append((at, nr))
        at += nr
    pad = (-at) % 16
    if pad:
        rows.append(jnp.zeros((pad, 1024), dtype))
    return jnp.concatenate(rows, axis=0), spans


def _unpack_rows(packed, spans, shapes):
    out = []
    for (at, nr), shp in zip(spans, shapes):
        n = math.prod(shp)
        out.append(packed[at:at + nr].reshape(-1)[:n].reshape(shp))
    return out


BIG = ("w_in", "w_out", "w_ffn_in", "w_ffn_out")
SMALL_REPL = ("norm_mix_g", "b_gate", "conv_b", "lru_wa", "lru_wx", "attn_sink", "norm_ffn_g", "norm_final_g")
SMALL_SHARD = ("conv_w", "lru_lambda", "lru_ba", "lru_bx")
ORDER = ("norm_mix_g", "w_in", "b_gate", "conv_w", "conv_b", "lru_lambda", "lru_wa", "lru_ba", "lru_wx",
         "lru_bx", "attn_sink", "w_out", "norm_ffn_g", "w_ffn_in", "w_ffn_out", "norm_final_g")
EARLY_F32 = ("loss", "b_gate", "lru_lambda", "lru_ba", "lru_bx", "attn_sink", "norm_ffn_g", "norm_final_g")
EARLY_BF16 = ("lru_wa", "lru_wx")
LATE = ("norm_mix_g", "conv_w", "conv_b")


def kernel(x, norm_mix_g, w_in, b_gate, conv_w, conv_b, lru_lambda, lru_wa, lru_ba, lru_wx, lru_bx, attn_sink, w_out, norm_ffn_g, w_ffn_in, w_ffn_out, norm_final_g, loss_target, m_norm_mix_g, m_w_in, m_b_gate, m_conv_w, m_conv_b, m_lru_lambda, m_lru_wa, m_lru_ba, m_lru_wx, m_lru_bx, m_attn_sink, m_w_out, m_norm_ffn_g, m_w_ffn_in, m_w_ffn_out, m_norm_final_g, v_norm_mix_g, v_w_in, v_b_gate, v_conv_w, v_conv_b, v_lru_lambda, v_lru_wa, v_lru_ba, v_lru_wx, v_lru_bx, v_attn_sink, v_w_out, v_norm_ffn_g, v_w_ffn_in, v_w_ffn_out, v_norm_final_g):
    w = dict(norm_mix_g=norm_mix_g, w_in=w_in, b_gate=b_gate, conv_w=conv_w, conv_b=conv_b, lru_lambda=lru_lambda,
             lru_wa=lru_wa, lru_ba=lru_ba, lru_wx=lru_wx, lru_bx=lru_bx, attn_sink=attn_sink, w_out=w_out,
             norm_ffn_g=norm_ffn_g, w_ffn_in=w_ffn_in, w_ffn_out=w_ffn_out, norm_final_g=norm_final_g)
    m = dict(norm_mix_g=m_norm_mix_g, w_in=m_w_in, b_gate=m_b_gate, conv_w=m_conv_w, conv_b=m_conv_b,
             lru_lambda=m_lru_lambda, lru_wa=m_lru_wa, lru_ba=m_lru_ba, lru_wx=m_lru_wx, lru_bx=m_lru_bx,
             attn_sink=m_attn_sink, w_out=m_w_out, norm_ffn_g=m_norm_ffn_g, w_ffn_in=m_w_ffn_in,
             w_ffn_out=m_w_ffn_out, norm_final_g=m_norm_final_g)
    v = dict(norm_mix_g=v_norm_mix_g, w_in=v_w_in, b_gate=v_b_gate, conv_w=v_conv_w, conv_b=v_conv_b,
             lru_lambda=v_lru_lambda, lru_wa=v_lru_wa, lru_ba=v_lru_ba, lru_wx=v_lru_wx, lru_bx=v_lru_bx,
             attn_sink=v_attn_sink, w_out=v_w_out, norm_ffn_g=v_norm_ffn_g, w_ffn_in=v_w_ffn_in,
             w_ffn_out=v_w_ffn_out, norm_final_g=v_norm_final_g)
    me = 4 * lax.axis_index("x") + 2 * lax.axis_index("y") + lax.axis_index("c")

    def shard_t(a):
        return jnp.swapaxes(a[0], 0, 1)

    def rows_parts(g):
        return g.reshape(N_DEV, -1, g.shape[1])

    shard_rows = jnp.concatenate([w[n][0] for n in SMALL_SHARD], axis=0)
    small = {n: w[n] for n in ("norm_mix_g", "b_gate", "conv_b", "attn_sink", "norm_ffn_g")}
    small["lru_wa"], small["lru_wx"] = lru_wa[0], lru_wx[0]
    small["norm_final_g"] = norm_final_g.reshape(1, D)
    env, recv = {}, {}

    def before(name):
        if name == "norm_x":
            return [(shard_t(w_in).astype(bf16), False), (shard_rows, False)]
        if name == "inproj":
            return [(w_out[0].astype(bf16), False), (w_ffn_out[0].astype(bf16), False)]
        if name == "attn_fwd":
            return [(shard_t(w_ffn_in).astype(bf16), False)]
        if name == "ffn_in_bwd":
            return [(rows_parts(env["dw_fo"]), True)]
        if name == "attn_bwd":
            return [(rows_parts(env["dw_out"]), True)]
        if name == "lru_bwd":
            return [(rows_parts(env["dw_fi_t"]), True)]
        if name == "conv_bwd":
            ge = env["grads_early"]
            p32, env["early_f32_spans"] = _pack_rows([ge[n] for n in EARLY_F32])
            p16, env["early_bf16_spans"] = _pack_rows([ge[n] for n in EARLY_BF16], bf16)
            return [(p32, False), (p16, False)]
        if name == "inproj_bwd":
            return [(rows_parts(env["dw_in_t"]), True)]
        return []

    def after(name, got):
        if name == "norm_x":
            env["w_in_t"] = got[0].reshape(IN_W, D)
            full_rows = jnp.swapaxes(got[1], 0, 1).reshape(shard_rows.shape[0], -1)
            small["conv_w"], small["lru_lambda"] = full_rows[0:4], full_rows[4:6]
            small["lru_ba"], small["lru_bx"] = full_rows[6:8], full_rows[8:10]
        elif name == "inproj":
            env["w_out"], env["w_fo"] = got[0].reshape(D, D), got[1].reshape(D_FF, D)
        elif name == "attn_fwd":
            env["w_fi_t"] = got[0].reshape(2 * D_FF, D)
        elif name == "ffn_in_bwd":
            recv["w_ffn_out"] = got[0]
        elif name == "attn_bwd":
            recv["w_out"] = got[0]
        elif name == "lru_bwd":
            recv["w_ffn_in"] = got[0]
        elif name == "conv_bwd":
            recv["early_f32"], recv["early_bf16"] = got
        elif name == "inproj_bwd":
            recv["w_in"] = got[0]

    grad_x, grads = _local_step(x[0], loss_target[0], small, env, before, after)

    outs = {}
    for name in ("w_out", "w_ffn_out"):
        outs[name] = _adamw(recv[name], w[name], m[name], v[name], "adamw_" + name)
    for name in ("w_in", "w_ffn_in"):
        t = lambda a: jnp.swapaxes(a, 1, 2)
        outs[name] = [t(r) for r in _adamw(recv[name], t(w[name]), t(m[name]), t(v[name]), "adamw_" + name)]

    small_names = SMALL_REPL + SMALL_SHARD
    late_packed, late_spans = _pack_rows([grads[n] for n in LATE])
    (got_late,) = _exchange([(late_packed, False)], "gather_late_grads")
    summed = {}
    for names, got, spans, tag in ((EARLY_F32, recv["early_f32"], env["early_f32_spans"], "early_f32"),
                                   (EARLY_BF16, recv["early_bf16"], env["early_bf16_spans"], "early_bf16"),
                                   (LATE, got_late, late_spans, "late")):
        total = _sum_parts(got, "sum_small_" + tag)
        summed.update(zip(names, _unpack_rows(total, spans, [grads[n].shape for n in names])))
    loss = summed["loss"].reshape(())
    gsm = {n: summed[n].reshape(w[n].shape) for n in SMALL_REPL}
    for n in SMALL_SHARD:
        full = summed[n]
        gsm[n] = lax.dynamic_slice_in_dim(full, me * 128, 128, axis=1).reshape(w[n].shape)
    pk = lambda dct: _pack_rows([dct[n] for n in small_names])[0]
    gp, sp = _pack_rows([gsm[n] for n in small_names])
    res = _adamw(gp[None], pk(w)[None], pk(m)[None], pk(v)[None], "adamw_small")
    sshapes = [w[n].shape for n in small_names]
    for idx, t in enumerate(res):
        for n, a in zip(small_names, _unpack_rows(t[0], sp, sshapes)):
            outs.setdefault(n, [None] * 4)[idx] = a

    result = [loss, grad_x[None]]
    for idx in range(4):
        result += [outs[n][idx] for n in ORDER]
    return tuple(result)
```

```python
import functools
import math

import jax
import jax.numpy as jnp
from jax import lax
from jax.experimental import pallas as pl
from jax.experimental.pallas import tpu as pltpu

f32 = jnp.float32
bf16 = jnp.bfloat16

D = 1024
D_FF = 2816
IN_W = 5632
N_HEADS = 16
N_KV = 4
HEAD_DIM = 64
WINDOW = 128
BLK = 128
LRU_HEADS = 16
LRU_BLOCK = 64
LRU_GROUPS = 4
LRU_GW = 256
LRU_CHUNK = 128
LRU_ROWS = 2048
RGLRU_C = 8.0
EPS = 1e-6
NEG_INF = -1e30
N_DEV = 8

ADAM_LR = 0.001
ADAM_B1 = 0.9
ADAM_B2 = 0.999
ADAM_EPS = 1e-08
ADAM_WD = 0.01
ADAM_STEP = 10

VMEM_MB = 56

C_U, C_G, C_Q, C_Z0, C_Z1, C_K, C_V = 0, 1024, 2048, 3072, 4096, 5120, 5376


def _cparams(vmem_mb=VMEM_MB):
    return pltpu.CompilerParams(vmem_limit_bytes=vmem_mb << 20)


def _div_tile(n, pref):
    if n <= pref:
        return n
    return max(t for t in range(8, pref + 1, 8) if n % t == 0)


def _sigmoid(x):
    return 0.5 * jnp.tanh(0.5 * x) + 0.5


def _log1p(x):
    u = 1.0 + x
    d = u - 1.0
    return jnp.where(d == 0.0, x, jnp.log(u) * (x / jnp.where(d == 0.0, 1.0, d)))


def _softplus(x):
    return jnp.maximum(x, 0.0) + _log1p(jnp.exp(-jnp.abs(x)))


def _gelu_and_grad(x):
    c = math.sqrt(2.0 / math.pi)
    inner = c * (x + 0.044715 * (x * x * x))
    t = jnp.tanh(inner)
    gelu = 0.5 * x * (1.0 + t)
    dinner = c * (1.0 + 3 * 0.044715 * (x * x))
    dgelu = 0.5 * (1.0 + t) + 0.5 * x * (1.0 - t * t) * dinner
    return gelu, dgelu


def _rms_bwd(dn, xv, g):
    r = lax.rsqrt(jnp.mean(xv * xv, axis=-1, keepdims=True) + EPS)
    xh = xv * r
    dxh = dn * g
    dx = r * (dxh - xh * jnp.mean(dxh * xh, axis=-1, keepdims=True))
    return dx, dn * xh


ANY_SPEC = pl.BlockSpec(memory_space=pl.ANY)


def _comm_out_shape(src, scatter):
    return jax.ShapeDtypeStruct((N_DEV, *(src.shape[1:] if scatter else src.shape)), src.dtype)


def _comm_sems():
    return [pltpu.SemaphoreType.DMA((N_DEV - 1,)), pltpu.SemaphoreType.DMA((N_DEV - 1,)), pltpu.SemaphoreType.DMA]


def _scatter_descs(src_ref, out_ref, send_sems, recv_sems, local_sem):
    x, y, c = lax.axis_index("x"), lax.axis_index("y"), lax.axis_index("c")
    me = 4 * x + 2 * y + c
    descs = [pltpu.make_async_copy(src_ref.at[me], out_ref.at[me], local_sem)]
    for k in range(1, N_DEV):
        px, py, pc = x ^ (k >> 2), y ^ ((k >> 1) & 1), c ^ (k & 1)
        descs.append(pltpu.make_async_remote_copy(
            src_ref=src_ref.at[4 * px + 2 * py + pc], dst_ref=out_ref.at[me],
            send_sem=send_sems.at[k - 1], recv_sem=recv_sems.at[k - 1],
            device_id=(px, py, pc), device_id_type=pl.DeviceIdType.MESH))
    return descs


def _gather_copies(src_ref, out_ref, send_sems, recv_sems, local_sem, which):
    x, y, c = lax.axis_index("x"), lax.axis_index("y"), lax.axis_index("c")
    me, sibling = (x, y, c), (x, y, 1 - c)
    chips = [(1 - x, y), (x, 1 - y), (1 - x, 1 - y)]

    def slot(px, py, pc):
        return out_ref.at[4 * px + 2 * py + pc]

    def copy(k, block, to, src=None):
        return pltpu.make_async_remote_copy(
            src_ref=slot(*block) if src is None else src, dst_ref=slot(*block),
            send_sem=send_sems.at[k], recv_sem=recv_sems.at[k], device_id=to, device_id_type=pl.DeviceIdType.MESH)

    make = {
        "local": lambda: pltpu.make_async_copy(src_ref, slot(*me), local_sem),
        "first": lambda: [copy(0, me, sibling, src=src_ref)] + [copy(1 + j, me, (*chip, c), src=src_ref)
                                                                 for j, chip in enumerate(chips)],
        "passed": lambda: [copy(4 + j, (*chip, c), sibling) for j, chip in enumerate(chips)],
        "landed": lambda: [copy(1 + j, (*chip, c), me) for j, chip in enumerate(chips)],
        "later": lambda: [copy(0, sibling, me)] + [copy(4 + j, (*chip, 1 - c), me) for j, chip in enumerate(chips)],
    }
    return [make[name]() for name in which]


def _comm_start(src_ref, out_ref, sems, scatter):
    if scatter:
        for d in _scatter_descs(src_ref, out_ref, *sems):
            d.start()
    else:
        local, first = _gather_copies(src_ref, out_ref, *sems, which=("local", "first"))
        local.start()
        for cp in first:
            cp.start()


def _comm_pass_on(src_ref, out_ref, sems, scatter):
    if not scatter:
        landed, passed = _gather_copies(src_ref, out_ref, *sems, which=("landed", "passed"))
        for arrived, onward in zip(landed, passed):
            arrived.wait_recv()
            onward.start()


def _comm_finish(src_ref, out_ref, sems, scatter):
    if scatter:
        for d in _scatter_descs(src_ref, out_ref, *sems):
            d.wait()
    else:
        later, first, passed, local = _gather_copies(src_ref, out_ref, *sems,
                                                     which=("later", "first", "passed", "local"))
        for cp in later:
            cp.wait_recv()
        for cp in first + passed:
            cp.wait_send()
        local.wait()


def _exchange(comm, name):
    nc = len(comm)

    def body(*refs):
        srcs, outs, sems = refs[:nc], refs[nc:2 * nc], refs[2 * nc:]
        for stage in (_comm_start, _comm_pass_on, _comm_finish):
            for i in range(nc):
                stage(srcs[i], outs[i], sems[3 * i:3 * i + 3], comm[i][1])

    return pl.pallas_call(
        body, name=name, in_specs=[ANY_SPEC] * nc, out_specs=[ANY_SPEC] * nc,
        out_shape=[_comm_out_shape(*c) for c in comm],
        scratch_shapes=[s for _ in comm for s in _comm_sems()],
    )(*[c[0] for c in comm])


def _hosted_call(body, *, name, grid, in_specs, out_specs, out_shape, args, scratch_shapes=(), comm=()):
    nin, nout, nscr, nc = len(in_specs), len(out_specs), len(scratch_shapes), len(comm)
    steps = math.prod(grid)

    def wrapped(*refs):
        ins = refs[:nin]
        csrc = refs[nin:nin + nc]
        outs = refs[nin + nc:nin + nc + nout]
        cout = refs[nin + nc + nout:nin + 2 * nc + nout]
        scr = refs[nin + 2 * nc + nout:]
        sems = scr[nscr:]

        def at(step, stage):
            lin = 0
            for a in range(len(grid)):
                lin = lin * grid[a] + pl.program_id(a)

            @pl.when(lin == step)
            def _():
                for i in range(nc):
                    stage(csrc[i], cout[i], sems[3 * i:3 * i + 3], comm[i][1])

        if nc:
            at(0, _comm_start)

        body(*ins, *outs, *scr[:nscr])

        if nc:
            at((3 * (steps - 1)) // 4, _comm_pass_on)
            at(steps - 1, _comm_finish)

    res = pl.pallas_call(
        wrapped, name=name, grid=grid,
        in_specs=[*in_specs, *[ANY_SPEC] * nc], out_specs=[*out_specs, *[ANY_SPEC] * nc],
        out_shape=[*out_shape, *[_comm_out_shape(*c) for c in comm]],
        scratch_shapes=[*scratch_shapes, *[s for _ in comm for s in _comm_sems()]],
        compiler_params=_cparams())(*args, *[c[0] for c in comm])
    return res[:nout], res[nout:]


def _rmsnorm_bf16(x, g, name, tm=1024, comm=()):
    S, dm = x.shape
    tm = min(tm, S)

    def body(x_ref, g_ref, xn_ref):
        xv = x_ref[...]
        r = lax.rsqrt(jnp.mean(xv * xv, axis=-1, keepdims=True) + EPS)
        xn_ref[...] = ((xv * r) * g_ref[...]).astype(bf16)

    row = pl.BlockSpec((tm, dm), lambda i: (i, 0))
    return _hosted_call(
        body, name=name, grid=(S // tm,), in_specs=[row, pl.BlockSpec((1, dm), lambda i: (0, 0))],
        out_specs=[row], out_shape=[jax.ShapeDtypeStruct((S, dm), bf16)], args=(x, g), comm=comm)


def _matmul_t(a, wt, name, tm=2048, tn=512, row_block=lambda j: j, comm=()):
    S, dm = a.shape
    n = wt.shape[0]
    tm = min(tm, S)

    def body(a_ref, w_ref, o_ref):
        o_ref[...] = lax.dot_general(a_ref[...], w_ref[...], (((1,), (1,)), ((), ())),
                                     preferred_element_type=f32).astype(bf16)

    return _hosted_call(
        body, name=name, grid=(S // tm, n // tn),
        in_specs=[pl.BlockSpec((tm, dm), lambda i, j: (i, 0)),
                  pl.BlockSpec((tn, dm), lambda i, j: (row_block(j), 0))],
        out_specs=[pl.BlockSpec((tm, tn), lambda i, j: (i, j))],
        out_shape=[jax.ShapeDtypeStruct((S, n), bf16)], args=(a, wt), comm=comm)


def _norm_matmul(x, g, wt, name, tm=1024, tn=1408, row_block=lambda j: j, comm=()):
    S, dm = x.shape
    n = wt.shape[0]
    tm = min(tm, S)

    def body(x_ref, g_ref, w_ref, xn_ref, o_ref):
        @pl.when(pl.program_id(1) == 0)
        def _():
            xv = x_ref[...]
            r = lax.rsqrt(jnp.mean(xv * xv, axis=-1, keepdims=True) + EPS)
            xn_ref[...] = ((xv * r) * g_ref[...]).astype(bf16)

        o_ref[...] = lax.dot_general(xn_ref[...], w_ref[...], (((1,), (1,)), ((), ())),
                                     preferred_element_type=f32).astype(bf16)

    return _hosted_call(
        body, name=name, grid=(S // tm, n // tn),
        in_specs=[pl.BlockSpec((tm, dm), lambda i, j: (i, 0)),
                  pl.BlockSpec((1, dm), lambda i, j: (0, 0)),
                  pl.BlockSpec((tn, dm), lambda i, j: (row_block(j), 0))],
        out_specs=[pl.BlockSpec((tm, dm), lambda i, j: (i, 0)),
                   pl.BlockSpec((tm, tn), lambda i, j: (i, j))],
        out_shape=[jax.ShapeDtypeStruct((S, dm), bf16), jax.ShapeDtypeStruct((S, n), bf16)],
        args=(x, g, wt), comm=comm)


def _mm_tn(a, b, name, tk, tn, tmc=2048, into=None, row=0, out_rows=None):
    m, ka = a.shape
    n = b.shape[1]
    tmc = min(tmc, m)
    nk = m // tmc

    def body(a_ref, b_ref, *rest):
        o_ref, acc_ref = rest[-2:]
        k = pl.program_id(2)
        part = lax.dot_general(a_ref[...], b_ref[...], (((0,), (0,)), ((), ())), preferred_element_type=f32)

        @pl.when(k == 0)
        def _():
            acc_ref[...] = part

        @pl.when(k > 0)
        def _():
            acc_ref[...] += part

        @pl.when(k == nk - 1)
        def _():
            o_ref[...] = acc_ref[...].astype(bf16)

    in_specs = [pl.BlockSpec((tmc, tk), lambda i, j, k: (k, i)), pl.BlockSpec((tmc, tn), lambda i, j, k: (k, j))]
    if into is None:
        return pl.pallas_call(
            body, name=name, grid=(ka // tk, n // tn, nk), in_specs=in_specs,
            out_specs=pl.BlockSpec((tk, tn), lambda i, j, k: (i + row, j)),
            out_shape=jax.ShapeDtypeStruct((out_rows or ka, n), bf16),
            scratch_shapes=[pltpu.VMEM((tk, tn), f32)],
            compiler_params=_cparams())(a, b)
    return pl.pallas_call(
        body, name=name, grid=(ka // tk, n // tn, nk), in_specs=[*in_specs, ANY_SPEC],
        out_specs=pl.BlockSpec((tk, tn), lambda i, j, k: (i + row, j)),
        out_shape=jax.ShapeDtypeStruct(into.shape, into.dtype),
        scratch_shapes=[pltpu.VMEM((tk, tn), f32)], input_output_aliases={2: 0},
        compiler_params=_cparams())(a, b, into)


HALO = 16


def _rows_at(ext, o, tc):
    if o == 0:
        return ext[HALO:HALO + tc]
    return pltpu.roll(ext, (-o) % ext.shape[0], 0)[HALO:HALO + tc]


def _halo_specs(tc, S, width, col):
    per = tc // HALO
    last = S // HALO - 1
    return (pl.BlockSpec((tc, width), lambda i: (i, col)),
            pl.BlockSpec((HALO, width), lambda i: (jnp.maximum(i * per - 1, 0), col)),
            pl.BlockSpec((HALO, width), lambda i: (jnp.minimum((i + 1) * per, last), col)))


def _extended(cur_ref, prev_ref, next_ref, i, nsteps):
    prev = jnp.where(i > 0, prev_ref[...].astype(f32), 0.0)
    nxt = jnp.where(i < nsteps - 1, next_ref[...].astype(f32), 0.0)
    return jnp.concatenate([prev, cur_ref[...].astype(f32), nxt], axis=0)


def _conv_fwd(proj, cw, cb, tc=1024):
    S = proj.shape[0]
    tc = min(tc, S)
    nsteps = S // tc

    def body(cur_ref, prev_ref, next_ref, w_ref, b_ref, o_ref):
        ext = _extended(cur_ref, prev_ref, next_ref, pl.program_id(0), nsteps)
        acc = _rows_at(ext, -2, tc) * w_ref[0:1, :]
        for k in range(1, 4):
            acc = acc + _rows_at(ext, k - 2, tc) * w_ref[k:k + 1, :]
        o_ref[...] = acc + b_ref[...]

    return pl.pallas_call(
        body, name="conv_fwd", grid=(nsteps,),
        in_specs=[*_halo_specs(tc, S, D, 0),
                  pl.BlockSpec((4, D), lambda i: (0, 0)), pl.BlockSpec((1, D), lambda i: (0, 0))],
        out_specs=pl.BlockSpec((tc, D), lambda i: (i, 0)),
        out_shape=jax.ShapeDtypeStruct((S, D), f32),
        compiler_params=_cparams())(proj, proj, proj, cw, cb)


def _conv_bwd(duc_f, duc_b, proj, cw, tc=1024, comm=()):
    S = proj.shape[0]
    tc = min(tc, S)
    nsteps = S // tc

    def body(fc, fp, fn, bc, bp, bn, uc_, up, un, w_ref, du_ref, dw_ref, db_ref):
        i = pl.program_id(0)

        @pl.when(i == 0)
        def _():
            dw_ref[...] = jnp.zeros_like(dw_ref)
            db_ref[...] = jnp.zeros_like(db_ref)

        dext = _extended(fc, fp, fn, i, nsteps) + _extended(bc, bp, bn, i, nsteps)
        uext = _extended(uc_, up, un, i, nsteps)
        d = dext[HALO:HALO + tc]
        acc = _rows_at(dext, 2, tc) * w_ref[0:1, :]
        for k in range(1, 4):
            acc = acc + _rows_at(dext, 2 - k, tc) * w_ref[k:k + 1, :]
        du_ref[...] = acc.astype(bf16)
        wrow = lax.broadcasted_iota(jnp.int32, (4, D), 0)
        for k in range(4):
            dw_ref[...] += jnp.where(wrow == k, jnp.sum(d * _rows_at(uext, k - 2, tc), axis=0, keepdims=True), 0.0)
        db_ref[...] += jnp.sum(d, axis=0, keepdims=True)

    return _hosted_call(
        body, name="conv_bwd", grid=(nsteps,),
        in_specs=[*_halo_specs(tc, S, D, 0), *_halo_specs(tc, S, D, 0), *_halo_specs(tc, S, D, 0),
                  pl.BlockSpec((4, D), lambda i: (0, 0))],
        out_specs=[pl.BlockSpec((tc, D), lambda i: (i, 0)),
                   pl.BlockSpec((4, D), lambda i: (0, 0)), pl.BlockSpec((1, D), lambda i: (0, 0))],
        out_shape=[jax.ShapeDtypeStruct((S, D), bf16), jax.ShapeDtypeStruct((4, D), f32),
                   jax.ShapeDtypeStruct((1, D), f32)],
        args=(duc_f, duc_f, duc_f, duc_b, duc_b, duc_b, proj, proj, proj, cw), comm=comm)


def _scan_scratch():
    halves = [pltpu.VMEM((LRU_CHUNK, 128), f32) for _ in range(2 * (LRU_GW // 128))]
    return [*halves, pltpu.VMEM((LRU_CHUNK // 8, LRU_GW), f32), pltpu.VMEM((LRU_CHUNK // 8, LRU_GW), f32)]


def _log_scan(a, b, row, n, reverse, steps):
    for s in steps:
        shift = a.shape[0] - s if reverse else s
        keep = (row < n - s) if reverse else (row >= s)
        a_sh = pltpu.roll(a, shift, 0)
        b_sh = pltpu.roll(b, shift, 0)
        b = jnp.where(keep, a * b_sh + b, b)
        a = jnp.where(keep, a * a_sh, a)
    return a, b


def _scan_chunk(a, b, carry, reverse, *scratch):
    tc, w = a.shape
    ng = tc // 8
    nl = w // 128
    sa_refs, sb_refs, sc_ref, st_ref = scratch[:nl], scratch[nl:2 * nl], scratch[2 * nl], scratch[2 * nl + 1]
    sub = lax.broadcasted_iota(jnp.int32, (8, w), 0)
    ag, bg = [], []
    for k in range(ng):
        ak, bk = _log_scan(a[8 * k:8 * k + 8], b[8 * k:8 * k + 8], sub, 8, reverse, (1, 2, 4))
        ag.append(ak)
        bg.append(bk)
    a = jnp.concatenate(ag, axis=0)
    b = jnp.concatenate(bg, axis=0)
    edge = 0 if reverse else 7
    for i in range(nl):
        sa_refs[i][...] = a[:, 128 * i:128 * (i + 1)]
        sb_refs[i][...] = b[:, 128 * i:128 * (i + 1)]
    ta = jnp.concatenate([r[pl.ds(edge, ng, stride=8), :] for r in sa_refs], axis=1)
    tb = jnp.concatenate([r[pl.ds(edge, ng, stride=8), :] for r in sb_refs], axis=1)
    grow = lax.broadcasted_iota(jnp.int32, (ng, w), 0)
    ta, tb = _log_scan(ta, tb, grow, ng, reverse, [1 << i for i in range(ng.bit_length() - 1)])
    state = tb + ta * carry
    st_ref[...] = state
    if reverse:
        sc_ref[...] = jnp.where(grow == ng - 1, carry, pltpu.roll(state, ng - 1, 0))
    else:
        sc_ref[...] = jnp.where(grow == 0, carry, pltpu.roll(state, 1, 0))
    h = jnp.concatenate([bg[k] + ag[k] * sc_ref[k:k + 1, :] for k in range(ng)], axis=0)
    return h, (st_ref[0:1, :] if reverse else st_ref[ng - 1:ng, :])


def _lru_gates(uc, w, p_ref):
    pre = jnp.dot(uc.astype(bf16), w, preferred_element_type=f32)
    r = _sigmoid(pre[:, :LRU_GW] + p_ref[0, 1:2, :])
    gi = _sigmoid(pre[:, LRU_GW:] + p_ref[0, 2:3, :])
    sp = _softplus(-p_ref[0, 0:1, :])
    log_a = -RGLRU_C * r * sp
    a = jnp.exp(log_a)
    x = 2.0 * log_a
    series = -x * (1.0 + x * (0.5 + x * (1.0 / 6 + x * (1.0 / 24))))
    beta = jnp.sqrt(jnp.maximum(jnp.where(x > -0.0625, series, 1.0 - a * a), 0.0))
    return r, gi, sp, a, beta


def _lru_fwd(uc, wg, lp, reverse, comm=()):
    S = uc.shape[0]
    tc = LRU_CHUNK
    rows = min(LRU_ROWS, S)
    nsub = rows // tc
    nblk = S // rows
    d = 1 if reverse else 0

    def bidx(c):
        return nblk - 1 - c if reverse else c

    def body(uc_ref, w_ref, p_ref, h_ref, carry_ref, *scan_scratch):
        @pl.when(pl.program_id(1) == 0)
        def _():
            carry_ref[...] = jnp.zeros_like(carry_ref)

        carry = carry_ref[...]
        for j in (reversed(range(nsub)) if reverse else range(nsub)):
            sl = slice(j * tc, (j + 1) * tc)
            ucv = uc_ref[sl, :]
            _, gi, _, a, beta = _lru_gates(ucv, w_ref[0], p_ref)
            h, carry = _scan_chunk(a, beta * (gi * ucv), carry, reverse, *scan_scratch)
            h_ref[sl, :] = h.astype(bf16)
        carry_ref[...] = carry

    return _hosted_call(
        body, name="lru_fwd_rev" if reverse else "lru_fwd", grid=(LRU_GROUPS, nblk),
        in_specs=[pl.BlockSpec((rows, LRU_GW), lambda g, c: (bidx(c), g)),
                  pl.BlockSpec((1, LRU_GW, 2 * LRU_GW), lambda g, c: (g, 0, d)),
                  pl.BlockSpec((1, 8, LRU_GW), lambda g, c: (d, 0, g))],
        out_specs=[pl.BlockSpec((rows, LRU_GW), lambda g, c: (bidx(c), g))],
        out_shape=[jax.ShapeDtypeStruct((S, D), bf16)],
        scratch_shapes=[pltpu.VMEM((1, LRU_GW), f32), *_scan_scratch()],
        args=(uc, wg, lp), comm=comm)


def _lru_bwd(uc, dh, h, wg, lp, reverse, comm=()):
    S = uc.shape[0]
    tc = LRU_CHUNK
    rows = min(LRU_ROWS, S)
    nsub = rows // tc
    nblk = S // rows
    d = 1 if reverse else 0
    per = rows // HALO
    last8 = S // HALO - 1

    def bidx(c):
        return c if reverse else nblk - 1 - c

    def halo_idx(c):
        if reverse:
            return jnp.minimum((bidx(c) + 1) * per, last8)
        return jnp.maximum(bidx(c) * per - 1, 0)

    def body(uc_ref, dh_ref, h_ref, halo_ref, w_ref, p_ref, duc_ref, dw_ref, dp_ref, carry_ref, tmp_ref,
             *scan_scratch):
        c = pl.program_id(1)
        bi = bidx(c)

        @pl.when(c == 0)
        def _():
            carry_ref[...] = jnp.zeros_like(carry_ref)
            dw_ref[...] = jnp.zeros_like(dw_ref)
            dp_ref[...] = jnp.zeros_like(dp_ref)

        row = lax.broadcasted_iota(jnp.int32, (tc, LRU_GW), 0)
        carry = carry_ref[...]
        dw = jnp.zeros((LRU_GW, 2 * LRU_GW), f32)
        dsp = jnp.zeros((1, LRU_GW), f32)
        dba = jnp.zeros((1, LRU_GW), f32)
        dbx = jnp.zeros((1, LRU_GW), f32)
        for j in (range(nsub) if reverse else reversed(range(nsub))):
            sl = slice(j * tc, (j + 1) * tc)
            ucv = uc_ref[sl, :]
            ucb = ucv.astype(bf16)
            r, gi, sp, a, beta = _lru_gates(ucv, w_ref[0], p_ref)
            hv = h_ref[sl, :].astype(f32)
            dhv = dh_ref[sl, :].astype(f32)
            if reverse:
                alpha = jnp.where(row == 0, 1.0, pltpu.roll(a, 1, 0))
                gsc, _ = _scan_chunk(alpha, dhv, carry, False, *scan_scratch)
                if j < nsub - 1:
                    edge = h_ref[(j + 1) * tc:(j + 1) * tc + HALO, :].astype(f32)[0:1, :]
                else:
                    edge = jnp.where(bi < nblk - 1, halo_ref[...].astype(f32)[0:1, :], 0.0)
                h_nb = jnp.where(row == tc - 1, edge, pltpu.roll(hv, tc - 1, 0))
            else:
                alpha = jnp.where(row == tc - 1, 1.0, pltpu.roll(a, tc - 1, 0))
                gsc, _ = _scan_chunk(alpha, dhv, carry, True, *scan_scratch)
                if j > 0:
                    edge = h_ref[j * tc - HALO:j * tc, :].astype(f32)[HALO - 1:HALO, :]
                else:
                    edge = jnp.where(bi > 0, halo_ref[...].astype(f32)[HALO - 1:HALO, :], 0.0)
                h_nb = jnp.where(row == 0, edge, pltpu.roll(hv, 1, 0))
            tmp_ref[...] = a * gsc
            carry = tmp_ref[tc - 1:tc, :] if reverse else tmp_ref[0:1, :]

            da = gsc * h_nb
            dbeta = gsc * (gi * ucv)
            dl = da * a - dbeta * (a * a) / beta
            dr = dl * (-RGLRU_C * sp)
            dsp = dsp + jnp.sum(dl * (-RGLRU_C * r), axis=0, keepdims=True)
            dgi = gsc * beta * ucv
            dpre_r = dr * r * (1.0 - r)
            dpre_i = dgi * gi * (1.0 - gi)
            dba = dba + jnp.sum(dpre_r, axis=0, keepdims=True)
            dbx = dbx + jnp.sum(dpre_i, axis=0, keepdims=True)
            dpre = jnp.concatenate([dpre_r, dpre_i], axis=1).astype(bf16)
            back = lax.dot_general(dpre, w_ref[0], (((1,), (1,)), ((), ())), preferred_element_type=f32)
            duc_ref[sl, :] = (gsc * beta * gi + back).astype(bf16)
            dw = dw + lax.dot_general(ucb, dpre, (((0,), (0,)), ((), ())), preferred_element_type=f32)
        carry_ref[...] = carry
        dw_ref[0] += dw
        dlam = -dsp / (1.0 + jnp.exp(p_ref[0, 0:1, :]))
        prow = lax.broadcasted_iota(jnp.int32, (8, LRU_GW), 0)
        dp_ref[...] += (jnp.where(prow == 0, dlam, 0.0) + jnp.where(prow == 1, dba, 0.0)
                        + jnp.where(prow == 2, dbx, 0.0))

    chunk = pl.BlockSpec((rows, LRU_GW), lambda g, c: (bidx(c), g))
    return _hosted_call(
        body, name="lru_bwd_rev" if reverse else "lru_bwd", grid=(LRU_GROUPS, nblk),
        in_specs=[chunk, chunk, chunk,
                  pl.BlockSpec((HALO, LRU_GW), lambda g, c: (halo_idx(c), g)),
                  pl.BlockSpec((1, LRU_GW, 2 * LRU_GW), lambda g, c: (g, 0, d)),
                  pl.BlockSpec((1, 8, LRU_GW), lambda g, c: (d, 0, g))],
        out_specs=[chunk,
                   pl.BlockSpec((1, LRU_GW, 2 * LRU_GW), lambda g, c: (g, 0, 0)),
                   pl.BlockSpec((8, LRU_GW), lambda g, c: (0, g))],
        out_shape=[jax.ShapeDtypeStruct((S, D), bf16),
                   jax.ShapeDtypeStruct((LRU_GROUPS, LRU_GW, 2 * LRU_GW), f32),
                   jax.ShapeDtypeStruct((8, D), f32)],
        scratch_shapes=[pltpu.VMEM((1, LRU_GW), f32), pltpu.VMEM((tc, LRU_GW), f32), *_scan_scratch()],
        args=(uc, dh, h, h, wg, lp), comm=comm)


def _slope(h):
    return 2.0 ** (-8.0 * (h + 1.0) / N_HEADS)


ATT_QB = 2


def _kv_specs(nb, col):
    return [pl.BlockSpec((BLK, N_KV * HEAD_DIM), lambda n: (jnp.maximum(ATT_QB * n - 1, 0), col)),
            pl.BlockSpec((ATT_QB * BLK, N_KV * HEAD_DIM), lambda n: (n, col)),
            pl.BlockSpec((BLK, N_KV * HEAD_DIM), lambda n: (jnp.minimum(ATT_QB * (n + 1), nb - 1), col))]


def _key_blocks(prev_ref, cur_ref, next_ref):
    return [prev_ref[...], *[cur_ref[BLK * s:BLK * (s + 1), :] for s in range(ATT_QB)], next_ref[...]]


def _dup_windows(r0, r1, r2):
    left = lax.broadcasted_iota(jnp.int32, (3 * BLK, 128), 1) < HEAD_DIM
    win = jnp.concatenate([r0, r1, r2], axis=0)
    out = []
    for i in range(N_KV // 2):
        t = win[:, i * 128:(i + 1) * 128]
        r = pltpu.roll(t, HEAD_DIM, 1)
        out += [jnp.where(left, t, r).astype(bf16), jnp.where(left, r, t).astype(bf16)]
    return out


def _attn_bias_init(bias_ref):
    k_loc = lax.broadcasted_iota(jnp.int32, (3 * BLK, BLK), 0)
    q_loc = lax.broadcasted_iota(jnp.int32, (3 * BLK, BLK), 1)
    adist = jnp.abs(q_loc + BLK - k_loc)
    adf = adist.astype(f32)
    for e in range(3):
        ok = adist <= WINDOW
        if e == 0:
            ok = ok & (k_loc >= BLK)
        if e == 2:
            ok = ok & (k_loc < 2 * BLK)
        for kv in range(N_KV):
            bias_ref[e, kv] = jnp.concatenate(
                [jnp.where(ok, (-_slope(4 * kv + j)) * adf, NEG_INF) for j in range(4)], axis=1)


def _stack_heads(ref, sub, kv, scale):
    left = lax.broadcasted_iota(jnp.int32, (BLK, 128), 1) < HEAD_DIM
    rows = []
    for pp in range(2):
        t = ref[BLK * sub:BLK * (sub + 1), (2 * kv + pp) * 128:(2 * kv + pp + 1) * 128]
        if scale != 1.0:
            t = t * scale
        zero = jnp.zeros_like(t)
        rows += [jnp.where(left, t, zero).astype(bf16), jnp.where(left, zero, t).astype(bf16)]
    return jnp.concatenate(rows, axis=0)


def _attn_softmax(qs, k2, bias, sink_ref, kv, stats=None):
    sink = jnp.concatenate([jnp.full((1, BLK), sink_ref[0, 4 * kv + j], f32) for j in range(4)], axis=1)
    s = lax.dot_general(k2, qs, (((1,), (1,)), ((), ())), preferred_element_type=f32) + bias
    m = jnp.maximum(jnp.max(s, axis=0, keepdims=True), sink) if stats is None else stats[0]
    p = jnp.exp(s - m)
    ps = jnp.exp(sink - m)
    inv = 1.0 / (jnp.sum(p, axis=0, keepdims=True) + ps) if stats is None else stats[1]
    return p, ps, m, inv


def _pair_tiles(t):
    return [jnp.concatenate([t[:HEAD_DIM, 256 * pp:256 * pp + 128],
                             t[HEAD_DIM:, 256 * pp + 128:256 * pp + 256]], axis=0).T for pp in range(2)]


def _attn_fwd(proj, sink, comm=()):
    S = proj.shape[0]
    nb = S // BLK
    assert nb >= 2 and nb % ATT_QB == 0

    def body(q_ref, k0, k1, k2_, v0, v1, v2_, sink_ref, o_ref, st_ref, bias_ref):
        n = pl.program_id(0)

        @pl.when(n == 0)
        def _():
            _attn_bias_init(bias_ref)

        kb = _key_blocks(k0, k1, k2_)
        vb = _key_blocks(v0, v1, v2_)
        for sub in range(ATT_QB):
            blk = ATT_QB * n + sub
            e = jnp.where(blk == 0, 0, jnp.where(blk == nb - 1, 2, 1))
            kk = _dup_windows(*kb[sub:sub + 3])
            vv = _dup_windows(*vb[sub:sub + 3])
            tiles = []
            for kv in range(N_KV):
                qs = _stack_heads(q_ref, sub, kv, HEAD_DIM ** -0.5)
                p, _, m, inv = _attn_softmax(qs, kk[kv], bias_ref[e, kv], sink_ref, kv)
                st_ref[sub, kv:kv + 1, :] = m
                st_ref[sub, N_KV + kv:N_KV + kv + 1, :] = inv
                ot = lax.dot_general(vv[kv], p.astype(bf16), (((0,), (0,)), ((), ())), preferred_element_type=f32)
                tiles += _pair_tiles(ot * inv)
            o_ref[BLK * sub:BLK * (sub + 1), :] = jnp.concatenate(tiles, axis=1).astype(bf16)

    return _hosted_call(
        body, name="attn_fwd", grid=(nb // ATT_QB,),
        in_specs=[pl.BlockSpec((ATT_QB * BLK, D), lambda n: (n, C_Q // D)),
                  *_kv_specs(nb, C_K // (N_KV * HEAD_DIM)), *_kv_specs(nb, C_V // (N_KV * HEAD_DIM)),
                  pl.BlockSpec(memory_space=pltpu.SMEM)],
        out_specs=[pl.BlockSpec((ATT_QB * BLK, D), lambda n: (n, 0)),
                   pl.BlockSpec((ATT_QB, 2 * N_KV, 4 * BLK), lambda n: (n, 0, 0))],
        out_shape=[jax.ShapeDtypeStruct((S, D), bf16), jax.ShapeDtypeStruct((nb, 2 * N_KV, 4 * BLK), f32)],
        scratch_shapes=[pltpu.VMEM((3, N_KV, 3 * BLK, 4 * BLK), f32)],
        args=(proj, proj, proj, proj, proj, proj, proj, sink), comm=comm)


def _attn_bwd(proj, sink, dyb, stats, comm=()):
    S = proj.shape[0]
    nb = S // BLK
    assert nb >= 2 and nb % ATT_QB == 0
    nsteps = nb // ATT_QB

    def body(q_ref, k0, k1, k2_, v0, v1, v2_, sink_ref, do_ref, st_ref, dq_ref, dk_out, dv_out, ds_ref,
             bias_ref, dk_ref, dv_ref, dsk_ref):
        n = pl.program_id(0)

        @pl.when(n == 0)
        def _():
            _attn_bias_init(bias_ref)
            dk_ref[...] = jnp.zeros_like(dk_ref)
            dv_ref[...] = jnp.zeros_like(dv_ref)
            dsk_ref[...] = jnp.zeros_like(dsk_ref)

        kb = _key_blocks(k0, k1, k2_)
        vb = _key_blocks(v0, v1, v2_)
        left3 = lax.broadcasted_iota(jnp.int32, (3 * BLK, 128), 1) < HEAD_DIM
        for sub in range(ATT_QB):
            blk = ATT_QB * n + sub
            e = jnp.where(blk == 0, 0, jnp.where(blk == nb - 1, 2, 1))
            kk = _dup_windows(*kb[sub:sub + 3])
            vv = _dup_windows(*vb[sub:sub + 3])
            start = pl.multiple_of(blk * BLK, BLK)
            dq_tiles, dks, dvs = [], [], []
            for kv in range(N_KV):
                qs = _stack_heads(q_ref, sub, kv, HEAD_DIM ** -0.5)
                dos = _stack_heads(do_ref, sub, kv, 1.0)
                stats = (st_ref[sub, kv:kv + 1, :], st_ref[sub, N_KV + kv:N_KV + kv + 1, :])
                p, ps, _, inv = _attn_softmax(qs, kk[kv], bias_ref[e, kv], sink_ref, kv, stats)
                pn = p * inv
                dp = lax.dot_general(vv[kv], dos, (((1,), (1,)), ((), ())), preferred_element_type=f32)
                delta = jnp.sum(pn * dp, axis=0, keepdims=True)
                dsc = (pn * (dp - delta)).astype(bf16)
                dsk_ref[kv:kv + 1, :] += delta * (ps * inv)
                dqt = lax.dot_general(kk[kv], dsc, (((0,), (0,)), ((), ())), preferred_element_type=f32)
                dq_tiles += _pair_tiles(dqt * (HEAD_DIM ** -0.5))
                dk = jnp.dot(dsc, qs, preferred_element_type=f32)
                dv = jnp.dot(pn.astype(bf16), dos, preferred_element_type=f32)
                dks.append(dk + pltpu.roll(dk, HEAD_DIM, 1))
                dvs.append(dv + pltpu.roll(dv, HEAD_DIM, 1))
            for jp in range(N_KV // 2):
                cols = slice(jp * 128, (jp + 1) * 128)
                dk_ref[pl.ds(start, 3 * BLK), cols] += jnp.where(left3, dks[2 * jp], dks[2 * jp + 1])
                dv_ref[pl.ds(start, 3 * BLK), cols] += jnp.where(left3, dvs[2 * jp], dvs[2 * jp + 1])
            dq_ref[BLK * sub:BLK * (sub + 1), :] = jnp.concatenate(dq_tiles, axis=1).astype(bf16)

        @pl.when(n == nsteps - 1)
        def _():
            pltpu.sync_copy(dk_ref, dk_out)
            pltpu.sync_copy(dv_ref, dv_out)
            lane = lax.broadcasted_iota(jnp.int32, (1, 128), 1)
            dsink = jnp.zeros((1, 128), f32)
            for h in range(N_HEADS):
                part = dsk_ref[h // 4:h // 4 + 1, (h % 4) * BLK:(h % 4 + 1) * BLK]
                dsink = dsink + jnp.where(lane == h, -jnp.sum(part), 0.0)
            ds_ref[...] = dsink

    acc = jax.ShapeDtypeStruct((S + 2 * BLK, N_KV * HEAD_DIM), f32)
    return _hosted_call(
        body, name="attn_bwd", grid=(nsteps,),
        in_specs=[pl.BlockSpec((ATT_QB * BLK, D), lambda n: (n, C_Q // D)),
                  *_kv_specs(nb, C_K // (N_KV * HEAD_DIM)), *_kv_specs(nb, C_V // (N_KV * HEAD_DIM)),
                  pl.BlockSpec(memory_space=pltpu.SMEM),
                  pl.BlockSpec((ATT_QB * BLK, D), lambda n: (n, 0)),
                  pl.BlockSpec((ATT_QB, 2 * N_KV, 4 * BLK), lambda n: (n, 0, 0))],
        out_specs=[pl.BlockSpec((ATT_QB * BLK, D), lambda n: (n, 0)), ANY_SPEC, ANY_SPEC,
                   pl.BlockSpec((1, 128), lambda n: (0, 0))],
        out_shape=[jax.ShapeDtypeStruct((S, D), bf16), acc, acc, jax.ShapeDtypeStruct((1, 128), f32)],
        scratch_shapes=[pltpu.VMEM((3, N_KV, 3 * BLK, 4 * BLK), f32), pltpu.VMEM(acc.shape, f32),
                        pltpu.VMEM(acc.shape, f32), pltpu.VMEM((8, 4 * BLK), f32)],
        args=(proj, proj, proj, proj, proj, proj, proj, sink, dyb, stats), comm=comm)


def _merge_parts(hf, hb, g, z0, z1, yb, bg):
    g0 = _sigmoid(z0.astype(f32) + bg[:, :D])
    g1 = _sigmoid(z1.astype(f32) + bg[:, D:])
    gelu, dgelu = _gelu_and_grad(g.astype(f32))
    hs = hf.astype(f32) + hb.astype(f32)
    ya = hs * gelu
    return g0, g1, gelu, dgelu, hs, ya


def _merge_outproj(x, hf, hb, proj, yb, bg, w_out, tm=1024):
    S = x.shape[0]
    tm = min(tm, S)

    def body(x_ref, hf_ref, hb_ref, g_ref, z0_ref, z1_ref, yb_ref, bg_ref, w_ref, mg_ref, x1_ref):
        ybv = yb_ref[...].astype(f32)
        g0, g1, _, _, _, ya = _merge_parts(hf_ref[...], hb_ref[...], g_ref[...], z0_ref[...], z1_ref[...],
                                           ybv, bg_ref[...])
        mg = (g0 * ya + g1 * ybv).astype(bf16)
        mg_ref[...] = mg
        x1_ref[...] = x_ref[...] + jnp.dot(mg, w_ref[...], preferred_element_type=f32)

    row = pl.BlockSpec((tm, D), lambda i: (i, 0))
    return pl.pallas_call(
        body, name="merge_outproj", grid=(S // tm,),
        in_specs=[row, row, row,
                  pl.BlockSpec((tm, D), lambda i: (i, C_G // D)),
                  pl.BlockSpec((tm, D), lambda i: (i, C_Z0 // D)),
                  pl.BlockSpec((tm, D), lambda i: (i, C_Z1 // D)),
                  row, pl.BlockSpec((1, 2 * D), lambda i: (0, 0)), pl.BlockSpec((D, D), lambda i: (0, 0))],
        out_specs=[row, row],
        out_shape=[jax.ShapeDtypeStruct((S, D), bf16), jax.ShapeDtypeStruct((S, D), f32)],
        compiler_params=_cparams())(x, hf, hb, proj, proj, proj, yb, bg, w_out)


def _ffn_out_loss(gu, x1, w_fo, g3, tgt, tm=256):
    S = x1.shape[0]
    tm = min(tm, S)

    def body(gt_ref, up_ref, x1_ref, w_ref, g_ref, t_ref, ff_ref, dx_ref, dxb_ref, loss_ref, dg_ref,
             dgt_ref, dup_ref):
        @pl.when(pl.program_id(0) == 0)
        def _():
            loss_ref[...] = jnp.zeros_like(loss_ref)
            dg_ref[...] = jnp.zeros_like(dg_ref)

        gt = gt_ref[...].astype(f32)
        up = up_ref[...].astype(f32)
        sg = _sigmoid(gt)
        silu = gt * sg
        ff = (silu * up).astype(bf16)
        ff_ref[...] = ff
        x2 = x1_ref[...] + jnp.dot(ff, w_ref[...], preferred_element_type=f32)
        gv = g_ref[...]
        r = lax.rsqrt(jnp.mean(x2 * x2, axis=-1, keepdims=True) + EPS)
        xh = x2 * r
        diff = xh * gv - t_ref[...]
        loss_ref[...] += (0.5 / D) * jnp.sum(diff * diff)
        dy = diff * (1.0 / D)
        dg_ref[...] += jnp.sum(dy * xh, axis=0, keepdims=True)
        dxh = dy * gv
        dx = r * (dxh - xh * jnp.mean(dxh * xh, axis=-1, keepdims=True))
        dx_ref[...] = dx
        dxb = dx.astype(bf16)
        dxb_ref[...] = dxb
        dff = lax.dot_general(dxb, w_ref[...], (((1,), (1,)), ((), ())), preferred_element_type=f32)
        dup_ref[...] = (dff * silu).astype(bf16)
        dgt_ref[...] = ((dff * up) * (sg * (1.0 + gt * (1.0 - sg)))).astype(bf16)

    row = pl.BlockSpec((tm, D), lambda i: (i, 0))
    vec = pl.BlockSpec((1, D), lambda i: (0, 0))
    wide = pl.BlockSpec((tm, D_FF), lambda i: (i, 0))
    wide_shape = jax.ShapeDtypeStruct((S, D_FF), bf16)
    return pl.pallas_call(
        body, name="ffn_out_loss", grid=(S // tm,),
        in_specs=[wide, pl.BlockSpec((tm, D_FF), lambda i: (i, 1)),
                  row, pl.BlockSpec((D_FF, D), lambda i: (0, 0)), vec, row],
        out_specs=[wide, row, row, pl.BlockSpec((1, 128), lambda i: (0, 0)), vec, wide, wide],
        out_shape=[wide_shape, jax.ShapeDtypeStruct((S, D), f32), jax.ShapeDtypeStruct((S, D), bf16),
                   jax.ShapeDtypeStruct((1, 128), f32), jax.ShapeDtypeStruct((1, D), f32), wide_shape, wide_shape],
        compiler_params=_cparams())(gu, gu, x1, w_fo, g3, tgt)


def _proj_bwd(pieces, wt, xres, g, dres, name, tm=512, comm=()):
    S = xres.shape[0]
    tm = min(tm, S)
    np_ = len(pieces)

    def body(*refs):
        p_refs = refs[:np_]
        w_refs = refs[np_:2 * np_]
        x_ref, g_ref, dres_ref, dx_ref, dxb_ref, dg_ref = refs[2 * np_:]

        @pl.when(pl.program_id(0) == 0)
        def _():
            dg_ref[...] = jnp.zeros_like(dg_ref)

        dn = jnp.dot(p_refs[0][...], w_refs[0][...], preferred_element_type=f32)
        for pr, wr in zip(p_refs[1:], w_refs[1:]):
            dn = dn + jnp.dot(pr[...], wr[...], preferred_element_type=f32)
        dxn, dgc = _rms_bwd(dn, x_ref[...], g_ref[...])
        dx = dres_ref[...] + dxn
        dx_ref[...] = dx
        dxb_ref[...] = dx.astype(bf16)
        dg_ref[...] += jnp.sum(dgc, axis=0, keepdims=True)

    row = pl.BlockSpec((tm, D), lambda i: (i, 0))
    vec = pl.BlockSpec((1, D), lambda i: (0, 0))
    return _hosted_call(
        body, name=name, grid=(S // tm,),
        in_specs=[*[pl.BlockSpec((tm, wd), functools.partial(lambda i, cb: (i, cb), cb=acb))
                    for _, acb, _, wd in pieces],
                  *[pl.BlockSpec((wd, D), functools.partial(lambda i, rb: (rb, 0), rb=wrb))
                    for _, _, wrb, wd in pieces],
                  row, vec, row],
        out_specs=[row, row, vec],
        out_shape=[jax.ShapeDtypeStruct((S, D), f32), jax.ShapeDtypeStruct((S, D), bf16),
                   jax.ShapeDtypeStruct((1, D), f32)],
        args=(*[p[0] for p in pieces], *[wt] * np_, xres, g, dres), comm=comm)


def _outproj_bwd(dx1b, w_out, hf, hb, proj, yb, bg, tm=1024):
    S = dx1b.shape[0]
    tm = min(tm, S)

    def body(dx_ref, w_ref, hf_ref, hb_ref, g_ref, z0_ref, z1_ref, yb_ref, bg_ref,
             dh_ref, dg_ref, dz_ref, dyb_ref, dbg_ref):
        @pl.when(pl.program_id(0) == 0)
        def _():
            dbg_ref[...] = jnp.zeros_like(dbg_ref)

        dm = lax.dot_general(dx_ref[...], w_ref[...], (((1,), (1,)), ((), ())), preferred_element_type=f32)
        ybv = yb_ref[...].astype(f32)
        g0, g1, gelu, dgelu, hs, ya = _merge_parts(hf_ref[...], hb_ref[...], g_ref[...], z0_ref[...],
                                                   z1_ref[...], ybv, bg_ref[...])
        dya = dm * g0
        dh_ref[...] = (dya * gelu).astype(bf16)
        dg_ref[...] = (dya * hs * dgelu).astype(bf16)
        dyb_ref[...] = (dm * g1).astype(bf16)
        dz0 = (dm * ya) * (g0 * (1.0 - g0))
        dz1 = (dm * ybv) * (g1 * (1.0 - g1))
        dz = jnp.concatenate([dz0, dz1], axis=1)
        dz_ref[...] = dz.astype(bf16)
        dbg_ref[...] += jnp.sum(dz, axis=0, keepdims=True)

    row = pl.BlockSpec((tm, D), lambda i: (i, 0))
    return pl.pallas_call(
        body, name="outproj_bwd", grid=(S // tm,),
        in_specs=[row, pl.BlockSpec((D, D), lambda i: (0, 0)), row, row,
                  pl.BlockSpec((tm, D), lambda i: (i, C_G // D)),
                  pl.BlockSpec((tm, D), lambda i: (i, C_Z0 // D)),
                  pl.BlockSpec((tm, D), lambda i: (i, C_Z1 // D)),
                  row, pl.BlockSpec((1, 2 * D), lambda i: (0, 0))],
        out_specs=[row, row, pl.BlockSpec((tm, 2 * D), lambda i: (i, 0)), row,
                   pl.BlockSpec((1, 2 * D), lambda i: (0, 0))],
        out_shape=[jax.ShapeDtypeStruct((S, D), bf16), jax.ShapeDtypeStruct((S, D), bf16),
                   jax.ShapeDtypeStruct((S, 2 * D), bf16), jax.ShapeDtypeStruct((S, D), bf16),
                   jax.ShapeDtypeStruct((1, 2 * D), f32)],
        compiler_params=_cparams())(dx1b, w_out, hf, hb, proj, proj, proj, yb, bg)


def _block_diag_groups(w):
    w4 = w.reshape(LRU_GROUPS, 4, LRU_BLOCK, LRU_BLOCK)
    eye = jnp.eye(4, dtype=w.dtype)
    return jnp.einsum("ghij,hk->ghikj", w4, eye).reshape(LRU_GROUPS, LRU_GW, LRU_GW)


def _diag_blocks(dw):
    d5 = dw.reshape(LRU_GROUPS, 4, LRU_BLOCK, 4, LRU_BLOCK)
    return jnp.stack([d5[:, h, :, h, :] for h in range(4)], axis=1).reshape(LRU_HEADS, LRU_BLOCK, LRU_BLOCK)


def _local_step(x, tgt, small, env, before=lambda name: (), after=lambda name, got: None):
    S = x.shape[0]
    g1, g2, g3 = small["norm_mix_g"], small["norm_ffn_g"], small["norm_final_g"]
    bg, cb, sink = small["b_gate"], small["conv_b"], small["attn_sink"]

    def hosted(name, fn, *args, **kw):
        outs, got = fn(*args, comm=tuple(before(name)), **kw)
        after(name, got)
        return outs

    (xn,) = hosted("norm_x", _rmsnorm_bf16, x, g1, "norm_x")
    cw = small["conv_w"]
    wg = jnp.concatenate([_block_diag_groups(small["lru_wa"][0]), _block_diag_groups(small["lru_wx"][0]),
                          _block_diag_groups(small["lru_wa"][1]), _block_diag_groups(small["lru_wx"][1])],
                         axis=2).astype(bf16)
    zeros5 = jnp.zeros((5, D), f32)
    lp = jnp.stack([jnp.concatenate([small["lru_lambda"][d:d + 1], small["lru_ba"][d:d + 1],
                                     small["lru_bx"][d:d + 1], zeros5], axis=0) for d in range(2)])
    (proj,) = hosted("inproj", _matmul_t, xn, env["w_in_t"], "inproj", tm=2048, tn=512,
                     row_block=lambda j: jnp.where(j < 6, j, jnp.where(j < 10, j + 1, 6)))
    uc = _conv_fwd(proj, cw, cb)
    (hf,), _ = _lru_fwd(uc, wg, lp, False)
    (hb,), _ = _lru_fwd(uc, wg, lp, True)
    yb, attn_stats = hosted("attn_fwd", _attn_fwd, proj, sink)
    merged, x1 = _merge_outproj(x, hf, hb, proj, yb, bg, env["w_out"])
    (xn2, gu), _ = _norm_matmul(x1, g2, env["w_fi_t"], "norm_ffn_in", tn=D_FF)
    ff, dx2, dx2b, loss, dg3, dgt, dup = _ffn_out_loss(gu, x1, env["w_fo"], g3, tgt)

    env["dw_fo"] = _mm_tn(ff, dx2b, "dw_ffn_out", tk=1408, tn=1024)
    dx1, dx1b, dg2 = hosted("ffn_in_bwd", _proj_bwd, [(dgt, 0, 0, D_FF), (dup, 0, 1, D_FF)], env["w_fi_t"],
                            x1, g2, dx2, "ffn_in_bwd")
    dw_gate = _mm_tn(dgt, xn2, "dw_ffn_in_gate", tk=1408, tn=1024, out_rows=2 * D_FF)
    env["dw_fi_t"] = _mm_tn(dup, xn2, "dw_ffn_in_up", tk=1408, tn=1024, into=dw_gate, row=D_FF // 1408)
    env["dw_out"] = _mm_tn(merged, dx1b, "dw_out", tk=1024, tn=1024)
    dh, dgl, dz, dyb, dbg = _outproj_bwd(dx1b, env["w_out"], hf, hb, proj, yb, bg)
    dq, dk2, dv2, dsink = hosted("attn_bwd", _attn_bwd, proj, sink, dyb, attn_stats)
    dkv = jnp.concatenate([dk2[BLK:BLK + S], dv2[BLK:BLK + S]], axis=1).astype(bf16)
    duc_f, dwg_f, dp_f = hosted("lru_bwd", _lru_bwd, uc, dh, hf, wg, lp, False)
    (duc_b, dwg_b, dp_b), _ = _lru_bwd(uc, dh, hb, wg, lp, True)
    env["grads_early"] = {
        "loss": loss[:, :1], "b_gate": dbg,
        "lru_lambda": jnp.concatenate([dp_f[0:1], dp_b[0:1]], axis=0),
        "lru_wa": jnp.stack([_diag_blocks(dwg_f[:, :, :LRU_GW]), _diag_blocks(dwg_b[:, :, :LRU_GW])]),
        "lru_ba": jnp.concatenate([dp_f[1:2], dp_b[1:2]], axis=0),
        "lru_wx": jnp.stack([_diag_blocks(dwg_f[:, :, LRU_GW:]), _diag_blocks(dwg_b[:, :, LRU_GW:])]),
        "lru_bx": jnp.concatenate([dp_f[2:3], dp_b[2:3]], axis=0),
        "attn_sink": dsink[:, :N_HEADS], "norm_ffn_g": dg2, "norm_final_g": dg3,
    }
    du, dcw, dcb = hosted("conv_bwd", _conv_bwd, duc_f, duc_b, proj, cw)
    dw_in = _mm_tn(du, xn, "dw_in_u", tk=1024, tn=1024, out_rows=IN_W)
    dw_in = _mm_tn(dgl, xn, "dw_in_g", tk=1024, tn=1024, into=dw_in, row=1)
    dw_in = _mm_tn(dq, xn, "dw_in_q", tk=1024, tn=1024, into=dw_in, row=2)
    dw_in = _mm_tn(dkv, xn, "dw_in_kv", tk=512, tn=1024, into=dw_in, row=3072 // 512)
    env["dw_in_t"] = _mm_tn(dz, xn, "dw_in_z", tk=512, tn=1024, into=dw_in, row=3584 // 512)
    col_pieces = [(du, 0, 0, D), (dgl, 0, 1, D), (dq, 0, 2, D), (dkv, 0, 3072 // 512, 512),
                  *[(dz, i, 3584 // 512 + i, 512) for i in range(4)]]
    dx, _, dg1 = hosted("inproj_bwd", _proj_bwd, col_pieces, env["w_in_t"], x, g1, dx1, "inproj_bwd")

    grads = dict(env["grads_early"], norm_mix_g=dg1, conv_w=dcw, conv_b=dcb)
    return dx, grads


def _adamw(gparts, w, m, v, name, tr=256):
    n, rows, cols = gparts.shape
    tr = _div_tile(rows, tr)
    c1 = 1.0 - ADAM_B1 ** ADAM_STEP
    c2 = 1.0 - ADAM_B2 ** ADAM_STEP

    def body(g_ref, w_ref, m_ref, v_ref, go_ref, d_ref, mo_ref, vo_ref):
        g = g_ref[0].astype(f32)
        for j in range(1, n):
            g = g + g_ref[j].astype(f32)
        mn = ADAM_B1 * m_ref[0] + (1.0 - ADAM_B1) * g
        vn = ADAM_B2 * v_ref[0] + (1.0 - ADAM_B2) * (g * g)
        m_hat = mn / c1
        v_hat = vn / c2
        go_ref[0] = g
        d_ref[0] = -ADAM_LR * (m_hat / (jnp.sqrt(v_hat) + ADAM_EPS) + ADAM_WD * w_ref[0])
        mo_ref[0] = mn
        vo_ref[0] = vn

    blk = pl.BlockSpec((1, tr, cols), lambda i: (0, i, 0))
    shp = jax.ShapeDtypeStruct((1, rows, cols), f32)
    return pl.pallas_call(
        body, name=name, grid=(rows // tr,),
        in_specs=[pl.BlockSpec((n, tr, cols), lambda i: (0, i, 0)), blk, blk, blk],
        out_specs=[blk, blk, blk, blk], out_shape=[shp, shp, shp, shp],
        compiler_params=_cparams())(gparts, w, m, v)


def _sum_parts(parts, name):
    n, rows, cols = parts.shape

    def body(p_ref, o_ref):
        acc = p_ref[0].astype(f32)
        for j in range(1, n):
            acc = acc + p_ref[j].astype(f32)
        o_ref[...] = acc

    return pl.pallas_call(
        body, name=name, out_shape=jax.ShapeDtypeStruct((rows, cols), f32),
        compiler_params=_cparams())(parts)


def _pack_rows(arrs, dtype=f32):
    rows, spans, at = [], [], 0
    for a in arrs:
        flat = a.reshape(-1).astype(dtype)
        nr = -(-flat.shape[0] // 1024)
        rows.append(jnp.pad(flat, (0, nr * 1024 - flat.shape[0])).reshape(nr, 1024))
        spans.append((at, nr))
        at += nr
    pad = (-at) % 16
    if pad:
        rows.append(jnp.zeros((pad, 1024), dtype))
    return jnp.concatenate(rows, axis=0), spans


def _unpack_rows(packed, spans, shapes):
    out = []
    for (at, nr), shp in zip(spans, shapes):
        n = math.prod(shp)
        out.append(packed[at:at + nr].reshape(-1)[:n].reshape(shp))
    return out


BIG = ("w_in", "w_out", "w_ffn_in", "w_ffn_out")
SMALL_REPL = ("norm_mix_g", "b_gate", "conv_b", "lru_wa", "lru_wx", "attn_sink", "norm_ffn_g", "norm_final_g")
SMALL_SHARD = ("conv_w", "lru_lambda", "lru_ba", "lru_bx")
ORDER = ("norm_mix_g", "w_in", "b_gate", "conv_w", "conv_b", "lru_lambda", "lru_wa", "lru_ba", "lru_wx",
         "lru_bx", "attn_sink", "w_out", "norm_ffn_g", "w_ffn_in", "w_ffn_out", "norm_final_g")
EARLY_F32 = ("loss", "b_gate", "lru_lambda", "lru_ba", "lru_bx", "attn_sink", "norm_ffn_g", "norm_final_g")
EARLY_BF16 = ("lru_wa", "lru_wx")
LATE = ("norm_mix_g", "conv_w", "conv_b")


def kernel(x, norm_mix_g, w_in, b_gate, conv_w, conv_b, lru_lambda, lru_wa, lru_ba, lru_wx, lru_bx, attn_sink, w_out, norm_ffn_g, w_ffn_in, w_ffn_out, norm_final_g, loss_target, m_norm_mix_g, m_w_in, m_b_gate, m_conv_w, m_conv_b, m_lru_lambda, m_lru_wa, m_lru_ba, m_lru_wx, m_lru_bx, m_attn_sink, m_w_out, m_norm_ffn_g, m_w_ffn_in, m_w_ffn_out, m_norm_final_g, v_norm_mix_g, v_w_in, v_b_gate, v_conv_w, v_conv_b, v_lru_lambda, v_lru_wa, v_lru_ba, v_lru_wx, v_lru_bx, v_attn_sink, v_w_out, v_norm_ffn_g, v_w_ffn_in, v_w_ffn_out, v_norm_final_g):
    w = dict(norm_mix_g=norm_mix_g, w_in=w_in, b_gate=b_gate, conv_w=conv_w, conv_b=conv_b, lru_lambda=lru_lambda,
             lru_wa=lru_wa, lru_ba=lru_ba, lru_wx=lru_wx, lru_bx=lru_bx, attn_sink=attn_sink, w_out=w_out,
             norm_ffn_g=norm_ffn_g, w_ffn_in=w_ffn_in, w_ffn_out=w_ffn_out, norm_final_g=norm_final_g)
    m = dict(norm_mix_g=m_norm_mix_g, w_in=m_w_in, b_gate=m_b_gate, conv_w=m_conv_w, conv_b=m_conv_b,
             lru_lambda=m_lru_lambda, lru_wa=m_lru_wa, lru_ba=m_lru_ba, lru_wx=m_lru_wx, lru_bx=m_lru_bx,
             attn_sink=m_attn_sink, w_out=m_w_out, norm_ffn_g=m_norm_ffn_g, w_ffn_in=m_w_ffn_in,
             w_ffn_out=m_w_ffn_out, norm_final_g=m_norm_final_g)
    v = dict(norm_mix_g=v_norm_mix_g, w_in=v_w_in, b_gate=v_b_gate, conv_w=v_conv_w, conv_b=v_conv_b,
             lru_lambda=v_lru_lambda, lru_wa=v_lru_wa, lru_ba=v_lru_ba, lru_wx=v_lru_wx, lru_bx=v_lru_bx,
             attn_sink=v_attn_sink, w_out=v_w_out, norm_ffn_g=v_norm_ffn_g, w_ffn_in=v_w_ffn_in,
             w_ffn_out=v_w_ffn_out, norm_final_g=v_norm_final_g)
    me = 4 * lax.axis_index("x") + 2 * lax.axis_index("y") + lax.axis_index("c")

    def shard_t(a):
        return jnp.swapaxes(a[0], 0, 1)

    def rows_parts(g):
        return g.reshape(N_DEV, -1, g.shape[1])

    shard_rows = jnp.concatenate([w[n][0] for n in SMALL_SHARD], axis=0)
    small = {n: w[n] for n in ("norm_mix_g", "b_gate", "conv_b", "attn_sink", "norm_ffn_g")}
    small["lru_wa"], small["lru_wx"] = lru_wa[0], lru_wx[0]
    small["norm_final_g"] = norm_final_g.reshape(1, D)
    env, recv = {}, {}

    def before(name):
        if name == "norm_x":
            return [(shard_t(w_in).astype(bf16), False), (shard_rows, False)]
        if name == "inproj":
            return [(w_out[0].astype(bf16), False), (w_ffn_out[0].astype(bf16), False)]
        if name == "attn_fwd":
            return [(shard_t(w_ffn_in).astype(bf16), False)]
        if name == "ffn_in_bwd":
            return [(rows_parts(env["dw_fo"]), True)]
        if name == "attn_bwd":
            return [(rows_parts(env["dw_out"]), True)]
        if name == "lru_bwd":
            return [(rows_parts(env["dw_fi_t"]), True)]
        if name == "conv_bwd":
            ge = env["grads_early"]
            p32, env["early_f32_spans"] = _pack_rows([ge[n] for n in EARLY_F32])
            p16, env["early_bf16_spans"] = _pack_rows([ge[n] for n in EARLY_BF16], bf16)
            return [(p32, False), (p16, False)]
        if name == "inproj_bwd":
            return [(rows_parts(env["dw_in_t"]), True)]
        return []

    def after(name, got):
        if name == "norm_x":
            env["w_in_t"] = got[0].reshape(IN_W, D)
            full_rows = jnp.swapaxes(got[1], 0, 1).reshape(shard_rows.shape[0], -1)
            small["conv_w"], small["lru_lambda"] = full_rows[0:4], full_rows[4:6]
            small["lru_ba"], small["lru_bx"] = full_rows[6:8], full_rows[8:10]
        elif name == "inproj":
            env["w_out"], env["w_fo"] = got[0].reshape(D, D), got[1].reshape(D_FF, D)
        elif name == "attn_fwd":
            env["w_fi_t"] = got[0].reshape(2 * D_FF, D)
        elif name == "ffn_in_bwd":
            recv["w_ffn_out"] = got[0]
        elif name == "attn_bwd":
            recv["w_out"] = got[0]
        elif name == "lru_bwd":
            recv["w_ffn_in"] = got[0]
        elif name == "conv_bwd":
            recv["early_f32"], recv["early_bf16"] = got
        elif name == "inproj_bwd":
            recv["w_in"] = got[0]

    grad_x, grads = _local_step(x[0], loss_target[0], small, env, before, after)

    outs = {}
    for name in ("w_out", "w_ffn_out"):
        outs[name] = _adamw(recv[name], w[name], m[name], v[name], "adamw_" + name)
    for name in ("w_in", "w_ffn_in"):
        t = lambda a: jnp.swapaxes(a, 1, 2)
        outs[name] = [t(r) for r in _adamw(recv[name], t(w[name]), t(m[name]), t(v[name]), "adamw_" + name)]

    small_names = SMALL_REPL + SMALL_SHARD
    late_packed, late_spans = _pack_rows([grads[n] for n in LATE])
    (got_late,) = _exchange([(late_packed, False)], "gather_late_grads")
    summed = {}
    for names, got, spans, tag in ((EARLY_F32, recv["early_f32"], env["early_f32_spans"], "early_f32"),
                                   (EARLY_BF16, recv["early_bf16"], env["early_bf16_spans"], "early_bf16"),
                                   (LATE, got_late, late_spans, "late")):
        total = _sum_parts(got, "sum_small_" + tag)
        summed.update(zip(names, _unpack_rows(total, spans, [grads[n].shape for n in names])))
    loss = summed["loss"].reshape(())
    gsm = {n: summed[n].reshape(w[n].shape) for n in SMALL_REPL}
    for n in SMALL_SHARD:
        full = summed[n]
        gsm[n] = lax.dynamic_slice_in_dim(full, me * 128, 128, axis=1).reshape(w[n].shape)
    pk = lambda dct: _pack_rows([dct[n] for n in small_names])[0]
    gp, sp = _pack_rows([gsm[n] for n in small_names])
    res = _adamw(gp[None], pk(w)[None], pk(m)[None], pk(v)[None], "adamw_small")
    sshapes = [w[n].shape for n in small_names]
    for idx, t in enumerate(res):
        for n, a in zip(small_names, _unpack_rows(t[0], sp, sshapes)):
            outs.setdefault(n, [None] * 4)[idx] = a

    result = [loss, grad_x[None]]
    for idx in range(4):
        result += [outs[n][idx] for n in ORDER]
    return tuple(result)
```

```python
import functools
import math

import jax
import jax.numpy as jnp
from jax import lax
from jax.experimental import pallas as pl
from jax.experimental.pallas import tpu as pltpu

f32 = jnp.float32
bf16 = jnp.bfloat16

D = 1024
D_FF = 2816
IN_W = 5632
N_HEADS = 16
N_KV = 4
HEAD_DIM = 64
WINDOW = 128
BLK = 128
LRU_HEADS = 16
LRU_BLOCK = 64
LRU_GROUPS = 4
LRU_GW = 256
LRU_CHUNK = 128
LRU_ROWS = 2048
RGLRU_C = 8.0
EPS = 1e-6
NEG_INF = -1e30
N_DEV = 8

ADAM_LR = 0.001
ADAM_B1 = 0.9
ADAM_B2 = 0.999
ADAM_EPS = 1e-08
ADAM_WD = 0.01
ADAM_STEP = 10

VMEM_MB = 56

C_U, C_G, C_Q, C_Z0, C_Z1, C_K, C_V = 0, 1024, 2048, 3072, 4096, 5120, 5376


def _cparams(vmem_mb=VMEM_MB):
    return pltpu.CompilerParams(vmem_limit_bytes=vmem_mb << 20)


def _div_tile(n, pref):
    if n <= pref:
        return n
    return max(t for t in range(8, pref + 1, 8) if n % t == 0)


def _sigmoid(x):
    return 0.5 * jnp.tanh(0.5 * x) + 0.5


def _log1p(x):
    u = 1.0 + x
    d = u - 1.0
    return jnp.where(d == 0.0, x, jnp.log(u) * (x / jnp.where(d == 0.0, 1.0, d)))


def _softplus(x):
    return jnp.maximum(x, 0.0) + _log1p(jnp.exp(-jnp.abs(x)))


def _gelu_and_grad(x):
    c = math.sqrt(2.0 / math.pi)
    inner = c * (x + 0.044715 * (x * x * x))
    t = jnp.tanh(inner)
    gelu = 0.5 * x * (1.0 + t)
    dinner = c * (1.0 + 3 * 0.044715 * (x * x))
    dgelu = 0.5 * (1.0 + t) + 0.5 * x * (1.0 - t * t) * dinner
    return gelu, dgelu


def _rms_bwd(dn, xv, g):
    r = lax.rsqrt(jnp.mean(xv * xv, axis=-1, keepdims=True) + EPS)
    xh = xv * r
    dxh = dn * g
    dx = r * (dxh - xh * jnp.mean(dxh * xh, axis=-1, keepdims=True))
    return dx, dn * xh


ANY_SPEC = pl.BlockSpec(memory_space=pl.ANY)


def _comm_out_shape(src, scatter):
    return jax.ShapeDtypeStruct((N_DEV, *(src.shape[1:] if scatter else src.shape)), src.dtype)


def _comm_sems():
    return [pltpu.SemaphoreType.DMA((N_DEV - 1,)), pltpu.SemaphoreType.DMA((N_DEV - 1,)), pltpu.SemaphoreType.DMA]


def _scatter_descs(src_ref, out_ref, send_sems, recv_sems, local_sem):
    x, y, c = lax.axis_index("x"), lax.axis_index("y"), lax.axis_index("c")
    me = 4 * x + 2 * y + c
    descs = [pltpu.make_async_copy(src_ref.at[me], out_ref.at[me], local_sem)]
    for k in range(1, N_DEV):
        px, py, pc = x ^ (k >> 2), y ^ ((k >> 1) & 1), c ^ (k & 1)
        descs.append(pltpu.make_async_remote_copy(
            src_ref=src_ref.at[4 * px + 2 * py + pc], dst_ref=out_ref.at[me],
            send_sem=send_sems.at[k - 1], recv_sem=recv_sems.at[k - 1],
            device_id=(px, py, pc), device_id_type=pl.DeviceIdType.MESH))
    return descs


def _gather_copies(src_ref, out_ref, send_sems, recv_sems, local_sem, which):
    x, y, c = lax.axis_index("x"), lax.axis_index("y"), lax.axis_index("c")
    me, sibling = (x, y, c), (x, y, 1 - c)
    chips = [(1 - x, y), (x, 1 - y), (1 - x, 1 - y)]

    def slot(px, py, pc):
        return out_ref.at[4 * px + 2 * py + pc]

    def copy(k, block, to, src=None):
        return pltpu.make_async_remote_copy(
            src_ref=slot(*block) if src is None else src, dst_ref=slot(*block),
            send_sem=send_sems.at[k], recv_sem=recv_sems.at[k], device_id=to, device_id_type=pl.DeviceIdType.MESH)

    make = {
        "local": lambda: pltpu.make_async_copy(src_ref, slot(*me), local_sem),
        "first": lambda: [copy(0, me, sibling, src=src_ref)] + [copy(1 + j, me, (*chip, c), src=src_ref)
                                                                 for j, chip in enumerate(chips)],
        "passed": lambda: [copy(4 + j, (*chip, c), sibling) for j, chip in enumerate(chips)],
        "landed": lambda: [copy(1 + j, (*chip, c), me) for j, chip in enumerate(chips)],
        "later": lambda: [copy(0, sibling, me)] + [copy(4 + j, (*chip, 1 - c), me) for j, chip in enumerate(chips)],
    }
    return [make[name]() for name in which]


def _comm_start(src_ref, out_ref, sems, scatter):
    if scatter:
        for d in _scatter_descs(src_ref, out_ref, *sems):
            d.start()
    else:
        local, first = _gather_copies(src_ref, out_ref, *sems, which=("local", "first"))
        local.start()
        for cp in first:
            cp.start()


def _comm_pass_on(src_ref, out_ref, sems, scatter):
    if not scatter:
        landed, passed = _gather_copies(src_ref, out_ref, *sems, which=("landed", "passed"))
        for arrived, onward in zip(landed, passed):
            arrived.wait_recv()
            onward.start()


def _comm_finish(src_ref, out_ref, sems, scatter):
    if scatter:
        for d in _scatter_descs(src_ref, out_ref, *sems):
            d.wait()
    else:
        later, first, passed, local = _gather_copies(src_ref, out_ref, *sems,
                                                     which=("later", "first", "passed", "local"))
        for cp in later:
            cp.wait_recv()
        for cp in first + passed:
            cp.wait_send()
        local.wait()


def _exchange(comm, name):
    nc = len(comm)

    def body(*refs):
        srcs, outs, sems = refs[:nc], refs[nc:2 * nc], refs[2 * nc:]
        for stage in (_comm_start, _comm_pass_on, _comm_finish):
            for i in range(nc):
                stage(srcs[i], outs[i], sems[3 * i:3 * i + 3], comm[i][1])

    return pl.pallas_call(
        body, name=name, in_specs=[ANY_SPEC] * nc, out_specs=[ANY_SPEC] * nc,
        out_shape=[_comm_out_shape(*c) for c in comm],
        scratch_shapes=[s for _ in comm for s in _comm_sems()],
    )(*[c[0] for c in comm])


def _hosted_call(body, *, name, grid, in_specs, out_specs, out_shape, args, scratch_shapes=(), comm=()):
    nin, nout, nscr, nc = len(in_specs), len(out_specs), len(scratch_shapes), len(comm)
    steps = math.prod(grid)

    def wrapped(*refs):
        ins = refs[:nin]
        csrc = refs[nin:nin + nc]
        outs = refs[nin + nc:nin + nc + nout]
        cout = refs[nin + nc + nout:nin + 2 * nc + nout]
        scr = refs[nin + 2 * nc + nout:]
        sems = scr[nscr:]

        def at(step, stage):
            lin = 0
            for a in range(len(grid)):
                lin = lin * grid[a] + pl.program_id(a)

            @pl.when(lin == step)
            def _():
                for i in range(nc):
                    stage(csrc[i], cout[i], sems[3 * i:3 * i + 3], comm[i][1])

        if nc:
            at(0, _comm_start)

        body(*ins, *outs, *scr[:nscr])

        if nc:
            at((3 * (steps - 1)) // 4, _comm_pass_on)
            at(steps - 1, _comm_finish)

    res = pl.pallas_call(
        wrapped, name=name, grid=grid,
        in_specs=[*in_specs, *[ANY_SPEC] * nc], out_specs=[*out_specs, *[ANY_SPEC] * nc],
        out_shape=[*out_shape, *[_comm_out_shape(*c) for c in comm]],
        scratch_shapes=[*scratch_shapes, *[s for _ in comm for s in _comm_sems()]],
        compiler_params=_cparams())(*args, *[c[0] for c in comm])
    return res[:nout], res[nout:]


def _rmsnorm_bf16(x, g, name, tm=1024, comm=()):
    S, dm = x.shape
    tm = min(tm, S)

    def body(x_ref, g_ref, xn_ref):
        xv = x_ref[...]
        r = lax.rsqrt(jnp.mean(xv * xv, axis=-1, keepdims=True) + EPS)
        xn_ref[...] = ((xv * r) * g_ref[...]).astype(bf16)

    row = pl.BlockSpec((tm, dm), lambda i: (i, 0))
    return _hosted_call(
        body, name=name, grid=(S // tm,), in_specs=[row, pl.BlockSpec((1, dm), lambda i: (0, 0))],
        out_specs=[row], out_shape=[jax.ShapeDtypeStruct((S, dm), bf16)], args=(x, g), comm=comm)


def _matmul_t(a, wt, name, tm=2048, tn=512, row_block=lambda j: j, comm=()):
    S, dm = a.shape
    n = wt.shape[0]
    tm = min(tm, S)

    def body(a_ref, w_ref, o_ref):
        o_ref[...] = lax.dot_general(a_ref[...], w_ref[...], (((1,), (1,)), ((), ())),
                                     preferred_element_type=f32).astype(bf16)

    return _hosted_call(
        body, name=name, grid=(S // tm, n // tn),
        in_specs=[pl.BlockSpec((tm, dm), lambda i, j: (i, 0)),
                  pl.BlockSpec((tn, dm), lambda i, j: (row_block(j), 0))],
        out_specs=[pl.BlockSpec((tm, tn), lambda i, j: (i, j))],
        out_shape=[jax.ShapeDtypeStruct((S, n), bf16)], args=(a, wt), comm=comm)


def _norm_matmul(x, g, wt, name, tm=1024, tn=1408, row_block=lambda j: j, comm=()):
    S, dm = x.shape
    n = wt.shape[0]
    tm = min(tm, S)

    def body(x_ref, g_ref, w_ref, xn_ref, o_ref):
        @pl.when(pl.program_id(1) == 0)
        def _():
            xv = x_ref[...]
            r = lax.rsqrt(jnp.mean(xv * xv, axis=-1, keepdims=True) + EPS)
            xn_ref[...] = ((xv * r) * g_ref[...]).astype(bf16)

        o_ref[...] = lax.dot_general(xn_ref[...], w_ref[...], (((1,), (1,)), ((), ())),
                                     preferred_element_type=f32).astype(bf16)

    return _hosted_call(
        body, name=name, grid=(S // tm, n // tn),
        in_specs=[pl.BlockSpec((tm, dm), lambda i, j: (i, 0)),
                  pl.BlockSpec((1, dm), lambda i, j: (0, 0)),
                  pl.BlockSpec((tn, dm), lambda i, j: (row_block(j), 0))],
        out_specs=[pl.BlockSpec((tm, dm), lambda i, j: (i, 0)),
                   pl.BlockSpec((tm, tn), lambda i, j: (i, j))],
        out_shape=[jax.ShapeDtypeStruct((S, dm), bf16), jax.ShapeDtypeStruct((S, n), bf16)],
        args=(x, g, wt), comm=comm)


def _mm_tn(a, b, name, tk, tn, tmc=2048, into=None, row=0, out_rows=None):
    m, ka = a.shape
    n = b.shape[1]
    tmc = min(tmc, m)
    nk = m // tmc

    def body(a_ref, b_ref, *rest):
        o_ref, acc_ref = rest[-2:]
        k = pl.program_id(2)
        part = lax.dot_general(a_ref[...], b_ref[...], (((0,), (0,)), ((), ())), preferred_element_type=f32)

        @pl.when(k == 0)
        def _():
            acc_ref[...] = part

        @pl.when(k > 0)
        def _():
            acc_ref[...] += part

        @pl.when(k == nk - 1)
        def _():
            o_ref[...] = acc_ref[...].astype(bf16)

    in_specs = [pl.BlockSpec((tmc, tk), lambda i, j, k: (k, i)), pl.BlockSpec((tmc, tn), lambda i, j, k: (k, j))]
    if into is None:
        return pl.pallas_call(
            body, name=name, grid=(ka // tk, n // tn, nk), in_specs=in_specs,
            out_specs=pl.BlockSpec((tk, tn), lambda i, j, k: (i + row, j)),
            out_shape=jax.ShapeDtypeStruct((out_rows or ka, n), bf16),
            scratch_shapes=[pltpu.VMEM((tk, tn), f32)],
            compiler_params=_cparams())(a, b)
    return pl.pallas_call(
        body, name=name, grid=(ka // tk, n // tn, nk), in_specs=[*in_specs, ANY_SPEC],
        out_specs=pl.BlockSpec((tk, tn), lambda i, j, k: (i + row, j)),
        out_shape=jax.ShapeDtypeStruct(into.shape, into.dtype),
        scratch_shapes=[pltpu.VMEM((tk, tn), f32)], input_output_aliases={2: 0},
        compiler_params=_cparams())(a, b, into)


HALO = 16


def _rows_at(ext, o, tc):
    if o == 0:
        return ext[HALO:HALO + tc]
    return pltpu.roll(ext, (-o) % ext.shape[0], 0)[HALO:HALO + tc]


def _halo_specs(tc, S, width, col):
    per = tc // HALO
    last = S // HALO - 1
    return (pl.BlockSpec((tc, width), lambda i: (i, col)),
            pl.BlockSpec((HALO, width), lambda i: (jnp.maximum(i * per - 1, 0), col)),
            pl.BlockSpec((HALO, width), lambda i: (jnp.minimum((i + 1) * per, last), col)))


def _extended(cur_ref, prev_ref, next_ref, i, nsteps):
    prev = jnp.where(i > 0, prev_ref[...].astype(f32), 0.0)
    nxt = jnp.where(i < nsteps - 1, next_ref[...].astype(f32), 0.0)
    return jnp.concatenate([prev, cur_ref[...].astype(f32), nxt], axis=0)


def _conv_fwd(proj, cw, cb, tc=1024):
    S = proj.shape[0]
    tc = min(tc, S)
    nsteps = S // tc

    def body(cur_ref, prev_ref, next_ref, w_ref, b_ref, o_ref):
        ext = _extended(cur_ref, prev_ref, next_ref, pl.program_id(0), nsteps)
        acc = _rows_at(ext, -2, tc) * w_ref[0:1, :]
        for k in range(1, 4):
            acc = acc + _rows_at(ext, k - 2, tc) * w_ref[k:k + 1, :]
        o_ref[...] = acc + b_ref[...]

    return pl.pallas_call(
        body, name="conv_fwd", grid=(nsteps,),
        in_specs=[*_halo_specs(tc, S, D, 0),
                  pl.BlockSpec((4, D), lambda i: (0, 0)), pl.BlockSpec((1, D), lambda i: (0, 0))],
        out_specs=pl.BlockSpec((tc, D), lambda i: (i, 0)),
        out_shape=jax.ShapeDtypeStruct((S, D), f32),
        compiler_params=_cparams())(proj, proj, proj, cw, cb)


def _conv_bwd(duc_f, duc_b, proj, cw, tc=1024, comm=()):
    S = proj.shape[0]
    tc = min(tc, S)
    nsteps = S // tc

    def body(fc, fp, fn, bc, bp, bn, uc_, up, un, w_ref, du_ref, dw_ref, db_ref):
        i = pl.program_id(0)

        @pl.when(i == 0)
        def _():
            dw_ref[...] = jnp.zeros_like(dw_ref)
            db_ref[...] = jnp.zeros_like(db_ref)

        dext = _extended(fc, fp, fn, i, nsteps) + _extended(bc, bp, bn, i, nsteps)
        uext = _extended(uc_, up, un, i, nsteps)
        d = dext[HALO:HALO + tc]
        acc = _rows_at(dext, 2, tc) * w_ref[0:1, :]
        for k in range(1, 4):
            acc = acc + _rows_at(dext, 2 - k, tc) * w_ref[k:k + 1, :]
        du_ref[...] = acc.astype(bf16)
        wrow = lax.broadcasted_iota(jnp.int32, (4, D), 0)
        for k in range(4):
            dw_ref[...] += jnp.where(wrow == k, jnp.sum(d * _rows_at(uext, k - 2, tc), axis=0, keepdims=True), 0.0)
        db_ref[...] += jnp.sum(d, axis=0, keepdims=True)

    return _hosted_call(
        body, name="conv_bwd", grid=(nsteps,),
        in_specs=[*_halo_specs(tc, S, D, 0), *_halo_specs(tc, S, D, 0), *_halo_specs(tc, S, D, 0),
                  pl.BlockSpec((4, D), lambda i: (0, 0))],
        out_specs=[pl.BlockSpec((tc, D), lambda i: (i, 0)),
                   pl.BlockSpec((4, D), lambda i: (0, 0)), pl.BlockSpec((1, D), lambda i: (0, 0))],
        out_shape=[jax.ShapeDtypeStruct((S, D), bf16), jax.ShapeDtypeStruct((4, D), f32),
                   jax.ShapeDtypeStruct((1, D), f32)],
        args=(duc_f, duc_f, duc_f, duc_b, duc_b, duc_b, proj, proj, proj, cw), comm=comm)


def _scan_scratch():
    halves = [pltpu.VMEM((LRU_CHUNK, 128), f32) for _ in range(2 * (LRU_GW // 128))]
    return [*halves, pltpu.VMEM((LRU_CHUNK // 8, LRU_GW), f32), pltpu.VMEM((LRU_CHUNK // 8, LRU_GW), f32)]


def _log_scan(a, b, row, n, reverse, steps):
    for s in steps:
        shift = a.shape[0] - s if reverse else s
        keep = (row < n - s) if reverse else (row >= s)
        a_sh = pltpu.roll(a, shift, 0)
        b_sh = pltpu.roll(b, shift, 0)
        b = jnp.where(keep, a * b_sh + b, b)
        a = jnp.where(keep, a * a_sh, a)
    return a, b


def _scan_chunk(a, b, carry, reverse, *scratch):
    tc, w = a.shape
    ng = tc // 8
    nl = w // 128
    sa_refs, sb_refs, sc_ref, st_ref = scratch[:nl], scratch[nl:2 * nl], scratch[2 * nl], scratch[2 * nl + 1]
    sub = lax.broadcasted_iota(jnp.int32, (8, w), 0)
    ag, bg = [], []
    for k in range(ng):
        ak, bk = _log_scan(a[8 * k:8 * k + 8], b[8 * k:8 * k + 8], sub, 8, reverse, (1, 2, 4))
        ag.append(ak)
        bg.append(bk)
    a = jnp.concatenate(ag, axis=0)
    b = jnp.concatenate(bg, axis=0)
    edge = 0 if reverse else 7
    for i in range(nl):
        sa_refs[i][...] = a[:, 128 * i:128 * (i + 1)]
        sb_refs[i][...] = b[:, 128 * i:128 * (i + 1)]
    ta = jnp.concatenate([r[pl.ds(edge, ng, stride=8), :] for r in sa_refs], axis=1)
    tb = jnp.concatenate([r[pl.ds(edge, ng, stride=8), :] for r in sb_refs], axis=1)
    grow = lax.broadcasted_iota(jnp.int32, (ng, w), 0)
    ta, tb = _log_scan(ta, tb, grow, ng, reverse, [1 << i for i in range(ng.bit_length() - 1)])
    state = tb + ta * carry
    st_ref[...] = state
    if reverse:
        sc_ref[...] = jnp.where(grow == ng - 1, carry, pltpu.roll(state, ng - 1, 0))
    else:
        sc_ref[...] = jnp.where(grow == 0, carry, pltpu.roll(state, 1, 0))
    h = jnp.concatenate([bg[k] + ag[k] * sc_ref[k:k + 1, :] for k in range(ng)], axis=0)
    return h, (st_ref[0:1, :] if reverse else st_ref[ng - 1:ng, :])


def _lru_gates(uc, w, p_ref):
    pre = jnp.dot(uc.astype(bf16), w, preferred_element_type=f32)
    r = _sigmoid(pre[:, :LRU_GW] + p_ref[0, 1:2, :])
    gi = _sigmoid(pre[:, LRU_GW:] + p_ref[0, 2:3, :])
    sp = _softplus(-p_ref[0, 0:1, :])
    log_a = -RGLRU_C * r * sp
    a = jnp.exp(log_a)
    x = 2.0 * log_a
    series = -x * (1.0 + x * (0.5 + x * (1.0 / 6 + x * (1.0 / 24))))
    beta = jnp.sqrt(jnp.maximum(jnp.where(x > -0.0625, series, 1.0 - a * a), 0.0))
    return r, gi, sp, a, beta


def _lru_fwd(uc, wg, lp, reverse, comm=()):
    S = uc.shape[0]
    tc = LRU_CHUNK
    rows = min(LRU_ROWS, S)
    nsub = rows // tc
    nblk = S // rows
    d = 1 if reverse else 0

    def bidx(c):
        return nblk - 1 - c if reverse else c

    def body(uc_ref, w_ref, p_ref, h_ref, carry_ref, *scan_scratch):
        @pl.when(pl.program_id(1) == 0)
        def _():
            carry_ref[...] = jnp.zeros_like(carry_ref)

        carry = carry_ref[...]
        for j in (reversed(range(nsub)) if reverse else range(nsub)):
            sl = slice(j * tc, (j + 1) * tc)
            ucv = uc_ref[sl, :]
            _, gi, _, a, beta = _lru_gates(ucv, w_ref[0], p_ref)
            h, carry = _scan_chunk(a, beta * (gi * ucv), carry, reverse, *scan_scratch)
            h_ref[sl, :] = h.astype(bf16)
        carry_ref[...] = carry

    return _hosted_call(
        body, name="lru_fwd_rev" if reverse else "lru_fwd", grid=(LRU_GROUPS, nblk),
        in_specs=[pl.BlockSpec((rows, LRU_GW), lambda g, c: (bidx(c), g)),
                  pl.BlockSpec((1, LRU_GW, 2 * LRU_GW), lambda g, c: (g, 0, d)),
                  pl.BlockSpec((1, 8, LRU_GW), lambda g, c: (d, 0, g))],
        out_specs=[pl.BlockSpec((rows, LRU_GW), lambda g, c: (bidx(c), g))],
        out_shape=[jax.ShapeDtypeStruct((S, D), bf16)],
        scratch_shapes=[pltpu.VMEM((1, LRU_GW), f32), *_scan_scratch()],
        args=(uc, wg, lp), comm=comm)


def _lru_bwd(uc, dh, h, wg, lp, reverse, comm=()):
    S = uc.shape[0]
    tc = LRU_CHUNK
    rows = min(LRU_ROWS, S)
    nsub = rows // tc
    nblk = S // rows
    d = 1 if reverse else 0
    per = rows // HALO
    last8 = S // HALO - 1

    def bidx(c):
        return c if reverse else nblk - 1 - c

    def halo_idx(c):
        if reverse:
            return jnp.minimum((bidx(c) + 1) * per, last8)
        return jnp.maximum(bidx(c) * per - 1, 0)

    def body(uc_ref, dh_ref, h_ref, halo_ref, w_ref, p_ref, duc_ref, dw_ref, dp_ref, carry_ref, tmp_ref,
             *scan_scratch):
        c = pl.program_id(1)
        bi = bidx(c)

        @pl.when(c == 0)
        def _():
            carry_ref[...] = jnp.zeros_like(carry_ref)
            dw_ref[...] = jnp.zeros_like(dw_ref)
            dp_ref[...] = jnp.zeros_like(dp_ref)

        row = lax.broadcasted_iota(jnp.int32, (tc, LRU_GW), 0)
        carry = carry_ref[...]
        dw = jnp.zeros((LRU_GW, 2 * LRU_GW), f32)
        dsp = jnp.zeros((1, LRU_GW), f32)
        dba = jnp.zeros((1, LRU_GW), f32)
        dbx = jnp.zeros((1, LRU_GW), f32)
        for j in (range(nsub) if reverse else reversed(range(nsub))):
            sl = slice(j * tc, (j + 1) * tc)
            ucv = uc_ref[sl, :]
            ucb = ucv.astype(bf16)
            r, gi, sp, a, beta = _lru_gates(ucv, w_ref[0], p_ref)
            hv = h_ref[sl, :].astype(f32)
            dhv = dh_ref[sl, :].astype(f32)
            if reverse:
                alpha = jnp.where(row == 0, 1.0, pltpu.roll(a, 1, 0))
                gsc, _ = _scan_chunk(alpha, dhv, carry, False, *scan_scratch)
                if j < nsub - 1:
                    edge = h_ref[(j + 1) * tc:(j + 1) * tc + HALO, :].astype(f32)[0:1, :]
                else:
                    edge = jnp.where(bi < nblk - 1, halo_ref[...].astype(f32)[0:1, :], 0.0)
                h_nb = jnp.where(row == tc - 1, edge, pltpu.roll(hv, tc - 1, 0))
            else:
                alpha = jnp.where(row == tc - 1, 1.0, pltpu.roll(a, tc - 1, 0))
                gsc, _ = _scan_chunk(alpha, dhv, carry, True, *scan_scratch)
                if j > 0:
                    edge = h_ref[j * tc - HALO:j * tc, :].astype(f32)[HALO - 1:HALO, :]
                else:
                    edge = jnp.where(bi > 0, halo_ref[...].astype(f32)[HALO - 1:HALO, :], 0.0)
                h_nb = jnp.where(row == 0, edge, pltpu.roll(hv, 1, 0))
            tmp_ref[...] = a * gsc
            carry = tmp_ref[tc - 1:tc, :] if reverse else tmp_ref[0:1, :]

            da = gsc * h_nb
            dbeta = gsc * (gi * ucv)
            dl = da * a - dbeta * (a * a) / beta
            dr = dl * (-RGLRU_C * sp)
            dsp = dsp + jnp.sum(dl * (-RGLRU_C * r), axis=0, keepdims=True)
            dgi = gsc * beta * ucv
            dpre_r = dr * r * (1.0 - r)
            dpre_i = dgi * gi * (1.0 - gi)
            dba = dba + jnp.sum(dpre_r, axis=0, keepdims=True)
            dbx = dbx + jnp.sum(dpre_i, axis=0, keepdims=True)
            dpre = jnp.concatenate([dpre_r, dpre_i], axis=1).astype(bf16)
            back = lax.dot_general(dpre, w_ref[0], (((1,), (1,)), ((), ())), preferred_element_type=f32)
            duc_ref[sl, :] = (gsc * beta * gi + back).astype(bf16)
            dw = dw + lax.dot_general(ucb, dpre, (((0,), (0,)), ((), ())), preferred_element_type=f32)
        carry_ref[...] = carry
        dw_ref[0] += dw
        dlam = -dsp / (1.0 + jnp.exp(p_ref[0, 0:1, :]))
        prow = lax.broadcasted_iota(jnp.int32, (8, LRU_GW), 0)
        dp_ref[...] += (jnp.where(prow == 0, dlam, 0.0) + jnp.where(prow == 1, dba, 0.0)
                        + jnp.where(prow == 2, dbx, 0.0))

    chunk = pl.BlockSpec((rows, LRU_GW), lambda g, c: (bidx(c), g))
    return _hosted_call(
        body, name="lru_bwd_rev" if reverse else "lru_bwd", grid=(LRU_GROUPS, nblk),
        in_specs=[chunk, chunk, chunk,
                  pl.BlockSpec((HALO, LRU_GW), lambda g, c: (halo_idx(c), g)),
                  pl.BlockSpec((1, LRU_GW, 2 * LRU_GW), lambda g, c: (g, 0, d)),
                  pl.BlockSpec((1, 8, LRU_GW), lambda g, c: (d, 0, g))],
        out_specs=[chunk,
                   pl.BlockSpec((1, LRU_GW, 2 * LRU_GW), lambda g, c: (g, 0, 0)),
                   pl.BlockSpec((8, LRU_GW), lambda g, c: (0, g))],
        out_shape=[jax.ShapeDtypeStruct((S, D), bf16),
                   jax.ShapeDtypeStruct((LRU_GROUPS, LRU_GW, 2 * LRU_GW), f32),
                   jax.ShapeDtypeStruct((8, D), f32)],
        scratch_shapes=[pltpu.VMEM((1, LRU_GW), f32), pltpu.VMEM((tc, LRU_GW), f32), *_scan_scratch()],
        args=(uc, dh, h, h, wg, lp), comm=comm)


def _slope(h):
    return 2.0 ** (-8.0 * (h + 1.0) / N_HEADS)


ATT_QB = 4


def _kv_specs(nb, col):
    return [pl.BlockSpec((BLK, N_KV * HEAD_DIM), lambda n: (jnp.maximum(ATT_QB * n - 1, 0), col)),
            pl.BlockSpec((ATT_QB * BLK, N_KV * HEAD_DIM), lambda n: (n, col)),
            pl.BlockSpec((BLK, N_KV * HEAD_DIM), lambda n: (jnp.minimum(ATT_QB * (n + 1), nb - 1), col))]


def _key_blocks(prev_ref, cur_ref, next_ref):
    return [prev_ref[...], *[cur_ref[BLK * s:BLK * (s + 1), :] for s in range(ATT_QB)], next_ref[...]]


def _dup_windows(r0, r1, r2):
    left = lax.broadcasted_iota(jnp.int32, (3 * BLK, 128), 1) < HEAD_DIM
    win = jnp.concatenate([r0, r1, r2], axis=0)
    out = []
    for i in range(N_KV // 2):
        t = win[:, i * 128:(i + 1) * 128]
        r = pltpu.roll(t, HEAD_DIM, 1)
        out += [jnp.where(left, t, r).astype(bf16), jnp.where(left, r, t).astype(bf16)]
    return out


def _attn_bias_init(bias_ref):
    k_loc = lax.broadcasted_iota(jnp.int32, (3 * BLK, BLK), 0)
    q_loc = lax.broadcasted_iota(jnp.int32, (3 * BLK, BLK), 1)
    adist = jnp.abs(q_loc + BLK - k_loc)
    adf = adist.astype(f32)
    for e in range(3):
        ok = adist <= WINDOW
        if e == 0:
            ok = ok & (k_loc >= BLK)
        if e == 2:
            ok = ok & (k_loc < 2 * BLK)
        for kv in range(N_KV):
            bias_ref[e, kv] = jnp.concatenate(
                [jnp.where(ok, (-_slope(4 * kv + j)) * adf, NEG_INF) for j in range(4)], axis=1)


def _stack_heads(ref, sub, kv, scale):
    left = lax.broadcasted_iota(jnp.int32, (BLK, 128), 1) < HEAD_DIM
    rows = []
    for pp in range(2):
        t = ref[BLK * sub:BLK * (sub + 1), (2 * kv + pp) * 128:(2 * kv + pp + 1) * 128]
        if scale != 1.0:
            t = t * scale
        zero = jnp.zeros_like(t)
        rows += [jnp.where(left, t, zero).astype(bf16), jnp.where(left, zero, t).astype(bf16)]
    return jnp.concatenate(rows, axis=0)


def _attn_softmax(qs, k2, bias, sink_ref, kv, stats=None):
    sink = jnp.concatenate([jnp.full((1, BLK), sink_ref[0, 4 * kv + j], f32) for j in range(4)], axis=1)
    s = lax.dot_general(k2, qs, (((1,), (1,)), ((), ())), preferred_element_type=f32) + bias
    m = jnp.maximum(jnp.max(s, axis=0, keepdims=True), sink) if stats is None else stats[0]
    p = jnp.exp(s - m)
    ps = jnp.exp(sink - m)
    inv = 1.0 / (jnp.sum(p, axis=0, keepdims=True) + ps) if stats is None else stats[1]
    return p, ps, m, inv


def _pair_tiles(t):
    return [jnp.concatenate([t[:HEAD_DIM, 256 * pp:256 * pp + 128],
                             t[HEAD_DIM:, 256 * pp + 128:256 * pp + 256]], axis=0).T for pp in range(2)]


def _attn_fwd(proj, sink, comm=()):
    S = proj.shape[0]
    nb = S // BLK
    assert nb >= 2 and nb % ATT_QB == 0

    def body(q_ref, k0, k1, k2_, v0, v1, v2_, sink_ref, o_ref, st_ref, bias_ref):
        n = pl.program_id(0)

        @pl.when(n == 0)
        def _():
            _attn_bias_init(bias_ref)

        kb = _key_blocks(k0, k1, k2_)
        vb = _key_blocks(v0, v1, v2_)
        for sub in range(ATT_QB):
            blk = ATT_QB * n + sub
            e = jnp.where(blk == 0, 0, jnp.where(blk == nb - 1, 2, 1))
            kk = _dup_windows(*kb[sub:sub + 3])
            vv = _dup_windows(*vb[sub:sub + 3])
            tiles = []
            for kv in range(N_KV):
                qs = _stack_heads(q_ref, sub, kv, HEAD_DIM ** -0.5)
                p, _, m, inv = _attn_softmax(qs, kk[kv], bias_ref[e, kv], sink_ref, kv)
                st_ref[sub, kv:kv + 1, :] = m
                st_ref[sub, N_KV + kv:N_KV + kv + 1, :] = inv
                ot = lax.dot_general(vv[kv], p.astype(bf16), (((0,), (0,)), ((), ())), preferred_element_type=f32)
                tiles += _pair_tiles(ot * inv)
            o_ref[BLK * sub:BLK * (sub + 1), :] = jnp.concatenate(tiles, axis=1).astype(bf16)

    return _hosted_call(
        body, name="attn_fwd", grid=(nb // ATT_QB,),
        in_specs=[pl.BlockSpec((ATT_QB * BLK, D), lambda n: (n, C_Q // D)),
                  *_kv_specs(nb, C_K // (N_KV * HEAD_DIM)), *_kv_specs(nb, C_V // (N_KV * HEAD_DIM)),
                  pl.BlockSpec(memory_space=pltpu.SMEM)],
        out_specs=[pl.BlockSpec((ATT_QB * BLK, D), lambda n: (n, 0)),
                   pl.BlockSpec((ATT_QB, 2 * N_KV, 4 * BLK), lambda n: (n, 0, 0))],
        out_shape=[jax.ShapeDtypeStruct((S, D), bf16), jax.ShapeDtypeStruct((nb, 2 * N_KV, 4 * BLK), f32)],
        scratch_shapes=[pltpu.VMEM((3, N_KV, 3 * BLK, 4 * BLK), f32)],
        args=(proj, proj, proj, proj, proj, proj, proj, sink), comm=comm)


def _attn_bwd(proj, sink, dyb, stats, comm=()):
    S = proj.shape[0]
    nb = S // BLK
    assert nb >= 2 and nb % ATT_QB == 0
    nsteps = nb // ATT_QB

    def body(q_ref, k0, k1, k2_, v0, v1, v2_, sink_ref, do_ref, st_ref, dq_ref, dk_out, dv_out, ds_ref,
             bias_ref, dk_ref, dv_ref, dsk_ref):
        n = pl.program_id(0)

        @pl.when(n == 0)
        def _():
            _attn_bias_init(bias_ref)
            dk_ref[...] = jnp.zeros_like(dk_ref)
            dv_ref[...] = jnp.zeros_like(dv_ref)
            dsk_ref[...] = jnp.zeros_like(dsk_ref)

        kb = _key_blocks(k0, k1, k2_)
        vb = _key_blocks(v0, v1, v2_)
        left3 = lax.broadcasted_iota(jnp.int32, (3 * BLK, 128), 1) < HEAD_DIM
        for sub in range(ATT_QB):
            blk = ATT_QB * n + sub
            e = jnp.where(blk == 0, 0, jnp.where(blk == nb - 1, 2, 1))
            kk = _dup_windows(*kb[sub:sub + 3])
            vv = _dup_windows(*vb[sub:sub + 3])
            start = pl.multiple_of(blk * BLK, BLK)
            dq_tiles, dks, dvs = [], [], []
            for kv in range(N_KV):
                qs = _stack_heads(q_ref, sub, kv, HEAD_DIM ** -0.5)
                dos = _stack_heads(do_ref, sub, kv, 1.0)
                stats = (st_ref[sub, kv:kv + 1, :], st_ref[sub, N_KV + kv:N_KV + kv + 1, :])
                p, ps, _, inv = _attn_softmax(qs, kk[kv], bias_ref[e, kv], sink_ref, kv, stats)
                pn = p * inv
                dp = lax.dot_general(vv[kv], dos, (((1,), (1,)), ((), ())), preferred_element_type=f32)
                delta = jnp.sum(pn * dp, axis=0, keepdims=True)
                dsc = (pn * (dp - delta)).astype(bf16)
                dsk_ref[kv:kv + 1, :] += delta * (ps * inv)
                dqt = lax.dot_general(kk[kv], dsc, (((0,), (0,)), ((), ())), preferred_element_type=f32)
                dq_tiles += _pair_tiles(dqt * (HEAD_DIM ** -0.5))
                dk = jnp.dot(dsc, qs, preferred_element_type=f32)
                dv = jnp.dot(pn.astype(bf16), dos, preferred_element_type=f32)
                dks.append(dk + pltpu.roll(dk, HEAD_DIM, 1))
                dvs.append(dv + pltpu.roll(dv, HEAD_DIM, 1))
            for jp in range(N_KV // 2):
                cols = slice(jp * 128, (jp + 1) * 128)
                dk_ref[pl.ds(start, 3 * BLK), cols] += jnp.where(left3, dks[2 * jp], dks[2 * jp + 1])
                dv_ref[pl.ds(start, 3 * BLK), cols] += jnp.where(left3, dvs[2 * jp], dvs[2 * jp + 1])
            dq_ref[BLK * sub:BLK * (sub + 1), :] = jnp.concatenate(dq_tiles, axis=1).astype(bf16)

        @pl.when(n == nsteps - 1)
        def _():
            pltpu.sync_copy(dk_ref, dk_out)
            pltpu.sync_copy(dv_ref, dv_out)
            lane = lax.broadcasted_iota(jnp.int32, (1, 128), 1)
            dsink = jnp.zeros((1, 128), f32)
            for h in range(N_HEADS):
                part = dsk_ref[h // 4:h // 4 + 1, (h % 4) * BLK:(h % 4 + 1) * BLK]
                dsink = dsink + jnp.where(lane == h, -jnp.sum(part), 0.0)
            ds_ref[...] = dsink

    acc = jax.ShapeDtypeStruct((S + 2 * BLK, N_KV * HEAD_DIM), f32)
    return _hosted_call(
        body, name="attn_bwd", grid=(nsteps,),
        in_specs=[pl.BlockSpec((ATT_QB * BLK, D), lambda n: (n, C_Q // D)),
                  *_kv_specs(nb, C_K // (N_KV * HEAD_DIM)), *_kv_specs(nb, C_V // (N_KV * HEAD_DIM)),
                  pl.BlockSpec(memory_space=pltpu.SMEM),
                  pl.BlockSpec((ATT_QB * BLK, D), lambda n: (n, 0)),
                  pl.BlockSpec((ATT_QB, 2 * N_KV, 4 * BLK), lambda n: (n, 0, 0))],
        out_specs=[pl.BlockSpec((ATT_QB * BLK, D), lambda n: (n, 0)), ANY_SPEC, ANY_SPEC,
                   pl.BlockSpec((1, 128), lambda n: (0, 0))],
        out_shape=[jax.ShapeDtypeStruct((S, D), bf16), acc, acc, jax.ShapeDtypeStruct((1, 128), f32)],
        scratch_shapes=[pltpu.VMEM((3, N_KV, 3 * BLK, 4 * BLK), f32), pltpu.VMEM(acc.shape, f32),
                        pltpu.VMEM(acc.shape, f32), pltpu.VMEM((8, 4 * BLK), f32)],
        args=(proj, proj, proj, proj, proj, proj, proj, sink, dyb, stats), comm=comm)


def _merge_parts(hf, hb, g, z0, z1, yb, bg):
    g0 = _sigmoid(z0.astype(f32) + bg[:, :D])
    g1 = _sigmoid(z1.astype(f32) + bg[:, D:])
    gelu, dgelu = _gelu_and_grad(g.astype(f32))
    hs = hf.astype(f32) + hb.astype(f32)
    ya = hs * gelu
    return g0, g1, gelu, dgelu, hs, ya


def _merge_outproj(x, hf, hb, proj, yb, bg, w_out, tm=1024):
    S = x.shape[0]
    tm = min(tm, S)

    def body(x_ref, hf_ref, hb_ref, g_ref, z0_ref, z1_ref, yb_ref, bg_ref, w_ref, mg_ref, x1_ref):
        ybv = yb_ref[...].astype(f32)
        g0, g1, _, _, _, ya = _merge_parts(hf_ref[...], hb_ref[...], g_ref[...], z0_ref[...], z1_ref[...],
                                           ybv, bg_ref[...])
        mg = (g0 * ya + g1 * ybv).astype(bf16)
        mg_ref[...] = mg
        x1_ref[...] = x_ref[...] + jnp.dot(mg, w_ref[...], preferred_element_type=f32)

    row = pl.BlockSpec((tm, D), lambda i: (i, 0))
    return pl.pallas_call(
        body, name="merge_outproj", grid=(S // tm,),
        in_specs=[row, row, row,
                  pl.BlockSpec((tm, D), lambda i: (i, C_G // D)),
                  pl.BlockSpec((tm, D), lambda i: (i, C_Z0 // D)),
                  pl.BlockSpec((tm, D), lambda i: (i, C_Z1 // D)),
                  row, pl.BlockSpec((1, 2 * D), lambda i: (0, 0)), pl.BlockSpec((D, D), lambda i: (0, 0))],
        out_specs=[row, row],
        out_shape=[jax.ShapeDtypeStruct((S, D), bf16), jax.ShapeDtypeStruct((S, D), f32)],
        compiler_params=_cparams())(x, hf, hb, proj, proj, proj, yb, bg, w_out)


def _ffn_out_loss(gu, x1, w_fo, g3, tgt, tm=256):
    S = x1.shape[0]
    tm = min(tm, S)

    def body(gt_ref, up_ref, x1_ref, w_ref, g_ref, t_ref, ff_ref, dx_ref, dxb_ref, loss_ref, dg_ref,
             dgt_ref, dup_ref):
        @pl.when(pl.program_id(0) == 0)
        def _():
            loss_ref[...] = jnp.zeros_like(loss_ref)
            dg_ref[...] = jnp.zeros_like(dg_ref)

        gt = gt_ref[...].astype(f32)
        up = up_ref[...].astype(f32)
        sg = _sigmoid(gt)
        silu = gt * sg
        ff = (silu * up).astype(bf16)
        ff_ref[...] = ff
        x2 = x1_ref[...] + jnp.dot(ff, w_ref[...], preferred_element_type=f32)
        gv = g_ref[...]
        r = lax.rsqrt(jnp.mean(x2 * x2, axis=-1, keepdims=True) + EPS)
        xh = x2 * r
        diff = xh * gv - t_ref[...]
        loss_ref[...] += (0.5 / D) * jnp.sum(diff * diff)
        dy = diff * (1.0 / D)
        dg_ref[...] += jnp.sum(dy * xh, axis=0, keepdims=True)
        dxh = dy * gv
        dx = r * (dxh - xh * jnp.mean(dxh * xh, axis=-1, keepdims=True))
        dx_ref[...] = dx
        dxb = dx.astype(bf16)
        dxb_ref[...] = dxb
        dff = lax.dot_general(dxb, w_ref[...], (((1,), (1,)), ((), ())), preferred_element_type=f32)
        dup_ref[...] = (dff * silu).astype(bf16)
        dgt_ref[...] = ((dff * up) * (sg * (1.0 + gt * (1.0 - sg)))).astype(bf16)

    row = pl.BlockSpec((tm, D), lambda i: (i, 0))
    vec = pl.BlockSpec((1, D), lambda i: (0, 0))
    wide = pl.BlockSpec((tm, D_FF), lambda i: (i, 0))
    wide_shape = jax.ShapeDtypeStruct((S, D_FF), bf16)
    return pl.pallas_call(
        body, name="ffn_out_loss", grid=(S // tm,),
        in_specs=[wide, pl.BlockSpec((tm, D_FF), lambda i: (i, 1)),
                  row, pl.BlockSpec((D_FF, D), lambda i: (0, 0)), vec, row],
        out_specs=[wide, row, row, pl.BlockSpec((1, 128), lambda i: (0, 0)), vec, wide, wide],
        out_shape=[wide_shape, jax.ShapeDtypeStruct((S, D), f32), jax.ShapeDtypeStruct((S, D), bf16),
                   jax.ShapeDtypeStruct((1, 128), f32), jax.ShapeDtypeStruct((1, D), f32), wide_shape, wide_shape],
        compiler_params=_cparams())(gu, gu, x1, w_fo, g3, tgt)


def _proj_bwd(pieces, wt, xres, g, dres, name, tm=512, comm=()):
    S = xres.shape[0]
    tm = min(tm, S)
    np_ = len(pieces)

    def body(*refs):
        p_refs = refs[:np_]
        w_refs = refs[np_:2 * np_]
        x_ref, g_ref, dres_ref, dx_ref, dxb_ref, dg_ref = refs[2 * np_:]

        @pl.when(pl.program_id(0) == 0)
        def _():
            dg_ref[...] = jnp.zeros_like(dg_ref)

        dn = jnp.dot(p_refs[0][...], w_refs[0][...], preferred_element_type=f32)
        for pr, wr in zip(p_refs[1:], w_refs[1:]):
            dn = dn + jnp.dot(pr[...], wr[...], preferred_element_type=f32)
        dxn, dgc = _rms_bwd(dn, x_ref[...], g_ref[...])
        dx = dres_ref[...] + dxn
        dx_ref[...] = dx
        dxb_ref[...] = dx.astype(bf16)
        dg_ref[...] += jnp.sum(dgc, axis=0, keepdims=True)

    row = pl.BlockSpec((tm, D), lambda i: (i, 0))
    vec = pl.BlockSpec((1, D), lambda i: (0, 0))
    return _hosted_call(
        body, name=name, grid=(S // tm,),
        in_specs=[*[pl.BlockSpec((tm, wd), functools.partial(lambda i, cb: (i, cb), cb=acb))
                    for _, acb, _, wd in pieces],
                  *[pl.BlockSpec((wd, D), functools.partial(lambda i, rb: (rb, 0), rb=wrb))
                    for _, _, wrb, wd in pieces],
                  row, vec, row],
        out_specs=[row, row, vec],
        out_shape=[jax.ShapeDtypeStruct((S, D), f32), jax.ShapeDtypeStruct((S, D), bf16),
                   jax.ShapeDtypeStruct((1, D), f32)],
        args=(*[p[0] for p in pieces], *[wt] * np_, xres, g, dres), comm=comm)


def _outproj_bwd(dx1b, w_out, hf, hb, proj, yb, bg, tm=1024):
    S = dx1b.shape[0]
    tm = min(tm, S)

    def body(dx_ref, w_ref, hf_ref, hb_ref, g_ref, z0_ref, z1_ref, yb_ref, bg_ref,
             dh_ref, dg_ref, dz_ref, dyb_ref, dbg_ref):
        @pl.when(pl.program_id(0) == 0)
        def _():
            dbg_ref[...] = jnp.zeros_like(dbg_ref)

        dm = lax.dot_general(dx_ref[...], w_ref[...], (((1,), (1,)), ((), ())), preferred_element_type=f32)
        ybv = yb_ref[...].astype(f32)
        g0, g1, gelu, dgelu, hs, ya = _merge_parts(hf_ref[...], hb_ref[...], g_ref[...], z0_ref[...],
                                                   z1_ref[...], ybv, bg_ref[...])
        dya = dm * g0
        dh_ref[...] = (dya * gelu).astype(bf16)
        dg_ref[...] = (dya * hs * dgelu).astype(bf16)
        dyb_ref[...] = (dm * g1).astype(bf16)
        dz0 = (dm * ya) * (g0 * (1.0 - g0))
        dz1 = (dm * ybv) * (g1 * (1.0 - g1))
        dz = jnp.concatenate([dz0, dz1], axis=1)
        dz_ref[...] = dz.astype(bf16)
        dbg_ref[...] += jnp.sum(dz, axis=0, keepdims=True)

    row = pl.BlockSpec((tm, D), lambda i: (i, 0))
    return pl.pallas_call(
        body, name="outproj_bwd", grid=(S // tm,),
        in_specs=[row, pl.BlockSpec((D, D), lambda i: (0, 0)), row, row,
                  pl.BlockSpec((tm, D), lambda i: (i, C_G // D)),
                  pl.BlockSpec((tm, D), lambda i: (i, C_Z0 // D)),
                  pl.BlockSpec((tm, D), lambda i: (i, C_Z1 // D)),
                  row, pl.BlockSpec((1, 2 * D), lambda i: (0, 0))],
        out_specs=[row, row, pl.BlockSpec((tm, 2 * D), lambda i: (i, 0)), row,
                   pl.BlockSpec((1, 2 * D), lambda i: (0, 0))],
        out_shape=[jax.ShapeDtypeStruct((S, D), bf16), jax.ShapeDtypeStruct((S, D), bf16),
                   jax.ShapeDtypeStruct((S, 2 * D), bf16), jax.ShapeDtypeStruct((S, D), bf16),
                   jax.ShapeDtypeStruct((1, 2 * D), f32)],
        compiler_params=_cparams())(dx1b, w_out, hf, hb, proj, proj, proj, yb, bg)


def _block_diag_groups(w):
    w4 = w.reshape(LRU_GROUPS, 4, LRU_BLOCK, LRU_BLOCK)
    eye = jnp.eye(4, dtype=w.dtype)
    return jnp.einsum("ghij,hk->ghikj", w4, eye).reshape(LRU_GROUPS, LRU_GW, LRU_GW)


def _diag_blocks(dw):
    d5 = dw.reshape(LRU_GROUPS, 4, LRU_BLOCK, 4, LRU_BLOCK)
    return jnp.stack([d5[:, h, :, h, :] for h in range(4)], axis=1).reshape(LRU_HEADS, LRU_BLOCK, LRU_BLOCK)


def _local_step(x, tgt, small, env, before=lambda name: (), after=lambda name, got: None):
    S = x.shape[0]
    g1, g2, g3 = small["norm_mix_g"], small["norm_ffn_g"], small["norm_final_g"]
    bg, cb, sink = small["b_gate"], small["conv_b"], small["attn_sink"]

    def hosted(name, fn, *args, **kw):
        outs, got = fn(*args, comm=tuple(before(name)), **kw)
        after(name, got)
        return outs

    (xn,) = hosted("norm_x", _rmsnorm_bf16, x, g1, "norm_x")
    cw = small["conv_w"]
    wg = jnp.concatenate([_block_diag_groups(small["lru_wa"][0]), _block_diag_groups(small["lru_wx"][0]),
                          _block_diag_groups(small["lru_wa"][1]), _block_diag_groups(small["lru_wx"][1])],
                         axis=2).astype(bf16)
    zeros5 = jnp.zeros((5, D), f32)
    lp = jnp.stack([jnp.concatenate([small["lru_lambda"][d:d + 1], small["lru_ba"][d:d + 1],
                                     small["lru_bx"][d:d + 1], zeros5], axis=0) for d in range(2)])
    (proj,) = hosted("inproj", _matmul_t, xn, env["w_in_t"], "inproj", tm=4096, tn=512,
                     row_block=lambda j: jnp.where(j < 6, j, jnp.where(j < 10, j + 1, 6)))
    uc = _conv_fwd(proj, cw, cb)
    (hf,), _ = _lru_fwd(uc, wg, lp, False)
    (hb,), _ = _lru_fwd(uc, wg, lp, True)
    yb, attn_stats = hosted("attn_fwd", _attn_fwd, proj, sink)
    merged, x1 = _merge_outproj(x, hf, hb, proj, yb, bg, env["w_out"])
    (xn2, gu), _ = _norm_matmul(x1, g2, env["w_fi_t"], "norm_ffn_in", tn=D_FF)
    ff, dx2, dx2b, loss, dg3, dgt, dup = _ffn_out_loss(gu, x1, env["w_fo"], g3, tgt)

    env["dw_fo"] = _mm_tn(ff, dx2b, "dw_ffn_out", tk=1408, tn=1024)
    dx1, dx1b, dg2 = hosted("ffn_in_bwd", _proj_bwd, [(dgt, 0, 0, D_FF), (dup, 0, 1, D_FF)], env["w_fi_t"],
                            x1, g2, dx2, "ffn_in_bwd")
    dw_gate = _mm_tn(dgt, xn2, "dw_ffn_in_gate", tk=1408, tn=1024, out_rows=2 * D_FF)
    env["dw_fi_t"] = _mm_tn(dup, xn2, "dw_ffn_in_up", tk=1408, tn=1024, into=dw_gate, row=D_FF // 1408)
    env["dw_out"] = _mm_tn(merged, dx1b, "dw_out", tk=1024, tn=1024, tmc=4096)
    dh, dgl, dz, dyb, dbg = _outproj_bwd(dx1b, env["w_out"], hf, hb, proj, yb, bg)
    dq, dk2, dv2, dsink = hosted("attn_bwd", _attn_bwd, proj, sink, dyb, attn_stats)
    dkv = jnp.concatenate([dk2[BLK:BLK + S], dv2[BLK:BLK + S]], axis=1).astype(bf16)
    duc_f, dwg_f, dp_f = hosted("lru_bwd", _lru_bwd, uc, dh, hf, wg, lp, False)
    (duc_b, dwg_b, dp_b), _ = _lru_bwd(uc, dh, hb, wg, lp, True)
    env["grads_early"] = {
        "loss": loss[:, :1], "b_gate": dbg,
        "lru_lambda": jnp.concatenate([dp_f[0:1], dp_b[0:1]], axis=0),
        "lru_wa": jnp.stack([_diag_blocks(dwg_f[:, :, :LRU_GW]), _diag_blocks(dwg_b[:, :, :LRU_GW])]),
        "lru_ba": jnp.concatenate([dp_f[1:2], dp_b[1:2]], axis=0),
        "lru_wx": jnp.stack([_diag_blocks(dwg_f[:, :, LRU_GW:]), _diag_blocks(dwg_b[:, :, LRU_GW:])]),
        "lru_bx": jnp.concatenate([dp_f[2:3], dp_b[2:3]], axis=0),
        "attn_sink": dsink[:, :N_HEADS], "norm_ffn_g": dg2, "norm_final_g": dg3,
    }
    du, dcw, dcb = hosted("conv_bwd", _conv_bwd, duc_f, duc_b, proj, cw)
    dw_in = _mm_tn(du, xn, "dw_in_u", tk=1024, tn=1024, tmc=4096, out_rows=IN_W)
    dw_in = _mm_tn(dgl, xn, "dw_in_g", tk=1024, tn=1024, tmc=4096, into=dw_in, row=1)
    dw_in = _mm_tn(dq, xn, "dw_in_q", tk=1024, tn=1024, tmc=4096, into=dw_in, row=2)
    dw_in = _mm_tn(dkv, xn, "dw_in_kv", tk=512, tn=1024, tmc=4096, into=dw_in, row=3072 // 512)
    env["dw_in_t"] = _mm_tn(dz, xn, "dw_in_z", tk=512, tn=1024, tmc=4096, into=dw_in, row=3584 // 512)
    col_pieces = [(du, 0, 0, D), (dgl, 0, 1, D), (dq, 0, 2, D), (dkv, 0, 3072 // 512, 512),
                  *[(dz, i, 3584 // 512 + i, 512) for i in range(4)]]
    dx, _, dg1 = hosted("inproj_bwd", _proj_bwd, col_pieces, env["w_in_t"], x, g1, dx1, "inproj_bwd")

    grads = dict(env["grads_early"], norm_mix_g=dg1, conv_w=dcw, conv_b=dcb)
    return dx, grads


def _adamw(gparts, w, m, v, name, tr=256):
    n, rows, cols = gparts.shape
    tr = _div_tile(rows, tr)
    c1 = 1.0 - ADAM_B1 ** ADAM_STEP
    c2 = 1.0 - ADAM_B2 ** ADAM_STEP

    def body(g_ref, w_ref, m_ref, v_ref, go_ref, d_ref, mo_ref, vo_ref):
        g = g_ref[0].astype(f32)
        for j in range(1, n):
            g = g + g_ref[j].astype(f32)
        mn = ADAM_B1 * m_ref[0] + (1.0 - ADAM_B1) * g
        vn = ADAM_B2 * v_ref[0] + (1.0 - ADAM_B2) * (g * g)
        m_hat = mn / c1
        v_hat = vn / c2
        go_ref[0] = g
        d_ref[0] = -ADAM_LR * (m_hat / (jnp.sqrt(v_hat) + ADAM_EPS) + ADAM_WD * w_ref[0])
        mo_ref[0] = mn
        vo_ref[0] = vn

    blk = pl.BlockSpec((1, tr, cols), lambda i: (0, i, 0))
    shp = jax.ShapeDtypeStruct((1, rows, cols), f32)
    return pl.pallas_call(
        body, name=name, grid=(rows // tr,),
        in_specs=[pl.BlockSpec((n, tr, cols), lambda i: (0, i, 0)), blk, blk, blk],
        out_specs=[blk, blk, blk, blk], out_shape=[shp, shp, shp, shp],
        compiler_params=_cparams())(gparts, w, m, v)


def _sum_parts(parts, name):
    n, rows, cols = parts.shape

    def body(p_ref, o_ref):
        acc = p_ref[0].astype(f32)
        for j in range(1, n):
            acc = acc + p_ref[j].astype(f32)
        o_ref[...] = acc

    return pl.pallas_call(
        body, name=name, out_shape=jax.ShapeDtypeStruct((rows, cols), f32),
        compiler_params=_cparams())(parts)


def _pack_rows(arrs, dtype=f32):
    rows, spans, at = [], [], 0
    for a in arrs:
        flat = a.reshape(-1).astype(dtype)
        nr = -(-flat.shape[0] // 1024)
        rows.append(jnp.pad(flat, (0, nr * 1024 - flat.shape[0])).reshape(nr, 1024))
        spans.append((at, nr))
        at += nr
    pad = (-at) % 16
    if pad:
        rows.append(jnp.zeros((pad, 1024), dtype))
    return jnp.concatenate(rows, axis=0), spans


def _unpack_rows(packed, spans, shapes):
    out = []
    for (at, nr), shp in zip(spans, shapes):
        n = math.prod(shp)
        out.append(packed[at:at + nr].reshape(-1)[:n].reshape(shp))
    return out


BIG = ("w_in", "w_out", "w_ffn_in", "w_ffn_out")
SMALL_REPL = ("norm_mix_g", "b_gate", "conv_b", "lru_wa", "lru_wx", "attn_sink", "norm_ffn_g", "norm_final_g")
SMALL_SHARD = ("conv_w", "lru_lambda", "lru_ba", "lru_bx")
ORDER = ("norm_mix_g", "w_in", "b_gate", "conv_w", "conv_b", "lru_lambda", "lru_wa", "lru_ba", "lru_wx",
         "lru_bx", "attn_sink", "w_out", "norm_ffn_g", "w_ffn_in", "w_ffn_out", "norm_final_g")
EARLY_F32 = ("loss", "b_gate", "lru_lambda", "lru_ba", "lru_bx", "attn_sink", "norm_ffn_g", "norm_final_g")
EARLY_BF16 = ("lru_wa", "lru_wx")
LATE = ("norm_mix_g", "conv_w", "conv_b")


def kernel(x, norm_mix_g, w_in, b_gate, conv_w, conv_b, lru_lambda, lru_wa, lru_ba, lru_wx, lru_bx, attn_sink, w_out, norm_ffn_g, w_ffn_in, w_ffn_out, norm_final_g, loss_target, m_norm_mix_g, m_w_in, m_b_gate, m_conv_w, m_conv_b, m_lru_lambda, m_lru_wa, m_lru_ba, m_lru_wx, m_lru_bx, m_attn_sink, m_w_out, m_norm_ffn_g, m_w_ffn_in, m_w_ffn_out, m_norm_final_g, v_norm_mix_g, v_w_in, v_b_gate, v_conv_w, v_conv_b, v_lru_lambda, v_lru_wa, v_lru_ba, v_lru_wx, v_lru_bx, v_attn_sink, v_w_out, v_norm_ffn_g, v_w_ffn_in, v_w_ffn_out, v_norm_final_g):
    w = dict(norm_mix_g=norm_mix_g, w_in=w_in, b_gate=b_gate, conv_w=conv_w, conv_b=conv_b, lru_lambda=lru_lambda,
             lru_wa=lru_wa, lru_ba=lru_ba, lru_wx=lru_wx, lru_bx=lru_bx, attn_sink=attn_sink, w_out=w_out,
             norm_ffn_g=norm_ffn_g, w_ffn_in=w_ffn_in, w_ffn_out=w_ffn_out, norm_final_g=norm_final_g)
    m = dict(norm_mix_g=m_norm_mix_g, w_in=m_w_in, b_gate=m_b_gate, conv_w=m_conv_w, conv_b=m_conv_b,
             lru_lambda=m_lru_lambda, lru_wa=m_lru_wa, lru_ba=m_lru_ba, lru_wx=m_lru_wx, lru_bx=m_lru_bx,
             attn_sink=m_attn_sink, w_out=m_w_out, norm_ffn_g=m_norm_ffn_g, w_ffn_in=m_w_ffn_in,
             w_ffn_out=m_w_ffn_out, norm_final_g=m_norm_final_g)
    v = dict(norm_mix_g=v_norm_mix_g, w_in=v_w_in, b_gate=v_b_gate, conv_w=v_conv_w, conv_b=v_conv_b,
             lru_lambda=v_lru_lambda, lru_wa=v_lru_wa, lru_ba=v_lru_ba, lru_wx=v_lru_wx, lru_bx=v_lru_bx,
             attn_sink=v_attn_sink, w_out=v_w_out, norm_ffn_g=v_norm_ffn_g, w_ffn_in=v_w_ffn_in,
             w_ffn_out=v_w_ffn_out, norm_final_g=v_norm_final_g)
    me = 4 * lax.axis_index("x") + 2 * lax.axis_index("y") + lax.axis_index("c")

    def shard_t(a):
        return jnp.swapaxes(a[0], 0, 1)

    def rows_parts(g):
        return g.reshape(N_DEV, -1, g.shape[1])

    shard_rows = jnp.concatenate([w[n][0] for n in SMALL_SHARD], axis=0)
    small = {n: w[n] for n in ("norm_mix_g", "b_gate", "conv_b", "attn_sink", "norm_ffn_g")}
    small["lru_wa"], small["lru_wx"] = lru_wa[0], lru_wx[0]
    small["norm_final_g"] = norm_final_g.reshape(1, D)
    env, recv = {}, {}

    def before(name):
        if name == "norm_x":
            return [(shard_t(w_in).astype(bf16), False), (shard_rows, False)]
        if name == "inproj":
            return [(w_out[0].astype(bf16), False), (w_ffn_out[0].astype(bf16), False)]
        if name == "attn_fwd":
            return [(shard_t(w_ffn_in).astype(bf16), False)]
        if name == "ffn_in_bwd":
            return [(rows_parts(env["dw_fo"]), True)]
        if name == "attn_bwd":
            return [(rows_parts(env["dw_out"]), True)]
        if name == "lru_bwd":
            return [(rows_parts(env["dw_fi_t"]), True)]
        if name == "conv_bwd":
            ge = env["grads_early"]
            p32, env["early_f32_spans"] = _pack_rows([ge[n] for n in EARLY_F32])
            p16, env["early_bf16_spans"] = _pack_rows([ge[n] for n in EARLY_BF16], bf16)
            return [(p32, False), (p16, False)]
        if name == "inproj_bwd":
            return [(rows_parts(env["dw_in_t"]), True)]
        return []

    def after(name, got):
        if name == "norm_x":
            env["w_in_t"] = got[0].reshape(IN_W, D)
            full_rows = jnp.swapaxes(got[1], 0, 1).reshape(shard_rows.shape[0], -1)
            small["conv_w"], small["lru_lambda"] = full_rows[0:4], full_rows[4:6]
            small["lru_ba"], small["lru_bx"] = full_rows[6:8], full_rows[8:10]
        elif name == "inproj":
            env["w_out"], env["w_fo"] = got[0].reshape(D, D), got[1].reshape(D_FF, D)
        elif name == "attn_fwd":
            env["w_fi_t"] = got[0].reshape(2 * D_FF, D)
        elif name == "ffn_in_bwd":
            recv["w_ffn_out"] = got[0]
        elif name == "attn_bwd":
            recv["w_out"] = got[0]
        elif name == "lru_bwd":
            recv["w_ffn_in"] = got[0]
        elif name == "conv_bwd":
            recv["early_f32"], recv["early_bf16"] = got
        elif name == "inproj_bwd":
            recv["w_in"] = got[0]

    grad_x, grads = _local_step(x[0], loss_target[0], small, env, before, after)

    outs = {}
    for name in ("w_out", "w_ffn_out"):
        outs[name] = _adamw(recv[name], w[name], m[name], v[name], "adamw_" + name)
    for name in ("w_in", "w_ffn_in"):
        t = lambda a: jnp.swapaxes(a, 1, 2)
        outs[name] = [t(r) for r in _adamw(recv[name], t(w[name]), t(m[name]), t(v[name]), "adamw_" + name)]

    small_names = SMALL_REPL + SMALL_SHARD
    late_packed, late_spans = _pack_rows([grads[n] for n in LATE])
    (got_late,) = _exchange([(late_packed, False)], "gather_late_grads")
    summed = {}
    for names, got, spans, tag in ((EARLY_F32, recv["early_f32"], env["early_f32_spans"], "early_f32"),
                                   (EARLY_BF16, recv["early_bf16"], env["early_bf16_spans"], "early_bf16"),
                                   (LATE, got_late, late_spans, "late")):
        total = _sum_parts(got, "sum_small_" + tag)
        summed.update(zip(names, _unpack_rows(total, spans, [grads[n].shape for n in names])))
    loss = summed["loss"].reshape(())
    gsm = {n: summed[n].reshape(w[n].shape) for n in SMALL_REPL}
    for n in SMALL_SHARD:
        full = summed[n]
        gsm[n] = lax.dynamic_slice_in_dim(full, me * 128, 128, axis=1).reshape(w[n].shape)
    pk = lambda dct: _pack_rows([dct[n] for n in small_names])[0]
    gp, sp = _pack_rows([gsm[n] for n in small_names])
    res = _adamw(gp[None], pk(w)[None], pk(m)[None], pk(v)[None], "adamw_small")
    sshapes = [w[n].shape for n in small_names]
    for idx, t in enumerate(res):
        for n, a in zip(small_names, _unpack_rows(t[0], sp, sshapes)):
            outs.setdefault(n, [None] * 4)[idx] = a

    result = [loss, grad_x[None]]
    for idx in range(4):
        result += [outs[n][idx] for n in ORDER]
    return tuple(result)
```

```python
import functools
import math

import jax
import jax.numpy as jnp
from jax import lax
from jax.experimental import pallas as pl
from jax.experimental.pallas import tpu as pltpu

f32 = jnp.float32
bf16 = jnp.bfloat16

D = 1024
D_FF = 2816
IN_W = 5632
N_HEADS = 16
N_KV = 4
HEAD_DIM = 64
WINDOW = 128
BLK = 128
LRU_HEADS = 16
LRU_BLOCK = 64
LRU_GROUPS = 4
LRU_GW = 256
LRU_CHUNK = 128
LRU_ROWS = 2048
RGLRU_C = 8.0
EPS = 1e-6
NEG_INF = -1e30
N_DEV = 8

ADAM_LR = 0.001
ADAM_B1 = 0.9
ADAM_B2 = 0.999
ADAM_EPS = 1e-08
ADAM_WD = 0.01
ADAM_STEP = 10

VMEM_MB = 56

C_U, C_G, C_Q, C_Z0, C_Z1, C_K, C_V = 0, 1024, 2048, 3072, 4096, 5120, 5376


def _cparams(vmem_mb=VMEM_MB):
    return pltpu.CompilerParams(vmem_limit_bytes=vmem_mb << 20)


def _div_tile(n, pref):
    if n <= pref:
        return n
    return max(t for t in range(8, pref + 1, 8) if n % t == 0)


def _sigmoid(x):
    return 0.5 * jnp.tanh(0.5 * x) + 0.5


def _log1p(x):
    u = 1.0 + x
    d = u - 1.0
    return jnp.where(d == 0.0, x, jnp.log(u) * (x / jnp.where(d == 0.0, 1.0, d)))


def _softplus(x):
    return jnp.maximum(x, 0.0) + _log1p(jnp.exp(-jnp.abs(x)))


def _gelu_and_grad(x):
    c = math.sqrt(2.0 / math.pi)
    inner = c * (x + 0.044715 * (x * x * x))
    t = jnp.tanh(inner)
    gelu = 0.5 * x * (1.0 + t)
    dinner = c * (1.0 + 3 * 0.044715 * (x * x))
    dgelu = 0.5 * (1.0 + t) + 0.5 * x * (1.0 - t * t) * dinner
    return gelu, dgelu


def _rms_bwd(dn, xv, g):
    r = lax.rsqrt(jnp.mean(xv * xv, axis=-1, keepdims=True) + EPS)
    xh = xv * r
    dxh = dn * g
    dx = r * (dxh - xh * jnp.mean(dxh * xh, axis=-1, keepdims=True))
    return dx, dn * xh


ANY_SPEC = pl.BlockSpec(memory_space=pl.ANY)


def _comm_out_shape(src, scatter):
    return jax.ShapeDtypeStruct((N_DEV, *(src.shape[1:] if scatter else src.shape)), src.dtype)


def _comm_sems():
    return [pltpu.SemaphoreType.DMA((N_DEV - 1,)), pltpu.SemaphoreType.DMA((N_DEV - 1,)), pltpu.SemaphoreType.DMA]


def _scatter_descs(src_ref, out_ref, send_sems, recv_sems, local_sem):
    x, y, c = lax.axis_index("x"), lax.axis_index("y"), lax.axis_index("c")
    me = 4 * x + 2 * y + c
    descs = [pltpu.make_async_copy(src_ref.at[me], out_ref.at[me], local_sem)]
    for k in range(1, N_DEV):
        px, py, pc = x ^ (k >> 2), y ^ ((k >> 1) & 1), c ^ (k & 1)
        descs.append(pltpu.make_async_remote_copy(
            src_ref=src_ref.at[4 * px + 2 * py + pc], dst_ref=out_ref.at[me],
            send_sem=send_sems.at[k - 1], recv_sem=recv_sems.at[k - 1],
            device_id=(px, py, pc), device_id_type=pl.DeviceIdType.MESH))
    return descs


def _gather_copies(src_ref, out_ref, send_sems, recv_sems, local_sem, which):
    x, y, c = lax.axis_index("x"), lax.axis_index("y"), lax.axis_index("c")
    me, sibling = (x, y, c), (x, y, 1 - c)
    chips = [(1 - x, y), (x, 1 - y), (1 - x, 1 - y)]

    def slot(px, py, pc):
        return out_ref.at[4 * px + 2 * py + pc]

    def copy(k, block, to, src=None):
        return pltpu.make_async_remote_copy(
            src_ref=slot(*block) if src is None else src, dst_ref=slot(*block),
            send_sem=send_sems.at[k], recv_sem=recv_sems.at[k], device_id=to, device_id_type=pl.DeviceIdType.MESH)

    make = {
        "local": lambda: pltpu.make_async_copy(src_ref, slot(*me), local_sem),
        "first": lambda: [copy(0, me, sibling, src=src_ref)] + [copy(1 + j, me, (*chip, c), src=src_ref)
                                                                 for j, chip in enumerate(chips)],
        "passed": lambda: [copy(4 + j, (*chip, c), sibling) for j, chip in enumerate(chips)],
        "landed": lambda: [copy(1 + j, (*chip, c), me) for j, chip in enumerate(chips)],
        "later": lambda: [copy(0, sibling, me)] + [copy(4 + j, (*chip, 1 - c), me) for j, chip in enumerate(chips)],
    }
    return [make[name]() for name in which]


def _comm_start(src_ref, out_ref, sems, scatter):
    if scatter:
        for d in _scatter_descs(src_ref, out_ref, *sems):
            d.start()
    else:
        local, first = _gather_copies(src_ref, out_ref, *sems, which=("local", "first"))
        local.start()
        for cp in first:
            cp.start()


def _comm_pass_on(src_ref, out_ref, sems, scatter):
    if not scatter:
        landed, passed = _gather_copies(src_ref, out_ref, *sems, which=("landed", "passed"))
        for arrived, onward in zip(landed, passed):
            arrived.wait_recv()
            onward.start()


def _comm_finish(src_ref, out_ref, sems, scatter):
    if scatter:
        for d in _scatter_descs(src_ref, out_ref, *sems):
            d.wait()
    else:
        later, first, passed, local = _gather_copies(src_ref, out_ref, *sems,
                                                     which=("later", "first", "passed", "local"))
        for cp in later:
            cp.wait_recv()
        for cp in first + passed:
            cp.wait_send()
        local.wait()


def _exchange(comm, name):
    nc = len(comm)

    def body(*refs):
        srcs, outs, sems = refs[:nc], refs[nc:2 * nc], refs[2 * nc:]
        for stage in (_comm_start, _comm_pass_on, _comm_finish):
            for i in range(nc):
                stage(srcs[i], outs[i], sems[3 * i:3 * i + 3], comm[i][1])

    return pl.pallas_call(
        body, name=name, in_specs=[ANY_SPEC] * nc, out_specs=[ANY_SPEC] * nc,
        out_shape=[_comm_out_shape(*c) for c in comm],
        scratch_shapes=[s for _ in comm for s in _comm_sems()],
    )(*[c[0] for c in comm])


def _hosted_call(body, *, name, grid, in_specs, out_specs, out_shape, args, scratch_shapes=(), comm=()):
    nin, nout, nscr, nc = len(in_specs), len(out_specs), len(scratch_shapes), len(comm)
    steps = math.prod(grid)

    def wrapped(*refs):
        ins = refs[:nin]
        csrc = refs[nin:nin + nc]
        outs = refs[nin + nc:nin + nc + nout]
        cout = refs[nin + nc + nout:nin + 2 * nc + nout]
        scr = refs[nin + 2 * nc + nout:]
        sems = scr[nscr:]

        def at(step, stage):
            lin = 0
            for a in range(len(grid)):
                lin = lin * grid[a] + pl.program_id(a)

            @pl.when(lin == step)
            def _():
                for i in range(nc):
                    stage(csrc[i], cout[i], sems[3 * i:3 * i + 3], comm[i][1])

        if nc:
            at(0, _comm_start)

        body(*ins, *outs, *scr[:nscr])

        if nc:
            at((3 * (steps - 1)) // 4, _comm_pass_on)
            at(steps - 1, _comm_finish)

    res = pl.pallas_call(
        wrapped, name=name, grid=grid,
        in_specs=[*in_specs, *[ANY_SPEC] * nc], out_specs=[*out_specs, *[ANY_SPEC] * nc],
        out_shape=[*out_shape, *[_comm_out_shape(*c) for c in comm]],
        scratch_shapes=[*scratch_shapes, *[s for _ in comm for s in _comm_sems()]],
        compiler_params=_cparams())(*args, *[c[0] for c in comm])
    return res[:nout], res[nout:]


def _rmsnorm_bf16(x, g, name, tm=1024, comm=()):
    S, dm = x.shape
    tm = min(tm, S)

    def body(x_ref, g_ref, xn_ref):
        xv = x_ref[...]
        r = lax.rsqrt(jnp.mean(xv * xv, axis=-1, keepdims=True) + EPS)
        xn_ref[...] = ((xv * r) * g_ref[...]).astype(bf16)

    row = pl.BlockSpec((tm, dm), lambda i: (i, 0))
    return _hosted_call(
        body, name=name, grid=(S // tm,), in_specs=[row, pl.BlockSpec((1, dm), lambda i: (0, 0))],
        out_specs=[row], out_shape=[jax.ShapeDtypeStruct((S, dm), bf16)], args=(x, g), comm=comm)


def _matmul_t(a, wt, name, tm=2048, tn=512, row_block=lambda j: j, comm=()):
    S, dm = a.shape
    n = wt.shape[0]
    tm = min(tm, S)

    def body(a_ref, w_ref, o_ref):
        o_ref[...] = lax.dot_general(a_ref[...], w_ref[...], (((1,), (1,)), ((), ())),
                                     preferred_element_type=f32).astype(bf16)

    return _hosted_call(
        body, name=name, grid=(S // tm, n // tn),
        in_specs=[pl.BlockSpec((tm, dm), lambda i, j: (i, 0)),
                  pl.BlockSpec((tn, dm), lambda i, j: (row_block(j), 0))],
        out_specs=[pl.BlockSpec((tm, tn), lambda i, j: (i, j))],
        out_shape=[jax.ShapeDtypeStruct((S, n), bf16)], args=(a, wt), comm=comm)


def _norm_matmul(x, g, wt, name, tm=1024, tn=1408, row_block=lambda j: j, comm=()):
    S, dm = x.shape
    n = wt.shape[0]
    tm = min(tm, S)

    def body(x_ref, g_ref, w_ref, xn_ref, o_ref):
        @pl.when(pl.program_id(1) == 0)
        def _():
            xv = x_ref[...]
            r = lax.rsqrt(jnp.mean(xv * xv, axis=-1, keepdims=True) + EPS)
            xn_ref[...] = ((xv * r) * g_ref[...]).astype(bf16)

        o_ref[...] = lax.dot_general(xn_ref[...], w_ref[...], (((1,), (1,)), ((), ())),
                                     preferred_element_type=f32).astype(bf16)

    return _hosted_call(
        body, name=name, grid=(S // tm, n // tn),
        in_specs=[pl.BlockSpec((tm, dm), lambda i, j: (i, 0)),
                  pl.BlockSpec((1, dm), lambda i, j: (0, 0)),
                  pl.BlockSpec((tn, dm), lambda i, j: (row_block(j), 0))],
        out_specs=[pl.BlockSpec((tm, dm), lambda i, j: (i, 0)),
                   pl.BlockSpec((tm, tn), lambda i, j: (i, j))],
        out_shape=[jax.ShapeDtypeStruct((S, dm), bf16), jax.ShapeDtypeStruct((S, n), bf16)],
        args=(x, g, wt), comm=comm)


def _mm_tn(a, b, name, tk, tn, tmc=2048, into=None, row=0, out_rows=None):
    m, ka = a.shape
    n = b.shape[1]
    tmc = min(tmc, m)
    nk = m // tmc

    def body(a_ref, b_ref, *rest):
        o_ref, acc_ref = rest[-2:]
        k = pl.program_id(2)
        part = lax.dot_general(a_ref[...], b_ref[...], (((0,), (0,)), ((), ())), preferred_element_type=f32)

        @pl.when(k == 0)
        def _():
            acc_ref[...] = part

        @pl.when(k > 0)
        def _():
            acc_ref[...] += part

        @pl.when(k == nk - 1)
        def _():
            o_ref[...] = acc_ref[...].astype(bf16)

    in_specs = [pl.BlockSpec((tmc, tk), lambda i, j, k: (k, i)), pl.BlockSpec((tmc, tn), lambda i, j, k: (k, j))]
    if into is None:
        return pl.pallas_call(
            body, name=name, grid=(ka // tk, n // tn, nk), in_specs=in_specs,
            out_specs=pl.BlockSpec((tk, tn), lambda i, j, k: (i + row, j)),
            out_shape=jax.ShapeDtypeStruct((out_rows or ka, n), bf16),
            scratch_shapes=[pltpu.VMEM((tk, tn), f32)],
            compiler_params=_cparams())(a, b)
    return pl.pallas_call(
        body, name=name, grid=(ka // tk, n // tn, nk), in_specs=[*in_specs, ANY_SPEC],
        out_specs=pl.BlockSpec((tk, tn), lambda i, j, k: (i + row, j)),
        out_shape=jax.ShapeDtypeStruct(into.shape, into.dtype),
        scratch_shapes=[pltpu.VMEM((tk, tn), f32)], input_output_aliases={2: 0},
        compiler_params=_cparams())(a, b, into)


HALO = 16


def _rows_at(ext, o, tc):
    if o == 0:
        return ext[HALO:HALO + tc]
    return pltpu.roll(ext, (-o) % ext.shape[0], 0)[HALO:HALO + tc]


def _halo_specs(tc, S, width, col):
    per = tc // HALO
    last = S // HALO - 1
    return (pl.BlockSpec((tc, width), lambda i: (i, col)),
            pl.BlockSpec((HALO, width), lambda i: (jnp.maximum(i * per - 1, 0), col)),
            pl.BlockSpec((HALO, width), lambda i: (jnp.minimum((i + 1) * per, last), col)))


def _extended(cur_ref, prev_ref, next_ref, i, nsteps):
    prev = jnp.where(i > 0, prev_ref[...].astype(f32), 0.0)
    nxt = jnp.where(i < nsteps - 1, next_ref[...].astype(f32), 0.0)
    return jnp.concatenate([prev, cur_ref[...].astype(f32), nxt], axis=0)


def _conv_fwd(proj, cw, cb, tc=1024):
    S = proj.shape[0]
    tc = min(tc, S)
    nsteps = S // tc

    def body(cur_ref, prev_ref, next_ref, w_ref, b_ref, o_ref):
        ext = _extended(cur_ref, prev_ref, next_ref, pl.program_id(0), nsteps)
        acc = _rows_at(ext, -2, tc) * w_ref[0:1, :]
        for k in range(1, 4):
            acc = acc + _rows_at(ext, k - 2, tc) * w_ref[k:k + 1, :]
        o_ref[...] = acc + b_ref[...]

    return pl.pallas_call(
        body, name="conv_fwd", grid=(nsteps,),
        in_specs=[*_halo_specs(tc, S, D, 0),
                  pl.BlockSpec((4, D), lambda i: (0, 0)), pl.BlockSpec((1, D), lambda i: (0, 0))],
        out_specs=pl.BlockSpec((tc, D), lambda i: (i, 0)),
        out_shape=jax.ShapeDtypeStruct((S, D), f32),
        compiler_params=_cparams())(proj, proj, proj, cw, cb)


def _conv_bwd(duc_f, duc_b, proj, cw, tc=1024, comm=()):
    S = proj.shape[0]
    tc = min(tc, S)
    nsteps = S // tc

    def body(fc, fp, fn, bc, bp, bn, uc_, up, un, w_ref, du_ref, dw_ref, db_ref):
        i = pl.program_id(0)

        @pl.when(i == 0)
        def _():
            dw_ref[...] = jnp.zeros_like(dw_ref)
            db_ref[...] = jnp.zeros_like(db_ref)

        dext = _extended(fc, fp, fn, i, nsteps) + _extended(bc, bp, bn, i, nsteps)
        uext = _extended(uc_, up, un, i, nsteps)
        d = dext[HALO:HALO + tc]
        acc = _rows_at(dext, 2, tc) * w_ref[0:1, :]
        for k in range(1, 4):
            acc = acc + _rows_at(dext, 2 - k, tc) * w_ref[k:k + 1, :]
        du_ref[...] = acc.astype(bf16)
        wrow = lax.broadcasted_iota(jnp.int32, (4, D), 0)
        for k in range(4):
            dw_ref[...] += jnp.where(wrow == k, jnp.sum(d * _rows_at(uext, k - 2, tc), axis=0, keepdims=True), 0.0)
        db_ref[...] += jnp.sum(d, axis=0, keepdims=True)

    return _hosted_call(
        body, name="conv_bwd", grid=(nsteps,),
        in_specs=[*_halo_specs(tc, S, D, 0), *_halo_specs(tc, S, D, 0), *_halo_specs(tc, S, D, 0),
                  pl.BlockSpec((4, D), lambda i: (0, 0))],
        out_specs=[pl.BlockSpec((tc, D), lambda i: (i, 0)),
                   pl.BlockSpec((4, D), lambda i: (0, 0)), pl.BlockSpec((1, D), lambda i: (0, 0))],
        out_shape=[jax.ShapeDtypeStruct((S, D), bf16), jax.ShapeDtypeStruct((4, D), f32),
                   jax.ShapeDtypeStruct((1, D), f32)],
        args=(duc_f, duc_f, duc_f, duc_b, duc_b, duc_b, proj, proj, proj, cw), comm=comm)


def _scan_scratch():
    halves = [pltpu.VMEM((LRU_CHUNK, 128), f32) for _ in range(2 * (LRU_GW // 128))]
    return [*halves, pltpu.VMEM((LRU_CHUNK // 8, LRU_GW), f32), pltpu.VMEM((LRU_CHUNK // 8, LRU_GW), f32)]


def _log_scan(a, b, row, n, reverse, steps):
    for s in steps:
        shift = a.shape[0] - s if reverse else s
        keep = (row < n - s) if reverse else (row >= s)
        a_sh = pltpu.roll(a, shift, 0)
        b_sh = pltpu.roll(b, shift, 0)
        b = jnp.where(keep, a * b_sh + b, b)
        a = jnp.where(keep, a * a_sh, a)
    return a, b


def _scan_chunk(a, b, carry, reverse, *scratch):
    tc, w = a.shape
    ng = tc // 8
    nl = w // 128
    sa_refs, sb_refs, sc_ref, st_ref = scratch[:nl], scratch[nl:2 * nl], scratch[2 * nl], scratch[2 * nl + 1]
    sub = lax.broadcasted_iota(jnp.int32, (8, w), 0)
    ag, bg = [], []
    for k in range(ng):
        ak, bk = _log_scan(a[8 * k:8 * k + 8], b[8 * k:8 * k + 8], sub, 8, reverse, (1, 2, 4))
        ag.append(ak)
        bg.append(bk)
    a = jnp.concatenate(ag, axis=0)
    b = jnp.concatenate(bg, axis=0)
    edge = 0 if reverse else 7
    for i in range(nl):
        sa_refs[i][...] = a[:, 128 * i:128 * (i + 1)]
        sb_refs[i][...] = b[:, 128 * i:128 * (i + 1)]
    ta = jnp.concatenate([r[pl.ds(edge, ng, stride=8), :] for r in sa_refs], axis=1)
    tb = jnp.concatenate([r[pl.ds(edge, ng, stride=8), :] for r in sb_refs], axis=1)
    grow = lax.broadcasted_iota(jnp.int32, (ng, w), 0)
    ta, tb = _log_scan(ta, tb, grow, ng, reverse, [1 << i for i in range(ng.bit_length() - 1)])
    state = tb + ta * carry
    st_ref[...] = state
    if reverse:
        sc_ref[...] = jnp.where(grow == ng - 1, carry, pltpu.roll(state, ng - 1, 0))
    else:
        sc_ref[...] = jnp.where(grow == 0, carry, pltpu.roll(state, 1, 0))
    h = jnp.concatenate([bg[k] + ag[k] * sc_ref[k:k + 1, :] for k in range(ng)], axis=0)
    return h, (st_ref[0:1, :] if reverse else st_ref[ng - 1:ng, :])


def _lru_gates(uc, w, p_ref):
    pre = jnp.dot(uc.astype(bf16), w, preferred_element_type=f32)
    r = _sigmoid(pre[:, :LRU_GW] + p_ref[0, 1:2, :])
    gi = _sigmoid(pre[:, LRU_GW:] + p_ref[0, 2:3, :])
    sp = _softplus(-p_ref[0, 0:1, :])
    log_a = -RGLRU_C * r * sp
    a = jnp.exp(log_a)
    x = 2.0 * log_a
    series = -x * (1.0 + x * (0.5 + x * (1.0 / 6 + x * (1.0 / 24))))
    beta = jnp.sqrt(jnp.maximum(jnp.where(x > -0.0625, series, 1.0 - a * a), 0.0))
    return r, gi, sp, a, beta


def _lru_fwd(uc, wg, lp, reverse, comm=()):
    S = uc.shape[0]
    tc = LRU_CHUNK
    rows = min(LRU_ROWS, S)
    nsub = rows // tc
    nblk = S // rows
    d = 1 if reverse else 0

    def bidx(c):
        return nblk - 1 - c if reverse else c

    def body(uc_ref, w_ref, p_ref, h_ref, carry_ref, *scan_scratch):
        @pl.when(pl.program_id(1) == 0)
        def _():
            carry_ref[...] = jnp.zeros_like(carry_ref)

        carry = carry_ref[...]
        for j in (reversed(range(nsub)) if reverse else range(nsub)):
            sl = slice(j * tc, (j + 1) * tc)
            ucv = uc_ref[sl, :]
            _, gi, _, a, beta = _lru_gates(ucv, w_ref[0], p_ref)
            h, carry = _scan_chunk(a, beta * (gi * ucv), carry, reverse, *scan_scratch)
            h_ref[sl, :] = h.astype(bf16)
        carry_ref[...] = carry

    return _hosted_call(
        body, name="lru_fwd_rev" if reverse else "lru_fwd", grid=(LRU_GROUPS, nblk),
        in_specs=[pl.BlockSpec((rows, LRU_GW), lambda g, c: (bidx(c), g)),
                  pl.BlockSpec((1, LRU_GW, 2 * LRU_GW), lambda g, c: (g, 0, d)),
                  pl.BlockSpec((1, 8, LRU_GW), lambda g, c: (d, 0, g))],
        out_specs=[pl.BlockSpec((rows, LRU_GW), lambda g, c: (bidx(c), g))],
        out_shape=[jax.ShapeDtypeStruct((S, D), bf16)],
        scratch_shapes=[pltpu.VMEM((1, LRU_GW), f32), *_scan_scratch()],
        args=(uc, wg, lp), comm=comm)


def _lru_bwd(uc, dh, h, wg, lp, reverse, comm=()):
    S = uc.shape[0]
    tc = LRU_CHUNK
    rows = min(LRU_ROWS, S)
    nsub = rows // tc
    nblk = S // rows
    d = 1 if reverse else 0
    per = rows // HALO
    last8 = S // HALO - 1

    def bidx(c):
        return c if reverse else nblk - 1 - c

    def halo_idx(c):
        if reverse:
            return jnp.minimum((bidx(c) + 1) * per, last8)
        return jnp.maximum(bidx(c) * per - 1, 0)

    def body(uc_ref, dh_ref, h_ref, halo_ref, w_ref, p_ref, duc_ref, dw_ref, dp_ref, carry_ref, tmp_ref,
             *scan_scratch):
        c = pl.program_id(1)
        bi = bidx(c)

        @pl.when(c == 0)
        def _():
            carry_ref[...] = jnp.zeros_like(carry_ref)
            dw_ref[...] = jnp.zeros_like(dw_ref)
            dp_ref[...] = jnp.zeros_like(dp_ref)

        row = lax.broadcasted_iota(jnp.int32, (tc, LRU_GW), 0)
        carry = carry_ref[...]
        dw = jnp.zeros((LRU_GW, 2 * LRU_GW), f32)
        dsp = jnp.zeros((1, LRU_GW), f32)
        dba = jnp.zeros((1, LRU_GW), f32)
        dbx = jnp.zeros((1, LRU_GW), f32)
        for j in (range(nsub) if reverse else reversed(range(nsub))):
            sl = slice(j * tc, (j + 1) * tc)
            ucv = uc_ref[sl, :]
            ucb = ucv.astype(bf16)
            r, gi, sp, a, beta = _lru_gates(ucv, w_ref[0], p_ref)
            hv = h_ref[sl, :].astype(f32)
            dhv = dh_ref[sl, :].astype(f32)
            if reverse:
                alpha = jnp.where(row == 0, 1.0, pltpu.roll(a, 1, 0))
                gsc, _ = _scan_chunk(alpha, dhv, carry, False, *scan_scratch)
                if j < nsub - 1:
                    edge = h_ref[(j + 1) * tc:(j + 1) * tc + HALO, :].astype(f32)[0:1, :]
                else:
                    edge = jnp.where(bi < nblk - 1, halo_ref[...].astype(f32)[0:1, :], 0.0)
                h_nb = jnp.where(row == tc - 1, edge, pltpu.roll(hv, tc - 1, 0))
            else:
                alpha = jnp.where(row == tc - 1, 1.0, pltpu.roll(a, tc - 1, 0))
                gsc, _ = _scan_chunk(alpha, dhv, carry, True, *scan_scratch)
                if j > 0:
                    edge = h_ref[j * tc - HALO:j * tc, :].astype(f32)[HALO - 1:HALO, :]
                else:
                    edge = jnp.where(bi > 0, halo_ref[...].astype(f32)[HALO - 1:HALO, :], 0.0)
                h_nb = jnp.where(row == 0, edge, pltpu.roll(hv, 1, 0))
            tmp_ref[...] = a * gsc
            carry = tmp_ref[tc - 1:tc, :] if reverse else tmp_ref[0:1, :]

            da = gsc * h_nb
            dbeta = gsc * (gi * ucv)
            dl = da * a - dbeta * (a * a) / beta
            dr = dl * (-RGLRU_C * sp)
            dsp = dsp + jnp.sum(dl * (-RGLRU_C * r), axis=0, keepdims=True)
            dgi = gsc * beta * ucv
            dpre_r = dr * r * (1.0 - r)
            dpre_i = dgi * gi * (1.0 - gi)
            dba = dba + jnp.sum(dpre_r, axis=0, keepdims=True)
            dbx = dbx + jnp.sum(dpre_i, axis=0, keepdims=True)
            dpre = jnp.concatenate([dpre_r, dpre_i], axis=1).astype(bf16)
            back = lax.dot_general(dpre, w_ref[0], (((1,), (1,)), ((), ())), preferred_element_type=f32)
            duc_ref[sl, :] = (gsc * beta * gi + back).astype(bf16)
            dw = dw + lax.dot_general(ucb, dpre, (((0,), (0,)), ((), ())), preferred_element_type=f32)
        carry_ref[...] = carry
        dw_ref[0] += dw
        dlam = -dsp / (1.0 + jnp.exp(p_ref[0, 0:1, :]))
        prow = lax.broadcasted_iota(jnp.int32, (8, LRU_GW), 0)
        dp_ref[...] += (jnp.where(prow == 0, dlam, 0.0) + jnp.where(prow == 1, dba, 0.0)
                        + jnp.where(prow == 2, dbx, 0.0))

    chunk = pl.BlockSpec((rows, LRU_GW), lambda g, c: (bidx(c), g))
    return _hosted_call(
        body, name="lru_bwd_rev" if reverse else "lru_bwd", grid=(LRU_GROUPS, nblk),
        in_specs=[chunk, chunk, chunk,
                  pl.BlockSpec((HALO, LRU_GW), lambda g, c: (halo_idx(c), g)),
                  pl.BlockSpec((1, LRU_GW, 2 * LRU_GW), lambda g, c: (g, 0, d)),
                  pl.BlockSpec((1, 8, LRU_GW), lambda g, c: (d, 0, g))],
        out_specs=[chunk,
                   pl.BlockSpec((1, LRU_GW, 2 * LRU_GW), lambda g, c: (g, 0, 0)),
                   pl.BlockSpec((8, LRU_GW), lambda g, c: (0, g))],
        out_shape=[jax.ShapeDtypeStruct((S, D), bf16),
                   jax.ShapeDtypeStruct((LRU_GROUPS, LRU_GW, 2 * LRU_GW), f32),
                   jax.ShapeDtypeStruct((8, D), f32)],
        scratch_shapes=[pltpu.VMEM((1, LRU_GW), f32), pltpu.VMEM((tc, LRU_GW), f32), *_scan_scratch()],
        args=(uc, dh, h, h, wg, lp), comm=comm)


def _slope(h):
    return 2.0 ** (-8.0 * (h + 1.0) / N_HEADS)


ATT_QB = 4


def _kv_specs(nb, col):
    return [pl.BlockSpec((BLK, N_KV * HEAD_DIM), lambda n: (jnp.maximum(ATT_QB * n - 1, 0), col)),
            pl.BlockSpec((ATT_QB * BLK, N_KV * HEAD_DIM), lambda n: (n, col)),
            pl.BlockSpec((BLK, N_KV * HEAD_DIM), lambda n: (jnp.minimum(ATT_QB * (n + 1), nb - 1), col))]


def _key_blocks(prev_ref, cur_ref, next_ref):
    return [prev_ref[...], *[cur_ref[BLK * s:BLK * (s + 1), :] for s in range(ATT_QB)], next_ref[...]]


def _dup_windows(r0, r1, r2):
    left = lax.broadcasted_iota(jnp.int32, (3 * BLK, 128), 1) < HEAD_DIM
    win = jnp.concatenate([r0, r1, r2], axis=0)
    out = []
    for i in range(N_KV // 2):
        t = win[:, i * 128:(i + 1) * 128]
        r = pltpu.roll(t, HEAD_DIM, 1)
        out += [jnp.where(left, t, r).astype(bf16), jnp.where(left, r, t).astype(bf16)]
    return out


def _attn_bias_init(bias_ref):
    k_loc = lax.broadcasted_iota(jnp.int32, (3 * BLK, BLK), 0)
    q_loc = lax.broadcasted_iota(jnp.int32, (3 * BLK, BLK), 1)
    adist = jnp.abs(q_loc + BLK - k_loc)
    adf = adist.astype(f32)
    for e in range(3):
        ok = adist <= WINDOW
        if e == 0:
            ok = ok & (k_loc >= BLK)
        if e == 2:
            ok = ok & (k_loc < 2 * BLK)
        for kv in range(N_KV):
            bias_ref[e, kv] = jnp.concatenate(
                [jnp.where(ok, (-_slope(4 * kv + j)) * adf, NEG_INF) for j in range(4)], axis=1)


def _stack_heads(ref, sub, kv, scale):
    left = lax.broadcasted_iota(jnp.int32, (BLK, 128), 1) < HEAD_DIM
    rows = []
    for pp in range(2):
        t = ref[BLK * sub:BLK * (sub + 1), (2 * kv + pp) * 128:(2 * kv + pp + 1) * 128]
        if scale != 1.0:
            t = t * scale
        zero = jnp.zeros_like(t)
        rows += [jnp.where(left, t, zero).astype(bf16), jnp.where(left, zero, t).astype(bf16)]
    return jnp.concatenate(rows, axis=0)


def _attn_softmax(qs, k2, bias, sink_ref, kv, stats=None):
    sink = jnp.concatenate([jnp.full((1, BLK), sink_ref[0, 4 * kv + j], f32) for j in range(4)], axis=1)
    s = lax.dot_general(k2, qs, (((1,), (1,)), ((), ())), preferred_element_type=f32) + bias
    m = jnp.maximum(jnp.max(s, axis=0, keepdims=True), sink) if stats is None else stats[0]
    p = jnp.exp(s - m)
    ps = jnp.exp(sink - m)
    inv = 1.0 / (jnp.sum(p, axis=0, keepdims=True) + ps) if stats is None else stats[1]
    return p, ps, m, inv


def _pair_tiles(t):
    return [jnp.concatenate([t[:HEAD_DIM, 256 * pp:256 * pp + 128],
                             t[HEAD_DIM:, 256 * pp + 128:256 * pp + 256]], axis=0).T for pp in range(2)]


def _attn_fwd(proj, sink, comm=()):
    S = proj.shape[0]
    nb = S // BLK
    assert nb >= 2 and nb % ATT_QB == 0

    def body(q_ref, k0, k1, k2_, v0, v1, v2_, sink_ref, o_ref, st_ref, bias_ref):
        n = pl.program_id(0)

        @pl.when(n == 0)
        def _():
            _attn_bias_init(bias_ref)

        kb = _key_blocks(k0, k1, k2_)
        vb = _key_blocks(v0, v1, v2_)
        for sub in range(ATT_QB):
            blk = ATT_QB * n + sub
            e = jnp.where(blk == 0, 0, jnp.where(blk == nb - 1, 2, 1))
            kk = _dup_windows(*kb[sub:sub + 3])
            vv = _dup_windows(*vb[sub:sub + 3])
            tiles = []
            for kv in range(N_KV):
                qs = _stack_heads(q_ref, sub, kv, HEAD_DIM ** -0.5)
                p, _, m, inv = _attn_softmax(qs, kk[kv], bias_ref[e, kv], sink_ref, kv)
                st_ref[sub, kv:kv + 1, :] = m
                st_ref[sub, N_KV + kv:N_KV + kv + 1, :] = inv
                ot = lax.dot_general(vv[kv], p.astype(bf16), (((0,), (0,)), ((), ())), preferred_element_type=f32)
                tiles += _pair_tiles(ot * inv)
            o_ref[BLK * sub:BLK * (sub + 1), :] = jnp.concatenate(tiles, axis=1).astype(bf16)

    return _hosted_call(
        body, name="attn_fwd", grid=(nb // ATT_QB,),
        in_specs=[pl.BlockSpec((ATT_QB * BLK, D), lambda n: (n, C_Q // D)),
                  *_kv_specs(nb, C_K // (N_KV * HEAD_DIM)), *_kv_specs(nb, C_V // (N_KV * HEAD_DIM)),
                  pl.BlockSpec(memory_space=pltpu.SMEM)],
        out_specs=[pl.BlockSpec((ATT_QB * BLK, D), lambda n: (n, 0)),
                   pl.BlockSpec((ATT_QB, 2 * N_KV, 4 * BLK), lambda n: (n, 0, 0))],
        out_shape=[jax.ShapeDtypeStruct((S, D), bf16), jax.ShapeDtypeStruct((nb, 2 * N_KV, 4 * BLK), f32)],
        scratch_shapes=[pltpu.VMEM((3, N_KV, 3 * BLK, 4 * BLK), f32)],
        args=(proj, proj, proj, proj, proj, proj, proj, sink), comm=comm)


def _attn_bwd(proj, sink, dyb, stats, comm=()):
    S = proj.shape[0]
    nb = S // BLK
    assert nb >= 2 and nb % ATT_QB == 0
    nsteps = nb // ATT_QB

    def body(q_ref, k0, k1, k2_, v0, v1, v2_, sink_ref, do_ref, st_ref, dq_ref, dk_out, dv_out, ds_ref,
             bias_ref, dk_ref, dv_ref, dsk_ref):
        n = pl.program_id(0)

        @pl.when(n == 0)
        def _():
            _attn_bias_init(bias_ref)
            dk_ref[...] = jnp.zeros_like(dk_ref)
            dv_ref[...] = jnp.zeros_like(dv_ref)
            dsk_ref[...] = jnp.zeros_like(dsk_ref)

        kb = _key_blocks(k0, k1, k2_)
        vb = _key_blocks(v0, v1, v2_)
        left3 = lax.broadcasted_iota(jnp.int32, (3 * BLK, 128), 1) < HEAD_DIM
        for sub in range(ATT_QB):
            blk = ATT_QB * n + sub
            e = jnp.where(blk == 0, 0, jnp.where(blk == nb - 1, 2, 1))
            kk = _dup_windows(*kb[sub:sub + 3])
            vv = _dup_windows(*vb[sub:sub + 3])
            start = pl.multiple_of(blk * BLK, BLK)
            dq_tiles, dks, dvs = [], [], []
            for kv in range(N_KV):
                qs = _stack_heads(q_ref, sub, kv, HEAD_DIM ** -0.5)
                dos = _stack_heads(do_ref, sub, kv, 1.0)
                stats = (st_ref[sub, kv:kv + 1, :], st_ref[sub, N_KV + kv:N_KV + kv + 1, :])
                p, ps, _, inv = _attn_softmax(qs, kk[kv], bias_ref[e, kv], sink_ref, kv, stats)
                pn = p * inv
                dp = lax.dot_general(vv[kv], dos, (((1,), (1,)), ((), ())), preferred_element_type=f32)
                delta = jnp.sum(pn * dp, axis=0, keepdims=True)
                dsc = (pn * (dp - delta)).astype(bf16)
                dsk_ref[kv:kv + 1, :] += delta * (ps * inv)
                dqt = lax.dot_general(kk[kv], dsc, (((0,), (0,)), ((), ())), preferred_element_type=f32)
                dq_tiles += _pair_tiles(dqt * (HEAD_DIM ** -0.5))
                dk = jnp.dot(dsc, qs, preferred_element_type=f32)
                dv = jnp.dot(pn.astype(bf16), dos, preferred_element_type=f32)
                dks.append(dk + pltpu.roll(dk, HEAD_DIM, 1))
                dvs.append(dv + pltpu.roll(dv, HEAD_DIM, 1))
            for jp in range(N_KV // 2):
                cols = slice(jp * 128, (jp + 1) * 128)
                dk_ref[pl.ds(start, 3 * BLK), cols] += jnp.where(left3, dks[2 * jp], dks[2 * jp + 1])
                dv_ref[pl.ds(start, 3 * BLK), cols] += jnp.where(left3, dvs[2 * jp], dvs[2 * jp + 1])
            dq_ref[BLK * sub:BLK * (sub + 1), :] = jnp.concatenate(dq_tiles, axis=1).astype(bf16)

        @pl.when(n == nsteps - 1)
        def _():
            pltpu.sync_copy(dk_ref, dk_out)
            pltpu.sync_copy(dv_ref, dv_out)
            lane = lax.broadcasted_iota(jnp.int32, (1, 128), 1)
            dsink = jnp.zeros((1, 128), f32)
            for h in range(N_HEADS):
                part = dsk_ref[h // 4:h // 4 + 1, (h % 4) * BLK:(h % 4 + 1) * BLK]
                dsink = dsink + jnp.where(lane == h, -jnp.sum(part), 0.0)
            ds_ref[...] = dsink

    acc = jax.ShapeDtypeStruct((S + 2 * BLK, N_KV * HEAD_DIM), f32)
    return _hosted_call(
        body, name="attn_bwd", grid=(nsteps,),
        in_specs=[pl.BlockSpec((ATT_QB * BLK, D), lambda n: (n, C_Q // D)),
                  *_kv_specs(nb, C_K // (N_KV * HEAD_DIM)), *_kv_specs(nb, C_V // (N_KV * HEAD_DIM)),
                  pl.BlockSpec(memory_space=pltpu.SMEM),
                  pl.BlockSpec((ATT_QB * BLK, D), lambda n: (n, 0)),
                  pl.BlockSpec((ATT_QB, 2 * N_KV, 4 * BLK), lambda n: (n, 0, 0))],
        out_specs=[pl.BlockSpec((ATT_QB * BLK, D), lambda n: (n, 0)), ANY_SPEC, ANY_SPEC,
                   pl.BlockSpec((1, 128), lambda n: (0, 0))],
        out_shape=[jax.ShapeDtypeStruct((S, D), bf16), acc, acc, jax.ShapeDtypeStruct((1, 128), f32)],
        scratch_shapes=[pltpu.VMEM((3, N_KV, 3 * BLK, 4 * BLK), f32), pltpu.VMEM(acc.shape, f32),
                        pltpu.VMEM(acc.shape, f32), pltpu.VMEM((8, 4 * BLK), f32)],
        args=(proj, proj, proj, proj, proj, proj, proj, sink, dyb, stats), comm=comm)


def _merge_parts(hf, hb, g, z0, z1, yb, bg):
    g0 = _sigmoid(z0 + bg[:, :D].astype(bf16))
    g1 = _sigmoid(z1 + bg[:, D:].astype(bf16))
    gelu, dgelu = _gelu_and_grad(g)
    hs = hf + hb
    ya = hs * gelu
    return g0, g1, gelu, dgelu, hs, ya


def _merge_outproj(x, hf, hb, proj, yb, bg, w_out, tm=1024):
    S = x.shape[0]
    tm = min(tm, S)

    def body(x_ref, hf_ref, hb_ref, g_ref, z0_ref, z1_ref, yb_ref, bg_ref, w_ref, mg_ref, x1_ref):
        ybv = yb_ref[...]
        g0, g1, _, _, _, ya = _merge_parts(hf_ref[...], hb_ref[...], g_ref[...], z0_ref[...], z1_ref[...],
                                           ybv, bg_ref[...])
        mg = g0 * ya + g1 * ybv
        mg_ref[...] = mg
        x1_ref[...] = x_ref[...] + jnp.dot(mg, w_ref[...], preferred_element_type=f32)

    row = pl.BlockSpec((tm, D), lambda i: (i, 0))
    return pl.pallas_call(
        body, name="merge_outproj", grid=(S // tm,),
        in_specs=[row, row, row,
                  pl.BlockSpec((tm, D), lambda i: (i, C_G // D)),
                  pl.BlockSpec((tm, D), lambda i: (i, C_Z0 // D)),
                  pl.BlockSpec((tm, D), lambda i: (i, C_Z1 // D)),
                  row, pl.BlockSpec((1, 2 * D), lambda i: (0, 0)), pl.BlockSpec((D, D), lambda i: (0, 0))],
        out_specs=[row, row],
        out_shape=[jax.ShapeDtypeStruct((S, D), bf16), jax.ShapeDtypeStruct((S, D), f32)],
        compiler_params=_cparams())(x, hf, hb, proj, proj, proj, yb, bg, w_out)


def _ffn_out_loss(gu, x1, w_fo, g3, tgt, tm=256):
    S = x1.shape[0]
    tm = min(tm, S)

    def body(gt_ref, up_ref, x1_ref, w_ref, g_ref, t_ref, ff_ref, dx_ref, dxb_ref, loss_ref, dg_ref,
             dgt_ref, dup_ref):
        @pl.when(pl.program_id(0) == 0)
        def _():
            loss_ref[...] = jnp.zeros_like(loss_ref)
            dg_ref[...] = jnp.zeros_like(dg_ref)

        gt = gt_ref[...].astype(f32)
        up = up_ref[...].astype(f32)
        sg = _sigmoid(gt)
        silu = gt * sg
        ff = (silu * up).astype(bf16)
        ff_ref[...] = ff
        x2 = x1_ref[...] + jnp.dot(ff, w_ref[...], preferred_element_type=f32)
        gv = g_ref[...]
        r = lax.rsqrt(jnp.mean(x2 * x2, axis=-1, keepdims=True) + EPS)
        xh = x2 * r
        diff = xh * gv - t_ref[...]
        loss_ref[...] += (0.5 / D) * jnp.sum(diff * diff)
        dy = diff * (1.0 / D)
        dg_ref[...] += jnp.sum(dy * xh, axis=0, keepdims=True)
        dxh = dy * gv
        dx = r * (dxh - xh * jnp.mean(dxh * xh, axis=-1, keepdims=True))
        dx_ref[...] = dx
        dxb = dx.astype(bf16)
        dxb_ref[...] = dxb
        dff = lax.dot_general(dxb, w_ref[...], (((1,), (1,)), ((), ())), preferred_element_type=f32)
        dup_ref[...] = (dff * silu).astype(bf16)
        dgt_ref[...] = ((dff * up) * (sg * (1.0 + gt * (1.0 - sg)))).astype(bf16)

    row = pl.BlockSpec((tm, D), lambda i: (i, 0))
    vec = pl.BlockSpec((1, D), lambda i: (0, 0))
    wide = pl.BlockSpec((tm, D_FF), lambda i: (i, 0))
    wide_shape = jax.ShapeDtypeStruct((S, D_FF), bf16)
    return pl.pallas_call(
        body, name="ffn_out_loss", grid=(S // tm,),
        in_specs=[wide, pl.BlockSpec((tm, D_FF), lambda i: (i, 1)),
                  row, pl.BlockSpec((D_FF, D), lambda i: (0, 0)), vec, row],
        out_specs=[wide, row, row, pl.BlockSpec((1, 128), lambda i: (0, 0)), vec, wide, wide],
        out_shape=[wide_shape, jax.ShapeDtypeStruct((S, D), f32), jax.ShapeDtypeStruct((S, D), bf16),
                   jax.ShapeDtypeStruct((1, 128), f32), jax.ShapeDtypeStruct((1, D), f32), wide_shape, wide_shape],
        compiler_params=_cparams())(gu, gu, x1, w_fo, g3, tgt)


def _proj_bwd(pieces, wt, xres, g, dres, name, tm=512, comm=()):
    S = xres.shape[0]
    tm = min(tm, S)
    np_ = len(pieces)

    def body(*refs):
        p_refs = refs[:np_]
        w_refs = refs[np_:2 * np_]
        x_ref, g_ref, dres_ref, dx_ref, dxb_ref, dg_ref = refs[2 * np_:]

        @pl.when(pl.program_id(0) == 0)
        def _():
            dg_ref[...] = jnp.zeros_like(dg_ref)

        dn = jnp.dot(p_refs[0][...], w_refs[0][...], preferred_element_type=f32)
        for pr, wr in zip(p_refs[1:], w_refs[1:]):
            dn = dn + jnp.dot(pr[...], wr[...], preferred_element_type=f32)
        dxn, dgc = _rms_bwd(dn, x_ref[...], g_ref[...])
        dx = dres_ref[...] + dxn
        dx_ref[...] = dx
        dxb_ref[...] = dx.astype(bf16)
        dg_ref[...] += jnp.sum(dgc, axis=0, keepdims=True)

    row = pl.BlockSpec((tm, D), lambda i: (i, 0))
    vec = pl.BlockSpec((1, D), lambda i: (0, 0))
    return _hosted_call(
        body, name=name, grid=(S // tm,),
        in_specs=[*[pl.BlockSpec((tm, wd), functools.partial(lambda i, cb: (i, cb), cb=acb))
                    for _, acb, _, wd in pieces],
                  *[pl.BlockSpec((wd, D), functools.partial(lambda i, rb: (rb, 0), rb=wrb))
                    for _, _, wrb, wd in pieces],
                  row, vec, row],
        out_specs=[row, row, vec],
        out_shape=[jax.ShapeDtypeStruct((S, D), f32), jax.ShapeDtypeStruct((S, D), bf16),
                   jax.ShapeDtypeStruct((1, D), f32)],
        args=(*[p[0] for p in pieces], *[wt] * np_, xres, g, dres), comm=comm)


def _outproj_bwd(dx1b, w_out, hf, hb, proj, yb, bg, tm=1024):
    S = dx1b.shape[0]
    tm = min(tm, S)

    def body(dx_ref, w_ref, hf_ref, hb_ref, g_ref, z0_ref, z1_ref, yb_ref, bg_ref,
             dh_ref, dg_ref, dz_ref, dyb_ref, dbg_ref):
        @pl.when(pl.program_id(0) == 0)
        def _():
            dbg_ref[...] = jnp.zeros_like(dbg_ref)

        dm = lax.dot_general(dx_ref[...], w_ref[...], (((1,), (1,)), ((), ())), preferred_element_type=f32)
        ybv = yb_ref[...]
        g0, g1, gelu, dgelu, hs, ya = _merge_parts(hf_ref[...], hb_ref[...], g_ref[...], z0_ref[...],
                                                   z1_ref[...], ybv, bg_ref[...])
        dh_ref[...] = (dm * (g0 * gelu).astype(f32)).astype(bf16)
        dg_ref[...] = (dm * (g0 * hs * dgelu).astype(f32)).astype(bf16)
        dyb_ref[...] = (dm * g1.astype(f32)).astype(bf16)
        dz0 = dm * (ya * (g0 * (1.0 - g0))).astype(f32)
        dz1 = dm * (ybv * (g1 * (1.0 - g1))).astype(f32)
        dz = jnp.concatenate([dz0, dz1], axis=1)
        dz_ref[...] = dz.astype(bf16)
        dbg_ref[...] += jnp.sum(dz, axis=0, keepdims=True)

    row = pl.BlockSpec((tm, D), lambda i: (i, 0))
    return pl.pallas_call(
        body, name="outproj_bwd", grid=(S // tm,),
        in_specs=[row, pl.BlockSpec((D, D), lambda i: (0, 0)), row, row,
                  pl.BlockSpec((tm, D), lambda i: (i, C_G // D)),
                  pl.BlockSpec((tm, D), lambda i: (i, C_Z0 // D)),
                  pl.BlockSpec((tm, D), lambda i: (i, C_Z1 // D)),
                  row, pl.BlockSpec((1, 2 * D), lambda i: (0, 0))],
        out_specs=[row, row, pl.BlockSpec((tm, 2 * D), lambda i: (i, 0)), row,
                   pl.BlockSpec((1, 2 * D), lambda i: (0, 0))],
        out_shape=[jax.ShapeDtypeStruct((S, D), bf16), jax.ShapeDtypeStruct((S, D), bf16),
                   jax.ShapeDtypeStruct((S, 2 * D), bf16), jax.ShapeDtypeStruct((S, D), bf16),
                   jax.ShapeDtypeStruct((1, 2 * D), f32)],
        compiler_params=_cparams())(dx1b, w_out, hf, hb, proj, proj, proj, yb, bg)


def _block_diag_groups(w):
    w4 = w.reshape(LRU_GROUPS, 4, LRU_BLOCK, LRU_BLOCK)
    eye = jnp.eye(4, dtype=w.dtype)
    return jnp.einsum("ghij,hk->ghikj", w4, eye).reshape(LRU_GROUPS, LRU_GW, LRU_GW)


def _diag_blocks(dw):
    d5 = dw.reshape(LRU_GROUPS, 4, LRU_BLOCK, 4, LRU_BLOCK)
    return jnp.stack([d5[:, h, :, h, :] for h in range(4)], axis=1).reshape(LRU_HEADS, LRU_BLOCK, LRU_BLOCK)


def _local_step(x, tgt, small, env, before=lambda name: (), after=lambda name, got: None):
    S = x.shape[0]
    g1, g2, g3 = small["norm_mix_g"], small["norm_ffn_g"], small["norm_final_g"]
    bg, cb, sink = small["b_gate"], small["conv_b"], small["attn_sink"]

    def hosted(name, fn, *args, **kw):
        outs, got = fn(*args, comm=tuple(before(name)), **kw)
        after(name, got)
        return outs

    (xn,) = hosted("norm_x", _rmsnorm_bf16, x, g1, "norm_x")
    cw = small["conv_w"]
    wg = jnp.concatenate([_block_diag_groups(small["lru_wa"][0]), _block_diag_groups(small["lru_wx"][0]),
                          _block_diag_groups(small["lru_wa"][1]), _block_diag_groups(small["lru_wx"][1])],
                         axis=2).astype(bf16)
    zeros5 = jnp.zeros((5, D), f32)
    lp = jnp.stack([jnp.concatenate([small["lru_lambda"][d:d + 1], small["lru_ba"][d:d + 1],
                                     small["lru_bx"][d:d + 1], zeros5], axis=0) for d in range(2)])
    (proj,) = hosted("inproj", _matmul_t, xn, env["w_in_t"], "inproj", tm=4096, tn=512,
                     row_block=lambda j: jnp.where(j < 6, j, jnp.where(j < 10, j + 1, 6)))
    uc = _conv_fwd(proj, cw, cb)
    (hf,), _ = _lru_fwd(uc, wg, lp, False)
    (hb,), _ = _lru_fwd(uc, wg, lp, True)
    yb, attn_stats = hosted("attn_fwd", _attn_fwd, proj, sink)
    merged, x1 = _merge_outproj(x, hf, hb, proj, yb, bg, env["w_out"])
    (xn2, gu), _ = _norm_matmul(x1, g2, env["w_fi_t"], "norm_ffn_in", tn=D_FF)
    ff, dx2, dx2b, loss, dg3, dgt, dup = _ffn_out_loss(gu, x1, env["w_fo"], g3, tgt)

    env["dw_fo"] = _mm_tn(ff, dx2b, "dw_ffn_out", tk=1408, tn=1024)
    dx1, dx1b, dg2 = hosted("ffn_in_bwd", _proj_bwd, [(dgt, 0, 0, D_FF), (dup, 0, 1, D_FF)], env["w_fi_t"],
                            x1, g2, dx2, "ffn_in_bwd")
    dw_gate = _mm_tn(dgt, xn2, "dw_ffn_in_gate", tk=1408, tn=1024, out_rows=2 * D_FF)
    env["dw_fi_t"] = _mm_tn(dup, xn2, "dw_ffn_in_up", tk=1408, tn=1024, into=dw_gate, row=D_FF // 1408)
    env["dw_out"] = _mm_tn(merged, dx1b, "dw_out", tk=1024, tn=1024)
    dh, dgl, dz, dyb, dbg = _outproj_bwd(dx1b, env["w_out"], hf, hb, proj, yb, bg)
    dq, dk2, dv2, dsink = hosted("attn_bwd", _attn_bwd, proj, sink, dyb, attn_stats)
    dkv = jnp.concatenate([dk2[BLK:BLK + S], dv2[BLK:BLK + S]], axis=1).astype(bf16)
    duc_f, dwg_f, dp_f = hosted("lru_bwd", _lru_bwd, uc, dh, hf, wg, lp, False)
    (duc_b, dwg_b, dp_b), _ = _lru_bwd(uc, dh, hb, wg, lp, True)
    env["grads_early"] = {
        "loss": loss[:, :1], "b_gate": dbg,
        "lru_lambda": jnp.concatenate([dp_f[0:1], dp_b[0:1]], axis=0),
        "lru_wa": jnp.stack([_diag_blocks(dwg_f[:, :, :LRU_GW]), _diag_blocks(dwg_b[:, :, :LRU_GW])]),
        "lru_ba": jnp.concatenate([dp_f[1:2], dp_b[1:2]], axis=0),
        "lru_wx": jnp.stack([_diag_blocks(dwg_f[:, :, LRU_GW:]), _diag_blocks(dwg_b[:, :, LRU_GW:])]),
        "lru_bx": jnp.concatenate([dp_f[2:3], dp_b[2:3]], axis=0),
        "attn_sink": dsink[:, :N_HEADS], "norm_ffn_g": dg2, "norm_final_g": dg3,
    }
    du, dcw, dcb = hosted("conv_bwd", _conv_bwd, duc_f, duc_b, proj, cw)
    dw_in = _mm_tn(du, xn, "dw_in_u", tk=1024, tn=1024, out_rows=IN_W)
    dw_in = _mm_tn(dgl, xn, "dw_in_g", tk=1024, tn=1024, into=dw_in, row=1)
    dw_in = _mm_tn(dq, xn, "dw_in_q", tk=1024, tn=1024, into=dw_in, row=2)
    dw_in = _mm_tn(dkv, xn, "dw_in_kv", tk=512, tn=1024, into=dw_in, row=3072 // 512)
    env["dw_in_t"] = _mm_tn(dz, xn, "dw_in_z", tk=512, tn=1024, tmc=4096, into=dw_in, row=3584 // 512)
    col_pieces = [(du, 0, 0, D), (dgl, 0, 1, D), (dq, 0, 2, D), (dkv, 0, 3072 // 512, 512),
                  *[(dz, i, 3584 // 512 + i, 512) for i in range(4)]]
    dx, _, dg1 = hosted("inproj_bwd", _proj_bwd, col_pieces, env["w_in_t"], x, g1, dx1, "inproj_bwd")

    grads = dict(env["grads_early"], norm_mix_g=dg1, conv_w=dcw, conv_b=dcb)
    return dx, grads


def _adamw(gparts, w, m, v, name, tr=256):
    n, rows, cols = gparts.shape
    tr = _div_tile(rows, tr)
    c1 = 1.0 - ADAM_B1 ** ADAM_STEP
    c2 = 1.0 - ADAM_B2 ** ADAM_STEP

    def body(g_ref, w_ref, m_ref, v_ref, go_ref, d_ref, mo_ref, vo_ref):
        g = g_ref[0].astype(f32)
        for j in range(1, n):
            g = g + g_ref[j].astype(f32)
        mn = ADAM_B1 * m_ref[0] + (1.0 - ADAM_B1) * g
        vn = ADAM_B2 * v_ref[0] + (1.0 - ADAM_B2) * (g * g)
        m_hat = mn / c1
        v_hat = vn / c2
        go_ref[0] = g
        d_ref[0] = -ADAM_LR * (m_hat / (jnp.sqrt(v_hat) + ADAM_EPS) + ADAM_WD * w_ref[0])
        mo_ref[0] = mn
        vo_ref[0] = vn

    blk = pl.BlockSpec((1, tr, cols), lambda i: (0, i, 0))
    shp = jax.ShapeDtypeStruct((1, rows, cols), f32)
    return pl.pallas_call(
        body, name=name, grid=(rows // tr,),
        in_specs=[pl.BlockSpec((n, tr, cols), lambda i: (0, i, 0)), blk, blk, blk],
        out_specs=[blk, blk, blk, blk], out_shape=[shp, shp, shp, shp],
        compiler_params=_cparams())(gparts, w, m, v)


def _sum_parts(parts, name):
    n, rows, cols = parts.shape

    def body(p_ref, o_ref):
        acc = p_ref[0].astype(f32)
        for j in range(1, n):
            acc = acc + p_ref[j].astype(f32)
        o_ref[...] = acc

    return pl.pallas_call(
        body, name=name, out_shape=jax.ShapeDtypeStruct((rows, cols), f32),
        compiler_params=_cparams())(parts)


def _pack_rows(arrs, dtype=f32):
    rows, spans, at = [], [], 0
    for a in arrs:
        flat = a.reshape(-1).astype(dtype)
        nr = -(-flat.shape[0] // 1024)
        rows.append(jnp.pad(flat, (0, nr * 1024 - flat.shape[0])).reshape(nr, 1024))
        spans.append((at, nr))
        at += nr
    pad = (-at) % 16
    if pad:
        rows.append(jnp.zeros((pad, 1024), dtype))
    return jnp.concatenate(rows, axis=0), spans


def _unpack_rows(packed, spans, shapes):
    out = []
    for (at, nr), shp in zip(spans, shapes):
        n = math.prod(shp)
        out.append(packed[at:at + nr].reshape(-1)[:n].reshape(shp))
    return out


BIG = ("w_in", "w_out", "w_ffn_in", "w_ffn_out")
SMALL_REPL = ("norm_mix_g", "b_gate", "conv_b", "lru_wa", "lru_wx", "attn_sink", "norm_ffn_g", "norm_final_g")
SMALL_SHARD = ("conv_w", "lru_lambda", "lru_ba", "lru_bx")
ORDER = ("norm_mix_g", "w_in", "b_gate", "conv_w", "conv_b", "lru_lambda", "lru_wa", "lru_ba", "lru_wx",
         "lru_bx", "attn_sink", "w_out", "norm_ffn_g", "w_ffn_in", "w_ffn_out", "norm_final_g")
EARLY_F32 = ("loss", "b_gate", "lru_lambda", "lru_ba", "lru_bx", "attn_sink", "norm_ffn_g", "norm_final_g")
EARLY_BF16 = ("lru_wa", "lru_wx")
LATE = ("norm_mix_g", "conv_w", "conv_b")


def kernel(x, norm_mix_g, w_in, b_gate, conv_w, conv_b, lru_lambda, lru_wa, lru_ba, lru_wx, lru_bx, attn_sink, w_out, norm_ffn_g, w_ffn_in, w_ffn_out, norm_final_g, loss_target, m_norm_mix_g, m_w_in, m_b_gate, m_conv_w, m_conv_b, m_lru_lambda, m_lru_wa, m_lru_ba, m_lru_wx, m_lru_bx, m_attn_sink, m_w_out, m_norm_ffn_g, m_w_ffn_in, m_w_ffn_out, m_norm_final_g, v_norm_mix_g, v_w_in, v_b_gate, v_conv_w, v_conv_b, v_lru_lambda, v_lru_wa, v_lru_ba, v_lru_wx, v_lru_bx, v_attn_sink, v_w_out, v_norm_ffn_g, v_w_ffn_in, v_w_ffn_out, v_norm_final_g):
    w = dict(norm_mix_g=norm_mix_g, w_in=w_in, b_gate=b_gate, conv_w=conv_w, conv_b=conv_b, lru_lambda=lru_lambda,
             lru_wa=lru_wa, lru_ba=lru_ba, lru_wx=lru_wx, lru_bx=lru_bx, attn_sink=attn_sink, w_out=w_out,
             norm_ffn_g=norm_ffn_g, w_ffn_in=w_ffn_in, w_ffn_out=w_ffn_out, norm_final_g=norm_final_g)
    m = dict(norm_mix_g=m_norm_mix_g, w_in=m_w_in, b_gate=m_b_gate, conv_w=m_conv_w, conv_b=m_conv_b,
             lru_lambda=m_lru_lambda, lru_wa=m_lru_wa, lru_ba=m_lru_ba, lru_wx=m_lru_wx, lru_bx=m_lru_bx,
             attn_sink=m_attn_sink, w_out=m_w_out, norm_ffn_g=m_norm_ffn_g, w_ffn_in=m_w_ffn_in,
             w_ffn_out=m_w_ffn_out, norm_final_g=m_norm_final_g)
    v = dict(norm_mix_g=v_norm_mix_g, w_in=v_w_in, b_gate=v_b_gate, conv_w=v_conv_w, conv_b=v_conv_b,
             lru_lambda=v_lru_lambda, lru_wa=v_lru_wa, lru_ba=v_lru_ba, lru_wx=v_lru_wx, lru_bx=v_lru_bx,
             attn_sink=v_attn_sink, w_out=v_w_out, norm_ffn_g=v_norm_ffn_g, w_ffn_in=v_w_ffn_in,
             w_ffn_out=v_w_ffn_out, norm_final_g=v_norm_final_g)
    me = 4 * lax.axis_index("x") + 2 * lax.axis_index("y") + lax.axis_index("c")

    def shard_t(a):
        return jnp.swapaxes(a[0], 0, 1)

    def rows_parts(g):
        return g.reshape(N_DEV, -1, g.shape[1])

    shard_rows = jnp.concatenate([w[n][0] for n in SMALL_SHARD], axis=0)
    small = {n: w[n] for n in ("norm_mix_g", "b_gate", "conv_b", "attn_sink", "norm_ffn_g")}
    small["lru_wa"], small["lru_wx"] = lru_wa[0], lru_wx[0]
    small["norm_final_g"] = norm_final_g.reshape(1, D)
    env, recv = {}, {}

    def before(name):
        if name == "norm_x":
            return [(shard_t(w_in).astype(bf16), False), (shard_rows, False)]
        if name == "inproj":
            return [(w_out[0].astype(bf16), False), (w_ffn_out[0].astype(bf16), False)]
        if name == "attn_fwd":
            return [(shard_t(w_ffn_in).astype(bf16), False)]
        if name == "ffn_in_bwd":
            return [(rows_parts(env["dw_fo"]), True)]
        if name == "attn_bwd":
            return [(rows_parts(env["dw_out"]), True)]
        if name == "lru_bwd":
            return [(rows_parts(env["dw_fi_t"]), True)]
        if name == "conv_bwd":
            ge = env["grads_early"]
            p32, env["early_f32_spans"] = _pack_rows([ge[n] for n in EARLY_F32])
            p16, env["early_bf16_spans"] = _pack_rows([ge[n] for n in EARLY_BF16], bf16)
            return [(p32, False), (p16, False)]
        if name == "inproj_bwd":
            return [(rows_parts(env["dw_in_t"]), True)]
        return []

    def after(name, got):
        if name == "norm_x":
            env["w_in_t"] = got[0].reshape(IN_W, D)
            full_rows = jnp.swapaxes(got[1], 0, 1).reshape(shard_rows.shape[0], -1)
            small["conv_w"], small["lru_lambda"] = full_rows[0:4], full_rows[4:6]
            small["lru_ba"], small["lru_bx"] = full_rows[6:8], full_rows[8:10]
        elif name == "inproj":
            env["w_out"], env["w_fo"] = got[0].reshape(D, D), got[1].reshape(D_FF, D)
        elif name == "attn_fwd":
            env["w_fi_t"] = got[0].reshape(2 * D_FF, D)
        elif name == "ffn_in_bwd":
            recv["w_ffn_out"] = got[0]
        elif name == "attn_bwd":
            recv["w_out"] = got[0]
        elif name == "lru_bwd":
            recv["w_ffn_in"] = got[0]
        elif name == "conv_bwd":
            recv["early_f32"], recv["early_bf16"] = got
        elif name == "inproj_bwd":
            recv["w_in"] = got[0]

    grad_x, grads = _local_step(x[0], loss_target[0], small, env, before, after)

    outs = {}
    for name in ("w_out", "w_ffn_out"):
        outs[name] = _adamw(recv[name], w[name], m[name], v[name], "adamw_" + name)
    for name in ("w_in", "w_ffn_in"):
        t = lambda a: jnp.swapaxes(a, 1, 2)
        outs[name] = [t(r) for r in _adamw(recv[name], t(w[name]), t(m[name]), t(v[name]), "adamw_" + name)]

    small_names = SMALL_REPL + SMALL_SHARD
    late_packed, late_spans = _pack_rows([grads[n] for n in LATE])
    (got_late,) = _exchange([(late_packed, False)], "gather_late_grads")
    summed = {}
    for names, got, spans, tag in ((EARLY_F32, recv["early_f32"], env["early_f32_spans"], "early_f32"),
                                   (EARLY_BF16, recv["early_bf16"], env["early_bf16_spans"], "early_bf16"),
                                   (LATE, got_late, late_spans, "late")):
        total = _sum_parts(got, "sum_small_" + tag)
        summed.update(zip(names, _unpack_rows(total, spans, [grads[n].shape for n in names])))
    loss = summed["loss"].reshape(())
    gsm = {n: summed[n].reshape(w[n].shape) for n in SMALL_REPL}
    for n in SMALL_SHARD:
        full = summed[n]
        gsm[n] = lax.dynamic_slice_in_dim(full, me * 128, 128, axis=1).reshape(w[n].shape)
    pk = lambda dct: _pack_rows([dct[n] for n in small_names])[0]
    gp, sp = _pack_rows([gsm[n] for n in small_names])
    res = _adamw(gp[None], pk(w)[None], pk(m)[None], pk(v)[None], "adamw_small")
    sshapes = [w[n].shape for n in small_names]
    for idx, t in enumerate(res):
        for n, a in zip(small_names, _unpack_rows(t[0], sp, sshapes)):
            outs.setdefault(n, [None] * 4)[idx] = a

    result = [loss, grad_x[None]]
    for idx in range(4):
        result += [outs[n][idx] for n in ORDER]
    return tuple(result)
```

```python
import functools
import math

import jax
import jax.numpy as jnp
from jax import lax
from jax.experimental import pallas as pl
from jax.experimental.pallas import tpu as pltpu

f32 = jnp.float32
bf16 = jnp.bfloat16

D = 1024
D_FF = 2816
IN_W = 5632
N_HEADS = 16
N_KV = 4
HEAD_DIM = 64
WINDOW = 128
BLK = 128
LRU_HEADS = 16
LRU_BLOCK = 64
LRU_GROUPS = 4
LRU_GW = 256
LRU_CHUNK = 128
LRU_ROWS = 2048
RGLRU_C = 8.0
EPS = 1e-6
NEG_INF = -1e30
N_DEV = 8

ADAM_LR = 0.001
ADAM_B1 = 0.9
ADAM_B2 = 0.999
ADAM_EPS = 1e-08
ADAM_WD = 0.01
ADAM_STEP = 10

VMEM_MB = 56

C_U, C_G, C_Q, C_Z0, C_Z1, C_K, C_V = 0, 1024, 2048, 3072, 4096, 5120, 5376


def _cparams(vmem_mb=VMEM_MB):
    return pltpu.CompilerParams(vmem_limit_bytes=vmem_mb << 20)


def _div_tile(n, pref):
    if n <= pref:
        return n
    return max(t for t in range(8, pref + 1, 8) if n % t == 0)


def _sigmoid(x):
    return 0.5 * jnp.tanh(0.5 * x) + 0.5


def _log1p(x):
    u = 1.0 + x
    d = u - 1.0
    return jnp.where(d == 0.0, x, jnp.log(u) * (x / jnp.where(d == 0.0, 1.0, d)))


def _softplus(x):
    return jnp.maximum(x, 0.0) + _log1p(jnp.exp(-jnp.abs(x)))


def _gelu_and_grad(x):
    c = math.sqrt(2.0 / math.pi)
    inner = c * (x + 0.044715 * (x * x * x))
    t = jnp.tanh(inner)
    gelu = 0.5 * x * (1.0 + t)
    dinner = c * (1.0 + 3 * 0.044715 * (x * x))
    dgelu = 0.5 * (1.0 + t) + 0.5 * x * (1.0 - t * t) * dinner
    return gelu, dgelu


def _rms_bwd(dn, xv, g):
    r = lax.rsqrt(jnp.mean(xv * xv, axis=-1, keepdims=True) + EPS)
    xh = xv * r
    dxh = dn * g
    dx = r * (dxh - xh * jnp.mean(dxh * xh, axis=-1, keepdims=True))
    return dx, dn * xh


ANY_SPEC = pl.BlockSpec(memory_space=pl.ANY)


def _comm_out_shape(src, scatter):
    return jax.ShapeDtypeStruct((N_DEV, *(src.shape[1:] if scatter else src.shape)), src.dtype)


def _comm_sems():
    return [pltpu.SemaphoreType.DMA((N_DEV - 1,)), pltpu.SemaphoreType.DMA((N_DEV - 1,)), pltpu.SemaphoreType.DMA]


def _scatter_descs(src_ref, out_ref, send_sems, recv_sems, local_sem):
    x, y, c = lax.axis_index("x"), lax.axis_index("y"), lax.axis_index("c")
    me = 4 * x + 2 * y + c
    descs = [pltpu.make_async_copy(src_ref.at[me], out_ref.at[me], local_sem)]
    for k in range(1, N_DEV):
        px, py, pc = x ^ (k >> 2), y ^ ((k >> 1) & 1), c ^ (k & 1)
        descs.append(pltpu.make_async_remote_copy(
            src_ref=src_ref.at[4 * px + 2 * py + pc], dst_ref=out_ref.at[me],
            send_sem=send_sems.at[k - 1], recv_sem=recv_sems.at[k - 1],
            device_id=(px, py, pc), device_id_type=pl.DeviceIdType.MESH))
    return descs


def _gather_copies(src_ref, out_ref, send_sems, recv_sems, local_sem, which):
    x, y, c = lax.axis_index("x"), lax.axis_index("y"), lax.axis_index("c")
    me, sibling = (x, y, c), (x, y, 1 - c)
    chips = [(1 - x, y), (x, 1 - y), (1 - x, 1 - y)]

    def slot(px, py, pc):
        return out_ref.at[4 * px + 2 * py + pc]

    def copy(k, block, to, src=None):
        return pltpu.make_async_remote_copy(
            src_ref=slot(*block) if src is None else src, dst_ref=slot(*block),
            send_sem=send_sems.at[k], recv_sem=recv_sems.at[k], device_id=to, device_id_type=pl.DeviceIdType.MESH)

    make = {
        "local": lambda: pltpu.make_async_copy(src_ref, slot(*me), local_sem),
        "first": lambda: [copy(0, me, sibling, src=src_ref)] + [copy(1 + j, me, (*chip, c), src=src_ref)
                                                                 for j, chip in enumerate(chips)],
        "passed": lambda: [copy(4 + j, (*chip, c), sibling) for j, chip in enumerate(chips)],
        "landed": lambda: [copy(1 + j, (*chip, c), me) for j, chip in enumerate(chips)],
        "later": lambda: [copy(0, sibling, me)] + [copy(4 + j, (*chip, 1 - c), me) for j, chip in enumerate(chips)],
    }
    return [make[name]() for name in which]


def _comm_start(src_ref, out_ref, sems, scatter):
    if scatter:
        for d in _scatter_descs(src_ref, out_ref, *sems):
            d.start()
    else:
        local, first = _gather_copies(src_ref, out_ref, *sems, which=("local", "first"))
        local.start()
        for cp in first:
            cp.start()


def _comm_pass_on(src_ref, out_ref, sems, scatter):
    if not scatter:
        landed, passed = _gather_copies(src_ref, out_ref, *sems, which=("landed", "passed"))
        for arrived, onward in zip(landed, passed):
            arrived.wait_recv()
            onward.start()


def _comm_finish(src_ref, out_ref, sems, scatter):
    if scatter:
        for d in _scatter_descs(src_ref, out_ref, *sems):
            d.wait()
    else:
        later, first, passed, local = _gather_copies(src_ref, out_ref, *sems,
                                                     which=("later", "first", "passed", "local"))
        for cp in later:
            cp.wait_recv()
        for cp in first + passed:
            cp.wait_send()
        local.wait()


def _exchange(comm, name):
    nc = len(comm)

    def body(*refs):
        srcs, outs, sems = refs[:nc], refs[nc:2 * nc], refs[2 * nc:]
        for stage in (_comm_start, _comm_pass_on, _comm_finish):
            for i in range(nc):
                stage(srcs[i], outs[i], sems[3 * i:3 * i + 3], comm[i][1])

    return pl.pallas_call(
        body, name=name, in_specs=[ANY_SPEC] * nc, out_specs=[ANY_SPEC] * nc,
        out_shape=[_comm_out_shape(*c) for c in comm],
        scratch_shapes=[s for _ in comm for s in _comm_sems()],
    )(*[c[0] for c in comm])


def _hosted_call(body, *, name, grid, in_specs, out_specs, out_shape, args, scratch_shapes=(), comm=()):
    nin, nout, nscr, nc = len(in_specs), len(out_specs), len(scratch_shapes), len(comm)
    steps = math.prod(grid)

    def wrapped(*refs):
        ins = refs[:nin]
        csrc = refs[nin:nin + nc]
        outs = refs[nin + nc:nin + nc + nout]
        cout = refs[nin + nc + nout:nin + 2 * nc + nout]
        scr = refs[nin + 2 * nc + nout:]
        sems = scr[nscr:]

        def at(step, stage):
            lin = 0
            for a in range(len(grid)):
                lin = lin * grid[a] + pl.program_id(a)

            @pl.when(lin == step)
            def _():
                for i in range(nc):
                    stage(csrc[i], cout[i], sems[3 * i:3 * i + 3], comm[i][1])

        if nc:
            at(0, _comm_start)

        body(*ins, *outs, *scr[:nscr])

        if nc:
            at((3 * (steps - 1)) // 4, _comm_pass_on)
            at(steps - 1, _comm_finish)

    res = pl.pallas_call(
        wrapped, name=name, grid=grid,
        in_specs=[*in_specs, *[ANY_SPEC] * nc], out_specs=[*out_specs, *[ANY_SPEC] * nc],
        out_shape=[*out_shape, *[_comm_out_shape(*c) for c in comm]],
        scratch_shapes=[*scratch_shapes, *[s for _ in comm for s in _comm_sems()]],
        compiler_params=_cparams())(*args, *[c[0] for c in comm])
    return res[:nout], res[nout:]


def _rmsnorm_bf16(x, g, name, tm=1024, comm=()):
    S, dm = x.shape
    tm = min(tm, S)

    def body(x_ref, g_ref, xn_ref):
        xv = x_ref[...]
        r = lax.rsqrt(jnp.mean(xv * xv, axis=-1, keepdims=True) + EPS)
        xn_ref[...] = ((xv * r) * g_ref[...]).astype(bf16)

    row = pl.BlockSpec((tm, dm), lambda i: (i, 0))
    return _hosted_call(
        body, name=name, grid=(S // tm,), in_specs=[row, pl.BlockSpec((1, dm), lambda i: (0, 0))],
        out_specs=[row], out_shape=[jax.ShapeDtypeStruct((S, dm), bf16)], args=(x, g), comm=comm)


def _matmul_t(a, wt, name, tm=2048, tn=512, row_block=lambda j: j, comm=()):
    S, dm = a.shape
    n = wt.shape[0]
    tm = min(tm, S)

    def body(a_ref, w_ref, o_ref):
        o_ref[...] = lax.dot_general(a_ref[...], w_ref[...], (((1,), (1,)), ((), ())),
                                     preferred_element_type=f32).astype(bf16)

    return _hosted_call(
        body, name=name, grid=(S // tm, n // tn),
        in_specs=[pl.BlockSpec((tm, dm), lambda i, j: (i, 0)),
                  pl.BlockSpec((tn, dm), lambda i, j: (row_block(j), 0))],
        out_specs=[pl.BlockSpec((tm, tn), lambda i, j: (i, j))],
        out_shape=[jax.ShapeDtypeStruct((S, n), bf16)], args=(a, wt), comm=comm)


def _norm_matmul(x, g, wt, name, tm=1024, tn=1408, row_block=lambda j: j, comm=()):
    S, dm = x.shape
    n = wt.shape[0]
    tm = min(tm, S)

    def body(x_ref, g_ref, w_ref, xn_ref, o_ref):
        @pl.when(pl.program_id(1) == 0)
        def _():
            xv = x_ref[...]
            r = lax.rsqrt(jnp.mean(xv * xv, axis=-1, keepdims=True) + EPS)
            xn_ref[...] = ((xv * r) * g_ref[...]).astype(bf16)

        o_ref[...] = lax.dot_general(xn_ref[...], w_ref[...], (((1,), (1,)), ((), ())),
                                     preferred_element_type=f32).astype(bf16)

    return _hosted_call(
        body, name=name, grid=(S // tm, n // tn),
        in_specs=[pl.BlockSpec((tm, dm), lambda i, j: (i, 0)),
                  pl.BlockSpec((1, dm), lambda i, j: (0, 0)),
                  pl.BlockSpec((tn, dm), lambda i, j: (row_block(j), 0))],
        out_specs=[pl.BlockSpec((tm, dm), lambda i, j: (i, 0)),
                   pl.BlockSpec((tm, tn), lambda i, j: (i, j))],
        out_shape=[jax.ShapeDtypeStruct((S, dm), bf16), jax.ShapeDtypeStruct((S, n), bf16)],
        args=(x, g, wt), comm=comm)


def _mm_tn(a, b, name, tk, tn, tmc=2048, into=None, row=0, out_rows=None):
    m, ka = a.shape
    n = b.shape[1]
    tmc = min(tmc, m)
    nk = m // tmc

    def body(a_ref, b_ref, *rest):
        o_ref, acc_ref = rest[-2:]
        k = pl.program_id(2)
        part = lax.dot_general(a_ref[...], b_ref[...], (((0,), (0,)), ((), ())), preferred_element_type=f32)

        @pl.when(k == 0)
        def _():
            acc_ref[...] = part

        @pl.when(k > 0)
        def _():
            acc_ref[...] += part

        @pl.when(k == nk - 1)
        def _():
            o_ref[...] = acc_ref[...].astype(bf16)

    in_specs = [pl.BlockSpec((tmc, tk), lambda i, j, k: (k, i)), pl.BlockSpec((tmc, tn), lambda i, j, k: (k, j))]
    if into is None:
        return pl.pallas_call(
            body, name=name, grid=(ka // tk, n // tn, nk), in_specs=in_specs,
            out_specs=pl.BlockSpec((tk, tn), lambda i, j, k: (i + row, j)),
            out_shape=jax.ShapeDtypeStruct((out_rows or ka, n), bf16),
            scratch_shapes=[pltpu.VMEM((tk, tn), f32)],
            compiler_params=_cparams())(a, b)
    return pl.pallas_call(
        body, name=name, grid=(ka // tk, n // tn, nk), in_specs=[*in_specs, ANY_SPEC],
        out_specs=pl.BlockSpec((tk, tn), lambda i, j, k: (i + row, j)),
        out_shape=jax.ShapeDtypeStruct(into.shape, into.dtype),
        scratch_shapes=[pltpu.VMEM((tk, tn), f32)], input_output_aliases={2: 0},
        compiler_params=_cparams())(a, b, into)


HALO = 16


def _rows_at(ext, o, tc):
    if o == 0:
        return ext[HALO:HALO + tc]
    return pltpu.roll(ext, (-o) % ext.shape[0], 0)[HALO:HALO + tc]


def _halo_specs(tc, S, width, col):
    per = tc // HALO
    last = S // HALO - 1
    return (pl.BlockSpec((tc, width), lambda i: (i, col)),
            pl.BlockSpec((HALO, width), lambda i: (jnp.maximum(i * per - 1, 0), col)),
            pl.BlockSpec((HALO, width), lambda i: (jnp.minimum((i + 1) * per, last), col)))


def _extended(cur_ref, prev_ref, next_ref, i, nsteps):
    prev = jnp.where(i > 0, prev_ref[...].astype(f32), 0.0)
    nxt = jnp.where(i < nsteps - 1, next_ref[...].astype(f32), 0.0)
    return jnp.concatenate([prev, cur_ref[...].astype(f32), nxt], axis=0)


def _conv_fwd(proj, cw, cb, tc=1024):
    S = proj.shape[0]
    tc = min(tc, S)
    nsteps = S // tc

    def body(cur_ref, prev_ref, next_ref, w_ref, b_ref, o_ref):
        ext = _extended(cur_ref, prev_ref, next_ref, pl.program_id(0), nsteps)
        acc = _rows_at(ext, -2, tc) * w_ref[0:1, :]
        for k in range(1, 4):
            acc = acc + _rows_at(ext, k - 2, tc) * w_ref[k:k + 1, :]
        o_ref[...] = acc + b_ref[...]

    return pl.pallas_call(
        body, name="conv_fwd", grid=(nsteps,),
        in_specs=[*_halo_specs(tc, S, D, 0),
                  pl.BlockSpec((4, D), lambda i: (0, 0)), pl.BlockSpec((1, D), lambda i: (0, 0))],
        out_specs=pl.BlockSpec((tc, D), lambda i: (i, 0)),
        out_shape=jax.ShapeDtypeStruct((S, D), f32),
        compiler_params=_cparams())(proj, proj, proj, cw, cb)


def _conv_bwd(duc_f, duc_b, proj, cw, tc=1024, comm=()):
    S = proj.shape[0]
    tc = min(tc, S)
    nsteps = S // tc

    def body(fc, fp, fn, bc, bp, bn, uc_, up, un, w_ref, du_ref, dw_ref, db_ref):
        i = pl.program_id(0)

        @pl.when(i == 0)
        def _():
            dw_ref[...] = jnp.zeros_like(dw_ref)
            db_ref[...] = jnp.zeros_like(db_ref)

        dext = _extended(fc, fp, fn, i, nsteps) + _extended(bc, bp, bn, i, nsteps)
        uext = _extended(uc_, up, un, i, nsteps)
        d = dext[HALO:HALO + tc]
        acc = _rows_at(dext, 2, tc) * w_ref[0:1, :]
        for k in range(1, 4):
            acc = acc + _rows_at(dext, 2 - k, tc) * w_ref[k:k + 1, :]
        du_ref[...] = acc.astype(bf16)
        wrow = lax.broadcasted_iota(jnp.int32, (4, D), 0)
        for k in range(4):
            dw_ref[...] += jnp.where(wrow == k, jnp.sum(d * _rows_at(uext, k - 2, tc), axis=0, keepdims=True), 0.0)
        db_ref[...] += jnp.sum(d, axis=0, keepdims=True)

    return _hosted_call(
        body, name="conv_bwd", grid=(nsteps,),
        in_specs=[*_halo_specs(tc, S, D, 0), *_halo_specs(tc, S, D, 0), *_halo_specs(tc, S, D, 0),
                  pl.BlockSpec((4, D), lambda i: (0, 0))],
        out_specs=[pl.BlockSpec((tc, D), lambda i: (i, 0)),
                   pl.BlockSpec((4, D), lambda i: (0, 0)), pl.BlockSpec((1, D), lambda i: (0, 0))],
        out_shape=[jax.ShapeDtypeStruct((S, D), bf16), jax.ShapeDtypeStruct((4, D), f32),
                   jax.ShapeDtypeStruct((1, D), f32)],
        args=(duc_f, duc_f, duc_f, duc_b, duc_b, duc_b, proj, proj, proj, cw), comm=comm)


def _scan_scratch():
    halves = [pltpu.VMEM((LRU_CHUNK, 128), f32) for _ in range(2 * (LRU_GW // 128))]
    return [*halves, pltpu.VMEM((LRU_CHUNK // 8, LRU_GW), f32), pltpu.VMEM((LRU_CHUNK // 8, LRU_GW), f32)]


def _log_scan(a, b, row, n, reverse, steps):
    for s in steps:
        shift = a.shape[0] - s if reverse else s
        keep = (row < n - s) if reverse else (row >= s)
        a_sh = pltpu.roll(a, shift, 0)
        b_sh = pltpu.roll(b, shift, 0)
        b = jnp.where(keep, a * b_sh + b, b)
        a = jnp.where(keep, a * a_sh, a)
    return a, b


def _scan_chunk(a, b, carry, reverse, *scratch):
    tc, w = a.shape
    ng = tc // 8
    nl = w // 128
    sa_refs, sb_refs, sc_ref, st_ref = scratch[:nl], scratch[nl:2 * nl], scratch[2 * nl], scratch[2 * nl + 1]
    sub = lax.broadcasted_iota(jnp.int32, (8, w), 0)
    ag, bg = [], []
    for k in range(ng):
        ak, bk = _log_scan(a[8 * k:8 * k + 8], b[8 * k:8 * k + 8], sub, 8, reverse, (1, 2, 4))
        ag.append(ak)
        bg.append(bk)
    a = jnp.concatenate(ag, axis=0)
    b = jnp.concatenate(bg, axis=0)
    edge = 0 if reverse else 7
    for i in range(nl):
        sa_refs[i][...] = a[:, 128 * i:128 * (i + 1)]
        sb_refs[i][...] = b[:, 128 * i:128 * (i + 1)]
    ta = jnp.concatenate([r[pl.ds(edge, ng, stride=8), :] for r in sa_refs], axis=1)
    tb = jnp.concatenate([r[pl.ds(edge, ng, stride=8), :] for r in sb_refs], axis=1)
    grow = lax.broadcasted_iota(jnp.int32, (ng, w), 0)
    ta, tb = _log_scan(ta, tb, grow, ng, reverse, [1 << i for i in range(ng.bit_length() - 1)])
    state = tb + ta * carry
    st_ref[...] = state
    if reverse:
        sc_ref[...] = jnp.where(grow == ng - 1, carry, pltpu.roll(state, ng - 1, 0))
    else:
        sc_ref[...] = jnp.where(grow == 0, carry, pltpu.roll(state, 1, 0))
    h = jnp.concatenate([bg[k] + ag[k] * sc_ref[k:k + 1, :] for k in range(ng)], axis=0)
    return h, (st_ref[0:1, :] if reverse else st_ref[ng - 1:ng, :])


def _lru_gates(uc, w, p_ref):
    pre = jnp.dot(uc.astype(bf16), w, preferred_element_type=f32)
    r = _sigmoid(pre[:, :LRU_GW] + p_ref[0, 1:2, :])
    gi = _sigmoid(pre[:, LRU_GW:] + p_ref[0, 2:3, :])
    sp = _softplus(-p_ref[0, 0:1, :])
    log_a = -RGLRU_C * r * sp
    a = jnp.exp(log_a)
    x = 2.0 * log_a
    series = -x * (1.0 + x * (0.5 + x * (1.0 / 6 + x * (1.0 / 24))))
    beta = jnp.sqrt(jnp.maximum(jnp.where(x > -0.0625, series, 1.0 - a * a), 0.0))
    return r, gi, sp, a, beta


def _lru_fwd(uc, wg, lp, reverse, comm=()):
    S = uc.shape[0]
    tc = LRU_CHUNK
    rows = min(LRU_ROWS, S)
    nsub = rows // tc
    nblk = S // rows
    d = 1 if reverse else 0

    def bidx(c):
        return nblk - 1 - c if reverse else c

    def body(uc_ref, w_ref, p_ref, h_ref, carry_ref, *scan_scratch):
        @pl.when(pl.program_id(1) == 0)
        def _():
            carry_ref[...] = jnp.zeros_like(carry_ref)

        carry = carry_ref[...]
        for j in (reversed(range(nsub)) if reverse else range(nsub)):
            sl = slice(j * tc, (j + 1) * tc)
            ucv = uc_ref[sl, :]
            _, gi, _, a, beta = _lru_gates(ucv, w_ref[0], p_ref)
            h, carry = _scan_chunk(a, beta * (gi * ucv), carry, reverse, *scan_scratch)
            h_ref[sl, :] = h.astype(bf16)
        carry_ref[...] = carry

    return _hosted_call(
        body, name="lru_fwd_rev" if reverse else "lru_fwd", grid=(LRU_GROUPS, nblk),
        in_specs=[pl.BlockSpec((rows, LRU_GW), lambda g, c: (bidx(c), g)),
                  pl.BlockSpec((1, LRU_GW, 2 * LRU_GW), lambda g, c: (g, 0, d)),
                  pl.BlockSpec((1, 8, LRU_GW), lambda g, c: (d, 0, g))],
        out_specs=[pl.BlockSpec((rows, LRU_GW), lambda g, c: (bidx(c), g))],
        out_shape=[jax.ShapeDtypeStruct((S, D), bf16)],
        scratch_shapes=[pltpu.VMEM((1, LRU_GW), f32), *_scan_scratch()],
        args=(uc, wg, lp), comm=comm)


def _lru_bwd(uc, dh, h, wg, lp, reverse, comm=()):
    S = uc.shape[0]
    tc = LRU_CHUNK
    rows = min(LRU_ROWS, S)
    nsub = rows // tc
    nblk = S // rows
    d = 1 if reverse else 0
    per = rows // HALO
    last8 = S // HALO - 1

    def bidx(c):
        return c if reverse else nblk - 1 - c

    def halo_idx(c):
        if reverse:
            return jnp.minimum((bidx(c) + 1) * per, last8)
        return jnp.maximum(bidx(c) * per - 1, 0)

    def body(uc_ref, dh_ref, h_ref, halo_ref, w_ref, p_ref, duc_ref, dw_ref, dp_ref, carry_ref, tmp_ref,
             *scan_scratch):
        c = pl.program_id(1)
        bi = bidx(c)

        @pl.when(c == 0)
        def _():
            carry_ref[...] = jnp.zeros_like(carry_ref)
            dw_ref[...] = jnp.zeros_like(dw_ref)
            dp_ref[...] = jnp.zeros_like(dp_ref)

        row = lax.broadcasted_iota(jnp.int32, (tc, LRU_GW), 0)
        carry = carry_ref[...]
        dw = jnp.zeros((LRU_GW, 2 * LRU_GW), f32)
        dsp = jnp.zeros((1, LRU_GW), f32)
        dba = jnp.zeros((1, LRU_GW), f32)
        dbx = jnp.zeros((1, LRU_GW), f32)
        for j in (range(nsub) if reverse else reversed(range(nsub))):
            sl = slice(j * tc, (j + 1) * tc)
            ucv = uc_ref[sl, :]
            ucb = ucv.astype(bf16)
            r, gi, sp, a, beta = _lru_gates(ucv, w_ref[0], p_ref)
            hv = h_ref[sl, :].astype(f32)
            dhv = dh_ref[sl, :].astype(f32)
            if reverse:
                alpha = jnp.where(row == 0, 1.0, pltpu.roll(a, 1, 0))
                gsc, _ = _scan_chunk(alpha, dhv, carry, False, *scan_scratch)
                if j < nsub - 1:
                    edge = h_ref[(j + 1) * tc:(j + 1) * tc + HALO, :].astype(f32)[0:1, :]
                else:
                    edge = jnp.where(bi < nblk - 1, halo_ref[...].astype(f32)[0:1, :], 0.0)
                h_nb = jnp.where(row == tc - 1, edge, pltpu.roll(hv, tc - 1, 0))
            else:
                alpha = jnp.where(row == tc - 1, 1.0, pltpu.roll(a, tc - 1, 0))
                gsc, _ = _scan_chunk(alpha, dhv, carry, True, *scan_scratch)
                if j > 0:
                    edge = h_ref[j * tc - HALO:j * tc, :].astype(f32)[HALO - 1:HALO, :]
                else:
                    edge = jnp.where(bi > 0, halo_ref[...].astype(f32)[HALO - 1:HALO, :], 0.0)
                h_nb = jnp.where(row == 0, edge, pltpu.roll(hv, 1, 0))
            tmp_ref[...] = a * gsc
            carry = tmp_ref[tc - 1:tc, :] if reverse else tmp_ref[0:1, :]

            da = gsc * h_nb
            dbeta = gsc * (gi * ucv)
            dl = da * a - dbeta * (a * a) / beta
            dr = dl * (-RGLRU_C * sp)
            dsp = dsp + jnp.sum(dl * (-RGLRU_C * r), axis=0, keepdims=True)
            dgi = gsc * beta * ucv
            dpre_r = dr * r * (1.0 - r)
            dpre_i = dgi * gi * (1.0 - gi)
            dba = dba + jnp.sum(dpre_r, axis=0, keepdims=True)
            dbx = dbx + jnp.sum(dpre_i, axis=0, keepdims=True)
            dpre = jnp.concatenate([dpre_r, dpre_i], axis=1).astype(bf16)
            back = lax.dot_general(dpre, w_ref[0], (((1,), (1,)), ((), ())), preferred_element_type=f32)
            duc_ref[sl, :] = (gsc * beta * gi + back).astype(bf16)
            dw = dw + lax.dot_general(ucb, dpre, (((0,), (0,)), ((), ())), preferred_element_type=f32)
        carry_ref[...] = carry
        dw_ref[0] += dw
        dlam = -dsp / (1.0 + jnp.exp(p_ref[0, 0:1, :]))
        prow = lax.broadcasted_iota(jnp.int32, (8, LRU_GW), 0)
        dp_ref[...] += (jnp.where(prow == 0, dlam, 0.0) + jnp.where(prow == 1, dba, 0.0)
                        + jnp.where(prow == 2, dbx, 0.0))

    chunk = pl.BlockSpec((rows, LRU_GW), lambda g, c: (bidx(c), g))
    return _hosted_call(
        body, name="lru_bwd_rev" if reverse else "lru_bwd", grid=(LRU_GROUPS, nblk),
        in_specs=[chunk, chunk, chunk,
                  pl.BlockSpec((HALO, LRU_GW), lambda g, c: (halo_idx(c), g)),
                  pl.BlockSpec((1, LRU_GW, 2 * LRU_GW), lambda g, c: (g, 0, d)),
                  pl.BlockSpec((1, 8, LRU_GW), lambda g, c: (d, 0, g))],
        out_specs=[chunk,
                   pl.BlockSpec((1, LRU_GW, 2 * LRU_GW), lambda g, c: (g, 0, 0)),
                   pl.BlockSpec((8, LRU_GW), lambda g, c: (0, g))],
        out_shape=[jax.ShapeDtypeStruct((S, D), bf16),
                   jax.ShapeDtypeStruct((LRU_GROUPS, LRU_GW, 2 * LRU_GW), f32),
                   jax.ShapeDtypeStruct((8, D), f32)],
        scratch_shapes=[pltpu.VMEM((1, LRU_GW), f32), pltpu.VMEM((tc, LRU_GW), f32), *_scan_scratch()],
        args=(uc, dh, h, h, wg, lp), comm=comm)


def _slope(h):
    return 2.0 ** (-8.0 * (h + 1.0) / N_HEADS)


ATT_QB = 4


def _kv_specs(nb, col):
    return [pl.BlockSpec((BLK, N_KV * HEAD_DIM), lambda n: (jnp.maximum(ATT_QB * n - 1, 0), col)),
            pl.BlockSpec((ATT_QB * BLK, N_KV * HEAD_DIM), lambda n: (n, col)),
            pl.BlockSpec((BLK, N_KV * HEAD_DIM), lambda n: (jnp.minimum(ATT_QB * (n + 1), nb - 1), col))]


def _key_blocks(prev_ref, cur_ref, next_ref):
    return [prev_ref[...], *[cur_ref[BLK * s:BLK * (s + 1), :] for s in range(ATT_QB)], next_ref[...]]


def _dup_windows(r0, r1, r2):
    left = lax.broadcasted_iota(jnp.int32, (3 * BLK, 128), 1) < HEAD_DIM
    win = jnp.concatenate([r0, r1, r2], axis=0)
    out = []
    for i in range(N_KV // 2):
        t = win[:, i * 128:(i + 1) * 128]
        r = pltpu.roll(t, HEAD_DIM, 1)
        out += [jnp.where(left, t, r).astype(bf16), jnp.where(left, r, t).astype(bf16)]
    return out


def _attn_bias_init(bias_ref):
    k_loc = lax.broadcasted_iota(jnp.int32, (3 * BLK, BLK), 0)
    q_loc = lax.broadcasted_iota(jnp.int32, (3 * BLK, BLK), 1)
    adist = jnp.abs(q_loc + BLK - k_loc)
    adf = adist.astype(f32)
    for e in range(3):
        ok = adist <= WINDOW
        if e == 0:
            ok = ok & (k_loc >= BLK)
        if e == 2:
            ok = ok & (k_loc < 2 * BLK)
        for kv in range(N_KV):
            bias_ref[e, kv] = jnp.concatenate(
                [jnp.where(ok, (-_slope(4 * kv + j)) * adf, NEG_INF) for j in range(4)], axis=1)


def _stack_heads(ref, sub, kv, scale):
    left = lax.broadcasted_iota(jnp.int32, (BLK, 128), 1) < HEAD_DIM
    rows = []
    for pp in range(2):
        t = ref[BLK * sub:BLK * (sub + 1), (2 * kv + pp) * 128:(2 * kv + pp + 1) * 128]
        if scale != 1.0:
            t = t * scale
        zero = jnp.zeros_like(t)
        rows += [jnp.where(left, t, zero).astype(bf16), jnp.where(left, zero, t).astype(bf16)]
    return jnp.concatenate(rows, axis=0)


def _attn_softmax(qs, k2, bias, sink_ref, kv, stats=None):
    sink = jnp.concatenate([jnp.full((1, BLK), sink_ref[0, 4 * kv + j], f32) for j in range(4)], axis=1)
    s = lax.dot_general(k2, qs, (((1,), (1,)), ((), ())), preferred_element_type=f32) + bias
    m = jnp.maximum(jnp.max(s, axis=0, keepdims=True), sink) if stats is None else stats[0]
    p = jnp.exp(s - m)
    ps = jnp.exp(sink - m)
    inv = 1.0 / (jnp.sum(p, axis=0, keepdims=True) + ps) if stats is None else stats[1]
    return p, ps, m, inv


def _pair_tiles(t):
    return [jnp.concatenate([t[:HEAD_DIM, 256 * pp:256 * pp + 128],
                             t[HEAD_DIM:, 256 * pp + 128:256 * pp + 256]], axis=0).T for pp in range(2)]


def _attn_fwd(proj, sink, comm=()):
    S = proj.shape[0]
    nb = S // BLK
    assert nb >= 2 and nb % ATT_QB == 0

    def body(q_ref, k0, k1, k2_, v0, v1, v2_, sink_ref, o_ref, st_ref, bias_ref):
        n = pl.program_id(0)

        @pl.when(n == 0)
        def _():
            _attn_bias_init(bias_ref)

        kb = _key_blocks(k0, k1, k2_)
        vb = _key_blocks(v0, v1, v2_)
        for sub in range(ATT_QB):
            blk = ATT_QB * n + sub
            e = jnp.where(blk == 0, 0, jnp.where(blk == nb - 1, 2, 1))
            kk = _dup_windows(*kb[sub:sub + 3])
            vv = _dup_windows(*vb[sub:sub + 3])
            tiles = []
            for kv in range(N_KV):
                qs = _stack_heads(q_ref, sub, kv, HEAD_DIM ** -0.5)
                p, _, m, inv = _attn_softmax(qs, kk[kv], bias_ref[e, kv], sink_ref, kv)
                st_ref[sub, kv:kv + 1, :] = m
                st_ref[sub, N_KV + kv:N_KV + kv + 1, :] = inv
                ot = lax.dot_general(vv[kv], p.astype(bf16), (((0,), (0,)), ((), ())), preferred_element_type=f32)
                tiles += _pair_tiles(ot * inv)
            o_ref[BLK * sub:BLK * (sub + 1), :] = jnp.concatenate(tiles, axis=1).astype(bf16)

    return _hosted_call(
        body, name="attn_fwd", grid=(nb // ATT_QB,),
        in_specs=[pl.BlockSpec((ATT_QB * BLK, D), lambda n: (n, C_Q // D)),
                  *_kv_specs(nb, C_K // (N_KV * HEAD_DIM)), *_kv_specs(nb, C_V // (N_KV * HEAD_DIM)),
                  pl.BlockSpec(memory_space=pltpu.SMEM)],
        out_specs=[pl.BlockSpec((ATT_QB * BLK, D), lambda n: (n, 0)),
                   pl.BlockSpec((ATT_QB, 2 * N_KV, 4 * BLK), lambda n: (n, 0, 0))],
        out_shape=[jax.ShapeDtypeStruct((S, D), bf16), jax.ShapeDtypeStruct((nb, 2 * N_KV, 4 * BLK), f32)],
        scratch_shapes=[pltpu.VMEM((3, N_KV, 3 * BLK, 4 * BLK), f32)],
        args=(proj, proj, proj, proj, proj, proj, proj, sink), comm=comm)


def _attn_bwd(proj, sink, dyb, stats, comm=()):
    S = proj.shape[0]
    nb = S // BLK
    assert nb >= 2 and nb % ATT_QB == 0
    nsteps = nb // ATT_QB

    def body(q_ref, k0, k1, k2_, v0, v1, v2_, sink_ref, do_ref, st_ref, dq_ref, dk_out, dv_out, ds_ref,
             bias_ref, dk_ref, dv_ref, dsk_ref):
        n = pl.program_id(0)

        @pl.when(n == 0)
        def _():
            _attn_bias_init(bias_ref)
            dk_ref[...] = jnp.zeros_like(dk_ref)
            dv_ref[...] = jnp.zeros_like(dv_ref)
            dsk_ref[...] = jnp.zeros_like(dsk_ref)

        kb = _key_blocks(k0, k1, k2_)
        vb = _key_blocks(v0, v1, v2_)
        left3 = lax.broadcasted_iota(jnp.int32, (3 * BLK, 128), 1) < HEAD_DIM
        for sub in range(ATT_QB):
            blk = ATT_QB * n + sub
            e = jnp.where(blk == 0, 0, jnp.where(blk == nb - 1, 2, 1))
            kk = _dup_windows(*kb[sub:sub + 3])
            vv = _dup_windows(*vb[sub:sub + 3])
            start = pl.multiple_of(blk * BLK, BLK)
            dq_tiles, dks, dvs = [], [], []
            for kv in range(N_KV):
                qs = _stack_heads(q_ref, sub, kv, HEAD_DIM ** -0.5)
                dos = _stack_heads(do_ref, sub, kv, 1.0)
                stats = (st_ref[sub, kv:kv + 1, :], st_ref[sub, N_KV + kv:N_KV + kv + 1, :])
                p, ps, _, inv = _attn_softmax(qs, kk[kv], bias_ref[e, kv], sink_ref, kv, stats)
                pn = p * inv
                dp = lax.dot_general(vv[kv], dos, (((1,), (1,)), ((), ())), preferred_element_type=f32)
                delta = jnp.sum(pn * dp, axis=0, keepdims=True)
                dsc = (pn * (dp - delta)).astype(bf16)
                dsk_ref[kv:kv + 1, :] += delta * (ps * inv)
                dqt = lax.dot_general(kk[kv], dsc, (((0,), (0,)), ((), ())), preferred_element_type=f32)
                dq_tiles += _pair_tiles(dqt * (HEAD_DIM ** -0.5))
                dk = jnp.dot(dsc, qs, preferred_element_type=f32)
                dv = jnp.dot(pn.astype(bf16), dos, preferred_element_type=f32)
                dks.append(dk + pltpu.roll(dk, HEAD_DIM, 1))
                dvs.append(dv + pltpu.roll(dv, HEAD_DIM, 1))
            for jp in range(N_KV // 2):
                cols = slice(jp * 128, (jp + 1) * 128)
                dk_ref[pl.ds(start, 3 * BLK), cols] += jnp.where(left3, dks[2 * jp], dks[2 * jp + 1])
                dv_ref[pl.ds(start, 3 * BLK), cols] += jnp.where(left3, dvs[2 * jp], dvs[2 * jp + 1])
            dq_ref[BLK * sub:BLK * (sub + 1), :] = jnp.concatenate(dq_tiles, axis=1).astype(bf16)

        @pl.when(n == nsteps - 1)
        def _():
            pltpu.sync_copy(dk_ref, dk_out)
            pltpu.sync_copy(dv_ref, dv_out)
            lane = lax.broadcasted_iota(jnp.int32, (1, 128), 1)
            dsink = jnp.zeros((1, 128), f32)
            for h in range(N_HEADS):
                part = dsk_ref[h // 4:h // 4 + 1, (h % 4) * BLK:(h % 4 + 1) * BLK]
                dsink = dsink + jnp.where(lane == h, -jnp.sum(part), 0.0)
            ds_ref[...] = dsink

    acc = jax.ShapeDtypeStruct((S + 2 * BLK, N_KV * HEAD_DIM), f32)
    return _hosted_call(
        body, name="attn_bwd", grid=(nsteps,),
        in_specs=[pl.BlockSpec((ATT_QB * BLK, D), lambda n: (n, C_Q // D)),
                  *_kv_specs(nb, C_K // (N_KV * HEAD_DIM)), *_kv_specs(nb, C_V // (N_KV * HEAD_DIM)),
                  pl.BlockSpec(memory_space=pltpu.SMEM),
                  pl.BlockSpec((ATT_QB * BLK, D), lambda n: (n, 0)),
                  pl.BlockSpec((ATT_QB, 2 * N_KV, 4 * BLK), lambda n: (n, 0, 0))],
        out_specs=[pl.BlockSpec((ATT_QB * BLK, D), lambda n: (n, 0)), ANY_SPEC, ANY_SPEC,
                   pl.BlockSpec((1, 128), lambda n: (0, 0))],
        out_shape=[jax.ShapeDtypeStruct((S, D), bf16), acc, acc, jax.ShapeDtypeStruct((1, 128), f32)],
        scratch_shapes=[pltpu.VMEM((3, N_KV, 3 * BLK, 4 * BLK), f32), pltpu.VMEM(acc.shape, f32),
                        pltpu.VMEM(acc.shape, f32), pltpu.VMEM((8, 4 * BLK), f32)],
        args=(proj, proj, proj, proj, proj, proj, proj, sink, dyb, stats), comm=comm)


def _merge_parts(hf, hb, g, z0, z1, yb, bg):
    g0 = _sigmoid(z0 + bg[:, :D].astype(bf16))
    g1 = _sigmoid(z1 + bg[:, D:].astype(bf16))
    gelu, dgelu = _gelu_and_grad(g)
    hs = hf + hb
    ya = hs * gelu
    return g0, g1, gelu, dgelu, hs, ya


def _merge_outproj(x, hf, hb, proj, yb, bg, w_out, tm=1024):
    S = x.shape[0]
    tm = min(tm, S)

    def body(x_ref, hf_ref, hb_ref, g_ref, z0_ref, z1_ref, yb_ref, bg_ref, w_ref, mg_ref, x1_ref):
        ybv = yb_ref[...]
        g0, g1, _, _, _, ya = _merge_parts(hf_ref[...], hb_ref[...], g_ref[...], z0_ref[...], z1_ref[...],
                                           ybv, bg_ref[...])
        mg = g0 * ya + g1 * ybv
        mg_ref[...] = mg
        x1_ref[...] = x_ref[...] + jnp.dot(mg, w_ref[...], preferred_element_type=f32)

    row = pl.BlockSpec((tm, D), lambda i: (i, 0))
    return pl.pallas_call(
        body, name="merge_outproj", grid=(S // tm,),
        in_specs=[row, row, row,
                  pl.BlockSpec((tm, D), lambda i: (i, C_G // D)),
                  pl.BlockSpec((tm, D), lambda i: (i, C_Z0 // D)),
                  pl.BlockSpec((tm, D), lambda i: (i, C_Z1 // D)),
                  row, pl.BlockSpec((1, 2 * D), lambda i: (0, 0)), pl.BlockSpec((D, D), lambda i: (0, 0))],
        out_specs=[row, row],
        out_shape=[jax.ShapeDtypeStruct((S, D), bf16), jax.ShapeDtypeStruct((S, D), f32)],
        compiler_params=_cparams())(x, hf, hb, proj, proj, proj, yb, bg, w_out)


def _ffn_out_loss(gu, x1, w_fo, g3, tgt, tm=256):
    S = x1.shape[0]
    tm = min(tm, S)

    def body(gt_ref, up_ref, x1_ref, w_ref, g_ref, t_ref, ff_ref, dx_ref, dxb_ref, loss_ref, dg_ref,
             dgt_ref, dup_ref):
        @pl.when(pl.program_id(0) == 0)
        def _():
            loss_ref[...] = jnp.zeros_like(loss_ref)
            dg_ref[...] = jnp.zeros_like(dg_ref)

        gt = gt_ref[...]
        up = up_ref[...]
        sg = _sigmoid(gt)
        silu = gt * sg
        ff = silu * up
        ff_ref[...] = ff
        x2 = x1_ref[...] + jnp.dot(ff, w_ref[...], preferred_element_type=f32)
        gv = g_ref[...]
        r = lax.rsqrt(jnp.mean(x2 * x2, axis=-1, keepdims=True) + EPS)
        xh = x2 * r
        diff = xh * gv - t_ref[...]
        loss_ref[...] += (0.5 / D) * jnp.sum(diff * diff)
        dy = diff * (1.0 / D)
        dg_ref[...] += jnp.sum(dy * xh, axis=0, keepdims=True)
        dxh = dy * gv
        dx = r * (dxh - xh * jnp.mean(dxh * xh, axis=-1, keepdims=True))
        dx_ref[...] = dx
        dxb = dx.astype(bf16)
        dxb_ref[...] = dxb
        dff = lax.dot_general(dxb, w_ref[...], (((1,), (1,)), ((), ())), preferred_element_type=f32)
        dup_ref[...] = (dff * silu.astype(f32)).astype(bf16)
        dgt_ref[...] = (dff * (up * (sg * (1.0 + gt * (1.0 - sg)))).astype(f32)).astype(bf16)

    row = pl.BlockSpec((tm, D), lambda i: (i, 0))
    vec = pl.BlockSpec((1, D), lambda i: (0, 0))
    wide = pl.BlockSpec((tm, D_FF), lambda i: (i, 0))
    wide_shape = jax.ShapeDtypeStruct((S, D_FF), bf16)
    return pl.pallas_call(
        body, name="ffn_out_loss", grid=(S // tm,),
        in_specs=[wide, pl.BlockSpec((tm, D_FF), lambda i: (i, 1)),
                  row, pl.BlockSpec((D_FF, D), lambda i: (0, 0)), vec, row],
        out_specs=[wide, row, row, pl.BlockSpec((1, 128), lambda i: (0, 0)), vec, wide, wide],
        out_shape=[wide_shape, jax.ShapeDtypeStruct((S, D), f32), jax.ShapeDtypeStruct((S, D), bf16),
                   jax.ShapeDtypeStruct((1, 128), f32), jax.ShapeDtypeStruct((1, D), f32), wide_shape, wide_shape],
        compiler_params=_cparams())(gu, gu, x1, w_fo, g3, tgt)


def _proj_bwd(pieces, wt, xres, g, dres, name, tm=512, comm=()):
    S = xres.shape[0]
    tm = min(tm, S)
    np_ = len(pieces)

    def body(*refs):
        p_refs = refs[:np_]
        w_refs = refs[np_:2 * np_]
        x_ref, g_ref, dres_ref, dx_ref, dxb_ref, dg_ref = refs[2 * np_:]

        @pl.when(pl.program_id(0) == 0)
        def _():
            dg_ref[...] = jnp.zeros_like(dg_ref)

        dn = jnp.dot(p_refs[0][...], w_refs[0][...], preferred_element_type=f32)
        for pr, wr in zip(p_refs[1:], w_refs[1:]):
            dn = dn + jnp.dot(pr[...], wr[...], preferred_element_type=f32)
        dxn, dgc = _rms_bwd(dn, x_ref[...], g_ref[...])
        dx = dres_ref[...] + dxn
        dx_ref[...] = dx
        dxb_ref[...] = dx.astype(bf16)
        dg_ref[...] += jnp.sum(dgc, axis=0, keepdims=True)

    row = pl.BlockSpec((tm, D), lambda i: (i, 0))
    vec = pl.BlockSpec((1, D), lambda i: (0, 0))
    return _hosted_call(
        body, name=name, grid=(S // tm,),
        in_specs=[*[pl.BlockSpec((tm, wd), functools.partial(lambda i, cb: (i, cb), cb=acb))
                    for _, acb, _, wd in pieces],
                  *[pl.BlockSpec((wd, D), functools.partial(lambda i, rb: (rb, 0), rb=wrb))
                    for _, _, wrb, wd in pieces],
                  row, vec, row],
        out_specs=[row, row, vec],
        out_shape=[jax.ShapeDtypeStruct((S, D), f32), jax.ShapeDtypeStruct((S, D), bf16),
                   jax.ShapeDtypeStruct((1, D), f32)],
        args=(*[p[0] for p in pieces], *[wt] * np_, xres, g, dres), comm=comm)


def _outproj_bwd(dx1b, w_out, hf, hb, proj, yb, bg, tm=1024):
    S = dx1b.shape[0]
    tm = min(tm, S)

    def body(dx_ref, w_ref, hf_ref, hb_ref, g_ref, z0_ref, z1_ref, yb_ref, bg_ref,
             dh_ref, dg_ref, dz_ref, dyb_ref, dbg_ref):
        @pl.when(pl.program_id(0) == 0)
        def _():
            dbg_ref[...] = jnp.zeros_like(dbg_ref)

        dm = lax.dot_general(dx_ref[...], w_ref[...], (((1,), (1,)), ((), ())), preferred_element_type=f32)
        ybv = yb_ref[...]
        g0, g1, gelu, dgelu, hs, ya = _merge_parts(hf_ref[...], hb_ref[...], g_ref[...], z0_ref[...],
                                                   z1_ref[...], ybv, bg_ref[...])
        dh_ref[...] = (dm * (g0 * gelu).astype(f32)).astype(bf16)
        dg_ref[...] = (dm * (g0 * hs * dgelu).astype(f32)).astype(bf16)
        dyb_ref[...] = (dm * g1.astype(f32)).astype(bf16)
        dz0 = dm * (ya * (g0 * (1.0 - g0))).astype(f32)
        dz1 = dm * (ybv * (g1 * (1.0 - g1))).astype(f32)
        dz = jnp.concatenate([dz0, dz1], axis=1)
        dz_ref[...] = dz.astype(bf16)
        dbg_ref[...] += jnp.sum(dz, axis=0, keepdims=True)

    row = pl.BlockSpec((tm, D), lambda i: (i, 0))
    return pl.pallas_call(
        body, name="outproj_bwd", grid=(S // tm,),
        in_specs=[row, pl.BlockSpec((D, D), lambda i: (0, 0)), row, row,
                  pl.BlockSpec((tm, D), lambda i: (i, C_G // D)),
                  pl.BlockSpec((tm, D), lambda i: (i, C_Z0 // D)),
                  pl.BlockSpec((tm, D), lambda i: (i, C_Z1 // D)),
                  row, pl.BlockSpec((1, 2 * D), lambda i: (0, 0))],
        out_specs=[row, row, pl.BlockSpec((tm, 2 * D), lambda i: (i, 0)), row,
                   pl.BlockSpec((1, 2 * D), lambda i: (0, 0))],
        out_shape=[jax.ShapeDtypeStruct((S, D), bf16), jax.ShapeDtypeStruct((S, D), bf16),
                   jax.ShapeDtypeStruct((S, 2 * D), bf16), jax.ShapeDtypeStruct((S, D), bf16),
                   jax.ShapeDtypeStruct((1, 2 * D), f32)],
        compiler_params=_cparams())(dx1b, w_out, hf, hb, proj, proj, proj, yb, bg)


def _block_diag_groups(w):
    w4 = w.reshape(LRU_GROUPS, 4, LRU_BLOCK, LRU_BLOCK)
    eye = jnp.eye(4, dtype=w.dtype)
    return jnp.einsum("ghij,hk->ghikj", w4, eye).reshape(LRU_GROUPS, LRU_GW, LRU_GW)


def _diag_blocks(dw):
    d5 = dw.reshape(LRU_GROUPS, 4, LRU_BLOCK, 4, LRU_BLOCK)
    return jnp.stack([d5[:, h, :, h, :] for h in range(4)], axis=1).reshape(LRU_HEADS, LRU_BLOCK, LRU_BLOCK)


def _local_step(x, tgt, small, env, before=lambda name: (), after=lambda name, got: None):
    S = x.shape[0]
    g1, g2, g3 = small["norm_mix_g"], small["norm_ffn_g"], small["norm_final_g"]
    bg, cb, sink = small["b_gate"], small["conv_b"], small["attn_sink"]

    def hosted(name, fn, *args, **kw):
        outs, got = fn(*args, comm=tuple(before(name)), **kw)
        after(name, got)
        return outs

    (xn,) = hosted("norm_x", _rmsnorm_bf16, x, g1, "norm_x")
    cw = small["conv_w"]
    wg = jnp.concatenate([_block_diag_groups(small["lru_wa"][0]), _block_diag_groups(small["lru_wx"][0]),
                          _block_diag_groups(small["lru_wa"][1]), _block_diag_groups(small["lru_wx"][1])],
                         axis=2).astype(bf16)
    zeros5 = jnp.zeros((5, D), f32)
    lp = jnp.stack([jnp.concatenate([small["lru_lambda"][d:d + 1], small["lru_ba"][d:d + 1],
                                     small["lru_bx"][d:d + 1], zeros5], axis=0) for d in range(2)])
    (proj,) = hosted("inproj", _matmul_t, xn, env["w_in_t"], "inproj", tm=4096, tn=512,
                     row_block=lambda j: jnp.where(j < 6, j, jnp.where(j < 10, j + 1, 6)))
    uc = _conv_fwd(proj, cw, cb)
    (hf,), _ = _lru_fwd(uc, wg, lp, False)
    (hb,), _ = _lru_fwd(uc, wg, lp, True)
    yb, attn_stats = hosted("attn_fwd", _attn_fwd, proj, sink)
    merged, x1 = _merge_outproj(x, hf, hb, proj, yb, bg, env["w_out"])
    (xn2, gu), _ = _norm_matmul(x1, g2, env["w_fi_t"], "norm_ffn_in", tn=D_FF)
    ff, dx2, dx2b, loss, dg3, dgt, dup = _ffn_out_loss(gu, x1, env["w_fo"], g3, tgt)

    env["dw_fo"] = _mm_tn(ff, dx2b, "dw_ffn_out", tk=1408, tn=1024)
    dx1, dx1b, dg2 = hosted("ffn_in_bwd", _proj_bwd, [(dgt, 0, 0, D_FF), (dup, 0, 1, D_FF)], env["w_fi_t"],
                            x1, g2, dx2, "ffn_in_bwd")
    dw_gate = _mm_tn(dgt, xn2, "dw_ffn_in_gate", tk=1408, tn=1024, out_rows=2 * D_FF)
    env["dw_fi_t"] = _mm_tn(dup, xn2, "dw_ffn_in_up", tk=1408, tn=1024, into=dw_gate, row=D_FF // 1408)
    env["dw_out"] = _mm_tn(merged, dx1b, "dw_out", tk=1024, tn=1024)
    dh, dgl, dz, dyb, dbg = _outproj_bwd(dx1b, env["w_out"], hf, hb, proj, yb, bg)
    dq, dk2, dv2, dsink = hosted("attn_bwd", _attn_bwd, proj, sink, dyb, attn_stats)
    dkv = jnp.concatenate([dk2[BLK:BLK + S], dv2[BLK:BLK + S]], axis=1).astype(bf16)
    duc_f, dwg_f, dp_f = hosted("lru_bwd", _lru_bwd, uc, dh, hf, wg, lp, False)
    (duc_b, dwg_b, dp_b), _ = _lru_bwd(uc, dh, hb, wg, lp, True)
    env["grads_early"] = {
        "loss": loss[:, :1], "b_gate": dbg,
        "lru_lambda": jnp.concatenate([dp_f[0:1], dp_b[0:1]], axis=0),
        "lru_wa": jnp.stack([_diag_blocks(dwg_f[:, :, :LRU_GW]), _diag_blocks(dwg_b[:, :, :LRU_GW])]),
        "lru_ba": jnp.concatenate([dp_f[1:2], dp_b[1:2]], axis=0),
        "lru_wx": jnp.stack([_diag_blocks(dwg_f[:, :, LRU_GW:]), _diag_blocks(dwg_b[:, :, LRU_GW:])]),
        "lru_bx": jnp.concatenate([dp_f[2:3], dp_b[2:3]], axis=0),
        "attn_sink": dsink[:, :N_HEADS], "norm_ffn_g": dg2, "norm_final_g": dg3,
    }
    du, dcw, dcb = hosted("conv_bwd", _conv_bwd, duc_f, duc_b, proj, cw)
    dw_in = _mm_tn(du, xn, "dw_in_u", tk=1024, tn=1024, out_rows=IN_W)
    dw_in = _mm_tn(dgl, xn, "dw_in_g", tk=1024, tn=1024, into=dw_in, row=1)
    dw_in = _mm_tn(dq, xn, "dw_in_q", tk=1024, tn=1024, into=dw_in, row=2)
    dw_in = _mm_tn(dkv, xn, "dw_in_kv", tk=512, tn=1024, into=dw_in, row=3072 // 512)
    env["dw_in_t"] = _mm_tn(dz, xn, "dw_in_z", tk=512, tn=1024, tmc=4096, into=dw_in, row=3584 // 512)
    col_pieces = [(du, 0, 0, D), (dgl, 0, 1, D), (dq, 0, 2, D), (dkv, 0, 3072 // 512, 512),
                  *[(dz, i, 3584 // 512 + i, 512) for i in range(4)]]
    dx, _, dg1 = hosted("inproj_bwd", _proj_bwd, col_pieces, env["w_in_t"], x, g1, dx1, "inproj_bwd")

    grads = dict(env["grads_early"], norm_mix_g=dg1, conv_w=dcw, conv_b=dcb)
    return dx, grads


def _adamw(gparts, w, m, v, name, tr=256):
    n, rows, cols = gparts.shape
    tr = _div_tile(rows, tr)
    c1 = 1.0 - ADAM_B1 ** ADAM_STEP
    c2 = 1.0 - ADAM_B2 ** ADAM_STEP

    def body(g_ref, w_ref, m_ref, v_ref, go_ref, d_ref, mo_ref, vo_ref):
        g = g_ref[0].astype(f32)
        for j in range(1, n):
            g = g + g_ref[j].astype(f32)
        mn = ADAM_B1 * m_ref[0] + (1.0 - ADAM_B1) * g
        vn = ADAM_B2 * v_ref[0] + (1.0 - ADAM_B2) * (g * g)
        m_hat = mn / c1
        v_hat = vn / c2
        go_ref[0] = g
        d_ref[0] = -ADAM_LR * (m_hat / (jnp.sqrt(v_hat) + ADAM_EPS) + ADAM_WD * w_ref[0])
        mo_ref[0] = mn
        vo_ref[0] = vn

    blk = pl.BlockSpec((1, tr, cols), lambda i: (0, i, 0))
    shp = jax.ShapeDtypeStruct((1, rows, cols), f32)
    return pl.pallas_call(
        body, name=name, grid=(rows // tr,),
        in_specs=[pl.BlockSpec((n, tr, cols), lambda i: (0, i, 0)), blk, blk, blk],
        out_specs=[blk, blk, blk, blk], out_shape=[shp, shp, shp, shp],
        compiler_params=_cparams())(gparts, w, m, v)


def _sum_parts(parts, name):
    n, rows, cols = parts.shape

    def body(p_ref, o_ref):
        acc = p_ref[0].astype(f32)
        for j in range(1, n):
            acc = acc + p_ref[j].astype(f32)
        o_ref[...] = acc

    return pl.pallas_call(
        body, name=name, out_shape=jax.ShapeDtypeStruct((rows, cols), f32),
        compiler_params=_cparams())(parts)


def _pack_rows(arrs, dtype=f32):
    rows, spans, at = [], [], 0
    for a in arrs:
        flat = a.reshape(-1).astype(dtype)
        nr = -(-flat.shape[0] // 1024)
        rows.append(jnp.pad(flat, (0, nr * 1024 - flat.shape[0])).reshape(nr, 1024))
        spans.append((at, nr))
        at += nr
    pad = (-at) % 16
    if pad:
        rows.append(jnp.zeros((pad, 1024), dtype))
    return jnp.concatenate(rows, axis=0), spans


def _unpack_rows(packed, spans, shapes):
    out = []
    for (at, nr), shp in zip(spans, shapes):
        n = math.prod(shp)
        out.append(packed[at:at + nr].reshape(-1)[:n].reshape(shp))
    return out


BIG = ("w_in", "w_out", "w_ffn_in", "w_ffn_out")
SMALL_REPL = ("norm_mix_g", "b_gate", "conv_b", "attn_sink", "norm_ffn_g", "norm_final_g")
GATE_W = ("lru_wa", "lru_wx")
SMALL_SHARD = ("conv_w", "lru_lambda", "lru_ba", "lru_bx")
ORDER = ("norm_mix_g", "w_in", "b_gate", "conv_w", "conv_b", "lru_lambda", "lru_wa", "lru_ba", "lru_wx",
         "lru_bx", "attn_sink", "w_out", "norm_ffn_g", "w_ffn_in", "w_ffn_out", "norm_final_g")
EARLY_F32 = ("loss", "b_gate", "lru_lambda", "lru_ba", "lru_bx", "attn_sink", "norm_ffn_g", "norm_final_g")
LATE = ("norm_mix_g", "conv_w", "conv_b")


def kernel(x, norm_mix_g, w_in, b_gate, conv_w, conv_b, lru_lambda, lru_wa, lru_ba, lru_wx, lru_bx, attn_sink, w_out, norm_ffn_g, w_ffn_in, w_ffn_out, norm_final_g, loss_target, m_norm_mix_g, m_w_in, m_b_gate, m_conv_w, m_conv_b, m_lru_lambda, m_lru_wa, m_lru_ba, m_lru_wx, m_lru_bx, m_attn_sink, m_w_out, m_norm_ffn_g, m_w_ffn_in, m_w_ffn_out, m_norm_final_g, v_norm_mix_g, v_w_in, v_b_gate, v_conv_w, v_conv_b, v_lru_lambda, v_lru_wa, v_lru_ba, v_lru_wx, v_lru_bx, v_attn_sink, v_w_out, v_norm_ffn_g, v_w_ffn_in, v_w_ffn_out, v_norm_final_g):
    w = dict(norm_mix_g=norm_mix_g, w_in=w_in, b_gate=b_gate, conv_w=conv_w, conv_b=conv_b, lru_lambda=lru_lambda,
             lru_wa=lru_wa, lru_ba=lru_ba, lru_wx=lru_wx, lru_bx=lru_bx, attn_sink=attn_sink, w_out=w_out,
             norm_ffn_g=norm_ffn_g, w_ffn_in=w_ffn_in, w_ffn_out=w_ffn_out, norm_final_g=norm_final_g)
    m = dict(norm_mix_g=m_norm_mix_g, w_in=m_w_in, b_gate=m_b_gate, conv_w=m_conv_w, conv_b=m_conv_b,
             lru_lambda=m_lru_lambda, lru_wa=m_lru_wa, lru_ba=m_lru_ba, lru_wx=m_lru_wx, lru_bx=m_lru_bx,
             attn_sink=m_attn_sink, w_out=m_w_out, norm_ffn_g=m_norm_ffn_g, w_ffn_in=m_w_ffn_in,
             w_ffn_out=m_w_ffn_out, norm_final_g=m_norm_final_g)
    v = dict(norm_mix_g=v_norm_mix_g, w_in=v_w_in, b_gate=v_b_gate, conv_w=v_conv_w, conv_b=v_conv_b,
             lru_lambda=v_lru_lambda, lru_wa=v_lru_wa, lru_ba=v_lru_ba, lru_wx=v_lru_wx, lru_bx=v_lru_bx,
             attn_sink=v_attn_sink, w_out=v_w_out, norm_ffn_g=v_norm_ffn_g, w_ffn_in=v_w_ffn_in,
             w_ffn_out=v_w_ffn_out, norm_final_g=v_norm_final_g)
    me = 4 * lax.axis_index("x") + 2 * lax.axis_index("y") + lax.axis_index("c")

    def shard_t(a):
        return jnp.swapaxes(a[0], 0, 1)

    def rows_parts(g):
        return g.reshape(N_DEV, -1, g.shape[1])

    shard_rows = jnp.concatenate([w[n][0] for n in SMALL_SHARD], axis=0)
    small = {n: w[n] for n in ("norm_mix_g", "b_gate", "conv_b", "attn_sink", "norm_ffn_g")}
    small["lru_wa"], small["lru_wx"] = lru_wa[0], lru_wx[0]
    small["norm_final_g"] = norm_final_g.reshape(1, D)
    env, recv = {}, {}

    def before(name):
        if name == "norm_x":
            return [(shard_t(w_in).astype(bf16), False), (shard_rows, False)]
        if name == "inproj":
            return [(w_out[0].astype(bf16), False), (w_ffn_out[0].astype(bf16), False)]
        if name == "attn_fwd":
            return [(shard_t(w_ffn_in).astype(bf16), False)]
        if name == "ffn_in_bwd":
            return [(rows_parts(env["dw_fo"]), True)]
        if name == "attn_bwd":
            return [(rows_parts(env["dw_out"]), True)]
        if name == "lru_bwd":
            return [(rows_parts(env["dw_fi_t"]), True)]
        if name == "conv_bwd":
            ge = env["grads_early"]
            p32, env["early_f32_spans"] = _pack_rows([ge[n] for n in EARLY_F32])
            return [(p32, False), *[(ge[n].astype(bf16).reshape(-1, LRU_BLOCK), False) for n in GATE_W]]
        if name == "inproj_bwd":
            return [(rows_parts(env["dw_in_t"]), True)]
        return []

    def after(name, got):
        if name == "norm_x":
            env["w_in_t"] = got[0].reshape(IN_W, D)
            full_rows = jnp.swapaxes(got[1], 0, 1).reshape(shard_rows.shape[0], -1)
            small["conv_w"], small["lru_lambda"] = full_rows[0:4], full_rows[4:6]
            small["lru_ba"], small["lru_bx"] = full_rows[6:8], full_rows[8:10]
        elif name == "inproj":
            env["w_out"], env["w_fo"] = got[0].reshape(D, D), got[1].reshape(D_FF, D)
        elif name == "attn_fwd":
            env["w_fi_t"] = got[0].reshape(2 * D_FF, D)
        elif name == "ffn_in_bwd":
            recv["w_ffn_out"] = got[0]
        elif name == "attn_bwd":
            recv["w_out"] = got[0]
        elif name == "lru_bwd":
            recv["w_ffn_in"] = got[0]
        elif name == "conv_bwd":
            recv["early_f32"], recv["lru_wa"], recv["lru_wx"] = got
        elif name == "inproj_bwd":
            recv["w_in"] = got[0]

    grad_x, grads = _local_step(x[0], loss_target[0], small, env, before, after)

    outs = {}
    for name in ("w_out", "w_ffn_out"):
        outs[name] = _adamw(recv[name], w[name], m[name], v[name], "adamw_" + name)
    for name in ("w_in", "w_ffn_in"):
        t = lambda a: jnp.swapaxes(a, 1, 2)
        outs[name] = [t(r) for r in _adamw(recv[name], t(w[name]), t(m[name]), t(v[name]), "adamw_" + name)]
    for name in GATE_W:
        t = lambda a: a.reshape(1, -1, LRU_BLOCK)
        res = _adamw(recv[name], t(w[name]), t(m[name]), t(v[name]), "adamw_" + name)
        outs[name] = [r.reshape(w[name].shape) for r in res]

    small_names = SMALL_REPL + SMALL_SHARD
    late_packed, late_spans = _pack_rows([grads[n] for n in LATE])
    (got_late,) = _exchange([(late_packed, False)], "gather_late_grads")
    summed = {}
    for names, got, spans, tag in ((EARLY_F32, recv["early_f32"], env["early_f32_spans"], "early_f32"),
                                   (LATE, got_late, late_spans, "late")):
        total = _sum_parts(got, "sum_small_" + tag)
        summed.update(zip(names, _unpack_rows(total, spans, [grads[n].shape for n in names])))
    loss = summed["loss"].reshape(())
    gsm = {n: summed[n].reshape(w[n].shape) for n in SMALL_REPL}
    for n in SMALL_SHARD:
        full = summed[n]
        gsm[n] = lax.dynamic_slice_in_dim(full, me * 128, 128, axis=1).reshape(w[n].shape)
    pk = lambda dct: _pack_rows([dct[n] for n in small_names])[0]
    gp, sp = _pack_rows([gsm[n] for n in small_names])
    res = _adamw(gp[None], pk(w)[None], pk(m)[None], pk(v)[None], "adamw_small")
    sshapes = [w[n].shape for n in small_names]
    for idx, t in enumerate(res):
        for n, a in zip(small_names, _unpack_rows(t[0], sp, sshapes)):
            outs.setdefault(n, [None] * 4)[idx] = a

    result = [loss, grad_x[None]]
    for idx in range(4):
        result += [outs[n][idx] for n in ORDER]
    return tuple(result)
```

```python
import functools
import math

import jax
import jax.numpy as jnp
from jax import lax
from jax.experimental import pallas as pl
from jax.experimental.pallas import tpu as pltpu

f32 = jnp.float32
bf16 = jnp.bfloat16

D = 1024
D_FF = 2816
IN_W = 5632
N_HEADS = 16
N_KV = 4
HEAD_DIM = 64
WINDOW = 128
BLK = 128
LRU_HEADS = 16
LRU_BLOCK = 64
LRU_GROUPS = 4
LRU_GW = 256
LRU_CHUNK = 128
LRU_ROWS = 2048
RGLRU_C = 8.0
EPS = 1e-6
NEG_INF = -1e30
N_DEV = 8

ADAM_LR = 0.001
ADAM_B1 = 0.9
ADAM_B2 = 0.999
ADAM_EPS = 1e-08
ADAM_WD = 0.01
ADAM_STEP = 10

VMEM_MB = 56

C_U, C_G, C_Q, C_Z0, C_Z1, C_K, C_V = 0, 1024, 2048, 3072, 4096, 5120, 5376


def _cparams(vmem_mb=VMEM_MB):
    return pltpu.CompilerParams(vmem_limit_bytes=vmem_mb << 20)


def _div_tile(n, pref):
    if n <= pref:
        return n
    return max(t for t in range(8, pref + 1, 8) if n % t == 0)


def _sigmoid(x):
    return 0.5 * jnp.tanh(0.5 * x) + 0.5


def _log1p(x):
    u = 1.0 + x
    d = u - 1.0
    return jnp.where(d == 0.0, x, jnp.log(u) * (x / jnp.where(d == 0.0, 1.0, d)))


def _softplus(x):
    return jnp.maximum(x, 0.0) + _log1p(jnp.exp(-jnp.abs(x)))


def _gelu_and_grad(x):
    c = math.sqrt(2.0 / math.pi)
    inner = c * (x + 0.044715 * (x * x * x))
    t = jnp.tanh(inner)
    gelu = 0.5 * x * (1.0 + t)
    dinner = c * (1.0 + 3 * 0.044715 * (x * x))
    dgelu = 0.5 * (1.0 + t) + 0.5 * x * (1.0 - t * t) * dinner
    return gelu, dgelu


def _rms_bwd(dn, xv, g):
    r = lax.rsqrt(jnp.mean(xv * xv, axis=-1, keepdims=True) + EPS)
    xh = xv * r
    dxh = dn * g
    dx = r * (dxh - xh * jnp.mean(dxh * xh, axis=-1, keepdims=True))
    return dx, dn * xh


ANY_SPEC = pl.BlockSpec(memory_space=pl.ANY)


def _comm_out_shape(src, scatter):
    return jax.ShapeDtypeStruct((N_DEV, *(src.shape[1:] if scatter else src.shape)), src.dtype)


def _comm_sems():
    return [pltpu.SemaphoreType.DMA((N_DEV - 1,)), pltpu.SemaphoreType.DMA((N_DEV - 1,)), pltpu.SemaphoreType.DMA]


def _scatter_descs(src_ref, out_ref, send_sems, recv_sems, local_sem):
    x, y, c = lax.axis_index("x"), lax.axis_index("y"), lax.axis_index("c")
    me = 4 * x + 2 * y + c
    descs = [pltpu.make_async_copy(src_ref.at[me], out_ref.at[me], local_sem)]
    for k in range(1, N_DEV):
        px, py, pc = x ^ (k >> 2), y ^ ((k >> 1) & 1), c ^ (k & 1)
        descs.append(pltpu.make_async_remote_copy(
            src_ref=src_ref.at[4 * px + 2 * py + pc], dst_ref=out_ref.at[me],
            send_sem=send_sems.at[k - 1], recv_sem=recv_sems.at[k - 1],
            device_id=(px, py, pc), device_id_type=pl.DeviceIdType.MESH))
    return descs


def _gather_copies(src_ref, out_ref, send_sems, recv_sems, local_sem, which):
    x, y, c = lax.axis_index("x"), lax.axis_index("y"), lax.axis_index("c")
    me, sibling = (x, y, c), (x, y, 1 - c)
    chips = [(1 - x, y), (x, 1 - y), (1 - x, 1 - y)]

    def slot(px, py, pc):
        return out_ref.at[4 * px + 2 * py + pc]

    def copy(k, block, to, src=None):
        return pltpu.make_async_remote_copy(
            src_ref=slot(*block) if src is None else src, dst_ref=slot(*block),
            send_sem=send_sems.at[k], recv_sem=recv_sems.at[k], device_id=to, device_id_type=pl.DeviceIdType.MESH)

    make = {
        "local": lambda: pltpu.make_async_copy(src_ref, slot(*me), local_sem),
        "first": lambda: [copy(0, me, sibling, src=src_ref)] + [copy(1 + j, me, (*chip, c), src=src_ref)
                                                                 for j, chip in enumerate(chips)],
        "passed": lambda: [copy(4 + j, (*chip, c), sibling) for j, chip in enumerate(chips)],
        "landed": lambda: [copy(1 + j, (*chip, c), me) for j, chip in enumerate(chips)],
        "later": lambda: [copy(0, sibling, me)] + [copy(4 + j, (*chip, 1 - c), me) for j, chip in enumerate(chips)],
    }
    return [make[name]() for name in which]


def _comm_start(src_ref, out_ref, sems, scatter):
    if scatter:
        for d in _scatter_descs(src_ref, out_ref, *sems):
            d.start()
    else:
        local, first = _gather_copies(src_ref, out_ref, *sems, which=("local", "first"))
        local.start()
        for cp in first:
            cp.start()


def _comm_pass_on(src_ref, out_ref, sems, scatter):
    if not scatter:
        landed, passed = _gather_copies(src_ref, out_ref, *sems, which=("landed", "passed"))
        for arrived, onward in zip(landed, passed):
            arrived.wait_recv()
            onward.start()


def _comm_finish(src_ref, out_ref, sems, scatter):
    if scatter:
        for d in _scatter_descs(src_ref, out_ref, *sems):
            d.wait()
    else:
        later, first, passed, local = _gather_copies(src_ref, out_ref, *sems,
                                                     which=("later", "first", "passed", "local"))
        for cp in later:
            cp.wait_recv()
        for cp in first + passed:
            cp.wait_send()
        local.wait()


def _exchange(comm, name):
    nc = len(comm)

    def body(*refs):
        srcs, outs, sems = refs[:nc], refs[nc:2 * nc], refs[2 * nc:]
        for stage in (_comm_start, _comm_pass_on, _comm_finish):
            for i in range(nc):
                stage(srcs[i], outs[i], sems[3 * i:3 * i + 3], comm[i][1])

    return pl.pallas_call(
        body, name=name, in_specs=[ANY_SPEC] * nc, out_specs=[ANY_SPEC] * nc,
        out_shape=[_comm_out_shape(*c) for c in comm],
        scratch_shapes=[s for _ in comm for s in _comm_sems()],
    )(*[c[0] for c in comm])


def _hosted_call(body, *, name, grid, in_specs, out_specs, out_shape, args, scratch_shapes=(), comm=()):
    nin, nout, nscr, nc = len(in_specs), len(out_specs), len(scratch_shapes), len(comm)
    steps = math.prod(grid)

    def wrapped(*refs):
        ins = refs[:nin]
        csrc = refs[nin:nin + nc]
        outs = refs[nin + nc:nin + nc + nout]
        cout = refs[nin + nc + nout:nin + 2 * nc + nout]
        scr = refs[nin + 2 * nc + nout:]
        sems = scr[nscr:]

        def at(step, stage):
            lin = 0
            for a in range(len(grid)):
                lin = lin * grid[a] + pl.program_id(a)

            @pl.when(lin == step)
            def _():
                for i in range(nc):
                    stage(csrc[i], cout[i], sems[3 * i:3 * i + 3], comm[i][1])

        if nc:
            at(0, _comm_start)

        body(*ins, *outs, *scr[:nscr])

        if nc:
            at((3 * (steps - 1)) // 4, _comm_pass_on)
            at(steps - 1, _comm_finish)

    res = pl.pallas_call(
        wrapped, name=name, grid=grid,
        in_specs=[*in_specs, *[ANY_SPEC] * nc], out_specs=[*out_specs, *[ANY_SPEC] * nc],
        out_shape=[*out_shape, *[_comm_out_shape(*c) for c in comm]],
        scratch_shapes=[*scratch_shapes, *[s for _ in comm for s in _comm_sems()]],
        compiler_params=_cparams())(*args, *[c[0] for c in comm])
    return res[:nout], res[nout:]


def _rmsnorm_bf16(x, g, name, tm=1024, comm=()):
    S, dm = x.shape
    tm = min(tm, S)

    def body(x_ref, g_ref, xn_ref):
        xv = x_ref[...]
        r = lax.rsqrt(jnp.mean(xv * xv, axis=-1, keepdims=True) + EPS)
        xn_ref[...] = ((xv * r) * g_ref[...]).astype(bf16)

    row = pl.BlockSpec((tm, dm), lambda i: (i, 0))
    return _hosted_call(
        body, name=name, grid=(S // tm,), in_specs=[row, pl.BlockSpec((1, dm), lambda i: (0, 0))],
        out_specs=[row], out_shape=[jax.ShapeDtypeStruct((S, dm), bf16)], args=(x, g), comm=comm)


def _matmul_t(a, wt, name, tm=2048, tn=512, row_block=lambda j: j, comm=()):
    S, dm = a.shape
    n = wt.shape[0]
    tm = min(tm, S)

    def body(a_ref, w_ref, o_ref):
        o_ref[...] = lax.dot_general(a_ref[...], w_ref[...], (((1,), (1,)), ((), ())),
                                     preferred_element_type=f32).astype(bf16)

    return _hosted_call(
        body, name=name, grid=(S // tm, n // tn),
        in_specs=[pl.BlockSpec((tm, dm), lambda i, j: (i, 0)),
                  pl.BlockSpec((tn, dm), lambda i, j: (row_block(j), 0))],
        out_specs=[pl.BlockSpec((tm, tn), lambda i, j: (i, j))],
        out_shape=[jax.ShapeDtypeStruct((S, n), bf16)], args=(a, wt), comm=comm)


def _norm_matmul(x, g, wt, name, tm=1024, tn=1408, row_block=lambda j: j, comm=()):
    S, dm = x.shape
    n = wt.shape[0]
    tm = min(tm, S)

    def body(x_ref, g_ref, w_ref, xn_ref, o_ref):
        @pl.when(pl.program_id(1) == 0)
        def _():
            xv = x_ref[...]
            r = lax.rsqrt(jnp.mean(xv * xv, axis=-1, keepdims=True) + EPS)
            xn_ref[...] = ((xv * r) * g_ref[...]).astype(bf16)

        o_ref[...] = lax.dot_general(xn_ref[...], w_ref[...], (((1,), (1,)), ((), ())),
                                     preferred_element_type=f32).astype(bf16)

    return _hosted_call(
        body, name=name, grid=(S // tm, n // tn),
        in_specs=[pl.BlockSpec((tm, dm), lambda i, j: (i, 0)),
                  pl.BlockSpec((1, dm), lambda i, j: (0, 0)),
                  pl.BlockSpec((tn, dm), lambda i, j: (row_block(j), 0))],
        out_specs=[pl.BlockSpec((tm, dm), lambda i, j: (i, 0)),
                   pl.BlockSpec((tm, tn), lambda i, j: (i, j))],
        out_shape=[jax.ShapeDtypeStruct((S, dm), bf16), jax.ShapeDtypeStruct((S, n), bf16)],
        args=(x, g, wt), comm=comm)


def _mm_tn(a, b, name, tk, tn, tmc=2048, into=None, row=0, out_rows=None):
    m, ka = a.shape
    n = b.shape[1]
    tmc = min(tmc, m)
    nk = m // tmc

    def body(a_ref, b_ref, *rest):
        o_ref, acc_ref = rest[-2:]
        k = pl.program_id(2)
        part = lax.dot_general(a_ref[...], b_ref[...], (((0,), (0,)), ((), ())), preferred_element_type=f32)

        @pl.when(k == 0)
        def _():
            acc_ref[...] = part

        @pl.when(k > 0)
        def _():
            acc_ref[...] += part

        @pl.when(k == nk - 1)
        def _():
            o_ref[...] = acc_ref[...].astype(bf16)

    in_specs = [pl.BlockSpec((tmc, tk), lambda i, j, k: (k, i)), pl.BlockSpec((tmc, tn), lambda i, j, k: (k, j))]
    if into is None:
        return pl.pallas_call(
            body, name=name, grid=(ka // tk, n // tn, nk), in_specs=in_specs,
            out_specs=pl.BlockSpec((tk, tn), lambda i, j, k: (i + row, j)),
            out_shape=jax.ShapeDtypeStruct((out_rows or ka, n), bf16),
            scratch_shapes=[pltpu.VMEM((tk, tn), f32)],
            compiler_params=_cparams())(a, b)
    return pl.pallas_call(
        body, name=name, grid=(ka // tk, n // tn, nk), in_specs=[*in_specs, ANY_SPEC],
        out_specs=pl.BlockSpec((tk, tn), lambda i, j, k: (i + row, j)),
        out_shape=jax.ShapeDtypeStruct(into.shape, into.dtype),
        scratch_shapes=[pltpu.VMEM((tk, tn), f32)], input_output_aliases={2: 0},
        compiler_params=_cparams())(a, b, into)


HALO = 16


def _rows_at(ext, o, tc):
    if o == 0:
        return ext[HALO:HALO + tc]
    return pltpu.roll(ext, (-o) % ext.shape[0], 0)[HALO:HALO + tc]


def _halo_specs(tc, S, width, col):
    per = tc // HALO
    last = S // HALO - 1
    return (pl.BlockSpec((tc, width), lambda i: (i, col)),
            pl.BlockSpec((HALO, width), lambda i: (jnp.maximum(i * per - 1, 0), col)),
            pl.BlockSpec((HALO, width), lambda i: (jnp.minimum((i + 1) * per, last), col)))


def _extended(cur_ref, prev_ref, next_ref, i, nsteps):
    prev = jnp.where(i > 0, prev_ref[...].astype(f32), 0.0)
    nxt = jnp.where(i < nsteps - 1, next_ref[...].astype(f32), 0.0)
    return jnp.concatenate([prev, cur_ref[...].astype(f32), nxt], axis=0)


def _conv_fwd(proj, cw, cb, tc=1024):
    S = proj.shape[0]
    tc = min(tc, S)
    nsteps = S // tc

    def body(cur_ref, prev_ref, next_ref, w_ref, b_ref, o_ref):
        ext = _extended(cur_ref, prev_ref, next_ref, pl.program_id(0), nsteps)
        acc = _rows_at(ext, -2, tc) * w_ref[0:1, :]
        for k in range(1, 4):
            acc = acc + _rows_at(ext, k - 2, tc) * w_ref[k:k + 1, :]
        o_ref[...] = acc + b_ref[...]

    return pl.pallas_call(
        body, name="conv_fwd", grid=(nsteps,),
        in_specs=[*_halo_specs(tc, S, D, 0),
                  pl.BlockSpec((4, D), lambda i: (0, 0)), pl.BlockSpec((1, D), lambda i: (0, 0))],
        out_specs=pl.BlockSpec((tc, D), lambda i: (i, 0)),
        out_shape=jax.ShapeDtypeStruct((S, D), f32),
        compiler_params=_cparams())(proj, proj, proj, cw, cb)


def _conv_bwd(duc_f, duc_b, proj, cw, tc=1024, comm=()):
    S = proj.shape[0]
    tc = min(tc, S)
    nsteps = S // tc

    def body(fc, fp, fn, bc, bp, bn, uc_, up, un, w_ref, du_ref, dw_ref, db_ref):
        i = pl.program_id(0)

        @pl.when(i == 0)
        def _():
            dw_ref[...] = jnp.zeros_like(dw_ref)
            db_ref[...] = jnp.zeros_like(db_ref)

        dext = _extended(fc, fp, fn, i, nsteps) + _extended(bc, bp, bn, i, nsteps)
        uext = _extended(uc_, up, un, i, nsteps)
        d = dext[HALO:HALO + tc]
        acc = _rows_at(dext, 2, tc) * w_ref[0:1, :]
        for k in range(1, 4):
            acc = acc + _rows_at(dext, 2 - k, tc) * w_ref[k:k + 1, :]
        du_ref[...] = acc.astype(bf16)
        wrow = lax.broadcasted_iota(jnp.int32, (4, D), 0)
        for k in range(4):
            dw_ref[...] += jnp.where(wrow == k, jnp.sum(d * _rows_at(uext, k - 2, tc), axis=0, keepdims=True), 0.0)
        db_ref[...] += jnp.sum(d, axis=0, keepdims=True)

    return _hosted_call(
        body, name="conv_bwd", grid=(nsteps,),
        in_specs=[*_halo_specs(tc, S, D, 0), *_halo_specs(tc, S, D, 0), *_halo_specs(tc, S, D, 0),
                  pl.BlockSpec((4, D), lambda i: (0, 0))],
        out_specs=[pl.BlockSpec((tc, D), lambda i: (i, 0)),
                   pl.BlockSpec((4, D), lambda i: (0, 0)), pl.BlockSpec((1, D), lambda i: (0, 0))],
        out_shape=[jax.ShapeDtypeStruct((S, D), bf16), jax.ShapeDtypeStruct((4, D), f32),
                   jax.ShapeDtypeStruct((1, D), f32)],
        args=(duc_f, duc_f, duc_f, duc_b, duc_b, duc_b, proj, proj, proj, cw), comm=comm)


def _scan_scratch():
    halves = [pltpu.VMEM((LRU_CHUNK, 128), f32) for _ in range(2 * (LRU_GW // 128))]
    return [*halves, pltpu.VMEM((LRU_CHUNK // 8, LRU_GW), f32), pltpu.VMEM((LRU_CHUNK // 8, LRU_GW), f32)]


def _log_scan(a, b, row, n, reverse, steps):
    for s in steps:
        shift = a.shape[0] - s if reverse else s
        keep = (row < n - s) if reverse else (row >= s)
        a_sh = pltpu.roll(a, shift, 0)
        b_sh = pltpu.roll(b, shift, 0)
        b = jnp.where(keep, a * b_sh + b, b)
        a = jnp.where(keep, a * a_sh, a)
    return a, b


def _scan_chunk(a, b, carry, reverse, *scratch):
    tc, w = a.shape
    ng = tc // 8
    nl = w // 128
    sa_refs, sb_refs, sc_ref, st_ref = scratch[:nl], scratch[nl:2 * nl], scratch[2 * nl], scratch[2 * nl + 1]
    sub = lax.broadcasted_iota(jnp.int32, (8, w), 0)
    ag, bg = [], []
    for k in range(ng):
        ak, bk = _log_scan(a[8 * k:8 * k + 8], b[8 * k:8 * k + 8], sub, 8, reverse, (1, 2, 4))
        ag.append(ak)
        bg.append(bk)
    a = jnp.concatenate(ag, axis=0)
    b = jnp.concatenate(bg, axis=0)
    edge = 0 if reverse else 7
    for i in range(nl):
        sa_refs[i][...] = a[:, 128 * i:128 * (i + 1)]
        sb_refs[i][...] = b[:, 128 * i:128 * (i + 1)]
    ta = jnp.concatenate([r[pl.ds(edge, ng, stride=8), :] for r in sa_refs], axis=1)
    tb = jnp.concatenate([r[pl.ds(edge, ng, stride=8), :] for r in sb_refs], axis=1)
    grow = lax.broadcasted_iota(jnp.int32, (ng, w), 0)
    ta, tb = _log_scan(ta, tb, grow, ng, reverse, [1 << i for i in range(ng.bit_length() - 1)])
    state = tb + ta * carry
    st_ref[...] = state
    if reverse:
        sc_ref[...] = jnp.where(grow == ng - 1, carry, pltpu.roll(state, ng - 1, 0))
    else:
        sc_ref[...] = jnp.where(grow == 0, carry, pltpu.roll(state, 1, 0))
    h = jnp.concatenate([bg[k] + ag[k] * sc_ref[k:k + 1, :] for k in range(ng)], axis=0)
    return h, (st_ref[0:1, :] if reverse else st_ref[ng - 1:ng, :])


def _lru_gates(uc, w, p_ref):
    pre = jnp.dot(uc.astype(bf16), w, preferred_element_type=f32)
    r = _sigmoid(pre[:, :LRU_GW] + p_ref[0, 1:2, :])
    gi = _sigmoid(pre[:, LRU_GW:] + p_ref[0, 2:3, :])
    sp = _softplus(-p_ref[0, 0:1, :])
    log_a = -RGLRU_C * r * sp
    a = jnp.exp(log_a)
    x = 2.0 * log_a
    series = -x * (1.0 + x * (0.5 + x * (1.0 / 6 + x * (1.0 / 24))))
    beta = jnp.sqrt(jnp.maximum(jnp.where(x > -0.0625, series, 1.0 - a * a), 0.0))
    return r, gi, sp, a, beta


def _lru_fwd(uc, wg, lp, reverse, comm=()):
    S = uc.shape[0]
    tc = LRU_CHUNK
    rows = min(LRU_ROWS, S)
    nsub = rows // tc
    nblk = S // rows
    d = 1 if reverse else 0

    def bidx(c):
        return nblk - 1 - c if reverse else c

    def body(uc_ref, w_ref, p_ref, h_ref, carry_ref, *scan_scratch):
        @pl.when(pl.program_id(1) == 0)
        def _():
            carry_ref[...] = jnp.zeros_like(carry_ref)

        carry = carry_ref[...]
        for j in (reversed(range(nsub)) if reverse else range(nsub)):
            sl = slice(j * tc, (j + 1) * tc)
            ucv = uc_ref[sl, :]
            _, gi, _, a, beta = _lru_gates(ucv, w_ref[0], p_ref)
            h, carry = _scan_chunk(a, beta * (gi * ucv), carry, reverse, *scan_scratch)
            h_ref[sl, :] = h.astype(bf16)
        carry_ref[...] = carry

    return _hosted_call(
        body, name="lru_fwd_rev" if reverse else "lru_fwd", grid=(LRU_GROUPS, nblk),
        in_specs=[pl.BlockSpec((rows, LRU_GW), lambda g, c: (bidx(c), g)),
                  pl.BlockSpec((1, LRU_GW, 2 * LRU_GW), lambda g, c: (g, 0, d)),
                  pl.BlockSpec((1, 8, LRU_GW), lambda g, c: (d, 0, g))],
        out_specs=[pl.BlockSpec((rows, LRU_GW), lambda g, c: (bidx(c), g))],
        out_shape=[jax.ShapeDtypeStruct((S, D), bf16)],
        scratch_shapes=[pltpu.VMEM((1, LRU_GW), f32), *_scan_scratch()],
        args=(uc, wg, lp), comm=comm)


def _lru_bwd(uc, dh, h, wg, lp, reverse, comm=()):
    S = uc.shape[0]
    tc = LRU_CHUNK
    rows = min(LRU_ROWS, S)
    nsub = rows // tc
    nblk = S // rows
    d = 1 if reverse else 0
    per = rows // HALO
    last8 = S // HALO - 1

    def bidx(c):
        return c if reverse else nblk - 1 - c

    def halo_idx(c):
        if reverse:
            return jnp.minimum((bidx(c) + 1) * per, last8)
        return jnp.maximum(bidx(c) * per - 1, 0)

    def body(uc_ref, dh_ref, h_ref, halo_ref, w_ref, p_ref, duc_ref, dw_ref, dp_ref, carry_ref, tmp_ref,
             *scan_scratch):
        c = pl.program_id(1)
        bi = bidx(c)

        @pl.when(c == 0)
        def _():
            carry_ref[...] = jnp.zeros_like(carry_ref)
            dw_ref[...] = jnp.zeros_like(dw_ref)
            dp_ref[...] = jnp.zeros_like(dp_ref)

        row = lax.broadcasted_iota(jnp.int32, (tc, LRU_GW), 0)
        carry = carry_ref[...]
        dw = jnp.zeros((LRU_GW, 2 * LRU_GW), f32)
        dsp = jnp.zeros((1, LRU_GW), f32)
        dba = jnp.zeros((1, LRU_GW), f32)
        dbx = jnp.zeros((1, LRU_GW), f32)
        for j in (range(nsub) if reverse else reversed(range(nsub))):
            sl = slice(j * tc, (j + 1) * tc)
            ucv = uc_ref[sl, :]
            ucb = ucv.astype(bf16)
            r, gi, sp, a, beta = _lru_gates(ucv, w_ref[0], p_ref)
            hv = h_ref[sl, :].astype(f32)
            dhv = dh_ref[sl, :].astype(f32)
            if reverse:
                alpha = jnp.where(row == 0, 1.0, pltpu.roll(a, 1, 0))
                gsc, _ = _scan_chunk(alpha, dhv, carry, False, *scan_scratch)
                if j < nsub - 1:
                    edge = h_ref[(j + 1) * tc:(j + 1) * tc + HALO, :].astype(f32)[0:1, :]
                else:
                    edge = jnp.where(bi < nblk - 1, halo_ref[...].astype(f32)[0:1, :], 0.0)
                h_nb = jnp.where(row == tc - 1, edge, pltpu.roll(hv, tc - 1, 0))
            else:
                alpha = jnp.where(row == tc - 1, 1.0, pltpu.roll(a, tc - 1, 0))
                gsc, _ = _scan_chunk(alpha, dhv, carry, True, *scan_scratch)
                if j > 0:
                    edge = h_ref[j * tc - HALO:j * tc, :].astype(f32)[HALO - 1:HALO, :]
                else:
                    edge = jnp.where(bi > 0, halo_ref[...].astype(f32)[HALO - 1:HALO, :], 0.0)
                h_nb = jnp.where(row == 0, edge, pltpu.roll(hv, 1, 0))
            tmp_ref[...] = a * gsc
            carry = tmp_ref[tc - 1:tc, :] if reverse else tmp_ref[0:1, :]

            da = gsc * h_nb
            dbeta = gsc * (gi * ucv)
            dl = da * a - dbeta * (a * a) / beta
            dr = dl * (-RGLRU_C * sp)
            dsp = dsp + jnp.sum(dl * (-RGLRU_C * r), axis=0, keepdims=True)
            dgi = gsc * beta * ucv
            dpre_r = dr * r * (1.0 - r)
            dpre_i = dgi * gi * (1.0 - gi)
            dba = dba + jnp.sum(dpre_r, axis=0, keepdims=True)
            dbx = dbx + jnp.sum(dpre_i, axis=0, keepdims=True)
            dpre = jnp.concatenate([dpre_r, dpre_i], axis=1).astype(bf16)
            back = lax.dot_general(dpre, w_ref[0], (((1,), (1,)), ((), ())), preferred_element_type=f32)
            duc_ref[sl, :] = (gsc * beta * gi + back).astype(bf16)
            dw = dw + lax.dot_general(ucb, dpre, (((0,), (0,)), ((), ())), preferred_element_type=f32)
        carry_ref[...] = carry
        dw_ref[0] += dw
        dlam = -dsp / (1.0 + jnp.exp(p_ref[0, 0:1, :]))
        prow = lax.broadcasted_iota(jnp.int32, (8, LRU_GW), 0)
        dp_ref[...] += (jnp.where(prow == 0, dlam, 0.0) + jnp.where(prow == 1, dba, 0.0)
                        + jnp.where(prow == 2, dbx, 0.0))

    chunk = pl.BlockSpec((rows, LRU_GW), lambda g, c: (bidx(c), g))
    return _hosted_call(
        body, name="lru_bwd_rev" if reverse else "lru_bwd", grid=(LRU_GROUPS, nblk),
        in_specs=[chunk, chunk, chunk,
                  pl.BlockSpec((HALO, LRU_GW), lambda g, c: (halo_idx(c), g)),
                  pl.BlockSpec((1, LRU_GW, 2 * LRU_GW), lambda g, c: (g, 0, d)),
                  pl.BlockSpec((1, 8, LRU_GW), lambda g, c: (d, 0, g))],
        out_specs=[chunk,
                   pl.BlockSpec((1, LRU_GW, 2 * LRU_GW), lambda g, c: (g, 0, 0)),
                   pl.BlockSpec((8, LRU_GW), lambda g, c: (0, g))],
        out_shape=[jax.ShapeDtypeStruct((S, D), bf16),
                   jax.ShapeDtypeStruct((LRU_GROUPS, LRU_GW, 2 * LRU_GW), f32),
                   jax.ShapeDtypeStruct((8, D), f32)],
        scratch_shapes=[pltpu.VMEM((1, LRU_GW), f32), pltpu.VMEM((tc, LRU_GW), f32), *_scan_scratch()],
        args=(uc, dh, h, h, wg, lp), comm=comm)


def _slope(h):
    return 2.0 ** (-8.0 * (h + 1.0) / N_HEADS)


ATT_QB = 4


def _kv_specs(nb, col):
    return [pl.BlockSpec((BLK, N_KV * HEAD_DIM), lambda n: (jnp.maximum(ATT_QB * n - 1, 0), col)),
            pl.BlockSpec((ATT_QB * BLK, N_KV * HEAD_DIM), lambda n: (n, col)),
            pl.BlockSpec((BLK, N_KV * HEAD_DIM), lambda n: (jnp.minimum(ATT_QB * (n + 1), nb - 1), col))]


def _key_blocks(prev_ref, cur_ref, next_ref):
    return [prev_ref[...], *[cur_ref[BLK * s:BLK * (s + 1), :] for s in range(ATT_QB)], next_ref[...]]


def _dup_windows(r0, r1, r2):
    left = lax.broadcasted_iota(jnp.int32, (3 * BLK, 128), 1) < HEAD_DIM
    win = jnp.concatenate([r0, r1, r2], axis=0)
    out = []
    for i in range(N_KV // 2):
        t = win[:, i * 128:(i + 1) * 128]
        r = pltpu.roll(t, HEAD_DIM, 1)
        out += [jnp.where(left, t, r).astype(bf16), jnp.where(left, r, t).astype(bf16)]
    return out


def _attn_bias_init(bias_ref):
    k_loc = lax.broadcasted_iota(jnp.int32, (3 * BLK, BLK), 0)
    q_loc = lax.broadcasted_iota(jnp.int32, (3 * BLK, BLK), 1)
    adist = jnp.abs(q_loc + BLK - k_loc)
    adf = adist.astype(f32)
    for e in range(3):
        ok = adist <= WINDOW
        if e == 0:
            ok = ok & (k_loc >= BLK)
        if e == 2:
            ok = ok & (k_loc < 2 * BLK)
        for kv in range(N_KV):
            bias_ref[e, kv] = jnp.concatenate(
                [jnp.where(ok, (-_slope(4 * kv + j)) * adf, NEG_INF) for j in range(4)], axis=1)


def _stack_heads(ref, sub, kv, scale):
    left = lax.broadcasted_iota(jnp.int32, (BLK, 128), 1) < HEAD_DIM
    rows = []
    for pp in range(2):
        t = ref[BLK * sub:BLK * (sub + 1), (2 * kv + pp) * 128:(2 * kv + pp + 1) * 128]
        if scale != 1.0:
            t = t * scale
        zero = jnp.zeros_like(t)
        rows += [jnp.where(left, t, zero).astype(bf16), jnp.where(left, zero, t).astype(bf16)]
    return jnp.concatenate(rows, axis=0)


def _attn_softmax(qs, k2, bias, sink_ref, kv, stats=None):
    sink = jnp.concatenate([jnp.full((1, BLK), sink_ref[0, 4 * kv + j], f32) for j in range(4)], axis=1)
    s = lax.dot_general(k2, qs, (((1,), (1,)), ((), ())), preferred_element_type=f32) + bias
    m = jnp.maximum(jnp.max(s, axis=0, keepdims=True), sink) if stats is None else stats[0]
    p = jnp.exp(s - m)
    ps = jnp.exp(sink - m)
    inv = 1.0 / (jnp.sum(p, axis=0, keepdims=True) + ps) if stats is None else stats[1]
    return p, ps, m, inv


def _pair_tiles(t):
    return [jnp.concatenate([t[:HEAD_DIM, 256 * pp:256 * pp + 128],
                             t[HEAD_DIM:, 256 * pp + 128:256 * pp + 256]], axis=0).T for pp in range(2)]


def _attn_fwd(proj, sink, comm=()):
    S = proj.shape[0]
    nb = S // BLK
    assert nb >= 2 and nb % ATT_QB == 0

    def body(q_ref, k0, k1, k2_, v0, v1, v2_, sink_ref, o_ref, st_ref, bias_ref):
        n = pl.program_id(0)

        @pl.when(n == 0)
        def _():
            _attn_bias_init(bias_ref)

        kb = _key_blocks(k0, k1, k2_)
        vb = _key_blocks(v0, v1, v2_)
        for sub in range(ATT_QB):
            blk = ATT_QB * n + sub
            e = jnp.where(blk == 0, 0, jnp.where(blk == nb - 1, 2, 1))
            kk = _dup_windows(*kb[sub:sub + 3])
            vv = _dup_windows(*vb[sub:sub + 3])
            tiles = []
            for kv in range(N_KV):
                qs = _stack_heads(q_ref, sub, kv, HEAD_DIM ** -0.5)
                p, _, m, inv = _attn_softmax(qs, kk[kv], bias_ref[e, kv], sink_ref, kv)
                st_ref[sub, kv:kv + 1, :] = m
                st_ref[sub, N_KV + kv:N_KV + kv + 1, :] = inv
                ot = lax.dot_general(vv[kv], p.astype(bf16), (((0,), (0,)), ((), ())), preferred_element_type=f32)
                tiles += _pair_tiles(ot * inv)
            o_ref[BLK * sub:BLK * (sub + 1), :] = jnp.concatenate(tiles, axis=1).astype(bf16)

    return _hosted_call(
        body, name="attn_fwd", grid=(nb // ATT_QB,),
        in_specs=[pl.BlockSpec((ATT_QB * BLK, D), lambda n: (n, C_Q // D)),
                  *_kv_specs(nb, C_K // (N_KV * HEAD_DIM)), *_kv_specs(nb, C_V // (N_KV * HEAD_DIM)),
                  pl.BlockSpec(memory_space=pltpu.SMEM)],
        out_specs=[pl.BlockSpec((ATT_QB * BLK, D), lambda n: (n, 0)),
                   pl.BlockSpec((ATT_QB, 2 * N_KV, 4 * BLK), lambda n: (n, 0, 0))],
        out_shape=[jax.ShapeDtypeStruct((S, D), bf16), jax.ShapeDtypeStruct((nb, 2 * N_KV, 4 * BLK), f32)],
        scratch_shapes=[pltpu.VMEM((3, N_KV, 3 * BLK, 4 * BLK), f32)],
        args=(proj, proj, proj, proj, proj, proj, proj, sink), comm=comm)


def _attn_bwd(proj, sink, dyb, stats, comm=()):
    S = proj.shape[0]
    nb = S // BLK
    assert nb >= 2 and nb % ATT_QB == 0
    nsteps = nb // ATT_QB
    kvw = N_KV * HEAD_DIM

    def body(q_ref, k0, k1, k2_, v0, v1, v2_, sink_ref, do_ref, st_ref, dq_ref, dkv_out, ds_ref,
             bias_ref, dk_ref, dv_ref, dsk_ref, dkv_ref):
        n = pl.program_id(0)

        @pl.when(n == 0)
        def _():
            _attn_bias_init(bias_ref)
            dk_ref[...] = jnp.zeros_like(dk_ref)
            dv_ref[...] = jnp.zeros_like(dv_ref)
            dsk_ref[...] = jnp.zeros_like(dsk_ref)

        kb = _key_blocks(k0, k1, k2_)
        vb = _key_blocks(v0, v1, v2_)
        left3 = lax.broadcasted_iota(jnp.int32, (3 * BLK, 128), 1) < HEAD_DIM
        for sub in range(ATT_QB):
            blk = ATT_QB * n + sub
            e = jnp.where(blk == 0, 0, jnp.where(blk == nb - 1, 2, 1))
            kk = _dup_windows(*kb[sub:sub + 3])
            vv = _dup_windows(*vb[sub:sub + 3])
            start = pl.multiple_of(blk * BLK, BLK)
            dq_tiles, dks, dvs = [], [], []
            for kv in range(N_KV):
                qs = _stack_heads(q_ref, sub, kv, HEAD_DIM ** -0.5)
                dos = _stack_heads(do_ref, sub, kv, 1.0)
                stats = (st_ref[sub, kv:kv + 1, :], st_ref[sub, N_KV + kv:N_KV + kv + 1, :])
                p, ps, _, inv = _attn_softmax(qs, kk[kv], bias_ref[e, kv], sink_ref, kv, stats)
                pn = p * inv
                dp = lax.dot_general(vv[kv], dos, (((1,), (1,)), ((), ())), preferred_element_type=f32)
                delta = jnp.sum(pn * dp, axis=0, keepdims=True)
                dsc = (pn * (dp - delta)).astype(bf16)
                dsk_ref[kv:kv + 1, :] += delta * (ps * inv)
                dqt = lax.dot_general(kk[kv], dsc, (((0,), (0,)), ((), ())), preferred_element_type=f32)
                dq_tiles += _pair_tiles(dqt * (HEAD_DIM ** -0.5))
                dk = jnp.dot(dsc, qs, preferred_element_type=f32)
                dv = jnp.dot(pn.astype(bf16), dos, preferred_element_type=f32)
                dks.append(dk + pltpu.roll(dk, HEAD_DIM, 1))
                dvs.append(dv + pltpu.roll(dv, HEAD_DIM, 1))
            for jp in range(N_KV // 2):
                cols = slice(jp * 128, (jp + 1) * 128)
                dk_ref[pl.ds(start, 3 * BLK), cols] += jnp.where(left3, dks[2 * jp], dks[2 * jp + 1])
                dv_ref[pl.ds(start, 3 * BLK), cols] += jnp.where(left3, dvs[2 * jp], dvs[2 * jp + 1])
            dq_ref[BLK * sub:BLK * (sub + 1), :] = jnp.concatenate(dq_tiles, axis=1).astype(bf16)

        @pl.when(n == nsteps - 1)
        def _():
            rows = min(S, 512)
            for c in range(S // rows):
                dkv_ref[rows * c:rows * (c + 1), :kvw] = dk_ref[BLK + rows * c:BLK + rows * (c + 1), :].astype(bf16)
                dkv_ref[rows * c:rows * (c + 1), kvw:] = dv_ref[BLK + rows * c:BLK + rows * (c + 1), :].astype(bf16)
            pltpu.sync_copy(dkv_ref, dkv_out)
            lane = lax.broadcasted_iota(jnp.int32, (1, 128), 1)
            dsink = jnp.zeros((1, 128), f32)
            for h in range(N_HEADS):
                part = dsk_ref[h // 4:h // 4 + 1, (h % 4) * BLK:(h % 4 + 1) * BLK]
                dsink = dsink + jnp.where(lane == h, -jnp.sum(part), 0.0)
            ds_ref[...] = dsink

    acc = jax.ShapeDtypeStruct((S + 2 * BLK, N_KV * HEAD_DIM), f32)
    return _hosted_call(
        body, name="attn_bwd", grid=(nsteps,),
        in_specs=[pl.BlockSpec((ATT_QB * BLK, D), lambda n: (n, C_Q // D)),
                  *_kv_specs(nb, C_K // (N_KV * HEAD_DIM)), *_kv_specs(nb, C_V // (N_KV * HEAD_DIM)),
                  pl.BlockSpec(memory_space=pltpu.SMEM),
                  pl.BlockSpec((ATT_QB * BLK, D), lambda n: (n, 0)),
                  pl.BlockSpec((ATT_QB, 2 * N_KV, 4 * BLK), lambda n: (n, 0, 0))],
        out_specs=[pl.BlockSpec((ATT_QB * BLK, D), lambda n: (n, 0)), ANY_SPEC,
                   pl.BlockSpec((1, 128), lambda n: (0, 0))],
        out_shape=[jax.ShapeDtypeStruct((S, D), bf16), jax.ShapeDtypeStruct((S, 2 * kvw), bf16),
                   jax.ShapeDtypeStruct((1, 128), f32)],
        scratch_shapes=[pltpu.VMEM((3, N_KV, 3 * BLK, 4 * BLK), f32), pltpu.VMEM(acc.shape, f32),
                        pltpu.VMEM(acc.shape, f32), pltpu.VMEM((8, 4 * BLK), f32), pltpu.VMEM((S, 2 * kvw), bf16)],
        args=(proj, proj, proj, proj, proj, proj, proj, sink, dyb, stats), comm=comm)


def _merge_parts(hf, hb, g, z0, z1, yb, bg):
    g0 = _sigmoid(z0 + bg[:, :D].astype(bf16))
    g1 = _sigmoid(z1 + bg[:, D:].astype(bf16))
    gelu, dgelu = _gelu_and_grad(g)
    hs = hf + hb
    ya = hs * gelu
    return g0, g1, gelu, dgelu, hs, ya


def _merge_outproj(x, hf, hb, proj, yb, bg, w_out, tm=1024):
    S = x.shape[0]
    tm = min(tm, S)

    def body(x_ref, hf_ref, hb_ref, g_ref, z0_ref, z1_ref, yb_ref, bg_ref, w_ref, mg_ref, x1_ref):
        ybv = yb_ref[...]
        g0, g1, _, _, _, ya = _merge_parts(hf_ref[...], hb_ref[...], g_ref[...], z0_ref[...], z1_ref[...],
                                           ybv, bg_ref[...])
        mg = g0 * ya + g1 * ybv
        mg_ref[...] = mg
        x1_ref[...] = x_ref[...] + jnp.dot(mg, w_ref[...], preferred_element_type=f32)

    row = pl.BlockSpec((tm, D), lambda i: (i, 0))
    return pl.pallas_call(
        body, name="merge_outproj", grid=(S // tm,),
        in_specs=[row, row, row,
                  pl.BlockSpec((tm, D), lambda i: (i, C_G // D)),
                  pl.BlockSpec((tm, D), lambda i: (i, C_Z0 // D)),
                  pl.BlockSpec((tm, D), lambda i: (i, C_Z1 // D)),
                  row, pl.BlockSpec((1, 2 * D), lambda i: (0, 0)), pl.BlockSpec((D, D), lambda i: (0, 0))],
        out_specs=[row, row],
        out_shape=[jax.ShapeDtypeStruct((S, D), bf16), jax.ShapeDtypeStruct((S, D), f32)],
        compiler_params=_cparams())(x, hf, hb, proj, proj, proj, yb, bg, w_out)


def _ffn_out_loss(gu, x1, w_fo, g3, tgt, tm=256):
    S = x1.shape[0]
    tm = min(tm, S)

    def body(gt_ref, up_ref, x1_ref, w_ref, g_ref, t_ref, ff_ref, dx_ref, dxb_ref, loss_ref, dg_ref,
             dgt_ref, dup_ref):
        @pl.when(pl.program_id(0) == 0)
        def _():
            loss_ref[...] = jnp.zeros_like(loss_ref)
            dg_ref[...] = jnp.zeros_like(dg_ref)

        gt = gt_ref[...]
        up = up_ref[...]
        sg = _sigmoid(gt)
        silu = gt * sg
        ff = silu * up
        ff_ref[...] = ff
        x2 = x1_ref[...] + jnp.dot(ff, w_ref[...], preferred_element_type=f32)
        gv = g_ref[...]
        r = lax.rsqrt(jnp.mean(x2 * x2, axis=-1, keepdims=True) + EPS)
        xh = x2 * r
        diff = xh * gv - t_ref[...]
        loss_ref[...] += (0.5 / D) * jnp.sum(diff * diff)
        dy = diff * (1.0 / D)
        dg_ref[...] += jnp.sum(dy * xh, axis=0, keepdims=True)
        dxh = dy * gv
        dx = r * (dxh - xh * jnp.mean(dxh * xh, axis=-1, keepdims=True))
        dx_ref[...] = dx
        dxb = dx.astype(bf16)
        dxb_ref[...] = dxb
        dff = lax.dot_general(dxb, w_ref[...], (((1,), (1,)), ((), ())), preferred_element_type=f32)
        dup_ref[...] = (dff * silu.astype(f32)).astype(bf16)
        dgt_ref[...] = (dff * (up * (sg * (1.0 + gt * (1.0 - sg)))).astype(f32)).astype(bf16)

    row = pl.BlockSpec((tm, D), lambda i: (i, 0))
    vec = pl.BlockSpec((1, D), lambda i: (0, 0))
    wide = pl.BlockSpec((tm, D_FF), lambda i: (i, 0))
    wide_shape = jax.ShapeDtypeStruct((S, D_FF), bf16)
    return pl.pallas_call(
        body, name="ffn_out_loss", grid=(S // tm,),
        in_specs=[wide, pl.BlockSpec((tm, D_FF), lambda i: (i, 1)),
                  row, pl.BlockSpec((D_FF, D), lambda i: (0, 0)), vec, row],
        out_specs=[wide, row, row, pl.BlockSpec((1, 128), lambda i: (0, 0)), vec, wide, wide],
        out_shape=[wide_shape, jax.ShapeDtypeStruct((S, D), f32), jax.ShapeDtypeStruct((S, D), bf16),
                   jax.ShapeDtypeStruct((1, 128), f32), jax.ShapeDtypeStruct((1, D), f32), wide_shape, wide_shape],
        compiler_params=_cparams())(gu, gu, x1, w_fo, g3, tgt)


def _proj_bwd(pieces, wt, xres, g, dres, name, tm=512, comm=()):
    S = xres.shape[0]
    tm = min(tm, S)
    np_ = len(pieces)

    def body(*refs):
        p_refs = refs[:np_]
        w_refs = refs[np_:2 * np_]
        x_ref, g_ref, dres_ref, dx_ref, dxb_ref, dg_ref = refs[2 * np_:]

        @pl.when(pl.program_id(0) == 0)
        def _():
            dg_ref[...] = jnp.zeros_like(dg_ref)

        dn = jnp.dot(p_refs[0][...], w_refs[0][...], preferred_element_type=f32)
        for pr, wr in zip(p_refs[1:], w_refs[1:]):
            dn = dn + jnp.dot(pr[...], wr[...], preferred_element_type=f32)
        dxn, dgc = _rms_bwd(dn, x_ref[...], g_ref[...])
        dx = dres_ref[...] + dxn
        dx_ref[...] = dx
        dxb_ref[...] = dx.astype(bf16)
        dg_ref[...] += jnp.sum(dgc, axis=0, keepdims=True)

    row = pl.BlockSpec((tm, D), lambda i: (i, 0))
    vec = pl.BlockSpec((1, D), lambda i: (0, 0))
    return _hosted_call(
        body, name=name, grid=(S // tm,),
        in_specs=[*[pl.BlockSpec((tm, wd), functools.partial(lambda i, cb: (i, cb), cb=acb))
                    for _, acb, _, wd in pieces],
                  *[pl.BlockSpec((wd, D), functools.partial(lambda i, rb: (rb, 0), rb=wrb))
                    for _, _, wrb, wd in pieces],
                  row, vec, row],
        out_specs=[row, row, vec],
        out_shape=[jax.ShapeDtypeStruct((S, D), f32), jax.ShapeDtypeStruct((S, D), bf16),
                   jax.ShapeDtypeStruct((1, D), f32)],
        args=(*[p[0] for p in pieces], *[wt] * np_, xres, g, dres), comm=comm)


def _outproj_bwd(dx1b, w_out, hf, hb, proj, yb, bg, tm=1024):
    S = dx1b.shape[0]
    tm = min(tm, S)

    def body(dx_ref, w_ref, hf_ref, hb_ref, g_ref, z0_ref, z1_ref, yb_ref, bg_ref,
             dh_ref, dg_ref, dz_ref, dyb_ref, dbg_ref):
        @pl.when(pl.program_id(0) == 0)
        def _():
            dbg_ref[...] = jnp.zeros_like(dbg_ref)

        dm = lax.dot_general(dx_ref[...], w_ref[...], (((1,), (1,)), ((), ())), preferred_element_type=f32)
        ybv = yb_ref[...]
        g0, g1, gelu, dgelu, hs, ya = _merge_parts(hf_ref[...], hb_ref[...], g_ref[...], z0_ref[...],
                                                   z1_ref[...], ybv, bg_ref[...])
        dh_ref[...] = (dm * (g0 * gelu).astype(f32)).astype(bf16)
        dg_ref[...] = (dm * (g0 * hs * dgelu).astype(f32)).astype(bf16)
        dyb_ref[...] = (dm * g1.astype(f32)).astype(bf16)
        dz0 = dm * (ya * (g0 * (1.0 - g0))).astype(f32)
        dz1 = dm * (ybv * (g1 * (1.0 - g1))).astype(f32)
        dz = jnp.concatenate([dz0, dz1], axis=1)
        dz_ref[...] = dz.astype(bf16)
        dbg_ref[...] += jnp.sum(dz, axis=0, keepdims=True)

    row = pl.BlockSpec((tm, D), lambda i: (i, 0))
    return pl.pallas_call(
        body, name="outproj_bwd", grid=(S // tm,),
        in_specs=[row, pl.BlockSpec((D, D), lambda i: (0, 0)), row, row,
                  pl.BlockSpec((tm, D), lambda i: (i, C_G // D)),
                  pl.BlockSpec((tm, D), lambda i: (i, C_Z0 // D)),
                  pl.BlockSpec((tm, D), lambda i: (i, C_Z1 // D)),
                  row, pl.BlockSpec((1, 2 * D), lambda i: (0, 0))],
        out_specs=[row, row, pl.BlockSpec((tm, 2 * D), lambda i: (i, 0)), row,
                   pl.BlockSpec((1, 2 * D), lambda i: (0, 0))],
        out_shape=[jax.ShapeDtypeStruct((S, D), bf16), jax.ShapeDtypeStruct((S, D), bf16),
                   jax.ShapeDtypeStruct((S, 2 * D), bf16), jax.ShapeDtypeStruct((S, D), bf16),
                   jax.ShapeDtypeStruct((1, 2 * D), f32)],
        compiler_params=_cparams())(dx1b, w_out, hf, hb, proj, proj, proj, yb, bg)


def _block_diag_groups(w):
    w4 = w.reshape(LRU_GROUPS, 4, LRU_BLOCK, LRU_BLOCK)
    eye = jnp.eye(4, dtype=w.dtype)
    return jnp.einsum("ghij,hk->ghikj", w4, eye).reshape(LRU_GROUPS, LRU_GW, LRU_GW)


def _diag_blocks(dw):
    d5 = dw.reshape(LRU_GROUPS, 4, LRU_BLOCK, 4, LRU_BLOCK)
    return jnp.stack([d5[:, h, :, h, :] for h in range(4)], axis=1).reshape(LRU_HEADS, LRU_BLOCK, LRU_BLOCK)


def _local_step(x, tgt, small, env, before=lambda name: (), after=lambda name, got: None):
    S = x.shape[0]
    g1, g2, g3 = small["norm_mix_g"], small["norm_ffn_g"], small["norm_final_g"]
    bg, cb, sink = small["b_gate"], small["conv_b"], small["attn_sink"]

    def hosted(name, fn, *args, **kw):
        outs, got = fn(*args, comm=tuple(before(name)), **kw)
        after(name, got)
        return outs

    (xn,) = hosted("norm_x", _rmsnorm_bf16, x, g1, "norm_x")
    cw = small["conv_w"]
    wg = jnp.concatenate([_block_diag_groups(small["lru_wa"][0]), _block_diag_groups(small["lru_wx"][0]),
                          _block_diag_groups(small["lru_wa"][1]), _block_diag_groups(small["lru_wx"][1])],
                         axis=2).astype(bf16)
    zeros5 = jnp.zeros((5, D), f32)
    lp = jnp.stack([jnp.concatenate([small["lru_lambda"][d:d + 1], small["lru_ba"][d:d + 1],
                                     small["lru_bx"][d:d + 1], zeros5], axis=0) for d in range(2)])
    (proj,) = hosted("inproj", _matmul_t, xn, env["w_in_t"], "inproj", tm=4096, tn=512,
                     row_block=lambda j: jnp.where(j < 6, j, jnp.where(j < 10, j + 1, 6)))
    uc = _conv_fwd(proj, cw, cb)
    (hf,), _ = _lru_fwd(uc, wg, lp, False)
    (hb,), _ = _lru_fwd(uc, wg, lp, True)
    yb, attn_stats = hosted("attn_fwd", _attn_fwd, proj, sink)
    merged, x1 = _merge_outproj(x, hf, hb, proj, yb, bg, env["w_out"])
    (xn2, gu), _ = _norm_matmul(x1, g2, env["w_fi_t"], "norm_ffn_in", tn=D_FF)
    ff, dx2, dx2b, loss, dg3, dgt, dup = _ffn_out_loss(gu, x1, env["w_fo"], g3, tgt)

    env["dw_fo"] = _mm_tn(ff, dx2b, "dw_ffn_out", tk=1408, tn=1024)
    dx1, dx1b, dg2 = hosted("ffn_in_bwd", _proj_bwd, [(dgt, 0, 0, D_FF), (dup, 0, 1, D_FF)], env["w_fi_t"],
                            x1, g2, dx2, "ffn_in_bwd")
    dw_gate = _mm_tn(dgt, xn2, "dw_ffn_in_gate", tk=1408, tn=1024, out_rows=2 * D_FF)
    env["dw_fi_t"] = _mm_tn(dup, xn2, "dw_ffn_in_up", tk=1408, tn=1024, into=dw_gate, row=D_FF // 1408)
    env["dw_out"] = _mm_tn(merged, dx1b, "dw_out", tk=1024, tn=1024)
    dh, dgl, dz, dyb, dbg = _outproj_bwd(dx1b, env["w_out"], hf, hb, proj, yb, bg)
    dq, dkv, dsink = hosted("attn_bwd", _attn_bwd, proj, sink, dyb, attn_stats)
    duc_f, dwg_f, dp_f = hosted("lru_bwd", _lru_bwd, uc, dh, hf, wg, lp, False)
    (duc_b, dwg_b, dp_b), _ = _lru_bwd(uc, dh, hb, wg, lp, True)
    env["grads_early"] = {
        "loss": loss[:, :1], "b_gate": dbg,
        "lru_lambda": jnp.concatenate([dp_f[0:1], dp_b[0:1]], axis=0),
        "lru_wa": jnp.stack([_diag_blocks(dwg_f[:, :, :LRU_GW]), _diag_blocks(dwg_b[:, :, :LRU_GW])]),
        "lru_ba": jnp.concatenate([dp_f[1:2], dp_b[1:2]], axis=0),
        "lru_wx": jnp.stack([_diag_blocks(dwg_f[:, :, LRU_GW:]), _diag_blocks(dwg_b[:, :, LRU_GW:])]),
        "lru_bx": jnp.concatenate([dp_f[2:3], dp_b[2:3]], axis=0),
        "attn_sink": dsink[:, :N_HEADS], "norm_ffn_g": dg2, "norm_final_g": dg3,
    }
    du, dcw, dcb = hosted("conv_bwd", _conv_bwd, duc_f, duc_b, proj, cw)
    dw_in = _mm_tn(du, xn, "dw_in_u", tk=1024, tn=1024, out_rows=IN_W)
    dw_in = _mm_tn(dgl, xn, "dw_in_g", tk=1024, tn=1024, into=dw_in, row=1)
    dw_in = _mm_tn(dq, xn, "dw_in_q", tk=1024, tn=1024, into=dw_in, row=2)
    dw_in = _mm_tn(dkv, xn, "dw_in_kv", tk=512, tn=1024, into=dw_in, row=3072 // 512)
    env["dw_in_t"] = _mm_tn(dz, xn, "dw_in_z", tk=512, tn=1024, tmc=4096, into=dw_in, row=3584 // 512)
    col_pieces = [(du, 0, 0, D), (dgl, 0, 1, D), (dq, 0, 2, D), (dkv, 0, 3072 // 512, 512),
                  *[(dz, i, 3584 // 512 + i, 512) for i in range(4)]]
    dx, _, dg1 = hosted("inproj_bwd", _proj_bwd, col_pieces, env["w_in_t"], x, g1, dx1, "inproj_bwd")

    grads = dict(env["grads_early"], norm_mix_g=dg1, conv_w=dcw, conv_b=dcb)
    return dx, grads


def _adamw(gparts, w, m, v, name, tr=256):
    n, rows, cols = gparts.shape
    tr = _div_tile(rows, tr)
    c1 = 1.0 - ADAM_B1 ** ADAM_STEP
    c2 = 1.0 - ADAM_B2 ** ADAM_STEP

    def body(g_ref, w_ref, m_ref, v_ref, go_ref, d_ref, mo_ref, vo_ref):
        g = g_ref[0].astype(f32)
        for j in range(1, n):
            g = g + g_ref[j].astype(f32)
        mn = ADAM_B1 * m_ref[0] + (1.0 - ADAM_B1) * g
        vn = ADAM_B2 * v_ref[0] + (1.0 - ADAM_B2) * (g * g)
        m_hat = mn / c1
        v_hat = vn / c2
        go_ref[0] = g
        d_ref[0] = -ADAM_LR * (m_hat / (jnp.sqrt(v_hat) + ADAM_EPS) + ADAM_WD * w_ref[0])
        mo_ref[0] = mn
        vo_ref[0] = vn

    blk = pl.BlockSpec((1, tr, cols), lambda i: (0, i, 0))
    shp = jax.ShapeDtypeStruct((1, rows, cols), f32)
    return pl.pallas_call(
        body, name=name, grid=(rows // tr,),
        in_specs=[pl.BlockSpec((n, tr, cols), lambda i: (0, i, 0)), blk, blk, blk],
        out_specs=[blk, blk, blk, blk], out_shape=[shp, shp, shp, shp],
        compiler_params=_cparams())(gparts, w, m, v)


def _sum_parts(parts, name):
    n, rows, cols = parts.shape

    def body(p_ref, o_ref):
        acc = p_ref[0].astype(f32)
        for j in range(1, n):
            acc = acc + p_ref[j].astype(f32)
        o_ref[...] = acc

    return pl.pallas_call(
        body, name=name, out_shape=jax.ShapeDtypeStruct((rows, cols), f32),
        compiler_params=_cparams())(parts)


def _pack_rows(arrs, dtype=f32):
    rows, spans, at = [], [], 0
    for a in arrs:
        flat = a.reshape(-1).astype(dtype)
        nr = -(-flat.shape[0] // 1024)
        rows.append(jnp.pad(flat, (0, nr * 1024 - flat.shape[0])).reshape(nr, 1024))
        spans.append((at, nr))
        at += nr
    pad = (-at) % 16
    if pad:
        rows.append(jnp.zeros((pad, 1024), dtype))
    return jnp.concatenate(rows, axis=0), spans


def _unpack_rows(packed, spans, shapes):
    out = []
    for (at, nr), shp in zip(spans, shapes):
        n = math.prod(shp)
        out.append(packed[at:at + nr].reshape(-1)[:n].reshape(shp))
    return out


BIG = ("w_in", "w_out", "w_ffn_in", "w_ffn_out")
SMALL_REPL = ("norm_mix_g", "b_gate", "conv_b", "attn_sink", "norm_ffn_g", "norm_final_g")
GATE_W = ("lru_wa", "lru_wx")
SMALL_SHARD = ("conv_w", "lru_lambda", "lru_ba", "lru_bx")
ORDER = ("norm_mix_g", "w_in", "b_gate", "conv_w", "conv_b", "lru_lambda", "lru_wa", "lru_ba", "lru_wx",
         "lru_bx", "attn_sink", "w_out", "norm_ffn_g", "w_ffn_in", "w_ffn_out", "norm_final_g")
EARLY_F32 = ("loss", "b_gate", "lru_lambda", "lru_ba", "lru_bx", "attn_sink", "norm_ffn_g", "norm_final_g")
LATE = ("norm_mix_g", "conv_w", "conv_b")


def kernel(x, norm_mix_g, w_in, b_gate, conv_w, conv_b, lru_lambda, lru_wa, lru_ba, lru_wx, lru_bx, attn_sink, w_out, norm_ffn_g, w_ffn_in, w_ffn_out, norm_final_g, loss_target, m_norm_mix_g, m_w_in, m_b_gate, m_conv_w, m_conv_b, m_lru_lambda, m_lru_wa, m_lru_ba, m_lru_wx, m_lru_bx, m_attn_sink, m_w_out, m_norm_ffn_g, m_w_ffn_in, m_w_ffn_out, m_norm_final_g, v_norm_mix_g, v_w_in, v_b_gate, v_conv_w, v_conv_b, v_lru_lambda, v_lru_wa, v_lru_ba, v_lru_wx, v_lru_bx, v_attn_sink, v_w_out, v_norm_ffn_g, v_w_ffn_in, v_w_ffn_out, v_norm_final_g):
    w = dict(norm_mix_g=norm_mix_g, w_in=w_in, b_gate=b_gate, conv_w=conv_w, conv_b=conv_b, lru_lambda=lru_lambda,
             lru_wa=lru_wa, lru_ba=lru_ba, lru_wx=lru_wx, lru_bx=lru_bx, attn_sink=attn_sink, w_out=w_out,
             norm_ffn_g=norm_ffn_g, w_ffn_in=w_ffn_in, w_ffn_out=w_ffn_out, norm_final_g=norm_final_g)
    m = dict(norm_mix_g=m_norm_mix_g, w_in=m_w_in, b_gate=m_b_gate, conv_w=m_conv_w, conv_b=m_conv_b,
             lru_lambda=m_lru_lambda, lru_wa=m_lru_wa, lru_ba=m_lru_ba, lru_wx=m_lru_wx, lru_bx=m_lru_bx,
             attn_sink=m_attn_sink, w_out=m_w_out, norm_ffn_g=m_norm_ffn_g, w_ffn_in=m_w_ffn_in,
             w_ffn_out=m_w_ffn_out, norm_final_g=m_norm_final_g)
    v = dict(norm_mix_g=v_norm_mix_g, w_in=v_w_in, b_gate=v_b_gate, conv_w=v_conv_w, conv_b=v_conv_b,
             lru_lambda=v_lru_lambda, lru_wa=v_lru_wa, lru_ba=v_lru_ba, lru_wx=v_lru_wx, lru_bx=v_lru_bx,
             attn_sink=v_attn_sink, w_out=v_w_out, norm_ffn_g=v_norm_ffn_g, w_ffn_in=v_w_ffn_in,
             w_ffn_out=v_w_ffn_out, norm_final_g=v_norm_final_g)
    me = 4 * lax.axis_index("x") + 2 * lax.axis_index("y") + lax.axis_index("c")

    def shard_t(a):
        return jnp.swapaxes(a[0], 0, 1)

    def rows_parts(g):
        return g.reshape(N_DEV, -1, g.shape[1])

    shard_rows = jnp.concatenate([w[n][0] for n in SMALL_SHARD], axis=0)
    small = {n: w[n] for n in ("norm_mix_g", "b_gate", "conv_b", "attn_sink", "norm_ffn_g")}
    small["lru_wa"], small["lru_wx"] = lru_wa[0], lru_wx[0]
    small["norm_final_g"] = norm_final_g.reshape(1, D)
    env, recv = {}, {}

    def before(name):
        if name == "norm_x":
            return [(shard_t(w_in).astype(bf16), False), (shard_rows, False)]
        if name == "inproj":
            return [(w_out[0].astype(bf16), False), (w_ffn_out[0].astype(bf16), False)]
        if name == "attn_fwd":
            return [(shard_t(w_ffn_in).astype(bf16), False)]
        if name == "ffn_in_bwd":
            return [(rows_parts(env["dw_fo"]), True)]
        if name == "attn_bwd":
            return [(rows_parts(env["dw_out"]), True)]
        if name == "lru_bwd":
            return [(rows_parts(env["dw_fi_t"]), True)]
        if name == "conv_bwd":
            ge = env["grads_early"]
            p32, env["early_f32_spans"] = _pack_rows([ge[n] for n in EARLY_F32])
            return [(p32, False), *[(ge[n].astype(bf16).reshape(-1, LRU_BLOCK), False) for n in GATE_W]]
        if name == "inproj_bwd":
            return [(rows_parts(env["dw_in_t"]), True)]
        return []

    def after(name, got):
        if name == "norm_x":
            env["w_in_t"] = got[0].reshape(IN_W, D)
            full_rows = jnp.swapaxes(got[1], 0, 1).reshape(shard_rows.shape[0], -1)
            small["conv_w"], small["lru_lambda"] = full_rows[0:4], full_rows[4:6]
            small["lru_ba"], small["lru_bx"] = full_rows[6:8], full_rows[8:10]
        elif name == "inproj":
            env["w_out"], env["w_fo"] = got[0].reshape(D, D), got[1].reshape(D_FF, D)
        elif name == "attn_fwd":
            env["w_fi_t"] = got[0].reshape(2 * D_FF, D)
        elif name == "ffn_in_bwd":
            recv["w_ffn_out"] = got[0]
        elif name == "attn_bwd":
            recv["w_out"] = got[0]
        elif name == "lru_bwd":
            recv["w_ffn_in"] = got[0]
        elif name == "conv_bwd":
            recv["early_f32"], recv["lru_wa"], recv["lru_wx"] = got
        elif name == "inproj_bwd":
            recv["w_in"] = got[0]

    grad_x, grads = _local_step(x[0], loss_target[0], small, env, before, after)

    outs = {}
    for name in ("w_out", "w_ffn_out"):
        outs[name] = _adamw(recv[name], w[name], m[name], v[name], "adamw_" + name)
    for name in ("w_in", "w_ffn_in"):
        t = lambda a: jnp.swapaxes(a, 1, 2)
        outs[name] = [t(r) for r in _adamw(recv[name], t(w[name]), t(m[name]), t(v[name]), "adamw_" + name)]
    for name in GATE_W:
        t = lambda a: a.reshape(1, -1, LRU_BLOCK)
        res = _adamw(recv[name], t(w[name]), t(m[name]), t(v[name]), "adamw_" + name)
        outs[name] = [r.reshape(w[name].shape) for r in res]

    small_names = SMALL_REPL + SMALL_SHARD
    late_packed, late_spans = _pack_rows([grads[n] for n in LATE])
    (got_late,) = _exchange([(late_packed, False)], "gather_late_grads")
    summed = {}
    for names, got, spans, tag in ((EARLY_F32, recv["early_f32"], env["early_f32_spans"], "early_f32"),
                                   (LATE, got_late, late_spans, "late")):
        total = _sum_parts(got, "sum_small_" + tag)
        summed.update(zip(names, _unpack_rows(total, spans, [grads[n].shape for n in names])))
    loss = summed["loss"].reshape(())
    gsm = {n: summed[n].reshape(w[n].shape) for n in SMALL_REPL}
    for n in SMALL_SHARD:
        full = summed[n]
        gsm[n] = lax.dynamic_slice_in_dim(full, me * 128, 128, axis=1).reshape(w[n].shape)
    pk = lambda dct: _pack_rows([dct[n] for n in small_names])[0]
    gp, sp = _pack_rows([gsm[n] for n in small_names])
    res = _adamw(gp[None], pk(w)[None], pk(m)[None], pk(v)[None], "adamw_small")
    sshapes = [w[n].shape for n in small_names]
    for idx, t in enumerate(res):
        for n, a in zip(small_names, _unpack_rows(t[0], sp, sshapes)):
            outs.setdefault(n, [None] * 4)[idx] = a

    result = [loss, grad_x[None]]
    for idx in range(4):
        result += [outs[n][idx] for n in ORDER]
    return tuple(result)
```

```python
import functools
import math

import jax
import jax.numpy as jnp
from jax import lax
from jax.experimental import pallas as pl
from jax.experimental.pallas import tpu as pltpu

f32 = jnp.float32
bf16 = jnp.bfloat16

D = 1024
D_FF = 2816
IN_W = 5632
N_HEADS = 16
N_KV = 4
HEAD_DIM = 64
WINDOW = 128
BLK = 128
LRU_HEADS = 16
LRU_BLOCK = 64
LRU_GROUPS = 4
LRU_GW = 256
LRU_CHUNK = 128
LRU_CHUNK_BWD = 256
LRU_ROWS = 2048
RGLRU_C = 8.0
EPS = 1e-6
NEG_INF = -1e30
N_DEV = 8

ADAM_LR = 0.001
ADAM_B1 = 0.9
ADAM_B2 = 0.999
ADAM_EPS = 1e-08
ADAM_WD = 0.01
ADAM_STEP = 10

VMEM_MB = 56

C_U, C_G, C_Q, C_Z0, C_Z1, C_K, C_V = 0, 1024, 2048, 3072, 4096, 5120, 5376


def _cparams(vmem_mb=VMEM_MB):
    return pltpu.CompilerParams(vmem_limit_bytes=vmem_mb << 20)


def _div_tile(n, pref):
    if n <= pref:
        return n
    return max(t for t in range(8, pref + 1, 8) if n % t == 0)


def _sigmoid(x):
    return 0.5 * jnp.tanh(0.5 * x) + 0.5


def _log1p(x):
    u = 1.0 + x
    d = u - 1.0
    return jnp.where(d == 0.0, x, jnp.log(u) * (x / jnp.where(d == 0.0, 1.0, d)))


def _softplus(x):
    return jnp.maximum(x, 0.0) + _log1p(jnp.exp(-jnp.abs(x)))


def _gelu_and_grad(x):
    c = math.sqrt(2.0 / math.pi)
    inner = c * (x + 0.044715 * (x * x * x))
    t = jnp.tanh(inner)
    gelu = 0.5 * x * (1.0 + t)
    dinner = c * (1.0 + 3 * 0.044715 * (x * x))
    dgelu = 0.5 * (1.0 + t) + 0.5 * x * (1.0 - t * t) * dinner
    return gelu, dgelu


def _rms_bwd(dn, xv, g):
    r = lax.rsqrt(jnp.mean(xv * xv, axis=-1, keepdims=True) + EPS)
    xh = xv * r
    dxh = dn * g
    dx = r * (dxh - xh * jnp.mean(dxh * xh, axis=-1, keepdims=True))
    return dx, dn * xh


ANY_SPEC = pl.BlockSpec(memory_space=pl.ANY)


def _comm_out_shape(src, scatter):
    return jax.ShapeDtypeStruct((N_DEV, *(src.shape[1:] if scatter else src.shape)), src.dtype)


def _comm_sems():
    return [pltpu.SemaphoreType.DMA((N_DEV - 1,)), pltpu.SemaphoreType.DMA((N_DEV - 1,)), pltpu.SemaphoreType.DMA]


def _scatter_descs(src_ref, out_ref, send_sems, recv_sems, local_sem):
    x, y, c = lax.axis_index("x"), lax.axis_index("y"), lax.axis_index("c")
    me = 4 * x + 2 * y + c
    descs = [pltpu.make_async_copy(src_ref.at[me], out_ref.at[me], local_sem)]
    for k in range(1, N_DEV):
        px, py, pc = x ^ (k >> 2), y ^ ((k >> 1) & 1), c ^ (k & 1)
        descs.append(pltpu.make_async_remote_copy(
            src_ref=src_ref.at[4 * px + 2 * py + pc], dst_ref=out_ref.at[me],
            send_sem=send_sems.at[k - 1], recv_sem=recv_sems.at[k - 1],
            device_id=(px, py, pc), device_id_type=pl.DeviceIdType.MESH))
    return descs


def _gather_copies(src_ref, out_ref, send_sems, recv_sems, local_sem, which):
    x, y, c = lax.axis_index("x"), lax.axis_index("y"), lax.axis_index("c")
    me, sibling = (x, y, c), (x, y, 1 - c)
    chips = [(1 - x, y), (x, 1 - y), (1 - x, 1 - y)]

    def slot(px, py, pc):
        return out_ref.at[4 * px + 2 * py + pc]

    def copy(k, block, to, src=None):
        return pltpu.make_async_remote_copy(
            src_ref=slot(*block) if src is None else src, dst_ref=slot(*block),
            send_sem=send_sems.at[k], recv_sem=recv_sems.at[k], device_id=to, device_id_type=pl.DeviceIdType.MESH)

    make = {
        "local": lambda: pltpu.make_async_copy(src_ref, slot(*me), local_sem),
        "first": lambda: [copy(0, me, sibling, src=src_ref)] + [copy(1 + j, me, (*chip, c), src=src_ref)
                                                                 for j, chip in enumerate(chips)],
        "passed": lambda: [copy(4 + j, (*chip, c), sibling) for j, chip in enumerate(chips)],
        "landed": lambda: [copy(1 + j, (*chip, c), me) for j, chip in enumerate(chips)],
        "later": lambda: [copy(0, sibling, me)] + [copy(4 + j, (*chip, 1 - c), me) for j, chip in enumerate(chips)],
    }
    return [make[name]() for name in which]


def _comm_start(src_ref, out_ref, sems, scatter):
    if scatter:
        for d in _scatter_descs(src_ref, out_ref, *sems):
            d.start()
    else:
        local, first = _gather_copies(src_ref, out_ref, *sems, which=("local", "first"))
        local.start()
        for cp in first:
            cp.start()


def _comm_pass_on(src_ref, out_ref, sems, scatter):
    if not scatter:
        landed, passed = _gather_copies(src_ref, out_ref, *sems, which=("landed", "passed"))
        for arrived, onward in zip(landed, passed):
            arrived.wait_recv()
            onward.start()


def _comm_finish(src_ref, out_ref, sems, scatter):
    if scatter:
        for d in _scatter_descs(src_ref, out_ref, *sems):
            d.wait()
    else:
        later, first, passed, local = _gather_copies(src_ref, out_ref, *sems,
                                                     which=("later", "first", "passed", "local"))
        for cp in later:
            cp.wait_recv()
        for cp in first + passed:
            cp.wait_send()
        local.wait()


def _exchange(comm, name):
    nc = len(comm)

    def body(*refs):
        srcs, outs, sems = refs[:nc], refs[nc:2 * nc], refs[2 * nc:]
        for stage in (_comm_start, _comm_pass_on, _comm_finish):
            for i in range(nc):
                stage(srcs[i], outs[i], sems[3 * i:3 * i + 3], comm[i][1])

    return pl.pallas_call(
        body, name=name, in_specs=[ANY_SPEC] * nc, out_specs=[ANY_SPEC] * nc,
        out_shape=[_comm_out_shape(*c) for c in comm],
        scratch_shapes=[s for _ in comm for s in _comm_sems()],
    )(*[c[0] for c in comm])


def _hosted_call(body, *, name, grid, in_specs, out_specs, out_shape, args, scratch_shapes=(), comm=()):
    nin, nout, nscr, nc = len(in_specs), len(out_specs), len(scratch_shapes), len(comm)
    steps = math.prod(grid)

    def wrapped(*refs):
        ins = refs[:nin]
        csrc = refs[nin:nin + nc]
        outs = refs[nin + nc:nin + nc + nout]
        cout = refs[nin + nc + nout:nin + 2 * nc + nout]
        scr = refs[nin + 2 * nc + nout:]
        sems = scr[nscr:]

        def at(step, stage):
            lin = 0
            for a in range(len(grid)):
                lin = lin * grid[a] + pl.program_id(a)

            @pl.when(lin == step)
            def _():
                for i in range(nc):
                    stage(csrc[i], cout[i], sems[3 * i:3 * i + 3], comm[i][1])

        if nc:
            at(0, _comm_start)

        body(*ins, *outs, *scr[:nscr])

        if nc:
            at((3 * (steps - 1)) // 4, _comm_pass_on)
            at(steps - 1, _comm_finish)

    res = pl.pallas_call(
        wrapped, name=name, grid=grid,
        in_specs=[*in_specs, *[ANY_SPEC] * nc], out_specs=[*out_specs, *[ANY_SPEC] * nc],
        out_shape=[*out_shape, *[_comm_out_shape(*c) for c in comm]],
        scratch_shapes=[*scratch_shapes, *[s for _ in comm for s in _comm_sems()]],
        compiler_params=_cparams())(*args, *[c[0] for c in comm])
    return res[:nout], res[nout:]


def _rmsnorm_bf16(x, g, name, tm=1024, comm=()):
    S, dm = x.shape
    tm = min(tm, S)

    def body(x_ref, g_ref, xn_ref):
        xv = x_ref[...]
        r = lax.rsqrt(jnp.mean(xv * xv, axis=-1, keepdims=True) + EPS)
        xn_ref[...] = ((xv * r) * g_ref[...]).astype(bf16)

    row = pl.BlockSpec((tm, dm), lambda i: (i, 0))
    return _hosted_call(
        body, name=name, grid=(S // tm,), in_specs=[row, pl.BlockSpec((1, dm), lambda i: (0, 0))],
        out_specs=[row], out_shape=[jax.ShapeDtypeStruct((S, dm), bf16)], args=(x, g), comm=comm)


def _matmul_t(a, wt, name, tm=2048, tn=512, row_block=lambda j: j, comm=()):
    S, dm = a.shape
    n = wt.shape[0]
    tm = min(tm, S)

    def body(a_ref, w_ref, o_ref):
        o_ref[...] = lax.dot_general(a_ref[...], w_ref[...], (((1,), (1,)), ((), ())),
                                     preferred_element_type=f32).astype(bf16)

    return _hosted_call(
        body, name=name, grid=(S // tm, n // tn),
        in_specs=[pl.BlockSpec((tm, dm), lambda i, j: (i, 0)),
                  pl.BlockSpec((tn, dm), lambda i, j: (row_block(j), 0))],
        out_specs=[pl.BlockSpec((tm, tn), lambda i, j: (i, j))],
        out_shape=[jax.ShapeDtypeStruct((S, n), bf16)], args=(a, wt), comm=comm)


def _norm_matmul(x, g, wt, name, tm=1024, tn=1408, row_block=lambda j: j, comm=()):
    S, dm = x.shape
    n = wt.shape[0]
    tm = min(tm, S)

    def body(x_ref, g_ref, w_ref, xn_ref, o_ref):
        @pl.when(pl.program_id(1) == 0)
        def _():
            xv = x_ref[...]
            r = lax.rsqrt(jnp.mean(xv * xv, axis=-1, keepdims=True) + EPS)
            xn_ref[...] = ((xv * r) * g_ref[...]).astype(bf16)

        o_ref[...] = lax.dot_general(xn_ref[...], w_ref[...], (((1,), (1,)), ((), ())),
                                     preferred_element_type=f32).astype(bf16)

    return _hosted_call(
        body, name=name, grid=(S // tm, n // tn),
        in_specs=[pl.BlockSpec((tm, dm), lambda i, j: (i, 0)),
                  pl.BlockSpec((1, dm), lambda i, j: (0, 0)),
                  pl.BlockSpec((tn, dm), lambda i, j: (row_block(j), 0))],
        out_specs=[pl.BlockSpec((tm, dm), lambda i, j: (i, 0)),
                   pl.BlockSpec((tm, tn), lambda i, j: (i, j))],
        out_shape=[jax.ShapeDtypeStruct((S, dm), bf16), jax.ShapeDtypeStruct((S, n), bf16)],
        args=(x, g, wt), comm=comm)


def _mm_tn(a, b, name, tk, tn, tmc=2048, into=None, row=0, out_rows=None):
    m, ka = a.shape
    n = b.shape[1]
    tmc = min(tmc, m)
    nk = m // tmc

    def body(a_ref, b_ref, *rest):
        o_ref, acc_ref = rest[-2:]
        k = pl.program_id(2)
        part = lax.dot_general(a_ref[...], b_ref[...], (((0,), (0,)), ((), ())), preferred_element_type=f32)

        @pl.when(k == 0)
        def _():
            acc_ref[...] = part

        @pl.when(k > 0)
        def _():
            acc_ref[...] += part

        @pl.when(k == nk - 1)
        def _():
            o_ref[...] = acc_ref[...].astype(bf16)

    in_specs = [pl.BlockSpec((tmc, tk), lambda i, j, k: (k, i)), pl.BlockSpec((tmc, tn), lambda i, j, k: (k, j))]
    if into is None:
        return pl.pallas_call(
            body, name=name, grid=(ka // tk, n // tn, nk), in_specs=in_specs,
            out_specs=pl.BlockSpec((tk, tn), lambda i, j, k: (i + row, j)),
            out_shape=jax.ShapeDtypeStruct((out_rows or ka, n), bf16),
            scratch_shapes=[pltpu.VMEM((tk, tn), f32)],
            compiler_params=_cparams())(a, b)
    return pl.pallas_call(
        body, name=name, grid=(ka // tk, n // tn, nk), in_specs=[*in_specs, ANY_SPEC],
        out_specs=pl.BlockSpec((tk, tn), lambda i, j, k: (i + row, j)),
        out_shape=jax.ShapeDtypeStruct(into.shape, into.dtype),
        scratch_shapes=[pltpu.VMEM((tk, tn), f32)], input_output_aliases={2: 0},
        compiler_params=_cparams())(a, b, into)


HALO = 16


def _rows_at(ext, o, tc):
    if o == 0:
        return ext[HALO:HALO + tc]
    return pltpu.roll(ext, (-o) % ext.shape[0], 0)[HALO:HALO + tc]


def _halo_specs(tc, S, width, col):
    per = tc // HALO
    last = S // HALO - 1
    return (pl.BlockSpec((tc, width), lambda i: (i, col)),
            pl.BlockSpec((HALO, width), lambda i: (jnp.maximum(i * per - 1, 0), col)),
            pl.BlockSpec((HALO, width), lambda i: (jnp.minimum((i + 1) * per, last), col)))


def _extended(cur_ref, prev_ref, next_ref, i, nsteps):
    prev = jnp.where(i > 0, prev_ref[...].astype(f32), 0.0)
    nxt = jnp.where(i < nsteps - 1, next_ref[...].astype(f32), 0.0)
    return jnp.concatenate([prev, cur_ref[...].astype(f32), nxt], axis=0)


def _conv_fwd(proj, cw, cb, tc=1024):
    S = proj.shape[0]
    tc = min(tc, S)
    nsteps = S // tc

    def body(cur_ref, prev_ref, next_ref, w_ref, b_ref, o_ref):
        ext = _extended(cur_ref, prev_ref, next_ref, pl.program_id(0), nsteps)
        acc = _rows_at(ext, -2, tc) * w_ref[0:1, :]
        for k in range(1, 4):
            acc = acc + _rows_at(ext, k - 2, tc) * w_ref[k:k + 1, :]
        o_ref[...] = acc + b_ref[...]

    return pl.pallas_call(
        body, name="conv_fwd", grid=(nsteps,),
        in_specs=[*_halo_specs(tc, S, D, 0),
                  pl.BlockSpec((4, D), lambda i: (0, 0)), pl.BlockSpec((1, D), lambda i: (0, 0))],
        out_specs=pl.BlockSpec((tc, D), lambda i: (i, 0)),
        out_shape=jax.ShapeDtypeStruct((S, D), f32),
        compiler_params=_cparams())(proj, proj, proj, cw, cb)


def _conv_bwd(duc_f, duc_b, proj, cw, tc=1024, comm=()):
    S = proj.shape[0]
    tc = min(tc, S)
    nsteps = S // tc

    def body(fc, fp, fn, bc, bp, bn, uc_, up, un, w_ref, du_ref, dw_ref, db_ref):
        i = pl.program_id(0)

        @pl.when(i == 0)
        def _():
            dw_ref[...] = jnp.zeros_like(dw_ref)
            db_ref[...] = jnp.zeros_like(db_ref)

        dext = _extended(fc, fp, fn, i, nsteps) + _extended(bc, bp, bn, i, nsteps)
        uext = _extended(uc_, up, un, i, nsteps)
        d = dext[HALO:HALO + tc]
        acc = _rows_at(dext, 2, tc) * w_ref[0:1, :]
        for k in range(1, 4):
            acc = acc + _rows_at(dext, 2 - k, tc) * w_ref[k:k + 1, :]
        du_ref[...] = acc.astype(bf16)
        wrow = lax.broadcasted_iota(jnp.int32, (4, D), 0)
        for k in range(4):
            dw_ref[...] += jnp.where(wrow == k, jnp.sum(d * _rows_at(uext, k - 2, tc), axis=0, keepdims=True), 0.0)
        db_ref[...] += jnp.sum(d, axis=0, keepdims=True)

    return _hosted_call(
        body, name="conv_bwd", grid=(nsteps,),
        in_specs=[*_halo_specs(tc, S, D, 0), *_halo_specs(tc, S, D, 0), *_halo_specs(tc, S, D, 0),
                  pl.BlockSpec((4, D), lambda i: (0, 0))],
        out_specs=[pl.BlockSpec((tc, D), lambda i: (i, 0)),
                   pl.BlockSpec((4, D), lambda i: (0, 0)), pl.BlockSpec((1, D), lambda i: (0, 0))],
        out_shape=[jax.ShapeDtypeStruct((S, D), bf16), jax.ShapeDtypeStruct((4, D), f32),
                   jax.ShapeDtypeStruct((1, D), f32)],
        args=(duc_f, duc_f, duc_f, duc_b, duc_b, duc_b, proj, proj, proj, cw), comm=comm)


def _scan_scratch(tc):
    halves = [pltpu.VMEM((tc, 128), f32) for _ in range(2 * (LRU_GW // 128))]
    return [*halves, pltpu.VMEM((tc // 8, LRU_GW), f32), pltpu.VMEM((tc // 8, LRU_GW), f32)]


def _log_scan(a, b, row, n, reverse, steps):
    for s in steps:
        shift = a.shape[0] - s if reverse else s
        keep = (row < n - s) if reverse else (row >= s)
        a_sh = pltpu.roll(a, shift, 0)
        b_sh = pltpu.roll(b, shift, 0)
        b = jnp.where(keep, a * b_sh + b, b)
        a = jnp.where(keep, a * a_sh, a)
    return a, b


def _scan_chunk(a, b, carry, reverse, *scratch):
    tc, w = a.shape
    ng = tc // 8
    nl = w // 128
    sa_refs, sb_refs, sc_ref, st_ref = scratch[:nl], scratch[nl:2 * nl], scratch[2 * nl], scratch[2 * nl + 1]
    sub = lax.broadcasted_iota(jnp.int32, (8, w), 0)
    ag, bg = [], []
    for k in range(ng):
        ak, bk = _log_scan(a[8 * k:8 * k + 8], b[8 * k:8 * k + 8], sub, 8, reverse, (1, 2, 4))
        ag.append(ak)
        bg.append(bk)
    a = jnp.concatenate(ag, axis=0)
    b = jnp.concatenate(bg, axis=0)
    edge = 0 if reverse else 7
    for i in range(nl):
        sa_refs[i][...] = a[:, 128 * i:128 * (i + 1)]
        sb_refs[i][...] = b[:, 128 * i:128 * (i + 1)]
    ta = jnp.concatenate([r[pl.ds(edge, ng, stride=8), :] for r in sa_refs], axis=1)
    tb = jnp.concatenate([r[pl.ds(edge, ng, stride=8), :] for r in sb_refs], axis=1)
    grow = lax.broadcasted_iota(jnp.int32, (ng, w), 0)
    ta, tb = _log_scan(ta, tb, grow, ng, reverse, [1 << i for i in range(ng.bit_length() - 1)])
    state = tb + ta * carry
    st_ref[...] = state
    if reverse:
        sc_ref[...] = jnp.where(grow == ng - 1, carry, pltpu.roll(state, ng - 1, 0))
    else:
        sc_ref[...] = jnp.where(grow == 0, carry, pltpu.roll(state, 1, 0))
    h = jnp.concatenate([bg[k] + ag[k] * sc_ref[k:k + 1, :] for k in range(ng)], axis=0)
    return h, (st_ref[0:1, :] if reverse else st_ref[ng - 1:ng, :])


def _lru_gates(uc, w, p_ref):
    pre = jnp.dot(uc.astype(bf16), w, preferred_element_type=f32)
    r = _sigmoid(pre[:, :LRU_GW] + p_ref[0, 1:2, :])
    gi = _sigmoid(pre[:, LRU_GW:] + p_ref[0, 2:3, :])
    sp = _softplus(-p_ref[0, 0:1, :])
    log_a = -RGLRU_C * r * sp
    a = jnp.exp(log_a)
    x = 2.0 * log_a
    series = -x * (1.0 + x * (0.5 + x * (1.0 / 6 + x * (1.0 / 24))))
    beta = jnp.sqrt(jnp.maximum(jnp.where(x > -0.0625, series, 1.0 - a * a), 0.0))
    return r, gi, sp, a, beta


def _lru_fwd(uc, wg, lp, reverse, comm=()):
    S = uc.shape[0]
    tc = LRU_CHUNK
    rows = min(LRU_ROWS, S)
    nsub = rows // tc
    nblk = S // rows
    d = 1 if reverse else 0

    def bidx(c):
        return nblk - 1 - c if reverse else c

    def body(uc_ref, w_ref, p_ref, h_ref, carry_ref, *scan_scratch):
        @pl.when(pl.program_id(1) == 0)
        def _():
            carry_ref[...] = jnp.zeros_like(carry_ref)

        carry = carry_ref[...]
        for j in (reversed(range(nsub)) if reverse else range(nsub)):
            sl = slice(j * tc, (j + 1) * tc)
            ucv = uc_ref[sl, :]
            _, gi, _, a, beta = _lru_gates(ucv, w_ref[0], p_ref)
            h, carry = _scan_chunk(a, beta * (gi * ucv), carry, reverse, *scan_scratch)
            h_ref[sl, :] = h.astype(bf16)
        carry_ref[...] = carry

    return _hosted_call(
        body, name="lru_fwd_rev" if reverse else "lru_fwd", grid=(LRU_GROUPS, nblk),
        in_specs=[pl.BlockSpec((rows, LRU_GW), lambda g, c: (bidx(c), g)),
                  pl.BlockSpec((1, LRU_GW, 2 * LRU_GW), lambda g, c: (g, 0, d)),
                  pl.BlockSpec((1, 8, LRU_GW), lambda g, c: (d, 0, g))],
        out_specs=[pl.BlockSpec((rows, LRU_GW), lambda g, c: (bidx(c), g))],
        out_shape=[jax.ShapeDtypeStruct((S, D), bf16)],
        scratch_shapes=[pltpu.VMEM((1, LRU_GW), f32), *_scan_scratch(tc)],
        args=(uc, wg, lp), comm=comm)


def _lru_bwd(uc, dh, h, wg, lp, reverse, comm=()):
    S = uc.shape[0]
    tc = LRU_CHUNK_BWD
    rows = min(LRU_ROWS, S)
    nsub = rows // tc
    nblk = S // rows
    d = 1 if reverse else 0
    per = rows // HALO
    last8 = S // HALO - 1

    def bidx(c):
        return c if reverse else nblk - 1 - c

    def halo_idx(c):
        if reverse:
            return jnp.minimum((bidx(c) + 1) * per, last8)
        return jnp.maximum(bidx(c) * per - 1, 0)

    def body(uc_ref, dh_ref, h_ref, halo_ref, w_ref, p_ref, duc_ref, dw_ref, dp_ref, carry_ref, tmp_ref,
             *scan_scratch):
        c = pl.program_id(1)
        bi = bidx(c)

        @pl.when(c == 0)
        def _():
            carry_ref[...] = jnp.zeros_like(carry_ref)
            dw_ref[...] = jnp.zeros_like(dw_ref)
            dp_ref[...] = jnp.zeros_like(dp_ref)

        row = lax.broadcasted_iota(jnp.int32, (tc, LRU_GW), 0)
        carry = carry_ref[...]
        dw = jnp.zeros((LRU_GW, 2 * LRU_GW), f32)
        dsp = jnp.zeros((1, LRU_GW), f32)
        dba = jnp.zeros((1, LRU_GW), f32)
        dbx = jnp.zeros((1, LRU_GW), f32)
        for j in (range(nsub) if reverse else reversed(range(nsub))):
            sl = slice(j * tc, (j + 1) * tc)
            ucv = uc_ref[sl, :]
            ucb = ucv.astype(bf16)
            r, gi, sp, a, beta = _lru_gates(ucv, w_ref[0], p_ref)
            hv = h_ref[sl, :].astype(f32)
            dhv = dh_ref[sl, :].astype(f32)
            if reverse:
                alpha = jnp.where(row == 0, 1.0, pltpu.roll(a, 1, 0))
                gsc, _ = _scan_chunk(alpha, dhv, carry, False, *scan_scratch)
                if j < nsub - 1:
                    edge = h_ref[(j + 1) * tc:(j + 1) * tc + HALO, :].astype(f32)[0:1, :]
                else:
                    edge = jnp.where(bi < nblk - 1, halo_ref[...].astype(f32)[0:1, :], 0.0)
                h_nb = jnp.where(row == tc - 1, edge, pltpu.roll(hv, tc - 1, 0))
            else:
                alpha = jnp.where(row == tc - 1, 1.0, pltpu.roll(a, tc - 1, 0))
                gsc, _ = _scan_chunk(alpha, dhv, carry, True, *scan_scratch)
                if j > 0:
                    edge = h_ref[j * tc - HALO:j * tc, :].astype(f32)[HALO - 1:HALO, :]
                else:
                    edge = jnp.where(bi > 0, halo_ref[...].astype(f32)[HALO - 1:HALO, :], 0.0)
                h_nb = jnp.where(row == 0, edge, pltpu.roll(hv, 1, 0))
            tmp_ref[...] = a * gsc
            carry = tmp_ref[tc - 1:tc, :] if reverse else tmp_ref[0:1, :]

            da = gsc * h_nb
            dbeta = gsc * (gi * ucv)
            dl = da * a - dbeta * (a * a) / beta
            dr = dl * (-RGLRU_C * sp)
            dsp = dsp + jnp.sum(dl * (-RGLRU_C * r), axis=0, keepdims=True)
            dgi = gsc * beta * ucv
            dpre_r = dr * r * (1.0 - r)
            dpre_i = dgi * gi * (1.0 - gi)
            dba = dba + jnp.sum(dpre_r, axis=0, keepdims=True)
            dbx = dbx + jnp.sum(dpre_i, axis=0, keepdims=True)
            dpre = jnp.concatenate([dpre_r, dpre_i], axis=1).astype(bf16)
            back = lax.dot_general(dpre, w_ref[0], (((1,), (1,)), ((), ())), preferred_element_type=f32)
            duc_ref[sl, :] = (gsc * beta * gi + back).astype(bf16)
            dw = dw + lax.dot_general(ucb, dpre, (((0,), (0,)), ((), ())), preferred_element_type=f32)
        carry_ref[...] = carry
        dw_ref[0] += dw
        dlam = -dsp / (1.0 + jnp.exp(p_ref[0, 0:1, :]))
        prow = lax.broadcasted_iota(jnp.int32, (8, LRU_GW), 0)
        dp_ref[...] += (jnp.where(prow == 0, dlam, 0.0) + jnp.where(prow == 1, dba, 0.0)
                        + jnp.where(prow == 2, dbx, 0.0))

    chunk = pl.BlockSpec((rows, LRU_GW), lambda g, c: (bidx(c), g))
    return _hosted_call(
        body, name="lru_bwd_rev" if reverse else "lru_bwd", grid=(LRU_GROUPS, nblk),
        in_specs=[chunk, chunk, chunk,
                  pl.BlockSpec((HALO, LRU_GW), lambda g, c: (halo_idx(c), g)),
                  pl.BlockSpec((1, LRU_GW, 2 * LRU_GW), lambda g, c: (g, 0, d)),
                  pl.BlockSpec((1, 8, LRU_GW), lambda g, c: (d, 0, g))],
        out_specs=[chunk,
                   pl.BlockSpec((1, LRU_GW, 2 * LRU_GW), lambda g, c: (g, 0, 0)),
                   pl.BlockSpec((8, LRU_GW), lambda g, c: (0, g))],
        out_shape=[jax.ShapeDtypeStruct((S, D), bf16),
                   jax.ShapeDtypeStruct((LRU_GROUPS, LRU_GW, 2 * LRU_GW), f32),
                   jax.ShapeDtypeStruct((8, D), f32)],
        scratch_shapes=[pltpu.VMEM((1, LRU_GW), f32), pltpu.VMEM((tc, LRU_GW), f32), *_scan_scratch(tc)],
        args=(uc, dh, h, h, wg, lp), comm=comm)


def _slope(h):
    return 2.0 ** (-8.0 * (h + 1.0) / N_HEADS)


ATT_QB = 4


def _kv_specs(nb, col):
    return [pl.BlockSpec((BLK, N_KV * HEAD_DIM), lambda n: (jnp.maximum(ATT_QB * n - 1, 0), col)),
            pl.BlockSpec((ATT_QB * BLK, N_KV * HEAD_DIM), lambda n: (n, col)),
            pl.BlockSpec((BLK, N_KV * HEAD_DIM), lambda n: (jnp.minimum(ATT_QB * (n + 1), nb - 1), col))]


def _key_blocks(prev_ref, cur_ref, next_ref):
    return [prev_ref[...], *[cur_ref[BLK * s:BLK * (s + 1), :] for s in range(ATT_QB)], next_ref[...]]


def _dup_windows(r0, r1, r2):
    left = lax.broadcasted_iota(jnp.int32, (3 * BLK, 128), 1) < HEAD_DIM
    win = jnp.concatenate([r0, r1, r2], axis=0)
    out = []
    for i in range(N_KV // 2):
        t = win[:, i * 128:(i + 1) * 128]
        r = pltpu.roll(t, HEAD_DIM, 1)
        out += [jnp.where(left, t, r).astype(bf16), jnp.where(left, r, t).astype(bf16)]
    return out


def _attn_bias_init(bias_ref):
    k_loc = lax.broadcasted_iota(jnp.int32, (3 * BLK, BLK), 0)
    q_loc = lax.broadcasted_iota(jnp.int32, (3 * BLK, BLK), 1)
    adist = jnp.abs(q_loc + BLK - k_loc)
    adf = adist.astype(f32)
    for e in range(3):
        ok = adist <= WINDOW
        if e == 0:
            ok = ok & (k_loc >= BLK)
        if e == 2:
            ok = ok & (k_loc < 2 * BLK)
        for kv in range(N_KV):
            bias_ref[e, kv] = jnp.concatenate(
                [jnp.where(ok, (-_slope(4 * kv + j)) * adf, NEG_INF) for j in range(4)], axis=1)


def _stack_heads(ref, sub, kv, scale):
    left = lax.broadcasted_iota(jnp.int32, (BLK, 128), 1) < HEAD_DIM
    rows = []
    for pp in range(2):
        t = ref[BLK * sub:BLK * (sub + 1), (2 * kv + pp) * 128:(2 * kv + pp + 1) * 128]
        if scale != 1.0:
            t = t * scale
        zero = jnp.zeros_like(t)
        rows += [jnp.where(left, t, zero).astype(bf16), jnp.where(left, zero, t).astype(bf16)]
    return jnp.concatenate(rows, axis=0)


def _attn_softmax(qs, k2, bias, sink_ref, kv, stats=None):
    sink = jnp.concatenate([jnp.full((1, BLK), sink_ref[0, 4 * kv + j], f32) for j in range(4)], axis=1)
    s = lax.dot_general(k2, qs, (((1,), (1,)), ((), ())), preferred_element_type=f32) + bias
    m = jnp.maximum(jnp.max(s, axis=0, keepdims=True), sink) if stats is None else stats[0]
    p = jnp.exp(s - m)
    ps = jnp.exp(sink - m)
    inv = 1.0 / (jnp.sum(p, axis=0, keepdims=True) + ps) if stats is None else stats[1]
    return p, ps, m, inv


def _pair_tiles(t):
    return [jnp.concatenate([t[:HEAD_DIM, 256 * pp:256 * pp + 128],
                             t[HEAD_DIM:, 256 * pp + 128:256 * pp + 256]], axis=0).T for pp in range(2)]


def _attn_fwd(proj, sink, comm=()):
    S = proj.shape[0]
    nb = S // BLK
    assert nb >= 2 and nb % ATT_QB == 0

    def body(q_ref, k0, k1, k2_, v0, v1, v2_, sink_ref, o_ref, st_ref, bias_ref):
        n = pl.program_id(0)

        @pl.when(n == 0)
        def _():
            _attn_bias_init(bias_ref)

        kb = _key_blocks(k0, k1, k2_)
        vb = _key_blocks(v0, v1, v2_)
        for sub in range(ATT_QB):
            blk = ATT_QB * n + sub
            e = jnp.where(blk == 0, 0, jnp.where(blk == nb - 1, 2, 1))
            kk = _dup_windows(*kb[sub:sub + 3])
            vv = _dup_windows(*vb[sub:sub + 3])
            tiles = []
            for kv in range(N_KV):
                qs = _stack_heads(q_ref, sub, kv, HEAD_DIM ** -0.5)
                p, _, m, inv = _attn_softmax(qs, kk[kv], bias_ref[e, kv], sink_ref, kv)
                st_ref[sub, kv:kv + 1, :] = m
                st_ref[sub, N_KV + kv:N_KV + kv + 1, :] = inv
                ot = lax.dot_general(vv[kv], p.astype(bf16), (((0,), (0,)), ((), ())), preferred_element_type=f32)
                tiles += _pair_tiles(ot * inv)
            o_ref[BLK * sub:BLK * (sub + 1), :] = jnp.concatenate(tiles, axis=1).astype(bf16)

    return _hosted_call(
        body, name="attn_fwd", grid=(nb // ATT_QB,),
        in_specs=[pl.BlockSpec((ATT_QB * BLK, D), lambda n: (n, C_Q // D)),
                  *_kv_specs(nb, C_K // (N_KV * HEAD_DIM)), *_kv_specs(nb, C_V // (N_KV * HEAD_DIM)),
                  pl.BlockSpec(memory_space=pltpu.SMEM)],
        out_specs=[pl.BlockSpec((ATT_QB * BLK, D), lambda n: (n, 0)),
                   pl.BlockSpec((ATT_QB, 2 * N_KV, 4 * BLK), lambda n: (n, 0, 0))],
        out_shape=[jax.ShapeDtypeStruct((S, D), bf16), jax.ShapeDtypeStruct((nb, 2 * N_KV, 4 * BLK), f32)],
        scratch_shapes=[pltpu.VMEM((3, N_KV, 3 * BLK, 4 * BLK), f32)],
        args=(proj, proj, proj, proj, proj, proj, proj, sink), comm=comm)


def _attn_bwd(proj, sink, dyb, stats, comm=()):
    S = proj.shape[0]
    nb = S // BLK
    assert nb >= 2 and nb % ATT_QB == 0
    nsteps = nb // ATT_QB
    kvw = N_KV * HEAD_DIM

    def body(q_ref, k0, k1, k2_, v0, v1, v2_, sink_ref, do_ref, st_ref, dq_ref, dkv_out, ds_ref,
             bias_ref, dk_ref, dv_ref, dsk_ref, dkv_ref):
        n = pl.program_id(0)

        @pl.when(n == 0)
        def _():
            _attn_bias_init(bias_ref)
            dk_ref[...] = jnp.zeros_like(dk_ref)
            dv_ref[...] = jnp.zeros_like(dv_ref)
            dsk_ref[...] = jnp.zeros_like(dsk_ref)

        kb = _key_blocks(k0, k1, k2_)
        vb = _key_blocks(v0, v1, v2_)
        left3 = lax.broadcasted_iota(jnp.int32, (3 * BLK, 128), 1) < HEAD_DIM
        for sub in range(ATT_QB):
            blk = ATT_QB * n + sub
            e = jnp.where(blk == 0, 0, jnp.where(blk == nb - 1, 2, 1))
            kk = _dup_windows(*kb[sub:sub + 3])
            vv = _dup_windows(*vb[sub:sub + 3])
            start = pl.multiple_of(blk * BLK, BLK)
            dq_tiles, dks, dvs = [], [], []
            for kv in range(N_KV):
                qs = _stack_heads(q_ref, sub, kv, HEAD_DIM ** -0.5)
                dos = _stack_heads(do_ref, sub, kv, 1.0)
                stats = (st_ref[sub, kv:kv + 1, :], st_ref[sub, N_KV + kv:N_KV + kv + 1, :])
                p, ps, _, inv = _attn_softmax(qs, kk[kv], bias_ref[e, kv], sink_ref, kv, stats)
                pn = p * inv
                dp = lax.dot_general(vv[kv], dos, (((1,), (1,)), ((), ())), preferred_element_type=f32)
                delta = jnp.sum(pn * dp, axis=0, keepdims=True)
                dsc = (pn * (dp - delta)).astype(bf16)
                dsk_ref[kv:kv + 1, :] += delta * (ps * inv)
                dqt = lax.dot_general(kk[kv], dsc, (((0,), (0,)), ((), ())), preferred_element_type=f32)
                dq_tiles += _pair_tiles(dqt * (HEAD_DIM ** -0.5))
                dk = jnp.dot(dsc, qs, preferred_element_type=f32)
                dv = jnp.dot(pn.astype(bf16), dos, preferred_element_type=f32)
                dks.append(dk + pltpu.roll(dk, HEAD_DIM, 1))
                dvs.append(dv + pltpu.roll(dv, HEAD_DIM, 1))
            for jp in range(N_KV // 2):
                cols = slice(jp * 128, (jp + 1) * 128)
                dk_ref[pl.ds(start, 3 * BLK), cols] += jnp.where(left3, dks[2 * jp], dks[2 * jp + 1])
                dv_ref[pl.ds(start, 3 * BLK), cols] += jnp.where(left3, dvs[2 * jp], dvs[2 * jp + 1])
            dq_ref[BLK * sub:BLK * (sub + 1), :] = jnp.concatenate(dq_tiles, axis=1).astype(bf16)

        @pl.when(n == nsteps - 1)
        def _():
            rows = min(S, 512)
            for c in range(S // rows):
                dkv_ref[rows * c:rows * (c + 1), :kvw] = dk_ref[BLK + rows * c:BLK + rows * (c + 1), :].astype(bf16)
                dkv_ref[rows * c:rows * (c + 1), kvw:] = dv_ref[BLK + rows * c:BLK + rows * (c + 1), :].astype(bf16)
            pltpu.sync_copy(dkv_ref, dkv_out)
            lane = lax.broadcasted_iota(jnp.int32, (1, 128), 1)
            dsink = jnp.zeros((1, 128), f32)
            for h in range(N_HEADS):
                part = dsk_ref[h // 4:h // 4 + 1, (h % 4) * BLK:(h % 4 + 1) * BLK]
                dsink = dsink + jnp.where(lane == h, -jnp.sum(part), 0.0)
            ds_ref[...] = dsink

    acc = jax.ShapeDtypeStruct((S + 2 * BLK, N_KV * HEAD_DIM), f32)
    return _hosted_call(
        body, name="attn_bwd", grid=(nsteps,),
        in_specs=[pl.BlockSpec((ATT_QB * BLK, D), lambda n: (n, C_Q // D)),
                  *_kv_specs(nb, C_K // (N_KV * HEAD_DIM)), *_kv_specs(nb, C_V // (N_KV * HEAD_DIM)),
                  pl.BlockSpec(memory_space=pltpu.SMEM),
                  pl.BlockSpec((ATT_QB * BLK, D), lambda n: (n, 0)),
                  pl.BlockSpec((ATT_QB, 2 * N_KV, 4 * BLK), lambda n: (n, 0, 0))],
        out_specs=[pl.BlockSpec((ATT_QB * BLK, D), lambda n: (n, 0)), ANY_SPEC,
                   pl.BlockSpec((1, 128), lambda n: (0, 0))],
        out_shape=[jax.ShapeDtypeStruct((S, D), bf16), jax.ShapeDtypeStruct((S, 2 * kvw), bf16),
                   jax.ShapeDtypeStruct((1, 128), f32)],
        scratch_shapes=[pltpu.VMEM((3, N_KV, 3 * BLK, 4 * BLK), f32), pltpu.VMEM(acc.shape, f32),
                        pltpu.VMEM(acc.shape, f32), pltpu.VMEM((8, 4 * BLK), f32), pltpu.VMEM((S, 2 * kvw), bf16)],
        args=(proj, proj, proj, proj, proj, proj, proj, sink, dyb, stats), comm=comm)


def _merge_parts(hf, hb, g, z0, z1, yb, bg):
    g0 = _sigmoid(z0 + bg[:, :D].astype(bf16))
    g1 = _sigmoid(z1 + bg[:, D:].astype(bf16))
    gelu, dgelu = _gelu_and_grad(g)
    hs = hf + hb
    ya = hs * gelu
    return g0, g1, gelu, dgelu, hs, ya


def _merge_outproj(x, hf, hb, proj, yb, bg, w_out, tm=1024):
    S = x.shape[0]
    tm = min(tm, S)

    def body(x_ref, hf_ref, hb_ref, g_ref, z0_ref, z1_ref, yb_ref, bg_ref, w_ref, mg_ref, x1_ref):
        ybv = yb_ref[...]
        g0, g1, _, _, _, ya = _merge_parts(hf_ref[...], hb_ref[...], g_ref[...], z0_ref[...], z1_ref[...],
                                           ybv, bg_ref[...])
        mg = g0 * ya + g1 * ybv
        mg_ref[...] = mg
        x1_ref[...] = x_ref[...] + jnp.dot(mg, w_ref[...], preferred_element_type=f32)

    row = pl.BlockSpec((tm, D), lambda i: (i, 0))
    return pl.pallas_call(
        body, name="merge_outproj", grid=(S // tm,),
        in_specs=[row, row, row,
                  pl.BlockSpec((tm, D), lambda i: (i, C_G // D)),
                  pl.BlockSpec((tm, D), lambda i: (i, C_Z0 // D)),
                  pl.BlockSpec((tm, D), lambda i: (i, C_Z1 // D)),
                  row, pl.BlockSpec((1, 2 * D), lambda i: (0, 0)), pl.BlockSpec((D, D), lambda i: (0, 0))],
        out_specs=[row, row],
        out_shape=[jax.ShapeDtypeStruct((S, D), bf16), jax.ShapeDtypeStruct((S, D), f32)],
        compiler_params=_cparams())(x, hf, hb, proj, proj, proj, yb, bg, w_out)


def _ffn_out_loss(gu, x1, w_fo, g3, tgt, tm=256):
    S = x1.shape[0]
    tm = min(tm, S)

    def body(gt_ref, up_ref, x1_ref, w_ref, g_ref, t_ref, ff_ref, dx_ref, dxb_ref, loss_ref, dg_ref,
             dgt_ref, dup_ref):
        @pl.when(pl.program_id(0) == 0)
        def _():
            loss_ref[...] = jnp.zeros_like(loss_ref)
            dg_ref[...] = jnp.zeros_like(dg_ref)

        gt = gt_ref[...]
        up = up_ref[...]
        sg = _sigmoid(gt)
        silu = gt * sg
        ff = silu * up
        ff_ref[...] = ff
        x2 = x1_ref[...] + jnp.dot(ff, w_ref[...], preferred_element_type=f32)
        gv = g_ref[...]
        r = lax.rsqrt(jnp.mean(x2 * x2, axis=-1, keepdims=True) + EPS)
        xh = x2 * r
        diff = xh * gv - t_ref[...]
        loss_ref[...] += (0.5 / D) * jnp.sum(diff * diff)
        dy = diff * (1.0 / D)
        dg_ref[...] += jnp.sum(dy * xh, axis=0, keepdims=True)
        dxh = dy * gv
        dx = r * (dxh - xh * jnp.mean(dxh * xh, axis=-1, keepdims=True))
        dx_ref[...] = dx
        dxb = dx.astype(bf16)
        dxb_ref[...] = dxb
        dff = lax.dot_general(dxb, w_ref[...], (((1,), (1,)), ((), ())), preferred_element_type=f32)
        dup_ref[...] = (dff * silu.astype(f32)).astype(bf16)
        dgt_ref[...] = (dff * (up * (sg * (1.0 + gt * (1.0 - sg)))).astype(f32)).astype(bf16)

    row = pl.BlockSpec((tm, D), lambda i: (i, 0))
    vec = pl.BlockSpec((1, D), lambda i: (0, 0))
    wide = pl.BlockSpec((tm, D_FF), lambda i: (i, 0))
    wide_shape = jax.ShapeDtypeStruct((S, D_FF), bf16)
    return pl.pallas_call(
        body, name="ffn_out_loss", grid=(S // tm,),
        in_specs=[wide, pl.BlockSpec((tm, D_FF), lambda i: (i, 1)),
                  row, pl.BlockSpec((D_FF, D), lambda i: (0, 0)), vec, row],
        out_specs=[wide, row, row, pl.BlockSpec((1, 128), lambda i: (0, 0)), vec, wide, wide],
        out_shape=[wide_shape, jax.ShapeDtypeStruct((S, D), f32), jax.ShapeDtypeStruct((S, D), bf16),
                   jax.ShapeDtypeStruct((1, 128), f32), jax.ShapeDtypeStruct((1, D), f32), wide_shape, wide_shape],
        compiler_params=_cparams())(gu, gu, x1, w_fo, g3, tgt)


def _proj_bwd(pieces, wt, xres, g, dres, name, tm=512, comm=()):
    S = xres.shape[0]
    tm = min(tm, S)
    np_ = len(pieces)

    def body(*refs):
        p_refs = refs[:np_]
        w_refs = refs[np_:2 * np_]
        x_ref, g_ref, dres_ref, dx_ref, dxb_ref, dg_ref = refs[2 * np_:]

        @pl.when(pl.program_id(0) == 0)
        def _():
            dg_ref[...] = jnp.zeros_like(dg_ref)

        dn = jnp.dot(p_refs[0][...], w_refs[0][...], preferred_element_type=f32)
        for pr, wr in zip(p_refs[1:], w_refs[1:]):
            dn = dn + jnp.dot(pr[...], wr[...], preferred_element_type=f32)
        dxn, dgc = _rms_bwd(dn, x_ref[...], g_ref[...])
        dx = dres_ref[...] + dxn
        dx_ref[...] = dx
        dxb_ref[...] = dx.astype(bf16)
        dg_ref[...] += jnp.sum(dgc, axis=0, keepdims=True)

    row = pl.BlockSpec((tm, D), lambda i: (i, 0))
    vec = pl.BlockSpec((1, D), lambda i: (0, 0))
    return _hosted_call(
        body, name=name, grid=(S // tm,),
        in_specs=[*[pl.BlockSpec((tm, wd), functools.partial(lambda i, cb: (i, cb), cb=acb))
                    for _, acb, _, wd in pieces],
                  *[pl.BlockSpec((wd, D), functools.partial(lambda i, rb: (rb, 0), rb=wrb))
                    for _, _, wrb, wd in pieces],
                  row, vec, row],
        out_specs=[row, row, vec],
        out_shape=[jax.ShapeDtypeStruct((S, D), f32), jax.ShapeDtypeStruct((S, D), bf16),
                   jax.ShapeDtypeStruct((1, D), f32)],
        args=(*[p[0] for p in pieces], *[wt] * np_, xres, g, dres), comm=comm)


def _outproj_bwd(dx1b, w_out, hf, hb, proj, yb, bg, tm=1024):
    S = dx1b.shape[0]
    tm = min(tm, S)

    def body(dx_ref, w_ref, hf_ref, hb_ref, g_ref, z0_ref, z1_ref, yb_ref, bg_ref,
             dh_ref, dg_ref, dz_ref, dyb_ref, dbg_ref):
        @pl.when(pl.program_id(0) == 0)
        def _():
            dbg_ref[...] = jnp.zeros_like(dbg_ref)

        dm = lax.dot_general(dx_ref[...], w_ref[...], (((1,), (1,)), ((), ())), preferred_element_type=f32)
        ybv = yb_ref[...]
        g0, g1, gelu, dgelu, hs, ya = _merge_parts(hf_ref[...], hb_ref[...], g_ref[...], z0_ref[...],
                                                   z1_ref[...], ybv, bg_ref[...])
        dh_ref[...] = (dm * (g0 * gelu).astype(f32)).astype(bf16)
        dg_ref[...] = (dm * (g0 * hs * dgelu).astype(f32)).astype(bf16)
        dyb_ref[...] = (dm * g1.astype(f32)).astype(bf16)
        dz0 = dm * (ya * (g0 * (1.0 - g0))).astype(f32)
        dz1 = dm * (ybv * (g1 * (1.0 - g1))).astype(f32)
        dz = jnp.concatenate([dz0, dz1], axis=1)
        dz_ref[...] = dz.astype(bf16)
        dbg_ref[...] += jnp.sum(dz, axis=0, keepdims=True)

    row = pl.BlockSpec((tm, D), lambda i: (i, 0))
    return pl.pallas_call(
        body, name="outproj_bwd", grid=(S // tm,),
        in_specs=[row, pl.BlockSpec((D, D), lambda i: (0, 0)), row, row,
                  pl.BlockSpec((tm, D), lambda i: (i, C_G // D)),
                  pl.BlockSpec((tm, D), lambda i: (i, C_Z0 // D)),
                  pl.BlockSpec((tm, D), lambda i: (i, C_Z1 // D)),
                  row, pl.BlockSpec((1, 2 * D), lambda i: (0, 0))],
        out_specs=[row, row, pl.BlockSpec((tm, 2 * D), lambda i: (i, 0)), row,
                   pl.BlockSpec((1, 2 * D), lambda i: (0, 0))],
        out_shape=[jax.ShapeDtypeStruct((S, D), bf16), jax.ShapeDtypeStruct((S, D), bf16),
                   jax.ShapeDtypeStruct((S, 2 * D), bf16), jax.ShapeDtypeStruct((S, D), bf16),
                   jax.ShapeDtypeStruct((1, 2 * D), f32)],
        compiler_params=_cparams())(dx1b, w_out, hf, hb, proj, proj, proj, yb, bg)


def _block_diag_groups(w):
    w4 = w.reshape(LRU_GROUPS, 4, LRU_BLOCK, LRU_BLOCK)
    eye = jnp.eye(4, dtype=w.dtype)
    return jnp.einsum("ghij,hk->ghikj", w4, eye).reshape(LRU_GROUPS, LRU_GW, LRU_GW)


def _diag_blocks(dw):
    d5 = dw.reshape(LRU_GROUPS, 4, LRU_BLOCK, 4, LRU_BLOCK)
    return jnp.stack([d5[:, h, :, h, :] for h in range(4)], axis=1).reshape(LRU_HEADS, LRU_BLOCK, LRU_BLOCK)


def _local_step(x, tgt, small, env, before=lambda name: (), after=lambda name, got: None):
    S = x.shape[0]
    g1, g2, g3 = small["norm_mix_g"], small["norm_ffn_g"], small["norm_final_g"]
    bg, cb, sink = small["b_gate"], small["conv_b"], small["attn_sink"]

    def hosted(name, fn, *args, **kw):
        outs, got = fn(*args, comm=tuple(before(name)), **kw)
        after(name, got)
        return outs

    (xn,) = hosted("norm_x", _rmsnorm_bf16, x, g1, "norm_x")
    cw = small["conv_w"]
    wg = jnp.concatenate([_block_diag_groups(small["lru_wa"][0]), _block_diag_groups(small["lru_wx"][0]),
                          _block_diag_groups(small["lru_wa"][1]), _block_diag_groups(small["lru_wx"][1])],
                         axis=2).astype(bf16)
    zeros5 = jnp.zeros((5, D), f32)
    lp = jnp.stack([jnp.concatenate([small["lru_lambda"][d:d + 1], small["lru_ba"][d:d + 1],
                                     small["lru_bx"][d:d + 1], zeros5], axis=0) for d in range(2)])
    (proj,) = hosted("inproj", _matmul_t, xn, env["w_in_t"], "inproj", tm=4096, tn=512,
                     row_block=lambda j: jnp.where(j < 6, j, jnp.where(j < 10, j + 1, 6)))
    uc = _conv_fwd(proj, cw, cb)
    (hf,), _ = _lru_fwd(uc, wg, lp, False)
    (hb,), _ = _lru_fwd(uc, wg, lp, True)
    yb, attn_stats = hosted("attn_fwd", _attn_fwd, proj, sink)
    merged, x1 = _merge_outproj(x, hf, hb, proj, yb, bg, env["w_out"])
    (xn2, gu), _ = _norm_matmul(x1, g2, env["w_fi_t"], "norm_ffn_in", tn=D_FF)
    ff, dx2, dx2b, loss, dg3, dgt, dup = _ffn_out_loss(gu, x1, env["w_fo"], g3, tgt)

    env["dw_fo"] = _mm_tn(ff, dx2b, "dw_ffn_out", tk=1408, tn=1024)
    dx1, dx1b, dg2 = hosted("ffn_in_bwd", _proj_bwd, [(dgt, 0, 0, D_FF), (dup, 0, 1, D_FF)], env["w_fi_t"],
                            x1, g2, dx2, "ffn_in_bwd")
    dw_gate = _mm_tn(dgt, xn2, "dw_ffn_in_gate", tk=1408, tn=1024, out_rows=2 * D_FF)
    env["dw_fi_t"] = _mm_tn(dup, xn2, "dw_ffn_in_up", tk=1408, tn=1024, into=dw_gate, row=D_FF // 1408)
    env["dw_out"] = _mm_tn(merged, dx1b, "dw_out", tk=1024, tn=1024)
    dh, dgl, dz, dyb, dbg = _outproj_bwd(dx1b, env["w_out"], hf, hb, proj, yb, bg)
    dq, dkv, dsink = hosted("attn_bwd", _attn_bwd, proj, sink, dyb, attn_stats)
    duc_f, dwg_f, dp_f = hosted("lru_bwd", _lru_bwd, uc, dh, hf, wg, lp, False)
    (duc_b, dwg_b, dp_b), _ = _lru_bwd(uc, dh, hb, wg, lp, True)
    env["grads_early"] = {
        "loss": loss[:, :1], "b_gate": dbg,
        "lru_lambda": jnp.concatenate([dp_f[0:1], dp_b[0:1]], axis=0),
        "lru_wa": jnp.stack([_diag_blocks(dwg_f[:, :, :LRU_GW]), _diag_blocks(dwg_b[:, :, :LRU_GW])]),
        "lru_ba": jnp.concatenate([dp_f[1:2], dp_b[1:2]], axis=0),
        "lru_wx": jnp.stack([_diag_blocks(dwg_f[:, :, LRU_GW:]), _diag_blocks(dwg_b[:, :, LRU_GW:])]),
        "lru_bx": jnp.concatenate([dp_f[2:3], dp_b[2:3]], axis=0),
        "attn_sink": dsink[:, :N_HEADS], "norm_ffn_g": dg2, "norm_final_g": dg3,
    }
    du, dcw, dcb = hosted("conv_bwd", _conv_bwd, duc_f, duc_b, proj, cw)
    dw_in = _mm_tn(du, xn, "dw_in_u", tk=1024, tn=1024, out_rows=IN_W)
    dw_in = _mm_tn(dgl, xn, "dw_in_g", tk=1024, tn=1024, into=dw_in, row=1)
    dw_in = _mm_tn(dq, xn, "dw_in_q", tk=1024, tn=1024, into=dw_in, row=2)
    dw_in = _mm_tn(dkv, xn, "dw_in_kv", tk=512, tn=1024, into=dw_in, row=3072 // 512)
    env["dw_in_t"] = _mm_tn(dz, xn, "dw_in_z", tk=512, tn=1024, tmc=4096, into=dw_in, row=3584 // 512)
    col_pieces = [(du, 0, 0, D), (dgl, 0, 1, D), (dq, 0, 2, D), (dkv, 0, 3072 // 512, 512),
                  *[(dz, i, 3584 // 512 + i, 512) for i in range(4)]]
    dx, _, dg1 = hosted("inproj_bwd", _proj_bwd, col_pieces, env["w_in_t"], x, g1, dx1, "inproj_bwd")

    grads = dict(env["grads_early"], norm_mix_g=dg1, conv_w=dcw, conv_b=dcb)
    return dx, grads


def _adamw(gparts, w, m, v, name, tr=256):
    n, rows, cols = gparts.shape
    tr = _div_tile(rows, tr)
    c1 = 1.0 - ADAM_B1 ** ADAM_STEP
    c2 = 1.0 - ADAM_B2 ** ADAM_STEP

    def body(g_ref, w_ref, m_ref, v_ref, go_ref, d_ref, mo_ref, vo_ref):
        g = g_ref[0].astype(f32)
        for j in range(1, n):
            g = g + g_ref[j].astype(f32)
        mn = ADAM_B1 * m_ref[0] + (1.0 - ADAM_B1) * g
        vn = ADAM_B2 * v_ref[0] + (1.0 - ADAM_B2) * (g * g)
        m_hat = mn / c1
        v_hat = vn / c2
        go_ref[0] = g
        d_ref[0] = -ADAM_LR * (m_hat / (jnp.sqrt(v_hat) + ADAM_EPS) + ADAM_WD * w_ref[0])
        mo_ref[0] = mn
        vo_ref[0] = vn

    blk = pl.BlockSpec((1, tr, cols), lambda i: (0, i, 0))
    shp = jax.ShapeDtypeStruct((1, rows, cols), f32)
    return pl.pallas_call(
        body, name=name, grid=(rows // tr,),
        in_specs=[pl.BlockSpec((n, tr, cols), lambda i: (0, i, 0)), blk, blk, blk],
        out_specs=[blk, blk, blk, blk], out_shape=[shp, shp, shp, shp],
        compiler_params=_cparams())(gparts, w, m, v)


def _sum_parts(parts, name):
    n, rows, cols = parts.shape

    def body(p_ref, o_ref):
        acc = p_ref[0].astype(f32)
        for j in range(1, n):
            acc = acc + p_ref[j].astype(f32)
        o_ref[...] = acc

    return pl.pallas_call(
        body, name=name, out_shape=jax.ShapeDtypeStruct((rows, cols), f32),
        compiler_params=_cparams())(parts)


def _pack_rows(arrs, dtype=f32):
    rows, spans, at = [], [], 0
    for a in arrs:
        flat = a.reshape(-1).astype(dtype)
        nr = -(-flat.shape[0] // 1024)
        rows.append(jnp.pad(flat, (0, nr * 1024 - flat.shape[0])).reshape(nr, 1024))
        spans.append((at, nr))
        at += nr
    pad = (-at) % 16
    if pad:
        rows.append(jnp.zeros((pad, 1024), dtype))
    return jnp.concatenate(rows, axis=0), spans


def _unpack_rows(packed, spans, shapes):
    out = []
    for (at, nr), shp in zip(spans, shapes):
        n = math.prod(shp)
        out.append(packed[at:at + nr].reshape(-1)[:n].reshape(shp))
    return out


BIG = ("w_in", "w_out", "w_ffn_in", "w_ffn_out")
SMALL_REPL = ("norm_mix_g", "b_gate", "conv_b", "attn_sink", "norm_ffn_g", "norm_final_g")
GATE_W = ("lru_wa", "lru_wx")
SMALL_SHARD = ("conv_w", "lru_lambda", "lru_ba", "lru_bx")
ORDER = ("norm_mix_g", "w_in", "b_gate", "conv_w", "conv_b", "lru_lambda", "lru_wa", "lru_ba", "lru_wx",
         "lru_bx", "attn_sink", "w_out", "norm_ffn_g", "w_ffn_in", "w_ffn_out", "norm_final_g")
EARLY_F32 = ("loss", "b_gate", "lru_lambda", "lru_ba", "lru_bx", "attn_sink", "norm_ffn_g", "norm_final_g")
LATE = ("norm_mix_g", "conv_w", "conv_b")


def kernel(x, norm_mix_g, w_in, b_gate, conv_w, conv_b, lru_lambda, lru_wa, lru_ba, lru_wx, lru_bx, attn_sink, w_out, norm_ffn_g, w_ffn_in, w_ffn_out, norm_final_g, loss_target, m_norm_mix_g, m_w_in, m_b_gate, m_conv_w, m_conv_b, m_lru_lambda, m_lru_wa, m_lru_ba, m_lru_wx, m_lru_bx, m_attn_sink, m_w_out, m_norm_ffn_g, m_w_ffn_in, m_w_ffn_out, m_norm_final_g, v_norm_mix_g, v_w_in, v_b_gate, v_conv_w, v_conv_b, v_lru_lambda, v_lru_wa, v_lru_ba, v_lru_wx, v_lru_bx, v_attn_sink, v_w_out, v_norm_ffn_g, v_w_ffn_in, v_w_ffn_out, v_norm_final_g):
    w = dict(norm_mix_g=norm_mix_g, w_in=w_in, b_gate=b_gate, conv_w=conv_w, conv_b=conv_b, lru_lambda=lru_lambda,
             lru_wa=lru_wa, lru_ba=lru_ba, lru_wx=lru_wx, lru_bx=lru_bx, attn_sink=attn_sink, w_out=w_out,
             norm_ffn_g=norm_ffn_g, w_ffn_in=w_ffn_in, w_ffn_out=w_ffn_out, norm_final_g=norm_final_g)
    m = dict(norm_mix_g=m_norm_mix_g, w_in=m_w_in, b_gate=m_b_gate, conv_w=m_conv_w, conv_b=m_conv_b,
             lru_lambda=m_lru_lambda, lru_wa=m_lru_wa, lru_ba=m_lru_ba, lru_wx=m_lru_wx, lru_bx=m_lru_bx,
             attn_sink=m_attn_sink, w_out=m_w_out, norm_ffn_g=m_norm_ffn_g, w_ffn_in=m_w_ffn_in,
             w_ffn_out=m_w_ffn_out, norm_final_g=m_norm_final_g)
    v = dict(norm_mix_g=v_norm_mix_g, w_in=v_w_in, b_gate=v_b_gate, conv_w=v_conv_w, conv_b=v_conv_b,
             lru_lambda=v_lru_lambda, lru_wa=v_lru_wa, lru_ba=v_lru_ba, lru_wx=v_lru_wx, lru_bx=v_lru_bx,
             attn_sink=v_attn_sink, w_out=v_w_out, norm_ffn_g=v_norm_ffn_g, w_ffn_in=v_w_ffn_in,
             w_ffn_out=v_w_ffn_out, norm_final_g=v_norm_final_g)
    me = 4 * lax.axis_index("x") + 2 * lax.axis_index("y") + lax.axis_index("c")

    def shard_t(a):
        return jnp.swapaxes(a[0], 0, 1)

    def rows_parts(g):
        return g.reshape(N_DEV, -1, g.shape[1])

    shard_rows = jnp.concatenate([w[n][0] for n in SMALL_SHARD], axis=0)
    small = {n: w[n] for n in ("norm_mix_g", "b_gate", "conv_b", "attn_sink", "norm_ffn_g")}
    small["lru_wa"], small["lru_wx"] = lru_wa[0], lru_wx[0]
    small["norm_final_g"] = norm_final_g.reshape(1, D)
    env, recv = {}, {}

    def before(name):
        if name == "norm_x":
            return [(shard_t(w_in).astype(bf16), False), (shard_rows, False)]
        if name == "inproj":
            return [(w_out[0].astype(bf16), False), (w_ffn_out[0].astype(bf16), False)]
        if name == "attn_fwd":
            return [(shard_t(w_ffn_in).astype(bf16), False)]
        if name == "ffn_in_bwd":
            return [(rows_parts(env["dw_fo"]), True)]
        if name == "attn_bwd":
            return [(rows_parts(env["dw_out"]), True)]
        if name == "lru_bwd":
            return [(rows_parts(env["dw_fi_t"]), True)]
        if name == "conv_bwd":
            ge = env["grads_early"]
            p32, env["early_f32_spans"] = _pack_rows([ge[n] for n in EARLY_F32])
            return [(p32, False), *[(ge[n].astype(bf16).reshape(-1, LRU_BLOCK), False) for n in GATE_W]]
        if name == "inproj_bwd":
            return [(rows_parts(env["dw_in_t"]), True)]
        return []

    def after(name, got):
        if name == "norm_x":
            env["w_in_t"] = got[0].reshape(IN_W, D)
            full_rows = jnp.swapaxes(got[1], 0, 1).reshape(shard_rows.shape[0], -1)
            small["conv_w"], small["lru_lambda"] = full_rows[0:4], full_rows[4:6]
            small["lru_ba"], small["lru_bx"] = full_rows[6:8], full_rows[8:10]
        elif name == "inproj":
            env["w_out"], env["w_fo"] = got[0].reshape(D, D), got[1].reshape(D_FF, D)
        elif name == "attn_fwd":
            env["w_fi_t"] = got[0].reshape(2 * D_FF, D)
        elif name == "ffn_in_bwd":
            recv["w_ffn_out"] = got[0]
        elif name == "attn_bwd":
            recv["w_out"] = got[0]
        elif name == "lru_bwd":
            recv["w_ffn_in"] = got[0]
        elif name == "conv_bwd":
            recv["early_f32"], recv["lru_wa"], recv["lru_wx"] = got
        elif name == "inproj_bwd":
            recv["w_in"] = got[0]

    grad_x, grads = _local_step(x[0], loss_target[0], small, env, before, after)

    outs = {}
    for name in ("w_out", "w_ffn_out"):
        outs[name] = _adamw(recv[name], w[name], m[name], v[name], "adamw_" + name)
    for name in ("w_in", "w_ffn_in"):
        t = lambda a: jnp.swapaxes(a, 1, 2)
        outs[name] = [t(r) for r in _adamw(recv[name], t(w[name]), t(m[name]), t(v[name]), "adamw_" + name)]
    for name in GATE_W:
        t = lambda a: a.reshape(1, -1, LRU_BLOCK)
        res = _adamw(recv[name], t(w[name]), t(m[name]), t(v[name]), "adamw_" + name)
        outs[name] = [r.reshape(w[name].shape) for r in res]

    small_names = SMALL_REPL + SMALL_SHARD
    late_packed, late_spans = _pack_rows([grads[n] for n in LATE])
    (got_late,) = _exchange([(late_packed, False)], "gather_late_grads")
    summed = {}
    for names, got, spans, tag in ((EARLY_F32, recv["early_f32"], env["early_f32_spans"], "early_f32"),
                                   (LATE, got_late, late_spans, "late")):
        total = _sum_parts(got, "sum_small_" + tag)
        summed.update(zip(names, _unpack_rows(total, spans, [grads[n].shape for n in names])))
    loss = summed["loss"].reshape(())
    gsm = {n: summed[n].reshape(w[n].shape) for n in SMALL_REPL}
    for n in SMALL_SHARD:
        full = summed[n]
        gsm[n] = lax.dynamic_slice_in_dim(full, me * 128, 128, axis=1).reshape(w[n].shape)
    pk = lambda dct: _pack_rows([dct[n] for n in small_names])[0]
    gp, sp = _pack_rows([gsm[n] for n in small_names])
    res = _adamw(gp[None], pk(w)[None], pk(m)[None], pk(v)[None], "adamw_small")
    sshapes = [w[n].shape for n in small_names]
    for idx, t in enumerate(res):
        for n, a in zip(small_names, _unpack_rows(t[0], sp, sshapes)):
            outs.setdefault(n, [None] * 4)[idx] = a

    result = [loss, grad_x[None]]
    for idx in range(4):
        result += [outs[n][idx] for n in ORDER]
    return tuple(result)
```

```python
import functools
import math

import jax
import jax.numpy as jnp
from jax import lax
from jax.experimental import pallas as pl
from jax.experimental.pallas import tpu as pltpu

f32 = jnp.float32
bf16 = jnp.bfloat16

D = 1024
D_FF = 2816
IN_W = 5632
N_HEADS = 16
N_KV = 4
HEAD_DIM = 64
WINDOW = 128
BLK = 128
LRU_HEADS = 16
LRU_BLOCK = 64
LRU_GROUPS = 4
LRU_GW = 256
LRU_CHUNK = 64
LRU_CHUNK_BWD = 512
LRU_ROWS = 2048
RGLRU_C = 8.0
EPS = 1e-6
NEG_INF = -1e30
N_DEV = 8

ADAM_LR = 0.001
ADAM_B1 = 0.9
ADAM_B2 = 0.999
ADAM_EPS = 1e-08
ADAM_WD = 0.01
ADAM_STEP = 10

VMEM_MB = 56

C_U, C_G, C_Q, C_Z0, C_Z1, C_K, C_V = 0, 1024, 2048, 3072, 4096, 5120, 5376


def _cparams(vmem_mb=VMEM_MB):
    return pltpu.CompilerParams(vmem_limit_bytes=vmem_mb << 20)


def _div_tile(n, pref):
    if n <= pref:
        return n
    return max(t for t in range(8, pref + 1, 8) if n % t == 0)


def _sigmoid(x):
    return 0.5 * jnp.tanh(0.5 * x) + 0.5


def _log1p(x):
    u = 1.0 + x
    d = u - 1.0
    return jnp.where(d == 0.0, x, jnp.log(u) * (x / jnp.where(d == 0.0, 1.0, d)))


def _softplus(x):
    return jnp.maximum(x, 0.0) + _log1p(jnp.exp(-jnp.abs(x)))


def _gelu_and_grad(x):
    c = math.sqrt(2.0 / math.pi)
    inner = c * (x + 0.044715 * (x * x * x))
    t = jnp.tanh(inner)
    gelu = 0.5 * x * (1.0 + t)
    dinner = c * (1.0 + 3 * 0.044715 * (x * x))
    dgelu = 0.5 * (1.0 + t) + 0.5 * x * (1.0 - t * t) * dinner
    return gelu, dgelu


def _rms_bwd(dn, xv, g):
    r = lax.rsqrt(jnp.mean(xv * xv, axis=-1, keepdims=True) + EPS)
    xh = xv * r
    dxh = dn * g
    dx = r * (dxh - xh * jnp.mean(dxh * xh, axis=-1, keepdims=True))
    return dx, dn * xh


ANY_SPEC = pl.BlockSpec(memory_space=pl.ANY)


def _comm_out_shape(src, scatter):
    return jax.ShapeDtypeStruct((N_DEV, *(src.shape[1:] if scatter else src.shape)), src.dtype)


def _comm_sems():
    return [pltpu.SemaphoreType.DMA((N_DEV - 1,)), pltpu.SemaphoreType.DMA((N_DEV - 1,)), pltpu.SemaphoreType.DMA]


def _scatter_descs(src_ref, out_ref, send_sems, recv_sems, local_sem):
    x, y, c = lax.axis_index("x"), lax.axis_index("y"), lax.axis_index("c")
    me = 4 * x + 2 * y + c
    descs = [pltpu.make_async_copy(src_ref.at[me], out_ref.at[me], local_sem)]
    for k in range(1, N_DEV):
        px, py, pc = x ^ (k >> 2), y ^ ((k >> 1) & 1), c ^ (k & 1)
        descs.append(pltpu.make_async_remote_copy(
            src_ref=src_ref.at[4 * px + 2 * py + pc], dst_ref=out_ref.at[me],
            send_sem=send_sems.at[k - 1], recv_sem=recv_sems.at[k - 1],
            device_id=(px, py, pc), device_id_type=pl.DeviceIdType.MESH))
    return descs


def _gather_copies(src_ref, out_ref, send_sems, recv_sems, local_sem, which):
    x, y, c = lax.axis_index("x"), lax.axis_index("y"), lax.axis_index("c")
    me, sibling = (x, y, c), (x, y, 1 - c)
    chips = [(1 - x, y), (x, 1 - y), (1 - x, 1 - y)]

    def slot(px, py, pc):
        return out_ref.at[4 * px + 2 * py + pc]

    def copy(k, block, to, src=None):
        return pltpu.make_async_remote_copy(
            src_ref=slot(*block) if src is None else src, dst_ref=slot(*block),
            send_sem=send_sems.at[k], recv_sem=recv_sems.at[k], device_id=to, device_id_type=pl.DeviceIdType.MESH)

    make = {
        "local": lambda: pltpu.make_async_copy(src_ref, slot(*me), local_sem),
        "first": lambda: [copy(0, me, sibling, src=src_ref)] + [copy(1 + j, me, (*chip, c), src=src_ref)
                                                                 for j, chip in enumerate(chips)],
        "passed": lambda: [copy(4 + j, (*chip, c), sibling) for j, chip in enumerate(chips)],
        "landed": lambda: [copy(1 + j, (*chip, c), me) for j, chip in enumerate(chips)],
        "later": lambda: [copy(0, sibling, me)] + [copy(4 + j, (*chip, 1 - c), me) for j, chip in enumerate(chips)],
    }
    return [make[name]() for name in which]


def _comm_start(src_ref, out_ref, sems, scatter):
    if scatter:
        for d in _scatter_descs(src_ref, out_ref, *sems):
            d.start()
    else:
        local, first = _gather_copies(src_ref, out_ref, *sems, which=("local", "first"))
        local.start()
        for cp in first:
            cp.start()


def _comm_pass_on(src_ref, out_ref, sems, scatter):
    if not scatter:
        landed, passed = _gather_copies(src_ref, out_ref, *sems, which=("landed", "passed"))
        for arrived, onward in zip(landed, passed):
            arrived.wait_recv()
            onward.start()


def _comm_finish(src_ref, out_ref, sems, scatter):
    if scatter:
        for d in _scatter_descs(src_ref, out_ref, *sems):
            d.wait()
    else:
        later, first, passed, local = _gather_copies(src_ref, out_ref, *sems,
                                                     which=("later", "first", "passed", "local"))
        for cp in later:
            cp.wait_recv()
        for cp in first + passed:
            cp.wait_send()
        local.wait()


def _exchange(comm, name):
    nc = len(comm)

    def body(*refs):
        srcs, outs, sems = refs[:nc], refs[nc:2 * nc], refs[2 * nc:]
        for stage in (_comm_start, _comm_pass_on, _comm_finish):
            for i in range(nc):
                stage(srcs[i], outs[i], sems[3 * i:3 * i + 3], comm[i][1])

    return pl.pallas_call(
        body, name=name, in_specs=[ANY_SPEC] * nc, out_specs=[ANY_SPEC] * nc,
        out_shape=[_comm_out_shape(*c) for c in comm],
        scratch_shapes=[s for _ in comm for s in _comm_sems()],
    )(*[c[0] for c in comm])


def _hosted_call(body, *, name, grid, in_specs, out_specs, out_shape, args, scratch_shapes=(), comm=()):
    nin, nout, nscr, nc = len(in_specs), len(out_specs), len(scratch_shapes), len(comm)
    steps = math.prod(grid)

    def wrapped(*refs):
        ins = refs[:nin]
        csrc = refs[nin:nin + nc]
        outs = refs[nin + nc:nin + nc + nout]
        cout = refs[nin + nc + nout:nin + 2 * nc + nout]
        scr = refs[nin + 2 * nc + nout:]
        sems = scr[nscr:]

        def at(step, stage):
            lin = 0
            for a in range(len(grid)):
                lin = lin * grid[a] + pl.program_id(a)

            @pl.when(lin == step)
            def _():
                for i in range(nc):
                    stage(csrc[i], cout[i], sems[3 * i:3 * i + 3], comm[i][1])

        if nc:
            at(0, _comm_start)

        body(*ins, *outs, *scr[:nscr])

        if nc:
            at((3 * (steps - 1)) // 4, _comm_pass_on)
            at(steps - 1, _comm_finish)

    res = pl.pallas_call(
        wrapped, name=name, grid=grid,
        in_specs=[*in_specs, *[ANY_SPEC] * nc], out_specs=[*out_specs, *[ANY_SPEC] * nc],
        out_shape=[*out_shape, *[_comm_out_shape(*c) for c in comm]],
        scratch_shapes=[*scratch_shapes, *[s for _ in comm for s in _comm_sems()]],
        compiler_params=_cparams())(*args, *[c[0] for c in comm])
    return res[:nout], res[nout:]


def _rmsnorm_bf16(x, g, name, tm=1024, comm=()):
    S, dm = x.shape
    tm = min(tm, S)

    def body(x_ref, g_ref, xn_ref):
        xv = x_ref[...]
        r = lax.rsqrt(jnp.mean(xv * xv, axis=-1, keepdims=True) + EPS)
        xn_ref[...] = ((xv * r) * g_ref[...]).astype(bf16)

    row = pl.BlockSpec((tm, dm), lambda i: (i, 0))
    return _hosted_call(
        body, name=name, grid=(S // tm,), in_specs=[row, pl.BlockSpec((1, dm), lambda i: (0, 0))],
        out_specs=[row], out_shape=[jax.ShapeDtypeStruct((S, dm), bf16)], args=(x, g), comm=comm)


def _matmul_t(a, wt, name, tm=2048, tn=512, row_block=lambda j: j, comm=()):
    S, dm = a.shape
    n = wt.shape[0]
    tm = min(tm, S)

    def body(a_ref, w_ref, o_ref):
        o_ref[...] = lax.dot_general(a_ref[...], w_ref[...], (((1,), (1,)), ((), ())),
                                     preferred_element_type=f32).astype(bf16)

    return _hosted_call(
        body, name=name, grid=(S // tm, n // tn),
        in_specs=[pl.BlockSpec((tm, dm), lambda i, j: (i, 0)),
                  pl.BlockSpec((tn, dm), lambda i, j: (row_block(j), 0))],
        out_specs=[pl.BlockSpec((tm, tn), lambda i, j: (i, j))],
        out_shape=[jax.ShapeDtypeStruct((S, n), bf16)], args=(a, wt), comm=comm)


def _norm_matmul(x, g, wt, name, tm=1024, tn=1408, row_block=lambda j: j, comm=()):
    S, dm = x.shape
    n = wt.shape[0]
    tm = min(tm, S)

    def body(x_ref, g_ref, w_ref, xn_ref, o_ref):
        @pl.when(pl.program_id(1) == 0)
        def _():
            xv = x_ref[...]
            r = lax.rsqrt(jnp.mean(xv * xv, axis=-1, keepdims=True) + EPS)
            xn_ref[...] = ((xv * r) * g_ref[...]).astype(bf16)

        o_ref[...] = lax.dot_general(xn_ref[...], w_ref[...], (((1,), (1,)), ((), ())),
                                     preferred_element_type=f32).astype(bf16)

    return _hosted_call(
        body, name=name, grid=(S // tm, n // tn),
        in_specs=[pl.BlockSpec((tm, dm), lambda i, j: (i, 0)),
                  pl.BlockSpec((1, dm), lambda i, j: (0, 0)),
                  pl.BlockSpec((tn, dm), lambda i, j: (row_block(j), 0))],
        out_specs=[pl.BlockSpec((tm, dm), lambda i, j: (i, 0)),
                   pl.BlockSpec((tm, tn), lambda i, j: (i, j))],
        out_shape=[jax.ShapeDtypeStruct((S, dm), bf16), jax.ShapeDtypeStruct((S, n), bf16)],
        args=(x, g, wt), comm=comm)


def _mm_tn(a, b, name, tk, tn, tmc=2048, into=None, row=0, out_rows=None):
    m, ka = a.shape
    n = b.shape[1]
    tmc = min(tmc, m)
    nk = m // tmc

    def body(a_ref, b_ref, *rest):
        o_ref, acc_ref = rest[-2:]
        k = pl.program_id(2)
        part = lax.dot_general(a_ref[...], b_ref[...], (((0,), (0,)), ((), ())), preferred_element_type=f32)

        @pl.when(k == 0)
        def _():
            acc_ref[...] = part

        @pl.when(k > 0)
        def _():
            acc_ref[...] += part

        @pl.when(k == nk - 1)
        def _():
            o_ref[...] = acc_ref[...].astype(bf16)

    in_specs = [pl.BlockSpec((tmc, tk), lambda i, j, k: (k, i)), pl.BlockSpec((tmc, tn), lambda i, j, k: (k, j))]
    if into is None:
        return pl.pallas_call(
            body, name=name, grid=(ka // tk, n // tn, nk), in_specs=in_specs,
            out_specs=pl.BlockSpec((tk, tn), lambda i, j, k: (i + row, j)),
            out_shape=jax.ShapeDtypeStruct((out_rows or ka, n), bf16),
            scratch_shapes=[pltpu.VMEM((tk, tn), f32)],
            compiler_params=_cparams())(a, b)
    return pl.pallas_call(
        body, name=name, grid=(ka // tk, n // tn, nk), in_specs=[*in_specs, ANY_SPEC],
        out_specs=pl.BlockSpec((tk, tn), lambda i, j, k: (i + row, j)),
        out_shape=jax.ShapeDtypeStruct(into.shape, into.dtype),
        scratch_shapes=[pltpu.VMEM((tk, tn), f32)], input_output_aliases={2: 0},
        compiler_params=_cparams())(a, b, into)


HALO = 16


def _rows_at(ext, o, tc):
    if o == 0:
        return ext[HALO:HALO + tc]
    return pltpu.roll(ext, (-o) % ext.shape[0], 0)[HALO:HALO + tc]


def _halo_specs(tc, S, width, col):
    per = tc // HALO
    last = S // HALO - 1
    return (pl.BlockSpec((tc, width), lambda i: (i, col)),
            pl.BlockSpec((HALO, width), lambda i: (jnp.maximum(i * per - 1, 0), col)),
            pl.BlockSpec((HALO, width), lambda i: (jnp.minimum((i + 1) * per, last), col)))


def _extended(cur_ref, prev_ref, next_ref, i, nsteps):
    prev = jnp.where(i > 0, prev_ref[...].astype(f32), 0.0)
    nxt = jnp.where(i < nsteps - 1, next_ref[...].astype(f32), 0.0)
    return jnp.concatenate([prev, cur_ref[...].astype(f32), nxt], axis=0)


def _conv_fwd(proj, cw, cb, tc=1024):
    S = proj.shape[0]
    tc = min(tc, S)
    nsteps = S // tc

    def body(cur_ref, prev_ref, next_ref, w_ref, b_ref, o_ref):
        ext = _extended(cur_ref, prev_ref, next_ref, pl.program_id(0), nsteps)
        acc = _rows_at(ext, -2, tc) * w_ref[0:1, :]
        for k in range(1, 4):
            acc = acc + _rows_at(ext, k - 2, tc) * w_ref[k:k + 1, :]
        o_ref[...] = acc + b_ref[...]

    return pl.pallas_call(
        body, name="conv_fwd", grid=(nsteps,),
        in_specs=[*_halo_specs(tc, S, D, 0),
                  pl.BlockSpec((4, D), lambda i: (0, 0)), pl.BlockSpec((1, D), lambda i: (0, 0))],
        out_specs=pl.BlockSpec((tc, D), lambda i: (i, 0)),
        out_shape=jax.ShapeDtypeStruct((S, D), f32),
        compiler_params=_cparams())(proj, proj, proj, cw, cb)


def _conv_bwd(duc_f, duc_b, proj, cw, tc=1024, comm=()):
    S = proj.shape[0]
    tc = min(tc, S)
    nsteps = S // tc

    def body(fc, fp, fn, bc, bp, bn, uc_, up, un, w_ref, du_ref, dw_ref, db_ref):
        i = pl.program_id(0)

        @pl.when(i == 0)
        def _():
            dw_ref[...] = jnp.zeros_like(dw_ref)
            db_ref[...] = jnp.zeros_like(db_ref)

        dext = _extended(fc, fp, fn, i, nsteps) + _extended(bc, bp, bn, i, nsteps)
        uext = _extended(uc_, up, un, i, nsteps)
        d = dext[HALO:HALO + tc]
        acc = _rows_at(dext, 2, tc) * w_ref[0:1, :]
        for k in range(1, 4):
            acc = acc + _rows_at(dext, 2 - k, tc) * w_ref[k:k + 1, :]
        du_ref[...] = acc.astype(bf16)
        wrow = lax.broadcasted_iota(jnp.int32, (4, D), 0)
        for k in range(4):
            dw_ref[...] += jnp.where(wrow == k, jnp.sum(d * _rows_at(uext, k - 2, tc), axis=0, keepdims=True), 0.0)
        db_ref[...] += jnp.sum(d, axis=0, keepdims=True)

    return _hosted_call(
        body, name="conv_bwd", grid=(nsteps,),
        in_specs=[*_halo_specs(tc, S, D, 0), *_halo_specs(tc, S, D, 0), *_halo_specs(tc, S, D, 0),
                  pl.BlockSpec((4, D), lambda i: (0, 0))],
        out_specs=[pl.BlockSpec((tc, D), lambda i: (i, 0)),
                   pl.BlockSpec((4, D), lambda i: (0, 0)), pl.BlockSpec((1, D), lambda i: (0, 0))],
        out_shape=[jax.ShapeDtypeStruct((S, D), bf16), jax.ShapeDtypeStruct((4, D), f32),
                   jax.ShapeDtypeStruct((1, D), f32)],
        args=(duc_f, duc_f, duc_f, duc_b, duc_b, duc_b, proj, proj, proj, cw), comm=comm)


def _scan_scratch(tc):
    halves = [pltpu.VMEM((tc, 128), f32) for _ in range(2 * (LRU_GW // 128))]
    return [*halves, pltpu.VMEM((tc // 8, LRU_GW), f32), pltpu.VMEM((tc // 8, LRU_GW), f32)]


def _log_scan(a, b, row, n, reverse, steps):
    for s in steps:
        shift = a.shape[0] - s if reverse else s
        keep = (row < n - s) if reverse else (row >= s)
        a_sh = pltpu.roll(a, shift, 0)
        b_sh = pltpu.roll(b, shift, 0)
        b = jnp.where(keep, a * b_sh + b, b)
        a = jnp.where(keep, a * a_sh, a)
    return a, b


def _scan_chunk(a, b, carry, reverse, *scratch):
    tc, w = a.shape
    ng = tc // 8
    nl = w // 128
    sa_refs, sb_refs, sc_ref, st_ref = scratch[:nl], scratch[nl:2 * nl], scratch[2 * nl], scratch[2 * nl + 1]
    sub = lax.broadcasted_iota(jnp.int32, (8, w), 0)
    ag, bg = [], []
    for k in range(ng):
        ak, bk = _log_scan(a[8 * k:8 * k + 8], b[8 * k:8 * k + 8], sub, 8, reverse, (1, 2, 4))
        ag.append(ak)
        bg.append(bk)
    a = jnp.concatenate(ag, axis=0)
    b = jnp.concatenate(bg, axis=0)
    edge = 0 if reverse else 7
    for i in range(nl):
        sa_refs[i][...] = a[:, 128 * i:128 * (i + 1)]
        sb_refs[i][...] = b[:, 128 * i:128 * (i + 1)]
    ta = jnp.concatenate([r[pl.ds(edge, ng, stride=8), :] for r in sa_refs], axis=1)
    tb = jnp.concatenate([r[pl.ds(edge, ng, stride=8), :] for r in sb_refs], axis=1)
    grow = lax.broadcasted_iota(jnp.int32, (ng, w), 0)
    ta, tb = _log_scan(ta, tb, grow, ng, reverse, [1 << i for i in range(ng.bit_length() - 1)])
    state = tb + ta * carry
    st_ref[...] = state
    if reverse:
        sc_ref[...] = jnp.where(grow == ng - 1, carry, pltpu.roll(state, ng - 1, 0))
    else:
        sc_ref[...] = jnp.where(grow == 0, carry, pltpu.roll(state, 1, 0))
    h = jnp.concatenate([bg[k] + ag[k] * sc_ref[k:k + 1, :] for k in range(ng)], axis=0)
    return h, (st_ref[0:1, :] if reverse else st_ref[ng - 1:ng, :])


def _lru_gates(uc, w, p_ref):
    pre = jnp.dot(uc.astype(bf16), w, preferred_element_type=f32)
    r = _sigmoid(pre[:, :LRU_GW] + p_ref[0, 1:2, :])
    gi = _sigmoid(pre[:, LRU_GW:] + p_ref[0, 2:3, :])
    sp = _softplus(-p_ref[0, 0:1, :])
    log_a = -RGLRU_C * r * sp
    a = jnp.exp(log_a)
    x = 2.0 * log_a
    series = -x * (1.0 + x * (0.5 + x * (1.0 / 6 + x * (1.0 / 24))))
    beta = jnp.sqrt(jnp.maximum(jnp.where(x > -0.0625, series, 1.0 - a * a), 0.0))
    return r, gi, sp, a, beta


def _lru_fwd(uc, wg, lp, reverse, comm=()):
    S = uc.shape[0]
    tc = LRU_CHUNK
    rows = min(LRU_ROWS, S)
    nsub = rows // tc
    nblk = S // rows
    d = 1 if reverse else 0

    def bidx(c):
        return nblk - 1 - c if reverse else c

    def body(uc_ref, w_ref, p_ref, h_ref, carry_ref, *scan_scratch):
        @pl.when(pl.program_id(1) == 0)
        def _():
            carry_ref[...] = jnp.zeros_like(carry_ref)

        carry = carry_ref[...]
        for j in (reversed(range(nsub)) if reverse else range(nsub)):
            sl = slice(j * tc, (j + 1) * tc)
            ucv = uc_ref[sl, :]
            _, gi, _, a, beta = _lru_gates(ucv, w_ref[0], p_ref)
            h, carry = _scan_chunk(a, beta * (gi * ucv), carry, reverse, *scan_scratch)
            h_ref[sl, :] = h.astype(bf16)
        carry_ref[...] = carry

    return _hosted_call(
        body, name="lru_fwd_rev" if reverse else "lru_fwd", grid=(LRU_GROUPS, nblk),
        in_specs=[pl.BlockSpec((rows, LRU_GW), lambda g, c: (bidx(c), g)),
                  pl.BlockSpec((1, LRU_GW, 2 * LRU_GW), lambda g, c: (g, 0, d)),
                  pl.BlockSpec((1, 8, LRU_GW), lambda g, c: (d, 0, g))],
        out_specs=[pl.BlockSpec((rows, LRU_GW), lambda g, c: (bidx(c), g))],
        out_shape=[jax.ShapeDtypeStruct((S, D), bf16)],
        scratch_shapes=[pltpu.VMEM((1, LRU_GW), f32), *_scan_scratch(tc)],
        args=(uc, wg, lp), comm=comm)


def _lru_bwd(uc, dh, h, wg, lp, reverse, comm=()):
    S = uc.shape[0]
    tc = LRU_CHUNK_BWD
    rows = min(LRU_ROWS, S)
    nsub = rows // tc
    nblk = S // rows
    d = 1 if reverse else 0
    per = rows // HALO
    last8 = S // HALO - 1

    def bidx(c):
        return c if reverse else nblk - 1 - c

    def halo_idx(c):
        if reverse:
            return jnp.minimum((bidx(c) + 1) * per, last8)
        return jnp.maximum(bidx(c) * per - 1, 0)

    def body(uc_ref, dh_ref, h_ref, halo_ref, w_ref, p_ref, duc_ref, dw_ref, dp_ref, carry_ref, tmp_ref,
             *scan_scratch):
        c = pl.program_id(1)
        bi = bidx(c)

        @pl.when(c == 0)
        def _():
            carry_ref[...] = jnp.zeros_like(carry_ref)
            dw_ref[...] = jnp.zeros_like(dw_ref)
            dp_ref[...] = jnp.zeros_like(dp_ref)

        row = lax.broadcasted_iota(jnp.int32, (tc, LRU_GW), 0)
        carry = carry_ref[...]
        dw = jnp.zeros((LRU_GW, 2 * LRU_GW), f32)
        dsp = jnp.zeros((1, LRU_GW), f32)
        dba = jnp.zeros((1, LRU_GW), f32)
        dbx = jnp.zeros((1, LRU_GW), f32)
        for j in (range(nsub) if reverse else reversed(range(nsub))):
            sl = slice(j * tc, (j + 1) * tc)
            ucv = uc_ref[sl, :]
            ucb = ucv.astype(bf16)
            r, gi, sp, a, beta = _lru_gates(ucv, w_ref[0], p_ref)
            hv = h_ref[sl, :].astype(f32)
            dhv = dh_ref[sl, :].astype(f32)
            if reverse:
                alpha = jnp.where(row == 0, 1.0, pltpu.roll(a, 1, 0))
                gsc, _ = _scan_chunk(alpha, dhv, carry, False, *scan_scratch)
                if j < nsub - 1:
                    edge = h_ref[(j + 1) * tc:(j + 1) * tc + HALO, :].astype(f32)[0:1, :]
                else:
                    edge = jnp.where(bi < nblk - 1, halo_ref[...].astype(f32)[0:1, :], 0.0)
                h_nb = jnp.where(row == tc - 1, edge, pltpu.roll(hv, tc - 1, 0))
            else:
                alpha = jnp.where(row == tc - 1, 1.0, pltpu.roll(a, tc - 1, 0))
                gsc, _ = _scan_chunk(alpha, dhv, carry, True, *scan_scratch)
                if j > 0:
                    edge = h_ref[j * tc - HALO:j * tc, :].astype(f32)[HALO - 1:HALO, :]
                else:
                    edge = jnp.where(bi > 0, halo_ref[...].astype(f32)[HALO - 1:HALO, :], 0.0)
                h_nb = jnp.where(row == 0, edge, pltpu.roll(hv, 1, 0))
            tmp_ref[...] = a * gsc
            carry = tmp_ref[tc - 1:tc, :] if reverse else tmp_ref[0:1, :]

            da = gsc * h_nb
            dbeta = gsc * (gi * ucv)
            dl = da * a - dbeta * (a * a) / beta
            dr = dl * (-RGLRU_C * sp)
            dsp = dsp + jnp.sum(dl * (-RGLRU_C * r), axis=0, keepdims=True)
            dgi = gsc * beta * ucv
            dpre_r = dr * r * (1.0 - r)
            dpre_i = dgi * gi * (1.0 - gi)
            dba = dba + jnp.sum(dpre_r, axis=0, keepdims=True)
            dbx = dbx + jnp.sum(dpre_i, axis=0, keepdims=True)
            dpre = jnp.concatenate([dpre_r, dpre_i], axis=1).astype(bf16)
            back = lax.dot_general(dpre, w_ref[0], (((1,), (1,)), ((), ())), preferred_element_type=f32)
            duc_ref[sl, :] = (gsc * beta * gi + back).astype(bf16)
            dw = dw + lax.dot_general(ucb, dpre, (((0,), (0,)), ((), ())), preferred_element_type=f32)
        carry_ref[...] = carry
        dw_ref[0] += dw
        dlam = -dsp / (1.0 + jnp.exp(p_ref[0, 0:1, :]))
        prow = lax.broadcasted_iota(jnp.int32, (8, LRU_GW), 0)
        dp_ref[...] += (jnp.where(prow == 0, dlam, 0.0) + jnp.where(prow == 1, dba, 0.0)
                        + jnp.where(prow == 2, dbx, 0.0))

    chunk = pl.BlockSpec((rows, LRU_GW), lambda g, c: (bidx(c), g))
    return _hosted_call(
        body, name="lru_bwd_rev" if reverse else "lru_bwd", grid=(LRU_GROUPS, nblk),
        in_specs=[chunk, chunk, chunk,
                  pl.BlockSpec((HALO, LRU_GW), lambda g, c: (halo_idx(c), g)),
                  pl.BlockSpec((1, LRU_GW, 2 * LRU_GW), lambda g, c: (g, 0, d)),
                  pl.BlockSpec((1, 8, LRU_GW), lambda g, c: (d, 0, g))],
        out_specs=[chunk,
                   pl.BlockSpec((1, LRU_GW, 2 * LRU_GW), lambda g, c: (g, 0, 0)),
                   pl.BlockSpec((8, LRU_GW), lambda g, c: (0, g))],
        out_shape=[jax.ShapeDtypeStruct((S, D), bf16),
                   jax.ShapeDtypeStruct((LRU_GROUPS, LRU_GW, 2 * LRU_GW), f32),
                   jax.ShapeDtypeStruct((8, D), f32)],
        scratch_shapes=[pltpu.VMEM((1, LRU_GW), f32), pltpu.VMEM((tc, LRU_GW), f32), *_scan_scratch(tc)],
        args=(uc, dh, h, h, wg, lp), comm=comm)


def _slope(h):
    return 2.0 ** (-8.0 * (h + 1.0) / N_HEADS)


ATT_QB = 4


def _kv_specs(nb, col):
    return [pl.BlockSpec((BLK, N_KV * HEAD_DIM), lambda n: (jnp.maximum(ATT_QB * n - 1, 0), col)),
            pl.BlockSpec((ATT_QB * BLK, N_KV * HEAD_DIM), lambda n: (n, col)),
            pl.BlockSpec((BLK, N_KV * HEAD_DIM), lambda n: (jnp.minimum(ATT_QB * (n + 1), nb - 1), col))]


def _key_blocks(prev_ref, cur_ref, next_ref):
    return [prev_ref[...], *[cur_ref[BLK * s:BLK * (s + 1), :] for s in range(ATT_QB)], next_ref[...]]


def _dup_windows(r0, r1, r2):
    left = lax.broadcasted_iota(jnp.int32, (3 * BLK, 128), 1) < HEAD_DIM
    win = jnp.concatenate([r0, r1, r2], axis=0)
    out = []
    for i in range(N_KV // 2):
        t = win[:, i * 128:(i + 1) * 128]
        r = pltpu.roll(t, HEAD_DIM, 1)
        out += [jnp.where(left, t, r).astype(bf16), jnp.where(left, r, t).astype(bf16)]
    return out


def _attn_bias_init(bias_ref):
    k_loc = lax.broadcasted_iota(jnp.int32, (3 * BLK, BLK), 0)
    q_loc = lax.broadcasted_iota(jnp.int32, (3 * BLK, BLK), 1)
    adist = jnp.abs(q_loc + BLK - k_loc)
    adf = adist.astype(f32)
    for e in range(3):
        ok = adist <= WINDOW
        if e == 0:
            ok = ok & (k_loc >= BLK)
        if e == 2:
            ok = ok & (k_loc < 2 * BLK)
        for kv in range(N_KV):
            bias_ref[e, kv] = jnp.concatenate(
                [jnp.where(ok, (-_slope(4 * kv + j)) * adf, NEG_INF) for j in range(4)], axis=1)


def _stack_heads(ref, sub, kv, scale):
    left = lax.broadcasted_iota(jnp.int32, (BLK, 128), 1) < HEAD_DIM
    rows = []
    for pp in range(2):
        t = ref[BLK * sub:BLK * (sub + 1), (2 * kv + pp) * 128:(2 * kv + pp + 1) * 128]
        if scale != 1.0:
            t = t * scale
        zero = jnp.zeros_like(t)
        rows += [jnp.where(left, t, zero).astype(bf16), jnp.where(left, zero, t).astype(bf16)]
    return jnp.concatenate(rows, axis=0)


def _attn_softmax(qs, k2, bias, sink_ref, kv, stats=None):
    sink = jnp.concatenate([jnp.full((1, BLK), sink_ref[0, 4 * kv + j], f32) for j in range(4)], axis=1)
    s = lax.dot_general(k2, qs, (((1,), (1,)), ((), ())), preferred_element_type=f32) + bias
    m = jnp.maximum(jnp.max(s, axis=0, keepdims=True), sink) if stats is None else stats[0]
    p = jnp.exp(s - m)
    ps = jnp.exp(sink - m)
    inv = 1.0 / (jnp.sum(p, axis=0, keepdims=True) + ps) if stats is None else stats[1]
    return p, ps, m, inv


def _pair_tiles(t):
    return [jnp.concatenate([t[:HEAD_DIM, 256 * pp:256 * pp + 128],
                             t[HEAD_DIM:, 256 * pp + 128:256 * pp + 256]], axis=0).T for pp in range(2)]


def _attn_fwd(proj, sink, comm=()):
    S = proj.shape[0]
    nb = S // BLK
    assert nb >= 2 and nb % ATT_QB == 0

    def body(q_ref, k0, k1, k2_, v0, v1, v2_, sink_ref, o_ref, st_ref, bias_ref):
        n = pl.program_id(0)

        @pl.when(n == 0)
        def _():
            _attn_bias_init(bias_ref)

        kb = _key_blocks(k0, k1, k2_)
        vb = _key_blocks(v0, v1, v2_)
        for sub in range(ATT_QB):
            blk = ATT_QB * n + sub
            e = jnp.where(blk == 0, 0, jnp.where(blk == nb - 1, 2, 1))
            kk = _dup_windows(*kb[sub:sub + 3])
            vv = _dup_windows(*vb[sub:sub + 3])
            tiles = []
            for kv in range(N_KV):
                qs = _stack_heads(q_ref, sub, kv, HEAD_DIM ** -0.5)
                p, _, m, inv = _attn_softmax(qs, kk[kv], bias_ref[e, kv], sink_ref, kv)
                st_ref[sub, kv:kv + 1, :] = m
                st_ref[sub, N_KV + kv:N_KV + kv + 1, :] = inv
                ot = lax.dot_general(vv[kv], p.astype(bf16), (((0,), (0,)), ((), ())), preferred_element_type=f32)
                tiles += _pair_tiles(ot * inv)
            o_ref[BLK * sub:BLK * (sub + 1), :] = jnp.concatenate(tiles, axis=1).astype(bf16)

    return _hosted_call(
        body, name="attn_fwd", grid=(nb // ATT_QB,),
        in_specs=[pl.BlockSpec((ATT_QB * BLK, D), lambda n: (n, C_Q // D)),
                  *_kv_specs(nb, C_K // (N_KV * HEAD_DIM)), *_kv_specs(nb, C_V // (N_KV * HEAD_DIM)),
                  pl.BlockSpec(memory_space=pltpu.SMEM)],
        out_specs=[pl.BlockSpec((ATT_QB * BLK, D), lambda n: (n, 0)),
                   pl.BlockSpec((ATT_QB, 2 * N_KV, 4 * BLK), lambda n: (n, 0, 0))],
        out_shape=[jax.ShapeDtypeStruct((S, D), bf16), jax.ShapeDtypeStruct((nb, 2 * N_KV, 4 * BLK), f32)],
        scratch_shapes=[pltpu.VMEM((3, N_KV, 3 * BLK, 4 * BLK), f32)],
        args=(proj, proj, proj, proj, proj, proj, proj, sink), comm=comm)


def _attn_bwd(proj, sink, dyb, stats, comm=()):
    S = proj.shape[0]
    nb = S // BLK
    assert nb >= 2 and nb % ATT_QB == 0
    nsteps = nb // ATT_QB
    kvw = N_KV * HEAD_DIM

    def body(q_ref, k0, k1, k2_, v0, v1, v2_, sink_ref, do_ref, st_ref, dq_ref, dkv_out, ds_ref,
             bias_ref, dk_ref, dv_ref, dsk_ref, dkv_ref):
        n = pl.program_id(0)

        @pl.when(n == 0)
        def _():
            _attn_bias_init(bias_ref)
            dk_ref[...] = jnp.zeros_like(dk_ref)
            dv_ref[...] = jnp.zeros_like(dv_ref)
            dsk_ref[...] = jnp.zeros_like(dsk_ref)

        kb = _key_blocks(k0, k1, k2_)
        vb = _key_blocks(v0, v1, v2_)
        left3 = lax.broadcasted_iota(jnp.int32, (3 * BLK, 128), 1) < HEAD_DIM
        for sub in range(ATT_QB):
            blk = ATT_QB * n + sub
            e = jnp.where(blk == 0, 0, jnp.where(blk == nb - 1, 2, 1))
            kk = _dup_windows(*kb[sub:sub + 3])
            vv = _dup_windows(*vb[sub:sub + 3])
            start = pl.multiple_of(blk * BLK, BLK)
            dq_tiles, dks, dvs = [], [], []
            for kv in range(N_KV):
                qs = _stack_heads(q_ref, sub, kv, HEAD_DIM ** -0.5)
                dos = _stack_heads(do_ref, sub, kv, 1.0)
                stats = (st_ref[sub, kv:kv + 1, :], st_ref[sub, N_KV + kv:N_KV + kv + 1, :])
                p, ps, _, inv = _attn_softmax(qs, kk[kv], bias_ref[e, kv], sink_ref, kv, stats)
                pn = p * inv
                dp = lax.dot_general(vv[kv], dos, (((1,), (1,)), ((), ())), preferred_element_type=f32)
                delta = jnp.sum(pn * dp, axis=0, keepdims=True)
                dsc = (pn * (dp - delta)).astype(bf16)
                dsk_ref[kv:kv + 1, :] += delta * (ps * inv)
                dqt = lax.dot_general(kk[kv], dsc, (((0,), (0,)), ((), ())), preferred_element_type=f32)
                dq_tiles += _pair_tiles(dqt * (HEAD_DIM ** -0.5))
                dk = jnp.dot(dsc, qs, preferred_element_type=f32)
                dv = jnp.dot(pn.astype(bf16), dos, preferred_element_type=f32)
                dks.append(dk + pltpu.roll(dk, HEAD_DIM, 1))
                dvs.append(dv + pltpu.roll(dv, HEAD_DIM, 1))
            for jp in range(N_KV // 2):
                cols = slice(jp * 128, (jp + 1) * 128)
                dk_ref[pl.ds(start, 3 * BLK), cols] += jnp.where(left3, dks[2 * jp], dks[2 * jp + 1])
                dv_ref[pl.ds(start, 3 * BLK), cols] += jnp.where(left3, dvs[2 * jp], dvs[2 * jp + 1])
            dq_ref[BLK * sub:BLK * (sub + 1), :] = jnp.concatenate(dq_tiles, axis=1).astype(bf16)

        @pl.when(n == nsteps - 1)
        def _():
            rows = min(S, 512)
            for c in range(S // rows):
                dkv_ref[rows * c:rows * (c + 1), :kvw] = dk_ref[BLK + rows * c:BLK + rows * (c + 1), :].astype(bf16)
                dkv_ref[rows * c:rows * (c + 1), kvw:] = dv_ref[BLK + rows * c:BLK + rows * (c + 1), :].astype(bf16)
            pltpu.sync_copy(dkv_ref, dkv_out)
            lane = lax.broadcasted_iota(jnp.int32, (1, 128), 1)
            dsink = jnp.zeros((1, 128), f32)
            for h in range(N_HEADS):
                part = dsk_ref[h // 4:h // 4 + 1, (h % 4) * BLK:(h % 4 + 1) * BLK]
                dsink = dsink + jnp.where(lane == h, -jnp.sum(part), 0.0)
            ds_ref[...] = dsink

    acc = jax.ShapeDtypeStruct((S + 2 * BLK, N_KV * HEAD_DIM), f32)
    return _hosted_call(
        body, name="attn_bwd", grid=(nsteps,),
        in_specs=[pl.BlockSpec((ATT_QB * BLK, D), lambda n: (n, C_Q // D)),
                  *_kv_specs(nb, C_K // (N_KV * HEAD_DIM)), *_kv_specs(nb, C_V // (N_KV * HEAD_DIM)),
                  pl.BlockSpec(memory_space=pltpu.SMEM),
                  pl.BlockSpec((ATT_QB * BLK, D), lambda n: (n, 0)),
                  pl.BlockSpec((ATT_QB, 2 * N_KV, 4 * BLK), lambda n: (n, 0, 0))],
        out_specs=[pl.BlockSpec((ATT_QB * BLK, D), lambda n: (n, 0)), ANY_SPEC,
                   pl.BlockSpec((1, 128), lambda n: (0, 0))],
        out_shape=[jax.ShapeDtypeStruct((S, D), bf16), jax.ShapeDtypeStruct((S, 2 * kvw), bf16),
                   jax.ShapeDtypeStruct((1, 128), f32)],
        scratch_shapes=[pltpu.VMEM((3, N_KV, 3 * BLK, 4 * BLK), f32), pltpu.VMEM(acc.shape, f32),
                        pltpu.VMEM(acc.shape, f32), pltpu.VMEM((8, 4 * BLK), f32), pltpu.VMEM((S, 2 * kvw), bf16)],
        args=(proj, proj, proj, proj, proj, proj, proj, sink, dyb, stats), comm=comm)


def _merge_parts(hf, hb, g, z0, z1, yb, bg):
    g0 = _sigmoid(z0 + bg[:, :D].astype(bf16))
    g1 = _sigmoid(z1 + bg[:, D:].astype(bf16))
    gelu, dgelu = _gelu_and_grad(g)
    hs = hf + hb
    ya = hs * gelu
    return g0, g1, gelu, dgelu, hs, ya


def _merge_outproj(x, hf, hb, proj, yb, bg, w_out, tm=1024):
    S = x.shape[0]
    tm = min(tm, S)

    def body(x_ref, hf_ref, hb_ref, g_ref, z0_ref, z1_ref, yb_ref, bg_ref, w_ref, mg_ref, x1_ref):
        ybv = yb_ref[...]
        g0, g1, _, _, _, ya = _merge_parts(hf_ref[...], hb_ref[...], g_ref[...], z0_ref[...], z1_ref[...],
                                           ybv, bg_ref[...])
        mg = g0 * ya + g1 * ybv
        mg_ref[...] = mg
        x1_ref[...] = x_ref[...] + jnp.dot(mg, w_ref[...], preferred_element_type=f32)

    row = pl.BlockSpec((tm, D), lambda i: (i, 0))
    return pl.pallas_call(
        body, name="merge_outproj", grid=(S // tm,),
        in_specs=[row, row, row,
                  pl.BlockSpec((tm, D), lambda i: (i, C_G // D)),
                  pl.BlockSpec((tm, D), lambda i: (i, C_Z0 // D)),
                  pl.BlockSpec((tm, D), lambda i: (i, C_Z1 // D)),
                  row, pl.BlockSpec((1, 2 * D), lambda i: (0, 0)), pl.BlockSpec((D, D), lambda i: (0, 0))],
        out_specs=[row, row],
        out_shape=[jax.ShapeDtypeStruct((S, D), bf16), jax.ShapeDtypeStruct((S, D), f32)],
        compiler_params=_cparams())(x, hf, hb, proj, proj, proj, yb, bg, w_out)


def _ffn_out_loss(gu, x1, w_fo, g3, tgt, tm=256):
    S = x1.shape[0]
    tm = min(tm, S)

    def body(gt_ref, up_ref, x1_ref, w_ref, g_ref, t_ref, ff_ref, dx_ref, dxb_ref, loss_ref, dg_ref,
             dgt_ref, dup_ref):
        @pl.when(pl.program_id(0) == 0)
        def _():
            loss_ref[...] = jnp.zeros_like(loss_ref)
            dg_ref[...] = jnp.zeros_like(dg_ref)

        gt = gt_ref[...]
        up = up_ref[...]
        sg = _sigmoid(gt)
        silu = gt * sg
        ff = silu * up
        ff_ref[...] = ff
        x2 = x1_ref[...] + jnp.dot(ff, w_ref[...], preferred_element_type=f32)
        gv = g_ref[...]
        r = lax.rsqrt(jnp.mean(x2 * x2, axis=-1, keepdims=True) + EPS)
        xh = x2 * r
        diff = xh * gv - t_ref[...]
        loss_ref[...] += (0.5 / D) * jnp.sum(diff * diff)
        dy = diff * (1.0 / D)
        dg_ref[...] += jnp.sum(dy * xh, axis=0, keepdims=True)
        dxh = dy * gv
        dx = r * (dxh - xh * jnp.mean(dxh * xh, axis=-1, keepdims=True))
        dx_ref[...] = dx
        dxb = dx.astype(bf16)
        dxb_ref[...] = dxb
        dff = lax.dot_general(dxb, w_ref[...], (((1,), (1,)), ((), ())), preferred_element_type=f32)
        dup_ref[...] = (dff * silu.astype(f32)).astype(bf16)
        dgt_ref[...] = (dff * (up * (sg * (1.0 + gt * (1.0 - sg)))).astype(f32)).astype(bf16)

    row = pl.BlockSpec((tm, D), lambda i: (i, 0))
    vec = pl.BlockSpec((1, D), lambda i: (0, 0))
    wide = pl.BlockSpec((tm, D_FF), lambda i: (i, 0))
    wide_shape = jax.ShapeDtypeStruct((S, D_FF), bf16)
    return pl.pallas_call(
        body, name="ffn_out_loss", grid=(S // tm,),
        in_specs=[wide, pl.BlockSpec((tm, D_FF), lambda i: (i, 1)),
                  row, pl.BlockSpec((D_FF, D), lambda i: (0, 0)), vec, row],
        out_specs=[wide, row, row, pl.BlockSpec((1, 128), lambda i: (0, 0)), vec, wide, wide],
        out_shape=[wide_shape, jax.ShapeDtypeStruct((S, D), f32), jax.ShapeDtypeStruct((S, D), bf16),
                   jax.ShapeDtypeStruct((1, 128), f32), jax.ShapeDtypeStruct((1, D), f32), wide_shape, wide_shape],
        compiler_params=_cparams())(gu, gu, x1, w_fo, g3, tgt)


def _proj_bwd(pieces, wt, xres, g, dres, name, tm=512, comm=()):
    S = xres.shape[0]
    tm = min(tm, S)
    np_ = len(pieces)

    def body(*refs):
        p_refs = refs[:np_]
        w_refs = refs[np_:2 * np_]
        x_ref, g_ref, dres_ref, dx_ref, dxb_ref, dg_ref = refs[2 * np_:]

        @pl.when(pl.program_id(0) == 0)
        def _():
            dg_ref[...] = jnp.zeros_like(dg_ref)

        dn = jnp.dot(p_refs[0][...], w_refs[0][...], preferred_element_type=f32)
        for pr, wr in zip(p_refs[1:], w_refs[1:]):
            dn = dn + jnp.dot(pr[...], wr[...], preferred_element_type=f32)
        dxn, dgc = _rms_bwd(dn, x_ref[...], g_ref[...])
        dx = dres_ref[...] + dxn
        dx_ref[...] = dx
        dxb_ref[...] = dx.astype(bf16)
        dg_ref[...] += jnp.sum(dgc, axis=0, keepdims=True)

    row = pl.BlockSpec((tm, D), lambda i: (i, 0))
    vec = pl.BlockSpec((1, D), lambda i: (0, 0))
    return _hosted_call(
        body, name=name, grid=(S // tm,),
        in_specs=[*[pl.BlockSpec((tm, wd), functools.partial(lambda i, cb: (i, cb), cb=acb))
                    for _, acb, _, wd in pieces],
                  *[pl.BlockSpec((wd, D), functools.partial(lambda i, rb: (rb, 0), rb=wrb))
                    for _, _, wrb, wd in pieces],
                  row, vec, row],
        out_specs=[row, row, vec],
        out_shape=[jax.ShapeDtypeStruct((S, D), f32), jax.ShapeDtypeStruct((S, D), bf16),
                   jax.ShapeDtypeStruct((1, D), f32)],
        args=(*[p[0] for p in pieces], *[wt] * np_, xres, g, dres), comm=comm)


def _outproj_bwd(dx1b, w_out, hf, hb, proj, yb, bg, tm=1024):
    S = dx1b.shape[0]
    tm = min(tm, S)

    def body(dx_ref, w_ref, hf_ref, hb_ref, g_ref, z0_ref, z1_ref, yb_ref, bg_ref,
             dh_ref, dg_ref, dz_ref, dyb_ref, dbg_ref):
        @pl.when(pl.program_id(0) == 0)
        def _():
            dbg_ref[...] = jnp.zeros_like(dbg_ref)

        dm = lax.dot_general(dx_ref[...], w_ref[...], (((1,), (1,)), ((), ())), preferred_element_type=f32)
        ybv = yb_ref[...]
        g0, g1, gelu, dgelu, hs, ya = _merge_parts(hf_ref[...], hb_ref[...], g_ref[...], z0_ref[...],
                                                   z1_ref[...], ybv, bg_ref[...])
        dh_ref[...] = (dm * (g0 * gelu).astype(f32)).astype(bf16)
        dg_ref[...] = (dm * (g0 * hs * dgelu).astype(f32)).astype(bf16)
        dyb_ref[...] = (dm * g1.astype(f32)).astype(bf16)
        dz0 = dm * (ya * (g0 * (1.0 - g0))).astype(f32)
        dz1 = dm * (ybv * (g1 * (1.0 - g1))).astype(f32)
        dz = jnp.concatenate([dz0, dz1], axis=1)
        dz_ref[...] = dz.astype(bf16)
        dbg_ref[...] += jnp.sum(dz, axis=0, keepdims=True)

    row = pl.BlockSpec((tm, D), lambda i: (i, 0))
    return pl.pallas_call(
        body, name="outproj_bwd", grid=(S // tm,),
        in_specs=[row, pl.BlockSpec((D, D), lambda i: (0, 0)), row, row,
                  pl.BlockSpec((tm, D), lambda i: (i, C_G // D)),
                  pl.BlockSpec((tm, D), lambda i: (i, C_Z0 // D)),
                  pl.BlockSpec((tm, D), lambda i: (i, C_Z1 // D)),
                  row, pl.BlockSpec((1, 2 * D), lambda i: (0, 0))],
        out_specs=[row, row, pl.BlockSpec((tm, 2 * D), lambda i: (i, 0)), row,
                   pl.BlockSpec((1, 2 * D), lambda i: (0, 0))],
        out_shape=[jax.ShapeDtypeStruct((S, D), bf16), jax.ShapeDtypeStruct((S, D), bf16),
                   jax.ShapeDtypeStruct((S, 2 * D), bf16), jax.ShapeDtypeStruct((S, D), bf16),
                   jax.ShapeDtypeStruct((1, 2 * D), f32)],
        compiler_params=_cparams())(dx1b, w_out, hf, hb, proj, proj, proj, yb, bg)


def _block_diag_groups(w):
    w4 = w.reshape(LRU_GROUPS, 4, LRU_BLOCK, LRU_BLOCK)
    eye = jnp.eye(4, dtype=w.dtype)
    return jnp.einsum("ghij,hk->ghikj", w4, eye).reshape(LRU_GROUPS, LRU_GW, LRU_GW)


def _diag_blocks(dw):
    d5 = dw.reshape(LRU_GROUPS, 4, LRU_BLOCK, 4, LRU_BLOCK)
    return jnp.stack([d5[:, h, :, h, :] for h in range(4)], axis=1).reshape(LRU_HEADS, LRU_BLOCK, LRU_BLOCK)


def _local_step(x, tgt, small, env, before=lambda name: (), after=lambda name, got: None):
    S = x.shape[0]
    g1, g2, g3 = small["norm_mix_g"], small["norm_ffn_g"], small["norm_final_g"]
    bg, cb, sink = small["b_gate"], small["conv_b"], small["attn_sink"]

    def hosted(name, fn, *args, **kw):
        outs, got = fn(*args, comm=tuple(before(name)), **kw)
        after(name, got)
        return outs

    (xn,) = hosted("norm_x", _rmsnorm_bf16, x, g1, "norm_x")
    cw = small["conv_w"]
    wg = jnp.concatenate([_block_diag_groups(small["lru_wa"][0]), _block_diag_groups(small["lru_wx"][0]),
                          _block_diag_groups(small["lru_wa"][1]), _block_diag_groups(small["lru_wx"][1])],
                         axis=2).astype(bf16)
    zeros5 = jnp.zeros((5, D), f32)
    lp = jnp.stack([jnp.concatenate([small["lru_lambda"][d:d + 1], small["lru_ba"][d:d + 1],
                                     small["lru_bx"][d:d + 1], zeros5], axis=0) for d in range(2)])
    (proj,) = hosted("inproj", _matmul_t, xn, env["w_in_t"], "inproj", tm=4096, tn=512,
                     row_block=lambda j: jnp.where(j < 6, j, jnp.where(j < 10, j + 1, 6)))
    uc = _conv_fwd(proj, cw, cb)
    (hf,), _ = _lru_fwd(uc, wg, lp, False)
    (hb,), _ = _lru_fwd(uc, wg, lp, True)
    yb, attn_stats = hosted("attn_fwd", _attn_fwd, proj, sink)
    merged, x1 = _merge_outproj(x, hf, hb, proj, yb, bg, env["w_out"])
    (xn2, gu), _ = _norm_matmul(x1, g2, env["w_fi_t"], "norm_ffn_in", tn=D_FF)
    ff, dx2, dx2b, loss, dg3, dgt, dup = _ffn_out_loss(gu, x1, env["w_fo"], g3, tgt)

    env["dw_fo"] = _mm_tn(ff, dx2b, "dw_ffn_out", tk=1408, tn=1024)
    dx1, dx1b, dg2 = hosted("ffn_in_bwd", _proj_bwd, [(dgt, 0, 0, D_FF), (dup, 0, 1, D_FF)], env["w_fi_t"],
                            x1, g2, dx2, "ffn_in_bwd")
    dw_gate = _mm_tn(dgt, xn2, "dw_ffn_in_gate", tk=1408, tn=1024, out_rows=2 * D_FF)
    env["dw_fi_t"] = _mm_tn(dup, xn2, "dw_ffn_in_up", tk=1408, tn=1024, into=dw_gate, row=D_FF // 1408)
    env["dw_out"] = _mm_tn(merged, dx1b, "dw_out", tk=1024, tn=1024)
    dh, dgl, dz, dyb, dbg = _outproj_bwd(dx1b, env["w_out"], hf, hb, proj, yb, bg)
    dq, dkv, dsink = hosted("attn_bwd", _attn_bwd, proj, sink, dyb, attn_stats)
    duc_f, dwg_f, dp_f = hosted("lru_bwd", _lru_bwd, uc, dh, hf, wg, lp, False)
    (duc_b, dwg_b, dp_b), _ = _lru_bwd(uc, dh, hb, wg, lp, True)
    env["grads_early"] = {
        "loss": loss[:, :1], "b_gate": dbg,
        "lru_lambda": jnp.concatenate([dp_f[0:1], dp_b[0:1]], axis=0),
        "lru_wa": jnp.stack([_diag_blocks(dwg_f[:, :, :LRU_GW]), _diag_blocks(dwg_b[:, :, :LRU_GW])]),
        "lru_ba": jnp.concatenate([dp_f[1:2], dp_b[1:2]], axis=0),
        "lru_wx": jnp.stack([_diag_blocks(dwg_f[:, :, LRU_GW:]), _diag_blocks(dwg_b[:, :, LRU_GW:])]),
        "lru_bx": jnp.concatenate([dp_f[2:3], dp_b[2:3]], axis=0),
        "attn_sink": dsink[:, :N_HEADS], "norm_ffn_g": dg2, "norm_final_g": dg3,
    }
    du, dcw, dcb = hosted("conv_bwd", _conv_bwd, duc_f, duc_b, proj, cw)
    dw_in = _mm_tn(du, xn, "dw_in_u", tk=1024, tn=1024, out_rows=IN_W)
    dw_in = _mm_tn(dgl, xn, "dw_in_g", tk=1024, tn=1024, into=dw_in, row=1)
    dw_in = _mm_tn(dq, xn, "dw_in_q", tk=1024, tn=1024, into=dw_in, row=2)
    dw_in = _mm_tn(dkv, xn, "dw_in_kv", tk=512, tn=1024, into=dw_in, row=3072 // 512)
    env["dw_in_t"] = _mm_tn(dz, xn, "dw_in_z", tk=512, tn=1024, tmc=4096, into=dw_in, row=3584 // 512)
    col_pieces = [(du, 0, 0, D), (dgl, 0, 1, D), (dq, 0, 2, D), (dkv, 0, 3072 // 512, 512),
                  *[(dz, i, 3584 // 512 + i, 512) for i in range(4)]]
    dx, _, dg1 = hosted("inproj_bwd", _proj_bwd, col_pieces, env["w_in_t"], x, g1, dx1, "inproj_bwd")

    grads = dict(env["grads_early"], norm_mix_g=dg1, conv_w=dcw, conv_b=dcb)
    return dx, grads


def _adamw(gparts, w, m, v, name, tr=256):
    n, rows, cols = gparts.shape
    tr = _div_tile(rows, tr)
    c1 = 1.0 - ADAM_B1 ** ADAM_STEP
    c2 = 1.0 - ADAM_B2 ** ADAM_STEP

    def body(g_ref, w_ref, m_ref, v_ref, go_ref, d_ref, mo_ref, vo_ref):
        g = g_ref[0].astype(f32)
        for j in range(1, n):
            g = g + g_ref[j].astype(f32)
        mn = ADAM_B1 * m_ref[0] + (1.0 - ADAM_B1) * g
        vn = ADAM_B2 * v_ref[0] + (1.0 - ADAM_B2) * (g * g)
        m_hat = mn / c1
        v_hat = vn / c2
        go_ref[0] = g
        d_ref[0] = -ADAM_LR * (m_hat / (jnp.sqrt(v_hat) + ADAM_EPS) + ADAM_WD * w_ref[0])
        mo_ref[0] = mn
        vo_ref[0] = vn

    blk = pl.BlockSpec((1, tr, cols), lambda i: (0, i, 0))
    shp = jax.ShapeDtypeStruct((1, rows, cols), f32)
    return pl.pallas_call(
        body, name=name, grid=(rows // tr,),
        in_specs=[pl.BlockSpec((n, tr, cols), lambda i: (0, i, 0)), blk, blk, blk],
        out_specs=[blk, blk, blk, blk], out_shape=[shp, shp, shp, shp],
        compiler_params=_cparams())(gparts, w, m, v)


def _sum_parts(parts, name):
    n, rows, cols = parts.shape

    def body(p_ref, o_ref):
        acc = p_ref[0].astype(f32)
        for j in range(1, n):
            acc = acc + p_ref[j].astype(f32)
        o_ref[...] = acc

    return pl.pallas_call(
        body, name=name, out_shape=jax.ShapeDtypeStruct((rows, cols), f32),
        compiler_params=_cparams())(parts)


def _pack_rows(arrs, dtype=f32):
    rows, spans, at = [], [], 0
    for a in arrs:
        flat = a.reshape(-1).astype(dtype)
        nr = -(-flat.shape[0] // 1024)
        rows.append(jnp.pad(flat, (0, nr * 1024 - flat.shape[0])).reshape(nr, 1024))
        spans.append((at, nr))
        at += nr
    pad = (-at) % 16
    if pad:
        rows.append(jnp.zeros((pad, 1024), dtype))
    return jnp.concatenate(rows, axis=0), spans


def _unpack_rows(packed, spans, shapes):
    out = []
    for (at, nr), shp in zip(spans, shapes):
        n = math.prod(shp)
        out.append(packed[at:at + nr].reshape(-1)[:n].reshape(shp))
    return out


BIG = ("w_in", "w_out", "w_ffn_in", "w_ffn_out")
SMALL_REPL = ("norm_mix_g", "b_gate", "conv_b", "attn_sink", "norm_ffn_g", "norm_final_g")
GATE_W = ("lru_wa", "lru_wx")
SMALL_SHARD = ("conv_w", "lru_lambda", "lru_ba", "lru_bx")
ORDER = ("norm_mix_g", "w_in", "b_gate", "conv_w", "conv_b", "lru_lambda", "lru_wa", "lru_ba", "lru_wx",
         "lru_bx", "attn_sink", "w_out", "norm_ffn_g", "w_ffn_in", "w_ffn_out", "norm_final_g")
EARLY_F32 = ("loss", "b_gate", "lru_lambda", "lru_ba", "lru_bx", "attn_sink", "norm_ffn_g", "norm_final_g")
LATE = ("norm_mix_g", "conv_w", "conv_b")


def kernel(x, norm_mix_g, w_in, b_gate, conv_w, conv_b, lru_lambda, lru_wa, lru_ba, lru_wx, lru_bx, attn_sink, w_out, norm_ffn_g, w_ffn_in, w_ffn_out, norm_final_g, loss_target, m_norm_mix_g, m_w_in, m_b_gate, m_conv_w, m_conv_b, m_lru_lambda, m_lru_wa, m_lru_ba, m_lru_wx, m_lru_bx, m_attn_sink, m_w_out, m_norm_ffn_g, m_w_ffn_in, m_w_ffn_out, m_norm_final_g, v_norm_mix_g, v_w_in, v_b_gate, v_conv_w, v_conv_b, v_lru_lambda, v_lru_wa, v_lru_ba, v_lru_wx, v_lru_bx, v_attn_sink, v_w_out, v_norm_ffn_g, v_w_ffn_in, v_w_ffn_out, v_norm_final_g):
    w = dict(norm_mix_g=norm_mix_g, w_in=w_in, b_gate=b_gate, conv_w=conv_w, conv_b=conv_b, lru_lambda=lru_lambda,
             lru_wa=lru_wa, lru_ba=lru_ba, lru_wx=lru_wx, lru_bx=lru_bx, attn_sink=attn_sink, w_out=w_out,
             norm_ffn_g=norm_ffn_g, w_ffn_in=w_ffn_in, w_ffn_out=w_ffn_out, norm_final_g=norm_final_g)
    m = dict(norm_mix_g=m_norm_mix_g, w_in=m_w_in, b_gate=m_b_gate, conv_w=m_conv_w, conv_b=m_conv_b,
             lru_lambda=m_lru_lambda, lru_wa=m_lru_wa, lru_ba=m_lru_ba, lru_wx=m_lru_wx, lru_bx=m_lru_bx,
             attn_sink=m_attn_sink, w_out=m_w_out, norm_ffn_g=m_norm_ffn_g, w_ffn_in=m_w_ffn_in,
             w_ffn_out=m_w_ffn_out, norm_final_g=m_norm_final_g)
    v = dict(norm_mix_g=v_norm_mix_g, w_in=v_w_in, b_gate=v_b_gate, conv_w=v_conv_w, conv_b=v_conv_b,
             lru_lambda=v_lru_lambda, lru_wa=v_lru_wa, lru_ba=v_lru_ba, lru_wx=v_lru_wx, lru_bx=v_lru_bx,
             attn_sink=v_attn_sink, w_out=v_w_out, norm_ffn_g=v_norm_ffn_g, w_ffn_in=v_w_ffn_in,
             w_ffn_out=v_w_ffn_out, norm_final_g=v_norm_final_g)
    me = 4 * lax.axis_index("x") + 2 * lax.axis_index("y") + lax.axis_index("c")

    def shard_t(a):
        return jnp.swapaxes(a[0], 0, 1)

    def rows_parts(g):
        return g.reshape(N_DEV, -1, g.shape[1])

    shard_rows = jnp.concatenate([w[n][0] for n in SMALL_SHARD], axis=0)
    small = {n: w[n] for n in ("norm_mix_g", "b_gate", "conv_b", "attn_sink", "norm_ffn_g")}
    small["lru_wa"], small["lru_wx"] = lru_wa[0], lru_wx[0]
    small["norm_final_g"] = norm_final_g.reshape(1, D)
    env, recv = {}, {}

    def before(name):
        if name == "norm_x":
            return [(shard_t(w_in).astype(bf16), False), (shard_rows, False)]
        if name == "inproj":
            return [(w_out[0].astype(bf16), False), (w_ffn_out[0].astype(bf16), False)]
        if name == "attn_fwd":
            return [(shard_t(w_ffn_in).astype(bf16), False)]
        if name == "ffn_in_bwd":
            return [(rows_parts(env["dw_fo"]), True)]
        if name == "attn_bwd":
            return [(rows_parts(env["dw_out"]), True)]
        if name == "lru_bwd":
            return [(rows_parts(env["dw_fi_t"]), True)]
        if name == "conv_bwd":
            ge = env["grads_early"]
            p32, env["early_f32_spans"] = _pack_rows([ge[n] for n in EARLY_F32])
            return [(p32, False), *[(ge[n].astype(bf16).reshape(-1, LRU_BLOCK), False) for n in GATE_W]]
        if name == "inproj_bwd":
            return [(rows_parts(env["dw_in_t"]), True)]
        return []

    def after(name, got):
        if name == "norm_x":
            env["w_in_t"] = got[0].reshape(IN_W, D)
            full_rows = jnp.swapaxes(got[1], 0, 1).reshape(shard_rows.shape[0], -1)
            small["conv_w"], small["lru_lambda"] = full_rows[0:4], full_rows[4:6]
            small["lru_ba"], small["lru_bx"] = full_rows[6:8], full_rows[8:10]
        elif name == "inproj":
            env["w_out"], env["w_fo"] = got[0].reshape(D, D), got[1].reshape(D_FF, D)
        elif name == "attn_fwd":
            env["w_fi_t"] = got[0].reshape(2 * D_FF, D)
        elif name == "ffn_in_bwd":
            recv["w_ffn_out"] = got[0]
        elif name == "attn_bwd":
            recv["w_out"] = got[0]
        elif name == "lru_bwd":
            recv["w_ffn_in"] = got[0]
        elif name == "conv_bwd":
            recv["early_f32"], recv["lru_wa"], recv["lru_wx"] = got
        elif name == "inproj_bwd":
            recv["w_in"] = got[0]

    grad_x, grads = _local_step(x[0], loss_target[0], small, env, before, after)

    outs = {}
    for name in ("w_out", "w_ffn_out"):
        outs[name] = _adamw(recv[name], w[name], m[name], v[name], "adamw_" + name)
    for name in ("w_in", "w_ffn_in"):
        t = lambda a: jnp.swapaxes(a, 1, 2)
        outs[name] = [t(r) for r in _adamw(recv[name], t(w[name]), t(m[name]), t(v[name]), "adamw_" + name)]
    for name in GATE_W:
        t = lambda a: a.reshape(1, -1, LRU_BLOCK)
        res = _adamw(recv[name], t(w[name]), t(m[name]), t(v[name]), "adamw_" + name)
        outs[name] = [r.reshape(w[name].shape) for r in res]

    small_names = SMALL_REPL + SMALL_SHARD
    late_packed, late_spans = _pack_rows([grads[n] for n in LATE])
    (got_late,) = _exchange([(late_packed, False)], "gather_late_grads")
    summed = {}
    for names, got, spans, tag in ((EARLY_F32, recv["early_f32"], env["early_f32_spans"], "early_f32"),
                                   (LATE, got_late, late_spans, "late")):
        total = _sum_parts(got, "sum_small_" + tag)
        summed.update(zip(names, _unpack_rows(total, spans, [grads[n].shape for n in names])))
    loss = summed["loss"].reshape(())
    gsm = {n: summed[n].reshape(w[n].shape) for n in SMALL_REPL}
    for n in SMALL_SHARD:
        full = summed[n]
        gsm[n] = lax.dynamic_slice_in_dim(full, me * 128, 128, axis=1).reshape(w[n].shape)
    pk = lambda dct: _pack_rows([dct[n] for n in small_names])[0]
    gp, sp = _pack_rows([gsm[n] for n in small_names])
    res = _adamw(gp[None], pk(w)[None], pk(m)[None], pk(v)[None], "adamw_small")
    sshapes = [w[n].shape for n in small_names]
    for idx, t in enumerate(res):
        for n, a in zip(small_names, _unpack_rows(t[0], sp, sshapes)):
            outs.setdefault(n, [None] * 4)[idx] = a

    result = [loss, grad_x[None]]
    for idx in range(4):
        result += [outs[n][idx] for n in ORDER]
    return tuple(result)
```

```python
import functools
import math

import jax
import jax.numpy as jnp
from jax import lax
from jax.experimental import pallas as pl
from jax.experimental.pallas import tpu as pltpu

f32 = jnp.float32
bf16 = jnp.bfloat16

D = 1024
D_FF = 2816
IN_W = 5632
N_HEADS = 16
N_KV = 4
HEAD_DIM = 64
WINDOW = 128
BLK = 128
LRU_HEADS = 16
LRU_BLOCK = 64
LRU_GROUPS = 4
LRU_GW = 256
LRU_CHUNK = 64
LRU_CHUNK_BWD = 512
LRU_ROWS = 4096
RGLRU_C = 8.0
EPS = 1e-6
NEG_INF = -1e30
N_DEV = 8

ADAM_LR = 0.001
ADAM_B1 = 0.9
ADAM_B2 = 0.999
ADAM_EPS = 1e-08
ADAM_WD = 0.01
ADAM_STEP = 10

VMEM_MB = 56

C_U, C_G, C_Q, C_Z0, C_Z1, C_K, C_V = 0, 1024, 2048, 3072, 4096, 5120, 5376


def _cparams(vmem_mb=VMEM_MB):
    return pltpu.CompilerParams(vmem_limit_bytes=vmem_mb << 20)


def _div_tile(n, pref):
    if n <= pref:
        return n
    return max(t for t in range(8, pref + 1, 8) if n % t == 0)


def _sigmoid(x):
    return 0.5 * jnp.tanh(0.5 * x) + 0.5


def _log1p(x):
    u = 1.0 + x
    d = u - 1.0
    return jnp.where(d == 0.0, x, jnp.log(u) * (x / jnp.where(d == 0.0, 1.0, d)))


def _softplus(x):
    return jnp.maximum(x, 0.0) + _log1p(jnp.exp(-jnp.abs(x)))


def _gelu_and_grad(x):
    c = math.sqrt(2.0 / math.pi)
    inner = c * (x + 0.044715 * (x * x * x))
    t = jnp.tanh(inner)
    gelu = 0.5 * x * (1.0 + t)
    dinner = c * (1.0 + 3 * 0.044715 * (x * x))
    dgelu = 0.5 * (1.0 + t) + 0.5 * x * (1.0 - t * t) * dinner
    return gelu, dgelu


def _rms_bwd(dn, xv, g):
    r = lax.rsqrt(jnp.mean(xv * xv, axis=-1, keepdims=True) + EPS)
    xh = xv * r
    dxh = dn * g
    dx = r * (dxh - xh * jnp.mean(dxh * xh, axis=-1, keepdims=True))
    return dx, dn * xh


ANY_SPEC = pl.BlockSpec(memory_space=pl.ANY)


def _comm_out_shape(src, scatter):
    return jax.ShapeDtypeStruct((N_DEV, *(src.shape[1:] if scatter else src.shape)), src.dtype)


def _comm_sems():
    return [pltpu.SemaphoreType.DMA((N_DEV - 1,)), pltpu.SemaphoreType.DMA((N_DEV - 1,)), pltpu.SemaphoreType.DMA]


def _scatter_descs(src_ref, out_ref, send_sems, recv_sems, local_sem):
    x, y, c = lax.axis_index("x"), lax.axis_index("y"), lax.axis_index("c")
    me = 4 * x + 2 * y + c
    descs = [pltpu.make_async_copy(src_ref.at[me], out_ref.at[me], local_sem)]
    for k in range(1, N_DEV):
        px, py, pc = x ^ (k >> 2), y ^ ((k >> 1) & 1), c ^ (k & 1)
        descs.append(pltpu.make_async_remote_copy(
            src_ref=src_ref.at[4 * px + 2 * py + pc], dst_ref=out_ref.at[me],
            send_sem=send_sems.at[k - 1], recv_sem=recv_sems.at[k - 1],
            device_id=(px, py, pc), device_id_type=pl.DeviceIdType.MESH))
    return descs


def _gather_copies(src_ref, out_ref, send_sems, recv_sems, local_sem, which):
    x, y, c = lax.axis_index("x"), lax.axis_index("y"), lax.axis_index("c")
    me, sibling = (x, y, c), (x, y, 1 - c)
    chips = [(1 - x, y), (x, 1 - y), (1 - x, 1 - y)]

    def slot(px, py, pc):
        return out_ref.at[4 * px + 2 * py + pc]

    def copy(k, block, to, src=None):
        return pltpu.make_async_remote_copy(
            src_ref=slot(*block) if src is None else src, dst_ref=slot(*block),
            send_sem=send_sems.at[k], recv_sem=recv_sems.at[k], device_id=to, device_id_type=pl.DeviceIdType.MESH)

    make = {
        "local": lambda: pltpu.make_async_copy(src_ref, slot(*me), local_sem),
        "first": lambda: [copy(0, me, sibling, src=src_ref)] + [copy(1 + j, me, (*chip, c), src=src_ref)
                                                                 for j, chip in enumerate(chips)],
        "passed": lambda: [copy(4 + j, (*chip, c), sibling) for j, chip in enumerate(chips)],
        "landed": lambda: [copy(1 + j, (*chip, c), me) for j, chip in enumerate(chips)],
        "later": lambda: [copy(0, sibling, me)] + [copy(4 + j, (*chip, 1 - c), me) for j, chip in enumerate(chips)],
    }
    return [make[name]() for name in which]


def _comm_start(src_ref, out_ref, sems, scatter):
    if scatter:
        for d in _scatter_descs(src_ref, out_ref, *sems):
            d.start()
    else:
        local, first = _gather_copies(src_ref, out_ref, *sems, which=("local", "first"))
        local.start()
        for cp in first:
            cp.start()


def _comm_pass_on(src_ref, out_ref, sems, scatter):
    if not scatter:
        landed, passed = _gather_copies(src_ref, out_ref, *sems, which=("landed", "passed"))
        for arrived, onward in zip(landed, passed):
            arrived.wait_recv()
            onward.start()


def _comm_finish(src_ref, out_ref, sems, scatter):
    if scatter:
        for d in _scatter_descs(src_ref, out_ref, *sems):
            d.wait()
    else:
        later, first, passed, local = _gather_copies(src_ref, out_ref, *sems,
                                                     which=("later", "first", "passed", "local"))
        for cp in later:
            cp.wait_recv()
        for cp in first + passed:
            cp.wait_send()
        local.wait()


def _exchange(comm, name):
    nc = len(comm)

    def body(*refs):
        srcs, outs, sems = refs[:nc], refs[nc:2 * nc], refs[2 * nc:]
        for stage in (_comm_start, _comm_pass_on, _comm_finish):
            for i in range(nc):
                stage(srcs[i], outs[i], sems[3 * i:3 * i + 3], comm[i][1])

    return pl.pallas_call(
        body, name=name, in_specs=[ANY_SPEC] * nc, out_specs=[ANY_SPEC] * nc,
        out_shape=[_comm_out_shape(*c) for c in comm],
        scratch_shapes=[s for _ in comm for s in _comm_sems()],
    )(*[c[0] for c in comm])


def _hosted_call(body, *, name, grid, in_specs, out_specs, out_shape, args, scratch_shapes=(), comm=()):
    nin, nout, nscr, nc = len(in_specs), len(out_specs), len(scratch_shapes), len(comm)
    steps = math.prod(grid)

    def wrapped(*refs):
        ins = refs[:nin]
        csrc = refs[nin:nin + nc]
        outs = refs[nin + nc:nin + nc + nout]
        cout = refs[nin + nc + nout:nin + 2 * nc + nout]
        scr = refs[nin + 2 * nc + nout:]
        sems = scr[nscr:]

        def at(step, stage):
            lin = 0
            for a in range(len(grid)):
                lin = lin * grid[a] + pl.program_id(a)

            @pl.when(lin == step)
            def _():
                for i in range(nc):
                    stage(csrc[i], cout[i], sems[3 * i:3 * i + 3], comm[i][1])

        if nc:
            at(0, _comm_start)

        body(*ins, *outs, *scr[:nscr])

        if nc:
            at((3 * (steps - 1)) // 4, _comm_pass_on)
            at(steps - 1, _comm_finish)

    res = pl.pallas_call(
        wrapped, name=name, grid=grid,
        in_specs=[*in_specs, *[ANY_SPEC] * nc], out_specs=[*out_specs, *[ANY_SPEC] * nc],
        out_shape=[*out_shape, *[_comm_out_shape(*c) for c in comm]],
        scratch_shapes=[*scratch_shapes, *[s for _ in comm for s in _comm_sems()]],
        compiler_params=_cparams())(*args, *[c[0] for c in comm])
    return res[:nout], res[nout:]


def _rmsnorm_bf16(x, g, name, tm=1024, comm=()):
    S, dm = x.shape
    tm = min(tm, S)

    def body(x_ref, g_ref, xn_ref):
        xv = x_ref[...]
        r = lax.rsqrt(jnp.mean(xv * xv, axis=-1, keepdims=True) + EPS)
        xn_ref[...] = ((xv * r) * g_ref[...]).astype(bf16)

    row = pl.BlockSpec((tm, dm), lambda i: (i, 0))
    return _hosted_call(
        body, name=name, grid=(S // tm,), in_specs=[row, pl.BlockSpec((1, dm), lambda i: (0, 0))],
        out_specs=[row], out_shape=[jax.ShapeDtypeStruct((S, dm), bf16)], args=(x, g), comm=comm)


def _matmul_t(a, wt, name, tm=2048, tn=512, row_block=lambda j: j, comm=()):
    S, dm = a.shape
    n = wt.shape[0]
    tm = min(tm, S)

    def body(a_ref, w_ref, o_ref):
        o_ref[...] = lax.dot_general(a_ref[...], w_ref[...], (((1,), (1,)), ((), ())),
                                     preferred_element_type=f32).astype(bf16)

    return _hosted_call(
        body, name=name, grid=(S // tm, n // tn),
        in_specs=[pl.BlockSpec((tm, dm), lambda i, j: (i, 0)),
                  pl.BlockSpec((tn, dm), lambda i, j: (row_block(j), 0))],
        out_specs=[pl.BlockSpec((tm, tn), lambda i, j: (i, j))],
        out_shape=[jax.ShapeDtypeStruct((S, n), bf16)], args=(a, wt), comm=comm)


def _norm_matmul(x, g, wt, name, tm=1024, tn=1408, row_block=lambda j: j, comm=()):
    S, dm = x.shape
    n = wt.shape[0]
    tm = min(tm, S)

    def body(x_ref, g_ref, w_ref, xn_ref, o_ref):
        @pl.when(pl.program_id(1) == 0)
        def _():
            xv = x_ref[...]
            r = lax.rsqrt(jnp.mean(xv * xv, axis=-1, keepdims=True) + EPS)
            xn_ref[...] = ((xv * r) * g_ref[...]).astype(bf16)

        o_ref[...] = lax.dot_general(xn_ref[...], w_ref[...], (((1,), (1,)), ((), ())),
                                     preferred_element_type=f32).astype(bf16)

    return _hosted_call(
        body, name=name, grid=(S // tm, n // tn),
        in_specs=[pl.BlockSpec((tm, dm), lambda i, j: (i, 0)),
                  pl.BlockSpec((1, dm), lambda i, j: (0, 0)),
                  pl.BlockSpec((tn, dm), lambda i, j: (row_block(j), 0))],
        out_specs=[pl.BlockSpec((tm, dm), lambda i, j: (i, 0)),
                   pl.BlockSpec((tm, tn), lambda i, j: (i, j))],
        out_shape=[jax.ShapeDtypeStruct((S, dm), bf16), jax.ShapeDtypeStruct((S, n), bf16)],
        args=(x, g, wt), comm=comm)


def _mm_tn(a, b, name, tk, tn, tmc=2048, into=None, row=0, out_rows=None):
    m, ka = a.shape
    n = b.shape[1]
    tmc = min(tmc, m)
    nk = m // tmc

    def body(a_ref, b_ref, *rest):
        o_ref, acc_ref = rest[-2:]
        k = pl.program_id(2)
        part = lax.dot_general(a_ref[...], b_ref[...], (((0,), (0,)), ((), ())), preferred_element_type=f32)

        @pl.when(k == 0)
        def _():
            acc_ref[...] = part

        @pl.when(k > 0)
        def _():
            acc_ref[...] += part

        @pl.when(k == nk - 1)
        def _():
            o_ref[...] = acc_ref[...].astype(bf16)

    in_specs = [pl.BlockSpec((tmc, tk), lambda i, j, k: (k, i)), pl.BlockSpec((tmc, tn), lambda i, j, k: (k, j))]
    if into is None:
        return pl.pallas_call(
            body, name=name, grid=(ka // tk, n // tn, nk), in_specs=in_specs,
            out_specs=pl.BlockSpec((tk, tn), lambda i, j, k: (i + row, j)),
            out_shape=jax.ShapeDtypeStruct((out_rows or ka, n), bf16),
            scratch_shapes=[pltpu.VMEM((tk, tn), f32)],
            compiler_params=_cparams())(a, b)
    return pl.pallas_call(
        body, name=name, grid=(ka // tk, n // tn, nk), in_specs=[*in_specs, ANY_SPEC],
        out_specs=pl.BlockSpec((tk, tn), lambda i, j, k: (i + row, j)),
        out_shape=jax.ShapeDtypeStruct(into.shape, into.dtype),
        scratch_shapes=[pltpu.VMEM((tk, tn), f32)], input_output_aliases={2: 0},
        compiler_params=_cparams())(a, b, into)


HALO = 16


def _rows_at(ext, o, tc):
    if o == 0:
        return ext[HALO:HALO + tc]
    return pltpu.roll(ext, (-o) % ext.shape[0], 0)[HALO:HALO + tc]


def _halo_specs(tc, S, width, col):
    per = tc // HALO
    last = S // HALO - 1
    return (pl.BlockSpec((tc, width), lambda i: (i, col)),
            pl.BlockSpec((HALO, width), lambda i: (jnp.maximum(i * per - 1, 0), col)),
            pl.BlockSpec((HALO, width), lambda i: (jnp.minimum((i + 1) * per, last), col)))


def _extended(cur_ref, prev_ref, next_ref, i, nsteps):
    prev = jnp.where(i > 0, prev_ref[...].astype(f32), 0.0)
    nxt = jnp.where(i < nsteps - 1, next_ref[...].astype(f32), 0.0)
    return jnp.concatenate([prev, cur_ref[...].astype(f32), nxt], axis=0)


def _conv_fwd(proj, cw, cb, tc=1024):
    S = proj.shape[0]
    tc = min(tc, S)
    nsteps = S // tc

    def body(cur_ref, prev_ref, next_ref, w_ref, b_ref, o_ref):
        ext = _extended(cur_ref, prev_ref, next_ref, pl.program_id(0), nsteps)
        acc = _rows_at(ext, -2, tc) * w_ref[0:1, :]
        for k in range(1, 4):
            acc = acc + _rows_at(ext, k - 2, tc) * w_ref[k:k + 1, :]
        o_ref[...] = acc + b_ref[...]

    return pl.pallas_call(
        body, name="conv_fwd", grid=(nsteps,),
        in_specs=[*_halo_specs(tc, S, D, 0),
                  pl.BlockSpec((4, D), lambda i: (0, 0)), pl.BlockSpec((1, D), lambda i: (0, 0))],
        out_specs=pl.BlockSpec((tc, D), lambda i: (i, 0)),
        out_shape=jax.ShapeDtypeStruct((S, D), f32),
        compiler_params=_cparams())(proj, proj, proj, cw, cb)


def _conv_bwd(duc_f, duc_b, proj, cw, tc=1024, comm=()):
    S = proj.shape[0]
    tc = min(tc, S)
    nsteps = S // tc

    def body(fc, fp, fn, bc, bp, bn, uc_, up, un, w_ref, du_ref, dw_ref, db_ref):
        i = pl.program_id(0)

        @pl.when(i == 0)
        def _():
            dw_ref[...] = jnp.zeros_like(dw_ref)
            db_ref[...] = jnp.zeros_like(db_ref)

        dext = _extended(fc, fp, fn, i, nsteps) + _extended(bc, bp, bn, i, nsteps)
        uext = _extended(uc_, up, un, i, nsteps)
        d = dext[HALO:HALO + tc]
        acc = _rows_at(dext, 2, tc) * w_ref[0:1, :]
        for k in range(1, 4):
            acc = acc + _rows_at(dext, 2 - k, tc) * w_ref[k:k + 1, :]
        du_ref[...] = acc.astype(bf16)
        wrow = lax.broadcasted_iota(jnp.int32, (4, D), 0)
        for k in range(4):
            dw_ref[...] += jnp.where(wrow == k, jnp.sum(d * _rows_at(uext, k - 2, tc), axis=0, keepdims=True), 0.0)
        db_ref[...] += jnp.sum(d, axis=0, keepdims=True)

    return _hosted_call(
        body, name="conv_bwd", grid=(nsteps,),
        in_specs=[*_halo_specs(tc, S, D, 0), *_halo_specs(tc, S, D, 0), *_halo_specs(tc, S, D, 0),
                  pl.BlockSpec((4, D), lambda i: (0, 0))],
        out_specs=[pl.BlockSpec((tc, D), lambda i: (i, 0)),
                   pl.BlockSpec((4, D), lambda i: (0, 0)), pl.BlockSpec((1, D), lambda i: (0, 0))],
        out_shape=[jax.ShapeDtypeStruct((S, D), bf16), jax.ShapeDtypeStruct((4, D), f32),
                   jax.ShapeDtypeStruct((1, D), f32)],
        args=(duc_f, duc_f, duc_f, duc_b, duc_b, duc_b, proj, proj, proj, cw), comm=comm)


def _scan_scratch(tc):
    halves = [pltpu.VMEM((tc, 128), f32) for _ in range(2 * (LRU_GW // 128))]
    return [*halves, pltpu.VMEM((tc // 8, LRU_GW), f32), pltpu.VMEM((tc // 8, LRU_GW), f32)]


def _log_scan(a, b, row, n, reverse, steps):
    for s in steps:
        shift = a.shape[0] - s if reverse else s
        keep = (row < n - s) if reverse else (row >= s)
        a_sh = pltpu.roll(a, shift, 0)
        b_sh = pltpu.roll(b, shift, 0)
        b = jnp.where(keep, a * b_sh + b, b)
        a = jnp.where(keep, a * a_sh, a)
    return a, b


def _scan_chunk(a, b, carry, reverse, *scratch):
    tc, w = a.shape
    ng = tc // 8
    nl = w // 128
    sa_refs, sb_refs, sc_ref, st_ref = scratch[:nl], scratch[nl:2 * nl], scratch[2 * nl], scratch[2 * nl + 1]
    sub = lax.broadcasted_iota(jnp.int32, (8, w), 0)
    ag, bg = [], []
    for k in range(ng):
        ak, bk = _log_scan(a[8 * k:8 * k + 8], b[8 * k:8 * k + 8], sub, 8, reverse, (1, 2, 4))
        ag.append(ak)
        bg.append(bk)
    a = jnp.concatenate(ag, axis=0)
    b = jnp.concatenate(bg, axis=0)
    edge = 0 if reverse else 7
    for i in range(nl):
        sa_refs[i][...] = a[:, 128 * i:128 * (i + 1)]
        sb_refs[i][...] = b[:, 128 * i:128 * (i + 1)]
    ta = jnp.concatenate([r[pl.ds(edge, ng, stride=8), :] for r in sa_refs], axis=1)
    tb = jnp.concatenate([r[pl.ds(edge, ng, stride=8), :] for r in sb_refs], axis=1)
    grow = lax.broadcasted_iota(jnp.int32, (ng, w), 0)
    ta, tb = _log_scan(ta, tb, grow, ng, reverse, [1 << i for i in range(ng.bit_length() - 1)])
    state = tb + ta * carry
    st_ref[...] = state
    if reverse:
        sc_ref[...] = jnp.where(grow == ng - 1, carry, pltpu.roll(state, ng - 1, 0))
    else:
        sc_ref[...] = jnp.where(grow == 0, carry, pltpu.roll(state, 1, 0))
    h = jnp.concatenate([bg[k] + ag[k] * sc_ref[k:k + 1, :] for k in range(ng)], axis=0)
    return h, (st_ref[0:1, :] if reverse else st_ref[ng - 1:ng, :])


def _lru_gates(uc, w, p_ref):
    pre = jnp.dot(uc.astype(bf16), w, preferred_element_type=f32)
    r = _sigmoid(pre[:, :LRU_GW] + p_ref[0, 1:2, :])
    gi = _sigmoid(pre[:, LRU_GW:] + p_ref[0, 2:3, :])
    sp = _softplus(-p_ref[0, 0:1, :])
    log_a = -RGLRU_C * r * sp
    a = jnp.exp(log_a)
    x = 2.0 * log_a
    series = -x * (1.0 + x * (0.5 + x * (1.0 / 6 + x * (1.0 / 24))))
    beta = jnp.sqrt(jnp.maximum(jnp.where(x > -0.0625, series, 1.0 - a * a), 0.0))
    return r, gi, sp, a, beta


def _lru_fwd(uc, wg, lp, reverse, comm=()):
    S = uc.shape[0]
    tc = LRU_CHUNK
    rows = min(LRU_ROWS, S)
    nsub = rows // tc
    nblk = S // rows
    d = 1 if reverse else 0

    def bidx(c):
        return nblk - 1 - c if reverse else c

    def body(uc_ref, w_ref, p_ref, h_ref, carry_ref, *scan_scratch):
        @pl.when(pl.program_id(1) == 0)
        def _():
            carry_ref[...] = jnp.zeros_like(carry_ref)

        carry = carry_ref[...]
        for j in (reversed(range(nsub)) if reverse else range(nsub)):
            sl = slice(j * tc, (j + 1) * tc)
            ucv = uc_ref[sl, :]
            _, gi, _, a, beta = _lru_gates(ucv, w_ref[0], p_ref)
            h, carry = _scan_chunk(a, beta * (gi * ucv), carry, reverse, *scan_scratch)
            h_ref[sl, :] = h.astype(bf16)
        carry_ref[...] = carry

    return _hosted_call(
        body, name="lru_fwd_rev" if reverse else "lru_fwd", grid=(LRU_GROUPS, nblk),
        in_specs=[pl.BlockSpec((rows, LRU_GW), lambda g, c: (bidx(c), g)),
                  pl.BlockSpec((1, LRU_GW, 2 * LRU_GW), lambda g, c: (g, 0, d)),
                  pl.BlockSpec((1, 8, LRU_GW), lambda g, c: (d, 0, g))],
        out_specs=[pl.BlockSpec((rows, LRU_GW), lambda g, c: (bidx(c), g))],
        out_shape=[jax.ShapeDtypeStruct((S, D), bf16)],
        scratch_shapes=[pltpu.VMEM((1, LRU_GW), f32), *_scan_scratch(tc)],
        args=(uc, wg, lp), comm=comm)


def _lru_bwd(uc, dh, h, wg, lp, reverse, comm=()):
    S = uc.shape[0]
    tc = LRU_CHUNK_BWD
    rows = min(LRU_ROWS, S)
    nsub = rows // tc
    nblk = S // rows
    d = 1 if reverse else 0
    per = rows // HALO
    last8 = S // HALO - 1

    def bidx(c):
        return c if reverse else nblk - 1 - c

    def halo_idx(c):
        if reverse:
            return jnp.minimum((bidx(c) + 1) * per, last8)
        return jnp.maximum(bidx(c) * per - 1, 0)

    def body(uc_ref, dh_ref, h_ref, halo_ref, w_ref, p_ref, duc_ref, dw_ref, dp_ref, carry_ref, tmp_ref,
             *scan_scratch):
        c = pl.program_id(1)
        bi = bidx(c)

        @pl.when(c == 0)
        def _():
            carry_ref[...] = jnp.zeros_like(carry_ref)
            dw_ref[...] = jnp.zeros_like(dw_ref)
            dp_ref[...] = jnp.zeros_like(dp_ref)

        row = lax.broadcasted_iota(jnp.int32, (tc, LRU_GW), 0)
        carry = carry_ref[...]
        dw = jnp.zeros((LRU_GW, 2 * LRU_GW), f32)
        dsp = jnp.zeros((1, LRU_GW), f32)
        dba = jnp.zeros((1, LRU_GW), f32)
        dbx = jnp.zeros((1, LRU_GW), f32)
        for j in (range(nsub) if reverse else reversed(range(nsub))):
            sl = slice(j * tc, (j + 1) * tc)
            ucv = uc_ref[sl, :]
            ucb = ucv.astype(bf16)
            r, gi, sp, a, beta = _lru_gates(ucv, w_ref[0], p_ref)
            hv = h_ref[sl, :].astype(f32)
            dhv = dh_ref[sl, :].astype(f32)
            if reverse:
                alpha = jnp.where(row == 0, 1.0, pltpu.roll(a, 1, 0))
                gsc, _ = _scan_chunk(alpha, dhv, carry, False, *scan_scratch)
                if j < nsub - 1:
                    edge = h_ref[(j + 1) * tc:(j + 1) * tc + HALO, :].astype(f32)[0:1, :]
                else:
                    edge = jnp.where(bi < nblk - 1, halo_ref[...].astype(f32)[0:1, :], 0.0)
                h_nb = jnp.where(row == tc - 1, edge, pltpu.roll(hv, tc - 1, 0))
            else:
                alpha = jnp.where(row == tc - 1, 1.0, pltpu.roll(a, tc - 1, 0))
                gsc, _ = _scan_chunk(alpha, dhv, carry, True, *scan_scratch)
                if j > 0:
                    edge = h_ref[j * tc - HALO:j * tc, :].astype(f32)[HALO - 1:HALO, :]
                else:
                    edge = jnp.where(bi > 0, halo_ref[...].astype(f32)[HALO - 1:HALO, :], 0.0)
                h_nb = jnp.where(row == 0, edge, pltpu.roll(hv, 1, 0))
            tmp_ref[...] = a * gsc
            carry = tmp_ref[tc - 1:tc, :] if reverse else tmp_ref[0:1, :]

            da = gsc * h_nb
            dbeta = gsc * (gi * ucv)
            dl = da * a - dbeta * (a * a) / beta
            dr = dl * (-RGLRU_C * sp)
            dsp = dsp + jnp.sum(dl * (-RGLRU_C * r), axis=0, keepdims=True)
            dgi = gsc * beta * ucv
            dpre_r = dr * r * (1.0 - r)
            dpre_i = dgi * gi * (1.0 - gi)
            dba = dba + jnp.sum(dpre_r, axis=0, keepdims=True)
            dbx = dbx + jnp.sum(dpre_i, axis=0, keepdims=True)
            dpre = jnp.concatenate([dpre_r, dpre_i], axis=1).astype(bf16)
            back = lax.dot_general(dpre, w_ref[0], (((1,), (1,)), ((), ())), preferred_element_type=f32)
            duc_ref[sl, :] = (gsc * beta * gi + back).astype(bf16)
            dw = dw + lax.dot_general(ucb, dpre, (((0,), (0,)), ((), ())), preferred_element_type=f32)
        carry_ref[...] = carry
        dw_ref[0] += dw
        dlam = -dsp / (1.0 + jnp.exp(p_ref[0, 0:1, :]))
        prow = lax.broadcasted_iota(jnp.int32, (8, LRU_GW), 0)
        dp_ref[...] += (jnp.where(prow == 0, dlam, 0.0) + jnp.where(prow == 1, dba, 0.0)
                        + jnp.where(prow == 2, dbx, 0.0))

    chunk = pl.BlockSpec((rows, LRU_GW), lambda g, c: (bidx(c), g))
    return _hosted_call(
        body, name="lru_bwd_rev" if reverse else "lru_bwd", grid=(LRU_GROUPS, nblk),
        in_specs=[chunk, chunk, chunk,
                  pl.BlockSpec((HALO, LRU_GW), lambda g, c: (halo_idx(c), g)),
                  pl.BlockSpec((1, LRU_GW, 2 * LRU_GW), lambda g, c: (g, 0, d)),
                  pl.BlockSpec((1, 8, LRU_GW), lambda g, c: (d, 0, g))],
        out_specs=[chunk,
                   pl.BlockSpec((1, LRU_GW, 2 * LRU_GW), lambda g, c: (g, 0, 0)),
                   pl.BlockSpec((8, LRU_GW), lambda g, c: (0, g))],
        out_shape=[jax.ShapeDtypeStruct((S, D), bf16),
                   jax.ShapeDtypeStruct((LRU_GROUPS, LRU_GW, 2 * LRU_GW), f32),
                   jax.ShapeDtypeStruct((8, D), f32)],
        scratch_shapes=[pltpu.VMEM((1, LRU_GW), f32), pltpu.VMEM((tc, LRU_GW), f32), *_scan_scratch(tc)],
        args=(uc, dh, h, h, wg, lp), comm=comm)


def _slope(h):
    return 2.0 ** (-8.0 * (h + 1.0) / N_HEADS)


ATT_QB = 4


def _kv_specs(nb, col):
    return [pl.BlockSpec((BLK, N_KV * HEAD_DIM), lambda n: (jnp.maximum(ATT_QB * n - 1, 0), col)),
            pl.BlockSpec((ATT_QB * BLK, N_KV * HEAD_DIM), lambda n: (n, col)),
            pl.BlockSpec((BLK, N_KV * HEAD_DIM), lambda n: (jnp.minimum(ATT_QB * (n + 1), nb - 1), col))]


def _key_blocks(prev_ref, cur_ref, next_ref):
    return [prev_ref[...], *[cur_ref[BLK * s:BLK * (s + 1), :] for s in range(ATT_QB)], next_ref[...]]


def _dup_windows(r0, r1, r2):
    left = lax.broadcasted_iota(jnp.int32, (3 * BLK, 128), 1) < HEAD_DIM
    win = jnp.concatenate([r0, r1, r2], axis=0)
    out = []
    for i in range(N_KV // 2):
        t = win[:, i * 128:(i + 1) * 128]
        r = pltpu.roll(t, HEAD_DIM, 1)
        out += [jnp.where(left, t, r).astype(bf16), jnp.where(left, r, t).astype(bf16)]
    return out


def _attn_bias_init(bias_ref):
    k_loc = lax.broadcasted_iota(jnp.int32, (3 * BLK, BLK), 0)
    q_loc = lax.broadcasted_iota(jnp.int32, (3 * BLK, BLK), 1)
    adist = jnp.abs(q_loc + BLK - k_loc)
    adf = adist.astype(f32)
    for e in range(3):
        ok = adist <= WINDOW
        if e == 0:
            ok = ok & (k_loc >= BLK)
        if e == 2:
            ok = ok & (k_loc < 2 * BLK)
        for kv in range(N_KV):
            bias_ref[e, kv] = jnp.concatenate(
                [jnp.where(ok, (-_slope(4 * kv + j)) * adf, NEG_INF) for j in range(4)], axis=1)


def _stack_heads(ref, sub, kv, scale):
    left = lax.broadcasted_iota(jnp.int32, (BLK, 128), 1) < HEAD_DIM
    rows = []
    for pp in range(2):
        t = ref[BLK * sub:BLK * (sub + 1), (2 * kv + pp) * 128:(2 * kv + pp + 1) * 128]
        if scale != 1.0:
            t = t * scale
        zero = jnp.zeros_like(t)
        rows += [jnp.where(left, t, zero).astype(bf16), jnp.where(left, zero, t).astype(bf16)]
    return jnp.concatenate(rows, axis=0)


def _attn_softmax(qs, k2, bias, sink_ref, kv, stats=None):
    sink = jnp.concatenate([jnp.full((1, BLK), sink_ref[0, 4 * kv + j], f32) for j in range(4)], axis=1)
    s = lax.dot_general(k2, qs, (((1,), (1,)), ((), ())), preferred_element_type=f32) + bias
    m = jnp.maximum(jnp.max(s, axis=0, keepdims=True), sink) if stats is None else stats[0]
    p = jnp.exp(s - m)
    ps = jnp.exp(sink - m)
    inv = 1.0 / (jnp.sum(p, axis=0, keepdims=True) + ps) if stats is None else stats[1]
    return p, ps, m, inv


def _pair_tiles(t):
    return [jnp.concatenate([t[:HEAD_DIM, 256 * pp:256 * pp + 128],
                             t[HEAD_DIM:, 256 * pp + 128:256 * pp + 256]], axis=0).T for pp in range(2)]


def _attn_fwd(proj, sink, comm=()):
    S = proj.shape[0]
    nb = S // BLK
    assert nb >= 2 and nb % ATT_QB == 0

    def body(q_ref, k0, k1, k2_, v0, v1, v2_, sink_ref, o_ref, st_ref, bias_ref):
        n = pl.program_id(0)

        @pl.when(n == 0)
        def _():
            _attn_bias_init(bias_ref)

        kb = _key_blocks(k0, k1, k2_)
        vb = _key_blocks(v0, v1, v2_)
        for sub in range(ATT_QB):
            blk = ATT_QB * n + sub
            e = jnp.where(blk == 0, 0, jnp.where(blk == nb - 1, 2, 1))
            kk = _dup_windows(*kb[sub:sub + 3])
            vv = _dup_windows(*vb[sub:sub + 3])
            tiles = []
            for kv in range(N_KV):
                qs = _stack_heads(q_ref, sub, kv, HEAD_DIM ** -0.5)
                p, _, m, inv = _attn_softmax(qs, kk[kv], bias_ref[e, kv], sink_ref, kv)
                st_ref[sub, kv:kv + 1, :] = m
                st_ref[sub, N_KV + kv:N_KV + kv + 1, :] = inv
                ot = lax.dot_general(vv[kv], p.astype(bf16), (((0,), (0,)), ((), ())), preferred_element_type=f32)
                tiles += _pair_tiles(ot * inv)
            o_ref[BLK * sub:BLK * (sub + 1), :] = jnp.concatenate(tiles, axis=1).astype(bf16)

    return _hosted_call(
        body, name="attn_fwd", grid=(nb // ATT_QB,),
        in_specs=[pl.BlockSpec((ATT_QB * BLK, D), lambda n: (n, C_Q // D)),
                  *_kv_specs(nb, C_K // (N_KV * HEAD_DIM)), *_kv_specs(nb, C_V // (N_KV * HEAD_DIM)),
                  pl.BlockSpec(memory_space=pltpu.SMEM)],
        out_specs=[pl.BlockSpec((ATT_QB * BLK, D), lambda n: (n, 0)),
                   pl.BlockSpec((ATT_QB, 2 * N_KV, 4 * BLK), lambda n: (n, 0, 0))],
        out_shape=[jax.ShapeDtypeStruct((S, D), bf16), jax.ShapeDtypeStruct((nb, 2 * N_KV, 4 * BLK), f32)],
        scratch_shapes=[pltpu.VMEM((3, N_KV, 3 * BLK, 4 * BLK), f32)],
        args=(proj, proj, proj, proj, proj, proj, proj, sink), comm=comm)


def _attn_bwd(proj, sink, dyb, stats, comm=()):
    S = proj.shape[0]
    nb = S // BLK
    assert nb >= 2 and nb % ATT_QB == 0
    nsteps = nb // ATT_QB
    kvw = N_KV * HEAD_DIM

    def body(q_ref, k0, k1, k2_, v0, v1, v2_, sink_ref, do_ref, st_ref, dq_ref, dkv_out, ds_ref,
             bias_ref, dk_ref, dv_ref, dsk_ref, dkv_ref):
        n = pl.program_id(0)

        @pl.when(n == 0)
        def _():
            _attn_bias_init(bias_ref)
            dk_ref[...] = jnp.zeros_like(dk_ref)
            dv_ref[...] = jnp.zeros_like(dv_ref)
            dsk_ref[...] = jnp.zeros_like(dsk_ref)

        kb = _key_blocks(k0, k1, k2_)
        vb = _key_blocks(v0, v1, v2_)
        left3 = lax.broadcasted_iota(jnp.int32, (3 * BLK, 128), 1) < HEAD_DIM
        for sub in range(ATT_QB):
            blk = ATT_QB * n + sub
            e = jnp.where(blk == 0, 0, jnp.where(blk == nb - 1, 2, 1))
            kk = _dup_windows(*kb[sub:sub + 3])
            vv = _dup_windows(*vb[sub:sub + 3])
            start = pl.multiple_of(blk * BLK, BLK)
            dq_tiles, dks, dvs = [], [], []
            for kv in range(N_KV):
                qs = _stack_heads(q_ref, sub, kv, HEAD_DIM ** -0.5)
                dos = _stack_heads(do_ref, sub, kv, 1.0)
                stats = (st_ref[sub, kv:kv + 1, :], st_ref[sub, N_KV + kv:N_KV + kv + 1, :])
                p, ps, _, inv = _attn_softmax(qs, kk[kv], bias_ref[e, kv], sink_ref, kv, stats)
                pn = p * inv
                dp = lax.dot_general(vv[kv], dos, (((1,), (1,)), ((), ())), preferred_element_type=f32)
                delta = jnp.sum(pn * dp, axis=0, keepdims=True)
                dsc = (pn * (dp - delta)).astype(bf16)
                dsk_ref[kv:kv + 1, :] += delta * (ps * inv)
                dqt = lax.dot_general(kk[kv], dsc, (((0,), (0,)), ((), ())), preferred_element_type=f32)
                dq_tiles += _pair_tiles(dqt * (HEAD_DIM ** -0.5))
                dk = jnp.dot(dsc, qs, preferred_element_type=f32)
                dv = jnp.dot(pn.astype(bf16), dos, preferred_element_type=f32)
                dks.append(dk + pltpu.roll(dk, HEAD_DIM, 1))
                dvs.append(dv + pltpu.roll(dv, HEAD_DIM, 1))
            for jp in range(N_KV // 2):
                cols = slice(jp * 128, (jp + 1) * 128)
                dk_ref[pl.ds(start, 3 * BLK), cols] += jnp.where(left3, dks[2 * jp], dks[2 * jp + 1])
                dv_ref[pl.ds(start, 3 * BLK), cols] += jnp.where(left3, dvs[2 * jp], dvs[2 * jp + 1])
            dq_ref[BLK * sub:BLK * (sub + 1), :] = jnp.concatenate(dq_tiles, axis=1).astype(bf16)

        @pl.when(n == nsteps - 1)
        def _():
            rows = min(S, 512)
            for c in range(S // rows):
                dkv_ref[rows * c:rows * (c + 1), :kvw] = dk_ref[BLK + rows * c:BLK + rows * (c + 1), :].astype(bf16)
                dkv_ref[rows * c:rows * (c + 1), kvw:] = dv_ref[BLK + rows * c:BLK + rows * (c + 1), :].astype(bf16)
            pltpu.sync_copy(dkv_ref, dkv_out)
            lane = lax.broadcasted_iota(jnp.int32, (1, 128), 1)
            dsink = jnp.zeros((1, 128), f32)
            for h in range(N_HEADS):
                part = dsk_ref[h // 4:h // 4 + 1, (h % 4) * BLK:(h % 4 + 1) * BLK]
                dsink = dsink + jnp.where(lane == h, -jnp.sum(part), 0.0)
            ds_ref[...] = dsink

    acc = jax.ShapeDtypeStruct((S + 2 * BLK, N_KV * HEAD_DIM), f32)
    return _hosted_call(
        body, name="attn_bwd", grid=(nsteps,),
        in_specs=[pl.BlockSpec((ATT_QB * BLK, D), lambda n: (n, C_Q // D)),
                  *_kv_specs(nb, C_K // (N_KV * HEAD_DIM)), *_kv_specs(nb, C_V // (N_KV * HEAD_DIM)),
                  pl.BlockSpec(memory_space=pltpu.SMEM),
                  pl.BlockSpec((ATT_QB * BLK, D), lambda n: (n, 0)),
                  pl.BlockSpec((ATT_QB, 2 * N_KV, 4 * BLK), lambda n: (n, 0, 0))],
        out_specs=[pl.BlockSpec((ATT_QB * BLK, D), lambda n: (n, 0)), ANY_SPEC,
                   pl.BlockSpec((1, 128), lambda n: (0, 0))],
        out_shape=[jax.ShapeDtypeStruct((S, D), bf16), jax.ShapeDtypeStruct((S, 2 * kvw), bf16),
                   jax.ShapeDtypeStruct((1, 128), f32)],
        scratch_shapes=[pltpu.VMEM((3, N_KV, 3 * BLK, 4 * BLK), f32), pltpu.VMEM(acc.shape, f32),
                        pltpu.VMEM(acc.shape, f32), pltpu.VMEM((8, 4 * BLK), f32), pltpu.VMEM((S, 2 * kvw), bf16)],
        args=(proj, proj, proj, proj, proj, proj, proj, sink, dyb, stats), comm=comm)


def _merge_parts(hf, hb, g, z0, z1, yb, bg):
    g0 = _sigmoid(z0 + bg[:, :D].astype(bf16))
    g1 = _sigmoid(z1 + bg[:, D:].astype(bf16))
    gelu, dgelu = _gelu_and_grad(g)
    hs = hf + hb
    ya = hs * gelu
    return g0, g1, gelu, dgelu, hs, ya


def _merge_outproj(x, hf, hb, proj, yb, bg, w_out, tm=1024):
    S = x.shape[0]
    tm = min(tm, S)

    def body(x_ref, hf_ref, hb_ref, g_ref, z0_ref, z1_ref, yb_ref, bg_ref, w_ref, mg_ref, x1_ref):
        ybv = yb_ref[...]
        g0, g1, _, _, _, ya = _merge_parts(hf_ref[...], hb_ref[...], g_ref[...], z0_ref[...], z1_ref[...],
                                           ybv, bg_ref[...])
        mg = g0 * ya + g1 * ybv
        mg_ref[...] = mg
        x1_ref[...] = x_ref[...] + jnp.dot(mg, w_ref[...], preferred_element_type=f32)

    row = pl.BlockSpec((tm, D), lambda i: (i, 0))
    return pl.pallas_call(
        body, name="merge_outproj", grid=(S // tm,),
        in_specs=[row, row, row,
                  pl.BlockSpec((tm, D), lambda i: (i, C_G // D)),
                  pl.BlockSpec((tm, D), lambda i: (i, C_Z0 // D)),
                  pl.BlockSpec((tm, D), lambda i: (i, C_Z1 // D)),
                  row, pl.BlockSpec((1, 2 * D), lambda i: (0, 0)), pl.BlockSpec((D, D), lambda i: (0, 0))],
        out_specs=[row, row],
        out_shape=[jax.ShapeDtypeStruct((S, D), bf16), jax.ShapeDtypeStruct((S, D), f32)],
        compiler_params=_cparams())(x, hf, hb, proj, proj, proj, yb, bg, w_out)


def _ffn_out_loss(gu, x1, w_fo, g3, tgt, tm=256):
    S = x1.shape[0]
    tm = min(tm, S)

    def body(gt_ref, up_ref, x1_ref, w_ref, g_ref, t_ref, ff_ref, dx_ref, dxb_ref, loss_ref, dg_ref,
             dgt_ref, dup_ref):
        @pl.when(pl.program_id(0) == 0)
        def _():
            loss_ref[...] = jnp.zeros_like(loss_ref)
            dg_ref[...] = jnp.zeros_like(dg_ref)

        gt = gt_ref[...]
        up = up_ref[...]
        sg = _sigmoid(gt)
        silu = gt * sg
        ff = silu * up
        ff_ref[...] = ff
        x2 = x1_ref[...] + jnp.dot(ff, w_ref[...], preferred_element_type=f32)
        gv = g_ref[...]
        r = lax.rsqrt(jnp.mean(x2 * x2, axis=-1, keepdims=True) + EPS)
        xh = x2 * r
        diff = xh * gv - t_ref[...]
        loss_ref[...] += (0.5 / D) * jnp.sum(diff * diff)
        dy = diff * (1.0 / D)
        dg_ref[...] += jnp.sum(dy * xh, axis=0, keepdims=True)
        dxh = dy * gv
        dx = r * (dxh - xh * jnp.mean(dxh * xh, axis=-1, keepdims=True))
        dx_ref[...] = dx
        dxb = dx.astype(bf16)
        dxb_ref[...] = dxb
        dff = lax.dot_general(dxb, w_ref[...], (((1,), (1,)), ((), ())), preferred_element_type=f32)
        dup_ref[...] = (dff * silu.astype(f32)).astype(bf16)
        dgt_ref[...] = (dff * (up * (sg * (1.0 + gt * (1.0 - sg)))).astype(f32)).astype(bf16)

    row = pl.BlockSpec((tm, D), lambda i: (i, 0))
    vec = pl.BlockSpec((1, D), lambda i: (0, 0))
    wide = pl.BlockSpec((tm, D_FF), lambda i: (i, 0))
    wide_shape = jax.ShapeDtypeStruct((S, D_FF), bf16)
    return pl.pallas_call(
        body, name="ffn_out_loss", grid=(S // tm,),
        in_specs=[wide, pl.BlockSpec((tm, D_FF), lambda i: (i, 1)),
                  row, pl.BlockSpec((D_FF, D), lambda i: (0, 0)), vec, row],
        out_specs=[wide, row, row, pl.BlockSpec((1, 128), lambda i: (0, 0)), vec, wide, wide],
        out_shape=[wide_shape, jax.ShapeDtypeStruct((S, D), f32), jax.ShapeDtypeStruct((S, D), bf16),
                   jax.ShapeDtypeStruct((1, 128), f32), jax.ShapeDtypeStruct((1, D), f32), wide_shape, wide_shape],
        compiler_params=_cparams())(gu, gu, x1, w_fo, g3, tgt)


def _proj_bwd(pieces, wt, xres, g, dres, name, tm=512, comm=()):
    S = xres.shape[0]
    tm = min(tm, S)
    np_ = len(pieces)

    def body(*refs):
        p_refs = refs[:np_]
        w_refs = refs[np_:2 * np_]
        x_ref, g_ref, dres_ref, dx_ref, dxb_ref, dg_ref = refs[2 * np_:]

        @pl.when(pl.program_id(0) == 0)
        def _():
            dg_ref[...] = jnp.zeros_like(dg_ref)

        dn = jnp.dot(p_refs[0][...], w_refs[0][...], preferred_element_type=f32)
        for pr, wr in zip(p_refs[1:], w_refs[1:]):
            dn = dn + jnp.dot(pr[...], wr[...], preferred_element_type=f32)
        dxn, dgc = _rms_bwd(dn, x_ref[...], g_ref[...])
        dx = dres_ref[...] + dxn
        dx_ref[...] = dx
        dxb_ref[...] = dx.astype(bf16)
        dg_ref[...] += jnp.sum(dgc, axis=0, keepdims=True)

    row = pl.BlockSpec((tm, D), lambda i: (i, 0))
    vec = pl.BlockSpec((1, D), lambda i: (0, 0))
    return _hosted_call(
        body, name=name, grid=(S // tm,),
        in_specs=[*[pl.BlockSpec((tm, wd), functools.partial(lambda i, cb: (i, cb), cb=acb))
                    for _, acb, _, wd in pieces],
                  *[pl.BlockSpec((wd, D), functools.partial(lambda i, rb: (rb, 0), rb=wrb))
                    for _, _, wrb, wd in pieces],
                  row, vec, row],
        out_specs=[row, row, vec],
        out_shape=[jax.ShapeDtypeStruct((S, D), f32), jax.ShapeDtypeStruct((S, D), bf16),
                   jax.ShapeDtypeStruct((1, D), f32)],
        args=(*[p[0] for p in pieces], *[wt] * np_, xres, g, dres), comm=comm)


def _outproj_bwd(dx1b, w_out, hf, hb, proj, yb, bg, tm=1024):
    S = dx1b.shape[0]
    tm = min(tm, S)

    def body(dx_ref, w_ref, hf_ref, hb_ref, g_ref, z0_ref, z1_ref, yb_ref, bg_ref,
             dh_ref, dg_ref, dz_ref, dyb_ref, dbg_ref):
        @pl.when(pl.program_id(0) == 0)
        def _():
            dbg_ref[...] = jnp.zeros_like(dbg_ref)

        dm = lax.dot_general(dx_ref[...], w_ref[...], (((1,), (1,)), ((), ())), preferred_element_type=f32)
        ybv = yb_ref[...]
        g0, g1, gelu, dgelu, hs, ya = _merge_parts(hf_ref[...], hb_ref[...], g_ref[...], z0_ref[...],
                                                   z1_ref[...], ybv, bg_ref[...])
        dh_ref[...] = (dm * (g0 * gelu).astype(f32)).astype(bf16)
        dg_ref[...] = (dm * (g0 * hs * dgelu).astype(f32)).astype(bf16)
        dyb_ref[...] = (dm * g1.astype(f32)).astype(bf16)
        dz0 = dm * (ya * (g0 * (1.0 - g0))).astype(f32)
        dz1 = dm * (ybv * (g1 * (1.0 - g1))).astype(f32)
        dz = jnp.concatenate([dz0, dz1], axis=1)
        dz_ref[...] = dz.astype(bf16)
        dbg_ref[...] += jnp.sum(dz, axis=0, keepdims=True)

    row = pl.BlockSpec((tm, D), lambda i: (i, 0))
    return pl.pallas_call(
        body, name="outproj_bwd", grid=(S // tm,),
        in_specs=[row, pl.BlockSpec((D, D), lambda i: (0, 0)), row, row,
                  pl.BlockSpec((tm, D), lambda i: (i, C_G // D)),
                  pl.BlockSpec((tm, D), lambda i: (i, C_Z0 // D)),
                  pl.BlockSpec((tm, D), lambda i: (i, C_Z1 // D)),
                  row, pl.BlockSpec((1, 2 * D), lambda i: (0, 0))],
        out_specs=[row, row, pl.BlockSpec((tm, 2 * D), lambda i: (i, 0)), row,
                   pl.BlockSpec((1, 2 * D), lambda i: (0, 0))],
        out_shape=[jax.ShapeDtypeStruct((S, D), bf16), jax.ShapeDtypeStruct((S, D), bf16),
                   jax.ShapeDtypeStruct((S, 2 * D), bf16), jax.ShapeDtypeStruct((S, D), bf16),
                   jax.ShapeDtypeStruct((1, 2 * D), f32)],
        compiler_params=_cparams())(dx1b, w_out, hf, hb, proj, proj, proj, yb, bg)


def _block_diag_groups(w):
    w4 = w.reshape(LRU_GROUPS, 4, LRU_BLOCK, LRU_BLOCK)
    eye = jnp.eye(4, dtype=w.dtype)
    return jnp.einsum("ghij,hk->ghikj", w4, eye).reshape(LRU_GROUPS, LRU_GW, LRU_GW)


def _diag_blocks(dw):
    d5 = dw.reshape(LRU_GROUPS, 4, LRU_BLOCK, 4, LRU_BLOCK)
    return jnp.stack([d5[:, h, :, h, :] for h in range(4)], axis=1).reshape(LRU_HEADS, LRU_BLOCK, LRU_BLOCK)


def _local_step(x, tgt, small, env, before=lambda name: (), after=lambda name, got: None):
    S = x.shape[0]
    g1, g2, g3 = small["norm_mix_g"], small["norm_ffn_g"], small["norm_final_g"]
    bg, cb, sink = small["b_gate"], small["conv_b"], small["attn_sink"]

    def hosted(name, fn, *args, **kw):
        outs, got = fn(*args, comm=tuple(before(name)), **kw)
        after(name, got)
        return outs

    (xn,) = hosted("norm_x", _rmsnorm_bf16, x, g1, "norm_x")
    cw = small["conv_w"]
    wg = jnp.concatenate([_block_diag_groups(small["lru_wa"][0]), _block_diag_groups(small["lru_wx"][0]),
                          _block_diag_groups(small["lru_wa"][1]), _block_diag_groups(small["lru_wx"][1])],
                         axis=2).astype(bf16)
    zeros5 = jnp.zeros((5, D), f32)
    lp = jnp.stack([jnp.concatenate([small["lru_lambda"][d:d + 1], small["lru_ba"][d:d + 1],
                                     small["lru_bx"][d:d + 1], zeros5], axis=0) for d in range(2)])
    (proj,) = hosted("inproj", _matmul_t, xn, env["w_in_t"], "inproj", tm=4096, tn=512,
                     row_block=lambda j: jnp.where(j < 6, j, jnp.where(j < 10, j + 1, 6)))
    uc = _conv_fwd(proj, cw, cb)
    (hf,), _ = _lru_fwd(uc, wg, lp, False)
    (hb,), _ = _lru_fwd(uc, wg, lp, True)
    yb, attn_stats = hosted("attn_fwd", _attn_fwd, proj, sink)
    merged, x1 = _merge_outproj(x, hf, hb, proj, yb, bg, env["w_out"])
    (xn2, gu), _ = _norm_matmul(x1, g2, env["w_fi_t"], "norm_ffn_in", tn=D_FF)
    ff, dx2, dx2b, loss, dg3, dgt, dup = _ffn_out_loss(gu, x1, env["w_fo"], g3, tgt)

    env["dw_fo"] = _mm_tn(ff, dx2b, "dw_ffn_out", tk=1408, tn=1024)
    dx1, dx1b, dg2 = hosted("ffn_in_bwd", _proj_bwd, [(dgt, 0, 0, D_FF), (dup, 0, 1, D_FF)], env["w_fi_t"],
                            x1, g2, dx2, "ffn_in_bwd")
    dw_gate = _mm_tn(dgt, xn2, "dw_ffn_in_gate", tk=1408, tn=1024, out_rows=2 * D_FF)
    env["dw_fi_t"] = _mm_tn(dup, xn2, "dw_ffn_in_up", tk=1408, tn=1024, into=dw_gate, row=D_FF // 1408)
    env["dw_out"] = _mm_tn(merged, dx1b, "dw_out", tk=1024, tn=1024)
    dh, dgl, dz, dyb, dbg = _outproj_bwd(dx1b, env["w_out"], hf, hb, proj, yb, bg)
    dq, dkv, dsink = hosted("attn_bwd", _attn_bwd, proj, sink, dyb, attn_stats)
    duc_f, dwg_f, dp_f = hosted("lru_bwd", _lru_bwd, uc, dh, hf, wg, lp, False)
    (duc_b, dwg_b, dp_b), _ = _lru_bwd(uc, dh, hb, wg, lp, True)
    env["grads_early"] = {
        "loss": loss[:, :1], "b_gate": dbg,
        "lru_lambda": jnp.concatenate([dp_f[0:1], dp_b[0:1]], axis=0),
        "lru_wa": jnp.stack([_diag_blocks(dwg_f[:, :, :LRU_GW]), _diag_blocks(dwg_b[:, :, :LRU_GW])]),
        "lru_ba": jnp.concatenate([dp_f[1:2], dp_b[1:2]], axis=0),
        "lru_wx": jnp.stack([_diag_blocks(dwg_f[:, :, LRU_GW:]), _diag_blocks(dwg_b[:, :, LRU_GW:])]),
        "lru_bx": jnp.concatenate([dp_f[2:3], dp_b[2:3]], axis=0),
        "attn_sink": dsink[:, :N_HEADS], "norm_ffn_g": dg2, "norm_final_g": dg3,
    }
    du, dcw, dcb = hosted("conv_bwd", _conv_bwd, duc_f, duc_b, proj, cw)
    dw_in = _mm_tn(du, xn, "dw_in_u", tk=1024, tn=1024, out_rows=IN_W)
    dw_in = _mm_tn(dgl, xn, "dw_in_g", tk=1024, tn=1024, into=dw_in, row=1)
    dw_in = _mm_tn(dq, xn, "dw_in_q", tk=1024, tn=1024, into=dw_in, row=2)
    dw_in = _mm_tn(dkv, xn, "dw_in_kv", tk=512, tn=1024, into=dw_in, row=3072 // 512)
    env["dw_in_t"] = _mm_tn(dz, xn, "dw_in_z", tk=512, tn=1024, tmc=4096, into=dw_in, row=3584 // 512)
    col_pieces = [(du, 0, 0, D), (dgl, 0, 1, D), (dq, 0, 2, D), (dkv, 0, 3072 // 512, 512),
                  *[(dz, i, 3584 // 512 + i, 512) for i in range(4)]]
    dx, _, dg1 = hosted("inproj_bwd", _proj_bwd, col_pieces, env["w_in_t"], x, g1, dx1, "inproj_bwd")

    grads = dict(env["grads_early"], norm_mix_g=dg1, conv_w=dcw, conv_b=dcb)
    return dx, grads


def _adamw(gparts, w, m, v, name, tr=256):
    n, rows, cols = gparts.shape
    tr = _div_tile(rows, tr)
    c1 = 1.0 - ADAM_B1 ** ADAM_STEP
    c2 = 1.0 - ADAM_B2 ** ADAM_STEP

    def body(g_ref, w_ref, m_ref, v_ref, go_ref, d_ref, mo_ref, vo_ref):
        g = g_ref[0].astype(f32)
        for j in range(1, n):
            g = g + g_ref[j].astype(f32)
        mn = ADAM_B1 * m_ref[0] + (1.0 - ADAM_B1) * g
        vn = ADAM_B2 * v_ref[0] + (1.0 - ADAM_B2) * (g * g)
        m_hat = mn / c1
        v_hat = vn / c2
        go_ref[0] = g
        d_ref[0] = -ADAM_LR * (m_hat / (jnp.sqrt(v_hat) + ADAM_EPS) + ADAM_WD * w_ref[0])
        mo_ref[0] = mn
        vo_ref[0] = vn

    blk = pl.BlockSpec((1, tr, cols), lambda i: (0, i, 0))
    shp = jax.ShapeDtypeStruct((1, rows, cols), f32)
    return pl.pallas_call(
        body, name=name, grid=(rows // tr,),
        in_specs=[pl.BlockSpec((n, tr, cols), lambda i: (0, i, 0)), blk, blk, blk],
        out_specs=[blk, blk, blk, blk], out_shape=[shp, shp, shp, shp],
        compiler_params=_cparams())(gparts, w, m, v)


def _sum_parts(parts, name):
    n, rows, cols = parts.shape

    def body(p_ref, o_ref):
        acc = p_ref[0].astype(f32)
        for j in range(1, n):
            acc = acc + p_ref[j].astype(f32)
        o_ref[...] = acc

    return pl.pallas_call(
        body, name=name, out_shape=jax.ShapeDtypeStruct((rows, cols), f32),
        compiler_params=_cparams())(parts)


def _pack_rows(arrs, dtype=f32):
    rows, spans, at = [], [], 0
    for a in arrs:
        flat = a.reshape(-1).astype(dtype)
        nr = -(-flat.shape[0] // 1024)
        rows.append(jnp.pad(flat, (0, nr * 1024 - flat.shape[0])).reshape(nr, 1024))
        spans.append((at, nr))
        at += nr
    pad = (-at) % 16
    if pad:
        rows.append(jnp.zeros((pad, 1024), dtype))
    return jnp.concatenate(rows, axis=0), spans


def _unpack_rows(packed, spans, shapes):
    out = []
    for (at, nr), shp in zip(spans, shapes):
        n = math.prod(shp)
        out.append(packed[at:at + nr].reshape(-1)[:n].reshape(shp))
    return out


BIG = ("w_in", "w_out", "w_ffn_in", "w_ffn_out")
SMALL_REPL = ("norm_mix_g", "b_gate", "conv_b", "attn_sink", "norm_ffn_g", "norm_final_g")
GATE_W = ("lru_wa", "lru_wx")
SMALL_SHARD = ("conv_w", "lru_lambda", "lru_ba", "lru_bx")
ORDER = ("norm_mix_g", "w_in", "b_gate", "conv_w", "conv_b", "lru_lambda", "lru_wa", "lru_ba", "lru_wx",
         "lru_bx", "attn_sink", "w_out", "norm_ffn_g", "w_ffn_in", "w_ffn_out", "norm_final_g")
EARLY_F32 = ("loss", "b_gate", "lru_lambda", "lru_ba", "lru_bx", "attn_sink", "norm_ffn_g", "norm_final_g")
LATE = ("norm_mix_g", "conv_w", "conv_b")


def kernel(x, norm_mix_g, w_in, b_gate, conv_w, conv_b, lru_lambda, lru_wa, lru_ba, lru_wx, lru_bx, attn_sink, w_out, norm_ffn_g, w_ffn_in, w_ffn_out, norm_final_g, loss_target, m_norm_mix_g, m_w_in, m_b_gate, m_conv_w, m_conv_b, m_lru_lambda, m_lru_wa, m_lru_ba, m_lru_wx, m_lru_bx, m_attn_sink, m_w_out, m_norm_ffn_g, m_w_ffn_in, m_w_ffn_out, m_norm_final_g, v_norm_mix_g, v_w_in, v_b_gate, v_conv_w, v_conv_b, v_lru_lambda, v_lru_wa, v_lru_ba, v_lru_wx, v_lru_bx, v_attn_sink, v_w_out, v_norm_ffn_g, v_w_ffn_in, v_w_ffn_out, v_norm_final_g):
    w = dict(norm_mix_g=norm_mix_g, w_in=w_in, b_gate=b_gate, conv_w=conv_w, conv_b=conv_b, lru_lambda=lru_lambda,
             lru_wa=lru_wa, lru_ba=lru_ba, lru_wx=lru_wx, lru_bx=lru_bx, attn_sink=attn_sink, w_out=w_out,
             norm_ffn_g=norm_ffn_g, w_ffn_in=w_ffn_in, w_ffn_out=w_ffn_out, norm_final_g=norm_final_g)
    m = dict(norm_mix_g=m_norm_mix_g, w_in=m_w_in, b_gate=m_b_gate, conv_w=m_conv_w, conv_b=m_conv_b,
             lru_lambda=m_lru_lambda, lru_wa=m_lru_wa, lru_ba=m_lru_ba, lru_wx=m_lru_wx, lru_bx=m_lru_bx,
             attn_sink=m_attn_sink, w_out=m_w_out, norm_ffn_g=m_norm_ffn_g, w_ffn_in=m_w_ffn_in,
             w_ffn_out=m_w_ffn_out, norm_final_g=m_norm_final_g)
    v = dict(norm_mix_g=v_norm_mix_g, w_in=v_w_in, b_gate=v_b_gate, conv_w=v_conv_w, conv_b=v_conv_b,
             lru_lambda=v_lru_lambda, lru_wa=v_lru_wa, lru_ba=v_lru_ba, lru_wx=v_lru_wx, lru_bx=v_lru_bx,
             attn_sink=v_attn_sink, w_out=v_w_out, norm_ffn_g=v_norm_ffn_g, w_ffn_in=v_w_ffn_in,
             w_ffn_out=v_w_ffn_out, norm_final_g=v_norm_final_g)
    me = 4 * lax.axis_index("x") + 2 * lax.axis_index("y") + lax.axis_index("c")

    def shard_t(a):
        return jnp.swapaxes(a[0], 0, 1)

    def rows_parts(g):
        return g.reshape(N_DEV, -1, g.shape[1])

    shard_rows = jnp.concatenate([w[n][0] for n in SMALL_SHARD], axis=0)
    small = {n: w[n] for n in ("norm_mix_g", "b_gate", "conv_b", "attn_sink", "norm_ffn_g")}
    small["lru_wa"], small["lru_wx"] = lru_wa[0], lru_wx[0]
    small["norm_final_g"] = norm_final_g.reshape(1, D)
    env, recv = {}, {}

    def before(name):
        if name == "norm_x":
            return [(shard_t(w_in).astype(bf16), False), (shard_rows, False)]
        if name == "inproj":
            return [(w_out[0].astype(bf16), False), (w_ffn_out[0].astype(bf16), False)]
        if name == "attn_fwd":
            return [(shard_t(w_ffn_in).astype(bf16), False)]
        if name == "ffn_in_bwd":
            return [(rows_parts(env["dw_fo"]), True)]
        if name == "attn_bwd":
            return [(rows_parts(env["dw_out"]), True)]
        if name == "lru_bwd":
            return [(rows_parts(env["dw_fi_t"]), True)]
        if name == "conv_bwd":
            ge = env["grads_early"]
            p32, env["early_f32_spans"] = _pack_rows([ge[n] for n in EARLY_F32])
            return [(p32, False), *[(ge[n].astype(bf16).reshape(-1, LRU_BLOCK), False) for n in GATE_W]]
        if name == "inproj_bwd":
            return [(rows_parts(env["dw_in_t"]), True)]
        return []

    def after(name, got):
        if name == "norm_x":
            env["w_in_t"] = got[0].reshape(IN_W, D)
            full_rows = jnp.swapaxes(got[1], 0, 1).reshape(shard_rows.shape[0], -1)
            small["conv_w"], small["lru_lambda"] = full_rows[0:4], full_rows[4:6]
            small["lru_ba"], small["lru_bx"] = full_rows[6:8], full_rows[8:10]
        elif name == "inproj":
            env["w_out"], env["w_fo"] = got[0].reshape(D, D), got[1].reshape(D_FF, D)
        elif name == "attn_fwd":
            env["w_fi_t"] = got[0].reshape(2 * D_FF, D)
        elif name == "ffn_in_bwd":
            recv["w_ffn_out"] = got[0]
        elif name == "attn_bwd":
            recv["w_out"] = got[0]
        elif name == "lru_bwd":
            recv["w_ffn_in"] = got[0]
        elif name == "conv_bwd":
            recv["early_f32"], recv["lru_wa"], recv["lru_wx"] = got
        elif name == "inproj_bwd":
            recv["w_in"] = got[0]

    grad_x, grads = _local_step(x[0], loss_target[0], small, env, before, after)

    outs = {}
    for name in ("w_out", "w_ffn_out"):
        outs[name] = _adamw(recv[name], w[name], m[name], v[name], "adamw_" + name)
    for name in ("w_in", "w_ffn_in"):
        t = lambda a: jnp.swapaxes(a, 1, 2)
        outs[name] = [t(r) for r in _adamw(recv[name], t(w[name]), t(m[name]), t(v[name]), "adamw_" + name)]
    for name in GATE_W:
        t = lambda a: a.reshape(1, -1, LRU_BLOCK)
        res = _adamw(recv[name], t(w[name]), t(m[name]), t(v[name]), "adamw_" + name)
        outs[name] = [r.reshape(w[name].shape) for r in res]

    small_names = SMALL_REPL + SMALL_SHARD
    late_packed, late_spans = _pack_rows([grads[n] for n in LATE])
    (got_late,) = _exchange([(late_packed, False)], "gather_late_grads")
    summed = {}
    for names, got, spans, tag in ((EARLY_F32, recv["early_f32"], env["early_f32_spans"], "early_f32"),
                                   (LATE, got_late, late_spans, "late")):
        total = _sum_parts(got, "sum_small_" + tag)
        summed.update(zip(names, _unpack_rows(total, spans, [grads[n].shape for n in names])))
    loss = summed["loss"].reshape(())
    gsm = {n: summed[n].reshape(w[n].shape) for n in SMALL_REPL}
    for n in SMALL_SHARD:
        full = summed[n]
        gsm[n] = lax.dynamic_slice_in_dim(full, me * 128, 128, axis=1).reshape(w[n].shape)
    pk = lambda dct: _pack_rows([dct[n] for n in small_names])[0]
    gp, sp = _pack_rows([gsm[n] for n in small_names])
    res = _adamw(gp[None], pk(w)[None], pk(m)[None], pk(v)[None], "adamw_small")
    sshapes = [w[n].shape for n in small_names]
    for idx, t in enumerate(res):
        for n, a in zip(small_names, _unpack_rows(t[0], sp, sshapes)):
            outs.setdefault(n, [None] * 4)[idx] = a

    result = [loss, grad_x[None]]
    for idx in range(4):
        result += [outs[n][idx] for n in ORDER]
    return tuple(result)
```

```python
import functools
import math

import jax
import jax.numpy as jnp
from jax import lax
from jax.experimental import pallas as pl
from jax.experimental.pallas import tpu as pltpu

f32 = jnp.float32
bf16 = jnp.bfloat16

D = 1024
D_FF = 2816
IN_W = 5632
N_HEADS = 16
N_KV = 4
HEAD_DIM = 64
WINDOW = 128
BLK = 128
LRU_HEADS = 16
LRU_BLOCK = 64
LRU_GROUPS = 4
LRU_GW = 256
LRU_CHUNK = 64
LRU_CHUNK_BWD = 512
LRU_ROWS = 4096
RGLRU_C = 8.0
EPS = 1e-6
NEG_INF = -1e30
N_DEV = 8

ADAM_LR = 0.001
ADAM_B1 = 0.9
ADAM_B2 = 0.999
ADAM_EPS = 1e-08
ADAM_WD = 0.01
ADAM_STEP = 10

VMEM_MB = 56

C_U, C_G, C_Q, C_Z0, C_Z1, C_K, C_V = 0, 1024, 2048, 3072, 4096, 5120, 5376


def _cparams(vmem_mb=VMEM_MB):
    return pltpu.CompilerParams(vmem_limit_bytes=vmem_mb << 20)


def _div_tile(n, pref):
    if n <= pref:
        return n
    return max(t for t in range(8, pref + 1, 8) if n % t == 0)


def _sigmoid(x):
    return 0.5 * jnp.tanh(0.5 * x) + 0.5


def _log1p(x):
    u = 1.0 + x
    d = u - 1.0
    return jnp.where(d == 0.0, x, jnp.log(u) * (x / jnp.where(d == 0.0, 1.0, d)))


def _softplus(x):
    return jnp.maximum(x, 0.0) + _log1p(jnp.exp(-jnp.abs(x)))


def _gelu_and_grad(x):
    c = math.sqrt(2.0 / math.pi)
    inner = c * (x + 0.044715 * (x * x * x))
    t = jnp.tanh(inner)
    gelu = 0.5 * x * (1.0 + t)
    dinner = c * (1.0 + 3 * 0.044715 * (x * x))
    dgelu = 0.5 * (1.0 + t) + 0.5 * x * (1.0 - t * t) * dinner
    return gelu, dgelu


def _rms_bwd(dn, xv, g):
    r = lax.rsqrt(jnp.mean(xv * xv, axis=-1, keepdims=True) + EPS)
    xh = xv * r
    dxh = dn * g
    dx = r * (dxh - xh * jnp.mean(dxh * xh, axis=-1, keepdims=True))
    return dx, dn * xh


ANY_SPEC = pl.BlockSpec(memory_space=pl.ANY)


def _comm_out_shape(src, scatter):
    return jax.ShapeDtypeStruct((N_DEV, *(src.shape[1:] if scatter else src.shape)), src.dtype)


def _comm_sems():
    return [pltpu.SemaphoreType.DMA((N_DEV - 1,)), pltpu.SemaphoreType.DMA((N_DEV - 1,)), pltpu.SemaphoreType.DMA]


def _scatter_descs(src_ref, out_ref, send_sems, recv_sems, local_sem):
    x, y, c = lax.axis_index("x"), lax.axis_index("y"), lax.axis_index("c")
    me = 4 * x + 2 * y + c
    descs = [pltpu.make_async_copy(src_ref.at[me], out_ref.at[me], local_sem)]
    for k in range(1, N_DEV):
        px, py, pc = x ^ (k >> 2), y ^ ((k >> 1) & 1), c ^ (k & 1)
        descs.append(pltpu.make_async_remote_copy(
            src_ref=src_ref.at[4 * px + 2 * py + pc], dst_ref=out_ref.at[me],
            send_sem=send_sems.at[k - 1], recv_sem=recv_sems.at[k - 1],
            device_id=(px, py, pc), device_id_type=pl.DeviceIdType.MESH))
    return descs


def _gather_copies(src_ref, out_ref, send_sems, recv_sems, local_sem, which):
    x, y, c = lax.axis_index("x"), lax.axis_index("y"), lax.axis_index("c")
    me, sibling = (x, y, c), (x, y, 1 - c)
    chips = [(1 - x, y), (x, 1 - y), (1 - x, 1 - y)]

    def slot(px, py, pc):
        return out_ref.at[4 * px + 2 * py + pc]

    def copy(k, block, to, src=None):
        return pltpu.make_async_remote_copy(
            src_ref=slot(*block) if src is None else src, dst_ref=slot(*block),
            send_sem=send_sems.at[k], recv_sem=recv_sems.at[k], device_id=to, device_id_type=pl.DeviceIdType.MESH)

    make = {
        "local": lambda: pltpu.make_async_copy(src_ref, slot(*me), local_sem),
        "first": lambda: [copy(0, me, sibling, src=src_ref)] + [copy(1 + j, me, (*chip, c), src=src_ref)
                                                                 for j, chip in enumerate(chips)],
        "passed": lambda: [copy(4 + j, (*chip, c), sibling) for j, chip in enumerate(chips)],
        "landed": lambda: [copy(1 + j, (*chip, c), me) for j, chip in enumerate(chips)],
        "later": lambda: [copy(0, sibling, me)] + [copy(4 + j, (*chip, 1 - c), me) for j, chip in enumerate(chips)],
    }
    return [make[name]() for name in which]


def _comm_start(src_ref, out_ref, sems, scatter):
    if scatter:
        for d in _scatter_descs(src_ref, out_ref, *sems):
            d.start()
    else:
        local, first = _gather_copies(src_ref, out_ref, *sems, which=("local", "first"))
        local.start()
        for cp in first:
            cp.start()


def _comm_pass_on(src_ref, out_ref, sems, scatter):
    if not scatter:
        landed, passed = _gather_copies(src_ref, out_ref, *sems, which=("landed", "passed"))
        for arrived, onward in zip(landed, passed):
            arrived.wait_recv()
            onward.start()


def _comm_finish(src_ref, out_ref, sems, scatter):
    if scatter:
        for d in _scatter_descs(src_ref, out_ref, *sems):
            d.wait()
    else:
        later, first, passed, local = _gather_copies(src_ref, out_ref, *sems,
                                                     which=("later", "first", "passed", "local"))
        for cp in later:
            cp.wait_recv()
        for cp in first + passed:
            cp.wait_send()
        local.wait()


def _exchange(comm, name):
    nc = len(comm)

    def body(*refs):
        srcs, outs, sems = refs[:nc], refs[nc:2 * nc], refs[2 * nc:]
        for stage in (_comm_start, _comm_pass_on, _comm_finish):
            for i in range(nc):
                stage(srcs[i], outs[i], sems[3 * i:3 * i + 3], comm[i][1])

    return pl.pallas_call(
        body, name=name, in_specs=[ANY_SPEC] * nc, out_specs=[ANY_SPEC] * nc,
        out_shape=[_comm_out_shape(*c) for c in comm],
        scratch_shapes=[s for _ in comm for s in _comm_sems()],
    )(*[c[0] for c in comm])


def _hosted_call(body, *, name, grid, in_specs, out_specs, out_shape, args, scratch_shapes=(), comm=()):
    nin, nout, nscr, nc = len(in_specs), len(out_specs), len(scratch_shapes), len(comm)
    steps = math.prod(grid)

    def wrapped(*refs):
        ins = refs[:nin]
        csrc = refs[nin:nin + nc]
        outs = refs[nin + nc:nin + nc + nout]
        cout = refs[nin + nc + nout:nin + 2 * nc + nout]
        scr = refs[nin + 2 * nc + nout:]
        sems = scr[nscr:]

        def at(step, stage):
            lin = 0
            for a in range(len(grid)):
                lin = lin * grid[a] + pl.program_id(a)

            @pl.when(lin == step)
            def _():
                for i in range(nc):
                    stage(csrc[i], cout[i], sems[3 * i:3 * i + 3], comm[i][1])

        if nc:
            at(0, _comm_start)

        body(*ins, *outs, *scr[:nscr])

        if nc:
            at((3 * (steps - 1)) // 4, _comm_pass_on)
            at(steps - 1, _comm_finish)

    res = pl.pallas_call(
        wrapped, name=name, grid=grid,
        in_specs=[*in_specs, *[ANY_SPEC] * nc], out_specs=[*out_specs, *[ANY_SPEC] * nc],
        out_shape=[*out_shape, *[_comm_out_shape(*c) for c in comm]],
        scratch_shapes=[*scratch_shapes, *[s for _ in comm for s in _comm_sems()]],
        compiler_params=_cparams())(*args, *[c[0] for c in comm])
    return res[:nout], res[nout:]


def _rmsnorm_bf16(x, g, name, tm=1024, comm=()):
    S, dm = x.shape
    tm = min(tm, S)

    def body(x_ref, g_ref, xn_ref):
        xv = x_ref[...]
        r = lax.rsqrt(jnp.mean(xv * xv, axis=-1, keepdims=True) + EPS)
        xn_ref[...] = ((xv * r) * g_ref[...]).astype(bf16)

    row = pl.BlockSpec((tm, dm), lambda i: (i, 0))
    return _hosted_call(
        body, name=name, grid=(S // tm,), in_specs=[row, pl.BlockSpec((1, dm), lambda i: (0, 0))],
        out_specs=[row], out_shape=[jax.ShapeDtypeStruct((S, dm), bf16)], args=(x, g), comm=comm)


def _matmul_t(a, wt, name, tm=2048, tn=512, row_block=lambda j: j, comm=()):
    S, dm = a.shape
    n = wt.shape[0]
    tm = min(tm, S)

    def body(a_ref, w_ref, o_ref):
        o_ref[...] = lax.dot_general(a_ref[...], w_ref[...], (((1,), (1,)), ((), ())),
                                     preferred_element_type=f32).astype(bf16)

    return _hosted_call(
        body, name=name, grid=(S // tm, n // tn),
        in_specs=[pl.BlockSpec((tm, dm), lambda i, j: (i, 0)),
                  pl.BlockSpec((tn, dm), lambda i, j: (row_block(j), 0))],
        out_specs=[pl.BlockSpec((tm, tn), lambda i, j: (i, j))],
        out_shape=[jax.ShapeDtypeStruct((S, n), bf16)], args=(a, wt), comm=comm)


def _norm_matmul(x, g, wt, name, tm=1024, tn=1408, row_block=lambda j: j, comm=()):
    S, dm = x.shape
    n = wt.shape[0]
    tm = min(tm, S)

    def body(x_ref, g_ref, w_ref, xn_ref, o_ref):
        @pl.when(pl.program_id(1) == 0)
        def _():
            xv = x_ref[...]
            r = lax.rsqrt(jnp.mean(xv * xv, axis=-1, keepdims=True) + EPS)
            xn_ref[...] = ((xv * r) * g_ref[...]).astype(bf16)

        o_ref[...] = lax.dot_general(xn_ref[...], w_ref[...], (((1,), (1,)), ((), ())),
                                     preferred_element_type=f32).astype(bf16)

    return _hosted_call(
        body, name=name, grid=(S // tm, n // tn),
        in_specs=[pl.BlockSpec((tm, dm), lambda i, j: (i, 0)),
                  pl.BlockSpec((1, dm), lambda i, j: (0, 0)),
                  pl.BlockSpec((tn, dm), lambda i, j: (row_block(j), 0))],
        out_specs=[pl.BlockSpec((tm, dm), lambda i, j: (i, 0)),
                   pl.BlockSpec((tm, tn), lambda i, j: (i, j))],
        out_shape=[jax.ShapeDtypeStruct((S, dm), bf16), jax.ShapeDtypeStruct((S, n), bf16)],
        args=(x, g, wt), comm=comm)


def _mm_tn(a, b, name, tk, tn, tmc=2048, into=None, row=0, out_rows=None):
    m, ka = a.shape
    n = b.shape[1]
    tmc = min(tmc, m)
    nk = m // tmc

    def body(a_ref, b_ref, *rest):
        o_ref, acc_ref = rest[-2:]
        k = pl.program_id(2)
        part = lax.dot_general(a_ref[...], b_ref[...], (((0,), (0,)), ((), ())), preferred_element_type=f32)

        @pl.when(k == 0)
        def _():
            acc_ref[...] = part

        @pl.when(k > 0)
        def _():
            acc_ref[...] += part

        @pl.when(k == nk - 1)
        def _():
            o_ref[...] = acc_ref[...].astype(bf16)

    in_specs = [pl.BlockSpec((tmc, tk), lambda i, j, k: (k, i)), pl.BlockSpec((tmc, tn), lambda i, j, k: (k, j))]
    if into is None:
        return pl.pallas_call(
            body, name=name, grid=(ka // tk, n // tn, nk), in_specs=in_specs,
            out_specs=pl.BlockSpec((tk, tn), lambda i, j, k: (i + row, j)),
            out_shape=jax.ShapeDtypeStruct((out_rows or ka, n), bf16),
            scratch_shapes=[pltpu.VMEM((tk, tn), f32)],
            compiler_params=_cparams())(a, b)
    return pl.pallas_call(
        body, name=name, grid=(ka // tk, n // tn, nk), in_specs=[*in_specs, ANY_SPEC],
        out_specs=pl.BlockSpec((tk, tn), lambda i, j, k: (i + row, j)),
        out_shape=jax.ShapeDtypeStruct(into.shape, into.dtype),
        scratch_shapes=[pltpu.VMEM((tk, tn), f32)], input_output_aliases={2: 0},
        compiler_params=_cparams())(a, b, into)


HALO = 16


def _rows_at(ext, o, tc):
    if o == 0:
        return ext[HALO:HALO + tc]
    return pltpu.roll(ext, (-o) % ext.shape[0], 0)[HALO:HALO + tc]


def _halo_specs(tc, S, width, col):
    per = tc // HALO
    last = S // HALO - 1
    return (pl.BlockSpec((tc, width), lambda i: (i, col)),
            pl.BlockSpec((HALO, width), lambda i: (jnp.maximum(i * per - 1, 0), col)),
            pl.BlockSpec((HALO, width), lambda i: (jnp.minimum((i + 1) * per, last), col)))


def _extended(cur_ref, prev_ref, next_ref, i, nsteps):
    prev = jnp.where(i > 0, prev_ref[...].astype(f32), 0.0)
    nxt = jnp.where(i < nsteps - 1, next_ref[...].astype(f32), 0.0)
    return jnp.concatenate([prev, cur_ref[...].astype(f32), nxt], axis=0)


def _conv_fwd(proj, cw, cb, tc=2048):
    S = proj.shape[0]
    tc = min(tc, S)
    nsteps = S // tc

    def body(cur_ref, prev_ref, next_ref, w_ref, b_ref, o_ref):
        ext = _extended(cur_ref, prev_ref, next_ref, pl.program_id(0), nsteps)
        acc = _rows_at(ext, -2, tc) * w_ref[0:1, :]
        for k in range(1, 4):
            acc = acc + _rows_at(ext, k - 2, tc) * w_ref[k:k + 1, :]
        o_ref[...] = acc + b_ref[...]

    return pl.pallas_call(
        body, name="conv_fwd", grid=(nsteps,),
        in_specs=[*_halo_specs(tc, S, D, 0),
                  pl.BlockSpec((4, D), lambda i: (0, 0)), pl.BlockSpec((1, D), lambda i: (0, 0))],
        out_specs=pl.BlockSpec((tc, D), lambda i: (i, 0)),
        out_shape=jax.ShapeDtypeStruct((S, D), f32),
        compiler_params=_cparams())(proj, proj, proj, cw, cb)


def _conv_bwd(duc_f, duc_b, proj, cw, tc=1024, comm=()):
    S = proj.shape[0]
    tc = min(tc, S)
    nsteps = S // tc

    def body(fc, fp, fn, bc, bp, bn, uc_, up, un, w_ref, du_ref, dw_ref, db_ref):
        i = pl.program_id(0)

        @pl.when(i == 0)
        def _():
            dw_ref[...] = jnp.zeros_like(dw_ref)
            db_ref[...] = jnp.zeros_like(db_ref)

        dext = _extended(fc, fp, fn, i, nsteps) + _extended(bc, bp, bn, i, nsteps)
        uext = _extended(uc_, up, un, i, nsteps)
        d = dext[HALO:HALO + tc]
        acc = _rows_at(dext, 2, tc) * w_ref[0:1, :]
        for k in range(1, 4):
            acc = acc + _rows_at(dext, 2 - k, tc) * w_ref[k:k + 1, :]
        du_ref[...] = acc.astype(bf16)
        wrow = lax.broadcasted_iota(jnp.int32, (4, D), 0)
        for k in range(4):
            dw_ref[...] += jnp.where(wrow == k, jnp.sum(d * _rows_at(uext, k - 2, tc), axis=0, keepdims=True), 0.0)
        db_ref[...] += jnp.sum(d, axis=0, keepdims=True)

    return _hosted_call(
        body, name="conv_bwd", grid=(nsteps,),
        in_specs=[*_halo_specs(tc, S, D, 0), *_halo_specs(tc, S, D, 0), *_halo_specs(tc, S, D, 0),
                  pl.BlockSpec((4, D), lambda i: (0, 0))],
        out_specs=[pl.BlockSpec((tc, D), lambda i: (i, 0)),
                   pl.BlockSpec((4, D), lambda i: (0, 0)), pl.BlockSpec((1, D), lambda i: (0, 0))],
        out_shape=[jax.ShapeDtypeStruct((S, D), bf16), jax.ShapeDtypeStruct((4, D), f32),
                   jax.ShapeDtypeStruct((1, D), f32)],
        args=(duc_f, duc_f, duc_f, duc_b, duc_b, duc_b, proj, proj, proj, cw), comm=comm)


def _scan_scratch(tc):
    halves = [pltpu.VMEM((tc, 128), f32) for _ in range(2 * (LRU_GW // 128))]
    return [*halves, pltpu.VMEM((tc // 8, LRU_GW), f32), pltpu.VMEM((tc // 8, LRU_GW), f32)]


def _log_scan(a, b, row, n, reverse, steps):
    for s in steps:
        shift = a.shape[0] - s if reverse else s
        keep = (row < n - s) if reverse else (row >= s)
        a_sh = pltpu.roll(a, shift, 0)
        b_sh = pltpu.roll(b, shift, 0)
        b = jnp.where(keep, a * b_sh + b, b)
        a = jnp.where(keep, a * a_sh, a)
    return a, b


def _scan_chunk(a, b, carry, reverse, *scratch):
    tc, w = a.shape
    ng = tc // 8
    nl = w // 128
    sa_refs, sb_refs, sc_ref, st_ref = scratch[:nl], scratch[nl:2 * nl], scratch[2 * nl], scratch[2 * nl + 1]
    sub = lax.broadcasted_iota(jnp.int32, (8, w), 0)
    ag, bg = [], []
    for k in range(ng):
        ak, bk = _log_scan(a[8 * k:8 * k + 8], b[8 * k:8 * k + 8], sub, 8, reverse, (1, 2, 4))
        ag.append(ak)
        bg.append(bk)
    a = jnp.concatenate(ag, axis=0)
    b = jnp.concatenate(bg, axis=0)
    edge = 0 if reverse else 7
    for i in range(nl):
        sa_refs[i][...] = a[:, 128 * i:128 * (i + 1)]
        sb_refs[i][...] = b[:, 128 * i:128 * (i + 1)]
    ta = jnp.concatenate([r[pl.ds(edge, ng, stride=8), :] for r in sa_refs], axis=1)
    tb = jnp.concatenate([r[pl.ds(edge, ng, stride=8), :] for r in sb_refs], axis=1)
    grow = lax.broadcasted_iota(jnp.int32, (ng, w), 0)
    ta, tb = _log_scan(ta, tb, grow, ng, reverse, [1 << i for i in range(ng.bit_length() - 1)])
    state = tb + ta * carry
    st_ref[...] = state
    if reverse:
        sc_ref[...] = jnp.where(grow == ng - 1, carry, pltpu.roll(state, ng - 1, 0))
    else:
        sc_ref[...] = jnp.where(grow == 0, carry, pltpu.roll(state, 1, 0))
    h = jnp.concatenate([bg[k] + ag[k] * sc_ref[k:k + 1, :] for k in range(ng)], axis=0)
    return h, (st_ref[0:1, :] if reverse else st_ref[ng - 1:ng, :])


def _lru_gates(uc, w, p_ref):
    pre = jnp.dot(uc.astype(bf16), w, preferred_element_type=f32)
    r = _sigmoid(pre[:, :LRU_GW] + p_ref[0, 1:2, :])
    gi = _sigmoid(pre[:, LRU_GW:] + p_ref[0, 2:3, :])
    sp = _softplus(-p_ref[0, 0:1, :])
    log_a = -RGLRU_C * r * sp
    a = jnp.exp(log_a)
    x = 2.0 * log_a
    series = -x * (1.0 + x * (0.5 + x * (1.0 / 6 + x * (1.0 / 24))))
    beta = jnp.sqrt(jnp.maximum(jnp.where(x > -0.0625, series, 1.0 - a * a), 0.0))
    return r, gi, sp, a, beta


def _lru_fwd(uc, wg, lp, reverse, comm=()):
    S = uc.shape[0]
    tc = LRU_CHUNK
    rows = min(LRU_ROWS, S)
    nsub = rows // tc
    nblk = S // rows
    d = 1 if reverse else 0

    def bidx(c):
        return nblk - 1 - c if reverse else c

    def body(uc_ref, w_ref, p_ref, h_ref, carry_ref, *scan_scratch):
        @pl.when(pl.program_id(1) == 0)
        def _():
            carry_ref[...] = jnp.zeros_like(carry_ref)

        carry = carry_ref[...]
        for j in (reversed(range(nsub)) if reverse else range(nsub)):
            sl = slice(j * tc, (j + 1) * tc)
            ucv = uc_ref[sl, :]
            _, gi, _, a, beta = _lru_gates(ucv, w_ref[0], p_ref)
            h, carry = _scan_chunk(a, beta * (gi * ucv), carry, reverse, *scan_scratch)
            h_ref[sl, :] = h.astype(bf16)
        carry_ref[...] = carry

    return _hosted_call(
        body, name="lru_fwd_rev" if reverse else "lru_fwd", grid=(LRU_GROUPS, nblk),
        in_specs=[pl.BlockSpec((rows, LRU_GW), lambda g, c: (bidx(c), g)),
                  pl.BlockSpec((1, LRU_GW, 2 * LRU_GW), lambda g, c: (g, 0, d)),
                  pl.BlockSpec((1, 8, LRU_GW), lambda g, c: (d, 0, g))],
        out_specs=[pl.BlockSpec((rows, LRU_GW), lambda g, c: (bidx(c), g))],
        out_shape=[jax.ShapeDtypeStruct((S, D), bf16)],
        scratch_shapes=[pltpu.VMEM((1, LRU_GW), f32), *_scan_scratch(tc)],
        args=(uc, wg, lp), comm=comm)


def _lru_bwd(uc, dh, h, wg, lp, reverse, comm=()):
    S = uc.shape[0]
    tc = LRU_CHUNK_BWD
    rows = min(LRU_ROWS, S)
    nsub = rows // tc
    nblk = S // rows
    d = 1 if reverse else 0
    per = rows // HALO
    last8 = S // HALO - 1

    def bidx(c):
        return c if reverse else nblk - 1 - c

    def halo_idx(c):
        if reverse:
            return jnp.minimum((bidx(c) + 1) * per, last8)
        return jnp.maximum(bidx(c) * per - 1, 0)

    def body(uc_ref, dh_ref, h_ref, halo_ref, w_ref, p_ref, duc_ref, dw_ref, dp_ref, carry_ref, tmp_ref,
             *scan_scratch):
        c = pl.program_id(1)
        bi = bidx(c)

        @pl.when(c == 0)
        def _():
            carry_ref[...] = jnp.zeros_like(carry_ref)
            dw_ref[...] = jnp.zeros_like(dw_ref)
            dp_ref[...] = jnp.zeros_like(dp_ref)

        row = lax.broadcasted_iota(jnp.int32, (tc, LRU_GW), 0)
        carry = carry_ref[...]
        dw = jnp.zeros((LRU_GW, 2 * LRU_GW), f32)
        dsp = jnp.zeros((1, LRU_GW), f32)
        dba = jnp.zeros((1, LRU_GW), f32)
        dbx = jnp.zeros((1, LRU_GW), f32)
        for j in (range(nsub) if reverse else reversed(range(nsub))):
            sl = slice(j * tc, (j + 1) * tc)
            ucv = uc_ref[sl, :]
            ucb = ucv.astype(bf16)
            r, gi, sp, a, beta = _lru_gates(ucv, w_ref[0], p_ref)
            hv = h_ref[sl, :].astype(f32)
            dhv = dh_ref[sl, :].astype(f32)
            if reverse:
                alpha = jnp.where(row == 0, 1.0, pltpu.roll(a, 1, 0))
                gsc, _ = _scan_chunk(alpha, dhv, carry, False, *scan_scratch)
                if j < nsub - 1:
                    edge = h_ref[(j + 1) * tc:(j + 1) * tc + HALO, :].astype(f32)[0:1, :]
                else:
                    edge = jnp.where(bi < nblk - 1, halo_ref[...].astype(f32)[0:1, :], 0.0)
                h_nb = jnp.where(row == tc - 1, edge, pltpu.roll(hv, tc - 1, 0))
            else:
                alpha = jnp.where(row == tc - 1, 1.0, pltpu.roll(a, tc - 1, 0))
                gsc, _ = _scan_chunk(alpha, dhv, carry, True, *scan_scratch)
                if j > 0:
                    edge = h_ref[j * tc - HALO:j * tc, :].astype(f32)[HALO - 1:HALO, :]
                else:
                    edge = jnp.where(bi > 0, halo_ref[...].astype(f32)[HALO - 1:HALO, :], 0.0)
                h_nb = jnp.where(row == 0, edge, pltpu.roll(hv, 1, 0))
            tmp_ref[...] = a * gsc
            carry = tmp_ref[tc - 1:tc, :] if reverse else tmp_ref[0:1, :]

            da = gsc * h_nb
            dbeta = gsc * (gi * ucv)
            dl = da * a - dbeta * (a * a) / beta
            dr = dl * (-RGLRU_C * sp)
            dsp = dsp + jnp.sum(dl * (-RGLRU_C * r), axis=0, keepdims=True)
            dgi = gsc * beta * ucv
            dpre_r = dr * r * (1.0 - r)
            dpre_i = dgi * gi * (1.0 - gi)
            dba = dba + jnp.sum(dpre_r, axis=0, keepdims=True)
            dbx = dbx + jnp.sum(dpre_i, axis=0, keepdims=True)
            dpre = jnp.concatenate([dpre_r, dpre_i], axis=1).astype(bf16)
            back = lax.dot_general(dpre, w_ref[0], (((1,), (1,)), ((), ())), preferred_element_type=f32)
            duc_ref[sl, :] = (gsc * beta * gi + back).astype(bf16)
            dw = dw + lax.dot_general(ucb, dpre, (((0,), (0,)), ((), ())), preferred_element_type=f32)
        carry_ref[...] = carry
        dw_ref[0] += dw
        dlam = -dsp / (1.0 + jnp.exp(p_ref[0, 0:1, :]))
        prow = lax.broadcasted_iota(jnp.int32, (8, LRU_GW), 0)
        dp_ref[...] += (jnp.where(prow == 0, dlam, 0.0) + jnp.where(prow == 1, dba, 0.0)
                        + jnp.where(prow == 2, dbx, 0.0))

    chunk = pl.BlockSpec((rows, LRU_GW), lambda g, c: (bidx(c), g))
    return _hosted_call(
        body, name="lru_bwd_rev" if reverse else "lru_bwd", grid=(LRU_GROUPS, nblk),
        in_specs=[chunk, chunk, chunk,
                  pl.BlockSpec((HALO, LRU_GW), lambda g, c: (halo_idx(c), g)),
                  pl.BlockSpec((1, LRU_GW, 2 * LRU_GW), lambda g, c: (g, 0, d)),
                  pl.BlockSpec((1, 8, LRU_GW), lambda g, c: (d, 0, g))],
        out_specs=[chunk,
                   pl.BlockSpec((1, LRU_GW, 2 * LRU_GW), lambda g, c: (g, 0, 0)),
                   pl.BlockSpec((8, LRU_GW), lambda g, c: (0, g))],
        out_shape=[jax.ShapeDtypeStruct((S, D), bf16),
                   jax.ShapeDtypeStruct((LRU_GROUPS, LRU_GW, 2 * LRU_GW), f32),
                   jax.ShapeDtypeStruct((8, D), f32)],
        scratch_shapes=[pltpu.VMEM((1, LRU_GW), f32), pltpu.VMEM((tc, LRU_GW), f32), *_scan_scratch(tc)],
        args=(uc, dh, h, h, wg, lp), comm=comm)


def _slope(h):
    return 2.0 ** (-8.0 * (h + 1.0) / N_HEADS)


ATT_QB = 4


def _kv_specs(nb, col):
    return [pl.BlockSpec((BLK, N_KV * HEAD_DIM), lambda n: (jnp.maximum(ATT_QB * n - 1, 0), col)),
            pl.BlockSpec((ATT_QB * BLK, N_KV * HEAD_DIM), lambda n: (n, col)),
            pl.BlockSpec((BLK, N_KV * HEAD_DIM), lambda n: (jnp.minimum(ATT_QB * (n + 1), nb - 1), col))]


def _key_blocks(prev_ref, cur_ref, next_ref):
    return [prev_ref[...], *[cur_ref[BLK * s:BLK * (s + 1), :] for s in range(ATT_QB)], next_ref[...]]


def _dup_windows(r0, r1, r2):
    left = lax.broadcasted_iota(jnp.int32, (3 * BLK, 128), 1) < HEAD_DIM
    win = jnp.concatenate([r0, r1, r2], axis=0)
    out = []
    for i in range(N_KV // 2):
        t = win[:, i * 128:(i + 1) * 128]
        r = pltpu.roll(t, HEAD_DIM, 1)
        out += [jnp.where(left, t, r).astype(bf16), jnp.where(left, r, t).astype(bf16)]
    return out


def _attn_bias_init(bias_ref):
    k_loc = lax.broadcasted_iota(jnp.int32, (3 * BLK, BLK), 0)
    q_loc = lax.broadcasted_iota(jnp.int32, (3 * BLK, BLK), 1)
    adist = jnp.abs(q_loc + BLK - k_loc)
    adf = adist.astype(f32)
    for e in range(3):
        ok = adist <= WINDOW
        if e == 0:
            ok = ok & (k_loc >= BLK)
        if e == 2:
            ok = ok & (k_loc < 2 * BLK)
        for kv in range(N_KV):
            bias_ref[e, kv] = jnp.concatenate(
                [jnp.where(ok, (-_slope(4 * kv + j)) * adf, NEG_INF) for j in range(4)], axis=1)


def _stack_heads(ref, sub, kv, scale):
    left = lax.broadcasted_iota(jnp.int32, (BLK, 128), 1) < HEAD_DIM
    rows = []
    for pp in range(2):
        t = ref[BLK * sub:BLK * (sub + 1), (2 * kv + pp) * 128:(2 * kv + pp + 1) * 128]
        if scale != 1.0:
            t = t * scale
        zero = jnp.zeros_like(t)
        rows += [jnp.where(left, t, zero).astype(bf16), jnp.where(left, zero, t).astype(bf16)]
    return jnp.concatenate(rows, axis=0)


def _attn_softmax(qs, k2, bias, sink_ref, kv, stats=None):
    sink = jnp.concatenate([jnp.full((1, BLK), sink_ref[0, 4 * kv + j], f32) for j in range(4)], axis=1)
    s = lax.dot_general(k2, qs, (((1,), (1,)), ((), ())), preferred_element_type=f32) + bias
    m = jnp.maximum(jnp.max(s, axis=0, keepdims=True), sink) if stats is None else stats[0]
    p = jnp.exp(s - m)
    ps = jnp.exp(sink - m)
    inv = 1.0 / (jnp.sum(p, axis=0, keepdims=True) + ps) if stats is None else stats[1]
    return p, ps, m, inv


def _pair_tiles(t):
    return [jnp.concatenate([t[:HEAD_DIM, 256 * pp:256 * pp + 128],
                             t[HEAD_DIM:, 256 * pp + 128:256 * pp + 256]], axis=0).T for pp in range(2)]


def _attn_fwd(proj, sink, comm=()):
    S = proj.shape[0]
    nb = S // BLK
    assert nb >= 2 and nb % ATT_QB == 0

    def body(q_ref, k0, k1, k2_, v0, v1, v2_, sink_ref, o_ref, st_ref, bias_ref):
        n = pl.program_id(0)

        @pl.when(n == 0)
        def _():
            _attn_bias_init(bias_ref)

        kb = _key_blocks(k0, k1, k2_)
        vb = _key_blocks(v0, v1, v2_)
        for sub in range(ATT_QB):
            blk = ATT_QB * n + sub
            e = jnp.where(blk == 0, 0, jnp.where(blk == nb - 1, 2, 1))
            kk = _dup_windows(*kb[sub:sub + 3])
            vv = _dup_windows(*vb[sub:sub + 3])
            tiles = []
            for kv in range(N_KV):
                qs = _stack_heads(q_ref, sub, kv, HEAD_DIM ** -0.5)
                p, _, m, inv = _attn_softmax(qs, kk[kv], bias_ref[e, kv], sink_ref, kv)
                st_ref[sub, kv:kv + 1, :] = m
                st_ref[sub, N_KV + kv:N_KV + kv + 1, :] = inv
                ot = lax.dot_general(vv[kv], p.astype(bf16), (((0,), (0,)), ((), ())), preferred_element_type=f32)
                tiles += _pair_tiles(ot * inv)
            o_ref[BLK * sub:BLK * (sub + 1), :] = jnp.concatenate(tiles, axis=1).astype(bf16)

    return _hosted_call(
        body, name="attn_fwd", grid=(nb // ATT_QB,),
        in_specs=[pl.BlockSpec((ATT_QB * BLK, D), lambda n: (n, C_Q // D)),
                  *_kv_specs(nb, C_K // (N_KV * HEAD_DIM)), *_kv_specs(nb, C_V // (N_KV * HEAD_DIM)),
                  pl.BlockSpec(memory_space=pltpu.SMEM)],
        out_specs=[pl.BlockSpec((ATT_QB * BLK, D), lambda n: (n, 0)),
                   pl.BlockSpec((ATT_QB, 2 * N_KV, 4 * BLK), lambda n: (n, 0, 0))],
        out_shape=[jax.ShapeDtypeStruct((S, D), bf16), jax.ShapeDtypeStruct((nb, 2 * N_KV, 4 * BLK), f32)],
        scratch_shapes=[pltpu.VMEM((3, N_KV, 3 * BLK, 4 * BLK), f32)],
        args=(proj, proj, proj, proj, proj, proj, proj, sink), comm=comm)


def _attn_bwd(proj, sink, dyb, stats, comm=()):
    S = proj.shape[0]
    nb = S // BLK
    assert nb >= 2 and nb % ATT_QB == 0
    nsteps = nb // ATT_QB
    kvw = N_KV * HEAD_DIM

    def body(q_ref, k0, k1, k2_, v0, v1, v2_, sink_ref, do_ref, st_ref, dq_ref, dkv_out, ds_ref,
             bias_ref, dk_ref, dv_ref, dsk_ref, dkv_ref):
        n = pl.program_id(0)

        @pl.when(n == 0)
        def _():
            _attn_bias_init(bias_ref)
            dk_ref[...] = jnp.zeros_like(dk_ref)
            dv_ref[...] = jnp.zeros_like(dv_ref)
            dsk_ref[...] = jnp.zeros_like(dsk_ref)

        kb = _key_blocks(k0, k1, k2_)
        vb = _key_blocks(v0, v1, v2_)
        left3 = lax.broadcasted_iota(jnp.int32, (3 * BLK, 128), 1) < HEAD_DIM
        for sub in range(ATT_QB):
            blk = ATT_QB * n + sub
            e = jnp.where(blk == 0, 0, jnp.where(blk == nb - 1, 2, 1))
            kk = _dup_windows(*kb[sub:sub + 3])
            vv = _dup_windows(*vb[sub:sub + 3])
            start = pl.multiple_of(blk * BLK, BLK)
            dq_tiles, dks, dvs = [], [], []
            for kv in range(N_KV):
                qs = _stack_heads(q_ref, sub, kv, HEAD_DIM ** -0.5)
                dos = _stack_heads(do_ref, sub, kv, 1.0)
                stats = (st_ref[sub, kv:kv + 1, :], st_ref[sub, N_KV + kv:N_KV + kv + 1, :])
                p, ps, _, inv = _attn_softmax(qs, kk[kv], bias_ref[e, kv], sink_ref, kv, stats)
                pn = p * inv
                dp = lax.dot_general(vv[kv], dos, (((1,), (1,)), ((), ())), preferred_element_type=f32)
                delta = jnp.sum(pn * dp, axis=0, keepdims=True)
                dsc = (pn * (dp - delta)).astype(bf16)
                dsk_ref[kv:kv + 1, :] += delta * (ps * inv)
                dqt = lax.dot_general(kk[kv], dsc, (((0,), (0,)), ((), ())), preferred_element_type=f32)
                dq_tiles += _pair_tiles(dqt * (HEAD_DIM ** -0.5))
                dk = jnp.dot(dsc, qs, preferred_element_type=f32)
                dv = jnp.dot(pn.astype(bf16), dos, preferred_element_type=f32)
                dks.append(dk + pltpu.roll(dk, HEAD_DIM, 1))
                dvs.append(dv + pltpu.roll(dv, HEAD_DIM, 1))
            for jp in range(N_KV // 2):
                cols = slice(jp * 128, (jp + 1) * 128)
                dk_ref[pl.ds(start, 3 * BLK), cols] += jnp.where(left3, dks[2 * jp], dks[2 * jp + 1])
                dv_ref[pl.ds(start, 3 * BLK), cols] += jnp.where(left3, dvs[2 * jp], dvs[2 * jp + 1])
            dq_ref[BLK * sub:BLK * (sub + 1), :] = jnp.concatenate(dq_tiles, axis=1).astype(bf16)

        @pl.when(n == nsteps - 1)
        def _():
            rows = min(S, 512)
            for c in range(S // rows):
                dkv_ref[rows * c:rows * (c + 1), :kvw] = dk_ref[BLK + rows * c:BLK + rows * (c + 1), :].astype(bf16)
                dkv_ref[rows * c:rows * (c + 1), kvw:] = dv_ref[BLK + rows * c:BLK + rows * (c + 1), :].astype(bf16)
            pltpu.sync_copy(dkv_ref, dkv_out)
            lane = lax.broadcasted_iota(jnp.int32, (1, 128), 1)
            dsink = jnp.zeros((1, 128), f32)
            for h in range(N_HEADS):
                part = dsk_ref[h // 4:h // 4 + 1, (h % 4) * BLK:(h % 4 + 1) * BLK]
                dsink = dsink + jnp.where(lane == h, -jnp.sum(part), 0.0)
            ds_ref[...] = dsink

    acc = jax.ShapeDtypeStruct((S + 2 * BLK, N_KV * HEAD_DIM), f32)
    return _hosted_call(
        body, name="attn_bwd", grid=(nsteps,),
        in_specs=[pl.BlockSpec((ATT_QB * BLK, D), lambda n: (n, C_Q // D)),
                  *_kv_specs(nb, C_K // (N_KV * HEAD_DIM)), *_kv_specs(nb, C_V // (N_KV * HEAD_DIM)),
                  pl.BlockSpec(memory_space=pltpu.SMEM),
                  pl.BlockSpec((ATT_QB * BLK, D), lambda n: (n, 0)),
                  pl.BlockSpec((ATT_QB, 2 * N_KV, 4 * BLK), lambda n: (n, 0, 0))],
        out_specs=[pl.BlockSpec((ATT_QB * BLK, D), lambda n: (n, 0)), ANY_SPEC,
                   pl.BlockSpec((1, 128), lambda n: (0, 0))],
        out_shape=[jax.ShapeDtypeStruct((S, D), bf16), jax.ShapeDtypeStruct((S, 2 * kvw), bf16),
                   jax.ShapeDtypeStruct((1, 128), f32)],
        scratch_shapes=[pltpu.VMEM((3, N_KV, 3 * BLK, 4 * BLK), f32), pltpu.VMEM(acc.shape, f32),
                        pltpu.VMEM(acc.shape, f32), pltpu.VMEM((8, 4 * BLK), f32), pltpu.VMEM((S, 2 * kvw), bf16)],
        args=(proj, proj, proj, proj, proj, proj, proj, sink, dyb, stats), comm=comm)


def _merge_parts(hf, hb, g, z0, z1, yb, bg):
    g0 = _sigmoid(z0 + bg[:, :D].astype(bf16))
    g1 = _sigmoid(z1 + bg[:, D:].astype(bf16))
    gelu, dgelu = _gelu_and_grad(g)
    hs = hf + hb
    ya = hs * gelu
    return g0, g1, gelu, dgelu, hs, ya


def _merge_outproj(x, hf, hb, proj, yb, bg, w_out, tm=1024):
    S = x.shape[0]
    tm = min(tm, S)

    def body(x_ref, hf_ref, hb_ref, g_ref, z0_ref, z1_ref, yb_ref, bg_ref, w_ref, mg_ref, x1_ref):
        ybv = yb_ref[...]
        g0, g1, _, _, _, ya = _merge_parts(hf_ref[...], hb_ref[...], g_ref[...], z0_ref[...], z1_ref[...],
                                           ybv, bg_ref[...])
        mg = g0 * ya + g1 * ybv
        mg_ref[...] = mg
        x1_ref[...] = x_ref[...] + jnp.dot(mg, w_ref[...], preferred_element_type=f32)

    row = pl.BlockSpec((tm, D), lambda i: (i, 0))
    return pl.pallas_call(
        body, name="merge_outproj", grid=(S // tm,),
        in_specs=[row, row, row,
                  pl.BlockSpec((tm, D), lambda i: (i, C_G // D)),
                  pl.BlockSpec((tm, D), lambda i: (i, C_Z0 // D)),
                  pl.BlockSpec((tm, D), lambda i: (i, C_Z1 // D)),
                  row, pl.BlockSpec((1, 2 * D), lambda i: (0, 0)), pl.BlockSpec((D, D), lambda i: (0, 0))],
        out_specs=[row, row],
        out_shape=[jax.ShapeDtypeStruct((S, D), bf16), jax.ShapeDtypeStruct((S, D), f32)],
        compiler_params=_cparams())(x, hf, hb, proj, proj, proj, yb, bg, w_out)


def _ffn_out_loss(gu, x1, w_fo, g3, tgt, tm=256):
    S = x1.shape[0]
    tm = min(tm, S)

    def body(gt_ref, up_ref, x1_ref, w_ref, g_ref, t_ref, ff_ref, dx_ref, dxb_ref, loss_ref, dg_ref,
             dgt_ref, dup_ref):
        @pl.when(pl.program_id(0) == 0)
        def _():
            loss_ref[...] = jnp.zeros_like(loss_ref)
            dg_ref[...] = jnp.zeros_like(dg_ref)

        gt = gt_ref[...]
        up = up_ref[...]
        sg = _sigmoid(gt)
        silu = gt * sg
        ff = silu * up
        ff_ref[...] = ff
        x2 = x1_ref[...] + jnp.dot(ff, w_ref[...], preferred_element_type=f32)
        gv = g_ref[...]
        r = lax.rsqrt(jnp.mean(x2 * x2, axis=-1, keepdims=True) + EPS)
        xh = x2 * r
        diff = xh * gv - t_ref[...]
        loss_ref[...] += (0.5 / D) * jnp.sum(diff * diff)
        dy = diff * (1.0 / D)
        dg_ref[...] += jnp.sum(dy * xh, axis=0, keepdims=True)
        dxh = dy * gv
        dx = r * (dxh - xh * jnp.mean(dxh * xh, axis=-1, keepdims=True))
        dx_ref[...] = dx
        dxb = dx.astype(bf16)
        dxb_ref[...] = dxb
        dff = lax.dot_general(dxb, w_ref[...], (((1,), (1,)), ((), ())), preferred_element_type=f32)
        dup_ref[...] = (dff * silu.astype(f32)).astype(bf16)
        dgt_ref[...] = (dff * (up * (sg * (1.0 + gt * (1.0 - sg)))).astype(f32)).astype(bf16)

    row = pl.BlockSpec((tm, D), lambda i: (i, 0))
    vec = pl.BlockSpec((1, D), lambda i: (0, 0))
    wide = pl.BlockSpec((tm, D_FF), lambda i: (i, 0))
    wide_shape = jax.ShapeDtypeStruct((S, D_FF), bf16)
    return pl.pallas_call(
        body, name="ffn_out_loss", grid=(S // tm,),
        in_specs=[wide, pl.BlockSpec((tm, D_FF), lambda i: (i, 1)),
                  row, pl.BlockSpec((D_FF, D), lambda i: (0, 0)), vec, row],
        out_specs=[wide, row, row, pl.BlockSpec((1, 128), lambda i: (0, 0)), vec, wide, wide],
        out_shape=[wide_shape, jax.ShapeDtypeStruct((S, D), f32), jax.ShapeDtypeStruct((S, D), bf16),
                   jax.ShapeDtypeStruct((1, 128), f32), jax.ShapeDtypeStruct((1, D), f32), wide_shape, wide_shape],
        compiler_params=_cparams())(gu, gu, x1, w_fo, g3, tgt)


def _proj_bwd(pieces, wt, xres, g, dres, name, tm=512, comm=()):
    S = xres.shape[0]
    tm = min(tm, S)
    np_ = len(pieces)

    def body(*refs):
        p_refs = refs[:np_]
        w_refs = refs[np_:2 * np_]
        x_ref, g_ref, dres_ref, dx_ref, dxb_ref, dg_ref = refs[2 * np_:]

        @pl.when(pl.program_id(0) == 0)
        def _():
            dg_ref[...] = jnp.zeros_like(dg_ref)

        dn = jnp.dot(p_refs[0][...], w_refs[0][...], preferred_element_type=f32)
        for pr, wr in zip(p_refs[1:], w_refs[1:]):
            dn = dn + jnp.dot(pr[...], wr[...], preferred_element_type=f32)
        dxn, dgc = _rms_bwd(dn, x_ref[...], g_ref[...])
        dx = dres_ref[...] + dxn
        dx_ref[...] = dx
        dxb_ref[...] = dx.astype(bf16)
        dg_ref[...] += jnp.sum(dgc, axis=0, keepdims=True)

    row = pl.BlockSpec((tm, D), lambda i: (i, 0))
    vec = pl.BlockSpec((1, D), lambda i: (0, 0))
    return _hosted_call(
        body, name=name, grid=(S // tm,),
        in_specs=[*[pl.BlockSpec((tm, wd), functools.partial(lambda i, cb: (i, cb), cb=acb))
                    for _, acb, _, wd in pieces],
                  *[pl.BlockSpec((wd, D), functools.partial(lambda i, rb: (rb, 0), rb=wrb))
                    for _, _, wrb, wd in pieces],
                  row, vec, row],
        out_specs=[row, row, vec],
        out_shape=[jax.ShapeDtypeStruct((S, D), f32), jax.ShapeDtypeStruct((S, D), bf16),
                   jax.ShapeDtypeStruct((1, D), f32)],
        args=(*[p[0] for p in pieces], *[wt] * np_, xres, g, dres), comm=comm)


def _outproj_bwd(dx1b, w_out, hf, hb, proj, yb, bg, tm=1024):
    S = dx1b.shape[0]
    tm = min(tm, S)

    def body(dx_ref, w_ref, hf_ref, hb_ref, g_ref, z0_ref, z1_ref, yb_ref, bg_ref,
             dh_ref, dg_ref, dz_ref, dyb_ref, dbg_ref):
        @pl.when(pl.program_id(0) == 0)
        def _():
            dbg_ref[...] = jnp.zeros_like(dbg_ref)

        dm = lax.dot_general(dx_ref[...], w_ref[...], (((1,), (1,)), ((), ())), preferred_element_type=f32)
        ybv = yb_ref[...]
        g0, g1, gelu, dgelu, hs, ya = _merge_parts(hf_ref[...], hb_ref[...], g_ref[...], z0_ref[...],
                                                   z1_ref[...], ybv, bg_ref[...])
        dh_ref[...] = (dm * (g0 * gelu).astype(f32)).astype(bf16)
        dg_ref[...] = (dm * (g0 * hs * dgelu).astype(f32)).astype(bf16)
        dyb_ref[...] = (dm * g1.astype(f32)).astype(bf16)
        dz0 = dm * (ya * (g0 * (1.0 - g0))).astype(f32)
        dz1 = dm * (ybv * (g1 * (1.0 - g1))).astype(f32)
        dz = jnp.concatenate([dz0, dz1], axis=1)
        dz_ref[...] = dz.astype(bf16)
        dbg_ref[...] += jnp.sum(dz, axis=0, keepdims=True)

    row = pl.BlockSpec((tm, D), lambda i: (i, 0))
    return pl.pallas_call(
        body, name="outproj_bwd", grid=(S // tm,),
        in_specs=[row, pl.BlockSpec((D, D), lambda i: (0, 0)), row, row,
                  pl.BlockSpec((tm, D), lambda i: (i, C_G // D)),
                  pl.BlockSpec((tm, D), lambda i: (i, C_Z0 // D)),
                  pl.BlockSpec((tm, D), lambda i: (i, C_Z1 // D)),
                  row, pl.BlockSpec((1, 2 * D), lambda i: (0, 0))],
        out_specs=[row, row, pl.BlockSpec((tm, 2 * D), lambda i: (i, 0)), row,
                   pl.BlockSpec((1, 2 * D), lambda i: (0, 0))],
        out_shape=[jax.ShapeDtypeStruct((S, D), bf16), jax.ShapeDtypeStruct((S, D), bf16),
                   jax.ShapeDtypeStruct((S, 2 * D), bf16), jax.ShapeDtypeStruct((S, D), bf16),
                   jax.ShapeDtypeStruct((1, 2 * D), f32)],
        compiler_params=_cparams())(dx1b, w_out, hf, hb, proj, proj, proj, yb, bg)


def _block_diag_groups(w):
    w4 = w.reshape(LRU_GROUPS, 4, LRU_BLOCK, LRU_BLOCK)
    eye = jnp.eye(4, dtype=w.dtype)
    return jnp.einsum("ghij,hk->ghikj", w4, eye).reshape(LRU_GROUPS, LRU_GW, LRU_GW)


def _diag_blocks(dw):
    d5 = dw.reshape(LRU_GROUPS, 4, LRU_BLOCK, 4, LRU_BLOCK)
    return jnp.stack([d5[:, h, :, h, :] for h in range(4)], axis=1).reshape(LRU_HEADS, LRU_BLOCK, LRU_BLOCK)


def _local_step(x, tgt, small, env, before=lambda name: (), after=lambda name, got: None):
    S = x.shape[0]
    g1, g2, g3 = small["norm_mix_g"], small["norm_ffn_g"], small["norm_final_g"]
    bg, cb, sink = small["b_gate"], small["conv_b"], small["attn_sink"]

    def hosted(name, fn, *args, **kw):
        outs, got = fn(*args, comm=tuple(before(name)), **kw)
        after(name, got)
        return outs

    (xn,) = hosted("norm_x", _rmsnorm_bf16, x, g1, "norm_x")
    cw = small["conv_w"]
    wg = jnp.concatenate([_block_diag_groups(small["lru_wa"][0]), _block_diag_groups(small["lru_wx"][0]),
                          _block_diag_groups(small["lru_wa"][1]), _block_diag_groups(small["lru_wx"][1])],
                         axis=2).astype(bf16)
    zeros5 = jnp.zeros((5, D), f32)
    lp = jnp.stack([jnp.concatenate([small["lru_lambda"][d:d + 1], small["lru_ba"][d:d + 1],
                                     small["lru_bx"][d:d + 1], zeros5], axis=0) for d in range(2)])
    (proj,) = hosted("inproj", _matmul_t, xn, env["w_in_t"], "inproj", tm=4096, tn=512,
                     row_block=lambda j: jnp.where(j < 6, j, jnp.where(j < 10, j + 1, 6)))
    uc = _conv_fwd(proj, cw, cb)
    (hf,), _ = _lru_fwd(uc, wg, lp, False)
    (hb,), _ = _lru_fwd(uc, wg, lp, True)
    yb, attn_stats = hosted("attn_fwd", _attn_fwd, proj, sink)
    merged, x1 = _merge_outproj(x, hf, hb, proj, yb, bg, env["w_out"])
    (xn2, gu), _ = _norm_matmul(x1, g2, env["w_fi_t"], "norm_ffn_in", tn=D_FF)
    ff, dx2, dx2b, loss, dg3, dgt, dup = _ffn_out_loss(gu, x1, env["w_fo"], g3, tgt)

    env["dw_fo"] = _mm_tn(ff, dx2b, "dw_ffn_out", tk=1408, tn=1024)
    dx1, dx1b, dg2 = hosted("ffn_in_bwd", _proj_bwd, [(dgt, 0, 0, D_FF), (dup, 0, 1, D_FF)], env["w_fi_t"],
                            x1, g2, dx2, "ffn_in_bwd")
    dw_gate = _mm_tn(dgt, xn2, "dw_ffn_in_gate", tk=1408, tn=1024, out_rows=2 * D_FF)
    env["dw_fi_t"] = _mm_tn(dup, xn2, "dw_ffn_in_up", tk=1408, tn=1024, into=dw_gate, row=D_FF // 1408)
    env["dw_out"] = _mm_tn(merged, dx1b, "dw_out", tk=1024, tn=1024)
    dh, dgl, dz, dyb, dbg = _outproj_bwd(dx1b, env["w_out"], hf, hb, proj, yb, bg)
    dq, dkv, dsink = hosted("attn_bwd", _attn_bwd, proj, sink, dyb, attn_stats)
    duc_f, dwg_f, dp_f = hosted("lru_bwd", _lru_bwd, uc, dh, hf, wg, lp, False)
    (duc_b, dwg_b, dp_b), _ = _lru_bwd(uc, dh, hb, wg, lp, True)
    env["grads_early"] = {
        "loss": loss[:, :1], "b_gate": dbg,
        "lru_lambda": jnp.concatenate([dp_f[0:1], dp_b[0:1]], axis=0),
        "lru_wa": jnp.stack([_diag_blocks(dwg_f[:, :, :LRU_GW]), _diag_blocks(dwg_b[:, :, :LRU_GW])]),
        "lru_ba": jnp.concatenate([dp_f[1:2], dp_b[1:2]], axis=0),
        "lru_wx": jnp.stack([_diag_blocks(dwg_f[:, :, LRU_GW:]), _diag_blocks(dwg_b[:, :, LRU_GW:])]),
        "lru_bx": jnp.concatenate([dp_f[2:3], dp_b[2:3]], axis=0),
        "attn_sink": dsink[:, :N_HEADS], "norm_ffn_g": dg2, "norm_final_g": dg3,
    }
    du, dcw, dcb = hosted("conv_bwd", _conv_bwd, duc_f, duc_b, proj, cw)
    dw_in = _mm_tn(du, xn, "dw_in_u", tk=1024, tn=1024, out_rows=IN_W)
    dw_in = _mm_tn(dgl, xn, "dw_in_g", tk=1024, tn=1024, into=dw_in, row=1)
    dw_in = _mm_tn(dq, xn, "dw_in_q", tk=1024, tn=1024, into=dw_in, row=2)
    dw_in = _mm_tn(dkv, xn, "dw_in_kv", tk=512, tn=1024, into=dw_in, row=3072 // 512)
    env["dw_in_t"] = _mm_tn(dz, xn, "dw_in_z", tk=512, tn=1024, tmc=4096, into=dw_in, row=3584 // 512)
    col_pieces = [(du, 0, 0, D), (dgl, 0, 1, D), (dq, 0, 2, D), (dkv, 0, 3072 // 512, 512),
                  *[(dz, i, 3584 // 512 + i, 512) for i in range(4)]]
    dx, _, dg1 = hosted("inproj_bwd", _proj_bwd, col_pieces, env["w_in_t"], x, g1, dx1, "inproj_bwd")

    grads = dict(env["grads_early"], norm_mix_g=dg1, conv_w=dcw, conv_b=dcb)
    return dx, grads


def _adamw(gparts, w, m, v, name, tr=256):
    n, rows, cols = gparts.shape
    tr = _div_tile(rows, tr)
    c1 = 1.0 - ADAM_B1 ** ADAM_STEP
    c2 = 1.0 - ADAM_B2 ** ADAM_STEP

    def body(g_ref, w_ref, m_ref, v_ref, go_ref, d_ref, mo_ref, vo_ref):
        g = g_ref[0].astype(f32)
        for j in range(1, n):
            g = g + g_ref[j].astype(f32)
        mn = ADAM_B1 * m_ref[0] + (1.0 - ADAM_B1) * g
        vn = ADAM_B2 * v_ref[0] + (1.0 - ADAM_B2) * (g * g)
        m_hat = mn / c1
        v_hat = vn / c2
        go_ref[0] = g
        d_ref[0] = -ADAM_LR * (m_hat / (jnp.sqrt(v_hat) + ADAM_EPS) + ADAM_WD * w_ref[0])
        mo_ref[0] = mn
        vo_ref[0] = vn

    blk = pl.BlockSpec((1, tr, cols), lambda i: (0, i, 0))
    shp = jax.ShapeDtypeStruct((1, rows, cols), f32)
    return pl.pallas_call(
        body, name=name, grid=(rows // tr,),
        in_specs=[pl.BlockSpec((n, tr, cols), lambda i: (0, i, 0)), blk, blk, blk],
        out_specs=[blk, blk, blk, blk], out_shape=[shp, shp, shp, shp],
        compiler_params=_cparams())(gparts, w, m, v)


def _sum_parts(parts, name):
    n, rows, cols = parts.shape

    def body(p_ref, o_ref):
        acc = p_ref[0].astype(f32)
        for j in range(1, n):
            acc = acc + p_ref[j].astype(f32)
        o_ref[...] = acc

    return pl.pallas_call(
        body, name=name, out_shape=jax.ShapeDtypeStruct((rows, cols), f32),
        compiler_params=_cparams())(parts)


def _pack_rows(arrs, dtype=f32):
    rows, spans, at = [], [], 0
    for a in arrs:
        flat = a.reshape(-1).astype(dtype)
        nr = -(-flat.shape[0] // 1024)
        rows.append(jnp.pad(flat, (0, nr * 1024 - flat.shape[0])).reshape(nr, 1024))
        spans.append((at, nr))
        at += nr
    pad = (-at) % 16
    if pad:
        rows.append(jnp.zeros((pad, 1024), dtype))
    return jnp.concatenate(rows, axis=0), spans


def _unpack_rows(packed, spans, shapes):
    out = []
    for (at, nr), shp in zip(spans, shapes):
        n = math.prod(shp)
        out.append(packed[at:at + nr].reshape(-1)[:n].reshape(shp))
    return out


BIG = ("w_in", "w_out", "w_ffn_in", "w_ffn_out")
SMALL_REPL = ("norm_mix_g", "b_gate", "conv_b", "attn_sink", "norm_ffn_g", "norm_final_g")
GATE_W = ("lru_wa", "lru_wx")
SMALL_SHARD = ("conv_w", "lru_lambda", "lru_ba", "lru_bx")
ORDER = ("norm_mix_g", "w_in", "b_gate", "conv_w", "conv_b", "lru_lambda", "lru_wa", "lru_ba", "lru_wx",
         "lru_bx", "attn_sink", "w_out", "norm_ffn_g", "w_ffn_in", "w_ffn_out", "norm_final_g")
EARLY_F32 = ("loss", "b_gate", "lru_lambda", "lru_ba", "lru_bx", "attn_sink", "norm_ffn_g", "norm_final_g")
LATE = ("norm_mix_g", "conv_w", "conv_b")


def kernel(x, norm_mix_g, w_in, b_gate, conv_w, conv_b, lru_lambda, lru_wa, lru_ba, lru_wx, lru_bx, attn_sink, w_out, norm_ffn_g, w_ffn_in, w_ffn_out, norm_final_g, loss_target, m_norm_mix_g, m_w_in, m_b_gate, m_conv_w, m_conv_b, m_lru_lambda, m_lru_wa, m_lru_ba, m_lru_wx, m_lru_bx, m_attn_sink, m_w_out, m_norm_ffn_g, m_w_ffn_in, m_w_ffn_out, m_norm_final_g, v_norm_mix_g, v_w_in, v_b_gate, v_conv_w, v_conv_b, v_lru_lambda, v_lru_wa, v_lru_ba, v_lru_wx, v_lru_bx, v_attn_sink, v_w_out, v_norm_ffn_g, v_w_ffn_in, v_w_ffn_out, v_norm_final_g):
    w = dict(norm_mix_g=norm_mix_g, w_in=w_in, b_gate=b_gate, conv_w=conv_w, conv_b=conv_b, lru_lambda=lru_lambda,
             lru_wa=lru_wa, lru_ba=lru_ba, lru_wx=lru_wx, lru_bx=lru_bx, attn_sink=attn_sink, w_out=w_out,
             norm_ffn_g=norm_ffn_g, w_ffn_in=w_ffn_in, w_ffn_out=w_ffn_out, norm_final_g=norm_final_g)
    m = dict(norm_mix_g=m_norm_mix_g, w_in=m_w_in, b_gate=m_b_gate, conv_w=m_conv_w, conv_b=m_conv_b,
             lru_lambda=m_lru_lambda, lru_wa=m_lru_wa, lru_ba=m_lru_ba, lru_wx=m_lru_wx, lru_bx=m_lru_bx,
             attn_sink=m_attn_sink, w_out=m_w_out, norm_ffn_g=m_norm_ffn_g, w_ffn_in=m_w_ffn_in,
             w_ffn_out=m_w_ffn_out, norm_final_g=m_norm_final_g)
    v = dict(norm_mix_g=v_norm_mix_g, w_in=v_w_in, b_gate=v_b_gate, conv_w=v_conv_w, conv_b=v_conv_b,
             lru_lambda=v_lru_lambda, lru_wa=v_lru_wa, lru_ba=v_lru_ba, lru_wx=v_lru_wx, lru_bx=v_lru_bx,
             attn_sink=v_attn_sink, w_out=v_w_out, norm_ffn_g=v_norm_ffn_g, w_ffn_in=v_w_ffn_in,
             w_ffn_out=v_w_ffn_out, norm_final_g=v_norm_final_g)
    me = 4 * lax.axis_index("x") + 2 * lax.axis_index("y") + lax.axis_index("c")

    def shard_t(a):
        return jnp.swapaxes(a[0], 0, 1)

    def rows_parts(g):
        return g.reshape(N_DEV, -1, g.shape[1])

    shard_rows = jnp.concatenate([w[n][0] for n in SMALL_SHARD], axis=0)
    small = {n: w[n] for n in ("norm_mix_g", "b_gate", "conv_b", "attn_sink", "norm_ffn_g")}
    small["lru_wa"], small["lru_wx"] = lru_wa[0], lru_wx[0]
    small["norm_final_g"] = norm_final_g.reshape(1, D)
    env, recv = {}, {}

    def before(name):
        if name == "norm_x":
            return [(shard_t(w_in).astype(bf16), False), (shard_rows, False)]
        if name == "inproj":
            return [(w_out[0].astype(bf16), False), (w_ffn_out[0].astype(bf16), False)]
        if name == "attn_fwd":
            return [(shard_t(w_ffn_in).astype(bf16), False)]
        if name == "ffn_in_bwd":
            return [(rows_parts(env["dw_fo"]), True)]
        if name == "attn_bwd":
            return [(rows_parts(env["dw_out"]), True)]
        if name == "lru_bwd":
            return [(rows_parts(env["dw_fi_t"]), True)]
        if name == "conv_bwd":
            ge = env["grads_early"]
            p32, env["early_f32_spans"] = _pack_rows([ge[n] for n in EARLY_F32])
            return [(p32, False), *[(ge[n].astype(bf16).reshape(-1, LRU_BLOCK), False) for n in GATE_W]]
        if name == "inproj_bwd":
            return [(rows_parts(env["dw_in_t"]), True)]
        return []

    def after(name, got):
        if name == "norm_x":
            env["w_in_t"] = got[0].reshape(IN_W, D)
            full_rows = jnp.swapaxes(got[1], 0, 1).reshape(shard_rows.shape[0], -1)
            small["conv_w"], small["lru_lambda"] = full_rows[0:4], full_rows[4:6]
            small["lru_ba"], small["lru_bx"] = full_rows[6:8], full_rows[8:10]
        elif name == "inproj":
            env["w_out"], env["w_fo"] = got[0].reshape(D, D), got[1].reshape(D_FF, D)
        elif name == "attn_fwd":
            env["w_fi_t"] = got[0].reshape(2 * D_FF, D)
        elif name == "ffn_in_bwd":
            recv["w_ffn_out"] = got[0]
        elif name == "attn_bwd":
            recv["w_out"] = got[0]
        elif name == "lru_bwd":
            recv["w_ffn_in"] = got[0]
        elif name == "conv_bwd":
            recv["early_f32"], recv["lru_wa"], recv["lru_wx"] = got
        elif name == "inproj_bwd":
            recv["w_in"] = got[0]

    grad_x, grads = _local_step(x[0], loss_target[0], small, env, before, after)

    outs = {}
    for name in ("w_out", "w_ffn_out"):
        outs[name] = _adamw(recv[name], w[name], m[name], v[name], "adamw_" + name)
    for name in ("w_in", "w_ffn_in"):
        t = lambda a: jnp.swapaxes(a, 1, 2)
        outs[name] = [t(r) for r in _adamw(recv[name], t(w[name]), t(m[name]), t(v[name]), "adamw_" + name)]
    for name in GATE_W:
        t = lambda a: a.reshape(1, -1, LRU_BLOCK)
        res = _adamw(recv[name], t(w[name]), t(m[name]), t(v[name]), "adamw_" + name)
        outs[name] = [r.reshape(w[name].shape) for r in res]

    small_names = SMALL_REPL + SMALL_SHARD
    late_packed, late_spans = _pack_rows([grads[n] for n in LATE])
    (got_late,) = _exchange([(late_packed, False)], "gather_late_grads")
    summed = {}
    for names, got, spans, tag in ((EARLY_F32, recv["early_f32"], env["early_f32_spans"], "early_f32"),
                                   (LATE, got_late, late_spans, "late")):
        total = _sum_parts(got, "sum_small_" + tag)
        summed.update(zip(names, _unpack_rows(total, spans, [grads[n].shape for n in names])))
    loss = summed["loss"].reshape(())
    gsm = {n: summed[n].reshape(w[n].shape) for n in SMALL_REPL}
    for n in SMALL_SHARD:
        full = summed[n]
        gsm[n] = lax.dynamic_slice_in_dim(full, me * 128, 128, axis=1).reshape(w[n].shape)
    pk = lambda dct: _pack_rows([dct[n] for n in small_names])[0]
    gp, sp = _pack_rows([gsm[n] for n in small_names])
    res = _adamw(gp[None], pk(w)[None], pk(m)[None], pk(v)[None], "adamw_small")
    sshapes = [w[n].shape for n in small_names]
    for idx, t in enumerate(res):
        for n, a in zip(small_names, _unpack_rows(t[0], sp, sshapes)):
            outs.setdefault(n, [None] * 4)[idx] = a

    result = [loss, grad_x[None]]
    for idx in range(4):
        result += [outs[n][idx] for n in ORDER]
    return tuple(result)
```

```python
import functools
import math

import jax
import jax.numpy as jnp
from jax import lax
from jax.experimental import pallas as pl
from jax.experimental.pallas import tpu as pltpu

f32 = jnp.float32
bf16 = jnp.bfloat16

D = 1024
D_FF = 2816
IN_W = 5632
N_HEADS = 16
N_KV = 4
HEAD_DIM = 64
WINDOW = 128
BLK = 128
LRU_HEADS = 16
LRU_BLOCK = 64
LRU_GROUPS = 4
LRU_GW = 256
LRU_CHUNK = 64
LRU_CHUNK_BWD = 512
LRU_ROWS = 4096
RGLRU_C = 8.0
EPS = 1e-6
NEG_INF = -1e30
N_DEV = 8

ADAM_LR = 0.001
ADAM_B1 = 0.9
ADAM_B2 = 0.999
ADAM_EPS = 1e-08
ADAM_WD = 0.01
ADAM_STEP = 10

VMEM_MB = 56

C_U, C_G, C_Q, C_Z0, C_Z1, C_K, C_V = 0, 1024, 2048, 3072, 4096, 5120, 5376


def _cparams(vmem_mb=VMEM_MB):
    return pltpu.CompilerParams(vmem_limit_bytes=vmem_mb << 20)


def _div_tile(n, pref):
    if n <= pref:
        return n
    return max(t for t in range(8, pref + 1, 8) if n % t == 0)


def _sigmoid(x):
    return 0.5 * jnp.tanh(0.5 * x) + 0.5


def _log1p(x):
    u = 1.0 + x
    d = u - 1.0
    return jnp.where(d == 0.0, x, jnp.log(u) * (x / jnp.where(d == 0.0, 1.0, d)))


def _softplus(x):
    return jnp.maximum(x, 0.0) + _log1p(jnp.exp(-jnp.abs(x)))


def _gelu_and_grad(x):
    c = math.sqrt(2.0 / math.pi)
    inner = c * (x + 0.044715 * (x * x * x))
    t = jnp.tanh(inner)
    gelu = 0.5 * x * (1.0 + t)
    dinner = c * (1.0 + 3 * 0.044715 * (x * x))
    dgelu = 0.5 * (1.0 + t) + 0.5 * x * (1.0 - t * t) * dinner
    return gelu, dgelu


def _rms_bwd(dn, xv, g):
    r = lax.rsqrt(jnp.mean(xv * xv, axis=-1, keepdims=True) + EPS)
    xh = xv * r
    dxh = dn * g
    dx = r * (dxh - xh * jnp.mean(dxh * xh, axis=-1, keepdims=True))
    return dx, dn * xh


ANY_SPEC = pl.BlockSpec(memory_space=pl.ANY)


def _comm_out_shape(src, scatter):
    return jax.ShapeDtypeStruct((N_DEV, *(src.shape[1:] if scatter else src.shape)), src.dtype)


def _comm_sems():
    return [pltpu.SemaphoreType.DMA((N_DEV - 1,)), pltpu.SemaphoreType.DMA((N_DEV - 1,)), pltpu.SemaphoreType.DMA]


def _scatter_descs(src_ref, out_ref, send_sems, recv_sems, local_sem):
    x, y, c = lax.axis_index("x"), lax.axis_index("y"), lax.axis_index("c")
    me = 4 * x + 2 * y + c
    descs = [pltpu.make_async_copy(src_ref.at[me], out_ref.at[me], local_sem)]
    for k in range(1, N_DEV):
        px, py, pc = x ^ (k >> 2), y ^ ((k >> 1) & 1), c ^ (k & 1)
        descs.append(pltpu.make_async_remote_copy(
            src_ref=src_ref.at[4 * px + 2 * py + pc], dst_ref=out_ref.at[me],
            send_sem=send_sems.at[k - 1], recv_sem=recv_sems.at[k - 1],
            device_id=(px, py, pc), device_id_type=pl.DeviceIdType.MESH))
    return descs


def _gather_copies(src_ref, out_ref, send_sems, recv_sems, local_sem, which):
    x, y, c = lax.axis_index("x"), lax.axis_index("y"), lax.axis_index("c")
    me, sibling = (x, y, c), (x, y, 1 - c)
    chips = [(1 - x, y), (x, 1 - y), (1 - x, 1 - y)]

    def slot(px, py, pc):
        return out_ref.at[4 * px + 2 * py + pc]

    def copy(k, block, to, src=None):
        return pltpu.make_async_remote_copy(
            src_ref=slot(*block) if src is None else src, dst_ref=slot(*block),
            send_sem=send_sems.at[k], recv_sem=recv_sems.at[k], device_id=to, device_id_type=pl.DeviceIdType.MESH)

    make = {
        "local": lambda: pltpu.make_async_copy(src_ref, slot(*me), local_sem),
        "first": lambda: [copy(0, me, sibling, src=src_ref)] + [copy(1 + j, me, (*chip, c), src=src_ref)
                                                                 for j, chip in enumerate(chips)],
        "passed": lambda: [copy(4 + j, (*chip, c), sibling) for j, chip in enumerate(chips)],
        "landed": lambda: [copy(1 + j, (*chip, c), me) for j, chip in enumerate(chips)],
        "later": lambda: [copy(0, sibling, me)] + [copy(4 + j, (*chip, 1 - c), me) for j, chip in enumerate(chips)],
    }
    return [make[name]() for name in which]


def _comm_start(src_ref, out_ref, sems, scatter):
    if scatter:
        for d in _scatter_descs(src_ref, out_ref, *sems):
            d.start()
    else:
        local, first = _gather_copies(src_ref, out_ref, *sems, which=("local", "first"))
        local.start()
        for cp in first:
            cp.start()


def _comm_pass_on(src_ref, out_ref, sems, scatter):
    if not scatter:
        landed, passed = _gather_copies(src_ref, out_ref, *sems, which=("landed", "passed"))
        for arrived, onward in zip(landed, passed):
            arrived.wait_recv()
            onward.start()


def _comm_finish(src_ref, out_ref, sems, scatter):
    if scatter:
        for d in _scatter_descs(src_ref, out_ref, *sems):
            d.wait()
    else:
        later, first, passed, local = _gather_copies(src_ref, out_ref, *sems,
                                                     which=("later", "first", "passed", "local"))
        for cp in later:
            cp.wait_recv()
        for cp in first + passed:
            cp.wait_send()
        local.wait()


def _exchange(comm, name):
    nc = len(comm)

    def body(*refs):
        srcs, outs, sems = refs[:nc], refs[nc:2 * nc], refs[2 * nc:]
        for stage in (_comm_start, _comm_pass_on, _comm_finish):
            for i in range(nc):
                stage(srcs[i], outs[i], sems[3 * i:3 * i + 3], comm[i][1])

    return pl.pallas_call(
        body, name=name, in_specs=[ANY_SPEC] * nc, out_specs=[ANY_SPEC] * nc,
        out_shape=[_comm_out_shape(*c) for c in comm],
        scratch_shapes=[s for _ in comm for s in _comm_sems()],
    )(*[c[0] for c in comm])


def _hosted_call(body, *, name, grid, in_specs, out_specs, out_shape, args, scratch_shapes=(), comm=()):
    nin, nout, nscr, nc = len(in_specs), len(out_specs), len(scratch_shapes), len(comm)
    steps = math.prod(grid)

    def wrapped(*refs):
        ins = refs[:nin]
        csrc = refs[nin:nin + nc]
        outs = refs[nin + nc:nin + nc + nout]
        cout = refs[nin + nc + nout:nin + 2 * nc + nout]
        scr = refs[nin + 2 * nc + nout:]
        sems = scr[nscr:]

        def at(step, stage):
            lin = 0
            for a in range(len(grid)):
                lin = lin * grid[a] + pl.program_id(a)

            @pl.when(lin == step)
            def _():
                for i in range(nc):
                    stage(csrc[i], cout[i], sems[3 * i:3 * i + 3], comm[i][1])

        if nc:
            at(0, _comm_start)

        body(*ins, *outs, *scr[:nscr])

        if nc:
            at((3 * (steps - 1)) // 4, _comm_pass_on)
            at(steps - 1, _comm_finish)

    res = pl.pallas_call(
        wrapped, name=name, grid=grid,
        in_specs=[*in_specs, *[ANY_SPEC] * nc], out_specs=[*out_specs, *[ANY_SPEC] * nc],
        out_shape=[*out_shape, *[_comm_out_shape(*c) for c in comm]],
        scratch_shapes=[*scratch_shapes, *[s for _ in comm for s in _comm_sems()]],
        compiler_params=_cparams())(*args, *[c[0] for c in comm])
    return res[:nout], res[nout:]


def _rmsnorm_bf16(x, g, name, tm=1024, comm=()):
    S, dm = x.shape
    tm = min(tm, S)

    def body(x_ref, g_ref, xn_ref):
        xv = x_ref[...]
        r = lax.rsqrt(jnp.mean(xv * xv, axis=-1, keepdims=True) + EPS)
        xn_ref[...] = ((xv * r) * g_ref[...]).astype(bf16)

    row = pl.BlockSpec((tm, dm), lambda i: (i, 0))
    return _hosted_call(
        body, name=name, grid=(S // tm,), in_specs=[row, pl.BlockSpec((1, dm), lambda i: (0, 0))],
        out_specs=[row], out_shape=[jax.ShapeDtypeStruct((S, dm), bf16)], args=(x, g), comm=comm)


def _matmul_t(a, wt, name, tm=2048, tn=512, row_block=lambda j: j, comm=()):
    S, dm = a.shape
    n = wt.shape[0]
    tm = min(tm, S)

    def body(a_ref, w_ref, o_ref):
        o_ref[...] = lax.dot_general(a_ref[...], w_ref[...], (((1,), (1,)), ((), ())),
                                     preferred_element_type=f32).astype(bf16)

    return _hosted_call(
        body, name=name, grid=(S // tm, n // tn),
        in_specs=[pl.BlockSpec((tm, dm), lambda i, j: (i, 0)),
                  pl.BlockSpec((tn, dm), lambda i, j: (row_block(j), 0))],
        out_specs=[pl.BlockSpec((tm, tn), lambda i, j: (i, j))],
        out_shape=[jax.ShapeDtypeStruct((S, n), bf16)], args=(a, wt), comm=comm)


def _norm_matmul(x, g, wt, name, tm=1024, tn=1408, row_block=lambda j: j, comm=()):
    S, dm = x.shape
    n = wt.shape[0]
    tm = min(tm, S)

    def body(x_ref, g_ref, w_ref, xn_ref, o_ref):
        @pl.when(pl.program_id(1) == 0)
        def _():
            xv = x_ref[...]
            r = lax.rsqrt(jnp.mean(xv * xv, axis=-1, keepdims=True) + EPS)
            xn_ref[...] = ((xv * r) * g_ref[...]).astype(bf16)

        o_ref[...] = lax.dot_general(xn_ref[...], w_ref[...], (((1,), (1,)), ((), ())),
                                     preferred_element_type=f32).astype(bf16)

    return _hosted_call(
        body, name=name, grid=(S // tm, n // tn),
        in_specs=[pl.BlockSpec((tm, dm), lambda i, j: (i, 0)),
                  pl.BlockSpec((1, dm), lambda i, j: (0, 0)),
                  pl.BlockSpec((tn, dm), lambda i, j: (row_block(j), 0))],
        out_specs=[pl.BlockSpec((tm, dm), lambda i, j: (i, 0)),
                   pl.BlockSpec((tm, tn), lambda i, j: (i, j))],
        out_shape=[jax.ShapeDtypeStruct((S, dm), bf16), jax.ShapeDtypeStruct((S, n), bf16)],
        args=(x, g, wt), comm=comm)


def _mm_tn(a, b, name, tk, tn, tmc=2048, into=None, row=0, out_rows=None):
    m, ka = a.shape
    n = b.shape[1]
    tmc = min(tmc, m)
    nk = m // tmc

    def body(a_ref, b_ref, *rest):
        o_ref, acc_ref = rest[-2:]
        k = pl.program_id(2)
        part = lax.dot_general(a_ref[...], b_ref[...], (((0,), (0,)), ((), ())), preferred_element_type=f32)

        @pl.when(k == 0)
        def _():
            acc_ref[...] = part

        @pl.when(k > 0)
        def _():
            acc_ref[...] += part

        @pl.when(k == nk - 1)
        def _():
            o_ref[...] = acc_ref[...].astype(bf16)

    in_specs = [pl.BlockSpec((tmc, tk), lambda i, j, k: (k, i)), pl.BlockSpec((tmc, tn), lambda i, j, k: (k, j))]
    if into is None:
        return pl.pallas_call(
            body, name=name, grid=(ka // tk, n // tn, nk), in_specs=in_specs,
            out_specs=pl.BlockSpec((tk, tn), lambda i, j, k: (i + row, j)),
            out_shape=jax.ShapeDtypeStruct((out_rows or ka, n), bf16),
            scratch_shapes=[pltpu.VMEM((tk, tn), f32)],
            compiler_params=_cparams())(a, b)
    return pl.pallas_call(
        body, name=name, grid=(ka // tk, n // tn, nk), in_specs=[*in_specs, ANY_SPEC],
        out_specs=pl.BlockSpec((tk, tn), lambda i, j, k: (i + row, j)),
        out_shape=jax.ShapeDtypeStruct(into.shape, into.dtype),
        scratch_shapes=[pltpu.VMEM((tk, tn), f32)], input_output_aliases={2: 0},
        compiler_params=_cparams())(a, b, into)


HALO = 16


def _rows_at(ext, o, tc):
    if o == 0:
        return ext[HALO:HALO + tc]
    return pltpu.roll(ext, (-o) % ext.shape[0], 0)[HALO:HALO + tc]


def _halo_specs(tc, S, width, col):
    per = tc // HALO
    last = S // HALO - 1
    return (pl.BlockSpec((tc, width), lambda i: (i, col)),
            pl.BlockSpec((HALO, width), lambda i: (jnp.maximum(i * per - 1, 0), col)),
            pl.BlockSpec((HALO, width), lambda i: (jnp.minimum((i + 1) * per, last), col)))


def _extended(cur_ref, prev_ref, next_ref, i, nsteps):
    prev = jnp.where(i > 0, prev_ref[...].astype(f32), 0.0)
    nxt = jnp.where(i < nsteps - 1, next_ref[...].astype(f32), 0.0)
    return jnp.concatenate([prev, cur_ref[...].astype(f32), nxt], axis=0)


def _conv_fwd(proj, cw, cb, tc=1024):
    S = proj.shape[0]
    tc = min(tc, S)
    nsteps = S // tc

    def body(cur_ref, prev_ref, next_ref, w_ref, b_ref, o_ref):
        ext = _extended(cur_ref, prev_ref, next_ref, pl.program_id(0), nsteps)
        acc = _rows_at(ext, -2, tc) * w_ref[0:1, :]
        for k in range(1, 4):
            acc = acc + _rows_at(ext, k - 2, tc) * w_ref[k:k + 1, :]
        o_ref[...] = acc + b_ref[...]

    return pl.pallas_call(
        body, name="conv_fwd", grid=(nsteps,),
        in_specs=[*_halo_specs(tc, S, D, 0),
                  pl.BlockSpec((4, D), lambda i: (0, 0)), pl.BlockSpec((1, D), lambda i: (0, 0))],
        out_specs=pl.BlockSpec((tc, D), lambda i: (i, 0)),
        out_shape=jax.ShapeDtypeStruct((S, D), f32),
        compiler_params=_cparams())(proj, proj, proj, cw, cb)


def _conv_bwd(duc_f, duc_b, proj, cw, tc=1024, comm=()):
    S = proj.shape[0]
    tc = min(tc, S)
    nsteps = S // tc

    def body(fc, fp, fn, bc, bp, bn, uc_, up, un, w_ref, du_ref, dw_ref, db_ref):
        i = pl.program_id(0)

        @pl.when(i == 0)
        def _():
            dw_ref[...] = jnp.zeros_like(dw_ref)
            db_ref[...] = jnp.zeros_like(db_ref)

        dext = _extended(fc, fp, fn, i, nsteps) + _extended(bc, bp, bn, i, nsteps)
        uext = _extended(uc_, up, un, i, nsteps)
        d = dext[HALO:HALO + tc]
        acc = _rows_at(dext, 2, tc) * w_ref[0:1, :]
        for k in range(1, 4):
            acc = acc + _rows_at(dext, 2 - k, tc) * w_ref[k:k + 1, :]
        du_ref[...] = acc.astype(bf16)
        wrow = lax.broadcasted_iota(jnp.int32, (4, D), 0)
        for k in range(4):
            dw_ref[...] += jnp.where(wrow == k, jnp.sum(d * _rows_at(uext, k - 2, tc), axis=0, keepdims=True), 0.0)
        db_ref[...] += jnp.sum(d, axis=0, keepdims=True)

    return _hosted_call(
        body, name="conv_bwd", grid=(nsteps,),
        in_specs=[*_halo_specs(tc, S, D, 0), *_halo_specs(tc, S, D, 0), *_halo_specs(tc, S, D, 0),
                  pl.BlockSpec((4, D), lambda i: (0, 0))],
        out_specs=[pl.BlockSpec((tc, D), lambda i: (i, 0)),
                   pl.BlockSpec((4, D), lambda i: (0, 0)), pl.BlockSpec((1, D), lambda i: (0, 0))],
        out_shape=[jax.ShapeDtypeStruct((S, D), bf16), jax.ShapeDtypeStruct((4, D), f32),
                   jax.ShapeDtypeStruct((1, D), f32)],
        args=(duc_f, duc_f, duc_f, duc_b, duc_b, duc_b, proj, proj, proj, cw), comm=comm)


def _scan_scratch(tc):
    halves = [pltpu.VMEM((tc, 128), f32) for _ in range(2 * (LRU_GW // 128))]
    return [*halves, pltpu.VMEM((tc // 8, LRU_GW), f32), pltpu.VMEM((tc // 8, LRU_GW), f32)]


def _log_scan(a, b, row, n, reverse, steps):
    for s in steps:
        shift = a.shape[0] - s if reverse else s
        keep = (row < n - s) if reverse else (row >= s)
        a_sh = pltpu.roll(a, shift, 0)
        b_sh = pltpu.roll(b, shift, 0)
        b = jnp.where(keep, a * b_sh + b, b)
        a = jnp.where(keep, a * a_sh, a)
    return a, b


def _scan_chunk(a, b, carry, reverse, *scratch):
    tc, w = a.shape
    ng = tc // 8
    nl = w // 128
    sa_refs, sb_refs, sc_ref, st_ref = scratch[:nl], scratch[nl:2 * nl], scratch[2 * nl], scratch[2 * nl + 1]
    sub = lax.broadcasted_iota(jnp.int32, (8, w), 0)
    ag, bg = [], []
    for k in range(ng):
        ak, bk = _log_scan(a[8 * k:8 * k + 8], b[8 * k:8 * k + 8], sub, 8, reverse, (1, 2, 4))
        ag.append(ak)
        bg.append(bk)
    a = jnp.concatenate(ag, axis=0)
    b = jnp.concatenate(bg, axis=0)
    edge = 0 if reverse else 7
    for i in range(nl):
        sa_refs[i][...] = a[:, 128 * i:128 * (i + 1)]
        sb_refs[i][...] = b[:, 128 * i:128 * (i + 1)]
    ta = jnp.concatenate([r[pl.ds(edge, ng, stride=8), :] for r in sa_refs], axis=1)
    tb = jnp.concatenate([r[pl.ds(edge, ng, stride=8), :] for r in sb_refs], axis=1)
    grow = lax.broadcasted_iota(jnp.int32, (ng, w), 0)
    ta, tb = _log_scan(ta, tb, grow, ng, reverse, [1 << i for i in range(ng.bit_length() - 1)])
    state = tb + ta * carry
    st_ref[...] = state
    if reverse:
        sc_ref[...] = jnp.where(grow == ng - 1, carry, pltpu.roll(state, ng - 1, 0))
    else:
        sc_ref[...] = jnp.where(grow == 0, carry, pltpu.roll(state, 1, 0))
    h = jnp.concatenate([bg[k] + ag[k] * sc_ref[k:k + 1, :] for k in range(ng)], axis=0)
    return h, (st_ref[0:1, :] if reverse else st_ref[ng - 1:ng, :])


def _lru_gates(uc, w, p_ref):
    pre = jnp.dot(uc.astype(bf16), w, preferred_element_type=f32)
    r = _sigmoid(pre[:, :LRU_GW] + p_ref[0, 1:2, :])
    gi = _sigmoid(pre[:, LRU_GW:] + p_ref[0, 2:3, :])
    sp = _softplus(-p_ref[0, 0:1, :])
    log_a = -RGLRU_C * r * sp
    a = jnp.exp(log_a)
    x = 2.0 * log_a
    series = -x * (1.0 + x * (0.5 + x * (1.0 / 6 + x * (1.0 / 24))))
    beta = jnp.sqrt(jnp.maximum(jnp.where(x > -0.0625, series, 1.0 - a * a), 0.0))
    return r, gi, sp, a, beta


def _lru_fwd(uc, wg, lp, reverse, comm=()):
    S = uc.shape[0]
    tc = LRU_CHUNK
    rows = min(LRU_ROWS, S)
    nsub = rows // tc
    nblk = S // rows
    d = 1 if reverse else 0

    def bidx(c):
        return nblk - 1 - c if reverse else c

    def body(uc_ref, w_ref, p_ref, h_ref, carry_ref, *scan_scratch):
        @pl.when(pl.program_id(1) == 0)
        def _():
            carry_ref[...] = jnp.zeros_like(carry_ref)

        carry = carry_ref[...]
        for j in (reversed(range(nsub)) if reverse else range(nsub)):
            sl = slice(j * tc, (j + 1) * tc)
            ucv = uc_ref[sl, :]
            _, gi, _, a, beta = _lru_gates(ucv, w_ref[0], p_ref)
            h, carry = _scan_chunk(a, beta * (gi * ucv), carry, reverse, *scan_scratch)
            h_ref[sl, :] = h.astype(bf16)
        carry_ref[...] = carry

    return _hosted_call(
        body, name="lru_fwd_rev" if reverse else "lru_fwd", grid=(LRU_GROUPS, nblk),
        in_specs=[pl.BlockSpec((rows, LRU_GW), lambda g, c: (bidx(c), g)),
                  pl.BlockSpec((1, LRU_GW, 2 * LRU_GW), lambda g, c: (g, 0, d)),
                  pl.BlockSpec((1, 8, LRU_GW), lambda g, c: (d, 0, g))],
        out_specs=[pl.BlockSpec((rows, LRU_GW), lambda g, c: (bidx(c), g))],
        out_shape=[jax.ShapeDtypeStruct((S, D), bf16)],
        scratch_shapes=[pltpu.VMEM((1, LRU_GW), f32), *_scan_scratch(tc)],
        args=(uc, wg, lp), comm=comm)


def _lru_bwd(uc, dh, h, wg, lp, reverse, comm=()):
    S = uc.shape[0]
    tc = LRU_CHUNK_BWD
    rows = min(LRU_ROWS, S)
    nsub = rows // tc
    nblk = S // rows
    d = 1 if reverse else 0
    per = rows // HALO
    last8 = S // HALO - 1

    def bidx(c):
        return c if reverse else nblk - 1 - c

    def halo_idx(c):
        if reverse:
            return jnp.minimum((bidx(c) + 1) * per, last8)
        return jnp.maximum(bidx(c) * per - 1, 0)

    def body(uc_ref, dh_ref, h_ref, halo_ref, w_ref, p_ref, duc_ref, dw_ref, dp_ref, carry_ref, tmp_ref,
             *scan_scratch):
        c = pl.program_id(1)
        bi = bidx(c)

        @pl.when(c == 0)
        def _():
            carry_ref[...] = jnp.zeros_like(carry_ref)
            dw_ref[...] = jnp.zeros_like(dw_ref)
            dp_ref[...] = jnp.zeros_like(dp_ref)

        row = lax.broadcasted_iota(jnp.int32, (tc, LRU_GW), 0)
        carry = carry_ref[...]
        dw = jnp.zeros((LRU_GW, 2 * LRU_GW), f32)
        dsp = jnp.zeros((1, LRU_GW), f32)
        dba = jnp.zeros((1, LRU_GW), f32)
        dbx = jnp.zeros((1, LRU_GW), f32)
        for j in (range(nsub) if reverse else reversed(range(nsub))):
            sl = slice(j * tc, (j + 1) * tc)
            ucv = uc_ref[sl, :]
            ucb = ucv.astype(bf16)
            r, gi, sp, a, beta = _lru_gates(ucv, w_ref[0], p_ref)
            hv = h_ref[sl, :].astype(f32)
            dhv = dh_ref[sl, :].astype(f32)
            if reverse:
                alpha = jnp.where(row == 0, 1.0, pltpu.roll(a, 1, 0))
                gsc, _ = _scan_chunk(alpha, dhv, carry, False, *scan_scratch)
                if j < nsub - 1:
                    edge = h_ref[(j + 1) * tc:(j + 1) * tc + HALO, :].astype(f32)[0:1, :]
                else:
                    edge = jnp.where(bi < nblk - 1, halo_ref[...].astype(f32)[0:1, :], 0.0)
                h_nb = jnp.where(row == tc - 1, edge, pltpu.roll(hv, tc - 1, 0))
            else:
                alpha = jnp.where(row == tc - 1, 1.0, pltpu.roll(a, tc - 1, 0))
                gsc, _ = _scan_chunk(alpha, dhv, carry, True, *scan_scratch)
                if j > 0:
                    edge = h_ref[j * tc - HALO:j * tc, :].astype(f32)[HALO - 1:HALO, :]
                else:
                    edge = jnp.where(bi > 0, halo_ref[...].astype(f32)[HALO - 1:HALO, :], 0.0)
                h_nb = jnp.where(row == 0, edge, pltpu.roll(hv, 1, 0))
            tmp_ref[...] = a * gsc
            carry = tmp_ref[tc - 1:tc, :] if reverse else tmp_ref[0:1, :]

            da = gsc * h_nb
            dbeta = gsc * (gi * ucv)
            dl = da * a - dbeta * (a * a) / beta
            dr = dl * (-RGLRU_C * sp)
            dsp = dsp + jnp.sum(dl * (-RGLRU_C * r), axis=0, keepdims=True)
            dgi = gsc * beta * ucv
            dpre_r = dr * r * (1.0 - r)
            dpre_i = dgi * gi * (1.0 - gi)
            dba = dba + jnp.sum(dpre_r, axis=0, keepdims=True)
            dbx = dbx + jnp.sum(dpre_i, axis=0, keepdims=True)
            dpre = jnp.concatenate([dpre_r, dpre_i], axis=1).astype(bf16)
            back = lax.dot_general(dpre, w_ref[0], (((1,), (1,)), ((), ())), preferred_element_type=f32)
            duc_ref[sl, :] = (gsc * beta * gi + back).astype(bf16)
            dw = dw + lax.dot_general(ucb, dpre, (((0,), (0,)), ((), ())), preferred_element_type=f32)
        carry_ref[...] = carry
        dw_ref[0] += dw
        dlam = -dsp / (1.0 + jnp.exp(p_ref[0, 0:1, :]))
        prow = lax.broadcasted_iota(jnp.int32, (8, LRU_GW), 0)
        dp_ref[...] += (jnp.where(prow == 0, dlam, 0.0) + jnp.where(prow == 1, dba, 0.0)
                        + jnp.where(prow == 2, dbx, 0.0))

    chunk = pl.BlockSpec((rows, LRU_GW), lambda g, c: (bidx(c), g))
    return _hosted_call(
        body, name="lru_bwd_rev" if reverse else "lru_bwd", grid=(LRU_GROUPS, nblk),
        in_specs=[chunk, chunk, chunk,
                  pl.BlockSpec((HALO, LRU_GW), lambda g, c: (halo_idx(c), g)),
                  pl.BlockSpec((1, LRU_GW, 2 * LRU_GW), lambda g, c: (g, 0, d)),
                  pl.BlockSpec((1, 8, LRU_GW), lambda g, c: (d, 0, g))],
        out_specs=[chunk,
                   pl.BlockSpec((1, LRU_GW, 2 * LRU_GW), lambda g, c: (g, 0, 0)),
                   pl.BlockSpec((8, LRU_GW), lambda g, c: (0, g))],
        out_shape=[jax.ShapeDtypeStruct((S, D), bf16),
                   jax.ShapeDtypeStruct((LRU_GROUPS, LRU_GW, 2 * LRU_GW), f32),
                   jax.ShapeDtypeStruct((8, D), f32)],
        scratch_shapes=[pltpu.VMEM((1, LRU_GW), f32), pltpu.VMEM((tc, LRU_GW), f32), *_scan_scratch(tc)],
        args=(uc, dh, h, h, wg, lp), comm=comm)


def _slope(h):
    return 2.0 ** (-8.0 * (h + 1.0) / N_HEADS)


ATT_QB = 4


def _kv_specs(nb, col):
    return [pl.BlockSpec((BLK, N_KV * HEAD_DIM), lambda n: (jnp.maximum(ATT_QB * n - 1, 0), col)),
            pl.BlockSpec((ATT_QB * BLK, N_KV * HEAD_DIM), lambda n: (n, col)),
            pl.BlockSpec((BLK, N_KV * HEAD_DIM), lambda n: (jnp.minimum(ATT_QB * (n + 1), nb - 1), col))]


def _key_blocks(prev_ref, cur_ref, next_ref):
    return [prev_ref[...], *[cur_ref[BLK * s:BLK * (s + 1), :] for s in range(ATT_QB)], next_ref[...]]


def _dup_windows(r0, r1, r2):
    left = lax.broadcasted_iota(jnp.int32, (3 * BLK, 128), 1) < HEAD_DIM
    win = jnp.concatenate([r0, r1, r2], axis=0)
    out = []
    for i in range(N_KV // 2):
        t = win[:, i * 128:(i + 1) * 128]
        r = pltpu.roll(t, HEAD_DIM, 1)
        out += [jnp.where(left, t, r).astype(bf16), jnp.where(left, r, t).astype(bf16)]
    return out


def _attn_bias_init(bias_ref):
    k_loc = lax.broadcasted_iota(jnp.int32, (3 * BLK, BLK), 0)
    q_loc = lax.broadcasted_iota(jnp.int32, (3 * BLK, BLK), 1)
    adist = jnp.abs(q_loc + BLK - k_loc)
    adf = adist.astype(f32)
    for e in range(3):
        ok = adist <= WINDOW
        if e == 0:
            ok = ok & (k_loc >= BLK)
        if e == 2:
            ok = ok & (k_loc < 2 * BLK)
        for kv in range(N_KV):
            bias_ref[e, kv] = jnp.concatenate(
                [jnp.where(ok, (-_slope(4 * kv + j)) * adf, NEG_INF) for j in range(4)], axis=1)


def _stack_heads(ref, sub, kv, scale):
    left = lax.broadcasted_iota(jnp.int32, (BLK, 128), 1) < HEAD_DIM
    rows = []
    for pp in range(2):
        t = ref[BLK * sub:BLK * (sub + 1), (2 * kv + pp) * 128:(2 * kv + pp + 1) * 128]
        if scale != 1.0:
            t = t * scale
        zero = jnp.zeros_like(t)
        rows += [jnp.where(left, t, zero).astype(bf16), jnp.where(left, zero, t).astype(bf16)]
    return jnp.concatenate(rows, axis=0)


def _attn_softmax(qs, k2, bias, sink_ref, kv, stats=None):
    sink = jnp.concatenate([jnp.full((1, BLK), sink_ref[0, 4 * kv + j], f32) for j in range(4)], axis=1)
    s = lax.dot_general(k2, qs, (((1,), (1,)), ((), ())), preferred_element_type=f32) + bias
    m = jnp.maximum(jnp.max(s, axis=0, keepdims=True), sink) if stats is None else stats[0]
    p = jnp.exp(s - m)
    ps = jnp.exp(sink - m)
    inv = 1.0 / (jnp.sum(p, axis=0, keepdims=True) + ps) if stats is None else stats[1]
    return p, ps, m, inv


def _pair_tiles(t):
    return [jnp.concatenate([t[:HEAD_DIM, 256 * pp:256 * pp + 128],
                             t[HEAD_DIM:, 256 * pp + 128:256 * pp + 256]], axis=0).T for pp in range(2)]


def _attn_fwd(proj, sink, comm=()):
    S = proj.shape[0]
    nb = S // BLK
    assert nb >= 2 and nb % ATT_QB == 0

    def body(q_ref, k0, k1, k2_, v0, v1, v2_, sink_ref, o_ref, st_ref, bias_ref):
        n = pl.program_id(0)

        @pl.when(n == 0)
        def _():
            _attn_bias_init(bias_ref)

        kb = _key_blocks(k0, k1, k2_)
        vb = _key_blocks(v0, v1, v2_)
        for sub in range(ATT_QB):
            blk = ATT_QB * n + sub
            e = jnp.where(blk == 0, 0, jnp.where(blk == nb - 1, 2, 1))
            kk = _dup_windows(*kb[sub:sub + 3])
            vv = _dup_windows(*vb[sub:sub + 3])
            tiles = []
            for kv in range(N_KV):
                qs = _stack_heads(q_ref, sub, kv, HEAD_DIM ** -0.5)
                p, _, m, inv = _attn_softmax(qs, kk[kv], bias_ref[e, kv], sink_ref, kv)
                st_ref[sub, kv:kv + 1, :] = m
                st_ref[sub, N_KV + kv:N_KV + kv + 1, :] = inv
                ot = lax.dot_general(vv[kv], p.astype(bf16), (((0,), (0,)), ((), ())), preferred_element_type=f32)
                tiles += _pair_tiles(ot * inv)
            o_ref[BLK * sub:BLK * (sub + 1), :] = jnp.concatenate(tiles, axis=1).astype(bf16)

    return _hosted_call(
        body, name="attn_fwd", grid=(nb // ATT_QB,),
        in_specs=[pl.BlockSpec((ATT_QB * BLK, D), lambda n: (n, C_Q // D)),
                  *_kv_specs(nb, C_K // (N_KV * HEAD_DIM)), *_kv_specs(nb, C_V // (N_KV * HEAD_DIM)),
                  pl.BlockSpec(memory_space=pltpu.SMEM)],
        out_specs=[pl.BlockSpec((ATT_QB * BLK, D), lambda n: (n, 0)),
                   pl.BlockSpec((ATT_QB, 2 * N_KV, 4 * BLK), lambda n: (n, 0, 0))],
        out_shape=[jax.ShapeDtypeStruct((S, D), bf16), jax.ShapeDtypeStruct((nb, 2 * N_KV, 4 * BLK), f32)],
        scratch_shapes=[pltpu.VMEM((3, N_KV, 3 * BLK, 4 * BLK), f32)],
        args=(proj, proj, proj, proj, proj, proj, proj, sink), comm=comm)


def _attn_bwd(proj, sink, dyb, stats, comm=()):
    S = proj.shape[0]
    nb = S // BLK
    assert nb >= 2 and nb % ATT_QB == 0
    nsteps = nb // ATT_QB
    kvw = N_KV * HEAD_DIM

    def body(q_ref, k0, k1, k2_, v0, v1, v2_, sink_ref, do_ref, st_ref, dq_ref, dkv_out, ds_ref,
             bias_ref, dk_ref, dv_ref, dsk_ref, dkv_ref):
        n = pl.program_id(0)

        @pl.when(n == 0)
        def _():
            _attn_bias_init(bias_ref)
            dk_ref[...] = jnp.zeros_like(dk_ref)
            dv_ref[...] = jnp.zeros_like(dv_ref)
            dsk_ref[...] = jnp.zeros_like(dsk_ref)

        kb = _key_blocks(k0, k1, k2_)
        vb = _key_blocks(v0, v1, v2_)
        left3 = lax.broadcasted_iota(jnp.int32, (3 * BLK, 128), 1) < HEAD_DIM
        for sub in range(ATT_QB):
            blk = ATT_QB * n + sub
            e = jnp.where(blk == 0, 0, jnp.where(blk == nb - 1, 2, 1))
            kk = _dup_windows(*kb[sub:sub + 3])
            vv = _dup_windows(*vb[sub:sub + 3])
            start = pl.multiple_of(blk * BLK, BLK)
            dq_tiles, dks, dvs = [], [], []
            for kv in range(N_KV):
                qs = _stack_heads(q_ref, sub, kv, HEAD_DIM ** -0.5)
                dos = _stack_heads(do_ref, sub, kv, 1.0)
                nt = (((1,), (1,)), ((), ()))
                dk = dv = None
                for hp in range(2):
                    cs = slice(2 * BLK * hp, 2 * BLK * (hp + 1))
                    qh, doh = qs[cs], dos[cs]
                    sink = jnp.concatenate(
                        [jnp.full((1, BLK), sink_ref[0, 4 * kv + 2 * hp + j], f32) for j in range(2)], axis=1)
                    m = st_ref[sub, kv:kv + 1, cs]
                    inv = st_ref[sub, N_KV + kv:N_KV + kv + 1, cs]
                    s = lax.dot_general(kk[kv], qh, nt, preferred_element_type=f32) + bias_ref[e, kv, :, cs]
                    pn = jnp.exp(s - m) * inv
                    dp = lax.dot_general(vv[kv], doh, nt, preferred_element_type=f32)
                    delta = jnp.sum(pn * dp, axis=0, keepdims=True)
                    dsc = (pn * (dp - delta)).astype(bf16)
                    dsk_ref[kv:kv + 1, cs] += delta * (jnp.exp(sink - m) * inv)
                    dqt = lax.dot_general(kk[kv], dsc, (((0,), (0,)), ((), ())), preferred_element_type=f32)
                    dqt = dqt * (HEAD_DIM ** -0.5)
                    dq_tiles.append(jnp.concatenate([dqt[:HEAD_DIM, :BLK], dqt[HEAD_DIM:, BLK:]], axis=0).T)
                    dkh = jnp.dot(dsc, qh, preferred_element_type=f32)
                    dvh = jnp.dot(pn.astype(bf16), doh, preferred_element_type=f32)
                    dk = dkh if dk is None else dk + dkh
                    dv = dvh if dv is None else dv + dvh
                dks.append(dk + pltpu.roll(dk, HEAD_DIM, 1))
                dvs.append(dv + pltpu.roll(dv, HEAD_DIM, 1))
            for jp in range(N_KV // 2):
                cols = slice(jp * 128, (jp + 1) * 128)
                dk_ref[pl.ds(start, 3 * BLK), cols] += jnp.where(left3, dks[2 * jp], dks[2 * jp + 1])
                dv_ref[pl.ds(start, 3 * BLK), cols] += jnp.where(left3, dvs[2 * jp], dvs[2 * jp + 1])
            dq_ref[BLK * sub:BLK * (sub + 1), :] = jnp.concatenate(dq_tiles, axis=1).astype(bf16)

        @pl.when(n == nsteps - 1)
        def _():
            rows = min(S, 512)
            for c in range(S // rows):
                dkv_ref[rows * c:rows * (c + 1), :kvw] = dk_ref[BLK + rows * c:BLK + rows * (c + 1), :].astype(bf16)
                dkv_ref[rows * c:rows * (c + 1), kvw:] = dv_ref[BLK + rows * c:BLK + rows * (c + 1), :].astype(bf16)
            pltpu.sync_copy(dkv_ref, dkv_out)
            lane = lax.broadcasted_iota(jnp.int32, (1, 128), 1)
            dsink = jnp.zeros((1, 128), f32)
            for h in range(N_HEADS):
                part = dsk_ref[h // 4:h // 4 + 1, (h % 4) * BLK:(h % 4 + 1) * BLK]
                dsink = dsink + jnp.where(lane == h, -jnp.sum(part), 0.0)
            ds_ref[...] = dsink

    acc = jax.ShapeDtypeStruct((S + 2 * BLK, N_KV * HEAD_DIM), f32)
    return _hosted_call(
        body, name="attn_bwd", grid=(nsteps,),
        in_specs=[pl.BlockSpec((ATT_QB * BLK, D), lambda n: (n, C_Q // D)),
                  *_kv_specs(nb, C_K // (N_KV * HEAD_DIM)), *_kv_specs(nb, C_V // (N_KV * HEAD_DIM)),
                  pl.BlockSpec(memory_space=pltpu.SMEM),
                  pl.BlockSpec((ATT_QB * BLK, D), lambda n: (n, 0)),
                  pl.BlockSpec((ATT_QB, 2 * N_KV, 4 * BLK), lambda n: (n, 0, 0))],
        out_specs=[pl.BlockSpec((ATT_QB * BLK, D), lambda n: (n, 0)), ANY_SPEC,
                   pl.BlockSpec((1, 128), lambda n: (0, 0))],
        out_shape=[jax.ShapeDtypeStruct((S, D), bf16), jax.ShapeDtypeStruct((S, 2 * kvw), bf16),
                   jax.ShapeDtypeStruct((1, 128), f32)],
        scratch_shapes=[pltpu.VMEM((3, N_KV, 3 * BLK, 4 * BLK), f32), pltpu.VMEM(acc.shape, f32),
                        pltpu.VMEM(acc.shape, f32), pltpu.VMEM((8, 4 * BLK), f32), pltpu.VMEM((S, 2 * kvw), bf16)],
        args=(proj, proj, proj, proj, proj, proj, proj, sink, dyb, stats), comm=comm)


def _merge_parts(hf, hb, g, z0, z1, yb, bg):
    g0 = _sigmoid(z0 + bg[:, :D].astype(bf16))
    g1 = _sigmoid(z1 + bg[:, D:].astype(bf16))
    gelu, dgelu = _gelu_and_grad(g)
    hs = hf + hb
    ya = hs * gelu
    return g0, g1, gelu, dgelu, hs, ya


def _merge_outproj(x, hf, hb, proj, yb, bg, w_out, tm=1024):
    S = x.shape[0]
    tm = min(tm, S)

    def body(x_ref, hf_ref, hb_ref, g_ref, z0_ref, z1_ref, yb_ref, bg_ref, w_ref, mg_ref, x1_ref):
        ybv = yb_ref[...]
        g0, g1, _, _, _, ya = _merge_parts(hf_ref[...], hb_ref[...], g_ref[...], z0_ref[...], z1_ref[...],
                                           ybv, bg_ref[...])
        mg = g0 * ya + g1 * ybv
        mg_ref[...] = mg
        x1_ref[...] = x_ref[...] + jnp.dot(mg, w_ref[...], preferred_element_type=f32)

    row = pl.BlockSpec((tm, D), lambda i: (i, 0))
    return pl.pallas_call(
        body, name="merge_outproj", grid=(S // tm,),
        in_specs=[row, row, row,
                  pl.BlockSpec((tm, D), lambda i: (i, C_G // D)),
                  pl.BlockSpec((tm, D), lambda i: (i, C_Z0 // D)),
                  pl.BlockSpec((tm, D), lambda i: (i, C_Z1 // D)),
                  row, pl.BlockSpec((1, 2 * D), lambda i: (0, 0)), pl.BlockSpec((D, D), lambda i: (0, 0))],
        out_specs=[row, row],
        out_shape=[jax.ShapeDtypeStruct((S, D), bf16), jax.ShapeDtypeStruct((S, D), f32)],
        compiler_params=_cparams())(x, hf, hb, proj, proj, proj, yb, bg, w_out)


def _ffn_out_loss(gu, x1, w_fo, g3, tgt, tm=256):
    S = x1.shape[0]
    tm = min(tm, S)

    def body(gt_ref, up_ref, x1_ref, w_ref, g_ref, t_ref, ff_ref, dx_ref, dxb_ref, loss_ref, dg_ref,
             dgt_ref, dup_ref):
        @pl.when(pl.program_id(0) == 0)
        def _():
            loss_ref[...] = jnp.zeros_like(loss_ref)
            dg_ref[...] = jnp.zeros_like(dg_ref)

        gt = gt_ref[...]
        up = up_ref[...]
        sg = _sigmoid(gt)
        silu = gt * sg
        ff = silu * up
        ff_ref[...] = ff
        x2 = x1_ref[...] + jnp.dot(ff, w_ref[...], preferred_element_type=f32)
        gv = g_ref[...]
        r = lax.rsqrt(jnp.mean(x2 * x2, axis=-1, keepdims=True) + EPS)
        xh = x2 * r
        diff = xh * gv - t_ref[...]
        loss_ref[...] += (0.5 / D) * jnp.sum(diff * diff)
        dy = diff * (1.0 / D)
        dg_ref[...] += jnp.sum(dy * xh, axis=0, keepdims=True)
        dxh = dy * gv
        dx = r * (dxh - xh * jnp.mean(dxh * xh, axis=-1, keepdims=True))
        dx_ref[...] = dx
        dxb = dx.astype(bf16)
        dxb_ref[...] = dxb
        dff = lax.dot_general(dxb, w_ref[...], (((1,), (1,)), ((), ())), preferred_element_type=f32)
        dup_ref[...] = (dff * silu.astype(f32)).astype(bf16)
        dgt_ref[...] = (dff * (up * (sg * (1.0 + gt * (1.0 - sg)))).astype(f32)).astype(bf16)

    row = pl.BlockSpec((tm, D), lambda i: (i, 0))
    vec = pl.BlockSpec((1, D), lambda i: (0, 0))
    wide = pl.BlockSpec((tm, D_FF), lambda i: (i, 0))
    wide_shape = jax.ShapeDtypeStruct((S, D_FF), bf16)
    return pl.pallas_call(
        body, name="ffn_out_loss", grid=(S // tm,),
        in_specs=[wide, pl.BlockSpec((tm, D_FF), lambda i: (i, 1)),
                  row, pl.BlockSpec((D_FF, D), lambda i: (0, 0)), vec, row],
        out_specs=[wide, row, row, pl.BlockSpec((1, 128), lambda i: (0, 0)), vec, wide, wide],
        out_shape=[wide_shape, jax.ShapeDtypeStruct((S, D), f32), jax.ShapeDtypeStruct((S, D), bf16),
                   jax.ShapeDtypeStruct((1, 128), f32), jax.ShapeDtypeStruct((1, D), f32), wide_shape, wide_shape],
        compiler_params=_cparams())(gu, gu, x1, w_fo, g3, tgt)


def _proj_bwd(pieces, wt, xres, g, dres, name, tm=512, comm=()):
    S = xres.shape[0]
    tm = min(tm, S)
    np_ = len(pieces)

    def body(*refs):
        p_refs = refs[:np_]
        w_refs = refs[np_:2 * np_]
        x_ref, g_ref, dres_ref, dx_ref, dxb_ref, dg_ref = refs[2 * np_:]

        @pl.when(pl.program_id(0) == 0)
        def _():
            dg_ref[...] = jnp.zeros_like(dg_ref)

        dn = jnp.dot(p_refs[0][...], w_refs[0][...], preferred_element_type=f32)
        for pr, wr in zip(p_refs[1:], w_refs[1:]):
            dn = dn + jnp.dot(pr[...], wr[...], preferred_element_type=f32)
        dxn, dgc = _rms_bwd(dn, x_ref[...], g_ref[...])
        dx = dres_ref[...] + dxn
        dx_ref[...] = dx
        dxb_ref[...] = dx.astype(bf16)
        dg_ref[...] += jnp.sum(dgc, axis=0, keepdims=True)

    row = pl.BlockSpec((tm, D), lambda i: (i, 0))
    vec = pl.BlockSpec((1, D), lambda i: (0, 0))
    return _hosted_call(
        body, name=name, grid=(S // tm,),
        in_specs=[*[pl.BlockSpec((tm, wd), functools.partial(lambda i, cb: (i, cb), cb=acb))
                    for _, acb, _, wd in pieces],
                  *[pl.BlockSpec((wd, D), functools.partial(lambda i, rb: (rb, 0), rb=wrb))
                    for _, _, wrb, wd in pieces],
                  row, vec, row],
        out_specs=[row, row, vec],
        out_shape=[jax.ShapeDtypeStruct((S, D), f32), jax.ShapeDtypeStruct((S, D), bf16),
                   jax.ShapeDtypeStruct((1, D), f32)],
        args=(*[p[0] for p in pieces], *[wt] * np_, xres, g, dres), comm=comm)


def _outproj_bwd(dx1b, w_out, hf, hb, proj, yb, bg, tm=1024):
    S = dx1b.shape[0]
    tm = min(tm, S)

    def body(dx_ref, w_ref, hf_ref, hb_ref, g_ref, z0_ref, z1_ref, yb_ref, bg_ref,
             dh_ref, dg_ref, dz_ref, dyb_ref, dbg_ref):
        @pl.when(pl.program_id(0) == 0)
        def _():
            dbg_ref[...] = jnp.zeros_like(dbg_ref)

        dm = lax.dot_general(dx_ref[...], w_ref[...], (((1,), (1,)), ((), ())), preferred_element_type=f32)
        ybv = yb_ref[...]
        g0, g1, gelu, dgelu, hs, ya = _merge_parts(hf_ref[...], hb_ref[...], g_ref[...], z0_ref[...],
                                                   z1_ref[...], ybv, bg_ref[...])
        dh_ref[...] = (dm * (g0 * gelu).astype(f32)).astype(bf16)
        dg_ref[...] = (dm * (g0 * hs * dgelu).astype(f32)).astype(bf16)
        dyb_ref[...] = (dm * g1.astype(f32)).astype(bf16)
        dz0 = dm * (ya * (g0 * (1.0 - g0))).astype(f32)
        dz1 = dm * (ybv * (g1 * (1.0 - g1))).astype(f32)
        dz = jnp.concatenate([dz0, dz1], axis=1)
        dz_ref[...] = dz.astype(bf16)
        dbg_ref[...] += jnp.sum(dz, axis=0, keepdims=True)

    row = pl.BlockSpec((tm, D), lambda i: (i, 0))
    return pl.pallas_call(
        body, name="outproj_bwd", grid=(S // tm,),
        in_specs=[row, pl.BlockSpec((D, D), lambda i: (0, 0)), row, row,
                  pl.BlockSpec((tm, D), lambda i: (i, C_G // D)),
                  pl.BlockSpec((tm, D), lambda i: (i, C_Z0 // D)),
                  pl.BlockSpec((tm, D), lambda i: (i, C_Z1 // D)),
                  row, pl.BlockSpec((1, 2 * D), lambda i: (0, 0))],
        out_specs=[row, row, pl.BlockSpec((tm, 2 * D), lambda i: (i, 0)), row,
                   pl.BlockSpec((1, 2 * D), lambda i: (0, 0))],
        out_shape=[jax.ShapeDtypeStruct((S, D), bf16), jax.ShapeDtypeStruct((S, D), bf16),
                   jax.ShapeDtypeStruct((S, 2 * D), bf16), jax.ShapeDtypeStruct((S, D), bf16),
                   jax.ShapeDtypeStruct((1, 2 * D), f32)],
        compiler_params=_cparams())(dx1b, w_out, hf, hb, proj, proj, proj, yb, bg)


def _block_diag_groups(w):
    w4 = w.reshape(LRU_GROUPS, 4, LRU_BLOCK, LRU_BLOCK)
    eye = jnp.eye(4, dtype=w.dtype)
    return jnp.einsum("ghij,hk->ghikj", w4, eye).reshape(LRU_GROUPS, LRU_GW, LRU_GW)


def _diag_blocks(dw):
    d5 = dw.reshape(LRU_GROUPS, 4, LRU_BLOCK, 4, LRU_BLOCK)
    return jnp.stack([d5[:, h, :, h, :] for h in range(4)], axis=1).reshape(LRU_HEADS, LRU_BLOCK, LRU_BLOCK)


def _local_step(x, tgt, small, env, before=lambda name: (), after=lambda name, got: None):
    S = x.shape[0]
    g1, g2, g3 = small["norm_mix_g"], small["norm_ffn_g"], small["norm_final_g"]
    bg, cb, sink = small["b_gate"], small["conv_b"], small["attn_sink"]

    def hosted(name, fn, *args, **kw):
        outs, got = fn(*args, comm=tuple(before(name)), **kw)
        after(name, got)
        return outs

    (xn,) = hosted("norm_x", _rmsnorm_bf16, x, g1, "norm_x")
    cw = small["conv_w"]
    wg = jnp.concatenate([_block_diag_groups(small["lru_wa"][0]), _block_diag_groups(small["lru_wx"][0]),
                          _block_diag_groups(small["lru_wa"][1]), _block_diag_groups(small["lru_wx"][1])],
                         axis=2).astype(bf16)
    zeros5 = jnp.zeros((5, D), f32)
    lp = jnp.stack([jnp.concatenate([small["lru_lambda"][d:d + 1], small["lru_ba"][d:d + 1],
                                     small["lru_bx"][d:d + 1], zeros5], axis=0) for d in range(2)])
    (proj,) = hosted("inproj", _matmul_t, xn, env["w_in_t"], "inproj", tm=4096, tn=512,
                     row_block=lambda j: jnp.where(j < 6, j, jnp.where(j < 10, j + 1, 6)))
    uc = _conv_fwd(proj, cw, cb)
    (hf,), _ = _lru_fwd(uc, wg, lp, False)
    (hb,), _ = _lru_fwd(uc, wg, lp, True)
    yb, attn_stats = hosted("attn_fwd", _attn_fwd, proj, sink)
    merged, x1 = _merge_outproj(x, hf, hb, proj, yb, bg, env["w_out"])
    (xn2, gu), _ = _norm_matmul(x1, g2, env["w_fi_t"], "norm_ffn_in", tn=D_FF)
    ff, dx2, dx2b, loss, dg3, dgt, dup = _ffn_out_loss(gu, x1, env["w_fo"], g3, tgt)

    env["dw_fo"] = _mm_tn(ff, dx2b, "dw_ffn_out", tk=1408, tn=1024)
    dx1, dx1b, dg2 = hosted("ffn_in_bwd", _proj_bwd, [(dgt, 0, 0, D_FF), (dup, 0, 1, D_FF)], env["w_fi_t"],
                            x1, g2, dx2, "ffn_in_bwd")
    dw_gate = _mm_tn(dgt, xn2, "dw_ffn_in_gate", tk=1408, tn=1024, out_rows=2 * D_FF)
    env["dw_fi_t"] = _mm_tn(dup, xn2, "dw_ffn_in_up", tk=1408, tn=1024, into=dw_gate, row=D_FF // 1408)
    env["dw_out"] = _mm_tn(merged, dx1b, "dw_out", tk=1024, tn=1024)
    dh, dgl, dz, dyb, dbg = _outproj_bwd(dx1b, env["w_out"], hf, hb, proj, yb, bg)
    dq, dkv, dsink = hosted("attn_bwd", _attn_bwd, proj, sink, dyb, attn_stats)
    duc_f, dwg_f, dp_f = hosted("lru_bwd", _lru_bwd, uc, dh, hf, wg, lp, False)
    (duc_b, dwg_b, dp_b), _ = _lru_bwd(uc, dh, hb, wg, lp, True)
    env["grads_early"] = {
        "loss": loss[:, :1], "b_gate": dbg,
        "lru_lambda": jnp.concatenate([dp_f[0:1], dp_b[0:1]], axis=0),
        "lru_wa": jnp.stack([_diag_blocks(dwg_f[:, :, :LRU_GW]), _diag_blocks(dwg_b[:, :, :LRU_GW])]),
        "lru_ba": jnp.concatenate([dp_f[1:2], dp_b[1:2]], axis=0),
        "lru_wx": jnp.stack([_diag_blocks(dwg_f[:, :, LRU_GW:]), _diag_blocks(dwg_b[:, :, LRU_GW:])]),
        "lru_bx": jnp.concatenate([dp_f[2:3], dp_b[2:3]], axis=0),
        "attn_sink": dsink[:, :N_HEADS], "norm_ffn_g": dg2, "norm_final_g": dg3,
    }
    du, dcw, dcb = hosted("conv_bwd", _conv_bwd, duc_f, duc_b, proj, cw)
    dw_in = _mm_tn(du, xn, "dw_in_u", tk=1024, tn=1024, out_rows=IN_W)
    dw_in = _mm_tn(dgl, xn, "dw_in_g", tk=1024, tn=1024, into=dw_in, row=1)
    dw_in = _mm_tn(dq, xn, "dw_in_q", tk=1024, tn=1024, into=dw_in, row=2)
    dw_in = _mm_tn(dkv, xn, "dw_in_kv", tk=512, tn=1024, into=dw_in, row=3072 // 512)
    env["dw_in_t"] = _mm_tn(dz, xn, "dw_in_z", tk=512, tn=1024, tmc=4096, into=dw_in, row=3584 // 512)
    col_pieces = [(du, 0, 0, D), (dgl, 0, 1, D), (dq, 0, 2, D), (dkv, 0, 3072 // 512, 512),
                  *[(dz, i, 3584 // 512 + i, 512) for i in range(4)]]
    dx, _, dg1 = hosted("inproj_bwd", _proj_bwd, col_pieces, env["w_in_t"], x, g1, dx1, "inproj_bwd")

    grads = dict(env["grads_early"], norm_mix_g=dg1, conv_w=dcw, conv_b=dcb)
    return dx, grads


def _adamw(gparts, w, m, v, name, tr=256):
    n, rows, cols = gparts.shape
    tr = _div_tile(rows, tr)
    c1 = 1.0 - ADAM_B1 ** ADAM_STEP
    c2 = 1.0 - ADAM_B2 ** ADAM_STEP

    def body(g_ref, w_ref, m_ref, v_ref, go_ref, d_ref, mo_ref, vo_ref):
        g = g_ref[0].astype(f32)
        for j in range(1, n):
            g = g + g_ref[j].astype(f32)
        mn = ADAM_B1 * m_ref[0] + (1.0 - ADAM_B1) * g
        vn = ADAM_B2 * v_ref[0] + (1.0 - ADAM_B2) * (g * g)
        m_hat = mn / c1
        v_hat = vn / c2
        go_ref[0] = g
        d_ref[0] = -ADAM_LR * (m_hat / (jnp.sqrt(v_hat) + ADAM_EPS) + ADAM_WD * w_ref[0])
        mo_ref[0] = mn
        vo_ref[0] = vn

    blk = pl.BlockSpec((1, tr, cols), lambda i: (0, i, 0))
    shp = jax.ShapeDtypeStruct((1, rows, cols), f32)
    return pl.pallas_call(
        body, name=name, grid=(rows // tr,),
        in_specs=[pl.BlockSpec((n, tr, cols), lambda i: (0, i, 0)), blk, blk, blk],
        out_specs=[blk, blk, blk, blk], out_shape=[shp, shp, shp, shp],
        compiler_params=_cparams())(gparts, w, m, v)


def _sum_parts(parts, name):
    n, rows, cols = parts.shape

    def body(p_ref, o_ref):
        acc = p_ref[0].astype(f32)
        for j in range(1, n):
            acc = acc + p_ref[j].astype(f32)
        o_ref[...] = acc

    return pl.pallas_call(
        body, name=name, out_shape=jax.ShapeDtypeStruct((rows, cols), f32),
        compiler_params=_cparams())(parts)


def _pack_rows(arrs, dtype=f32):
    rows, spans, at = [], [], 0
    for a in arrs:
        flat = a.reshape(-1).astype(dtype)
        nr = -(-flat.shape[0] // 1024)
        rows.append(jnp.pad(flat, (0, nr * 1024 - flat.shape[0])).reshape(nr, 1024))
        spans.append((at, nr))
        at += nr
    pad = (-at) % 16
    if pad:
        rows.append(jnp.zeros((pad, 1024), dtype))
    return jnp.concatenate(rows, axis=0), spans


def _unpack_rows(packed, spans, shapes):
    out = []
    for (at, nr), shp in zip(spans, shapes):
        n = math.prod(shp)
        out.append(packed[at:at + nr].reshape(-1)[:n].reshape(shp))
    return out


BIG = ("w_in", "w_out", "w_ffn_in", "w_ffn_out")
SMALL_REPL = ("norm_mix_g", "b_gate", "conv_b", "attn_sink", "norm_ffn_g", "norm_final_g")
GATE_W = ("lru_wa", "lru_wx")
SMALL_SHARD = ("conv_w", "lru_lambda", "lru_ba", "lru_bx")
ORDER = ("norm_mix_g", "w_in", "b_gate", "conv_w", "conv_b", "lru_lambda", "lru_wa", "lru_ba", "lru_wx",
         "lru_bx", "attn_sink", "w_out", "norm_ffn_g", "w_ffn_in", "w_ffn_out", "norm_final_g")
EARLY_F32 = ("loss", "b_gate", "lru_lambda", "lru_ba", "lru_bx", "attn_sink", "norm_ffn_g", "norm_final_g")
LATE = ("norm_mix_g", "conv_w", "conv_b")


def kernel(x, norm_mix_g, w_in, b_gate, conv_w, conv_b, lru_lambda, lru_wa, lru_ba, lru_wx, lru_bx, attn_sink, w_out, norm_ffn_g, w_ffn_in, w_ffn_out, norm_final_g, loss_target, m_norm_mix_g, m_w_in, m_b_gate, m_conv_w, m_conv_b, m_lru_lambda, m_lru_wa, m_lru_ba, m_lru_wx, m_lru_bx, m_attn_sink, m_w_out, m_norm_ffn_g, m_w_ffn_in, m_w_ffn_out, m_norm_final_g, v_norm_mix_g, v_w_in, v_b_gate, v_conv_w, v_conv_b, v_lru_lambda, v_lru_wa, v_lru_ba, v_lru_wx, v_lru_bx, v_attn_sink, v_w_out, v_norm_ffn_g, v_w_ffn_in, v_w_ffn_out, v_norm_final_g):
    w = dict(norm_mix_g=norm_mix_g, w_in=w_in, b_gate=b_gate, conv_w=conv_w, conv_b=conv_b, lru_lambda=lru_lambda,
             lru_wa=lru_wa, lru_ba=lru_ba, lru_wx=lru_wx, lru_bx=lru_bx, attn_sink=attn_sink, w_out=w_out,
             norm_ffn_g=norm_ffn_g, w_ffn_in=w_ffn_in, w_ffn_out=w_ffn_out, norm_final_g=norm_final_g)
    m = dict(norm_mix_g=m_norm_mix_g, w_in=m_w_in, b_gate=m_b_gate, conv_w=m_conv_w, conv_b=m_conv_b,
             lru_lambda=m_lru_lambda, lru_wa=m_lru_wa, lru_ba=m_lru_ba, lru_wx=m_lru_wx, lru_bx=m_lru_bx,
             attn_sink=m_attn_sink, w_out=m_w_out, norm_ffn_g=m_norm_ffn_g, w_ffn_in=m_w_ffn_in,
             w_ffn_out=m_w_ffn_out, norm_final_g=m_norm_final_g)
    v = dict(norm_mix_g=v_norm_mix_g, w_in=v_w_in, b_gate=v_b_gate, conv_w=v_conv_w, conv_b=v_conv_b,
             lru_lambda=v_lru_lambda, lru_wa=v_lru_wa, lru_ba=v_lru_ba, lru_wx=v_lru_wx, lru_bx=v_lru_bx,
             attn_sink=v_attn_sink, w_out=v_w_out, norm_ffn_g=v_norm_ffn_g, w_ffn_in=v_w_ffn_in,
             w_ffn_out=v_w_ffn_out, norm_final_g=v_norm_final_g)
    me = 4 * lax.axis_index("x") + 2 * lax.axis_index("y") + lax.axis_index("c")

    def shard_t(a):
        return jnp.swapaxes(a[0], 0, 1)

    def rows_parts(g):
        return g.reshape(N_DEV, -1, g.shape[1])

    shard_rows = jnp.concatenate([w[n][0] for n in SMALL_SHARD], axis=0)
    small = {n: w[n] for n in ("norm_mix_g", "b_gate", "conv_b", "attn_sink", "norm_ffn_g")}
    small["lru_wa"], small["lru_wx"] = lru_wa[0], lru_wx[0]
    small["norm_final_g"] = norm_final_g.reshape(1, D)
    env, recv = {}, {}

    def before(name):
        if name == "norm_x":
            return [(shard_t(w_in).astype(bf16), False), (shard_rows, False)]
        if name == "inproj":
            return [(w_out[0].astype(bf16), False), (w_ffn_out[0].astype(bf16), False)]
        if name == "attn_fwd":
            return [(shard_t(w_ffn_in).astype(bf16), False)]
        if name == "ffn_in_bwd":
            return [(rows_parts(env["dw_fo"]), True)]
        if name == "attn_bwd":
            return [(rows_parts(env["dw_out"]), True)]
        if name == "lru_bwd":
            return [(rows_parts(env["dw_fi_t"]), True)]
        if name == "conv_bwd":
            ge = env["grads_early"]
            p32, env["early_f32_spans"] = _pack_rows([ge[n] for n in EARLY_F32])
            return [(p32, False), *[(ge[n].astype(bf16).reshape(-1, LRU_BLOCK), False) for n in GATE_W]]
        if name == "inproj_bwd":
            return [(rows_parts(env["dw_in_t"]), True)]
        return []

    def after(name, got):
        if name == "norm_x":
            env["w_in_t"] = got[0].reshape(IN_W, D)
            full_rows = jnp.swapaxes(got[1], 0, 1).reshape(shard_rows.shape[0], -1)
            small["conv_w"], small["lru_lambda"] = full_rows[0:4], full_rows[4:6]
            small["lru_ba"], small["lru_bx"] = full_rows[6:8], full_rows[8:10]
        elif name == "inproj":
            env["w_out"], env["w_fo"] = got[0].reshape(D, D), got[1].reshape(D_FF, D)
        elif name == "attn_fwd":
            env["w_fi_t"] = got[0].reshape(2 * D_FF, D)
        elif name == "ffn_in_bwd":
            recv["w_ffn_out"] = got[0]
        elif name == "attn_bwd":
            recv["w_out"] = got[0]
        elif name == "lru_bwd":
            recv["w_ffn_in"] = got[0]
        elif name == "conv_bwd":
            recv["early_f32"], recv["lru_wa"], recv["lru_wx"] = got
        elif name == "inproj_bwd":
            recv["w_in"] = got[0]

    grad_x, grads = _local_step(x[0], loss_target[0], small, env, before, after)

    outs = {}
    for name in ("w_out", "w_ffn_out"):
        outs[name] = _adamw(recv[name], w[name], m[name], v[name], "adamw_" + name)
    for name in ("w_in", "w_ffn_in"):
        t = lambda a: jnp.swapaxes(a, 1, 2)
        outs[name] = [t(r) for r in _adamw(recv[name], t(w[name]), t(m[name]), t(v[name]), "adamw_" + name)]
    for name in GATE_W:
        t = lambda a: a.reshape(1, -1, LRU_BLOCK)
        res = _adamw(recv[name], t(w[name]), t(m[name]), t(v[name]), "adamw_" + name)
        outs[name] = [r.reshape(w[name].shape) for r in res]

    small_names = SMALL_REPL + SMALL_SHARD
    late_packed, late_spans = _pack_rows([grads[n] for n in LATE])
    (got_late,) = _exchange([(late_packed, False)], "gather_late_grads")
    summed = {}
    for names, got, spans, tag in ((EARLY_F32, recv["early_f32"], env["early_f32_spans"], "early_f32"),
                                   (LATE, got_late, late_spans, "late")):
        total = _sum_parts(got, "sum_small_" + tag)
        summed.update(zip(names, _unpack_rows(total, spans, [grads[n].shape for n in names])))
    loss = summed["loss"].reshape(())
    gsm = {n: summed[n].reshape(w[n].shape) for n in SMALL_REPL}
    for n in SMALL_SHARD:
        full = summed[n]
        gsm[n] = lax.dynamic_slice_in_dim(full, me * 128, 128, axis=1).reshape(w[n].shape)
    pk = lambda dct: _pack_rows([dct[n] for n in small_names])[0]
    gp, sp = _pack_rows([gsm[n] for n in small_names])
    res = _adamw(gp[None], pk(w)[None], pk(m)[None], pk(v)[None], "adamw_small")
    sshapes = [w[n].shape for n in small_names]
    for idx, t in enumerate(res):
        for n, a in zip(small_names, _unpack_rows(t[0], sp, sshapes)):
            outs.setdefault(n, [None] * 4)[idx] = a

    result = [loss, grad_x[None]]
    for idx in range(4):
        result += [outs[n][idx] for n in ORDER]
    return tuple(result)
```
